```python
import jax, jax.numpy as jnp
from jax import lax
import numpy as np

D_MODEL = 1024
BATCH = 8
SEQ = 2048
DEPTH = 4

N_MIXERS = 3
EPS = 1e-6
CONF_KERNEL = 31
GDN_HEADS = 8
GDN_HEAD_DIM = D_MODEL // GDN_HEADS
GDN_CONV = 4
GDN_CHUNK = 64
FOX_HEADS = 8
FOX_HEAD_DIM = D_MODEL // FOX_HEADS
FOX_BLOCK = 128
D_FF = ((8 * D_MODEL // 3 + 127) // 128) * 128
FFN_CONV = 3

kernel_name = "hybrid_conformer_gdn_fox_trunk"


def _rms_norm(x, g):
    xf = x.astype(jnp.float32)
    y = xf * lax.rsqrt(jnp.mean(xf * xf, axis=-1, keepdims=True) + EPS)
    return (y * g.astype(jnp.float32)).astype(x.dtype)


def _layer_norm(x, g, b):
    xf = x.astype(jnp.float32)
    xc = xf - jnp.mean(xf, axis=-1, keepdims=True)
    var = jnp.mean(xc * xc, axis=-1, keepdims=True)
    return (xc * lax.rsqrt(var + EPS) * g.astype(jnp.float32) + b.astype(jnp.float32)).astype(x.dtype)


def _l2norm(x):
    xf = x.astype(jnp.float32)
    return xf * lax.rsqrt(jnp.sum(xf * xf, axis=-1, keepdims=True) + EPS)


def _causal_dwconv(x, w):
    K, C = w.shape
    return lax.conv_general_dilated(
        x, w[:, None, :].astype(x.dtype), window_strides=(1,), padding=[(K - 1, 0)],
        dimension_numbers=("NWC", "WIO", "NWC"), feature_group_count=C)


def conformer_conv(h, w_in, b_in, w_dw, b_dw, ln_g, ln_b, w_out):
    u = h @ w_in + b_in
    val, gate = jnp.split(u, 2, axis=-1)
    u = val * jax.nn.sigmoid(gate)
    u = _causal_dwconv(u, w_dw) + b_dw
    u = jax.nn.silu(_layer_norm(u, ln_g, ln_b))
    return u @ w_out


def _chunk_gated_delta(q, k, v, g, beta):
    bsz, seq, H, Dh = q.shape
    C = GDN_CHUNK
    N = seq // C
    to_chunks = lambda t: t.reshape(bsz, N, C, H, -1).transpose(1, 0, 3, 2, 4)
    q, k, v = to_chunks(q), to_chunks(k), to_chunks(v)
    g = g.reshape(bsz, N, C, H).transpose(1, 0, 3, 2)
    beta = beta.reshape(bsz, N, C, H).transpose(1, 0, 3, 2)
    g = jnp.cumsum(g, axis=-1)
    kb = k * beta[..., None]
    vb = v * beta[..., None]
    idx = jnp.arange(C)
    lower = idx[:, None] >= idx[None, :]
    strict = idx[:, None] > idx[None, :]
    diff = g[..., :, None] - g[..., None, :]
    decay = jnp.where(lower, jnp.exp(jnp.where(lower, diff, 0.0)), 0.0)
    a_mat = jnp.where(strict, jnp.einsum("nbhid,nbhjd->nbhij", kb, k) * decay, 0.0)
    eye = jnp.eye(C, dtype=jnp.float32)
    t_mat = lax.linalg.triangular_solve(eye + a_mat, jnp.broadcast_to(eye, a_mat.shape),
                                        left_side=True, lower=True)
    u = jnp.einsum("nbhij,nbhjd->nbhid", t_mat, vb)
    w = jnp.einsum("nbhij,nbhjd->nbhid", t_mat, kb * jnp.exp(g)[..., None])
    qk = jnp.where(lower, jnp.einsum("nbhid,nbhjd->nbhij", q, k) * decay, 0.0)
    qg = q * jnp.exp(g)[..., None]
    kd = k * jnp.exp(g[..., -1:] - g)[..., None]
    g_last = jnp.exp(g[..., -1])

    def step(state, xs):
        u_n, w_n, qg_n, qk_n, kd_n, gl_n = xs
        v_new = u_n - jnp.einsum("bhck,bhkv->bhcv", w_n, state)
        o_n = jnp.einsum("bhck,bhkv->bhcv", qg_n, state) + jnp.einsum("bhij,bhjv->bhiv", qk_n, v_new)
        state = state * gl_n[..., None, None] + jnp.einsum("bhck,bhcv->bhkv", kd_n, v_new)
        return state, o_n

    s0 = jnp.zeros((bsz, H, Dh, Dh), jnp.float32)
    _, o = lax.scan(step, s0, (u, w, qg, qk, kd, g_last))
    return o.transpose(1, 0, 3, 2, 4).reshape(bsz, seq, H, Dh)


def gated_deltanet(h, w_in, conv_w, a_log, dt_bias, o_norm_g, w_out):
    bsz, seq, _ = h.shape
    H, Dh = GDN_HEADS, GDN_HEAD_DIM
    W = H * Dh
    proj = h @ w_in
    qkv = jax.nn.silu(_causal_dwconv(proj[..., :3 * W], conv_w))
    z = proj[..., 3 * W:4 * W].reshape(bsz, seq, H, Dh)
    a = proj[..., 4 * W:4 * W + H].astype(jnp.float32)
    b = proj[..., 4 * W + H:].astype(jnp.float32)
    q = _l2norm(qkv[..., :W].reshape(bsz, seq, H, Dh)) * (Dh ** -0.5)
    k = _l2norm(qkv[..., W:2 * W].reshape(bsz, seq, H, Dh))
    v = qkv[..., 2 * W:].reshape(bsz, seq, H, Dh).astype(jnp.float32)
    beta = jax.nn.sigmoid(b)
    g = -jnp.exp(a_log.astype(jnp.float32)) * jax.nn.softplus(a + dt_bias.astype(jnp.float32))
    o = _chunk_gated_delta(q, k, v, g, beta)
    o = _rms_norm(o, o_norm_g) * jax.nn.silu(z.astype(jnp.float32))
    return o.astype(h.dtype).reshape(bsz, seq, W) @ w_out


def forgetting_attention(h, w_in, b_f, q_norm_g, k_norm_g, w_out):
    bsz, seq, _ = h.shape
    H, Dh = FOX_HEADS, FOX_HEAD_DIM
    W = H * Dh
    proj = h @ w_in
    q = _rms_norm(proj[..., :W].reshape(bsz, seq, H, Dh), q_norm_g).transpose(0, 2, 1, 3)
    k = _rms_norm(proj[..., W:2 * W].reshape(bsz, seq, H, Dh), k_norm_g).transpose(0, 2, 1, 3)
    v = proj[..., 2 * W:3 * W].reshape(bsz, seq, H, Dh).transpose(0, 2, 1, 3)
    log_f = jax.nn.log_sigmoid(proj[..., 3 * W:].astype(jnp.float32) + b_f.astype(jnp.float32))
    c = jnp.cumsum(log_f, axis=1).transpose(0, 2, 1)
    scale = Dh ** -0.5
    q_idx = jnp.arange(FOX_BLOCK)
    outs = []
    for blk in range(seq // FOX_BLOCK):
        s0 = blk * FOX_BLOCK
        s1 = s0 + FOX_BLOCK
        logits = jnp.einsum("bhqd,bhkd->bhqk", q[:, :, s0:s1], k[:, :, :s1]).astype(jnp.float32) * scale
        logits = logits + c[:, :, s0:s1, None] - c[:, :, None, :s1]
        causal = jnp.arange(s1)[None, :] <= (s0 + q_idx)[:, None]
        p = jax.nn.softmax(jnp.where(causal, logits, -jnp.inf), axis=-1)
        outs.append(jnp.einsum("bhqk,bhkd->bhqd", p.astype(v.dtype), v[:, :, :s1]))
    o = jnp.concatenate(outs, axis=2).transpose(0, 2, 1, 3).reshape(bsz, seq, W)
    return o @ w_out


def conv_ffn(h, w_up, w_dw, w_down):
    u = _causal_dwconv(h @ w_up, w_dw)
    gate, up = jnp.split(u, 2, axis=-1)
    return (jax.nn.silu(gate) * up) @ w_down


def _fwd_setup_inputs(seed: int = 0) -> dict:
    key = jax.random.key(seed)
    ks = iter(jax.random.split(key, 40))
    n_a = len(range(0, DEPTH, N_MIXERS))
    n_b = len(range(1, DEPTH, N_MIXERS))
    n_c = len(range(2, DEPTH, N_MIXERS))
    D = D_MODEL
    Wg = GDN_HEADS * GDN_HEAD_DIM
    Wf = FOX_HEADS * FOX_HEAD_DIM
    f32 = jnp.float32
    dense = lambda shape, fan_in: jax.random.normal(next(ks), shape, f32) * (fan_in ** -0.5)
    gain = lambda shape: 1.0 + 0.02 * jax.random.normal(next(ks), shape, f32)
    small = lambda shape: 0.02 * jax.random.normal(next(ks), shape, f32)

    x = jax.random.normal(next(ks), (BATCH, SEQ, D), f32)
    mix_norm_g = gain((DEPTH, D))
    ffn_norm_g = gain((DEPTH, D))

    conv_w_in = dense((n_a, D, 2 * D), D)
    conv_b_in = small((n_a, 2 * D))
    conv_w_dw = dense((n_a, CONF_KERNEL, D), CONF_KERNEL)
    conv_b_dw = small((n_a, D))
    conv_ln_g = gain((n_a, D))
    conv_ln_b = small((n_a, D))
    conv_w_out = dense((n_a, D, D), D)

    gdn_w_in = dense((n_b, D, 4 * Wg + 2 * GDN_HEADS), D)
    gdn_conv_w = dense((n_b, GDN_CONV, 3 * Wg), GDN_CONV)
    gdn_a_log = jnp.log(jax.random.uniform(next(ks), (n_b, GDN_HEADS), f32, 1.0, 16.0))
    dt = jnp.exp(jax.random.uniform(next(ks), (n_b, GDN_HEADS), f32, np.log(1e-3), np.log(1e-1)))
    gdn_dt_bias = dt + jnp.log(-jnp.expm1(-dt))
    gdn_o_norm_g = gain((n_b, GDN_HEAD_DIM))
    gdn_w_out = dense((n_b, Wg, D), Wg)

    fox_w_in = dense((n_c, D, 3 * Wf + FOX_HEADS), D)
    fox_b_f = 3.0 + 0.5 * jax.random.normal(next(ks), (n_c, FOX_HEADS), f32)
    fox_q_norm_g = gain((n_c, FOX_HEAD_DIM))
    fox_k_norm_g = gain((n_c, FOX_HEAD_DIM))
    fox_w_out = dense((n_c, Wf, D), Wf)

    ffn_w_up = dense((DEPTH, D, 2 * D_FF), D)
    ffn_w_dw = dense((DEPTH, FFN_CONV, 2 * D_FF), FFN_CONV)
    ffn_w_down = dense((DEPTH, D_FF, D), D_FF)

    return {"x": x, "mix_norm_g": mix_norm_g, "ffn_norm_g": ffn_norm_g,
            "conv_w_in": conv_w_in, "conv_b_in": conv_b_in, "conv_w_dw": conv_w_dw, "conv_b_dw": conv_b_dw,
            "conv_ln_g": conv_ln_g, "conv_ln_b": conv_ln_b, "conv_w_out": conv_w_out,
            "gdn_w_in": gdn_w_in, "gdn_conv_w": gdn_conv_w, "gdn_a_log": gdn_a_log, "gdn_dt_bias": gdn_dt_bias,
            "gdn_o_norm_g": gdn_o_norm_g, "gdn_w_out": gdn_w_out,
            "fox_w_in": fox_w_in, "fox_b_f": fox_b_f, "fox_q_norm_g": fox_q_norm_g, "fox_k_norm_g": fox_k_norm_g,
            "fox_w_out": fox_w_out,
            "ffn_w_up": ffn_w_up, "ffn_w_dw": ffn_w_dw, "ffn_w_down": ffn_w_down}


def _fwd_reference(x, mix_norm_g, ffn_norm_g,
              conv_w_in, conv_b_in, conv_w_dw, conv_b_dw, conv_ln_g, conv_ln_b, conv_w_out,
              gdn_w_in, gdn_conv_w, gdn_a_log, gdn_dt_bias, gdn_o_norm_g, gdn_w_out,
              fox_w_in, fox_b_f, fox_q_norm_g, fox_k_norm_g, fox_w_out,
              ffn_w_up, ffn_w_dw, ffn_w_down):
    ia = ib = ic = 0
    for layer in range(DEPTH):
        h = _rms_norm(x, mix_norm_g[layer])
        kind = layer % N_MIXERS
        if kind == 0:
            y = conformer_conv(h, conv_w_in[ia], conv_b_in[ia], conv_w_dw[ia], conv_b_dw[ia],
                               conv_ln_g[ia], conv_ln_b[ia], conv_w_out[ia])
            ia += 1
        elif kind == 1:
            y = gated_deltanet(h, gdn_w_in[ib], gdn_conv_w[ib], gdn_a_log[ib], gdn_dt_bias[ib],
                               gdn_o_norm_g[ib], gdn_w_out[ib])
            ib += 1
        else:
            y = forgetting_attention(h, fox_w_in[ic], fox_b_f[ic], fox_q_norm_g[ic], fox_k_norm_g[ic],
                                     fox_w_out[ic])
            ic += 1
        x = x + y.astype(x.dtype)
        h = _rms_norm(x, ffn_norm_g[layer])
        x = x + conv_ffn(h, ffn_w_up[layer], ffn_w_dw[layer], ffn_w_down[layer]).astype(x.dtype)
    return x


import jax as _jax
import jax.numpy as _jnp

TWIN_FORMAT = 'train_step'
FWD_PARAMS = ['x', 'mix_norm_g', 'ffn_norm_g', 'conv_w_in', 'conv_b_in', 'conv_w_dw', 'conv_b_dw', 'conv_ln_g', 'conv_ln_b', 'conv_w_out', 'gdn_w_in', 'gdn_conv_w', 'gdn_a_log', 'gdn_dt_bias', 'gdn_o_norm_g', 'gdn_w_out', 'fox_w_in', 'fox_b_f', 'fox_q_norm_g', 'fox_k_norm_g', 'fox_w_out', 'ffn_w_up', 'ffn_w_dw', 'ffn_w_down']
TWIN_WEIGHTS = ['mix_norm_g', 'ffn_norm_g', 'conv_w_in', 'conv_b_in', 'conv_w_dw', 'conv_b_dw', 'conv_ln_g', 'conv_ln_b', 'conv_w_out', 'gdn_w_in', 'gdn_conv_w', 'gdn_a_log', 'gdn_dt_bias', 'gdn_o_norm_g', 'gdn_w_out', 'fox_w_in', 'fox_b_f', 'fox_q_norm_g', 'fox_k_norm_g', 'fox_w_out', 'ffn_w_up', 'ffn_w_dw', 'ffn_w_down']
TWIN_DIFF_INPUT = 'x'
TWIN_INPUTS = ['x', 'mix_norm_g', 'ffn_norm_g', 'conv_w_in', 'conv_b_in', 'conv_w_dw', 'conv_b_dw', 'conv_ln_g', 'conv_ln_b', 'conv_w_out', 'gdn_w_in', 'gdn_conv_w', 'gdn_a_log', 'gdn_dt_bias', 'gdn_o_norm_g', 'gdn_w_out', 'fox_w_in', 'fox_b_f', 'fox_q_norm_g', 'fox_k_norm_g', 'fox_w_out', 'ffn_w_up', 'ffn_w_dw', 'ffn_w_down', 'loss_target', 'm_mix_norm_g', 'm_ffn_norm_g', 'm_conv_w_in', 'm_conv_b_in', 'm_conv_w_dw', 'm_conv_b_dw', 'm_conv_ln_g', 'm_conv_ln_b', 'm_conv_w_out', 'm_gdn_w_in', 'm_gdn_conv_w', 'm_gdn_a_log', 'm_gdn_dt_bias', 'm_gdn_o_norm_g', 'm_gdn_w_out', 'm_fox_w_in', 'm_fox_b_f', 'm_fox_q_norm_g', 'm_fox_k_norm_g', 'm_fox_w_out', 'm_ffn_w_up', 'm_ffn_w_dw', 'm_ffn_w_down', 'v_mix_norm_g', 'v_ffn_norm_g', 'v_conv_w_in', 'v_conv_b_in', 'v_conv_w_dw', 'v_conv_b_dw', 'v_conv_ln_g', 'v_conv_ln_b', 'v_conv_w_out', 'v_gdn_w_in', 'v_gdn_conv_w', 'v_gdn_a_log', 'v_gdn_dt_bias', 'v_gdn_o_norm_g', 'v_gdn_w_out', 'v_fox_w_in', 'v_fox_b_f', 'v_fox_q_norm_g', 'v_fox_k_norm_g', 'v_fox_w_out', 'v_ffn_w_up', 'v_ffn_w_dw', 'v_ffn_w_down']
TWIN_OUTPUTS = ['loss', 'grad_x', 'grad_mix_norm_g', 'grad_ffn_norm_g', 'grad_conv_w_in', 'grad_conv_b_in', 'grad_conv_w_dw', 'grad_conv_b_dw', 'grad_conv_ln_g', 'grad_conv_ln_b', 'grad_conv_w_out', 'grad_gdn_w_in', 'grad_gdn_conv_w', 'grad_gdn_a_log', 'grad_gdn_dt_bias', 'grad_gdn_o_norm_g', 'grad_gdn_w_out', 'grad_fox_w_in', 'grad_fox_b_f', 'grad_fox_q_norm_g', 'grad_fox_k_norm_g', 'grad_fox_w_out', 'grad_ffn_w_up', 'grad_ffn_w_dw', 'grad_ffn_w_down', 'delta_mix_norm_g', 'delta_ffn_norm_g', 'delta_conv_w_in', 'delta_conv_b_in', 'delta_conv_w_dw', 'delta_conv_b_dw', 'delta_conv_ln_g', 'delta_conv_ln_b', 'delta_conv_w_out', 'delta_gdn_w_in', 'delta_gdn_conv_w', 'delta_gdn_a_log', 'delta_gdn_dt_bias', 'delta_gdn_o_norm_g', 'delta_gdn_w_out', 'delta_fox_w_in', 'delta_fox_b_f', 'delta_fox_q_norm_g', 'delta_fox_k_norm_g', 'delta_fox_w_out', 'delta_ffn_w_up', 'delta_ffn_w_dw', 'delta_ffn_w_down', 'new_m_mix_norm_g', 'new_m_ffn_norm_g', 'new_m_conv_w_in', 'new_m_conv_b_in', 'new_m_conv_w_dw', 'new_m_conv_b_dw', 'new_m_conv_ln_g', 'new_m_conv_ln_b', 'new_m_conv_w_out', 'new_m_gdn_w_in', 'new_m_gdn_conv_w', 'new_m_gdn_a_log', 'new_m_gdn_dt_bias', 'new_m_gdn_o_norm_g', 'new_m_gdn_w_out', 'new_m_fox_w_in', 'new_m_fox_b_f', 'new_m_fox_q_norm_g', 'new_m_fox_k_norm_g', 'new_m_fox_w_out', 'new_m_ffn_w_up', 'new_m_ffn_w_dw', 'new_m_ffn_w_down', 'new_v_mix_norm_g', 'new_v_ffn_norm_g', 'new_v_conv_w_in', 'new_v_conv_b_in', 'new_v_conv_w_dw', 'new_v_conv_b_dw', 'new_v_conv_ln_g', 'new_v_conv_ln_b', 'new_v_conv_w_out', 'new_v_gdn_w_in', 'new_v_gdn_conv_w', 'new_v_gdn_a_log', 'new_v_gdn_dt_bias', 'new_v_gdn_o_norm_g', 'new_v_gdn_w_out', 'new_v_fox_w_in', 'new_v_fox_b_f', 'new_v_fox_q_norm_g', 'new_v_fox_k_norm_g', 'new_v_fox_w_out', 'new_v_ffn_w_up', 'new_v_ffn_w_dw', 'new_v_ffn_w_down']
TWIN_LEAF_KINDS = {'loss': 'loss', 'grad_x': 'grad_x', 'grad_mix_norm_g': 'grad_w', 'grad_ffn_norm_g': 'grad_w', 'grad_conv_w_in': 'grad_w', 'grad_conv_b_in': 'grad_w', 'grad_conv_w_dw': 'grad_w', 'grad_conv_b_dw': 'grad_w', 'grad_conv_ln_g': 'grad_w', 'grad_conv_ln_b': 'grad_w', 'grad_conv_w_out': 'grad_w', 'grad_gdn_w_in': 'grad_w', 'grad_gdn_conv_w': 'grad_w', 'grad_gdn_a_log': 'grad_w', 'grad_gdn_dt_bias': 'grad_w', 'grad_gdn_o_norm_g': 'grad_w', 'grad_gdn_w_out': 'grad_w', 'grad_fox_w_in': 'grad_w', 'grad_fox_b_f': 'grad_w', 'grad_fox_q_norm_g': 'grad_w', 'grad_fox_k_norm_g': 'grad_w', 'grad_fox_w_out': 'grad_w', 'grad_ffn_w_up': 'grad_w', 'grad_ffn_w_dw': 'grad_w', 'grad_ffn_w_down': 'grad_w', 'delta_mix_norm_g': 'delta_w', 'delta_ffn_norm_g': 'delta_w', 'delta_conv_w_in': 'delta_w', 'delta_conv_b_in': 'delta_w', 'delta_conv_w_dw': 'delta_w', 'delta_conv_b_dw': 'delta_w', 'delta_conv_ln_g': 'delta_w', 'delta_conv_ln_b': 'delta_w', 'delta_conv_w_out': 'delta_w', 'delta_gdn_w_in': 'delta_w', 'delta_gdn_conv_w': 'delta_w', 'delta_gdn_a_log': 'delta_w', 'delta_gdn_dt_bias': 'delta_w', 'delta_gdn_o_norm_g': 'delta_w', 'delta_gdn_w_out': 'delta_w', 'delta_fox_w_in': 'delta_w', 'delta_fox_b_f': 'delta_w', 'delta_fox_q_norm_g': 'delta_w', 'delta_fox_k_norm_g': 'delta_w', 'delta_fox_w_out': 'delta_w', 'delta_ffn_w_up': 'delta_w', 'delta_ffn_w_dw': 'delta_w', 'delta_ffn_w_down': 'delta_w', 'new_m_mix_norm_g': 'new_m', 'new_m_ffn_norm_g': 'new_m', 'new_m_conv_w_in': 'new_m', 'new_m_conv_b_in': 'new_m', 'new_m_conv_w_dw': 'new_m', 'new_m_conv_b_dw': 'new_m', 'new_m_conv_ln_g': 'new_m', 'new_m_conv_ln_b': 'new_m', 'new_m_conv_w_out': 'new_m', 'new_m_gdn_w_in': 'new_m', 'new_m_gdn_conv_w': 'new_m', 'new_m_gdn_a_log': 'new_m', 'new_m_gdn_dt_bias': 'new_m', 'new_m_gdn_o_norm_g': 'new_m', 'new_m_gdn_w_out': 'new_m', 'new_m_fox_w_in': 'new_m', 'new_m_fox_b_f': 'new_m', 'new_m_fox_q_norm_g': 'new_m', 'new_m_fox_k_norm_g': 'new_m', 'new_m_fox_w_out': 'new_m', 'new_m_ffn_w_up': 'new_m', 'new_m_ffn_w_dw': 'new_m', 'new_m_ffn_w_down': 'new_m', 'new_v_mix_norm_g': 'new_v', 'new_v_ffn_norm_g': 'new_v', 'new_v_conv_w_in': 'new_v', 'new_v_conv_b_in': 'new_v', 'new_v_conv_w_dw': 'new_v', 'new_v_conv_b_dw': 'new_v', 'new_v_conv_ln_g': 'new_v', 'new_v_conv_ln_b': 'new_v', 'new_v_conv_w_out': 'new_v', 'new_v_gdn_w_in': 'new_v', 'new_v_gdn_conv_w': 'new_v', 'new_v_gdn_a_log': 'new_v', 'new_v_gdn_dt_bias': 'new_v', 'new_v_gdn_o_norm_g': 'new_v', 'new_v_gdn_w_out': 'new_v', 'new_v_fox_w_in': 'new_v', 'new_v_fox_b_f': 'new_v', 'new_v_fox_q_norm_g': 'new_v', 'new_v_fox_k_norm_g': 'new_v', 'new_v_fox_w_out': 'new_v', 'new_v_ffn_w_up': 'new_v', 'new_v_ffn_w_dw': 'new_v', 'new_v_ffn_w_down': 'new_v'}


def _forward(args):
    return _fwd_reference(*[args[k] for k in FWD_PARAMS])


def _output_shape():
    out = _jax.eval_shape(lambda: _forward(_fwd_setup_inputs(0)))
    return out.shape, out.dtype

N_MICROBATCH = 1
ADAM_LR = 0.001
ADAM_B1 = 0.9
ADAM_B2 = 0.999
ADAM_EPS = 1e-08
ADAM_WD = 0.01
ADAM_STEP = 10
PER_EXAMPLE_BATCH_AXIS = {'x': 0, 'loss_target': 0}
SHARED_INPUTS = []
_WEIGHT_DTYPES = {'mix_norm_g': _jnp.float32, 'ffn_norm_g': _jnp.float32, 'conv_w_in': _jnp.float32, 'conv_b_in': _jnp.float32, 'conv_w_dw': _jnp.float32, 'conv_b_dw': _jnp.float32, 'conv_ln_g': _jnp.float32, 'conv_ln_b': _jnp.float32, 'conv_w_out': _jnp.float32, 'gdn_w_in': _jnp.float32, 'gdn_conv_w': _jnp.float32, 'gdn_a_log': _jnp.float32, 'gdn_dt_bias': _jnp.float32, 'gdn_o_norm_g': _jnp.float32, 'gdn_w_out': _jnp.float32, 'fox_w_in': _jnp.float32, 'fox_b_f': _jnp.float32, 'fox_q_norm_g': _jnp.float32, 'fox_k_norm_g': _jnp.float32, 'fox_w_out': _jnp.float32, 'ffn_w_up': _jnp.float32, 'ffn_w_dw': _jnp.float32, 'ffn_w_down': _jnp.float32}
MOMENT_SCALE = {'mix_norm_g': 3.621063e+00, 'ffn_norm_g': 1.303330e+01, 'conv_w_in': 4.218591e-01, 'conv_b_in': 4.245053e+00, 'conv_w_dw': 8.007527e-01, 'conv_b_dw': 1.026930e+01, 'conv_ln_g': 8.564018e+00, 'conv_ln_b': 7.652363e+00, 'conv_w_out': 2.330504e+00, 'gdn_w_in': 6.605533e-01, 'gdn_conv_w': 7.872328e-01, 'gdn_a_log': 2.256296e+01, 'gdn_dt_bias': 2.160106e+01, 'gdn_o_norm_g': 4.744910e+01, 'gdn_w_out': 1.588157e+00, 'fox_w_in': 9.230715e-01, 'fox_b_f': 4.892327e+01, 'fox_q_norm_g': 4.742231e+00, 'fox_k_norm_g': 4.754317e+00, 'fox_w_out': 1.402680e+00, 'ffn_w_up': 4.671643e-01, 'ffn_w_dw': 1.886982e+00, 'ffn_w_down': 4.903209e-01}


def _to_microbatches(a, axis):
    t = _jnp.moveaxis(a, axis, 0)
    t = t.reshape((N_MICROBATCH, t.shape[0] // N_MICROBATCH) + t.shape[1:])
    return _jnp.moveaxis(t, 1, axis + 1)


def setup_inputs(seed: int = 0) -> dict:
    inp = _fwd_setup_inputs(seed)
    key = _jax.random.fold_in(_jax.random.key(seed), 7919)
    shape, _ = _output_shape()
    out = dict(inp)
    out["loss_target"] = _jax.random.normal(_jax.random.fold_in(key, 0), shape, _jnp.float32)
    for i, name in enumerate(TWIN_WEIGHTS):
        w = inp[name].astype(_jnp.float32)
        if MOMENT_SCALE is None:
            s = _jnp.sqrt(_jnp.mean(_jnp.square(w)) + 1e-30)
        else:
            s = MOMENT_SCALE[name]
        km, kv = _jax.random.split(_jax.random.fold_in(key, i + 1))
        out[name] = w
        out["m_" + name] = s * _jax.random.normal(km, w.shape, _jnp.float32)
        out["v_" + name] = (s * s) * _jax.random.uniform(kv, w.shape, _jnp.float32, 0.5, 1.5)
    if N_MICROBATCH > 1:
        for name, axis in PER_EXAMPLE_BATCH_AXIS.items():
            out[name] = _to_microbatches(out[name], axis)
    return {'x': out['x'], 'mix_norm_g': out['mix_norm_g'], 'ffn_norm_g': out['ffn_norm_g'], 'conv_w_in': out['conv_w_in'], 'conv_b_in': out['conv_b_in'], 'conv_w_dw': out['conv_w_dw'], 'conv_b_dw': out['conv_b_dw'], 'conv_ln_g': out['conv_ln_g'], 'conv_ln_b': out['conv_ln_b'], 'conv_w_out': out['conv_w_out'], 'gdn_w_in': out['gdn_w_in'], 'gdn_conv_w': out['gdn_conv_w'], 'gdn_a_log': out['gdn_a_log'], 'gdn_dt_bias': out['gdn_dt_bias'], 'gdn_o_norm_g': out['gdn_o_norm_g'], 'gdn_w_out': out['gdn_w_out'], 'fox_w_in': out['fox_w_in'], 'fox_b_f': out['fox_b_f'], 'fox_q_norm_g': out['fox_q_norm_g'], 'fox_k_norm_g': out['fox_k_norm_g'], 'fox_w_out': out['fox_w_out'], 'ffn_w_up': out['ffn_w_up'], 'ffn_w_dw': out['ffn_w_dw'], 'ffn_w_down': out['ffn_w_down'], 'loss_target': out['loss_target'], 'm_mix_norm_g': out['m_mix_norm_g'], 'm_ffn_norm_g': out['m_ffn_norm_g'], 'm_conv_w_in': out['m_conv_w_in'], 'm_conv_b_in': out['m_conv_b_in'], 'm_conv_w_dw': out['m_conv_w_dw'], 'm_conv_b_dw': out['m_conv_b_dw'], 'm_conv_ln_g': out['m_conv_ln_g'], 'm_conv_ln_b': out['m_conv_ln_b'], 'm_conv_w_out': out['m_conv_w_out'], 'm_gdn_w_in': out['m_gdn_w_in'], 'm_gdn_conv_w': out['m_gdn_conv_w'], 'm_gdn_a_log': out['m_gdn_a_log'], 'm_gdn_dt_bias': out['m_gdn_dt_bias'], 'm_gdn_o_norm_g': out['m_gdn_o_norm_g'], 'm_gdn_w_out': out['m_gdn_w_out'], 'm_fox_w_in': out['m_fox_w_in'], 'm_fox_b_f': out['m_fox_b_f'], 'm_fox_q_norm_g': out['m_fox_q_norm_g'], 'm_fox_k_norm_g': out['m_fox_k_norm_g'], 'm_fox_w_out': out['m_fox_w_out'], 'm_ffn_w_up': out['m_ffn_w_up'], 'm_ffn_w_dw': out['m_ffn_w_dw'], 'm_ffn_w_down': out['m_ffn_w_down'], 'v_mix_norm_g': out['v_mix_norm_g'], 'v_ffn_norm_g': out['v_ffn_norm_g'], 'v_conv_w_in': out['v_conv_w_in'], 'v_conv_b_in': out['v_conv_b_in'], 'v_conv_w_dw': out['v_conv_w_dw'], 'v_conv_b_dw': out['v_conv_b_dw'], 'v_conv_ln_g': out['v_conv_ln_g'], 'v_conv_ln_b': out['v_conv_ln_b'], 'v_conv_w_out': out['v_conv_w_out'], 'v_gdn_w_in': out['v_gdn_w_in'], 'v_gdn_conv_w': out['v_gdn_conv_w'], 'v_gdn_a_log': out['v_gdn_a_log'], 'v_gdn_dt_bias': out['v_gdn_dt_bias'], 'v_gdn_o_norm_g': out['v_gdn_o_norm_g'], 'v_gdn_w_out': out['v_gdn_w_out'], 'v_fox_w_in': out['v_fox_w_in'], 'v_fox_b_f': out['v_fox_b_f'], 'v_fox_q_norm_g': out['v_fox_q_norm_g'], 'v_fox_k_norm_g': out['v_fox_k_norm_g'], 'v_fox_w_out': out['v_fox_w_out'], 'v_ffn_w_up': out['v_ffn_w_up'], 'v_ffn_w_dw': out['v_ffn_w_dw'], 'v_ffn_w_down': out['v_ffn_w_down']}


def _loss(weights, diff, rest, loss_target):
    with _jax.named_scope("forward"):
        args = {**rest, TWIN_DIFF_INPUT: diff, **{k: w.astype(_WEIGHT_DTYPES[k]) for k, w in weights.items()}}
        y = _forward(args)
    with _jax.named_scope("loss_head"):
        err = _jnp.square(y.astype(_jnp.float32) - loss_target)
        return 0.5 * _jnp.sum(_jnp.mean(err, axis=-1)) if err.ndim else 0.5 * err


def _adamw(w, g, m, v):
    m = ADAM_B1 * m + (1.0 - ADAM_B1) * g
    v = ADAM_B2 * v + (1.0 - ADAM_B2) * _jnp.square(g)
    m_hat = m / (1.0 - ADAM_B1 ** ADAM_STEP)
    v_hat = v / (1.0 - ADAM_B2 ** ADAM_STEP)
    delta = -ADAM_LR * (m_hat / (_jnp.sqrt(v_hat) + ADAM_EPS) + ADAM_WD * w)
    return delta, m, v


def reference(x, mix_norm_g, ffn_norm_g, conv_w_in, conv_b_in, conv_w_dw, conv_b_dw, conv_ln_g, conv_ln_b, conv_w_out, gdn_w_in, gdn_conv_w, gdn_a_log, gdn_dt_bias, gdn_o_norm_g, gdn_w_out, fox_w_in, fox_b_f, fox_q_norm_g, fox_k_norm_g, fox_w_out, ffn_w_up, ffn_w_dw, ffn_w_down, loss_target, m_mix_norm_g, m_ffn_norm_g, m_conv_w_in, m_conv_b_in, m_conv_w_dw, m_conv_b_dw, m_conv_ln_g, m_conv_ln_b, m_conv_w_out, m_gdn_w_in, m_gdn_conv_w, m_gdn_a_log, m_gdn_dt_bias, m_gdn_o_norm_g, m_gdn_w_out, m_fox_w_in, m_fox_b_f, m_fox_q_norm_g, m_fox_k_norm_g, m_fox_w_out, m_ffn_w_up, m_ffn_w_dw, m_ffn_w_down, v_mix_norm_g, v_ffn_norm_g, v_conv_w_in, v_conv_b_in, v_conv_w_dw, v_conv_b_dw, v_conv_ln_g, v_conv_ln_b, v_conv_w_out, v_gdn_w_in, v_gdn_conv_w, v_gdn_a_log, v_gdn_dt_bias, v_gdn_o_norm_g, v_gdn_w_out, v_fox_w_in, v_fox_b_f, v_fox_q_norm_g, v_fox_k_norm_g, v_fox_w_out, v_ffn_w_up, v_ffn_w_dw, v_ffn_w_down):
    given = dict(x=x, mix_norm_g=mix_norm_g, ffn_norm_g=ffn_norm_g, conv_w_in=conv_w_in, conv_b_in=conv_b_in, conv_w_dw=conv_w_dw, conv_b_dw=conv_b_dw, conv_ln_g=conv_ln_g, conv_ln_b=conv_ln_b, conv_w_out=conv_w_out, gdn_w_in=gdn_w_in, gdn_conv_w=gdn_conv_w, gdn_a_log=gdn_a_log, gdn_dt_bias=gdn_dt_bias, gdn_o_norm_g=gdn_o_norm_g, gdn_w_out=gdn_w_out, fox_w_in=fox_w_in, fox_b_f=fox_b_f, fox_q_norm_g=fox_q_norm_g, fox_k_norm_g=fox_k_norm_g, fox_w_out=fox_w_out, ffn_w_up=ffn_w_up, ffn_w_dw=ffn_w_dw, ffn_w_down=ffn_w_down, loss_target=loss_target, m_mix_norm_g=m_mix_norm_g, m_ffn_norm_g=m_ffn_norm_g, m_conv_w_in=m_conv_w_in, m_conv_b_in=m_conv_b_in, m_conv_w_dw=m_conv_w_dw, m_conv_b_dw=m_conv_b_dw, m_conv_ln_g=m_conv_ln_g, m_conv_ln_b=m_conv_ln_b, m_conv_w_out=m_conv_w_out, m_gdn_w_in=m_gdn_w_in, m_gdn_conv_w=m_gdn_conv_w, m_gdn_a_log=m_gdn_a_log, m_gdn_dt_bias=m_gdn_dt_bias, m_gdn_o_norm_g=m_gdn_o_norm_g, m_gdn_w_out=m_gdn_w_out, m_fox_w_in=m_fox_w_in, m_fox_b_f=m_fox_b_f, m_fox_q_norm_g=m_fox_q_norm_g, m_fox_k_norm_g=m_fox_k_norm_g, m_fox_w_out=m_fox_w_out, m_ffn_w_up=m_ffn_w_up, m_ffn_w_dw=m_ffn_w_dw, m_ffn_w_down=m_ffn_w_down, v_mix_norm_g=v_mix_norm_g, v_ffn_norm_g=v_ffn_norm_g, v_conv_w_in=v_conv_w_in, v_conv_b_in=v_conv_b_in, v_conv_w_dw=v_conv_w_dw, v_conv_b_dw=v_conv_b_dw, v_conv_ln_g=v_conv_ln_g, v_conv_ln_b=v_conv_ln_b, v_conv_w_out=v_conv_w_out, v_gdn_w_in=v_gdn_w_in, v_gdn_conv_w=v_gdn_conv_w, v_gdn_a_log=v_gdn_a_log, v_gdn_dt_bias=v_gdn_dt_bias, v_gdn_o_norm_g=v_gdn_o_norm_g, v_gdn_w_out=v_gdn_w_out, v_fox_w_in=v_fox_w_in, v_fox_b_f=v_fox_b_f, v_fox_q_norm_g=v_fox_q_norm_g, v_fox_k_norm_g=v_fox_k_norm_g, v_fox_w_out=v_fox_w_out, v_ffn_w_up=v_ffn_w_up, v_ffn_w_dw=v_ffn_w_dw, v_ffn_w_down=v_ffn_w_down)
    weights = {n: given[n] for n in TWIN_WEIGHTS}
    shared = {n: given[n] for n in SHARED_INPUTS}
    per_example = {n: given[n] for n in ['x']}
    grad_fn = _jax.value_and_grad(_loss, argnums=(0, 1))

    def one_microbatch(ex, loss_target):
        ex = dict(ex)
        diff = ex.pop(TWIN_DIFF_INPUT)
        return grad_fn(weights, diff, {**shared, **ex}, loss_target)

    if N_MICROBATCH == 1:
        loss, (grad_w, grad_x) = one_microbatch(per_example, given["loss_target"])
    else:
        def body(carry, xs):
            loss_sum, grad_sum = carry
            l_k, (gw_k, gx_k) = one_microbatch(xs[0], xs[1])
            with _jax.named_scope("update"):
                return (loss_sum + l_k, _jax.tree.map(_jnp.add, grad_sum, gw_k)), gx_k

        init = (_jnp.zeros((), _jnp.float32), _jax.tree.map(_jnp.zeros_like, weights))
        (loss, grad_w), grad_x = _jax.lax.scan(body, init, (per_example, given["loss_target"]))
    with _jax.named_scope("update"):
        delta_w, new_m, new_v = {}, {}, {}
        for n in TWIN_WEIGHTS:
            delta_w[n], new_m[n], new_v[n] = _adamw(weights[n], grad_w[n], given["m_" + n], given["v_" + n])
    return (loss, grad_x, *[grad_w[n] for n in TWIN_WEIGHTS], *[delta_w[n] for n in TWIN_WEIGHTS],
            *[new_m[n] for n in TWIN_WEIGHTS], *[new_v[n] for n in TWIN_WEIGHTS])
```

```python
import functools

import jax
import jax.numpy as jnp
from jax import lax
from jax.experimental import pallas as pl
from jax.experimental.pallas import tpu as pltpu

F32 = jnp.float32
BF16 = jnp.bfloat16
HIGHEST = lax.Precision.HIGHEST

N_DEV = 8
LANE = 128
EPS = 1e-6
DEPTH = 4
N_MIXERS = 3
HEADS = 8
DH = 128
D = HEADS * DH
D_FF = 2816
CONF_K, GDN_K, FFN_K = 31, 4, 3
GDN_CHUNK = 64
GDN_PAD = 4224
FOX_PAD = 3200
ADAM_LR, ADAM_B1, ADAM_B2, ADAM_EPS, ADAM_WD, ADAM_STEP = 0.001, 0.9, 0.999, 1e-08, 0.01, 10
VMEM_LIMIT = 56 * 1024 * 1024

WEIGHTS = ['mix_norm_g', 'ffn_norm_g', 'conv_w_in', 'conv_b_in', 'conv_w_dw', 'conv_b_dw', 'conv_ln_g', 'conv_ln_b',
           'conv_w_out', 'gdn_w_in', 'gdn_conv_w', 'gdn_a_log', 'gdn_dt_bias', 'gdn_o_norm_g', 'gdn_w_out', 'fox_w_in',
           'fox_b_f', 'fox_q_norm_g', 'fox_k_norm_g', 'fox_w_out', 'ffn_w_up', 'ffn_w_dw', 'ffn_w_down']
REPLICATED = ['mix_norm_g', 'ffn_norm_g', 'gdn_a_log', 'gdn_dt_bias', 'gdn_o_norm_g', 'fox_b_f', 'fox_q_norm_g',
              'fox_k_norm_g']
ROW_SHARDED = ['conv_w_out', 'gdn_w_out', 'fox_w_out', 'ffn_w_down']
MATRICES = ['conv_w_in', 'conv_w_out', 'gdn_w_in', 'gdn_w_out', 'fox_w_in', 'fox_w_out', 'ffn_w_up', 'ffn_w_down']
SHARDED = [n for n in WEIGHTS if n not in REPLICATED]


def _params(*sem):
    return pltpu.CompilerParams(dimension_semantics=sem, vmem_limit_bytes=VMEM_LIMIT)


def _tile(n, cap):
    if n <= cap:
        return n
    d = (cap // LANE) * LANE
    while d >= LANE:
        if n % d == 0:
            return d
        d -= LANE
    raise ValueError(f"no lane-aligned tile of {n} under {cap}")


def _raw_dot(a, b, ca, cb, hp):
    batch = ((0,), (0,)) if a.ndim == 3 else ((), ())
    dn = (((ca,), (cb,)), batch)
    if hp:
        return lax.dot_general(a.astype(F32), b.astype(F32), dn, precision=HIGHEST, preferred_element_type=F32)
    return lax.dot_general(a.astype(BF16), b.astype(BF16), dn, preferred_element_type=F32)


def _raw_nn(a, b, hp=False):
    return _raw_dot(a, b, a.ndim - 1, b.ndim - 2, hp)


def _raw_nt(a, b, hp=False):
    return _raw_dot(a, b, a.ndim - 1, b.ndim - 1, hp)


def _raw_tn(a, b, hp=False):
    return _raw_dot(a, b, a.ndim - 2, b.ndim - 2, hp)


@functools.partial(jax.custom_vjp, nondiff_argnums=(2,))
def _nn(a, b, hp):
    return _raw_nn(a, b, hp)


def _nn_fwd(a, b, hp):
    return _raw_nn(a, b, hp), (a, b)


def _nn_bwd(hp, res, g):
    a, b = res
    return _raw_nt(g, b, hp), _raw_tn(a, g, hp)


_nn.defvjp(_nn_fwd, _nn_bwd)


@functools.partial(jax.custom_vjp, nondiff_argnums=(2,))
def _nt(a, b, hp):
    return _raw_nt(a, b, hp)


def _nt_fwd(a, b, hp):
    return _raw_nt(a, b, hp), (a, b)


def _nt_bwd(hp, res, g):
    a, b = res
    return _raw_nn(g, b, hp), _raw_tn(g, a, hp)


_nt.defvjp(_nt_fwd, _nt_bwd)


def _shift_down(x, s):
    if s == 0:
        return x
    t = lax.broadcasted_iota(jnp.int32, x.shape, 0)
    return jnp.where(t >= s, pltpu.roll(x, s, axis=0), 0.0)


def _shift_up(x, s):
    if s == 0:
        return x
    n = x.shape[0]
    t = lax.broadcasted_iota(jnp.int32, x.shape, 0)
    return jnp.where(t < n - s, pltpu.roll(x, n - s, axis=0), 0.0)


def _row(w, k):
    r = lax.broadcasted_iota(jnp.int32, w.shape, 0)
    return jnp.sum(jnp.where(r == k, w, 0.0), axis=0, keepdims=True)


@jax.custom_vjp
def _dwconv(x, w):
    taps = w.shape[0]
    y = _row(w, taps - 1) * x
    for k in range(taps - 1):
        y = y + _row(w, k) * _shift_down(x, taps - 1 - k)
    return y


def _dwconv_fwd(x, w):
    return _dwconv(x, w), (x, w)


def _dwconv_bwd(res, dy):
    x, w = res
    taps = w.shape[0]
    r = lax.broadcasted_iota(jnp.int32, w.shape, 0)
    dx = _row(w, taps - 1) * dy
    dw = jnp.where(r == taps - 1, jnp.sum(dy * x, axis=0, keepdims=True), 0.0)
    for k in range(taps - 1):
        s = taps - 1 - k
        dx = dx + _row(w, k) * _shift_up(dy, s)
        dw = dw + jnp.where(r == k, jnp.sum(dy * _shift_down(x, s), axis=0, keepdims=True), 0.0)
    return dx, dw


_dwconv.defvjp(_dwconv_fwd, _dwconv_bwd)


def _sigmoid(x):
    return 1.0 / (1.0 + jnp.exp(-x))


def _silu(x):
    return x * _sigmoid(x)


def _softplus(x):
    return jnp.maximum(x, 0.0) + jnp.log(1.0 + jnp.exp(-jnp.abs(x)))


def _head_scale(x, fn):
    tm = x.shape[0]
    x3 = x.reshape(tm, HEADS, DH)
    return (x3 * fn(jnp.sum(x3 * x3, axis=-1, keepdims=True))).reshape(tm, HEADS * DH)


def _tile_lanes(g):
    return jnp.concatenate([g] * HEADS, axis=1)


def _expand_heads(v, first):
    lane = lax.broadcasted_iota(jnp.int32, (LANE, HEADS * DH), 0)
    col = lax.broadcasted_iota(jnp.int32, (LANE, HEADS * DH), 1)
    sel = (lane == col // DH + first).astype(F32)
    return _nn(v, sel, True)


def _f_rms(x, g):
    return (x * lax.rsqrt(jnp.mean(x * x, axis=-1, keepdims=True) + EPS) * g,)


def _f_rms_res(x, g):
    return (_f_rms(x, g)[0], x)


def _f_conf_glu_conv(uv, ug, bv, bg, w):
    return (_dwconv((uv + bv) * _sigmoid(ug + bg), w),)


def _f_conf_ln_silu(cv, b_dw, ln_g, ln_b):
    u = cv + b_dw
    xc = u - jnp.mean(u, axis=-1, keepdims=True)
    y = xc * lax.rsqrt(jnp.mean(xc * xc, axis=-1, keepdims=True) + EPS) * ln_g + ln_b
    return (_silu(y),)


def _f_conv_silu(u, w):
    return (_silu(_dwconv(u, w)),)


def _f_gdn_gates(q, k, ab, a_log, dt_bias):
    qn = _head_scale(q, lambda ss: lax.rsqrt(ss + EPS)) * (DH ** -0.5)
    kn = _head_scale(k, lambda ss: lax.rsqrt(ss + EPS))
    g = -jnp.exp(a_log) * _softplus(ab + dt_bias)
    beta = _sigmoid(ab)
    return qn, kn, _expand_heads(g, 0), _expand_heads(beta, HEADS)


def _f_gdn_prescan(q, k, v, gb, bb):
    c = GDN_CHUNK
    n = q.shape[0] // c
    r3 = lambda t: t.reshape(n, c, DH)
    q3, k3, v3, g3, b3 = r3(q), r3(k), r3(v), r3(gb), r3(bb)
    ii = lax.broadcasted_iota(jnp.int32, (n, c, c), 1)
    jj = lax.broadcasted_iota(jnp.int32, (n, c, c), 2)
    lower, strict = ii >= jj, ii > jj
    gcb = _nn(lower.astype(F32), g3, True)
    gi = gcb[:, :, :c]
    gj = jnp.swapaxes(gi, 1, 2)
    decay = jnp.where(lower, jnp.exp(jnp.where(lower, gi - gj, 0.0)), 0.0)
    kb, vb = k3 * b3, v3 * b3
    a_mat = jnp.where(strict, _nt(kb, k3, False) * decay, 0.0)
    p = -a_mat
    t_mat = (ii == jj).astype(F32) + p
    for _ in range(5):
        p = _nn(p, p, True)
        t_mat = t_mat + _nn(t_mat, p, True)
    eg = jnp.exp(gcb)
    u = _nn(t_mat, vb, False)
    w = _nn(t_mat, kb * eg, False)
    qk = jnp.where(lower, _nt(q3, k3, False) * decay, 0.0)
    qg = q3 * eg
    g_last = jnp.sum(g3, axis=1, keepdims=True)
    kd = k3 * jnp.exp(g_last - gcb)
    r2 = lambda t: t.reshape(n * c, DH)
    return r2(u), r2(w), r2(qg), qk, r2(kd), jnp.exp(g_last)


def _f_gdn_post(o, z, o_g):
    on = _head_scale(o, lambda ss: lax.rsqrt(ss / DH + EPS)) * _tile_lanes(o_g)
    return (on * _silu(z),)


def _f_fox_pre(q, k, f, q_g, k_g, b_f):
    qn = _head_scale(q, lambda ss: lax.rsqrt(ss / DH + EPS)) * _tile_lanes(q_g)
    kn = _head_scale(k, lambda ss: lax.rsqrt(ss / DH + EPS)) * _tile_lanes(k_g)
    return qn, kn, -_softplus(-(f + b_f))


def _f_ffn_mid(ug, uu, wg, wu):
    return (_silu(_dwconv(ug, wg)) * _dwconv(uu, wu),)


def _seg_fwd(name, f, grid, ins, in_specs, out_shapes, out_specs):
    n_in = len(ins)

    def body(*refs):
        outs = f(*[r[...].astype(F32) for r in refs[:n_in]])
        for r, o in zip(refs[n_in:], outs):
            r[...] = o.astype(r.dtype)

    return pl.pallas_call(body, grid=grid, in_specs=in_specs, out_specs=out_specs, out_shape=out_shapes, name=name,
                          compiler_params=_params(*(["parallel"] * len(grid))))(*ins)


def _seg_bwd(name, f, grid, ins, in_specs, douts, dout_specs, want):
    n_in, n_dy = len(ins), len(douts)
    diff = [i for i, w in enumerate(want) if w is not None]
    out_shapes = [jax.ShapeDtypeStruct(ins[i].shape, F32 if want[i] == "acc" else want[i]) for i in diff]
    out_specs = [in_specs[i] for i in diff]
    acc_axis = len(grid) - 1

    def body(*refs):
        vals = [r[...].astype(F32) for r in refs[:n_in]]
        dys = [r[...].astype(F32) for r in refs[n_in:n_in + n_dy]]
        out_refs = refs[n_in + n_dy:]

        def g(*dv):
            full = list(vals)
            for i, v in zip(diff, dv):
                full[i] = v
            return f(*full)

        _, vjp = jax.vjp(g, *[vals[i] for i in diff])
        grads = vjp(tuple(dys))
        first = pl.program_id(acc_axis) == 0
        for i, r, gr in zip(diff, out_refs, grads):
            if want[i] == "acc":
                @pl.when(first)
                def _(r=r, gr=gr):
                    r[...] = gr

                @pl.when(jnp.logical_not(first))
                def _(r=r, gr=gr):
                    r[...] += gr
            else:
                r[...] = gr.astype(r.dtype)

    sem = ["parallel"] * (len(grid) - 1) + ["arbitrary"]
    return pl.pallas_call(body, grid=grid, in_specs=list(in_specs) + list(dout_specs), out_specs=out_specs,
                          out_shape=out_shapes, name=name, compiler_params=_params(*sem))(*ins, *douts)


def _rows(tm, width, col=0):
    return pl.BlockSpec((tm, width), lambda i, col=col: (i, col))


def _const(shape):
    return pl.BlockSpec(shape, lambda i: (0,) * len(shape))


def _cols(t, tc, off=0):
    return pl.BlockSpec((t, tc), lambda j, off=off: (0, j + off))


def _grid2(specs):
    return [pl.BlockSpec(s.block_shape, lambda j, i, f=s.index_map: f(j)) for s in specs]


def _mm(name, a, b, *, ta=False, tb=False, res=None, out_dtype=F32):
    k_dim, m = (a.shape[0], a.shape[1]) if ta else (a.shape[1], a.shape[0])
    n = b.shape[0] if tb else b.shape[1]
    tm, tn, tk = _tile(m, 1408), _tile(n, 1408), _tile(k_dim, 1408)
    nk = k_dim // tk
    grid = (m // tm, n // tn, nk)
    a_spec = pl.BlockSpec((tk, tm), lambda i, j, k: (k, i)) if ta else pl.BlockSpec((tm, tk), lambda i, j, k: (i, k))
    b_spec = pl.BlockSpec((tn, tk), lambda i, j, k: (j, k)) if tb else pl.BlockSpec((tk, tn), lambda i, j, k: (k, j))
    o_spec = pl.BlockSpec((tm, tn), lambda i, j, k: (i, j))
    dn = (((0 if ta else 1,), (1 if tb else 0,)), ((), ()))
    has_res = res is not None

    def body(*refs):
        a_ref, b_ref = refs[0], refs[1]
        res_ref = refs[2] if has_res else None
        o_ref = refs[3] if has_res else refs[2]
        p = lax.dot_general(a_ref[...].astype(BF16), b_ref[...].astype(BF16), dn, preferred_element_type=F32)

        def write(acc):
            if has_res:
                acc = acc + res_ref[...]
            o_ref[...] = acc.astype(o_ref.dtype)

        if nk == 1:
            write(p)
        else:
            acc_ref = refs[-1]
            k = pl.program_id(2)

            @pl.when(k == 0)
            def _():
                acc_ref[...] = p

            @pl.when(k > 0)
            def _():
                acc_ref[...] += p

            @pl.when(k == nk - 1)
            def _():
                write(acc_ref[...])

    ins, specs = [a, b], [a_spec, b_spec]
    if has_res:
        ins.append(res)
        specs.append(o_spec)
    scratch = [pltpu.VMEM((tm, tn), F32)] if nk > 1 else []
    return pl.pallas_call(body, grid=grid, in_specs=specs, out_specs=o_spec, scratch_shapes=scratch,
                          out_shape=jax.ShapeDtypeStruct((m, n), out_dtype), name=name,
                          compiler_params=_params("parallel", "parallel", "arbitrary"))(*ins)


def _head_specs(t):
    n = t // GDN_CHUNK
    col = pl.BlockSpec((t, DH), lambda h: (0, h))
    qk = pl.BlockSpec((None, n, GDN_CHUNK, GDN_CHUNK), lambda h: (h, 0, 0, 0))
    gl = pl.BlockSpec((None, n, 1, DH), lambda h: (h, 0, 0, 0))
    st = pl.BlockSpec((None, n, DH, DH), lambda h: (h, 0, 0, 0))
    return n, col, qk, gl, st


def _gdn_scan_fwd(name, u, w, qg, qk, kd, gl):
    t = u.shape[0]
    n, col, qk_spec, gl_spec, st_spec = _head_specs(t)
    c = GDN_CHUNK

    def body(u_ref, w_ref, qg_ref, qk_ref, kd_ref, gl_ref, o_ref, s_ref):
        def step(i, s):
            rows = pl.ds(pl.multiple_of(i * c, c), c)
            s_ref[i] = s
            vn = u_ref[rows, :] - _raw_nn(w_ref[rows, :], s)
            o_ref[rows, :] = _raw_nn(qg_ref[rows, :], s) + _raw_nn(qk_ref[i], vn)
            return s * gl_ref[i] + _raw_tn(kd_ref[rows, :], vn)

        lax.fori_loop(0, n, step, jnp.zeros((DH, DH), F32))

    return pl.pallas_call(
        body, grid=(HEADS,), in_specs=[col, col, col, qk_spec, col, gl_spec], out_specs=[col, st_spec],
        out_shape=[jax.ShapeDtypeStruct((t, D), F32), jax.ShapeDtypeStruct((HEADS, n, DH, DH), F32)], name=name,
        compiler_params=_params("parallel"))(u, w, qg, qk, kd, gl)


def _gdn_scan_bwd(name, u, w, qg, qk, kd, gl, states, do):
    t = u.shape[0]
    n, col, qk_spec, gl_spec, st_spec = _head_specs(t)
    c = GDN_CHUNK

    def body(u_ref, w_ref, qg_ref, qk_ref, kd_ref, gl_ref, s_ref, do_ref,
             du_ref, dw_ref, dqg_ref, dqk_ref, dkd_ref, dgl_ref):
        def step(r, ds):
            i = n - 1 - r
            rows = pl.ds(pl.multiple_of(i * c, c), c)
            s, do_c, w_c = s_ref[i], do_ref[rows, :], w_ref[rows, :]
            vn = u_ref[rows, :] - _raw_nn(w_c, s)
            dvn = _raw_tn(qk_ref[i], do_c) + _raw_nn(kd_ref[rows, :], ds)
            du_ref[rows, :] = dvn
            dw_ref[rows, :] = -_raw_nt(dvn, s)
            dqg_ref[rows, :] = _raw_nt(do_c, s)
            dqk_ref[i] = _raw_nt(do_c, vn)
            dkd_ref[rows, :] = _raw_nt(vn, ds)
            dgl_ref[i] = jnp.sum(ds * s, axis=0, keepdims=True)
            return _raw_tn(qg_ref[rows, :], do_c) + ds * gl_ref[i] - _raw_tn(w_c, dvn)

        lax.fori_loop(0, n, step, jnp.zeros((DH, DH), F32))

    big = jax.ShapeDtypeStruct((t, D), F32)
    return pl.pallas_call(
        body, grid=(HEADS,), in_specs=[col, col, col, qk_spec, col, gl_spec, st_spec, col],
        out_specs=[col, col, col, qk_spec, col, gl_spec],
        out_shape=[big, big, big, jax.ShapeDtypeStruct(qk.shape, F32), big, jax.ShapeDtypeStruct(gl.shape, F32)],
        name=name, compiler_params=_params("parallel"))(u, w, qg, qk, kd, gl, states, do)


def _cumsum_rows(name, parts, reverse):
    t = parts[0].shape[0]
    blk = min(t, 256)
    nb = t // blk
    n_in = len(parts)

    def body(*refs):
        o_ref = refs[n_in]
        ii = lax.broadcasted_iota(jnp.int32, (blk, blk), 0)
        jj = lax.broadcasted_iota(jnp.int32, (blk, blk), 1)
        tri = ((ii <= jj) if reverse else (ii >= jj)).astype(F32)
        carry = jnp.zeros((1, LANE), F32)
        for b in (range(nb - 1, -1, -1) if reverse else range(nb)):
            rows = pl.ds(b * blk, blk)
            x = refs[0][rows, :]
            for r in refs[1:n_in]:
                x = x + r[rows, :]
            o_ref[rows, :] = _raw_nn(tri, x, True) + carry
            carry = carry + jnp.sum(x, axis=0, keepdims=True)

    return pl.pallas_call(body, out_shape=jax.ShapeDtypeStruct((t, LANE), F32), name=name,
                          compiler_params=_params())(*parts)


def _fox_blocks(t):
    blk = 256 if t % 256 == 0 and t >= 1024 else 128
    return blk, t // blk


def _fox_logits(q, k, cq, ck, row0, col0):
    s = _raw_nt(q, k) * (DH ** -0.5) + cq - ck
    rows = row0 + lax.broadcasted_iota(jnp.int32, s.shape, 0)
    cols = col0 + lax.broadcasted_iota(jnp.int32, s.shape, 1)
    return jnp.where(cols <= rows, s, -jnp.inf)


def _fox_fwd(name, qn, kn, proj, c_col, c_row):
    t = qn.shape[0]
    blk, nb = _fox_blocks(t)
    voff = 2 * HEADS

    def body(q_ref, k_ref, v_ref, cc_ref, cr_ref, o_ref, lse_ref):
        i = pl.program_id(1)
        q, cq = q_ref[...], cc_ref[...]

        def step(j, carry):
            m, l, acc = carry
            rows = pl.ds(pl.multiple_of(j * blk, blk), blk)
            s = _fox_logits(q, k_ref[rows, :], cq, cr_ref[j], i * blk, j * blk)
            m_new = jnp.maximum(m, jnp.max(s, axis=1, keepdims=True))
            p = jnp.exp(s - m_new)
            alpha = jnp.exp(m - m_new)
            return m_new, alpha * l + jnp.sum(p, axis=1, keepdims=True), alpha * acc + _raw_nn(p, v_ref[rows, :])

        init = (jnp.full((blk, 1), -1e30, F32), jnp.zeros((blk, 1), F32), jnp.zeros((blk, DH), F32))
        m, l, acc = lax.fori_loop(0, i + 1, step, init)
        o_ref[...] = acc / l
        lse_ref[...] = m + jnp.log(l)

    tile = pl.BlockSpec((blk, DH), lambda h, i: (i, h))
    colv = pl.BlockSpec((None, blk, 1), lambda h, i: (h, i, 0))
    return pl.pallas_call(
        body, grid=(HEADS, nb),
        in_specs=[tile, pl.BlockSpec((t, DH), lambda h, i: (0, h)), pl.BlockSpec((t, DH), lambda h, i: (0, voff + h)),
                  colv, pl.BlockSpec((None, nb, 1, blk), lambda h, i: (h, 0, 0, 0))],
        out_specs=[tile, colv],
        out_shape=[jax.ShapeDtypeStruct((t, D), F32), jax.ShapeDtypeStruct((HEADS, t, 1), F32)], name=name,
        compiler_params=_params("parallel", "parallel"))(qn, kn, proj, c_col, c_row)


def _fox_dq(name, qn, kn, proj, c_col, c_row, o, lse, do):
    t = qn.shape[0]
    blk, nb = _fox_blocks(t)
    voff = 2 * HEADS

    def body(q_ref, k_ref, v_ref, cc_ref, cr_ref, o_ref, lse_ref, do_ref, dq_ref, dcc_ref, dl_ref):
        i = pl.program_id(1)
        q, cq, do_b, lse_b = q_ref[...], cc_ref[...], do_ref[...], lse_ref[...]
        delta = jnp.sum(do_b * o_ref[...], axis=1, keepdims=True)

        def step(j, carry):
            dq, dcc = carry
            rows = pl.ds(pl.multiple_of(j * blk, blk), blk)
            k = k_ref[rows, :]
            p = jnp.exp(_fox_logits(q, k, cq, cr_ref[j], i * blk, j * blk) - lse_b)
            ds = p * (_raw_nt(do_b, v_ref[rows, :]) - delta)
            return dq + _raw_nn(ds, k), dcc + jnp.sum(ds, axis=1, keepdims=True)

        dq, dcc = lax.fori_loop(0, i + 1, step, (jnp.zeros((blk, DH), F32), jnp.zeros((blk, 1), F32)))
        dq_ref[...] = dq * (DH ** -0.5)
        dcc_ref[...] = dcc
        dl_ref[...] = delta

    tile = pl.BlockSpec((blk, DH), lambda h, i: (i, h))
    colv = pl.BlockSpec((None, blk, 1), lambda h, i: (h, i, 0))
    vec = jax.ShapeDtypeStruct((HEADS, t, 1), F32)
    return pl.pallas_call(
        body, grid=(HEADS, nb),
        in_specs=[tile, pl.BlockSpec((t, DH), lambda h, i: (0, h)), pl.BlockSpec((t, DH), lambda h, i: (0, voff + h)),
                  colv, pl.BlockSpec((None, nb, 1, blk), lambda h, i: (h, 0, 0, 0)), tile, colv, tile],
        out_specs=[tile, colv, colv],
        out_shape=[jax.ShapeDtypeStruct((t, D), F32), vec, vec], name=name,
        compiler_params=_params("parallel", "parallel"))(qn, kn, proj, c_col, c_row, o, lse, do)


def _fox_dkv(name, qn, kn, proj, c_col, c_row, lse, delta, do):
    t = qn.shape[0]
    blk, nb = _fox_blocks(t)
    voff = 2 * HEADS

    def body(q_ref, k_ref, v_ref, cc_ref, cr_ref, lse_ref, dl_ref, do_ref, dk_ref, dv_ref, dcr_ref):
        j = pl.program_id(1)
        k, v, ck = k_ref[...], v_ref[...], cr_ref[...]

        def step(i, carry):
            dk, dv, dcr = carry
            rows = pl.ds(pl.multiple_of(i * blk, blk), blk)
            q, do_b = q_ref[rows, :], do_ref[rows, :]
            p = jnp.exp(_fox_logits(q, k, cc_ref[rows, :], ck, i * blk, j * blk) - lse_ref[rows, :])
            ds = p * (_raw_nt(do_b, v) - dl_ref[rows, :])
            return dk + _raw_tn(ds, q), dv + _raw_tn(p, do_b), dcr - jnp.sum(ds, axis=0, keepdims=True)

        init = (jnp.zeros((blk, DH), F32), jnp.zeros((blk, DH), F32), jnp.zeros((1, blk), F32))
        dk, dv, dcr = lax.fori_loop(j, nb, step, init)
        dk_ref[...] = dk * (DH ** -0.5)
        dv_ref[...] = dv
        dcr_ref[...] = dcr

    full = pl.BlockSpec((t, DH), lambda h, j: (0, h))
    colf = pl.BlockSpec((None, t, 1), lambda h, j: (h, 0, 0))
    tile = pl.BlockSpec((blk, DH), lambda h, j: (j, h))
    rowv = pl.BlockSpec((None, None, 1, blk), lambda h, j: (h, j, 0, 0))
    big = jax.ShapeDtypeStruct((t, D), F32)
    return pl.pallas_call(
        body, grid=(HEADS, nb),
        in_specs=[full, tile, pl.BlockSpec((blk, DH), lambda h, j: (j, voff + h)), colf, rowv, colf, colf, full],
        out_specs=[tile, tile, rowv],
        out_shape=[big, big, jax.ShapeDtypeStruct((HEADS, nb, 1, blk), F32)], name=name,
        compiler_params=_params("parallel", "parallel"))(qn, kn, proj, c_col, c_row, lse, delta, do)


def _loss_head(y, target):
    t = y.shape[0]
    tm = min(t, 512)

    def body(y_ref, t_ref, dy_ref, sum_ref):
        err = y_ref[...] - t_ref[...]
        dy_ref[...] = err * (1.0 / D)
        part = jnp.sum(jnp.sum(err * err, axis=1, keepdims=True), axis=0, keepdims=True)

        @pl.when(pl.program_id(0) == 0)
        def _():
            sum_ref[...] = jnp.zeros_like(sum_ref)

        sum_ref[...] += jnp.broadcast_to(part, sum_ref.shape)

    return pl.pallas_call(
        body, grid=(t // tm,), in_specs=[_rows(tm, D), _rows(tm, D)], out_specs=[_rows(tm, D), _const((1, LANE))],
        out_shape=[jax.ShapeDtypeStruct((t, D), F32), jax.ShapeDtypeStruct((1, LANE), F32)], name="loss_head",
        compiler_params=_params("arbitrary"))(y, target)


def _adam(name, parts, w, m, v):
    r, c = w.shape
    tr = r
    for cand in (512, 256, 128, 64, 32, 16):
        if r > cand and r % cand == 0:
            tr = cand
            break

    def body(p_ref, w_ref, m_ref, v_ref, g_ref, d_ref, nm_ref, nv_ref):
        g = p_ref[0].astype(F32)
        for s in range(1, N_DEV):
            g = g + p_ref[s].astype(F32)
        nm = ADAM_B1 * m_ref[...] + (1.0 - ADAM_B1) * g
        nv = ADAM_B2 * v_ref[...] + (1.0 - ADAM_B2) * (g * g)
        m_hat = nm / (1.0 - ADAM_B1 ** ADAM_STEP)
        v_hat = nv / (1.0 - ADAM_B2 ** ADAM_STEP)
        g_ref[...] = g
        d_ref[...] = -ADAM_LR * (m_hat / (jnp.sqrt(v_hat) + ADAM_EPS) + ADAM_WD * w_ref[...])
        nm_ref[...] = nm
        nv_ref[...] = nv

    blk = pl.BlockSpec((tr, c), lambda i: (i, 0))
    out = jax.ShapeDtypeStruct((r, c), F32)
    return pl.pallas_call(
        body, grid=(r // tr,), in_specs=[pl.BlockSpec((N_DEV, tr, c), lambda i: (0, i, 0)), blk, blk, blk],
        out_specs=[blk] * 4, out_shape=[out] * 4, name=name, compiler_params=_params("parallel"))(parts, w, m, v)


def _exchange(name, arrays, scatter):
    n = len(arrays)
    out_shapes = [jax.ShapeDtypeStruct(((N_DEV,) + (a.shape[1:] if sc else a.shape)), a.dtype)
                  for a, sc in zip(arrays, scatter)]

    def body(*refs):
        in_refs, out_refs = refs[:n], refs[n:2 * n]
        send_sems, recv_sems, local_sems = refs[2 * n:]
        x, y, c = lax.axis_index("x"), lax.axis_index("y"), lax.axis_index("c")
        me = 4 * x + 2 * y + c
        flip = lambda v, bit: 1 - v if bit else v
        piece = lambda a, dev: in_refs[a].at[dev] if scatter[a] else in_refs[a]

        local = [pltpu.make_async_copy(piece(a, me), out_refs[a].at[me], local_sems.at[a]) for a in range(n)]
        for cp in local:
            cp.start()
        sends = []
        for p in range(1, N_DEV):
            px, py, pc = flip(x, p & 4), flip(y, p & 2), flip(c, p & 1)
            peer = 4 * px + 2 * py + pc
            for a in range(n):
                cp = pltpu.make_async_remote_copy(
                    src_ref=piece(a, peer), dst_ref=out_refs[a].at[me], send_sem=send_sems.at[a, p - 1],
                    recv_sem=recv_sems.at[a, p - 1], device_id=(px, py, pc), device_id_type=pl.DeviceIdType.MESH)
                cp.start()
                sends.append((cp, a, p, peer))
        for _, a, p, peer in sends:
            pltpu.make_async_remote_copy(
                src_ref=piece(a, peer), dst_ref=out_refs[a].at[peer], send_sem=send_sems.at[a, p - 1],
                recv_sem=recv_sems.at[a, p - 1], device_id=(x, y, c), device_id_type=pl.DeviceIdType.MESH).wait_recv()
        for cp, _, _, _ in sends:
            cp.wait_send()
        for cp in local:
            cp.wait()

    any_spec = pl.BlockSpec(memory_space=pl.ANY)
    return pl.pallas_call(
        body, in_specs=[any_spec] * n, out_specs=[any_spec] * n, out_shape=out_shapes, name=name,
        scratch_shapes=[pltpu.SemaphoreType.DMA((n, N_DEV - 1)), pltpu.SemaphoreType.DMA((n, N_DEV - 1)),
                        pltpu.SemaphoreType.DMA((n,))])(*arrays)


def _rms_fwd(name, x, g):
    t = x.shape[0]
    tm = min(t, 512)
    return _seg_fwd(name, _f_rms, (t // tm,), [x, g], [_rows(tm, D), _const((1, D))],
                    [jax.ShapeDtypeStruct((t, D), BF16)], [_rows(tm, D)])[0]


def _rms_bwd(name, x, g, dh, dres):
    t = x.shape[0]
    tm = min(t, 512)
    return _seg_bwd(name, _f_rms_res, (t // tm,), [x, g], [_rows(tm, D), _const((1, D))],
                    [dh, dres], [_rows(tm, D), _rows(tm, D)], [F32, "acc"])


def _ffn_specs(t):
    tc = 256
    nf = D_FF // tc
    return nf, [_cols(t, tc), _cols(t, tc, nf), _cols(FFN_K, tc), _cols(FFN_K, tc, nf)], [_cols(t, tc)]


def _ffn_fwd(tag, x, p):
    t = x.shape[0]
    nf, in_specs, out_specs = _ffn_specs(t)
    h = _rms_fwd(tag + "_rms", x, p["g"])
    u0 = _mm(tag + "_up", h, p["w_up"])
    act = _seg_fwd(tag + "_mid", _f_ffn_mid, (nf,), [u0, u0, p["w_dw"], p["w_dw"]], in_specs,
                   [jax.ShapeDtypeStruct((t, D_FF), BF16)], out_specs)[0]
    out = _mm(tag + "_down", act, p["w_down"], res=x)
    return out, (x, h, u0, act)


def _ffn_bwd(tag, saved, p, dout):
    x, h, u0, act = saved
    t = x.shape[0]
    nf, in_specs, out_specs = _ffn_specs(t)
    g = {"w_down": _mm(tag + "_dwdown", act, dout, ta=True, out_dtype=BF16)}
    dact = _mm(tag + "_dact", dout, p["w_down"], tb=True, out_dtype=BF16)
    dug, duu, dwg, dwu = _seg_bwd(tag + "_dmid", _f_ffn_mid, (nf, 1), [u0, u0, p["w_dw"], p["w_dw"]],
                                  _grid2(in_specs), [dact], _grid2(out_specs), [BF16, BF16, "acc", "acc"])
    du0 = jnp.concatenate([dug[:, :D_FF], duu[:, D_FF:]], axis=1)
    g["w_dw"] = jnp.concatenate([dwg[:, :D_FF], dwu[:, D_FF:]], axis=1)
    g["w_up"] = _mm(tag + "_dwup", h, du0, ta=True, out_dtype=BF16)
    dh = _mm(tag + "_dh", du0, p["w_up"], tb=True, out_dtype=BF16)
    dx, g["g"] = _rms_bwd(tag + "_drms", x, p["g"], dh, dout)
    return dx, g


def _conf_specs(t):
    tc, tm = 128, min(t, 512)
    nc = D // tc
    vec = _const((1, D))
    glu_in = [_cols(t, tc), _cols(t, tc, nc), _cols(1, tc), _cols(1, tc, nc), _cols(CONF_K, tc)]
    return tm, nc, glu_in, [_cols(t, tc)], [_rows(tm, D), vec, vec, vec]


def _conf_fwd(tag, x, p):
    t = x.shape[0]
    tm, nc, glu_in, glu_out, ln_in = _conf_specs(t)
    h = _rms_fwd(tag + "_rms", x, p["g"])
    u = _mm(tag + "_in", h, p["w_in"])
    cv = _seg_fwd(tag + "_gluconv", _f_conf_glu_conv, (nc,), [u, u, p["b_in"], p["b_in"], p["w_dw"]], glu_in,
                  [jax.ShapeDtypeStruct((t, D), F32)], glu_out)[0]
    act = _seg_fwd(tag + "_lnsilu", _f_conf_ln_silu, (t // tm,), [cv, p["b_dw"], p["ln_g"], p["ln_b"]], ln_in,
                   [jax.ShapeDtypeStruct((t, D), BF16)], [_rows(tm, D)])[0]
    out = _mm(tag + "_out", act, p["w_out"], res=x)
    return out, (x, h, u, cv, act)


def _conf_bwd(tag, saved, p, dout):
    x, h, u, cv, act = saved
    t = x.shape[0]
    tm, nc, glu_in, glu_out, ln_in = _conf_specs(t)
    g = {"w_out": _mm(tag + "_dwout", act, dout, ta=True, out_dtype=BF16)}
    dact = _mm(tag + "_dact", dout, p["w_out"], tb=True, out_dtype=BF16)
    dcv, g["b_dw"], g["ln_g"], g["ln_b"] = _seg_bwd(
        tag + "_dlnsilu", _f_conf_ln_silu, (t // tm,), [cv, p["b_dw"], p["ln_g"], p["ln_b"]], ln_in, [dact],
        [_rows(tm, D)], [F32, "acc", "acc", "acc"])
    duv, dug, dbv, dbg, g["w_dw"] = _seg_bwd(
        tag + "_dgluconv", _f_conf_glu_conv, (nc, 1), [u, u, p["b_in"], p["b_in"], p["w_dw"]], _grid2(glu_in), [dcv],
        _grid2(glu_out), [BF16, BF16, "acc", "acc", "acc"])
    du = jnp.concatenate([duv[:, :D], dug[:, D:]], axis=1)
    g["b_in"] = jnp.concatenate([dbv[:, :D], dbg[:, D:]], axis=1)
    g["w_in"] = _mm(tag + "_dwin", h, du, ta=True, out_dtype=BF16)
    dh = _mm(tag + "_dh", du, p["w_in"], tb=True, out_dtype=BF16)
    dx, g["g"] = _rms_bwd(tag + "_drms", x, p["g"], dh, dout)
    return dx, g


def _gdn_specs(t):
    tc, tm, rows = 256, min(t, 256), min(t, 512)
    nq = 3 * D // tc
    conv = ([_cols(t, tc), _cols(GDN_K, tc)], [_cols(t, tc)])
    gate_in = [_rows(tm, D, 0), _rows(tm, D, 1), _rows(tm, LANE, 4 * D // LANE), _const((1, LANE)), _const((1, LANE))]
    gate_out = [_rows(tm, D)] * 4
    head = pl.BlockSpec((rows, DH), lambda h, i: (i, h))
    headv = pl.BlockSpec((rows, DH), lambda h, i: (i, 2 * HEADS + h))
    nch = rows // GDN_CHUNK
    pre_in = [head, head, headv, head, head]
    pre_out = [head, head, head, pl.BlockSpec((None, nch, GDN_CHUNK, GDN_CHUNK), lambda h, i: (h, i, 0, 0)), head,
               pl.BlockSpec((None, nch, 1, DH), lambda h, i: (h, i, 0, 0))]
    post_in = [_rows(tm, D), _rows(tm, D, 3), _const((1, DH))]
    return tm, rows, nq, conv, gate_in, gate_out, pre_in, pre_out, post_in


def _gdn_fwd(tag, x, p):
    t = x.shape[0]
    tm, rows, nq, conv, gate_in, gate_out, pre_in, pre_out, post_in = _gdn_specs(t)
    n = t // GDN_CHUNK
    big = jax.ShapeDtypeStruct((t, D), F32)
    h = _rms_fwd(tag + "_rms", x, p["g"])
    proj = _mm(tag + "_in", h, p["w_in"])
    qkv = _seg_fwd(tag + "_conv", _f_conv_silu, (nq,), [proj, p["conv_w"]], conv[0],
                   [jax.ShapeDtypeStruct((t, 3 * D), F32)], conv[1])[0]
    gate_ins = [qkv, qkv, proj, p["a_log"], p["dt_bias"]]
    qn, kn, gb, bb = _seg_fwd(tag + "_gates", _f_gdn_gates, (t // tm,), gate_ins, gate_in, [big] * 4, gate_out)
    pre_ins = [qn, kn, qkv, gb, bb]
    pre_shapes = [big, big, big, jax.ShapeDtypeStruct((HEADS, n, GDN_CHUNK, GDN_CHUNK), F32), big,
                  jax.ShapeDtypeStruct((HEADS, n, 1, DH), F32)]
    pre = _seg_fwd(tag + "_prescan", _f_gdn_prescan, (HEADS, t // rows), pre_ins, pre_in, pre_shapes, pre_out)
    o, states = _gdn_scan_fwd(tag + "_scan", *pre)
    post_ins = [o, proj, p["o_g"]]
    act = _seg_fwd(tag + "_post", _f_gdn_post, (t // tm,), post_ins, post_in, [jax.ShapeDtypeStruct((t, D), BF16)],
                   [_rows(tm, D)])[0]
    out = _mm(tag + "_out", act, p["w_out"], res=x)
    return out, (x, h, proj, gate_ins, pre_ins, pre, states, post_ins, act)


def _gdn_bwd(tag, saved, p, dout):
    x, h, proj, gate_ins, pre_ins, pre, states, post_ins, act = saved
    t = x.shape[0]
    tm, rows, nq, conv, gate_in, gate_out, pre_in, pre_out, post_in = _gdn_specs(t)
    g = {"w_out": _mm(tag + "_dwout", act, dout, ta=True, out_dtype=BF16)}
    dact = _mm(tag + "_dact", dout, p["w_out"], tb=True, out_dtype=BF16)
    do, dproj_z, g["o_g"] = _seg_bwd(tag + "_dpost", _f_gdn_post, (t // tm,), post_ins, post_in, [dact],
                                     [_rows(tm, D)], [F32, BF16, "acc"])
    dpre = _gdn_scan_bwd(tag + "_dscan", *pre, states, do)
    dqn, dkn, dqkv_v, dgb, dbb = _seg_bwd(tag + "_dprescan", _f_gdn_prescan, (HEADS, t // rows), pre_ins, pre_in,
                                          dpre, pre_out, [F32] * 5)
    dqkv_q, dqkv_k, dproj_ab, g["a_log"], g["dt_bias"] = _seg_bwd(
        tag + "_dgates", _f_gdn_gates, (t // tm,), gate_ins, gate_in, [dqn, dkn, dgb, dbb], gate_out,
        [F32, F32, BF16, "acc", "acc"])
    dqkv = jnp.concatenate([dqkv_q[:, :D], dqkv_k[:, D:2 * D], dqkv_v[:, 2 * D:]], axis=1)
    dqkv_pre, g["conv_w"] = _seg_bwd(tag + "_dconv", _f_conv_silu, (nq, 1), [proj, p["conv_w"]], _grid2(conv[0]),
                                     [dqkv], _grid2(conv[1]), [BF16, "acc"])
    dproj = jnp.concatenate([dqkv_pre[:, :3 * D], dproj_z[:, 3 * D:4 * D], dproj_ab[:, 4 * D:]], axis=1)
    g["w_in"] = _mm(tag + "_dwin", h, dproj, ta=True, out_dtype=BF16)
    dh = _mm(tag + "_dh", dproj, p["w_in"], tb=True, out_dtype=BF16)
    dx, g["g"] = _rms_bwd(tag + "_drms", x, p["g"], dh, dout)
    return dx, g


def _fox_specs(t):
    tm = min(t, 512)
    vec = _const((1, LANE))
    pre_in = [_rows(tm, D, 0), _rows(tm, D, 1), _rows(tm, LANE, 3 * D // LANE), vec, vec, vec]
    pre_out = [_rows(tm, D), _rows(tm, D), _rows(tm, LANE)]
    return tm, pre_in, pre_out


def _per_head(c):
    return jnp.transpose(c[:, :HEADS])


def _per_lane(ch):
    return jnp.pad(jnp.transpose(ch), ((0, 0), (0, LANE - HEADS)))


def _fox_fwd_layer(tag, x, p):
    t = x.shape[0]
    tm, pre_in, pre_out = _fox_specs(t)
    blk, nb = _fox_blocks(t)
    h = _rms_fwd(tag + "_rms", x, p["g"])
    proj = _mm(tag + "_in", h, p["w_in"])
    pre_ins = [proj, proj, proj, p["q_g"], p["k_g"], p["b_f"]]
    qn, kn, lf = _seg_fwd(tag + "_pre", _f_fox_pre, (t // tm,), pre_ins, pre_in,
                          [jax.ShapeDtypeStruct((t, D), BF16)] * 2 + [jax.ShapeDtypeStruct((t, LANE), F32)], pre_out)
    ch = _per_head(_cumsum_rows(tag + "_cumsum", [lf], False))
    c_col, c_row = ch.reshape(HEADS, t, 1), ch.reshape(HEADS, nb, 1, blk)
    o, lse = _fox_fwd(tag + "_attn", qn, kn, proj, c_col, c_row)
    out = _mm(tag + "_out", o, p["w_out"], res=x)
    return out, (x, h, proj, pre_ins, qn, kn, c_col, c_row, o, lse)


def _fox_bwd_layer(tag, saved, p, dout):
    x, h, proj, pre_ins, qn, kn, c_col, c_row, o, lse = saved
    t = x.shape[0]
    tm, pre_in, pre_out = _fox_specs(t)
    g = {"w_out": _mm(tag + "_dwout", o, dout, ta=True, out_dtype=BF16)}
    do = _mm(tag + "_do", dout, p["w_out"], tb=True)
    dqn, dc_col, delta = _fox_dq(tag + "_dq", qn, kn, proj, c_col, c_row, o, lse, do)
    dkn, dv, dc_row = _fox_dkv(tag + "_dkv", qn, kn, proj, c_col, c_row, lse, delta, do)
    dlf = _cumsum_rows(tag + "_dcumsum", [_per_lane(dc_col.reshape(HEADS, t)), _per_lane(dc_row.reshape(HEADS, t))],
                       True)
    dq, dk, df, g["q_g"], g["k_g"], g["b_f"] = _seg_bwd(
        tag + "_dpre", _f_fox_pre, (t // tm,), pre_ins, pre_in, [dqn, dkn, dlf], pre_out,
        [BF16, BF16, BF16, "acc", "acc", "acc"])
    dproj = jnp.concatenate([dq[:, :D], dk[:, D:2 * D], dv.astype(BF16), df[:, 3 * D:]], axis=1)
    g["w_in"] = _mm(tag + "_dwin", h, dproj, ta=True, out_dtype=BF16)
    dh = _mm(tag + "_dh", dproj, p["w_in"], tb=True, out_dtype=BF16)
    dx, g["g"] = _rms_bwd(tag + "_drms", x, p["g"], dh, dout)
    return dx, g


_MIXERS = ((_conf_fwd, _conf_bwd), (_gdn_fwd, _gdn_bwd), (_fox_fwd_layer, _fox_bwd_layer))


def _local_step(x, target, layers):
    saved = []
    for i, (mp, fp) in enumerate(layers):
        x, sm = _MIXERS[i % N_MIXERS][0](f"l{i}_mix", x, mp)
        x, sf = _ffn_fwd(f"l{i}_ffn", x, fp)
        saved.append((sm, sf))
    dx, sq = _loss_head(x, target)
    grads = [None] * len(layers)
    for i in reversed(range(len(layers))):
        mp, fp = layers[i]
        dx, gf = _ffn_bwd(f"l{i}_ffn", saved[i][1], fp, dx)
        dx, gm = _MIXERS[i % N_MIXERS][1](f"l{i}_mix", saved[i][0], mp, dx)
        grads[i] = (gm, gf)
    return sq, dx, grads


def _unshard(name, g):
    axis = g.ndim - 2 if name in ROW_SHARDED else g.ndim - 1
    m = jnp.moveaxis(g, 0, axis - 1)
    return m.reshape(m.shape[:axis - 1] + (N_DEV * m.shape[axis],) + m.shape[axis + 1:])


def _reshard(name, full):
    axis = full.ndim - 2 if name in ROW_SHARDED else full.ndim - 1
    s = full.shape
    return jnp.moveaxis(full.reshape(s[:axis] + (N_DEV, s[axis] // N_DEV) + s[axis + 1:]), axis, 0)


def _pad_cols(a, width):
    return jnp.pad(a, [(0, 0)] * (a.ndim - 1) + [(0, width - a.shape[-1])])


def _lane_vec(v):
    return _pad_cols(v.reshape(1, -1), LANE)


REP_ROWS = 16


def _pack_rep(r):
    small = [_pad_cols(r[n].reshape(1, -1), LANE) for n in REPLICATED[2:]]
    row = jnp.concatenate(small + [jnp.zeros((1, D - LANE * len(small)), F32)], axis=1)
    pad = jnp.zeros((REP_ROWS - 2 * DEPTH - 1, D), F32)
    return jnp.concatenate([r['mix_norm_g'].reshape(DEPTH, D), r['ffn_norm_g'].reshape(DEPTH, D), row, pad], axis=0)


def _unpack_rep(a, shapes):
    out = {'mix_norm_g': a[:DEPTH], 'ffn_norm_g': a[DEPTH:2 * DEPTH]}
    for i, n in enumerate(REPLICATED[2:]):
        out[n] = a[2 * DEPTH:2 * DEPTH + 1, i * LANE:i * LANE + shapes[n][1]].reshape(shapes[n])
    return out


def _view2d(a):
    return a.reshape(-1, a.shape[-1])


def _layer_params(full, rep):
    layers, kinds, count = [], [], [0] * N_MIXERS
    for i in range(DEPTH):
        kind = i % N_MIXERS
        j = count[kind]
        count[kind] += 1
        kinds.append((kind, j))
        if kind == 0:
            mp = dict(w_in=full['conv_w_in'][j], b_in=full['conv_b_in'][j][None], w_dw=full['conv_w_dw'][j],
                      b_dw=full['conv_b_dw'][j][None], ln_g=full['conv_ln_g'][j][None], ln_b=full['conv_ln_b'][j][None],
                      w_out=full['conv_w_out'][j])
        elif kind == 1:
            mp = dict(w_in=full['gdn_w_in'][j], conv_w=full['gdn_conv_w'][j], a_log=_lane_vec(rep['gdn_a_log'][j]),
                      dt_bias=_lane_vec(rep['gdn_dt_bias'][j]), o_g=rep['gdn_o_norm_g'][j][None],
                      w_out=full['gdn_w_out'][j])
        else:
            mp = dict(w_in=full['fox_w_in'][j], b_f=_lane_vec(rep['fox_b_f'][j]), q_g=rep['fox_q_norm_g'][j][None],
                      k_g=rep['fox_k_norm_g'][j][None], w_out=full['fox_w_out'][j])
        mp["g"] = rep['mix_norm_g'][i][None]
        fp = dict(g=rep['ffn_norm_g'][i][None], w_up=full['ffn_w_up'][i], w_dw=full['ffn_w_dw'][i],
                  w_down=full['ffn_w_down'][i])
        layers.append((mp, fp))
    return layers, kinds


def _collect_grads(grads, kinds):
    mix = {k: [grads[i][0] for i in range(DEPTH) if kinds[i][0] == k] for k in range(N_MIXERS)}
    stack = lambda gs, key, cut=None: jnp.stack([(g[key] if cut is None else g[key][:, :cut]) for g in gs])
    whole = {
        'conv_w_in': stack(mix[0], "w_in"), 'conv_b_in': stack(mix[0], "b_in")[:, 0], 'conv_w_dw': stack(mix[0], "w_dw"),
        'conv_b_dw': stack(mix[0], "b_dw")[:, 0], 'conv_ln_g': stack(mix[0], "ln_g")[:, 0],
        'conv_ln_b': stack(mix[0], "ln_b")[:, 0], 'conv_w_out': stack(mix[0], "w_out"),
        'gdn_w_in': stack(mix[1], "w_in", 4 * D + 2 * HEADS), 'gdn_conv_w': stack(mix[1], "conv_w"),
        'gdn_w_out': stack(mix[1], "w_out"),
        'fox_w_in': stack(mix[2], "w_in", 3 * D + HEADS), 'fox_w_out': stack(mix[2], "w_out"),
        'ffn_w_up': jnp.stack([g[1]["w_up"] for g in grads]), 'ffn_w_dw': jnp.stack([g[1]["w_dw"] for g in grads]),
        'ffn_w_down': jnp.stack([g[1]["w_down"] for g in grads]),
    }
    rep = {
        'mix_norm_g': jnp.concatenate([g[0]["g"] for g in grads]),
        'ffn_norm_g': jnp.concatenate([g[1]["g"] for g in grads]),
        'gdn_a_log': stack(mix[1], "a_log"), 'gdn_dt_bias': stack(mix[1], "dt_bias"), 'gdn_o_norm_g': stack(mix[1], "o_g"),
        'fox_b_f': stack(mix[2], "b_f"), 'fox_q_norm_g': stack(mix[2], "q_g"), 'fox_k_norm_g': stack(mix[2], "k_g"),
    }
    return whole, rep


def kernel(x, mix_norm_g, ffn_norm_g, conv_w_in, conv_b_in, conv_w_dw, conv_b_dw, conv_ln_g, conv_ln_b, conv_w_out, gdn_w_in, gdn_conv_w, gdn_a_log, gdn_dt_bias, gdn_o_norm_g, gdn_w_out, fox_w_in, fox_b_f, fox_q_norm_g, fox_k_norm_g, fox_w_out, ffn_w_up, ffn_w_dw, ffn_w_down, loss_target, m_mix_norm_g, m_ffn_norm_g, m_conv_w_in, m_conv_b_in, m_conv_w_dw, m_conv_b_dw, m_conv_ln_g, m_conv_ln_b, m_conv_w_out, m_gdn_w_in, m_gdn_conv_w, m_gdn_a_log, m_gdn_dt_bias, m_gdn_o_norm_g, m_gdn_w_out, m_fox_w_in, m_fox_b_f, m_fox_q_norm_g, m_fox_k_norm_g, m_fox_w_out, m_ffn_w_up, m_ffn_w_dw, m_ffn_w_down, v_mix_norm_g, v_ffn_norm_g, v_conv_w_in, v_conv_b_in, v_conv_w_dw, v_conv_b_dw, v_conv_ln_g, v_conv_ln_b, v_conv_w_out, v_gdn_w_in, v_gdn_conv_w, v_gdn_a_log, v_gdn_dt_bias, v_gdn_o_norm_g, v_gdn_w_out, v_fox_w_in, v_fox_b_f, v_fox_q_norm_g, v_fox_k_norm_g, v_fox_w_out, v_ffn_w_up, v_ffn_w_dw, v_ffn_w_down):
    given = dict(zip(
        WEIGHTS + ["m_" + n for n in WEIGHTS] + ["v_" + n for n in WEIGHTS],
        (mix_norm_g, ffn_norm_g, conv_w_in, conv_b_in, conv_w_dw, conv_b_dw, conv_ln_g, conv_ln_b, conv_w_out, gdn_w_in, gdn_conv_w, gdn_a_log, gdn_dt_bias, gdn_o_norm_g, gdn_w_out, fox_w_in, fox_b_f, fox_q_norm_g, fox_k_norm_g, fox_w_out, ffn_w_up, ffn_w_dw, ffn_w_down,
         m_mix_norm_g, m_ffn_norm_g, m_conv_w_in, m_conv_b_in, m_conv_w_dw, m_conv_b_dw, m_conv_ln_g, m_conv_ln_b, m_conv_w_out, m_gdn_w_in, m_gdn_conv_w, m_gdn_a_log, m_gdn_dt_bias, m_gdn_o_norm_g, m_gdn_w_out, m_fox_w_in, m_fox_b_f, m_fox_q_norm_g, m_fox_k_norm_g, m_fox_w_out, m_ffn_w_up, m_ffn_w_dw, m_ffn_w_down,
         v_mix_norm_g, v_ffn_norm_g, v_conv_w_in, v_conv_b_in, v_conv_w_dw, v_conv_b_dw, v_conv_ln_g, v_conv_ln_b, v_conv_w_out, v_gdn_w_in, v_gdn_conv_w, v_gdn_a_log, v_gdn_dt_bias, v_gdn_o_norm_g, v_gdn_w_out, v_fox_w_in, v_fox_b_f, v_fox_q_norm_g, v_fox_k_norm_g, v_fox_w_out, v_ffn_w_up, v_ffn_w_dw, v_ffn_w_down)))

    shards = [given[n].astype(BF16) if n in MATRICES else given[n] for n in SHARDED]
    gathered = _exchange("gather_weights", shards, [False] * len(shards))
    full = {n: _unshard(n, g) for n, g in zip(SHARDED, gathered)}
    full['gdn_w_in'] = _pad_cols(full['gdn_w_in'], GDN_PAD)
    full['fox_w_in'] = _pad_cols(full['fox_w_in'], FOX_PAD)

    layers, kinds = _layer_params(full, given)
    sq, dx, grads = _local_step(x[0], loss_target[0], layers)
    loss = (0.5 / D) * lax.psum(sq[0, 0], ("x", "y", "c"))

    whole, rep = _collect_grads(grads, kinds)
    parts = _exchange("exchange_grads", [_reshard(n, whole[n]) for n in SHARDED] + [_pack_rep(rep)],
                      [True] * len(SHARDED) + [False])
    new = {}
    for n, part in zip(SHARDED, parts[:-1]):
        w2 = _view2d(given[n])
        outs = _adam("adam_" + n, part.reshape((N_DEV,) + w2.shape), w2, _view2d(given["m_" + n]),
                     _view2d(given["v_" + n]))
        new[n] = [o.reshape(given[n].shape) for o in outs]
    packed = [_pack_rep({n: given[pre + n] for n in REPLICATED}) for pre in ("", "m_", "v_")]
    outs = _adam("adam_replicated", parts[-1], *packed)
    unpacked = [_unpack_rep(o, {n: given[n].shape for n in REPLICATED}) for o in outs]
    for n in REPLICATED:
        new[n] = [u[n] for u in unpacked]
    return (loss, dx[None], *[new[n][0] for n in WEIGHTS], *[new[n][1] for n in WEIGHTS],
            *[new[n][2] for n in WEIGHTS], *[new[n][3] for n in WEIGHTS])
```

```python
import functools

import jax
import jax.numpy as jnp
from jax import lax
from jax.experimental import pallas as pl
from jax.experimental.pallas import tpu as pltpu

F32 = jnp.float32
BF16 = jnp.bfloat16
HIGHEST = lax.Precision.HIGHEST

N_DEV = 8
LANE = 128
EPS = 1e-6
DEPTH = 4
N_MIXERS = 3
HEADS = 8
DH = 128
D = HEADS * DH
D_FF = 2816
CONF_K, GDN_K, FFN_K = 31, 4, 3
GDN_CHUNK = 64
GDN_PAD = 4224
FOX_PAD = 3200
ADAM_LR, ADAM_B1, ADAM_B2, ADAM_EPS, ADAM_WD, ADAM_STEP = 0.001, 0.9, 0.999, 1e-08, 0.01, 10
VMEM_LIMIT = 56 * 1024 * 1024

WEIGHTS = ['mix_norm_g', 'ffn_norm_g', 'conv_w_in', 'conv_b_in', 'conv_w_dw', 'conv_b_dw', 'conv_ln_g', 'conv_ln_b',
           'conv_w_out', 'gdn_w_in', 'gdn_conv_w', 'gdn_a_log', 'gdn_dt_bias', 'gdn_o_norm_g', 'gdn_w_out', 'fox_w_in',
           'fox_b_f', 'fox_q_norm_g', 'fox_k_norm_g', 'fox_w_out', 'ffn_w_up', 'ffn_w_dw', 'ffn_w_down']
REPLICATED = ['mix_norm_g', 'ffn_norm_g', 'gdn_a_log', 'gdn_dt_bias', 'gdn_o_norm_g', 'fox_b_f', 'fox_q_norm_g',
              'fox_k_norm_g']
ROW_SHARDED = ['conv_w_out', 'gdn_w_out', 'fox_w_out', 'ffn_w_down']
MATRICES = ['conv_w_in', 'conv_w_out', 'gdn_w_in', 'gdn_w_out', 'fox_w_in', 'fox_w_out', 'ffn_w_up', 'ffn_w_down']
SHARDED = [n for n in WEIGHTS if n not in REPLICATED]


def _params(*sem):
    return pltpu.CompilerParams(dimension_semantics=sem, vmem_limit_bytes=VMEM_LIMIT)


def _tile(n, cap):
    if n <= cap:
        return n
    d = (cap // LANE) * LANE
    while d >= LANE:
        if n % d == 0:
            return d
        d -= LANE
    raise ValueError(f"no lane-aligned tile of {n} under {cap}")


def _raw_dot(a, b, ca, cb, hp):
    batch = ((0,), (0,)) if a.ndim == 3 else ((), ())
    dn = (((ca,), (cb,)), batch)
    if hp:
        return lax.dot_general(a.astype(F32), b.astype(F32), dn, precision=HIGHEST, preferred_element_type=F32)
    return lax.dot_general(a.astype(BF16), b.astype(BF16), dn, preferred_element_type=F32)


def _raw_nn(a, b, hp=False):
    return _raw_dot(a, b, a.ndim - 1, b.ndim - 2, hp)


def _raw_nt(a, b, hp=False):
    return _raw_dot(a, b, a.ndim - 1, b.ndim - 1, hp)


def _raw_tn(a, b, hp=False):
    return _raw_dot(a, b, a.ndim - 2, b.ndim - 2, hp)


@functools.partial(jax.custom_vjp, nondiff_argnums=(2,))
def _nn(a, b, hp):
    return _raw_nn(a, b, hp)


def _nn_fwd(a, b, hp):
    return _raw_nn(a, b, hp), (a, b)


def _nn_bwd(hp, res, g):
    a, b = res
    return _raw_nt(g, b, hp), _raw_tn(a, g, hp)


_nn.defvjp(_nn_fwd, _nn_bwd)


@functools.partial(jax.custom_vjp, nondiff_argnums=(2,))
def _nt(a, b, hp):
    return _raw_nt(a, b, hp)


def _nt_fwd(a, b, hp):
    return _raw_nt(a, b, hp), (a, b)


def _nt_bwd(hp, res, g):
    a, b = res
    return _raw_nn(g, b, hp), _raw_tn(g, a, hp)


_nt.defvjp(_nt_fwd, _nt_bwd)


def _shift_down(x, s):
    if s == 0:
        return x
    t = lax.broadcasted_iota(jnp.int32, x.shape, 0)
    return jnp.where(t >= s, pltpu.roll(x, s, axis=0), 0.0)


def _shift_up(x, s):
    if s == 0:
        return x
    n = x.shape[0]
    t = lax.broadcasted_iota(jnp.int32, x.shape, 0)
    return jnp.where(t < n - s, pltpu.roll(x, n - s, axis=0), 0.0)


def _row(w, k):
    r = lax.broadcasted_iota(jnp.int32, w.shape, 0)
    return jnp.sum(jnp.where(r == k, w, 0.0), axis=0, keepdims=True)


@jax.custom_vjp
def _dwconv(x, w):
    taps = w.shape[0]
    y = _row(w, taps - 1) * x
    for k in range(taps - 1):
        y = y + _row(w, k) * _shift_down(x, taps - 1 - k)
    return y


def _dwconv_fwd(x, w):
    return _dwconv(x, w), (x, w)


def _dwconv_bwd(res, dy):
    x, w = res
    taps = w.shape[0]
    r = lax.broadcasted_iota(jnp.int32, w.shape, 0)
    dx = _row(w, taps - 1) * dy
    dw = jnp.where(r == taps - 1, jnp.sum(dy * x, axis=0, keepdims=True), 0.0)
    for k in range(taps - 1):
        s = taps - 1 - k
        dx = dx + _row(w, k) * _shift_up(dy, s)
        dw = dw + jnp.where(r == k, jnp.sum(dy * _shift_down(x, s), axis=0, keepdims=True), 0.0)
    return dx, dw


_dwconv.defvjp(_dwconv_fwd, _dwconv_bwd)


def _sigmoid(x):
    return 1.0 / (1.0 + jnp.exp(-x))


def _silu(x):
    return x * _sigmoid(x)


def _softplus(x):
    return jnp.maximum(x, 0.0) + jnp.log(1.0 + jnp.exp(-jnp.abs(x)))


def _head_scale(x, fn):
    tm = x.shape[0]
    x3 = x.reshape(tm, HEADS, DH)
    return (x3 * fn(jnp.sum(x3 * x3, axis=-1, keepdims=True))).reshape(tm, HEADS * DH)


def _tile_lanes(g):
    return jnp.concatenate([g] * HEADS, axis=1)


def _expand_heads(v, first):
    lane = lax.broadcasted_iota(jnp.int32, (LANE, HEADS * DH), 0)
    col = lax.broadcasted_iota(jnp.int32, (LANE, HEADS * DH), 1)
    sel = (lane == col // DH + first).astype(F32)
    return _nn(v, sel, True)


def _f_rms(x, g):
    return (x * lax.rsqrt(jnp.mean(x * x, axis=-1, keepdims=True) + EPS) * g,)


def _f_rms_res(x, g):
    return (_f_rms(x, g)[0], x)


def _f_conf_glu_conv(uv, ug, bv, bg, w):
    return (_dwconv((uv + bv) * _sigmoid(ug + bg), w),)


def _f_conf_ln_silu(cv, b_dw, ln_g, ln_b):
    u = cv + b_dw
    xc = u - jnp.mean(u, axis=-1, keepdims=True)
    y = xc * lax.rsqrt(jnp.mean(xc * xc, axis=-1, keepdims=True) + EPS) * ln_g + ln_b
    return (_silu(y),)


def _f_conv_silu(u, w):
    return (_silu(_dwconv(u, w)),)


def _f_gdn_gates(q, k, ab, a_log, dt_bias):
    qn = _head_scale(q, lambda ss: lax.rsqrt(ss + EPS)) * (DH ** -0.5)
    kn = _head_scale(k, lambda ss: lax.rsqrt(ss + EPS))
    g = -jnp.exp(a_log) * _softplus(ab + dt_bias)
    beta = _sigmoid(ab)
    return qn, kn, _expand_heads(g, 0), _expand_heads(beta, HEADS)


def _f_gdn_prescan(q, k, v, gb, bb):
    c = GDN_CHUNK
    n = q.shape[0] // c
    r3 = lambda t: t.reshape(n, c, DH)
    q3, k3, v3, g3, b3 = r3(q), r3(k), r3(v), r3(gb), r3(bb)
    ii = lax.broadcasted_iota(jnp.int32, (n, c, c), 1)
    jj = lax.broadcasted_iota(jnp.int32, (n, c, c), 2)
    lower, strict = ii >= jj, ii > jj
    gcb = _nn(lower.astype(F32), g3, True)
    gi = gcb[:, :, :c]
    gj = jnp.swapaxes(gi, 1, 2)
    decay = jnp.where(lower, jnp.exp(jnp.where(lower, gi - gj, 0.0)), 0.0)
    kb, vb = k3 * b3, v3 * b3
    a_mat = jnp.where(strict, _nt(kb, k3, False) * decay, 0.0)
    p = -a_mat
    t_mat = (ii == jj).astype(F32) + p
    for _ in range(5):
        p = _nn(p, p, True)
        t_mat = t_mat + _nn(t_mat, p, True)
    eg = jnp.exp(gcb)
    u = _nn(t_mat, vb, False)
    w = _nn(t_mat, kb * eg, False)
    qk = jnp.where(lower, _nt(q3, k3, False) * decay, 0.0)
    qg = q3 * eg
    g_last = jnp.sum(g3, axis=1, keepdims=True)
    kd = k3 * jnp.exp(g_last - gcb)
    r2 = lambda t: t.reshape(n * c, DH)
    return r2(u), r2(w), r2(qg), qk, r2(kd), jnp.exp(g_last)


def _f_gdn_post(o, z, o_g):
    on = _head_scale(o, lambda ss: lax.rsqrt(ss / DH + EPS)) * _tile_lanes(o_g)
    return (on * _silu(z),)


def _f_fox_pre(q, k, f, q_g, k_g, b_f):
    qn = _head_scale(q, lambda ss: lax.rsqrt(ss / DH + EPS)) * _tile_lanes(q_g)
    kn = _head_scale(k, lambda ss: lax.rsqrt(ss / DH + EPS)) * _tile_lanes(k_g)
    return qn, kn, -_softplus(-(f + b_f))


def _f_ffn_mid(ug, uu, wg, wu):
    return (_silu(_dwconv(ug, wg)) * _dwconv(uu, wu),)


def _seg_fwd(name, f, grid, ins, in_specs, out_shapes, out_specs):
    n_in = len(ins)

    def body(*refs):
        outs = f(*[r[...].astype(F32) for r in refs[:n_in]])
        for r, o in zip(refs[n_in:], outs):
            r[...] = o.astype(r.dtype)

    return pl.pallas_call(body, grid=grid, in_specs=in_specs, out_specs=out_specs, out_shape=out_shapes, name=name,
                          compiler_params=_params(*(["parallel"] * len(grid))))(*ins)


def _seg_bwd(name, f, grid, ins, in_specs, douts, dout_specs, want):
    n_in, n_dy = len(ins), len(douts)
    diff = [i for i, w in enumerate(want) if w is not None]
    out_shapes = [jax.ShapeDtypeStruct(ins[i].shape, F32 if want[i] == "acc" else want[i]) for i in diff]
    out_specs = [in_specs[i] for i in diff]
    acc_axis = len(grid) - 1

    def body(*refs):
        vals = [r[...].astype(F32) for r in refs[:n_in]]
        dys = [r[...].astype(F32) for r in refs[n_in:n_in + n_dy]]
        out_refs = refs[n_in + n_dy:]

        def g(*dv):
            full = list(vals)
            for i, v in zip(diff, dv):
                full[i] = v
            return f(*full)

        _, vjp = jax.vjp(g, *[vals[i] for i in diff])
        grads = vjp(tuple(dys))
        first = pl.program_id(acc_axis) == 0
        for i, r, gr in zip(diff, out_refs, grads):
            if want[i] == "acc":
                @pl.when(first)
                def _(r=r, gr=gr):
                    r[...] = gr

                @pl.when(jnp.logical_not(first))
                def _(r=r, gr=gr):
                    r[...] += gr
            else:
                r[...] = gr.astype(r.dtype)

    sem = ["parallel"] * (len(grid) - 1) + ["arbitrary"]
    return pl.pallas_call(body, grid=grid, in_specs=list(in_specs) + list(dout_specs), out_specs=out_specs,
                          out_shape=out_shapes, name=name, compiler_params=_params(*sem))(*ins, *douts)


def _rows(tm, width, col=0):
    return pl.BlockSpec((tm, width), lambda i, col=col: (i, col))


def _const(shape):
    return pl.BlockSpec(shape, lambda i: (0,) * len(shape))


def _cols(t, tc, off=0):
    return pl.BlockSpec((t, tc), lambda j, off=off: (0, j + off))


def _grid2(specs):
    return [pl.BlockSpec(s.block_shape, lambda j, i, f=s.index_map: f(j)) for s in specs]


def _mm(name, a, b, *, ta=False, tb=False, res=None, out_dtype=F32):
    k_dim, m = (a.shape[0], a.shape[1]) if ta else (a.shape[1], a.shape[0])
    n = b.shape[0] if tb else b.shape[1]
    tm, tn, tk = _tile(m, 1408), _tile(n, 1408), _tile(k_dim, 1408)
    nk = k_dim // tk
    grid = (m // tm, n // tn, nk)
    a_spec = pl.BlockSpec((tk, tm), lambda i, j, k: (k, i)) if ta else pl.BlockSpec((tm, tk), lambda i, j, k: (i, k))
    b_spec = pl.BlockSpec((tn, tk), lambda i, j, k: (j, k)) if tb else pl.BlockSpec((tk, tn), lambda i, j, k: (k, j))
    o_spec = pl.BlockSpec((tm, tn), lambda i, j, k: (i, j))
    dn = (((0 if ta else 1,), (1 if tb else 0,)), ((), ()))
    has_res = res is not None

    def body(*refs):
        a_ref, b_ref = refs[0], refs[1]
        res_ref = refs[2] if has_res else None
        o_ref = refs[3] if has_res else refs[2]
        p = lax.dot_general(a_ref[...].astype(BF16), b_ref[...].astype(BF16), dn, preferred_element_type=F32)

        def write(acc):
            if has_res:
                acc = acc + res_ref[...]
            o_ref[...] = acc.astype(o_ref.dtype)

        if nk == 1:
            write(p)
        else:
            acc_ref = refs[-1]
            k = pl.program_id(2)

            @pl.when(k == 0)
            def _():
                acc_ref[...] = p

            @pl.when(k > 0)
            def _():
                acc_ref[...] += p

            @pl.when(k == nk - 1)
            def _():
                write(acc_ref[...])

    ins, specs = [a, b], [a_spec, b_spec]
    if has_res:
        ins.append(res)
        specs.append(o_spec)
    scratch = [pltpu.VMEM((tm, tn), F32)] if nk > 1 else []
    return pl.pallas_call(body, grid=grid, in_specs=specs, out_specs=o_spec, scratch_shapes=scratch,
                          out_shape=jax.ShapeDtypeStruct((m, n), out_dtype), name=name,
                          compiler_params=_params("parallel", "parallel", "arbitrary"))(*ins)


def _head_specs(t):
    n = t // GDN_CHUNK
    col = pl.BlockSpec((t, DH), lambda h: (0, h))
    qk = pl.BlockSpec((None, n, GDN_CHUNK, GDN_CHUNK), lambda h: (h, 0, 0, 0))
    gl = pl.BlockSpec((None, n, 1, DH), lambda h: (h, 0, 0, 0))
    st = pl.BlockSpec((None, n, DH, DH), lambda h: (h, 0, 0, 0))
    return n, col, qk, gl, st


def _gdn_scan_fwd(name, u, w, qg, qk, kd, gl):
    t = u.shape[0]
    n, col, qk_spec, gl_spec, st_spec = _head_specs(t)
    c = GDN_CHUNK

    def body(u_ref, w_ref, qg_ref, qk_ref, kd_ref, gl_ref, o_ref, s_ref):
        def step(i, s):
            rows = pl.ds(pl.multiple_of(i * c, c), c)
            s_ref[i] = s
            vn = u_ref[rows, :] - _raw_nn(w_ref[rows, :], s)
            o_ref[rows, :] = _raw_nn(qg_ref[rows, :], s) + _raw_nn(qk_ref[i], vn)
            return s * gl_ref[i] + _raw_tn(kd_ref[rows, :], vn)

        lax.fori_loop(0, n, step, jnp.zeros((DH, DH), F32))

    return pl.pallas_call(
        body, grid=(HEADS,), in_specs=[col, col, col, qk_spec, col, gl_spec], out_specs=[col, st_spec],
        out_shape=[jax.ShapeDtypeStruct((t, D), F32), jax.ShapeDtypeStruct((HEADS, n, DH, DH), F32)], name=name,
        compiler_params=_params("parallel"))(u, w, qg, qk, kd, gl)


def _gdn_scan_bwd(name, u, w, qg, qk, kd, gl, states, do):
    t = u.shape[0]
    n, col, qk_spec, gl_spec, st_spec = _head_specs(t)
    c = GDN_CHUNK

    def body(u_ref, w_ref, qg_ref, qk_ref, kd_ref, gl_ref, s_ref, do_ref,
             du_ref, dw_ref, dqg_ref, dqk_ref, dkd_ref, dgl_ref):
        def step(r, ds):
            i = n - 1 - r
            rows = pl.ds(pl.multiple_of(i * c, c), c)
            s, do_c, w_c = s_ref[i], do_ref[rows, :], w_ref[rows, :]
            vn = u_ref[rows, :] - _raw_nn(w_c, s)
            dvn = _raw_tn(qk_ref[i], do_c) + _raw_nn(kd_ref[rows, :], ds)
            du_ref[rows, :] = dvn
            dw_ref[rows, :] = -_raw_nt(dvn, s)
            dqg_ref[rows, :] = _raw_nt(do_c, s)
            dqk_ref[i] = _raw_nt(do_c, vn)
            dkd_ref[rows, :] = _raw_nt(vn, ds)
            dgl_ref[i] = jnp.sum(ds * s, axis=0, keepdims=True)
            return _raw_tn(qg_ref[rows, :], do_c) + ds * gl_ref[i] - _raw_tn(w_c, dvn)

        lax.fori_loop(0, n, step, jnp.zeros((DH, DH), F32))

    big = jax.ShapeDtypeStruct((t, D), F32)
    return pl.pallas_call(
        body, grid=(HEADS,), in_specs=[col, col, col, qk_spec, col, gl_spec, st_spec, col],
        out_specs=[col, col, col, qk_spec, col, gl_spec],
        out_shape=[big, big, big, jax.ShapeDtypeStruct(qk.shape, F32), big, jax.ShapeDtypeStruct(gl.shape, F32)],
        name=name, compiler_params=_params("parallel"))(u, w, qg, qk, kd, gl, states, do)


def _cumsum_rows(name, parts, reverse):
    t = parts[0].shape[0]
    blk = min(t, 256)
    nb = t // blk
    n_in = len(parts)

    def body(*refs):
        o_ref = refs[n_in]
        ii = lax.broadcasted_iota(jnp.int32, (blk, blk), 0)
        jj = lax.broadcasted_iota(jnp.int32, (blk, blk), 1)
        tri = ((ii <= jj) if reverse else (ii >= jj)).astype(F32)
        carry = jnp.zeros((1, LANE), F32)
        for b in (range(nb - 1, -1, -1) if reverse else range(nb)):
            rows = pl.ds(b * blk, blk)
            x = refs[0][rows, :]
            for r in refs[1:n_in]:
                x = x + r[rows, :]
            o_ref[rows, :] = _raw_nn(tri, x, True) + carry
            carry = carry + jnp.sum(x, axis=0, keepdims=True)

    return pl.pallas_call(body, out_shape=jax.ShapeDtypeStruct((t, LANE), F32), name=name,
                          compiler_params=_params())(*parts)


def _fox_blocks(t):
    blk = 256 if t % 256 == 0 and t >= 1024 else 128
    return blk, t // blk


def _fox_logits(q, k, cq, ck, row0, col0):
    s = _raw_nt(q, k) * (DH ** -0.5) + cq - ck
    rows = row0 + lax.broadcasted_iota(jnp.int32, s.shape, 0)
    cols = col0 + lax.broadcasted_iota(jnp.int32, s.shape, 1)
    return jnp.where(cols <= rows, s, -jnp.inf)


def _fox_fwd(name, qn, kn, proj, c_col, c_row):
    t = qn.shape[0]
    blk, nb = _fox_blocks(t)
    voff = 2 * HEADS

    def body(q_ref, k_ref, v_ref, cc_ref, cr_ref, o_ref, lse_ref):
        i = pl.program_id(1)
        q, cq = q_ref[...], cc_ref[...]

        def step(j, carry):
            m, l, acc = carry
            rows = pl.ds(pl.multiple_of(j * blk, blk), blk)
            s = _fox_logits(q, k_ref[rows, :], cq, cr_ref[j], i * blk, j * blk)
            m_new = jnp.maximum(m, jnp.max(s, axis=1, keepdims=True))
            p = jnp.exp(s - m_new)
            alpha = jnp.exp(m - m_new)
            return m_new, alpha * l + jnp.sum(p, axis=1, keepdims=True), alpha * acc + _raw_nn(p, v_ref[rows, :])

        init = (jnp.full((blk, 1), -1e30, F32), jnp.zeros((blk, 1), F32), jnp.zeros((blk, DH), F32))
        m, l, acc = lax.fori_loop(0, i + 1, step, init)
        o_ref[...] = acc / l
        lse_ref[...] = m + jnp.log(l)

    tile = pl.BlockSpec((blk, DH), lambda h, i: (i, h))
    colv = pl.BlockSpec((None, blk, 1), lambda h, i: (h, i, 0))
    return pl.pallas_call(
        body, grid=(HEADS, nb),
        in_specs=[tile, pl.BlockSpec((t, DH), lambda h, i: (0, h)), pl.BlockSpec((t, DH), lambda h, i: (0, voff + h)),
                  colv, pl.BlockSpec((None, nb, 1, blk), lambda h, i: (h, 0, 0, 0))],
        out_specs=[tile, colv],
        out_shape=[jax.ShapeDtypeStruct((t, D), F32), jax.ShapeDtypeStruct((HEADS, t, 1), F32)], name=name,
        compiler_params=_params("parallel", "parallel"))(qn, kn, proj, c_col, c_row)


def _fox_dq(name, qn, kn, proj, c_col, c_row, o, lse, do):
    t = qn.shape[0]
    blk, nb = _fox_blocks(t)
    voff = 2 * HEADS

    def body(q_ref, k_ref, v_ref, cc_ref, cr_ref, o_ref, lse_ref, do_ref, dq_ref, dcc_ref, dl_ref):
        i = pl.program_id(1)
        q, cq, do_b, lse_b = q_ref[...], cc_ref[...], do_ref[...], lse_ref[...]
        delta = jnp.sum(do_b * o_ref[...], axis=1, keepdims=True)

        def step(j, carry):
            dq, dcc = carry
            rows = pl.ds(pl.multiple_of(j * blk, blk), blk)
            k = k_ref[rows, :]
            p = jnp.exp(_fox_logits(q, k, cq, cr_ref[j], i * blk, j * blk) - lse_b)
            ds = p * (_raw_nt(do_b, v_ref[rows, :]) - delta)
            return dq + _raw_nn(ds, k), dcc + jnp.sum(ds, axis=1, keepdims=True)

        dq, dcc = lax.fori_loop(0, i + 1, step, (jnp.zeros((blk, DH), F32), jnp.zeros((blk, 1), F32)))
        dq_ref[...] = dq * (DH ** -0.5)
        dcc_ref[...] = dcc
        dl_ref[...] = delta

    tile = pl.BlockSpec((blk, DH), lambda h, i: (i, h))
    colv = pl.BlockSpec((None, blk, 1), lambda h, i: (h, i, 0))
    vec = jax.ShapeDtypeStruct((HEADS, t, 1), F32)
    return pl.pallas_call(
        body, grid=(HEADS, nb),
        in_specs=[tile, pl.BlockSpec((t, DH), lambda h, i: (0, h)), pl.BlockSpec((t, DH), lambda h, i: (0, voff + h)),
                  colv, pl.BlockSpec((None, nb, 1, blk), lambda h, i: (h, 0, 0, 0)), tile, colv, tile],
        out_specs=[tile, colv, colv],
        out_shape=[jax.ShapeDtypeStruct((t, D), F32), vec, vec], name=name,
        compiler_params=_params("parallel", "parallel"))(qn, kn, proj, c_col, c_row, o, lse, do)


def _fox_dkv(name, qn, kn, proj, c_col, c_row, lse, delta, do):
    t = qn.shape[0]
    blk, nb = _fox_blocks(t)
    voff = 2 * HEADS

    def body(q_ref, k_ref, v_ref, cc_ref, cr_ref, lse_ref, dl_ref, do_ref, dk_ref, dv_ref, dcr_ref):
        j = pl.program_id(1)
        k, v, ck = k_ref[...], v_ref[...], cr_ref[...]

        def step(i, carry):
            dk, dv, dcr = carry
            rows = pl.ds(pl.multiple_of(i * blk, blk), blk)
            q, do_b = q_ref[rows, :], do_ref[rows, :]
            p = jnp.exp(_fox_logits(q, k, cc_ref[rows, :], ck, i * blk, j * blk) - lse_ref[rows, :])
            ds = p * (_raw_nt(do_b, v) - dl_ref[rows, :])
            return dk + _raw_tn(ds, q), dv + _raw_tn(p, do_b), dcr - jnp.sum(ds, axis=0, keepdims=True)

        init = (jnp.zeros((blk, DH), F32), jnp.zeros((blk, DH), F32), jnp.zeros((1, blk), F32))
        dk, dv, dcr = lax.fori_loop(j, nb, step, init)
        dk_ref[...] = dk * (DH ** -0.5)
        dv_ref[...] = dv
        dcr_ref[...] = dcr

    full = pl.BlockSpec((t, DH), lambda h, j: (0, h))
    colf = pl.BlockSpec((None, t, 1), lambda h, j: (h, 0, 0))
    tile = pl.BlockSpec((blk, DH), lambda h, j: (j, h))
    rowv = pl.BlockSpec((None, None, 1, blk), lambda h, j: (h, j, 0, 0))
    big = jax.ShapeDtypeStruct((t, D), F32)
    return pl.pallas_call(
        body, grid=(HEADS, nb),
        in_specs=[full, tile, pl.BlockSpec((blk, DH), lambda h, j: (j, voff + h)), colf, rowv, colf, colf, full],
        out_specs=[tile, tile, rowv],
        out_shape=[big, big, jax.ShapeDtypeStruct((HEADS, nb, 1, blk), F32)], name=name,
        compiler_params=_params("parallel", "parallel"))(qn, kn, proj, c_col, c_row, lse, delta, do)


def _loss_head(y, target):
    t = y.shape[0]
    tm = min(t, 512)

    def body(y_ref, t_ref, dy_ref, sum_ref):
        err = y_ref[...] - t_ref[...]
        dy_ref[...] = err * (1.0 / D)
        part = jnp.sum(jnp.sum(err * err, axis=1, keepdims=True), axis=0, keepdims=True)

        @pl.when(pl.program_id(0) == 0)
        def _():
            sum_ref[...] = jnp.zeros_like(sum_ref)

        sum_ref[...] += jnp.broadcast_to(part, sum_ref.shape)

    return pl.pallas_call(
        body, grid=(t // tm,), in_specs=[_rows(tm, D), _rows(tm, D)], out_specs=[_rows(tm, D), _const((1, LANE))],
        out_shape=[jax.ShapeDtypeStruct((t, D), F32), jax.ShapeDtypeStruct((1, LANE), F32)], name="loss_head",
        compiler_params=_params("arbitrary"))(y, target)


def _adam(name, parts, w, m, v):
    r, c = w.shape
    tr = r
    for cand in (512, 256, 128, 64, 32, 16):
        if r > cand and r % cand == 0:
            tr = cand
            break

    def body(p_ref, w_ref, m_ref, v_ref, g_ref, d_ref, nm_ref, nv_ref):
        g = p_ref[0].astype(F32)
        for s in range(1, N_DEV):
            g = g + p_ref[s].astype(F32)
        nm = ADAM_B1 * m_ref[...] + (1.0 - ADAM_B1) * g
        nv = ADAM_B2 * v_ref[...] + (1.0 - ADAM_B2) * (g * g)
        m_hat = nm / (1.0 - ADAM_B1 ** ADAM_STEP)
        v_hat = nv / (1.0 - ADAM_B2 ** ADAM_STEP)
        g_ref[...] = g
        d_ref[...] = -ADAM_LR * (m_hat / (jnp.sqrt(v_hat) + ADAM_EPS) + ADAM_WD * w_ref[...])
        nm_ref[...] = nm
        nv_ref[...] = nv

    blk = pl.BlockSpec((tr, c), lambda i: (i, 0))
    out = jax.ShapeDtypeStruct((r, c), F32)
    return pl.pallas_call(
        body, grid=(r // tr,), in_specs=[pl.BlockSpec((N_DEV, tr, c), lambda i: (0, i, 0)), blk, blk, blk],
        out_specs=[blk] * 4, out_shape=[out] * 4, name=name, compiler_params=_params("parallel"))(parts, w, m, v)


_HBM = pl.BlockSpec(memory_space=pltpu.HBM)
_SEM = pl.BlockSpec(memory_space=pltpu.SEMAPHORE)
_EFFECT = pltpu.SideEffectType.DATAFLOW_SIDE_EFFECTING


def _each_peer(x, y, c):
    flip = lambda v, bit: 1 - v if bit else v
    for p in range(1, N_DEV):
        px, py, pc = flip(x, p & 4), flip(y, p & 2), flip(c, p & 1)
        yield p, (px, py, pc), 4 * px + 2 * py + pc


def _sem(a, p):
    return a * (N_DEV - 1) + p - 1


def _exchange_start(name, arrays, scatter):
    n = len(arrays)
    lands = [lax.empty((N_DEV,) + (a.shape[1:] if sc else a.shape), a.dtype) for a, sc in zip(arrays, scatter)]

    def body(*refs):
        in_refs, land_refs = refs[:n], refs[n:2 * n]
        send_sems, recv_sems, token = refs[2 * n], refs[2 * n + 1], refs[-1]
        x, y, c = lax.axis_index("x"), lax.axis_index("y"), lax.axis_index("c")
        me = 4 * x + 2 * y + c
        for p, coords, peer in _each_peer(x, y, c):
            for a in range(n):
                pltpu.make_async_remote_copy(
                    src_ref=in_refs[a].at[peer] if scatter[a] else in_refs[a], dst_ref=land_refs[a].at[me],
                    send_sem=send_sems.at[_sem(a, p)], recv_sem=recv_sems.at[_sem(a, p)], device_id=coords,
                    device_id_type=pl.DeviceIdType.MESH).start()
        token[...] = jnp.zeros_like(token)

    sems = pltpu.SemaphoreType.DMA((n * (N_DEV - 1),))
    hbm = lambda a: pltpu.HBM(a.shape, a.dtype)
    out = pl.pallas_call(
        body, name=name,
        out_shape=(sems, sems, *[hbm(a) for a in arrays], *[hbm(l) for l in lands],
                   jax.ShapeDtypeStruct((8, LANE), F32)),
        in_specs=[_HBM] * (2 * n), out_specs=(_SEM, _SEM, *[_HBM] * (2 * n), pl.BlockSpec(memory_space=pltpu.VMEM)),
        input_output_aliases={i: 2 + i for i in range(2 * n)},
        compiler_params=pltpu.CompilerParams(has_side_effects=_EFFECT),
    )(*[pltpu.with_memory_space_constraint(a, pltpu.HBM) for a in arrays],
      *[pltpu.with_memory_space_constraint(l, pltpu.HBM) for l in lands])
    return (out[0], out[1], list(out[2:2 + n]), list(out[2 + n:2 + 2 * n]), scatter), out[-1]


def _exchange_wait(name, started, after, me):
    send_sems, recv_sems, sent, lands, scatter = started
    n = len(sent)

    def body(*refs):
        in_refs, land_refs = refs[:n], refs[n:2 * n]
        send_sems, recv_sems = refs[2 * n], refs[2 * n + 1]
        x, y, c = lax.axis_index("x"), lax.axis_index("y"), lax.axis_index("c")
        for p, coords, peer in _each_peer(x, y, c):
            for a in range(n):
                cp = pltpu.make_async_remote_copy(
                    src_ref=in_refs[a].at[peer] if scatter[a] else in_refs[a], dst_ref=land_refs[a].at[peer],
                    send_sem=send_sems.at[_sem(a, p)], recv_sem=recv_sems.at[_sem(a, p)], device_id=coords,
                    device_id_type=pl.DeviceIdType.MESH)
                cp.wait_send()
                cp.wait_recv()

    hbm = lambda a: pltpu.HBM(a.shape, a.dtype)
    out = pl.pallas_call(
        body, name=name, out_shape=(*[hbm(a) for a in sent], *[hbm(l) for l in lands]),
        in_specs=[_HBM] * (2 * n) + [_SEM, _SEM, pl.BlockSpec(memory_space=pl.ANY)], out_specs=[_HBM] * (2 * n),
        input_output_aliases={i: i for i in range(2 * n)},
        compiler_params=pltpu.CompilerParams(has_side_effects=_EFFECT),
    )(*sent, *lands, send_sems, recv_sems, after)
    got = []
    for a in range(n):
        own = lax.dynamic_index_in_dim(out[a], me, 0, keepdims=False) if scatter[a] else out[a]
        got.append(lax.dynamic_update_index_in_dim(out[n + a], own, me, 0))
    return got


def _rms_fwd(name, x, g):
    t = x.shape[0]
    tm = min(t, 512)
    return _seg_fwd(name, _f_rms, (t // tm,), [x, g], [_rows(tm, D), _const((1, D))],
                    [jax.ShapeDtypeStruct((t, D), BF16)], [_rows(tm, D)])[0]


def _rms_bwd(name, x, g, dh, dres):
    t = x.shape[0]
    tm = min(t, 512)
    return _seg_bwd(name, _f_rms_res, (t // tm,), [x, g], [_rows(tm, D), _const((1, D))],
                    [dh, dres], [_rows(tm, D), _rows(tm, D)], [F32, "acc"])


def _ffn_specs(t):
    tc = 256
    nf = D_FF // tc
    return nf, [_cols(t, tc), _cols(t, tc, nf), _cols(FFN_K, tc), _cols(FFN_K, tc, nf)], [_cols(t, tc)]


def _ffn_fwd(tag, x, p):
    t = x.shape[0]
    nf, in_specs, out_specs = _ffn_specs(t)
    h = _rms_fwd(tag + "_rms", x, p["g"])
    u0 = _mm(tag + "_up", h, p["w_up"])
    act = _seg_fwd(tag + "_mid", _f_ffn_mid, (nf,), [u0, u0, p["w_dw"], p["w_dw"]], in_specs,
                   [jax.ShapeDtypeStruct((t, D_FF), BF16)], out_specs)[0]
    out = _mm(tag + "_down", act, p["w_down"], res=x)
    return out, (x, h, u0, act)


def _ffn_bwd(tag, saved, p, dout):
    x, h, u0, act = saved
    t = x.shape[0]
    nf, in_specs, out_specs = _ffn_specs(t)
    g = {"w_down": _mm(tag + "_dwdown", act, dout, ta=True, out_dtype=BF16)}
    dact = _mm(tag + "_dact", dout, p["w_down"], tb=True, out_dtype=BF16)
    dug, duu, dwg, dwu = _seg_bwd(tag + "_dmid", _f_ffn_mid, (nf, 1), [u0, u0, p["w_dw"], p["w_dw"]],
                                  _grid2(in_specs), [dact], _grid2(out_specs), [BF16, BF16, "acc", "acc"])
    du0 = jnp.concatenate([dug[:, :D_FF], duu[:, D_FF:]], axis=1)
    g["w_dw"] = jnp.concatenate([dwg[:, :D_FF], dwu[:, D_FF:]], axis=1)
    g["w_up"] = _mm(tag + "_dwup", h, du0, ta=True, out_dtype=BF16)
    dh = _mm(tag + "_dh", du0, p["w_up"], tb=True, out_dtype=BF16)
    dx, g["g"] = _rms_bwd(tag + "_drms", x, p["g"], dh, dout)
    return dx, g


def _conf_specs(t):
    tc, tm = 128, min(t, 512)
    nc = D // tc
    vec = _const((1, D))
    glu_in = [_cols(t, tc), _cols(t, tc, nc), _cols(1, tc), _cols(1, tc, nc), _cols(CONF_K, tc)]
    return tm, nc, glu_in, [_cols(t, tc)], [_rows(tm, D), vec, vec, vec]


def _conf_fwd(tag, x, p):
    t = x.shape[0]
    tm, nc, glu_in, glu_out, ln_in = _conf_specs(t)
    h = _rms_fwd(tag + "_rms", x, p["g"])
    u = _mm(tag + "_in", h, p["w_in"])
    cv = _seg_fwd(tag + "_gluconv", _f_conf_glu_conv, (nc,), [u, u, p["b_in"], p["b_in"], p["w_dw"]], glu_in,
                  [jax.ShapeDtypeStruct((t, D), F32)], glu_out)[0]
    act = _seg_fwd(tag + "_lnsilu", _f_conf_ln_silu, (t // tm,), [cv, p["b_dw"], p["ln_g"], p["ln_b"]], ln_in,
                   [jax.ShapeDtypeStruct((t, D), BF16)], [_rows(tm, D)])[0]
    out = _mm(tag + "_out", act, p["w_out"], res=x)
    return out, (x, h, u, cv, act)


def _conf_bwd(tag, saved, p, dout):
    x, h, u, cv, act = saved
    t = x.shape[0]
    tm, nc, glu_in, glu_out, ln_in = _conf_specs(t)
    g = {"w_out": _mm(tag + "_dwout", act, dout, ta=True, out_dtype=BF16)}
    dact = _mm(tag + "_dact", dout, p["w_out"], tb=True, out_dtype=BF16)
    dcv, g["b_dw"], g["ln_g"], g["ln_b"] = _seg_bwd(
        tag + "_dlnsilu", _f_conf_ln_silu, (t // tm,), [cv, p["b_dw"], p["ln_g"], p["ln_b"]], ln_in, [dact],
        [_rows(tm, D)], [F32, "acc", "acc", "acc"])
    duv, dug, dbv, dbg, g["w_dw"] = _seg_bwd(
        tag + "_dgluconv", _f_conf_glu_conv, (nc, 1), [u, u, p["b_in"], p["b_in"], p["w_dw"]], _grid2(glu_in), [dcv],
        _grid2(glu_out), [BF16, BF16, "acc", "acc", "acc"])
    du = jnp.concatenate([duv[:, :D], dug[:, D:]], axis=1)
    g["b_in"] = jnp.concatenate([dbv[:, :D], dbg[:, D:]], axis=1)
    g["w_in"] = _mm(tag + "_dwin", h, du, ta=True, out_dtype=BF16)
    dh = _mm(tag + "_dh", du, p["w_in"], tb=True, out_dtype=BF16)
    dx, g["g"] = _rms_bwd(tag + "_drms", x, p["g"], dh, dout)
    return dx, g


def _gdn_specs(t):
    tc, tm, rows = 256, min(t, 256), min(t, 512)
    nq = 3 * D // tc
    conv = ([_cols(t, tc), _cols(GDN_K, tc)], [_cols(t, tc)])
    gate_in = [_rows(tm, D, 0), _rows(tm, D, 1), _rows(tm, LANE, 4 * D // LANE), _const((1, LANE)), _const((1, LANE))]
    gate_out = [_rows(tm, D)] * 4
    head = pl.BlockSpec((rows, DH), lambda h, i: (i, h))
    headv = pl.BlockSpec((rows, DH), lambda h, i: (i, 2 * HEADS + h))
    nch = rows // GDN_CHUNK
    pre_in = [head, head, headv, head, head]
    pre_out = [head, head, head, pl.BlockSpec((None, nch, GDN_CHUNK, GDN_CHUNK), lambda h, i: (h, i, 0, 0)), head,
               pl.BlockSpec((None, nch, 1, DH), lambda h, i: (h, i, 0, 0))]
    post_in = [_rows(tm, D), _rows(tm, D, 3), _const((1, DH))]
    return tm, rows, nq, conv, gate_in, gate_out, pre_in, pre_out, post_in


def _gdn_fwd(tag, x, p):
    t = x.shape[0]
    tm, rows, nq, conv, gate_in, gate_out, pre_in, pre_out, post_in = _gdn_specs(t)
    n = t // GDN_CHUNK
    big = jax.ShapeDtypeStruct((t, D), F32)
    h = _rms_fwd(tag + "_rms", x, p["g"])
    proj = _mm(tag + "_in", h, p["w_in"])
    qkv = _seg_fwd(tag + "_conv", _f_conv_silu, (nq,), [proj, p["conv_w"]], conv[0],
                   [jax.ShapeDtypeStruct((t, 3 * D), F32)], conv[1])[0]
    gate_ins = [qkv, qkv, proj, p["a_log"], p["dt_bias"]]
    qn, kn, gb, bb = _seg_fwd(tag + "_gates", _f_gdn_gates, (t // tm,), gate_ins, gate_in, [big] * 4, gate_out)
    pre_ins = [qn, kn, qkv, gb, bb]
    pre_shapes = [big, big, big, jax.ShapeDtypeStruct((HEADS, n, GDN_CHUNK, GDN_CHUNK), F32), big,
                  jax.ShapeDtypeStruct((HEADS, n, 1, DH), F32)]
    pre = _seg_fwd(tag + "_prescan", _f_gdn_prescan, (HEADS, t // rows), pre_ins, pre_in, pre_shapes, pre_out)
    o, states = _gdn_scan_fwd(tag + "_scan", *pre)
    post_ins = [o, proj, p["o_g"]]
    act = _seg_fwd(tag + "_post", _f_gdn_post, (t // tm,), post_ins, post_in, [jax.ShapeDtypeStruct((t, D), BF16)],
                   [_rows(tm, D)])[0]
    out = _mm(tag + "_out", act, p["w_out"], res=x)
    return out, (x, h, proj, gate_ins, pre_ins, pre, states, post_ins, act)


def _gdn_bwd(tag, saved, p, dout):
    x, h, proj, gate_ins, pre_ins, pre, states, post_ins, act = saved
    t = x.shape[0]
    tm, rows, nq, conv, gate_in, gate_out, pre_in, pre_out, post_in = _gdn_specs(t)
    g = {"w_out": _mm(tag + "_dwout", act, dout, ta=True, out_dtype=BF16)}
    dact = _mm(tag + "_dact", dout, p["w_out"], tb=True, out_dtype=BF16)
    do, dproj_z, g["o_g"] = _seg_bwd(tag + "_dpost", _f_gdn_post, (t // tm,), post_ins, post_in, [dact],
                                     [_rows(tm, D)], [F32, BF16, "acc"])
    dpre = _gdn_scan_bwd(tag + "_dscan", *pre, states, do)
    dqn, dkn, dqkv_v, dgb, dbb = _seg_bwd(tag + "_dprescan", _f_gdn_prescan, (HEADS, t // rows), pre_ins, pre_in,
                                          dpre, pre_out, [F32] * 5)
    dqkv_q, dqkv_k, dproj_ab, g["a_log"], g["dt_bias"] = _seg_bwd(
        tag + "_dgates", _f_gdn_gates, (t // tm,), gate_ins, gate_in, [dqn, dkn, dgb, dbb], gate_out,
        [F32, F32, BF16, "acc", "acc"])
    dqkv = jnp.concatenate([dqkv_q[:, :D], dqkv_k[:, D:2 * D], dqkv_v[:, 2 * D:]], axis=1)
    dqkv_pre, g["conv_w"] = _seg_bwd(tag + "_dconv", _f_conv_silu, (nq, 1), [proj, p["conv_w"]], _grid2(conv[0]),
                                     [dqkv], _grid2(conv[1]), [BF16, "acc"])
    dproj = jnp.concatenate([dqkv_pre[:, :3 * D], dproj_z[:, 3 * D:4 * D], dproj_ab[:, 4 * D:]], axis=1)
    g["w_in"] = _mm(tag + "_dwin", h, dproj, ta=True, out_dtype=BF16)
    dh = _mm(tag + "_dh", dproj, p["w_in"], tb=True, out_dtype=BF16)
    dx, g["g"] = _rms_bwd(tag + "_drms", x, p["g"], dh, dout)
    return dx, g


def _fox_specs(t):
    tm = min(t, 512)
    vec = _const((1, LANE))
    pre_in = [_rows(tm, D, 0), _rows(tm, D, 1), _rows(tm, LANE, 3 * D // LANE), vec, vec, vec]
    pre_out = [_rows(tm, D), _rows(tm, D), _rows(tm, LANE)]
    return tm, pre_in, pre_out


def _per_head(c):
    return jnp.transpose(c[:, :HEADS])


def _per_lane(ch):
    return jnp.pad(jnp.transpose(ch), ((0, 0), (0, LANE - HEADS)))


def _fox_fwd_layer(tag, x, p):
    t = x.shape[0]
    tm, pre_in, pre_out = _fox_specs(t)
    blk, nb = _fox_blocks(t)
    h = _rms_fwd(tag + "_rms", x, p["g"])
    proj = _mm(tag + "_in", h, p["w_in"])
    pre_ins = [proj, proj, proj, p["q_g"], p["k_g"], p["b_f"]]
    qn, kn, lf = _seg_fwd(tag + "_pre", _f_fox_pre, (t // tm,), pre_ins, pre_in,
                          [jax.ShapeDtypeStruct((t, D), BF16)] * 2 + [jax.ShapeDtypeStruct((t, LANE), F32)], pre_out)
    ch = _per_head(_cumsum_rows(tag + "_cumsum", [lf], False))
    c_col, c_row = ch.reshape(HEADS, t, 1), ch.reshape(HEADS, nb, 1, blk)
    o, lse = _fox_fwd(tag + "_attn", qn, kn, proj, c_col, c_row)
    out = _mm(tag + "_out", o, p["w_out"], res=x)
    return out, (x, h, proj, pre_ins, qn, kn, c_col, c_row, o, lse)


def _fox_bwd_layer(tag, saved, p, dout):
    x, h, proj, pre_ins, qn, kn, c_col, c_row, o, lse = saved
    t = x.shape[0]
    tm, pre_in, pre_out = _fox_specs(t)
    g = {"w_out": _mm(tag + "_dwout", o, dout, ta=True, out_dtype=BF16)}
    do = _mm(tag + "_do", dout, p["w_out"], tb=True)
    dqn, dc_col, delta = _fox_dq(tag + "_dq", qn, kn, proj, c_col, c_row, o, lse, do)
    dkn, dv, dc_row = _fox_dkv(tag + "_dkv", qn, kn, proj, c_col, c_row, lse, delta, do)
    dlf = _cumsum_rows(tag + "_dcumsum", [_per_lane(dc_col.reshape(HEADS, t)), _per_lane(dc_row.reshape(HEADS, t))],
                       True)
    dq, dk, df, g["q_g"], g["k_g"], g["b_f"] = _seg_bwd(
        tag + "_dpre", _f_fox_pre, (t // tm,), pre_ins, pre_in, [dqn, dkn, dlf], pre_out,
        [BF16, BF16, BF16, "acc", "acc", "acc"])
    dproj = jnp.concatenate([dq[:, :D], dk[:, D:2 * D], dv.astype(BF16), df[:, 3 * D:]], axis=1)
    g["w_in"] = _mm(tag + "_dwin", h, dproj, ta=True, out_dtype=BF16)
    dh = _mm(tag + "_dh", dproj, p["w_in"], tb=True, out_dtype=BF16)
    dx, g["g"] = _rms_bwd(tag + "_drms", x, p["g"], dh, dout)
    return dx, g


_MIXERS = ((_conf_fwd, _conf_bwd), (_gdn_fwd, _gdn_bwd), (_fox_fwd_layer, _fox_bwd_layer))


def _local_step(x, target, layer_params, on_grads):
    saved, layers = [], []
    for i in range(DEPTH):
        mp, fp = layer_params(i, x)
        layers.append((mp, fp))
        x, sm = _MIXERS[i % N_MIXERS][0](f"l{i}_mix", x, mp)
        x, sf = _ffn_fwd(f"l{i}_ffn", x, fp)
        saved.append((sm, sf))
    dx, sq = _loss_head(x, target)
    for i in reversed(range(DEPTH)):
        mp, fp = layers[i]
        dx, gf = _ffn_bwd(f"l{i}_ffn", saved[i][1], fp, dx)
        dx, gm = _MIXERS[i % N_MIXERS][1](f"l{i}_mix", saved[i][0], mp, dx)
        dx = on_grads(i, gm, gf, dx)
    return sq, dx


def _unshard(name, g):
    axis = g.ndim - 2 if name in ROW_SHARDED else g.ndim - 1
    m = jnp.moveaxis(g, 0, axis - 1)
    return m.reshape(m.shape[:axis - 1] + (N_DEV * m.shape[axis],) + m.shape[axis + 1:])


def _reshard(name, full):
    axis = full.ndim - 2 if name in ROW_SHARDED else full.ndim - 1
    s = full.shape
    return jnp.moveaxis(full.reshape(s[:axis] + (N_DEV, s[axis] // N_DEV) + s[axis + 1:]), axis, 0)


def _pad_cols(a, width):
    return jnp.pad(a, [(0, 0)] * (a.ndim - 1) + [(0, width - a.shape[-1])])


def _lane_vec(v):
    return _pad_cols(v.reshape(1, -1), LANE)


REP_ROWS = 16


def _pack_rep(r):
    small = [_pad_cols(r[n].reshape(1, -1), LANE) for n in REPLICATED[2:]]
    row = jnp.concatenate(small + [jnp.zeros((1, D - LANE * len(small)), F32)], axis=1)
    pad = jnp.zeros((REP_ROWS - 2 * DEPTH - 1, D), F32)
    return jnp.concatenate([r['mix_norm_g'].reshape(DEPTH, D), r['ffn_norm_g'].reshape(DEPTH, D), row, pad], axis=0)


def _unpack_rep(a, shapes):
    out = {'mix_norm_g': a[:DEPTH], 'ffn_norm_g': a[DEPTH:2 * DEPTH]}
    for i, n in enumerate(REPLICATED[2:]):
        out[n] = a[2 * DEPTH:2 * DEPTH + 1, i * LANE:i * LANE + shapes[n][1]].reshape(shapes[n])
    return out


def _view2d(a):
    return a.reshape(-1, a.shape[-1])


MIXER_SHARDED = (
    (('conv_w_in', 'w_in'), ('conv_w_out', 'w_out'), ('conv_b_in', 'b_in'), ('conv_w_dw', 'w_dw'),
     ('conv_b_dw', 'b_dw'), ('conv_ln_g', 'ln_g'), ('conv_ln_b', 'ln_b')),
    (('gdn_w_in', 'w_in'), ('gdn_w_out', 'w_out'), ('gdn_conv_w', 'conv_w')),
    (('fox_w_in', 'w_in'), ('fox_w_out', 'w_out')),
)
FFN_SHARDED = (('ffn_w_up', 'w_up'), ('ffn_w_down', 'w_down'), ('ffn_w_dw', 'w_dw'))
MIXER_REPLICATED = (
    (),
    (('gdn_a_log', 'a_log'), ('gdn_dt_bias', 'dt_bias'), ('gdn_o_norm_g', 'o_g')),
    (('fox_b_f', 'b_f'), ('fox_q_norm_g', 'q_g'), ('fox_k_norm_g', 'k_g')),
)
ROW_VECTORS = ('conv_b_in', 'conv_b_dw', 'conv_ln_g', 'conv_ln_b')
PADDED_COLS = {'gdn_w_in': (4 * D + 2 * HEADS, GDN_PAD), 'fox_w_in': (3 * D + HEADS, FOX_PAD)}
PACK_GROUP = 16 * D


def _layer_entries(i):
    ent = [(n, i // N_MIXERS, 0, k) for n, k in MIXER_SHARDED[i % N_MIXERS]] + [(n, i, 1, k) for n, k in FFN_SHARDED]
    return [e for e in ent if e[0] in MATRICES], [e for e in ent if e[0] not in MATRICES]


def _to_param(name, whole):
    if name in ROW_VECTORS:
        return whole[None]
    if name in PADDED_COLS:
        return _pad_cols(whole, PADDED_COLS[name][1])
    return whole


def _from_grad(name, g):
    if name in ROW_VECTORS:
        return g[0]
    if name in PADDED_COLS:
        return g[:, :PADDED_COLS[name][0]]
    return g


def _pack_rows(parts, lead):
    out = []
    for a in parts:
        flat = a.reshape(a.shape[:lead] + (-1,))
        size = flat.shape[-1]
        padded = -(-size // PACK_GROUP) * PACK_GROUP
        flat = jnp.pad(flat, [(0, 0)] * lead + [(0, padded - size)])
        out.append(flat.reshape(a.shape[:lead] + (padded // D, D)))
    return jnp.concatenate(out, axis=lead)


def _unpack_rows(packed, shapes, lead):
    out, row = [], 0
    head = packed.shape[:lead]
    for s in shapes:
        size = 1
        for d in s:
            size *= d
        rows = -(-size // PACK_GROUP) * (PACK_GROUP // D)
        part = lax.slice_in_dim(packed, row, row + rows, axis=lead)
        out.append(part.reshape(head + (rows * D,))[..., :size].reshape(head + tuple(s)))
        row += rows
    return out


def _layer_dicts(i, whole, rep):
    kind, j = i % N_MIXERS, i // N_MIXERS
    mp = {k: _to_param(n, whole[n]) for n, k in MIXER_SHARDED[kind]}
    for n, k in MIXER_REPLICATED[kind]:
        mp[k] = rep[n][j][None] if rep[n].shape[-1] == DH else _lane_vec(rep[n][j])
    mp["g"] = rep['mix_norm_g'][i][None]
    fp = {k: _to_param(n, whole[n]) for n, k in FFN_SHARDED}
    fp["g"] = rep['ffn_norm_g'][i][None]
    return mp, fp


def _layer_grads(i, gm, gf):
    out = {n: _from_grad(n, gm[k]) for n, k in MIXER_SHARDED[i % N_MIXERS]}
    out.update({n: _from_grad(n, gf[k]) for n, k in FFN_SHARDED})
    return out


def _replicated_grads(grads):
    rep = {'mix_norm_g': jnp.concatenate([g[0]["g"] for g in grads]),
           'ffn_norm_g': jnp.concatenate([g[1]["g"] for g in grads])}
    for kind in range(N_MIXERS):
        for n, k in MIXER_REPLICATED[kind]:
            rep[n] = jnp.stack([grads[i][0][k] for i in range(kind, DEPTH, N_MIXERS)])
    return rep


def kernel(x, mix_norm_g, ffn_norm_g, conv_w_in, conv_b_in, conv_w_dw, conv_b_dw, conv_ln_g, conv_ln_b, conv_w_out, gdn_w_in, gdn_conv_w, gdn_a_log, gdn_dt_bias, gdn_o_norm_g, gdn_w_out, fox_w_in, fox_b_f, fox_q_norm_g, fox_k_norm_g, fox_w_out, ffn_w_up, ffn_w_dw, ffn_w_down, loss_target, m_mix_norm_g, m_ffn_norm_g, m_conv_w_in, m_conv_b_in, m_conv_w_dw, m_conv_b_dw, m_conv_ln_g, m_conv_ln_b, m_conv_w_out, m_gdn_w_in, m_gdn_conv_w, m_gdn_a_log, m_gdn_dt_bias, m_gdn_o_norm_g, m_gdn_w_out, m_fox_w_in, m_fox_b_f, m_fox_q_norm_g, m_fox_k_norm_g, m_fox_w_out, m_ffn_w_up, m_ffn_w_dw, m_ffn_w_down, v_mix_norm_g, v_ffn_norm_g, v_conv_w_in, v_conv_b_in, v_conv_w_dw, v_conv_b_dw, v_conv_ln_g, v_conv_ln_b, v_conv_w_out, v_gdn_w_in, v_gdn_conv_w, v_gdn_a_log, v_gdn_dt_bias, v_gdn_o_norm_g, v_gdn_w_out, v_fox_w_in, v_fox_b_f, v_fox_q_norm_g, v_fox_k_norm_g, v_fox_w_out, v_ffn_w_up, v_ffn_w_dw, v_ffn_w_down):
    given = dict(zip(
        WEIGHTS + ["m_" + n for n in WEIGHTS] + ["v_" + n for n in WEIGHTS],
        (mix_norm_g, ffn_norm_g, conv_w_in, conv_b_in, conv_w_dw, conv_b_dw, conv_ln_g, conv_ln_b, conv_w_out, gdn_w_in, gdn_conv_w, gdn_a_log, gdn_dt_bias, gdn_o_norm_g, gdn_w_out, fox_w_in, fox_b_f, fox_q_norm_g, fox_k_norm_g, fox_w_out, ffn_w_up, ffn_w_dw, ffn_w_down,
         m_mix_norm_g, m_ffn_norm_g, m_conv_w_in, m_conv_b_in, m_conv_w_dw, m_conv_b_dw, m_conv_ln_g, m_conv_ln_b, m_conv_w_out, m_gdn_w_in, m_gdn_conv_w, m_gdn_a_log, m_gdn_dt_bias, m_gdn_o_norm_g, m_gdn_w_out, m_fox_w_in, m_fox_b_f, m_fox_q_norm_g, m_fox_k_norm_g, m_fox_w_out, m_ffn_w_up, m_ffn_w_dw, m_ffn_w_down,
         v_mix_norm_g, v_ffn_norm_g, v_conv_w_in, v_conv_b_in, v_conv_w_dw, v_conv_b_dw, v_conv_ln_g, v_conv_ln_b, v_conv_w_out, v_gdn_w_in, v_gdn_conv_w, v_gdn_a_log, v_gdn_dt_bias, v_gdn_o_norm_g, v_gdn_w_out, v_fox_w_in, v_fox_b_f, v_fox_q_norm_g, v_fox_k_norm_g, v_fox_w_out, v_ffn_w_up, v_ffn_w_dw, v_ffn_w_down)))

    me = 4 * lax.axis_index("x") + 2 * lax.axis_index("y") + lax.axis_index("c")

    gathers, token = [], jnp.zeros((1, 1), F32)
    for i in range(DEPTH):
        mats, smalls = _layer_entries(i)
        packs = [_pack_rows([given[n][j].astype(BF16) for n, j, _, _ in mats], 0),
                 _pack_rows([given[n][j] for n, j, _, _ in smalls], 0)]
        started, tok = _exchange_start(f"gather{i}_start", packs, [False, False])
        gathers.append(started)
        token = token + tok[0:1, 0:1]

    def layer_params(i, x_in):
        mats, smalls = _layer_entries(i)
        got = _exchange_wait(f"gather{i}_wait", gathers[i], token if i == 0 else x_in, me)
        whole = {}
        for ent, landed in zip((mats, smalls), got):
            shapes = [given[n].shape[1:] for n, _, _, _ in ent]
            for (n, _, _, _), g in zip(ent, _unpack_rows(landed, shapes, 1)):
                whole[n] = _unshard(n, g)
        return _layer_dicts(i, whole, given)

    grads, exchanges = [None] * DEPTH, [None] * DEPTH

    def on_grads(i, gm, gf, dx):
        grads[i] = (gm, gf)
        mats, smalls = _layer_entries(i)
        whole = _layer_grads(i, gm, gf)
        packs = [_pack_rows([_reshard(n, whole[n]) for n, _, _, _ in ent], 1) for ent in (mats, smalls)]
        if i == 0:
            packs.append(_pack_rep(_replicated_grads(grads)))
        exchanges[i], tok = _exchange_start(f"grads{i}_start", packs, [True, True] + [False] * (i == 0))
        return dx + tok[0:1, 0:1] if i > 0 else dx

    sq, dx = _local_step(x[0], loss_target[0], layer_params, on_grads)
    loss = (0.5 / D) * lax.psum(sq[0, 0], ("x", "y", "c"))

    pieces = {n: [] for n in SHARDED}
    for i in range(DEPTH):
        got = _exchange_wait(f"grads{i}_wait", exchanges[i], dx, me)
        if i == 0:
            rep_parts = got[2]
        for ent, landed in zip(_layer_entries(i), got):
            shapes = [given[n].shape[1:] for n, _, _, _ in ent]
            for (n, _, _, _), part in zip(ent, _unpack_rows(landed, shapes, 1)):
                pieces[n].append(part)
    new = {}
    for n in SHARDED:
        w2 = _view2d(given[n])
        outs = _adam("adam_" + n, jnp.stack(pieces[n], axis=1).reshape((N_DEV,) + w2.shape), w2,
                     _view2d(given["m_" + n]), _view2d(given["v_" + n]))
        new[n] = [o.reshape(given[n].shape) for o in outs]
    packed = [_pack_rep({n: given[pre + n] for n in REPLICATED}) for pre in ("", "m_", "v_")]
    outs = _adam("adam_replicated", rep_parts, *packed)
    unpacked = [_unpack_rep(o, {n: given[n].shape for n in REPLICATED}) for o in outs]
    for n in REPLICATED:
        new[n] = [u[n] for u in unpacked]
    return (loss, dx[None], *[new[n][0] for n in WEIGHTS], *[new[n][1] for n in WEIGHTS],
            *[new[n][2] for n in WEIGHTS], *[new[n][3] for n in WEIGHTS])
```

```python
import functools

import jax
import jax.numpy as jnp
from jax import lax
from jax.experimental import pallas as pl
from jax.experimental.pallas import tpu as pltpu

F32 = jnp.float32
BF16 = jnp.bfloat16
HIGHEST = lax.Precision.HIGHEST

N_DEV = 8
LANE = 128
EPS = 1e-6
DEPTH = 4
N_MIXERS = 3
HEADS = 8
DH = 128
D = HEADS * DH
D_FF = 2816
CONF_K, GDN_K, FFN_K = 31, 4, 3
GDN_CHUNK = 64
GDN_PAD = 4224
FOX_PAD = 3200
ADAM_LR, ADAM_B1, ADAM_B2, ADAM_EPS, ADAM_WD, ADAM_STEP = 0.001, 0.9, 0.999, 1e-08, 0.01, 10
VMEM_LIMIT = 56 * 1024 * 1024

WEIGHTS = ['mix_norm_g', 'ffn_norm_g', 'conv_w_in', 'conv_b_in', 'conv_w_dw', 'conv_b_dw', 'conv_ln_g', 'conv_ln_b',
           'conv_w_out', 'gdn_w_in', 'gdn_conv_w', 'gdn_a_log', 'gdn_dt_bias', 'gdn_o_norm_g', 'gdn_w_out', 'fox_w_in',
           'fox_b_f', 'fox_q_norm_g', 'fox_k_norm_g', 'fox_w_out', 'ffn_w_up', 'ffn_w_dw', 'ffn_w_down']
REPLICATED = ['mix_norm_g', 'ffn_norm_g', 'gdn_a_log', 'gdn_dt_bias', 'gdn_o_norm_g', 'fox_b_f', 'fox_q_norm_g',
              'fox_k_norm_g']
ROW_SHARDED = ['conv_w_out', 'gdn_w_out', 'fox_w_out', 'ffn_w_down']
MATRICES = ['conv_w_in', 'conv_w_out', 'gdn_w_in', 'gdn_w_out', 'fox_w_in', 'fox_w_out', 'ffn_w_up', 'ffn_w_down']
SHARDED = [n for n in WEIGHTS if n not in REPLICATED]


def _params(*sem):
    return pltpu.CompilerParams(dimension_semantics=sem, vmem_limit_bytes=VMEM_LIMIT)


def _tile(n, cap):
    if n <= cap:
        return n
    d = (cap // LANE) * LANE
    while d >= LANE:
        if n % d == 0:
            return d
        d -= LANE
    raise ValueError(f"no lane-aligned tile of {n} under {cap}")


def _raw_dot(a, b, ca, cb, hp):
    batch = ((0,), (0,)) if a.ndim == 3 else ((), ())
    dn = (((ca,), (cb,)), batch)
    if hp:
        return lax.dot_general(a.astype(F32), b.astype(F32), dn, precision=HIGHEST, preferred_element_type=F32)
    return lax.dot_general(a.astype(BF16), b.astype(BF16), dn, preferred_element_type=F32)


def _raw_nn(a, b, hp=False):
    return _raw_dot(a, b, a.ndim - 1, b.ndim - 2, hp)


def _raw_nt(a, b, hp=False):
    return _raw_dot(a, b, a.ndim - 1, b.ndim - 1, hp)


def _raw_tn(a, b, hp=False):
    return _raw_dot(a, b, a.ndim - 2, b.ndim - 2, hp)


@functools.partial(jax.custom_vjp, nondiff_argnums=(2,))
def _nn(a, b, hp):
    return _raw_nn(a, b, hp)


def _nn_fwd(a, b, hp):
    return _raw_nn(a, b, hp), (a, b)


def _nn_bwd(hp, res, g):
    a, b = res
    return _raw_nt(g, b, hp), _raw_tn(a, g, hp)


_nn.defvjp(_nn_fwd, _nn_bwd)


@functools.partial(jax.custom_vjp, nondiff_argnums=(2,))
def _nt(a, b, hp):
    return _raw_nt(a, b, hp)


def _nt_fwd(a, b, hp):
    return _raw_nt(a, b, hp), (a, b)


def _nt_bwd(hp, res, g):
    a, b = res
    return _raw_nn(g, b, hp), _raw_tn(g, a, hp)


_nt.defvjp(_nt_fwd, _nt_bwd)


def _shift_down(x, s):
    if s == 0:
        return x
    t = lax.broadcasted_iota(jnp.int32, x.shape, 0)
    return jnp.where(t >= s, pltpu.roll(x, s, axis=0), 0.0)


def _shift_up(x, s):
    if s == 0:
        return x
    n = x.shape[0]
    t = lax.broadcasted_iota(jnp.int32, x.shape, 0)
    return jnp.where(t < n - s, pltpu.roll(x, n - s, axis=0), 0.0)


def _row(w, k):
    r = lax.broadcasted_iota(jnp.int32, w.shape, 0)
    return jnp.sum(jnp.where(r == k, w, 0.0), axis=0, keepdims=True)


@jax.custom_vjp
def _dwconv(x, w):
    taps = w.shape[0]
    y = _row(w, taps - 1) * x
    for k in range(taps - 1):
        y = y + _row(w, k) * _shift_down(x, taps - 1 - k)
    return y


def _dwconv_fwd(x, w):
    return _dwconv(x, w), (x, w)


def _dwconv_bwd(res, dy):
    x, w = res
    taps = w.shape[0]
    r = lax.broadcasted_iota(jnp.int32, w.shape, 0)
    dx = _row(w, taps - 1) * dy
    dw = jnp.where(r == taps - 1, jnp.sum(dy * x, axis=0, keepdims=True), 0.0)
    for k in range(taps - 1):
        s = taps - 1 - k
        dx = dx + _row(w, k) * _shift_up(dy, s)
        dw = dw + jnp.where(r == k, jnp.sum(dy * _shift_down(x, s), axis=0, keepdims=True), 0.0)
    return dx, dw


_dwconv.defvjp(_dwconv_fwd, _dwconv_bwd)


def _sigmoid(x):
    return 1.0 / (1.0 + jnp.exp(-x))


def _silu(x):
    return x * _sigmoid(x)


def _softplus(x):
    return jnp.maximum(x, 0.0) + jnp.log(1.0 + jnp.exp(-jnp.abs(x)))


def _head_scale(x, fn):
    tm = x.shape[0]
    x3 = x.reshape(tm, HEADS, DH)
    return (x3 * fn(jnp.sum(x3 * x3, axis=-1, keepdims=True))).reshape(tm, HEADS * DH)


def _tile_lanes(g):
    return jnp.concatenate([g] * HEADS, axis=1)


def _expand_heads(v, first):
    lane = lax.broadcasted_iota(jnp.int32, (LANE, HEADS * DH), 0)
    col = lax.broadcasted_iota(jnp.int32, (LANE, HEADS * DH), 1)
    sel = (lane == col // DH + first).astype(F32)
    return _nn(v, sel, True)


def _f_rms(x, g):
    return (x * lax.rsqrt(jnp.mean(x * x, axis=-1, keepdims=True) + EPS) * g,)


def _f_rms_res(x, g):
    return (_f_rms(x, g)[0], x)


def _f_conf_glu_conv(uv, ug, bv, bg, w):
    return (_dwconv((uv + bv) * _sigmoid(ug + bg), w),)


def _f_conf_ln_silu(cv, b_dw, ln_g, ln_b):
    u = cv + b_dw
    xc = u - jnp.mean(u, axis=-1, keepdims=True)
    y = xc * lax.rsqrt(jnp.mean(xc * xc, axis=-1, keepdims=True) + EPS) * ln_g + ln_b
    return (_silu(y),)


def _f_conv_silu(u, w):
    return (_silu(_dwconv(u, w)),)


def _f_gdn_gates(q, k, ab, a_log, dt_bias):
    qn = _head_scale(q, lambda ss: lax.rsqrt(ss + EPS)) * (DH ** -0.5)
    kn = _head_scale(k, lambda ss: lax.rsqrt(ss + EPS))
    g = -jnp.exp(a_log) * _softplus(ab + dt_bias)
    beta = _sigmoid(ab)
    return qn, kn, _expand_heads(g, 0), _expand_heads(beta, HEADS)


def _f_gdn_prescan(q, k, v, gb, bb):
    c = GDN_CHUNK
    n = q.shape[0] // c
    r3 = lambda t: t.reshape(n, c, DH)
    q3, k3, v3, g3, b3 = r3(q), r3(k), r3(v), r3(gb), r3(bb)
    ii = lax.broadcasted_iota(jnp.int32, (n, c, c), 1)
    jj = lax.broadcasted_iota(jnp.int32, (n, c, c), 2)
    lower, strict = ii >= jj, ii > jj
    gcb = _nn(lower.astype(F32), g3, True)
    gi = gcb[:, :, :c]
    gj = jnp.swapaxes(gi, 1, 2)
    decay = jnp.where(lower, jnp.exp(jnp.where(lower, gi - gj, 0.0)), 0.0)
    kb, vb = k3 * b3, v3 * b3
    a_mat = jnp.where(strict, _nt(kb, k3, False) * decay, 0.0)
    p = -a_mat
    t_mat = (ii == jj).astype(F32) + p
    for _ in range(5):
        p = _nn(p, p, True)
        t_mat = t_mat + _nn(t_mat, p, True)
    eg = jnp.exp(gcb)
    u = _nn(t_mat, vb, False)
    w = _nn(t_mat, kb * eg, False)
    qk = jnp.where(lower, _nt(q3, k3, False) * decay, 0.0)
    qg = q3 * eg
    g_last = jnp.sum(g3, axis=1, keepdims=True)
    kd = k3 * jnp.exp(g_last - gcb)
    r2 = lambda t: t.reshape(n * c, DH)
    return r2(u), r2(w), r2(qg), qk, r2(kd), jnp.exp(g_last)


def _f_gdn_post(o, z, o_g):
    on = _head_scale(o, lambda ss: lax.rsqrt(ss / DH + EPS)) * _tile_lanes(o_g)
    return (on * _silu(z),)


def _f_fox_pre(q, k, f, q_g, k_g, b_f):
    qn = _head_scale(q, lambda ss: lax.rsqrt(ss / DH + EPS)) * _tile_lanes(q_g)
    kn = _head_scale(k, lambda ss: lax.rsqrt(ss / DH + EPS)) * _tile_lanes(k_g)
    return qn, kn, -_softplus(-(f + b_f))


def _f_ffn_mid(ug, uu, wg, wu):
    return (_silu(_dwconv(ug, wg)) * _dwconv(uu, wu),)


def _seg_fwd(name, f, grid, ins, in_specs, out_shapes, out_specs):
    n_in = len(ins)

    def body(*refs):
        outs = f(*[r[...].astype(F32) for r in refs[:n_in]])
        for r, o in zip(refs[n_in:], outs):
            r[...] = o.astype(r.dtype)

    return pl.pallas_call(body, grid=grid, in_specs=in_specs, out_specs=out_specs, out_shape=out_shapes, name=name,
                          compiler_params=_params(*(["parallel"] * len(grid))))(*ins)


def _seg_bwd(name, f, grid, ins, in_specs, douts, dout_specs, want):
    n_in, n_dy = len(ins), len(douts)
    diff = [i for i, w in enumerate(want) if w is not None]
    out_shapes = [jax.ShapeDtypeStruct(ins[i].shape, F32 if want[i] == "acc" else want[i]) for i in diff]
    out_specs = [in_specs[i] for i in diff]
    acc_axis = len(grid) - 1

    def body(*refs):
        vals = [r[...].astype(F32) for r in refs[:n_in]]
        dys = [r[...].astype(F32) for r in refs[n_in:n_in + n_dy]]
        out_refs = refs[n_in + n_dy:]

        def g(*dv):
            full = list(vals)
            for i, v in zip(diff, dv):
                full[i] = v
            return f(*full)

        _, vjp = jax.vjp(g, *[vals[i] for i in diff])
        grads = vjp(tuple(dys))
        first = pl.program_id(acc_axis) == 0
        for i, r, gr in zip(diff, out_refs, grads):
            if want[i] == "acc":
                @pl.when(first)
                def _(r=r, gr=gr):
                    r[...] = gr

                @pl.when(jnp.logical_not(first))
                def _(r=r, gr=gr):
                    r[...] += gr
            else:
                r[...] = gr.astype(r.dtype)

    sem = ["parallel"] * (len(grid) - 1) + ["arbitrary"]
    return pl.pallas_call(body, grid=grid, in_specs=list(in_specs) + list(dout_specs), out_specs=out_specs,
                          out_shape=out_shapes, name=name, compiler_params=_params(*sem))(*ins, *douts)


def _rows(tm, width, col=0):
    return pl.BlockSpec((tm, width), lambda i, col=col: (i, col))


def _const(shape):
    return pl.BlockSpec(shape, lambda i: (0,) * len(shape))


def _cols(t, tc, off=0):
    return pl.BlockSpec((t, tc), lambda j, off=off: (0, j + off))


def _grid2(specs):
    return [pl.BlockSpec(s.block_shape, lambda j, i, f=s.index_map: f(j)) for s in specs]


def _mm(name, a, b, *, ta=False, tb=False, res=None, out_dtype=F32):
    k_dim, m = (a.shape[0], a.shape[1]) if ta else (a.shape[1], a.shape[0])
    n = b.shape[0] if tb else b.shape[1]
    tm, tn, tk = _tile(m, 1408), _tile(n, 1408), _tile(k_dim, 1408)
    nk = k_dim // tk
    grid = (m // tm, n // tn, nk)
    a_spec = pl.BlockSpec((tk, tm), lambda i, j, k: (k, i)) if ta else pl.BlockSpec((tm, tk), lambda i, j, k: (i, k))
    b_spec = pl.BlockSpec((tn, tk), lambda i, j, k: (j, k)) if tb else pl.BlockSpec((tk, tn), lambda i, j, k: (k, j))
    o_spec = pl.BlockSpec((tm, tn), lambda i, j, k: (i, j))
    dn = (((0 if ta else 1,), (1 if tb else 0,)), ((), ()))
    has_res = res is not None

    def body(*refs):
        a_ref, b_ref = refs[0], refs[1]
        res_ref = refs[2] if has_res else None
        o_ref = refs[3] if has_res else refs[2]
        p = lax.dot_general(a_ref[...].astype(BF16), b_ref[...].astype(BF16), dn, preferred_element_type=F32)

        def write(acc):
            if has_res:
                acc = acc + res_ref[...]
            o_ref[...] = acc.astype(o_ref.dtype)

        if nk == 1:
            write(p)
        else:
            acc_ref = refs[-1]
            k = pl.program_id(2)

            @pl.when(k == 0)
            def _():
                acc_ref[...] = p

            @pl.when(k > 0)
            def _():
                acc_ref[...] += p

            @pl.when(k == nk - 1)
            def _():
                write(acc_ref[...])

    ins, specs = [a, b], [a_spec, b_spec]
    if has_res:
        ins.append(res)
        specs.append(o_spec)
    scratch = [pltpu.VMEM((tm, tn), F32)] if nk > 1 else []
    return pl.pallas_call(body, grid=grid, in_specs=specs, out_specs=o_spec, scratch_shapes=scratch,
                          out_shape=jax.ShapeDtypeStruct((m, n), out_dtype), name=name,
                          compiler_params=_params("parallel", "parallel", "arbitrary"))(*ins)


def _head_specs(t):
    n = t // GDN_CHUNK
    col = pl.BlockSpec((t, DH), lambda h: (0, h))
    qk = pl.BlockSpec((None, n, GDN_CHUNK, GDN_CHUNK), lambda h: (h, 0, 0, 0))
    gl = pl.BlockSpec((None, n, 1, DH), lambda h: (h, 0, 0, 0))
    st = pl.BlockSpec((None, n, DH, DH), lambda h: (h, 0, 0, 0))
    return n, col, qk, gl, st


def _gdn_scan_fwd(name, u, w, qg, qk, kd, gl):
    t = u.shape[0]
    n, col, qk_spec, gl_spec, st_spec = _head_specs(t)
    c = GDN_CHUNK

    def body(u_ref, w_ref, qg_ref, qk_ref, kd_ref, gl_ref, o_ref, s_ref):
        def step(i, s):
            rows = pl.ds(pl.multiple_of(i * c, c), c)
            s_ref[i] = s
            vn = u_ref[rows, :] - _raw_nn(w_ref[rows, :], s)
            o_ref[rows, :] = _raw_nn(qg_ref[rows, :], s) + _raw_nn(qk_ref[i], vn)
            return s * gl_ref[i] + _raw_tn(kd_ref[rows, :], vn)

        lax.fori_loop(0, n, step, jnp.zeros((DH, DH), F32))

    return pl.pallas_call(
        body, grid=(HEADS,), in_specs=[col, col, col, qk_spec, col, gl_spec], out_specs=[col, st_spec],
        out_shape=[jax.ShapeDtypeStruct((t, D), F32), jax.ShapeDtypeStruct((HEADS, n, DH, DH), F32)], name=name,
        compiler_params=_params("parallel"))(u, w, qg, qk, kd, gl)


def _gdn_scan_bwd(name, u, w, qg, qk, kd, gl, states, do):
    t = u.shape[0]
    n, col, qk_spec, gl_spec, st_spec = _head_specs(t)
    c = GDN_CHUNK

    def body(u_ref, w_ref, qg_ref, qk_ref, kd_ref, gl_ref, s_ref, do_ref,
             du_ref, dw_ref, dqg_ref, dqk_ref, dkd_ref, dgl_ref):
        def step(r, ds):
            i = n - 1 - r
            rows = pl.ds(pl.multiple_of(i * c, c), c)
            s, do_c, w_c = s_ref[i], do_ref[rows, :], w_ref[rows, :]
            vn = u_ref[rows, :] - _raw_nn(w_c, s)
            dvn = _raw_tn(qk_ref[i], do_c) + _raw_nn(kd_ref[rows, :], ds)
            du_ref[rows, :] = dvn
            dw_ref[rows, :] = -_raw_nt(dvn, s)
            dqg_ref[rows, :] = _raw_nt(do_c, s)
            dqk_ref[i] = _raw_nt(do_c, vn)
            dkd_ref[rows, :] = _raw_nt(vn, ds)
            dgl_ref[i] = jnp.sum(ds * s, axis=0, keepdims=True)
            return _raw_tn(qg_ref[rows, :], do_c) + ds * gl_ref[i] - _raw_tn(w_c, dvn)

        lax.fori_loop(0, n, step, jnp.zeros((DH, DH), F32))

    big = jax.ShapeDtypeStruct((t, D), F32)
    return pl.pallas_call(
        body, grid=(HEADS,), in_specs=[col, col, col, qk_spec, col, gl_spec, st_spec, col],
        out_specs=[col, col, col, qk_spec, col, gl_spec],
        out_shape=[big, big, big, jax.ShapeDtypeStruct(qk.shape, F32), big, jax.ShapeDtypeStruct(gl.shape, F32)],
        name=name, compiler_params=_params("parallel"))(u, w, qg, qk, kd, gl, states, do)


def _cumsum_rows(name, parts, reverse):
    t = parts[0].shape[0]
    blk = min(t, 256)
    nb = t // blk
    n_in = len(parts)

    def body(*refs):
        o_ref = refs[n_in]
        ii = lax.broadcasted_iota(jnp.int32, (blk, blk), 0)
        jj = lax.broadcasted_iota(jnp.int32, (blk, blk), 1)
        tri = ((ii <= jj) if reverse else (ii >= jj)).astype(F32)
        carry = jnp.zeros((1, LANE), F32)
        for b in (range(nb - 1, -1, -1) if reverse else range(nb)):
            rows = pl.ds(b * blk, blk)
            x = refs[0][rows, :]
            for r in refs[1:n_in]:
                x = x + r[rows, :]
            o_ref[rows, :] = _raw_nn(tri, x, True) + carry
            carry = carry + jnp.sum(x, axis=0, keepdims=True)

    return pl.pallas_call(body, out_shape=jax.ShapeDtypeStruct((t, LANE), F32), name=name,
                          compiler_params=_params())(*parts)


def _fox_blocks(t):
    blk = 256 if t % 256 == 0 and t >= 1024 else 128
    return blk, t // blk


def _fox_logits(q, k, cq, ck, row0, col0):
    s = _raw_nt(q, k) * (DH ** -0.5) + cq - ck
    rows = row0 + lax.broadcasted_iota(jnp.int32, s.shape, 0)
    cols = col0 + lax.broadcasted_iota(jnp.int32, s.shape, 1)
    return jnp.where(cols <= rows, s, -jnp.inf)


def _fox_fwd(name, qn, kn, proj, c_col, c_row):
    t = qn.shape[0]
    blk, nb = _fox_blocks(t)
    voff = 2 * HEADS

    def body(q_ref, k_ref, v_ref, cc_ref, cr_ref, o_ref, lse_ref):
        i = pl.program_id(1)
        q, cq = q_ref[...], cc_ref[...]

        def step(j, carry):
            m, l, acc = carry
            rows = pl.ds(pl.multiple_of(j * blk, blk), blk)
            s = _fox_logits(q, k_ref[rows, :], cq, cr_ref[j], i * blk, j * blk)
            m_new = jnp.maximum(m, jnp.max(s, axis=1, keepdims=True))
            p = jnp.exp(s - m_new)
            alpha = jnp.exp(m - m_new)
            return m_new, alpha * l + jnp.sum(p, axis=1, keepdims=True), alpha * acc + _raw_nn(p, v_ref[rows, :])

        init = (jnp.full((blk, 1), -1e30, F32), jnp.zeros((blk, 1), F32), jnp.zeros((blk, DH), F32))
        m, l, acc = lax.fori_loop(0, i + 1, step, init)
        o_ref[...] = acc / l
        lse_ref[...] = m + jnp.log(l)

    tile = pl.BlockSpec((blk, DH), lambda h, i: (i, h))
    colv = pl.BlockSpec((None, blk, 1), lambda h, i: (h, i, 0))
    return pl.pallas_call(
        body, grid=(HEADS, nb),
        in_specs=[tile, pl.BlockSpec((t, DH), lambda h, i: (0, h)), pl.BlockSpec((t, DH), lambda h, i: (0, voff + h)),
                  colv, pl.BlockSpec((None, nb, 1, blk), lambda h, i: (h, 0, 0, 0))],
        out_specs=[tile, colv],
        out_shape=[jax.ShapeDtypeStruct((t, D), F32), jax.ShapeDtypeStruct((HEADS, t, 1), F32)], name=name,
        compiler_params=_params("parallel", "parallel"))(qn, kn, proj, c_col, c_row)


def _fox_dq(name, qn, kn, proj, c_col, c_row, o, lse, do):
    t = qn.shape[0]
    blk, nb = _fox_blocks(t)
    voff = 2 * HEADS

    def body(q_ref, k_ref, v_ref, cc_ref, cr_ref, o_ref, lse_ref, do_ref, dq_ref, dcc_ref, dl_ref):
        i = pl.program_id(1)
        q, cq, do_b, lse_b = q_ref[...], cc_ref[...], do_ref[...], lse_ref[...]
        delta = jnp.sum(do_b * o_ref[...], axis=1, keepdims=True)

        def step(j, carry):
            dq, dcc = carry
            rows = pl.ds(pl.multiple_of(j * blk, blk), blk)
            k = k_ref[rows, :]
            p = jnp.exp(_fox_logits(q, k, cq, cr_ref[j], i * blk, j * blk) - lse_b)
            ds = p * (_raw_nt(do_b, v_ref[rows, :]) - delta)
            return dq + _raw_nn(ds, k), dcc + jnp.sum(ds, axis=1, keepdims=True)

        dq, dcc = lax.fori_loop(0, i + 1, step, (jnp.zeros((blk, DH), F32), jnp.zeros((blk, 1), F32)))
        dq_ref[...] = dq * (DH ** -0.5)
        dcc_ref[...] = dcc
        dl_ref[...] = delta

    tile = pl.BlockSpec((blk, DH), lambda h, i: (i, h))
    colv = pl.BlockSpec((None, blk, 1), lambda h, i: (h, i, 0))
    vec = jax.ShapeDtypeStruct((HEADS, t, 1), F32)
    return pl.pallas_call(
        body, grid=(HEADS, nb),
        in_specs=[tile, pl.BlockSpec((t, DH), lambda h, i: (0, h)), pl.BlockSpec((t, DH), lambda h, i: (0, voff + h)),
                  colv, pl.BlockSpec((None, nb, 1, blk), lambda h, i: (h, 0, 0, 0)), tile, colv, tile],
        out_specs=[tile, colv, colv],
        out_shape=[jax.ShapeDtypeStruct((t, D), F32), vec, vec], name=name,
        compiler_params=_params("parallel", "parallel"))(qn, kn, proj, c_col, c_row, o, lse, do)


def _fox_dkv(name, qn, kn, proj, c_col, c_row, lse, delta, do):
    t = qn.shape[0]
    blk, nb = _fox_blocks(t)
    voff = 2 * HEADS

    def body(q_ref, k_ref, v_ref, cc_ref, cr_ref, lse_ref, dl_ref, do_ref, dk_ref, dv_ref, dcr_ref):
        j = pl.program_id(1)
        k, v, ck = k_ref[...], v_ref[...], cr_ref[...]

        def step(i, carry):
            dk, dv, dcr = carry
            rows = pl.ds(pl.multiple_of(i * blk, blk), blk)
            q, do_b = q_ref[rows, :], do_ref[rows, :]
            p = jnp.exp(_fox_logits(q, k, cc_ref[rows, :], ck, i * blk, j * blk) - lse_ref[rows, :])
            ds = p * (_raw_nt(do_b, v) - dl_ref[rows, :])
            return dk + _raw_tn(ds, q), dv + _raw_tn(p, do_b), dcr - jnp.sum(ds, axis=0, keepdims=True)

        init = (jnp.zeros((blk, DH), F32), jnp.zeros((blk, DH), F32), jnp.zeros((1, blk), F32))
        dk, dv, dcr = lax.fori_loop(j, nb, step, init)
        dk_ref[...] = dk * (DH ** -0.5)
        dv_ref[...] = dv
        dcr_ref[...] = dcr

    full = pl.BlockSpec((t, DH), lambda h, j: (0, h))
    colf = pl.BlockSpec((None, t, 1), lambda h, j: (h, 0, 0))
    tile = pl.BlockSpec((blk, DH), lambda h, j: (j, h))
    rowv = pl.BlockSpec((None, None, 1, blk), lambda h, j: (h, j, 0, 0))
    big = jax.ShapeDtypeStruct((t, D), F32)
    return pl.pallas_call(
        body, grid=(HEADS, nb),
        in_specs=[full, tile, pl.BlockSpec((blk, DH), lambda h, j: (j, voff + h)), colf, rowv, colf, colf, full],
        out_specs=[tile, tile, rowv],
        out_shape=[big, big, jax.ShapeDtypeStruct((HEADS, nb, 1, blk), F32)], name=name,
        compiler_params=_params("parallel", "parallel"))(qn, kn, proj, c_col, c_row, lse, delta, do)


def _loss_head(y, target):
    t = y.shape[0]
    tm = min(t, 512)

    def body(y_ref, t_ref, dy_ref, sum_ref):
        err = y_ref[...] - t_ref[...]
        dy_ref[...] = err * (1.0 / D)
        part = jnp.sum(jnp.sum(err * err, axis=1, keepdims=True), axis=0, keepdims=True)

        @pl.when(pl.program_id(0) == 0)
        def _():
            sum_ref[...] = jnp.zeros_like(sum_ref)

        sum_ref[...] += jnp.broadcast_to(part, sum_ref.shape)

    return pl.pallas_call(
        body, grid=(t // tm,), in_specs=[_rows(tm, D), _rows(tm, D)], out_specs=[_rows(tm, D), _const((1, LANE))],
        out_shape=[jax.ShapeDtypeStruct((t, D), F32), jax.ShapeDtypeStruct((1, LANE), F32)], name="loss_head",
        compiler_params=_params("arbitrary"))(y, target)


def _adam(name, parts, w, m, v):
    r, c = w.shape
    tr = r
    for cand in (512, 256, 128, 64, 32, 16):
        if r > cand and r % cand == 0:
            tr = cand
            break

    def body(p_ref, w_ref, m_ref, v_ref, g_ref, d_ref, nm_ref, nv_ref):
        g = p_ref[0].astype(F32)
        for s in range(1, N_DEV):
            g = g + p_ref[s].astype(F32)
        nm = ADAM_B1 * m_ref[...] + (1.0 - ADAM_B1) * g
        nv = ADAM_B2 * v_ref[...] + (1.0 - ADAM_B2) * (g * g)
        m_hat = nm / (1.0 - ADAM_B1 ** ADAM_STEP)
        v_hat = nv / (1.0 - ADAM_B2 ** ADAM_STEP)
        g_ref[...] = g
        d_ref[...] = -ADAM_LR * (m_hat / (jnp.sqrt(v_hat) + ADAM_EPS) + ADAM_WD * w_ref[...])
        nm_ref[...] = nm
        nv_ref[...] = nv

    blk = pl.BlockSpec((tr, c), lambda i: (i, 0))
    out = jax.ShapeDtypeStruct((r, c), F32)
    return pl.pallas_call(
        body, grid=(r // tr,), in_specs=[pl.BlockSpec((N_DEV, tr, c), lambda i: (0, i, 0)), blk, blk, blk],
        out_specs=[blk] * 4, out_shape=[out] * 4, name=name, compiler_params=_params("parallel"))(parts, w, m, v)


_HBM = pl.BlockSpec(memory_space=pltpu.HBM)
_SEM = pl.BlockSpec(memory_space=pltpu.SEMAPHORE)
_EFFECT = pltpu.SideEffectType.DATAFLOW_SIDE_EFFECTING


def _each_peer(x, y, c):
    flip = lambda v, bit: 1 - v if bit else v
    for p in range(1, N_DEV):
        px, py, pc = flip(x, p & 4), flip(y, p & 2), flip(c, p & 1)
        yield p, (px, py, pc), 4 * px + 2 * py + pc


def _sem(a, p):
    return a * (N_DEV - 1) + p - 1


def _exchange_start(name, arrays, scatter):
    n = len(arrays)
    lands = [lax.empty((N_DEV,) + (a.shape[1:] if sc else a.shape), a.dtype) for a, sc in zip(arrays, scatter)]

    def body(*refs):
        in_refs, land_refs = refs[:n], refs[n:2 * n]
        send_sems, recv_sems, token = refs[2 * n], refs[2 * n + 1], refs[-1]
        x, y, c = lax.axis_index("x"), lax.axis_index("y"), lax.axis_index("c")
        me = 4 * x + 2 * y + c
        for p, coords, peer in _each_peer(x, y, c):
            for a in range(n):
                pltpu.make_async_remote_copy(
                    src_ref=in_refs[a].at[peer] if scatter[a] else in_refs[a], dst_ref=land_refs[a].at[me],
                    send_sem=send_sems.at[_sem(a, p)], recv_sem=recv_sems.at[_sem(a, p)], device_id=coords,
                    device_id_type=pl.DeviceIdType.MESH).start()
        token[...] = jnp.zeros_like(token)

    sems = pltpu.SemaphoreType.DMA((n * (N_DEV - 1),))
    hbm = lambda a: pltpu.HBM(a.shape, a.dtype)
    out = pl.pallas_call(
        body, name=name,
        out_shape=(sems, sems, *[hbm(a) for a in arrays], *[hbm(l) for l in lands],
                   jax.ShapeDtypeStruct((8, LANE), F32)),
        in_specs=[_HBM] * (2 * n), out_specs=(_SEM, _SEM, *[_HBM] * (2 * n), pl.BlockSpec(memory_space=pltpu.VMEM)),
        input_output_aliases={i: 2 + i for i in range(2 * n)},
        compiler_params=pltpu.CompilerParams(has_side_effects=_EFFECT),
    )(*[pltpu.with_memory_space_constraint(a, pltpu.HBM) for a in arrays],
      *[pltpu.with_memory_space_constraint(l, pltpu.HBM) for l in lands])
    return (out[0], out[1], list(out[2:2 + n]), list(out[2 + n:2 + 2 * n]), scatter), out[-1]


def _exchange_wait(name, started, after, me):
    send_sems, recv_sems, sent, lands, scatter = started
    n = len(sent)

    def body(*refs):
        in_refs, land_refs = refs[:n], refs[n:2 * n]
        send_sems, recv_sems = refs[2 * n], refs[2 * n + 1]
        x, y, c = lax.axis_index("x"), lax.axis_index("y"), lax.axis_index("c")
        for p, coords, peer in _each_peer(x, y, c):
            for a in range(n):
                cp = pltpu.make_async_remote_copy(
                    src_ref=in_refs[a].at[peer] if scatter[a] else in_refs[a], dst_ref=land_refs[a].at[peer],
                    send_sem=send_sems.at[_sem(a, p)], recv_sem=recv_sems.at[_sem(a, p)], device_id=coords,
                    device_id_type=pl.DeviceIdType.MESH)
                cp.wait_send()
                cp.wait_recv()

    hbm = lambda a: pltpu.HBM(a.shape, a.dtype)
    out = pl.pallas_call(
        body, name=name, out_shape=(*[hbm(a) for a in sent], *[hbm(l) for l in lands]),
        in_specs=[_HBM] * (2 * n) + [_SEM, _SEM, pl.BlockSpec(memory_space=pl.ANY)], out_specs=[_HBM] * (2 * n),
        input_output_aliases={i: i for i in range(2 * n)},
        compiler_params=pltpu.CompilerParams(has_side_effects=_EFFECT),
    )(*sent, *lands, send_sems, recv_sems, after)
    got = []
    for a in range(n):
        own = lax.dynamic_index_in_dim(out[a], me, 0, keepdims=True) if scatter[a] else out[a][None]
        slot = lax.broadcasted_iota(jnp.int32, (N_DEV,) + (1,) * (own.ndim - 1), 0)
        got.append(jnp.where(slot == me, own, out[n + a]))
    return got


def _rms_fwd(name, x, g):
    t = x.shape[0]
    tm = min(t, 512)
    return _seg_fwd(name, _f_rms, (t // tm,), [x, g], [_rows(tm, D), _const((1, D))],
                    [jax.ShapeDtypeStruct((t, D), BF16)], [_rows(tm, D)])[0]


def _rms_bwd(name, x, g, dh, dres):
    t = x.shape[0]
    tm = min(t, 512)
    return _seg_bwd(name, _f_rms_res, (t // tm,), [x, g], [_rows(tm, D), _const((1, D))],
                    [dh, dres], [_rows(tm, D), _rows(tm, D)], [F32, "acc"])


def _ffn_specs(t):
    tc = 256
    nf = D_FF // tc
    return nf, [_cols(t, tc), _cols(t, tc, nf), _cols(FFN_K, tc), _cols(FFN_K, tc, nf)], [_cols(t, tc)]


def _ffn_fwd(tag, x, p):
    t = x.shape[0]
    nf, in_specs, out_specs = _ffn_specs(t)
    h = _rms_fwd(tag + "_rms", x, p["g"])
    u0 = _mm(tag + "_up", h, p["w_up"])
    act = _seg_fwd(tag + "_mid", _f_ffn_mid, (nf,), [u0, u0, p["w_dw"], p["w_dw"]], in_specs,
                   [jax.ShapeDtypeStruct((t, D_FF), BF16)], out_specs)[0]
    out = _mm(tag + "_down", act, p["w_down"], res=x)
    return out, (x, h, u0, act)


def _ffn_bwd(tag, saved, p, dout):
    x, h, u0, act = saved
    t = x.shape[0]
    nf, in_specs, out_specs = _ffn_specs(t)
    g = {"w_down": _mm(tag + "_dwdown", act, dout, ta=True, out_dtype=BF16)}
    dact = _mm(tag + "_dact", dout, p["w_down"], tb=True, out_dtype=BF16)
    dug, duu, dwg, dwu = _seg_bwd(tag + "_dmid", _f_ffn_mid, (nf, 1), [u0, u0, p["w_dw"], p["w_dw"]],
                                  _grid2(in_specs), [dact], _grid2(out_specs), [BF16, BF16, "acc", "acc"])
    du0 = jnp.concatenate([dug[:, :D_FF], duu[:, D_FF:]], axis=1)
    g["w_dw"] = jnp.concatenate([dwg[:, :D_FF], dwu[:, D_FF:]], axis=1)
    g["w_up"] = _mm(tag + "_dwup", h, du0, ta=True, out_dtype=BF16)
    dh = _mm(tag + "_dh", du0, p["w_up"], tb=True, out_dtype=BF16)
    dx, g["g"] = _rms_bwd(tag + "_drms", x, p["g"], dh, dout)
    return dx, g


def _conf_specs(t):
    tc, tm = 128, min(t, 512)
    nc = D // tc
    vec = _const((1, D))
    glu_in = [_cols(t, tc), _cols(t, tc, nc), _cols(1, tc), _cols(1, tc, nc), _cols(CONF_K, tc)]
    return tm, nc, glu_in, [_cols(t, tc)], [_rows(tm, D), vec, vec, vec]


def _conf_fwd(tag, x, p):
    t = x.shape[0]
    tm, nc, glu_in, glu_out, ln_in = _conf_specs(t)
    h = _rms_fwd(tag + "_rms", x, p["g"])
    u = _mm(tag + "_in", h, p["w_in"])
    cv = _seg_fwd(tag + "_gluconv", _f_conf_glu_conv, (nc,), [u, u, p["b_in"], p["b_in"], p["w_dw"]], glu_in,
                  [jax.ShapeDtypeStruct((t, D), F32)], glu_out)[0]
    act = _seg_fwd(tag + "_lnsilu", _f_conf_ln_silu, (t // tm,), [cv, p["b_dw"], p["ln_g"], p["ln_b"]], ln_in,
                   [jax.ShapeDtypeStruct((t, D), BF16)], [_rows(tm, D)])[0]
    out = _mm(tag + "_out", act, p["w_out"], res=x)
    return out, (x, h, u, cv, act)


def _conf_bwd(tag, saved, p, dout):
    x, h, u, cv, act = saved
    t = x.shape[0]
    tm, nc, glu_in, glu_out, ln_in = _conf_specs(t)
    g = {"w_out": _mm(tag + "_dwout", act, dout, ta=True, out_dtype=BF16)}
    dact = _mm(tag + "_dact", dout, p["w_out"], tb=True, out_dtype=BF16)
    dcv, g["b_dw"], g["ln_g"], g["ln_b"] = _seg_bwd(
        tag + "_dlnsilu", _f_conf_ln_silu, (t // tm,), [cv, p["b_dw"], p["ln_g"], p["ln_b"]], ln_in, [dact],
        [_rows(tm, D)], [F32, "acc", "acc", "acc"])
    duv, dug, dbv, dbg, g["w_dw"] = _seg_bwd(
        tag + "_dgluconv", _f_conf_glu_conv, (nc, 1), [u, u, p["b_in"], p["b_in"], p["w_dw"]], _grid2(glu_in), [dcv],
        _grid2(glu_out), [BF16, BF16, "acc", "acc", "acc"])
    du = jnp.concatenate([duv[:, :D], dug[:, D:]], axis=1)
    g["b_in"] = jnp.concatenate([dbv[:, :D], dbg[:, D:]], axis=1)
    g["w_in"] = _mm(tag + "_dwin", h, du, ta=True, out_dtype=BF16)
    dh = _mm(tag + "_dh", du, p["w_in"], tb=True, out_dtype=BF16)
    dx, g["g"] = _rms_bwd(tag + "_drms", x, p["g"], dh, dout)
    return dx, g


def _gdn_specs(t):
    tc, tm, rows = 256, min(t, 256), min(t, 512)
    nq = 3 * D // tc
    conv = ([_cols(t, tc), _cols(GDN_K, tc)], [_cols(t, tc)])
    gate_in = [_rows(tm, D, 0), _rows(tm, D, 1), _rows(tm, LANE, 4 * D // LANE), _const((1, LANE)), _const((1, LANE))]
    gate_out = [_rows(tm, D)] * 4
    head = pl.BlockSpec((rows, DH), lambda h, i: (i, h))
    headv = pl.BlockSpec((rows, DH), lambda h, i: (i, 2 * HEADS + h))
    nch = rows // GDN_CHUNK
    pre_in = [head, head, headv, head, head]
    pre_out = [head, head, head, pl.BlockSpec((None, nch, GDN_CHUNK, GDN_CHUNK), lambda h, i: (h, i, 0, 0)), head,
               pl.BlockSpec((None, nch, 1, DH), lambda h, i: (h, i, 0, 0))]
    post_in = [_rows(tm, D), _rows(tm, D, 3), _const((1, DH))]
    return tm, rows, nq, conv, gate_in, gate_out, pre_in, pre_out, post_in


def _gdn_fwd(tag, x, p):
    t = x.shape[0]
    tm, rows, nq, conv, gate_in, gate_out, pre_in, pre_out, post_in = _gdn_specs(t)
    n = t // GDN_CHUNK
    big = jax.ShapeDtypeStruct((t, D), F32)
    h = _rms_fwd(tag + "_rms", x, p["g"])
    proj = _mm(tag + "_in", h, p["w_in"])
    qkv = _seg_fwd(tag + "_conv", _f_conv_silu, (nq,), [proj, p["conv_w"]], conv[0],
                   [jax.ShapeDtypeStruct((t, 3 * D), F32)], conv[1])[0]
    gate_ins = [qkv, qkv, proj, p["a_log"], p["dt_bias"]]
    qn, kn, gb, bb = _seg_fwd(tag + "_gates", _f_gdn_gates, (t // tm,), gate_ins, gate_in, [big] * 4, gate_out)
    pre_ins = [qn, kn, qkv, gb, bb]
    pre_shapes = [big, big, big, jax.ShapeDtypeStruct((HEADS, n, GDN_CHUNK, GDN_CHUNK), F32), big,
                  jax.ShapeDtypeStruct((HEADS, n, 1, DH), F32)]
    pre = _seg_fwd(tag + "_prescan", _f_gdn_prescan, (HEADS, t // rows), pre_ins, pre_in, pre_shapes, pre_out)
    o, states = _gdn_scan_fwd(tag + "_scan", *pre)
    post_ins = [o, proj, p["o_g"]]
    act = _seg_fwd(tag + "_post", _f_gdn_post, (t // tm,), post_ins, post_in, [jax.ShapeDtypeStruct((t, D), BF16)],
                   [_rows(tm, D)])[0]
    out = _mm(tag + "_out", act, p["w_out"], res=x)
    return out, (x, h, proj, gate_ins, pre_ins, pre, states, post_ins, act)


def _gdn_bwd(tag, saved, p, dout):
    x, h, proj, gate_ins, pre_ins, pre, states, post_ins, act = saved
    t = x.shape[0]
    tm, rows, nq, conv, gate_in, gate_out, pre_in, pre_out, post_in = _gdn_specs(t)
    g = {"w_out": _mm(tag + "_dwout", act, dout, ta=True, out_dtype=BF16)}
    dact = _mm(tag + "_dact", dout, p["w_out"], tb=True, out_dtype=BF16)
    do, dproj_z, g["o_g"] = _seg_bwd(tag + "_dpost", _f_gdn_post, (t // tm,), post_ins, post_in, [dact],
                                     [_rows(tm, D)], [F32, BF16, "acc"])
    dpre = _gdn_scan_bwd(tag + "_dscan", *pre, states, do)
    dqn, dkn, dqkv_v, dgb, dbb = _seg_bwd(tag + "_dprescan", _f_gdn_prescan, (HEADS, t // rows), pre_ins, pre_in,
                                          dpre, pre_out, [F32] * 5)
    dqkv_q, dqkv_k, dproj_ab, g["a_log"], g["dt_bias"] = _seg_bwd(
        tag + "_dgates", _f_gdn_gates, (t // tm,), gate_ins, gate_in, [dqn, dkn, dgb, dbb], gate_out,
        [F32, F32, BF16, "acc", "acc"])
    dqkv = jnp.concatenate([dqkv_q[:, :D], dqkv_k[:, D:2 * D], dqkv_v[:, 2 * D:]], axis=1)
    dqkv_pre, g["conv_w"] = _seg_bwd(tag + "_dconv", _f_conv_silu, (nq, 1), [proj, p["conv_w"]], _grid2(conv[0]),
                                     [dqkv], _grid2(conv[1]), [BF16, "acc"])
    dproj = jnp.concatenate([dqkv_pre[:, :3 * D], dproj_z[:, 3 * D:4 * D], dproj_ab[:, 4 * D:]], axis=1)
    g["w_in"] = _mm(tag + "_dwin", h, dproj, ta=True, out_dtype=BF16)
    dh = _mm(tag + "_dh", dproj, p["w_in"], tb=True, out_dtype=BF16)
    dx, g["g"] = _rms_bwd(tag + "_drms", x, p["g"], dh, dout)
    return dx, g


def _fox_specs(t):
    tm = min(t, 512)
    vec = _const((1, LANE))
    pre_in = [_rows(tm, D, 0), _rows(tm, D, 1), _rows(tm, LANE, 3 * D // LANE), vec, vec, vec]
    pre_out = [_rows(tm, D), _rows(tm, D), _rows(tm, LANE)]
    return tm, pre_in, pre_out


def _per_head(c):
    return jnp.transpose(c[:, :HEADS])


def _per_lane(ch):
    return jnp.pad(jnp.transpose(ch), ((0, 0), (0, LANE - HEADS)))


def _fox_fwd_layer(tag, x, p):
    t = x.shape[0]
    tm, pre_in, pre_out = _fox_specs(t)
    blk, nb = _fox_blocks(t)
    h = _rms_fwd(tag + "_rms", x, p["g"])
    proj = _mm(tag + "_in", h, p["w_in"])
    pre_ins = [proj, proj, proj, p["q_g"], p["k_g"], p["b_f"]]
    qn, kn, lf = _seg_fwd(tag + "_pre", _f_fox_pre, (t // tm,), pre_ins, pre_in,
                          [jax.ShapeDtypeStruct((t, D), BF16)] * 2 + [jax.ShapeDtypeStruct((t, LANE), F32)], pre_out)
    ch = _per_head(_cumsum_rows(tag + "_cumsum", [lf], False))
    c_col, c_row = ch.reshape(HEADS, t, 1), ch.reshape(HEADS, nb, 1, blk)
    o, lse = _fox_fwd(tag + "_attn", qn, kn, proj, c_col, c_row)
    out = _mm(tag + "_out", o, p["w_out"], res=x)
    return out, (x, h, proj, pre_ins, qn, kn, c_col, c_row, o, lse)


def _fox_bwd_layer(tag, saved, p, dout):
    x, h, proj, pre_ins, qn, kn, c_col, c_row, o, lse = saved
    t = x.shape[0]
    tm, pre_in, pre_out = _fox_specs(t)
    g = {"w_out": _mm(tag + "_dwout", o, dout, ta=True, out_dtype=BF16)}
    do = _mm(tag + "_do", dout, p["w_out"], tb=True)
    dqn, dc_col, delta = _fox_dq(tag + "_dq", qn, kn, proj, c_col, c_row, o, lse, do)
    dkn, dv, dc_row = _fox_dkv(tag + "_dkv", qn, kn, proj, c_col, c_row, lse, delta, do)
    dlf = _cumsum_rows(tag + "_dcumsum", [_per_lane(dc_col.reshape(HEADS, t)), _per_lane(dc_row.reshape(HEADS, t))],
                       True)
    dq, dk, df, g["q_g"], g["k_g"], g["b_f"] = _seg_bwd(
        tag + "_dpre", _f_fox_pre, (t // tm,), pre_ins, pre_in, [dqn, dkn, dlf], pre_out,
        [BF16, BF16, BF16, "acc", "acc", "acc"])
    dproj = jnp.concatenate([dq[:, :D], dk[:, D:2 * D], dv.astype(BF16), df[:, 3 * D:]], axis=1)
    g["w_in"] = _mm(tag + "_dwin", h, dproj, ta=True, out_dtype=BF16)
    dh = _mm(tag + "_dh", dproj, p["w_in"], tb=True, out_dtype=BF16)
    dx, g["g"] = _rms_bwd(tag + "_drms", x, p["g"], dh, dout)
    return dx, g


_MIXERS = ((_conf_fwd, _conf_bwd), (_gdn_fwd, _gdn_bwd), (_fox_fwd_layer, _fox_bwd_layer))


def _local_step(x, target, layer_params, on_grads):
    saved, layers = [], []
    for i in range(DEPTH):
        mp, fp = layer_params(i, x)
        layers.append((mp, fp))
        x, sm = _MIXERS[i % N_MIXERS][0](f"l{i}_mix", x, mp)
        x, sf = _ffn_fwd(f"l{i}_ffn", x, fp)
        saved.append((sm, sf))
    dx, sq = _loss_head(x, target)
    for i in reversed(range(DEPTH)):
        mp, fp = layers[i]
        dx, gf = _ffn_bwd(f"l{i}_ffn", saved[i][1], fp, dx)
        dx, gm = _MIXERS[i % N_MIXERS][1](f"l{i}_mix", saved[i][0], mp, dx)
        dx = on_grads(i, gm, gf, dx)
    return sq, dx


def _unshard(name, g):
    axis = g.ndim - 2 if name in ROW_SHARDED else g.ndim - 1
    m = jnp.moveaxis(g, 0, axis - 1)
    return m.reshape(m.shape[:axis - 1] + (N_DEV * m.shape[axis],) + m.shape[axis + 1:])


def _reshard(name, full):
    axis = full.ndim - 2 if name in ROW_SHARDED else full.ndim - 1
    s = full.shape
    return jnp.moveaxis(full.reshape(s[:axis] + (N_DEV, s[axis] // N_DEV) + s[axis + 1:]), axis, 0)


def _pad_cols(a, width):
    return jnp.pad(a, [(0, 0)] * (a.ndim - 1) + [(0, width - a.shape[-1])])


def _lane_vec(v):
    return _pad_cols(v.reshape(1, -1), LANE)


REP_ROWS = 16


def _pack_rep(r):
    small = [_pad_cols(r[n].reshape(1, -1), LANE) for n in REPLICATED[2:]]
    row = jnp.concatenate(small + [jnp.zeros((1, D - LANE * len(small)), F32)], axis=1)
    pad = jnp.zeros((REP_ROWS - 2 * DEPTH - 1, D), F32)
    return jnp.concatenate([r['mix_norm_g'].reshape(DEPTH, D), r['ffn_norm_g'].reshape(DEPTH, D), row, pad], axis=0)


def _unpack_rep(a, shapes):
    out = {'mix_norm_g': a[:DEPTH], 'ffn_norm_g': a[DEPTH:2 * DEPTH]}
    for i, n in enumerate(REPLICATED[2:]):
        out[n] = a[2 * DEPTH:2 * DEPTH + 1, i * LANE:i * LANE + shapes[n][1]].reshape(shapes[n])
    return out


def _view2d(a):
    return a.reshape(-1, a.shape[-1])


MIXER_SHARDED = (
    (('conv_w_in', 'w_in'), ('conv_w_out', 'w_out'), ('conv_b_in', 'b_in'), ('conv_w_dw', 'w_dw'),
     ('conv_b_dw', 'b_dw'), ('conv_ln_g', 'ln_g'), ('conv_ln_b', 'ln_b')),
    (('gdn_w_in', 'w_in'), ('gdn_w_out', 'w_out'), ('gdn_conv_w', 'conv_w')),
    (('fox_w_in', 'w_in'), ('fox_w_out', 'w_out')),
)
FFN_SHARDED = (('ffn_w_up', 'w_up'), ('ffn_w_down', 'w_down'), ('ffn_w_dw', 'w_dw'))
MIXER_REPLICATED = (
    (),
    (('gdn_a_log', 'a_log'), ('gdn_dt_bias', 'dt_bias'), ('gdn_o_norm_g', 'o_g')),
    (('fox_b_f', 'b_f'), ('fox_q_norm_g', 'q_g'), ('fox_k_norm_g', 'k_g')),
)
ROW_VECTORS = ('conv_b_in', 'conv_b_dw', 'conv_ln_g', 'conv_ln_b')
PADDED_COLS = {'gdn_w_in': (4 * D + 2 * HEADS, GDN_PAD), 'fox_w_in': (3 * D + HEADS, FOX_PAD)}
PACK_GROUP = 16 * D


def _layer_entries(i):
    ent = [(n, i // N_MIXERS, 0, k) for n, k in MIXER_SHARDED[i % N_MIXERS]] + [(n, i, 1, k) for n, k in FFN_SHARDED]
    return [e for e in ent if e[0] in MATRICES], [e for e in ent if e[0] not in MATRICES]


def _to_param(name, whole):
    if name in ROW_VECTORS:
        return whole[None]
    if name in PADDED_COLS:
        return _pad_cols(whole, PADDED_COLS[name][1])
    return whole


def _from_grad(name, g):
    if name in ROW_VECTORS:
        return g[0]
    if name in PADDED_COLS:
        return g[:, :PADDED_COLS[name][0]]
    return g


def _pack_rows(parts, lead):
    out = []
    for a in parts:
        flat = a.reshape(a.shape[:lead] + (-1,))
        size = flat.shape[-1]
        padded = -(-size // PACK_GROUP) * PACK_GROUP
        flat = jnp.pad(flat, [(0, 0)] * lead + [(0, padded - size)])
        out.append(flat.reshape(a.shape[:lead] + (padded // D, D)))
    return jnp.concatenate(out, axis=lead)


def _unpack_rows(packed, shapes, lead):
    out, row = [], 0
    head = packed.shape[:lead]
    for s in shapes:
        size = 1
        for d in s:
            size *= d
        rows = -(-size // PACK_GROUP) * (PACK_GROUP // D)
        part = lax.slice_in_dim(packed, row, row + rows, axis=lead)
        out.append(part.reshape(head + (rows * D,))[..., :size].reshape(head + tuple(s)))
        row += rows
    return out


def _layer_dicts(i, whole, rep):
    kind, j = i % N_MIXERS, i // N_MIXERS
    mp = {k: _to_param(n, whole[n]) for n, k in MIXER_SHARDED[kind]}
    for n, k in MIXER_REPLICATED[kind]:
        mp[k] = rep[n][j][None] if rep[n].shape[-1] == DH else _lane_vec(rep[n][j])
    mp["g"] = rep['mix_norm_g'][i][None]
    fp = {k: _to_param(n, whole[n]) for n, k in FFN_SHARDED}
    fp["g"] = rep['ffn_norm_g'][i][None]
    return mp, fp


def _layer_grads(i, gm, gf):
    out = {n: _from_grad(n, gm[k]) for n, k in MIXER_SHARDED[i % N_MIXERS]}
    out.update({n: _from_grad(n, gf[k]) for n, k in FFN_SHARDED})
    return out


def _replicated_grads(grads):
    rep = {'mix_norm_g': jnp.concatenate([g[0]["g"] for g in grads]),
           'ffn_norm_g': jnp.concatenate([g[1]["g"] for g in grads])}
    for kind in range(N_MIXERS):
        for n, k in MIXER_REPLICATED[kind]:
            rep[n] = jnp.stack([grads[i][0][k] for i in range(kind, DEPTH, N_MIXERS)])
    return rep


def kernel(x, mix_norm_g, ffn_norm_g, conv_w_in, conv_b_in, conv_w_dw, conv_b_dw, conv_ln_g, conv_ln_b, conv_w_out, gdn_w_in, gdn_conv_w, gdn_a_log, gdn_dt_bias, gdn_o_norm_g, gdn_w_out, fox_w_in, fox_b_f, fox_q_norm_g, fox_k_norm_g, fox_w_out, ffn_w_up, ffn_w_dw, ffn_w_down, loss_target, m_mix_norm_g, m_ffn_norm_g, m_conv_w_in, m_conv_b_in, m_conv_w_dw, m_conv_b_dw, m_conv_ln_g, m_conv_ln_b, m_conv_w_out, m_gdn_w_in, m_gdn_conv_w, m_gdn_a_log, m_gdn_dt_bias, m_gdn_o_norm_g, m_gdn_w_out, m_fox_w_in, m_fox_b_f, m_fox_q_norm_g, m_fox_k_norm_g, m_fox_w_out, m_ffn_w_up, m_ffn_w_dw, m_ffn_w_down, v_mix_norm_g, v_ffn_norm_g, v_conv_w_in, v_conv_b_in, v_conv_w_dw, v_conv_b_dw, v_conv_ln_g, v_conv_ln_b, v_conv_w_out, v_gdn_w_in, v_gdn_conv_w, v_gdn_a_log, v_gdn_dt_bias, v_gdn_o_norm_g, v_gdn_w_out, v_fox_w_in, v_fox_b_f, v_fox_q_norm_g, v_fox_k_norm_g, v_fox_w_out, v_ffn_w_up, v_ffn_w_dw, v_ffn_w_down):
    given = dict(zip(
        WEIGHTS + ["m_" + n for n in WEIGHTS] + ["v_" + n for n in WEIGHTS],
        (mix_norm_g, ffn_norm_g, conv_w_in, conv_b_in, conv_w_dw, conv_b_dw, conv_ln_g, conv_ln_b, conv_w_out, gdn_w_in, gdn_conv_w, gdn_a_log, gdn_dt_bias, gdn_o_norm_g, gdn_w_out, fox_w_in, fox_b_f, fox_q_norm_g, fox_k_norm_g, fox_w_out, ffn_w_up, ffn_w_dw, ffn_w_down,
         m_mix_norm_g, m_ffn_norm_g, m_conv_w_in, m_conv_b_in, m_conv_w_dw, m_conv_b_dw, m_conv_ln_g, m_conv_ln_b, m_conv_w_out, m_gdn_w_in, m_gdn_conv_w, m_gdn_a_log, m_gdn_dt_bias, m_gdn_o_norm_g, m_gdn_w_out, m_fox_w_in, m_fox_b_f, m_fox_q_norm_g, m_fox_k_norm_g, m_fox_w_out, m_ffn_w_up, m_ffn_w_dw, m_ffn_w_down,
         v_mix_norm_g, v_ffn_norm_g, v_conv_w_in, v_conv_b_in, v_conv_w_dw, v_conv_b_dw, v_conv_ln_g, v_conv_ln_b, v_conv_w_out, v_gdn_w_in, v_gdn_conv_w, v_gdn_a_log, v_gdn_dt_bias, v_gdn_o_norm_g, v_gdn_w_out, v_fox_w_in, v_fox_b_f, v_fox_q_norm_g, v_fox_k_norm_g, v_fox_w_out, v_ffn_w_up, v_ffn_w_dw, v_ffn_w_down)))

    me = 4 * lax.axis_index("x") + 2 * lax.axis_index("y") + lax.axis_index("c")

    gathers, token = [], jnp.zeros((1, 1), F32)
    for i in range(DEPTH):
        mats, smalls = _layer_entries(i)
        sent = [given[n][j].astype(BF16) for n, j, _, _ in mats] + [_pack_rows([given[n][j] for n, j, _, _ in smalls], 0)]
        started, tok = _exchange_start(f"gather{i}_start", sent, [False] * len(sent))
        gathers.append(started)
        token = token + tok[0:1, 0:1]

    def layer_params(i, x_in):
        mats, smalls = _layer_entries(i)
        got = _exchange_wait(f"gather{i}_wait", gathers[i], token if i == 0 else x_in, me)
        whole = {n: _unshard(n, g) for (n, _, _, _), g in zip(mats, got)}
        shapes = [given[n].shape[1:] for n, _, _, _ in smalls]
        for (n, _, _, _), g in zip(smalls, _unpack_rows(got[-1], shapes, 1)):
            whole[n] = _unshard(n, g)
        return _layer_dicts(i, whole, given)

    grads, exchanges = [None] * DEPTH, [None] * DEPTH

    def on_grads(i, gm, gf, dx):
        grads[i] = (gm, gf)
        mats, smalls = _layer_entries(i)
        whole = _layer_grads(i, gm, gf)
        sent = [_reshard(n, whole[n]) for n, _, _, _ in mats]
        sent.append(_pack_rows([_reshard(n, whole[n]) for n, _, _, _ in smalls], 1))
        if i == 0:
            sent.append(_pack_rep(_replicated_grads(grads)))
        exchanges[i], tok = _exchange_start(f"grads{i}_start", sent, [True] * (len(mats) + 1) + [False] * (i == 0))
        return dx + tok[0:1, 0:1] if i > 0 else dx

    sq, dx = _local_step(x[0], loss_target[0], layer_params, on_grads)
    loss = (0.5 / D) * lax.psum(sq[0, 0], ("x", "y", "c"))

    pieces = {n: [] for n in SHARDED}
    for i in range(DEPTH):
        mats, smalls = _layer_entries(i)
        got = _exchange_wait(f"grads{i}_wait", exchanges[i], dx, me)
        if i == 0:
            rep_parts = got[-1]
        for (n, _, _, _), part in zip(mats, got):
            pieces[n].append(part)
        shapes = [given[n].shape[1:] for n, _, _, _ in smalls]
        for (n, _, _, _), part in zip(smalls, _unpack_rows(got[len(mats)], shapes, 1)):
            pieces[n].append(part)
    new = {}
    for n in SHARDED:
        w2 = _view2d(given[n])
        outs = _adam("adam_" + n, jnp.stack(pieces[n], axis=1).reshape((N_DEV,) + w2.shape), w2,
                     _view2d(given["m_" + n]), _view2d(given["v_" + n]))
        new[n] = [o.reshape(given[n].shape) for o in outs]
    packed = [_pack_rep({n: given[pre + n] for n in REPLICATED}) for pre in ("", "m_", "v_")]
    outs = _adam("adam_replicated", rep_parts, *packed)
    unpacked = [_unpack_rep(o, {n: given[n].shape for n in REPLICATED}) for o in outs]
    for n in REPLICATED:
        new[n] = [u[n] for u in unpacked]
    return (loss, dx[None], *[new[n][0] for n in WEIGHTS], *[new[n][1] for n in WEIGHTS],
            *[new[n][2] for n in WEIGHTS], *[new[n][3] for n in WEIGHTS])
```

```python
import functools

import jax
import jax.numpy as jnp
from jax import lax
from jax.experimental import pallas as pl
from jax.experimental.pallas import tpu as pltpu

F32 = jnp.float32
BF16 = jnp.bfloat16
HIGHEST = lax.Precision.HIGHEST

N_DEV = 8
LANE = 128
EPS = 1e-6
DEPTH = 4
N_MIXERS = 3
HEADS = 8
DH = 128
D = HEADS * DH
D_FF = 2816
CONF_K, GDN_K, FFN_K = 31, 4, 3
GDN_CHUNK = 64
GDN_PAD = 4224
FOX_PAD = 3200
FOX_V_BLOCK = (2 * D + LANE) // DH
ADAM_LR, ADAM_B1, ADAM_B2, ADAM_EPS, ADAM_WD, ADAM_STEP = 0.001, 0.9, 0.999, 1e-08, 0.01, 10
VMEM_LIMIT = 56 * 1024 * 1024

WEIGHTS = ['mix_norm_g', 'ffn_norm_g', 'conv_w_in', 'conv_b_in', 'conv_w_dw', 'conv_b_dw', 'conv_ln_g', 'conv_ln_b',
           'conv_w_out', 'gdn_w_in', 'gdn_conv_w', 'gdn_a_log', 'gdn_dt_bias', 'gdn_o_norm_g', 'gdn_w_out', 'fox_w_in',
           'fox_b_f', 'fox_q_norm_g', 'fox_k_norm_g', 'fox_w_out', 'ffn_w_up', 'ffn_w_dw', 'ffn_w_down']
REPLICATED = ['mix_norm_g', 'ffn_norm_g', 'gdn_a_log', 'gdn_dt_bias', 'gdn_o_norm_g', 'fox_b_f', 'fox_q_norm_g',
              'fox_k_norm_g']
ROW_SHARDED = ['conv_w_out', 'gdn_w_out', 'fox_w_out', 'ffn_w_down']
MATRICES = ['conv_w_in', 'conv_w_out', 'gdn_w_in', 'gdn_w_out', 'fox_w_in', 'fox_w_out', 'ffn_w_up', 'ffn_w_down']
SHARDED = [n for n in WEIGHTS if n not in REPLICATED]


def _params(*sem):
    return pltpu.CompilerParams(dimension_semantics=sem, vmem_limit_bytes=VMEM_LIMIT)


def _tile(n, cap):
    if n <= cap:
        return n
    d = (cap // LANE) * LANE
    while d >= LANE:
        if n % d == 0:
            return d
        d -= LANE
    raise ValueError(f"no lane-aligned tile of {n} under {cap}")


def _raw_dot(a, b, ca, cb, hp):
    batch = ((0,), (0,)) if a.ndim == 3 else ((), ())
    dn = (((ca,), (cb,)), batch)
    if hp:
        return lax.dot_general(a.astype(F32), b.astype(F32), dn, precision=HIGHEST, preferred_element_type=F32)
    return lax.dot_general(a.astype(BF16), b.astype(BF16), dn, preferred_element_type=F32)


def _raw_nn(a, b, hp=False):
    return _raw_dot(a, b, a.ndim - 1, b.ndim - 2, hp)


def _raw_nt(a, b, hp=False):
    return _raw_dot(a, b, a.ndim - 1, b.ndim - 1, hp)


def _raw_tn(a, b, hp=False):
    return _raw_dot(a, b, a.ndim - 2, b.ndim - 2, hp)


@functools.partial(jax.custom_vjp, nondiff_argnums=(2,))
def _nn(a, b, hp):
    return _raw_nn(a, b, hp)


def _nn_fwd(a, b, hp):
    return _raw_nn(a, b, hp), (a, b)


def _nn_bwd(hp, res, g):
    a, b = res
    return _raw_nt(g, b, hp), _raw_tn(a, g, hp)


_nn.defvjp(_nn_fwd, _nn_bwd)


@functools.partial(jax.custom_vjp, nondiff_argnums=(2,))
def _nt(a, b, hp):
    return _raw_nt(a, b, hp)


def _nt_fwd(a, b, hp):
    return _raw_nt(a, b, hp), (a, b)


def _nt_bwd(hp, res, g):
    a, b = res
    return _raw_nn(g, b, hp), _raw_tn(g, a, hp)


_nt.defvjp(_nt_fwd, _nt_bwd)


def _shift_down(x, s):
    if s == 0:
        return x
    t = lax.broadcasted_iota(jnp.int32, x.shape, 0)
    return jnp.where(t >= s, pltpu.roll(x, s, axis=0), 0.0)


def _shift_up(x, s):
    if s == 0:
        return x
    n = x.shape[0]
    t = lax.broadcasted_iota(jnp.int32, x.shape, 0)
    return jnp.where(t < n - s, pltpu.roll(x, n - s, axis=0), 0.0)


def _row(w, k):
    r = lax.broadcasted_iota(jnp.int32, w.shape, 0)
    return jnp.sum(jnp.where(r == k, w, 0.0), axis=0, keepdims=True)


@jax.custom_vjp
def _dwconv(x, w):
    taps = w.shape[0]
    y = _row(w, taps - 1) * x
    for k in range(taps - 1):
        y = y + _row(w, k) * _shift_down(x, taps - 1 - k)
    return y


def _dwconv_fwd(x, w):
    return _dwconv(x, w), (x, w)


def _dwconv_bwd(res, dy):
    x, w = res
    taps = w.shape[0]
    r = lax.broadcasted_iota(jnp.int32, w.shape, 0)
    dx = _row(w, taps - 1) * dy
    dw = jnp.where(r == taps - 1, jnp.sum(dy * x, axis=0, keepdims=True), 0.0)
    for k in range(taps - 1):
        s = taps - 1 - k
        dx = dx + _row(w, k) * _shift_up(dy, s)
        dw = dw + jnp.where(r == k, jnp.sum(dy * _shift_down(x, s), axis=0, keepdims=True), 0.0)
    return dx, dw


_dwconv.defvjp(_dwconv_fwd, _dwconv_bwd)


def _sigmoid(x):
    return 1.0 / (1.0 + jnp.exp(-x))


def _silu(x):
    return x * _sigmoid(x)


def _softplus(x):
    return jnp.maximum(x, 0.0) + jnp.log(1.0 + jnp.exp(-jnp.abs(x)))


def _head_scale(x, fn):
    tm = x.shape[0]
    x3 = x.reshape(tm, HEADS, DH)
    return (x3 * fn(jnp.sum(x3 * x3, axis=-1, keepdims=True))).reshape(tm, HEADS * DH)


def _tile_lanes(g):
    return jnp.concatenate([g] * HEADS, axis=1)


def _expand_heads(v, first):
    lane = lax.broadcasted_iota(jnp.int32, (LANE, HEADS * DH), 0)
    col = lax.broadcasted_iota(jnp.int32, (LANE, HEADS * DH), 1)
    sel = (lane == col // DH + first).astype(F32)
    return _nn(v, sel, True)


def _f_rms(x, g):
    return (x * lax.rsqrt(jnp.mean(x * x, axis=-1, keepdims=True) + EPS) * g,)


def _f_rms_res(x, g):
    return (_f_rms(x, g)[0], x)


def _f_conf_glu_conv(u, b, w):
    c = u.shape[1] // 2
    return (_dwconv((u[:, :c] + b[:, :c]) * _sigmoid(u[:, c:] + b[:, c:]), w),)


def _f_conf_ln_silu(cv, b_dw, ln_g, ln_b):
    u = cv + b_dw
    xc = u - jnp.mean(u, axis=-1, keepdims=True)
    y = xc * lax.rsqrt(jnp.mean(xc * xc, axis=-1, keepdims=True) + EPS) * ln_g + ln_b
    return (_silu(y),)


def _f_conv_silu(u, w):
    return (_silu(_dwconv(u, w)),)


def _f_gdn_gates(qk, ab, a_log, dt_bias):
    qn = _head_scale(qk[:, :D], lambda ss: lax.rsqrt(ss + EPS)) * (DH ** -0.5)
    kn = _head_scale(qk[:, D:], lambda ss: lax.rsqrt(ss + EPS))
    g = -jnp.exp(a_log) * _softplus(ab + dt_bias)
    beta = _sigmoid(ab)
    return qn, kn, _expand_heads(g, 0), _expand_heads(beta, HEADS)


def _f_gdn_prescan(q, k, v, gb, bb):
    c = GDN_CHUNK
    n = q.shape[0] // c
    r3 = lambda t: t.reshape(n, c, DH)
    q3, k3, v3, g3, b3 = r3(q), r3(k), r3(v), r3(gb), r3(bb)
    ii = lax.broadcasted_iota(jnp.int32, (n, c, c), 1)
    jj = lax.broadcasted_iota(jnp.int32, (n, c, c), 2)
    lower, strict = ii >= jj, ii > jj
    gcb = _nn(lower.astype(F32), g3, True)
    gi = gcb[:, :, :c]
    gj = jnp.swapaxes(gi, 1, 2)
    decay = jnp.where(lower, jnp.exp(jnp.where(lower, gi - gj, 0.0)), 0.0)
    kb, vb = k3 * b3, v3 * b3
    a_mat = jnp.where(strict, _nt(kb, k3, False) * decay, 0.0)
    p = -a_mat
    t_mat = (ii == jj).astype(F32) + p
    for _ in range(5):
        p = _nn(p, p, True)
        t_mat = t_mat + _nn(t_mat, p, True)
    eg = jnp.exp(gcb)
    u = _nn(t_mat, vb, False)
    w = _nn(t_mat, kb * eg, False)
    qk = jnp.where(lower, _nt(q3, k3, False) * decay, 0.0)
    qg = q3 * eg
    g_last = jnp.sum(g3, axis=1, keepdims=True)
    kd = k3 * jnp.exp(g_last - gcb)
    r2 = lambda t: t.reshape(n * c, DH)
    return r2(u), r2(w), r2(qg), qk, r2(kd), jnp.exp(g_last)


def _f_gdn_post(o, z, o_g):
    on = _head_scale(o, lambda ss: lax.rsqrt(ss / DH + EPS)) * _tile_lanes(o_g)
    return (on * _silu(z),)


def _f_fox_pre(qkf, q_g, k_g, b_f):
    qn = _head_scale(qkf[:, :D], lambda ss: lax.rsqrt(ss / DH + EPS)) * _tile_lanes(q_g)
    kn = _head_scale(qkf[:, D:2 * D], lambda ss: lax.rsqrt(ss / DH + EPS)) * _tile_lanes(k_g)
    return qn, kn, -_softplus(-(qkf[:, 2 * D:] + b_f))


def _f_ffn_mid(u, w):
    c = u.shape[1] // 2
    return (_silu(_dwconv(u[:, :c], w[:, :c])) * _dwconv(u[:, c:], w[:, c:]),)


def _seg_fwd(name, f, grid, ins, in_specs, out_shapes, out_specs):
    n_in = len(ins)

    def body(*refs):
        outs = f(*[r[...].astype(F32) for r in refs[:n_in]])
        for r, o in zip(refs[n_in:], outs):
            r[...] = o.astype(r.dtype)

    return pl.pallas_call(body, grid=grid, in_specs=in_specs, out_specs=out_specs, out_shape=out_shapes, name=name,
                          compiler_params=_params(*(["parallel"] * len(grid))))(*ins)


def _seg_bwd(name, f, grid, ins, in_specs, douts, dout_specs, want, into=None):
    into = into or {}
    n_in, n_dy = len(ins), len(douts)
    diff = [i for i, w in enumerate(want) if w is not None]
    kept = [i for i in diff if i in into]
    out_shapes = [jax.ShapeDtypeStruct(ins[i].shape, into[i].dtype if i in into else F32 if want[i] == "acc" else want[i])
                  for i in diff]
    out_specs = [in_specs[i] for i in diff]
    acc_axis = len(grid) - 1

    def body(*refs):
        vals = [r[...].astype(F32) for r in refs[:n_in]]
        dys = [r[...].astype(F32) for r in refs[n_in:n_in + n_dy]]
        out_refs = refs[n_in + n_dy + len(kept):]

        def g(*dv):
            full = list(vals)
            for i, v in zip(diff, dv):
                full[i] = v
            return f(*full)

        _, vjp = jax.vjp(g, *[vals[i] for i in diff])
        grads = vjp(tuple(dys))
        first = pl.program_id(acc_axis) == 0
        for i, r, gr in zip(diff, out_refs, grads):
            if want[i] == "acc":
                @pl.when(first)
                def _(r=r, gr=gr):
                    r[...] = gr

                @pl.when(jnp.logical_not(first))
                def _(r=r, gr=gr):
                    r[...] += gr
            else:
                r[...] = gr.astype(r.dtype)

    sem = ["parallel"] * (len(grid) - 1) + ["arbitrary"]
    untouched = [pl.BlockSpec(memory_space=pl.ANY)] * len(kept)
    aliases = {n_in + n_dy + e: diff.index(i) for e, i in enumerate(kept)}
    return pl.pallas_call(body, grid=grid, in_specs=list(in_specs) + list(dout_specs) + untouched, out_specs=out_specs,
                          out_shape=out_shapes, input_output_aliases=aliases, name=name,
                          compiler_params=_params(*sem))(*ins, *douts, *[into[i] for i in kept])


def _rows(tm, width, col=0):
    return pl.BlockSpec((tm, width), lambda i, col=col: (i, col))


def _const(shape):
    return pl.BlockSpec(shape, lambda i: (0,) * len(shape))


def _cols(t, tc, off=0):
    return pl.BlockSpec((t, tc), lambda j, off=off: (0, j + off))


def _grid2(specs):
    return [pl.BlockSpec(s.block_shape, lambda j, i, f=s.index_map: f(j)) for s in specs]


def _mm(name, a, b, *, ta=False, tb=False, res=None, out_dtype=F32):
    k_dim, m = (a.shape[0], a.shape[1]) if ta else (a.shape[1], a.shape[0])
    n = b.shape[0] if tb else b.shape[1]
    tm, tn, tk = _tile(m, 1408), _tile(n, 1408), _tile(k_dim, 1408)
    nk = k_dim // tk
    grid = (m // tm, n // tn, nk)
    a_spec = pl.BlockSpec((tk, tm), lambda i, j, k: (k, i)) if ta else pl.BlockSpec((tm, tk), lambda i, j, k: (i, k))
    b_spec = pl.BlockSpec((tn, tk), lambda i, j, k: (j, k)) if tb else pl.BlockSpec((tk, tn), lambda i, j, k: (k, j))
    o_spec = pl.BlockSpec((tm, tn), lambda i, j, k: (i, j))
    dn = (((0 if ta else 1,), (1 if tb else 0,)), ((), ()))
    has_res = res is not None

    def body(*refs):
        a_ref, b_ref = refs[0], refs[1]
        res_ref = refs[2] if has_res else None
        o_ref = refs[3] if has_res else refs[2]
        p = lax.dot_general(a_ref[...].astype(BF16), b_ref[...].astype(BF16), dn, preferred_element_type=F32)

        def write(acc):
            if has_res:
                acc = acc + res_ref[...]
            o_ref[...] = acc.astype(o_ref.dtype)

        if nk == 1:
            write(p)
        else:
            acc_ref = refs[-1]
            k = pl.program_id(2)

            @pl.when(k == 0)
            def _():
                acc_ref[...] = p

            @pl.when(k > 0)
            def _():
                acc_ref[...] += p

            @pl.when(k == nk - 1)
            def _():
                write(acc_ref[...])

    ins, specs = [a, b], [a_spec, b_spec]
    if has_res:
        ins.append(res)
        specs.append(o_spec)
    scratch = [pltpu.VMEM((tm, tn), F32)] if nk > 1 else []
    return pl.pallas_call(body, grid=grid, in_specs=specs, out_specs=o_spec, scratch_shapes=scratch,
                          out_shape=jax.ShapeDtypeStruct((m, n), out_dtype), name=name,
                          compiler_params=_params("parallel", "parallel", "arbitrary"))(*ins)


def _head_specs(t):
    n = t // GDN_CHUNK
    col = pl.BlockSpec((t, DH), lambda h: (0, h))
    qk = pl.BlockSpec((None, n, GDN_CHUNK, GDN_CHUNK), lambda h: (h, 0, 0, 0))
    gl = pl.BlockSpec((None, n, 1, DH), lambda h: (h, 0, 0, 0))
    st = pl.BlockSpec((None, n, DH, DH), lambda h: (h, 0, 0, 0))
    return n, col, qk, gl, st


def _gdn_scan_fwd(name, u, w, qg, qk, kd, gl):
    t = u.shape[0]
    n, col, qk_spec, gl_spec, st_spec = _head_specs(t)
    c = GDN_CHUNK

    def body(u_ref, w_ref, qg_ref, qk_ref, kd_ref, gl_ref, o_ref, s_ref):
        def step(i, s):
            rows = pl.ds(pl.multiple_of(i * c, c), c)
            s_ref[i] = s
            vn = u_ref[rows, :] - _raw_nn(w_ref[rows, :], s)
            o_ref[rows, :] = _raw_nn(qg_ref[rows, :], s) + _raw_nn(qk_ref[i], vn)
            return s * gl_ref[i] + _raw_tn(kd_ref[rows, :], vn)

        lax.fori_loop(0, n, step, jnp.zeros((DH, DH), F32))

    return pl.pallas_call(
        body, grid=(HEADS,), in_specs=[col, col, col, qk_spec, col, gl_spec], out_specs=[col, st_spec],
        out_shape=[jax.ShapeDtypeStruct((t, D), F32), jax.ShapeDtypeStruct((HEADS, n, DH, DH), F32)], name=name,
        compiler_params=_params("parallel"))(u, w, qg, qk, kd, gl)


def _gdn_scan_bwd(name, u, w, qg, qk, kd, gl, states, do):
    t = u.shape[0]
    n, col, qk_spec, gl_spec, st_spec = _head_specs(t)
    c = GDN_CHUNK

    def body(u_ref, w_ref, qg_ref, qk_ref, kd_ref, gl_ref, s_ref, do_ref,
             du_ref, dw_ref, dqg_ref, dqk_ref, dkd_ref, dgl_ref):
        def step(r, ds):
            i = n - 1 - r
            rows = pl.ds(pl.multiple_of(i * c, c), c)
            s, do_c, w_c = s_ref[i], do_ref[rows, :], w_ref[rows, :]
            vn = u_ref[rows, :] - _raw_nn(w_c, s)
            dvn = _raw_tn(qk_ref[i], do_c) + _raw_nn(kd_ref[rows, :], ds)
            du_ref[rows, :] = dvn
            dw_ref[rows, :] = -_raw_nt(dvn, s)
            dqg_ref[rows, :] = _raw_nt(do_c, s)
            dqk_ref[i] = _raw_nt(do_c, vn)
            dkd_ref[rows, :] = _raw_nt(vn, ds)
            dgl_ref[i] = jnp.sum(ds * s, axis=0, keepdims=True)
            return _raw_tn(qg_ref[rows, :], do_c) + ds * gl_ref[i] - _raw_tn(w_c, dvn)

        lax.fori_loop(0, n, step, jnp.zeros((DH, DH), F32))

    big = jax.ShapeDtypeStruct((t, D), F32)
    return pl.pallas_call(
        body, grid=(HEADS,), in_specs=[col, col, col, qk_spec, col, gl_spec, st_spec, col],
        out_specs=[col, col, col, qk_spec, col, gl_spec],
        out_shape=[big, big, big, jax.ShapeDtypeStruct(qk.shape, F32), big, jax.ShapeDtypeStruct(gl.shape, F32)],
        name=name, compiler_params=_params("parallel"))(u, w, qg, qk, kd, gl, states, do)


def _cumsum_rows(name, parts, reverse):
    t = parts[0].shape[0]
    blk = min(t, 256)
    nb = t // blk
    n_in = len(parts)

    def body(*refs):
        o_ref = refs[n_in]
        ii = lax.broadcasted_iota(jnp.int32, (blk, blk), 0)
        jj = lax.broadcasted_iota(jnp.int32, (blk, blk), 1)
        tri = ((ii <= jj) if reverse else (ii >= jj)).astype(F32)
        carry = jnp.zeros((1, LANE), F32)
        for b in (range(nb - 1, -1, -1) if reverse else range(nb)):
            rows = pl.ds(b * blk, blk)
            x = refs[0][rows, :]
            for r in refs[1:n_in]:
                x = x + r[rows, :]
            o_ref[rows, :] = _raw_nn(tri, x, True) + carry
            carry = carry + jnp.sum(x, axis=0, keepdims=True)

    return pl.pallas_call(body, out_shape=jax.ShapeDtypeStruct((t, LANE), F32), name=name,
                          compiler_params=_params())(*parts)


def _fox_blocks(t):
    blk = 256 if t % 256 == 0 and t >= 1024 else 128
    return blk, t // blk


def _fox_logits(q, k, cq, ck, row0, col0):
    s = _raw_nt(q, k) * (DH ** -0.5) + cq - ck
    rows = row0 + lax.broadcasted_iota(jnp.int32, s.shape, 0)
    cols = col0 + lax.broadcasted_iota(jnp.int32, s.shape, 1)
    return jnp.where(cols <= rows, s, -jnp.inf)


def _fox_fwd(name, qn, kn, proj, c_col, c_row):
    t = qn.shape[0]
    blk, nb = _fox_blocks(t)
    voff = FOX_V_BLOCK

    def body(q_ref, k_ref, v_ref, cc_ref, cr_ref, o_ref, lse_ref):
        i = pl.program_id(1)
        q, cq = q_ref[...], cc_ref[...]

        def step(j, carry):
            m, l, acc = carry
            rows = pl.ds(pl.multiple_of(j * blk, blk), blk)
            s = _fox_logits(q, k_ref[rows, :], cq, cr_ref[j], i * blk, j * blk)
            m_new = jnp.maximum(m, jnp.max(s, axis=1, keepdims=True))
            p = jnp.exp(s - m_new)
            alpha = jnp.exp(m - m_new)
            return m_new, alpha * l + jnp.sum(p, axis=1, keepdims=True), alpha * acc + _raw_nn(p, v_ref[rows, :])

        init = (jnp.full((blk, 1), -1e30, F32), jnp.zeros((blk, 1), F32), jnp.zeros((blk, DH), F32))
        m, l, acc = lax.fori_loop(0, i + 1, step, init)
        o_ref[...] = acc / l
        lse_ref[...] = m + jnp.log(l)

    tile = pl.BlockSpec((blk, DH), lambda h, i: (i, h))
    colv = pl.BlockSpec((None, blk, 1), lambda h, i: (h, i, 0))
    return pl.pallas_call(
        body, grid=(HEADS, nb),
        in_specs=[tile, pl.BlockSpec((t, DH), lambda h, i: (0, h)), pl.BlockSpec((t, DH), lambda h, i: (0, voff + h)),
                  colv, pl.BlockSpec((None, nb, 1, blk), lambda h, i: (h, 0, 0, 0))],
        out_specs=[tile, colv],
        out_shape=[jax.ShapeDtypeStruct((t, D), F32), jax.ShapeDtypeStruct((HEADS, t, 1), F32)], name=name,
        compiler_params=_params("parallel", "parallel"))(qn, kn, proj, c_col, c_row)


def _fox_dq(name, qn, kn, proj, c_col, c_row, o, lse, do):
    t = qn.shape[0]
    blk, nb = _fox_blocks(t)
    voff = FOX_V_BLOCK

    def body(q_ref, k_ref, v_ref, cc_ref, cr_ref, o_ref, lse_ref, do_ref, dq_ref, dcc_ref, dl_ref):
        i = pl.program_id(1)
        q, cq, do_b, lse_b = q_ref[...], cc_ref[...], do_ref[...], lse_ref[...]
        delta = jnp.sum(do_b * o_ref[...], axis=1, keepdims=True)

        def step(j, carry):
            dq, dcc = carry
            rows = pl.ds(pl.multiple_of(j * blk, blk), blk)
            k = k_ref[rows, :]
            p = jnp.exp(_fox_logits(q, k, cq, cr_ref[j], i * blk, j * blk) - lse_b)
            ds = p * (_raw_nt(do_b, v_ref[rows, :]) - delta)
            return dq + _raw_nn(ds, k), dcc + jnp.sum(ds, axis=1, keepdims=True)

        dq, dcc = lax.fori_loop(0, i + 1, step, (jnp.zeros((blk, DH), F32), jnp.zeros((blk, 1), F32)))
        dq_ref[...] = dq * (DH ** -0.5)
        dcc_ref[...] = dcc
        dl_ref[...] = delta

    tile = pl.BlockSpec((blk, DH), lambda h, i: (i, h))
    colv = pl.BlockSpec((None, blk, 1), lambda h, i: (h, i, 0))
    vec = jax.ShapeDtypeStruct((HEADS, t, 1), F32)
    return pl.pallas_call(
        body, grid=(HEADS, nb),
        in_specs=[tile, pl.BlockSpec((t, DH), lambda h, i: (0, h)), pl.BlockSpec((t, DH), lambda h, i: (0, voff + h)),
                  colv, pl.BlockSpec((None, nb, 1, blk), lambda h, i: (h, 0, 0, 0)), tile, colv, tile],
        out_specs=[tile, colv, colv],
        out_shape=[jax.ShapeDtypeStruct((t, D), F32), vec, vec], name=name,
        compiler_params=_params("parallel", "parallel"))(qn, kn, proj, c_col, c_row, o, lse, do)


def _fox_dkv(name, qn, kn, proj, c_col, c_row, lse, delta, do):
    t = qn.shape[0]
    blk, nb = _fox_blocks(t)
    voff = FOX_V_BLOCK

    def body(q_ref, k_ref, v_ref, cc_ref, cr_ref, lse_ref, dl_ref, do_ref, dk_ref, dv_ref, dcr_ref):
        j = pl.program_id(1)
        k, v, ck = k_ref[...], v_ref[...], cr_ref[...]

        def step(i, carry):
            dk, dv, dcr = carry
            rows = pl.ds(pl.multiple_of(i * blk, blk), blk)
            q, do_b = q_ref[rows, :], do_ref[rows, :]
            p = jnp.exp(_fox_logits(q, k, cc_ref[rows, :], ck, i * blk, j * blk) - lse_ref[rows, :])
            ds = p * (_raw_nt(do_b, v) - dl_ref[rows, :])
            return dk + _raw_tn(ds, q), dv + _raw_tn(p, do_b), dcr - jnp.sum(ds, axis=0, keepdims=True)

        init = (jnp.zeros((blk, DH), F32), jnp.zeros((blk, DH), F32), jnp.zeros((1, blk), F32))
        dk, dv, dcr = lax.fori_loop(j, nb, step, init)
        dk_ref[...] = dk * (DH ** -0.5)
        dv_ref[...] = dv.astype(dv_ref.dtype)
        dcr_ref[...] = dcr

    full = pl.BlockSpec((t, DH), lambda h, j: (0, h))
    colf = pl.BlockSpec((None, t, 1), lambda h, j: (h, 0, 0))
    tile = pl.BlockSpec((blk, DH), lambda h, j: (j, h))
    rowv = pl.BlockSpec((None, None, 1, blk), lambda h, j: (h, j, 0, 0))
    tile_v = pl.BlockSpec((blk, DH), lambda h, j: (j, voff + h))
    big = jax.ShapeDtypeStruct((t, D), F32)
    return pl.pallas_call(
        body, grid=(HEADS, nb),
        in_specs=[full, tile, tile_v, colf, rowv, colf, colf, full],
        out_specs=[tile, tile_v, rowv],
        out_shape=[big, jax.ShapeDtypeStruct(proj.shape, BF16), jax.ShapeDtypeStruct((HEADS, nb, 1, blk), F32)], name=name,
        compiler_params=_params("parallel", "parallel"))(qn, kn, proj, c_col, c_row, lse, delta, do)


def _loss_head(y, target):
    t = y.shape[0]
    tm = min(t, 512)

    def body(y_ref, t_ref, dy_ref, sum_ref):
        err = y_ref[...] - t_ref[...]
        dy_ref[...] = err * (1.0 / D)
        part = jnp.sum(jnp.sum(err * err, axis=1, keepdims=True), axis=0, keepdims=True)

        @pl.when(pl.program_id(0) == 0)
        def _():
            sum_ref[...] = jnp.zeros_like(sum_ref)

        sum_ref[...] += jnp.broadcast_to(part, sum_ref.shape)

    return pl.pallas_call(
        body, grid=(t // tm,), in_specs=[_rows(tm, D), _rows(tm, D)], out_specs=[_rows(tm, D), _const((1, LANE))],
        out_shape=[jax.ShapeDtypeStruct((t, D), F32), jax.ShapeDtypeStruct((1, LANE), F32)], name="loss_head",
        compiler_params=_params("arbitrary"))(y, target)


ADAM_BLOCK_BYTES = 3 * 1024 * 1024


def _adam(name, lands, owns, mine, w, m, v):
    layers = len(lands)
    r, c = owns[0].shape
    block_bytes = lambda rows: N_DEV * rows * c * lands[0].dtype.itemsize
    tr = r
    for cand in (512, 256, 128, 64, 32, 16):
        if block_bytes(tr) > ADAM_BLOCK_BYTES and r % cand == 0:
            tr = cand
    nr = r // tr

    def body(*refs):
        me_ref, land_refs, own_refs = refs[0], refs[1:1 + layers], refs[1 + layers:1 + 2 * layers]
        w_ref, m_ref, v_ref, g_ref, d_ref, nm_ref, nv_ref = refs[1 + 2 * layers:]
        layer = pl.program_id(0)
        for l in range(layers):
            @pl.when(layer == l)
            def _(l=l):
                g = jnp.zeros((tr, c), F32)
                for s in range(N_DEV):
                    g = g + jnp.where(me_ref[0] == s, own_refs[l][...], land_refs[l][s]).astype(F32)
                nm = ADAM_B1 * m_ref[...] + (1.0 - ADAM_B1) * g
                nv = ADAM_B2 * v_ref[...] + (1.0 - ADAM_B2) * (g * g)
                m_hat = nm / (1.0 - ADAM_B1 ** ADAM_STEP)
                v_hat = nv / (1.0 - ADAM_B2 ** ADAM_STEP)
                g_ref[...] = g
                d_ref[...] = -ADAM_LR * (m_hat / (jnp.sqrt(v_hat) + ADAM_EPS) + ADAM_WD * w_ref[...])
                nm_ref[...] = nm
                nv_ref[...] = nv

    at = lambda l: (lambda layer, i: jnp.where(layer == l, i, 0))
    land_specs = [pl.BlockSpec((N_DEV, tr, c), lambda layer, i, f=at(l): (0, f(layer, i), 0)) for l in range(layers)]
    own_specs = [pl.BlockSpec((tr, c), lambda layer, i, f=at(l): (f(layer, i), 0)) for l in range(layers)]
    blk = pl.BlockSpec((tr, c), lambda layer, i: (layer * nr + i, 0))
    out = jax.ShapeDtypeStruct(w.shape, F32)
    return pl.pallas_call(
        body, grid=(layers, nr),
        in_specs=[pl.BlockSpec(memory_space=pltpu.SMEM)] + land_specs + own_specs + [blk, blk, blk],
        out_specs=[blk] * 4, out_shape=[out] * 4, name=name,
        compiler_params=_params("arbitrary", "arbitrary"))(mine, *lands, *owns, w, m, v)


_HBM = pl.BlockSpec(memory_space=pltpu.HBM)
_SEM = pl.BlockSpec(memory_space=pltpu.SEMAPHORE)
_EFFECT = pltpu.SideEffectType.DATAFLOW_SIDE_EFFECTING


def _each_peer(x, y, c):
    flip = lambda v, bit: 1 - v if bit else v
    for p in range(1, N_DEV):
        px, py, pc = flip(x, p & 4), flip(y, p & 2), flip(c, p & 1)
        yield p, (px, py, pc), 4 * px + 2 * py + pc


def _sem(a, p):
    return a * (N_DEV - 1) + p - 1


def _exchange_start(name, arrays, scatter):
    n = len(arrays)
    lands = [lax.empty((N_DEV,) + (a.shape[1:] if sc else a.shape), a.dtype) for a, sc in zip(arrays, scatter)]

    def body(*refs):
        in_refs, land_refs = refs[:n], refs[n:2 * n]
        send_sems, recv_sems, token = refs[2 * n], refs[2 * n + 1], refs[-1]
        x, y, c = lax.axis_index("x"), lax.axis_index("y"), lax.axis_index("c")
        me = 4 * x + 2 * y + c
        for p, coords, peer in _each_peer(x, y, c):
            for a in range(n):
                pltpu.make_async_remote_copy(
                    src_ref=in_refs[a].at[peer] if scatter[a] else in_refs[a], dst_ref=land_refs[a].at[me],
                    send_sem=send_sems.at[_sem(a, p)], recv_sem=recv_sems.at[_sem(a, p)], device_id=coords,
                    device_id_type=pl.DeviceIdType.MESH).start()
        token[...] = jnp.zeros_like(token)

    sems = pltpu.SemaphoreType.DMA((n * (N_DEV - 1),))
    hbm = lambda a: pltpu.HBM(a.shape, a.dtype)
    out = pl.pallas_call(
        body, name=name,
        out_shape=(sems, sems, *[hbm(a) for a in arrays], *[hbm(l) for l in lands],
                   jax.ShapeDtypeStruct((8, LANE), F32)),
        in_specs=[_HBM] * (2 * n), out_specs=(_SEM, _SEM, *[_HBM] * (2 * n), pl.BlockSpec(memory_space=pltpu.VMEM)),
        input_output_aliases={i: 2 + i for i in range(2 * n)},
        compiler_params=pltpu.CompilerParams(has_side_effects=_EFFECT),
    )(*[pltpu.with_memory_space_constraint(a, pltpu.HBM) for a in arrays],
      *[pltpu.with_memory_space_constraint(l, pltpu.HBM) for l in lands])
    return (out[0], out[1], list(out[2:2 + n]), list(out[2 + n:2 + 2 * n]), scatter), out[-1]


def _exchange_wait(name, started, after, me):
    send_sems, recv_sems, sent, lands, scatter = started
    n = len(sent)

    def body(*refs):
        in_refs, land_refs = refs[:n], refs[n:2 * n]
        send_sems, recv_sems = refs[2 * n], refs[2 * n + 1]
        x, y, c = lax.axis_index("x"), lax.axis_index("y"), lax.axis_index("c")
        for p, coords, peer in _each_peer(x, y, c):
            for a in range(n):
                cp = pltpu.make_async_remote_copy(
                    src_ref=in_refs[a].at[peer] if scatter[a] else in_refs[a], dst_ref=land_refs[a].at[peer],
                    send_sem=send_sems.at[_sem(a, p)], recv_sem=recv_sems.at[_sem(a, p)], device_id=coords,
                    device_id_type=pl.DeviceIdType.MESH)
                cp.wait_send()
                cp.wait_recv()

    hbm = lambda a: pltpu.HBM(a.shape, a.dtype)
    out = pl.pallas_call(
        body, name=name, out_shape=(*[hbm(a) for a in sent], *[hbm(l) for l in lands]),
        in_specs=[_HBM] * (2 * n) + [_SEM, _SEM, pl.BlockSpec(memory_space=pl.ANY)], out_specs=[_HBM] * (2 * n),
        input_output_aliases={i: i for i in range(2 * n)},
        compiler_params=pltpu.CompilerParams(has_side_effects=_EFFECT),
    )(*sent, *lands, send_sems, recv_sems, after)
    owns = [lax.dynamic_index_in_dim(out[a], me, 0, keepdims=False) if scatter[a] else out[a] for a in range(n)]
    return list(out[n:]), owns


def _fill_own(land, own, me):
    slot = lax.broadcasted_iota(jnp.int32, (N_DEV,) + (1,) * own.ndim, 0)
    return jnp.where(slot == me, own[None], land)


def _rms_fwd(name, x, g):
    t = x.shape[0]
    tm = min(t, 512)
    return _seg_fwd(name, _f_rms, (t // tm,), [x, g], [_rows(tm, D), _const((1, D))],
                    [jax.ShapeDtypeStruct((t, D), BF16)], [_rows(tm, D)])[0]


def _rms_bwd(name, x, g, dh, dres):
    t = x.shape[0]
    tm = min(t, 512)
    return _seg_bwd(name, _f_rms_res, (t // tm,), [x, g], [_rows(tm, D), _const((1, D))],
                    [dh, dres], [_rows(tm, D), _rows(tm, D)], [F32, "acc"])


FFN_TC = 256
CONF_TC = 128


def _ffn_specs(t):
    nf = D_FF // FFN_TC
    return nf, [_cols(t, 2 * FFN_TC), _cols(FFN_K, 2 * FFN_TC)], [_cols(t, FFN_TC)]


def _ffn_fwd(tag, x, p):
    t = x.shape[0]
    nf, in_specs, out_specs = _ffn_specs(t)
    h = _rms_fwd(tag + "_rms", x, p["g"])
    u0 = _mm(tag + "_up", h, p["w_up"])
    act = _seg_fwd(tag + "_mid", _f_ffn_mid, (nf,), [u0, p["w_dw"]], in_specs,
                   [jax.ShapeDtypeStruct((t, D_FF), BF16)], out_specs)[0]
    out = _mm(tag + "_down", act, p["w_down"], res=x)
    return out, (x, h, u0, act)


def _ffn_bwd(tag, saved, p, dout):
    x, h, u0, act = saved
    t = x.shape[0]
    nf, in_specs, out_specs = _ffn_specs(t)
    g = {"w_down": _mm(tag + "_dwdown", act, dout, ta=True, out_dtype=BF16)}
    dact = _mm(tag + "_dact", dout, p["w_down"], tb=True, out_dtype=BF16)
    du0, g["w_dw"] = _seg_bwd(tag + "_dmid", _f_ffn_mid, (nf, 1), [u0, p["w_dw"]], _grid2(in_specs), [dact],
                              _grid2(out_specs), [BF16, "acc"])
    g["w_up"] = _mm(tag + "_dwup", h, du0, ta=True, out_dtype=BF16)
    dh = _mm(tag + "_dh", du0, p["w_up"], tb=True, out_dtype=BF16)
    dx, g["g"] = _rms_bwd(tag + "_drms", x, p["g"], dh, dout)
    return dx, g


def _conf_specs(t):
    tm = min(t, 512)
    nc = D // CONF_TC
    vec = _const((1, D))
    glu_in = [_cols(t, 2 * CONF_TC), _cols(1, 2 * CONF_TC), _cols(CONF_K, CONF_TC)]
    return tm, nc, glu_in, [_cols(t, CONF_TC)], [_rows(tm, D), vec, vec, vec]


def _conf_fwd(tag, x, p):
    t = x.shape[0]
    tm, nc, glu_in, glu_out, ln_in = _conf_specs(t)
    h = _rms_fwd(tag + "_rms", x, p["g"])
    u = _mm(tag + "_in", h, p["w_in"])
    cv = _seg_fwd(tag + "_gluconv", _f_conf_glu_conv, (nc,), [u, p["b_in"], p["w_dw"]], glu_in,
                  [jax.ShapeDtypeStruct((t, D), F32)], glu_out)[0]
    act = _seg_fwd(tag + "_lnsilu", _f_conf_ln_silu, (t // tm,), [cv, p["b_dw"], p["ln_g"], p["ln_b"]], ln_in,
                   [jax.ShapeDtypeStruct((t, D), BF16)], [_rows(tm, D)])[0]
    out = _mm(tag + "_out", act, p["w_out"], res=x)
    return out, (x, h, u, cv, act)


def _conf_bwd(tag, saved, p, dout):
    x, h, u, cv, act = saved
    t = x.shape[0]
    tm, nc, glu_in, glu_out, ln_in = _conf_specs(t)
    g = {"w_out": _mm(tag + "_dwout", act, dout, ta=True, out_dtype=BF16)}
    dact = _mm(tag + "_dact", dout, p["w_out"], tb=True, out_dtype=BF16)
    dcv, g["b_dw"], g["ln_g"], g["ln_b"] = _seg_bwd(
        tag + "_dlnsilu", _f_conf_ln_silu, (t // tm,), [cv, p["b_dw"], p["ln_g"], p["ln_b"]], ln_in, [dact],
        [_rows(tm, D)], [F32, "acc", "acc", "acc"])
    du, g["b_in"], g["w_dw"] = _seg_bwd(tag + "_dgluconv", _f_conf_glu_conv, (nc, 1), [u, p["b_in"], p["w_dw"]],
                                        _grid2(glu_in), [dcv], _grid2(glu_out), [BF16, "acc", "acc"])
    g["w_in"] = _mm(tag + "_dwin", h, du, ta=True, out_dtype=BF16)
    dh = _mm(tag + "_dh", du, p["w_in"], tb=True, out_dtype=BF16)
    dx, g["g"] = _rms_bwd(tag + "_drms", x, p["g"], dh, dout)
    return dx, g


def _gdn_specs(t):
    tc, tm, rows = 256, min(t, 256), min(t, 512)
    nq = 3 * D // tc
    conv = ([_cols(t, tc), _cols(GDN_K, tc)], [_cols(t, tc)])
    gate_in = [_rows(tm, 2 * D, 0), _rows(tm, LANE, 4 * D // LANE), _const((1, LANE)), _const((1, LANE))]
    gate_out = [_rows(tm, D)] * 4
    head = pl.BlockSpec((rows, DH), lambda h, i: (i, h))
    headv = pl.BlockSpec((rows, DH), lambda h, i: (i, 2 * HEADS + h))
    nch = rows // GDN_CHUNK
    pre_in = [head, head, headv, head, head]
    pre_out = [head, head, head, pl.BlockSpec((None, nch, GDN_CHUNK, GDN_CHUNK), lambda h, i: (h, i, 0, 0)), head,
               pl.BlockSpec((None, nch, 1, DH), lambda h, i: (h, i, 0, 0))]
    post_in = [_rows(tm, D), _rows(tm, D, 3), _const((1, DH))]
    return tm, rows, nq, conv, gate_in, gate_out, pre_in, pre_out, post_in


def _gdn_fwd(tag, x, p):
    t = x.shape[0]
    tm, rows, nq, conv, gate_in, gate_out, pre_in, pre_out, post_in = _gdn_specs(t)
    n = t // GDN_CHUNK
    big = jax.ShapeDtypeStruct((t, D), F32)
    h = _rms_fwd(tag + "_rms", x, p["g"])
    proj = _mm(tag + "_in", h, p["w_in"])
    qkv = _seg_fwd(tag + "_conv", _f_conv_silu, (nq,), [proj, p["conv_w"]], conv[0],
                   [jax.ShapeDtypeStruct((t, 3 * D), F32)], conv[1])[0]
    gate_ins = [qkv, proj, p["a_log"], p["dt_bias"]]
    qn, kn, gb, bb = _seg_fwd(tag + "_gates", _f_gdn_gates, (t // tm,), gate_ins, gate_in, [big] * 4, gate_out)
    pre_ins = [qn, kn, qkv, gb, bb]
    pre_shapes = [big, big, big, jax.ShapeDtypeStruct((HEADS, n, GDN_CHUNK, GDN_CHUNK), F32), big,
                  jax.ShapeDtypeStruct((HEADS, n, 1, DH), F32)]
    pre = _seg_fwd(tag + "_prescan", _f_gdn_prescan, (HEADS, t // rows), pre_ins, pre_in, pre_shapes, pre_out)
    o, states = _gdn_scan_fwd(tag + "_scan", *pre)
    post_ins = [o, proj, p["o_g"]]
    act = _seg_fwd(tag + "_post", _f_gdn_post, (t // tm,), post_ins, post_in, [jax.ShapeDtypeStruct((t, D), BF16)],
                   [_rows(tm, D)])[0]
    out = _mm(tag + "_out", act, p["w_out"], res=x)
    return out, (x, h, proj, gate_ins, pre_ins, pre, states, post_ins, act)


def _gdn_bwd(tag, saved, p, dout):
    x, h, proj, gate_ins, pre_ins, pre, states, post_ins, act = saved
    t = x.shape[0]
    tm, rows, nq, conv, gate_in, gate_out, pre_in, pre_out, post_in = _gdn_specs(t)
    g = {"w_out": _mm(tag + "_dwout", act, dout, ta=True, out_dtype=BF16)}
    dact = _mm(tag + "_dact", dout, p["w_out"], tb=True, out_dtype=BF16)
    do, dproj, g["o_g"] = _seg_bwd(tag + "_dpost", _f_gdn_post, (t // tm,), post_ins, post_in, [dact],
                                   [_rows(tm, D)], [F32, BF16, "acc"])
    dpre = _gdn_scan_bwd(tag + "_dscan", *pre, states, do)
    dqn, dkn, dqkv, dgb, dbb = _seg_bwd(tag + "_dprescan", _f_gdn_prescan, (HEADS, t // rows), pre_ins, pre_in,
                                        dpre, pre_out, [F32] * 5)
    dqkv, dproj, g["a_log"], g["dt_bias"] = _seg_bwd(
        tag + "_dgates", _f_gdn_gates, (t // tm,), gate_ins, gate_in, [dqn, dkn, dgb, dbb], gate_out,
        [F32, BF16, "acc", "acc"], into={0: dqkv, 1: dproj})
    dproj, g["conv_w"] = _seg_bwd(tag + "_dconv", _f_conv_silu, (nq, 1), [proj, p["conv_w"]], _grid2(conv[0]),
                                  [dqkv], _grid2(conv[1]), [BF16, "acc"], into={0: dproj})
    g["w_in"] = _mm(tag + "_dwin", h, dproj, ta=True, out_dtype=BF16)
    dh = _mm(tag + "_dh", dproj, p["w_in"], tb=True, out_dtype=BF16)
    dx, g["g"] = _rms_bwd(tag + "_drms", x, p["g"], dh, dout)
    return dx, g


def _fox_specs(t):
    tm = min(t, 512)
    vec = _const((1, LANE))
    pre_in = [_rows(tm, 2 * D + LANE, 0), vec, vec, vec]
    pre_out = [_rows(tm, D), _rows(tm, D), _rows(tm, LANE)]
    return tm, pre_in, pre_out


def _per_head(c):
    return jnp.transpose(c[:, :HEADS])


def _per_lane(ch):
    return jnp.pad(jnp.transpose(ch), ((0, 0), (0, LANE - HEADS)))


def _fox_fwd_layer(tag, x, p):
    t = x.shape[0]
    tm, pre_in, pre_out = _fox_specs(t)
    blk, nb = _fox_blocks(t)
    h = _rms_fwd(tag + "_rms", x, p["g"])
    proj = _mm(tag + "_in", h, p["w_in"])
    pre_ins = [proj, p["q_g"], p["k_g"], p["b_f"]]
    qn, kn, lf = _seg_fwd(tag + "_pre", _f_fox_pre, (t // tm,), pre_ins, pre_in,
                          [jax.ShapeDtypeStruct((t, D), BF16)] * 2 + [jax.ShapeDtypeStruct((t, LANE), F32)], pre_out)
    ch = _per_head(_cumsum_rows(tag + "_cumsum", [lf], False))
    c_col, c_row = ch.reshape(HEADS, t, 1), ch.reshape(HEADS, nb, 1, blk)
    o, lse = _fox_fwd(tag + "_attn", qn, kn, proj, c_col, c_row)
    out = _mm(tag + "_out", o, p["w_out"], res=x)
    return out, (x, h, proj, pre_ins, qn, kn, c_col, c_row, o, lse)


def _fox_bwd_layer(tag, saved, p, dout):
    x, h, proj, pre_ins, qn, kn, c_col, c_row, o, lse = saved
    t = x.shape[0]
    tm, pre_in, pre_out = _fox_specs(t)
    g = {"w_out": _mm(tag + "_dwout", o, dout, ta=True, out_dtype=BF16)}
    do = _mm(tag + "_do", dout, p["w_out"], tb=True)
    dqn, dc_col, delta = _fox_dq(tag + "_dq", qn, kn, proj, c_col, c_row, o, lse, do)
    dkn, dproj, dc_row = _fox_dkv(tag + "_dkv", qn, kn, proj, c_col, c_row, lse, delta, do)
    dlf = _cumsum_rows(tag + "_dcumsum", [_per_lane(dc_col.reshape(HEADS, t)), _per_lane(dc_row.reshape(HEADS, t))],
                       True)
    dproj, g["q_g"], g["k_g"], g["b_f"] = _seg_bwd(
        tag + "_dpre", _f_fox_pre, (t // tm,), pre_ins, pre_in, [dqn, dkn, dlf], pre_out,
        [BF16, "acc", "acc", "acc"], into={0: dproj})
    g["w_in"] = _mm(tag + "_dwin", h, dproj, ta=True, out_dtype=BF16)
    dh = _mm(tag + "_dh", dproj, p["w_in"], tb=True, out_dtype=BF16)
    dx, g["g"] = _rms_bwd(tag + "_drms", x, p["g"], dh, dout)
    return dx, g


_MIXERS = ((_conf_fwd, _conf_bwd), (_gdn_fwd, _gdn_bwd), (_fox_fwd_layer, _fox_bwd_layer))


def _local_step(x, target, params_of, on_grads):
    saved, params = [], []
    for i in range(DEPTH):
        mp = params_of(i, 0, x)
        x, sm = _MIXERS[i % N_MIXERS][0](f"l{i}_mix", x, mp)
        fp = params_of(i, 1, x)
        x, sf = _ffn_fwd(f"l{i}_ffn", x, fp)
        saved.append((sm, sf))
        params.append((mp, fp))
    dx, sq = _loss_head(x, target)
    for i in reversed(range(DEPTH)):
        dx, gf = _ffn_bwd(f"l{i}_ffn", saved[i][1], params[i][1], dx)
        dx = on_grads(i, 1, gf, dx)
        dx, gm = _MIXERS[i % N_MIXERS][1](f"l{i}_mix", saved[i][0], params[i][0], dx)
        dx = on_grads(i, 0, gm, dx)
    return sq, dx


def _unshard(name, g):
    axis = g.ndim - 2 if name in ROW_SHARDED else g.ndim - 1
    m = jnp.moveaxis(g, 0, axis - 1)
    return m.reshape(m.shape[:axis - 1] + (N_DEV * m.shape[axis],) + m.shape[axis + 1:])


def _reshard(name, full):
    axis = full.ndim - 2 if name in ROW_SHARDED else full.ndim - 1
    s = full.shape
    return jnp.moveaxis(full.reshape(s[:axis] + (N_DEV, s[axis] // N_DEV) + s[axis + 1:]), axis, 0)


def _pad_cols(a, width):
    return jnp.pad(a, [(0, 0)] * (a.ndim - 1) + [(0, width - a.shape[-1])])


def _lane_vec(v):
    return _pad_cols(v.reshape(1, -1), LANE)


REP_ROWS = 16


def _pack_rep(r):
    small = [_pad_cols(r[n].reshape(1, -1), LANE) for n in REPLICATED[2:]]
    row = jnp.concatenate(small + [jnp.zeros((1, D - LANE * len(small)), F32)], axis=1)
    pad = jnp.zeros((REP_ROWS - 2 * DEPTH - 1, D), F32)
    return jnp.concatenate([r['mix_norm_g'].reshape(DEPTH, D), r['ffn_norm_g'].reshape(DEPTH, D), row, pad], axis=0)


def _unpack_rep(a, shapes):
    out = {'mix_norm_g': a[:DEPTH], 'ffn_norm_g': a[DEPTH:2 * DEPTH]}
    for i, n in enumerate(REPLICATED[2:]):
        out[n] = a[2 * DEPTH:2 * DEPTH + 1, i * LANE:i * LANE + shapes[n][1]].reshape(shapes[n])
    return out


def _view2d(a):
    return a.reshape(-1, a.shape[-1])


MIXER_SHARDED = (
    (('conv_w_in', 'w_in'), ('conv_w_out', 'w_out'), ('conv_b_in', 'b_in'), ('conv_w_dw', 'w_dw'),
     ('conv_b_dw', 'b_dw'), ('conv_ln_g', 'ln_g'), ('conv_ln_b', 'ln_b')),
    (('gdn_w_in', 'w_in'), ('gdn_w_out', 'w_out'), ('gdn_conv_w', 'conv_w')),
    (('fox_w_in', 'w_in'), ('fox_w_out', 'w_out')),
)
FFN_SHARDED = (('ffn_w_up', 'w_up'), ('ffn_w_down', 'w_down'), ('ffn_w_dw', 'w_dw'))
MIXER_REPLICATED = (
    (),
    (('gdn_a_log', 'a_log'), ('gdn_dt_bias', 'dt_bias'), ('gdn_o_norm_g', 'o_g')),
    (('fox_b_f', 'b_f'), ('fox_q_norm_g', 'q_g'), ('fox_k_norm_g', 'k_g')),
)
ROW_VECTORS = ('conv_b_in', 'conv_b_dw', 'conv_ln_g', 'conv_ln_b')
INTERLEAVED = {'ffn_w_up': (D_FF, FFN_TC), 'ffn_w_dw': (D_FF, FFN_TC), 'conv_w_in': (D, CONF_TC),
               'conv_b_in': (D, CONF_TC)}
PACK_GROUP = 16 * D


def _part_entries(i, part):
    ent = [(n, i, k) for n, k in FFN_SHARDED] if part else [(n, i // N_MIXERS, k) for n, k in MIXER_SHARDED[i % N_MIXERS]]
    return [e for e in ent if e[0] in MATRICES], [e for e in ent if e[0] not in MATRICES]


def _interleave(a, half, blk):
    s = a.shape[:-1]
    return jnp.swapaxes(a.reshape(s + (2, half // blk, blk)), -3, -2).reshape(s + (2 * half,))


def _deinterleave(a, half, blk):
    s = a.shape[:-1]
    return jnp.swapaxes(a.reshape(s + (half // blk, 2, blk)), -3, -2).reshape(s + (2 * half,))


def _to_param(name, whole):
    if name in ROW_VECTORS:
        whole = whole[None]
    if name in INTERLEAVED:
        return _interleave(whole, *INTERLEAVED[name])
    if name == 'gdn_w_in':
        return _pad_cols(whole, GDN_PAD)
    if name == 'fox_w_in':
        return jnp.concatenate([whole[:, :2 * D], _pad_cols(whole[:, 3 * D:], LANE), whole[:, 2 * D:3 * D]], axis=1)
    return whole


def _from_grad(name, g):
    if name in INTERLEAVED:
        g = _deinterleave(g, *INTERLEAVED[name])
    if name in ROW_VECTORS:
        return g[0]
    if name == 'gdn_w_in':
        return g[:, :4 * D + 2 * HEADS]
    if name == 'fox_w_in':
        return jnp.concatenate([g[:, :2 * D], g[:, 2 * D + LANE:], g[:, 2 * D:2 * D + HEADS]], axis=1)
    return g


def _pack_rows(parts, lead):
    out = []
    for a in parts:
        flat = a.reshape(a.shape[:lead] + (-1,))
        size = flat.shape[-1]
        padded = -(-size // PACK_GROUP) * PACK_GROUP
        flat = jnp.pad(flat, [(0, 0)] * lead + [(0, padded - size)])
        out.append(flat.reshape(a.shape[:lead] + (padded // D, D)))
    return jnp.concatenate(out, axis=lead)


def _unpack_rows(packed, shapes, lead):
    out, row = [], 0
    head = packed.shape[:lead]
    for s in shapes:
        size = 1
        for d in s:
            size *= d
        rows = -(-size // PACK_GROUP) * (PACK_GROUP // D)
        part = lax.slice_in_dim(packed, row, row + rows, axis=lead)
        out.append(part.reshape(head + (rows * D,))[..., :size].reshape(head + tuple(s)))
        row += rows
    return out


def _part_dict(i, part, whole, rep):
    if part:
        p = {k: _to_param(n, whole[n]) for n, k in FFN_SHARDED}
        p["g"] = rep['ffn_norm_g'][i][None]
        return p
    kind, j = i % N_MIXERS, i // N_MIXERS
    p = {k: _to_param(n, whole[n]) for n, k in MIXER_SHARDED[kind]}
    for n, k in MIXER_REPLICATED[kind]:
        p[k] = rep[n][j][None] if rep[n].shape[-1] == DH else _lane_vec(rep[n][j])
    p["g"] = rep['mix_norm_g'][i][None]
    return p


def _part_grads(i, part, g):
    return {n: _from_grad(n, g[k]) for n, k in (FFN_SHARDED if part else MIXER_SHARDED[i % N_MIXERS])}


def _replicated_grads(grads):
    rep = {'mix_norm_g': jnp.concatenate([g[0]["g"] for g in grads]),
           'ffn_norm_g': jnp.concatenate([g[1]["g"] for g in grads])}
    for kind in range(N_MIXERS):
        for n, k in MIXER_REPLICATED[kind]:
            rep[n] = jnp.stack([grads[i][0][k] for i in range(kind, DEPTH, N_MIXERS)])
    return rep


def kernel(x, mix_norm_g, ffn_norm_g, conv_w_in, conv_b_in, conv_w_dw, conv_b_dw, conv_ln_g, conv_ln_b, conv_w_out, gdn_w_in, gdn_conv_w, gdn_a_log, gdn_dt_bias, gdn_o_norm_g, gdn_w_out, fox_w_in, fox_b_f, fox_q_norm_g, fox_k_norm_g, fox_w_out, ffn_w_up, ffn_w_dw, ffn_w_down, loss_target, m_mix_norm_g, m_ffn_norm_g, m_conv_w_in, m_conv_b_in, m_conv_w_dw, m_conv_b_dw, m_conv_ln_g, m_conv_ln_b, m_conv_w_out, m_gdn_w_in, m_gdn_conv_w, m_gdn_a_log, m_gdn_dt_bias, m_gdn_o_norm_g, m_gdn_w_out, m_fox_w_in, m_fox_b_f, m_fox_q_norm_g, m_fox_k_norm_g, m_fox_w_out, m_ffn_w_up, m_ffn_w_dw, m_ffn_w_down, v_mix_norm_g, v_ffn_norm_g, v_conv_w_in, v_conv_b_in, v_conv_w_dw, v_conv_b_dw, v_conv_ln_g, v_conv_ln_b, v_conv_w_out, v_gdn_w_in, v_gdn_conv_w, v_gdn_a_log, v_gdn_dt_bias, v_gdn_o_norm_g, v_gdn_w_out, v_fox_w_in, v_fox_b_f, v_fox_q_norm_g, v_fox_k_norm_g, v_fox_w_out, v_ffn_w_up, v_ffn_w_dw, v_ffn_w_down):
    given = dict(zip(
        WEIGHTS + ["m_" + n for n in WEIGHTS] + ["v_" + n for n in WEIGHTS],
        (mix_norm_g, ffn_norm_g, conv_w_in, conv_b_in, conv_w_dw, conv_b_dw, conv_ln_g, conv_ln_b, conv_w_out, gdn_w_in, gdn_conv_w, gdn_a_log, gdn_dt_bias, gdn_o_norm_g, gdn_w_out, fox_w_in, fox_b_f, fox_q_norm_g, fox_k_norm_g, fox_w_out, ffn_w_up, ffn_w_dw, ffn_w_down,
         m_mix_norm_g, m_ffn_norm_g, m_conv_w_in, m_conv_b_in, m_conv_w_dw, m_conv_b_dw, m_conv_ln_g, m_conv_ln_b, m_conv_w_out, m_gdn_w_in, m_gdn_conv_w, m_gdn_a_log, m_gdn_dt_bias, m_gdn_o_norm_g, m_gdn_w_out, m_fox_w_in, m_fox_b_f, m_fox_q_norm_g, m_fox_k_norm_g, m_fox_w_out, m_ffn_w_up, m_ffn_w_dw, m_ffn_w_down,
         v_mix_norm_g, v_ffn_norm_g, v_conv_w_in, v_conv_b_in, v_conv_w_dw, v_conv_b_dw, v_conv_ln_g, v_conv_ln_b, v_conv_w_out, v_gdn_w_in, v_gdn_conv_w, v_gdn_a_log, v_gdn_dt_bias, v_gdn_o_norm_g, v_gdn_w_out, v_fox_w_in, v_fox_b_f, v_fox_q_norm_g, v_fox_k_norm_g, v_fox_w_out, v_ffn_w_up, v_ffn_w_dw, v_ffn_w_down)))

    me = 4 * lax.axis_index("x") + 2 * lax.axis_index("y") + lax.axis_index("c")

    gathers, token = {}, jnp.zeros((1, 1), F32)
    for i in range(DEPTH):
        for part in (0, 1):
            mats, smalls = _part_entries(i, part)
            sent = [given[n][j].astype(BF16) for n, j, _ in mats]
            if smalls:
                sent.append(_pack_rows([given[n][j] for n, j, _ in smalls], 0))
            gathers[i, part], tok = _exchange_start(f"gather{i}{'mf'[part]}_start", sent, [False] * len(sent))
            token = token + tok[0:1, 0:1]

    def params_of(i, part, x_in):
        mats, smalls = _part_entries(i, part)
        after = token if (i, part) == (0, 0) else x_in
        lands, owns = _exchange_wait(f"gather{i}{'mf'[part]}_wait", gathers[i, part], after, me)
        got = [_fill_own(l, o, me) for l, o in zip(lands, owns)]
        whole = {n: _unshard(n, g) for (n, _, _), g in zip(mats, got)}
        shapes = [given[n].shape[1:] for n, _, _ in smalls]
        for (n, _, _), g in zip(smalls, _unpack_rows(got[-1], shapes, 1) if smalls else []):
            whole[n] = _unshard(n, g)
        return _part_dict(i, part, whole, given)

    grads, exchanges = [[None, None] for _ in range(DEPTH)], {}

    def on_grads(i, part, g, dx):
        grads[i][part] = g
        mats, smalls = _part_entries(i, part)
        whole = _part_grads(i, part, g)
        sent = [_reshard(n, whole[n]) for n, _, _ in mats]
        if smalls:
            sent.append(_pack_rows([_reshard(n, whole[n]) for n, _, _ in smalls], 1))
        last = (i, part) == (0, 0)
        if last:
            sent.append(_pack_rep(_replicated_grads(grads)))
        exchanges[i, part], tok = _exchange_start(f"grads{i}{'mf'[part]}_start", sent,
                                                  [True] * (len(sent) - last) + [False] * last)
        return dx if last else dx + tok[0:1, 0:1]

    sq, dx = _local_step(x[0], loss_target[0], params_of, on_grads)
    loss = (0.5 / D) * lax.psum(sq[0, 0], ("x", "y", "c"))

    pieces = {n: [] for n in SHARDED}
    for i in range(DEPTH):
        for part in (0, 1):
            mats, smalls = _part_entries(i, part)
            lands, owns = _exchange_wait(f"grads{i}{'mf'[part]}_wait", exchanges[i, part], dx, me)
            if (i, part) == (0, 0):
                rep_piece = (lands[-1], owns[-1])
            for (n, _, _), land, own in zip(mats, lands, owns):
                pieces[n].append((land, own))
            if smalls:
                shapes = [given[n].shape[1:] for n, _, _ in smalls]
                for (n, _, _), land, own in zip(smalls, _unpack_rows(lands[len(mats)], shapes, 1),
                                                _unpack_rows(owns[len(mats)], shapes, 0)):
                    pieces[n].append((land, own))
    mine = me.astype(jnp.int32).reshape(1)
    new = {}
    for n in SHARDED:
        lands = [l.reshape((N_DEV, -1, l.shape[-1])) for l, _ in pieces[n]]
        owns = [o.reshape((-1, o.shape[-1])) for _, o in pieces[n]]
        if owns[0].shape[0] % 8:
            lands = [jnp.stack(lands, axis=1).reshape((N_DEV, -1, lands[0].shape[-1]))]
            owns = [jnp.stack(owns).reshape((-1, owns[0].shape[-1]))]
        outs = _adam("adam_" + n, lands, owns, mine, _view2d(given[n]), _view2d(given["m_" + n]),
                     _view2d(given["v_" + n]))
        new[n] = [o.reshape(given[n].shape) for o in outs]
    packed = [_pack_rep({n: given[pre + n] for n in REPLICATED}) for pre in ("", "m_", "v_")]
    outs = _adam("adam_replicated", [rep_piece[0]], [rep_piece[1]], mine, *packed)
    unpacked = [_unpack_rep(o, {n: given[n].shape for n in REPLICATED}) for o in outs]
    for n in REPLICATED:
        new[n] = [u[n] for u in unpacked]
    return (loss, dx[None], *[new[n][0] for n in WEIGHTS], *[new[n][1] for n in WEIGHTS],
            *[new[n][2] for n in WEIGHTS], *[new[n][3] for n in WEIGHTS])
```

```python
import functools

import jax
import jax.numpy as jnp
import numpy as np
from jax import lax
from jax.experimental import pallas as pl
from jax.experimental.pallas import tpu as pltpu

F32 = jnp.float32
BF16 = jnp.bfloat16
HIGHEST = lax.Precision.HIGHEST
HIGH = lax.Precision.HIGH

N_DEV = 8
LANE = 128
EPS = 1e-6
DEPTH = 4
N_MIXERS = 3
HEADS = 8
DH = 128
D = HEADS * DH
D_FF = 2816
CONF_K, GDN_K, FFN_K = 31, 4, 3
GDN_CHUNK = 64
GDN_PAD = 4224
FOX_PAD = 3200
FOX_V_BLOCK = (2 * D + LANE) // DH
ADAM_LR, ADAM_B1, ADAM_B2, ADAM_EPS, ADAM_WD, ADAM_STEP = 0.001, 0.9, 0.999, 1e-08, 0.01, 10
VMEM_LIMIT = 56 * 1024 * 1024

WEIGHTS = ['mix_norm_g', 'ffn_norm_g', 'conv_w_in', 'conv_b_in', 'conv_w_dw', 'conv_b_dw', 'conv_ln_g', 'conv_ln_b',
           'conv_w_out', 'gdn_w_in', 'gdn_conv_w', 'gdn_a_log', 'gdn_dt_bias', 'gdn_o_norm_g', 'gdn_w_out', 'fox_w_in',
           'fox_b_f', 'fox_q_norm_g', 'fox_k_norm_g', 'fox_w_out', 'ffn_w_up', 'ffn_w_dw', 'ffn_w_down']
REPLICATED = ['mix_norm_g', 'ffn_norm_g', 'gdn_a_log', 'gdn_dt_bias', 'gdn_o_norm_g', 'fox_b_f', 'fox_q_norm_g',
              'fox_k_norm_g']
ROW_SHARDED = ['conv_w_out', 'gdn_w_out', 'fox_w_out', 'ffn_w_down']
MATRICES = ['conv_w_in', 'conv_w_out', 'gdn_w_in', 'gdn_w_out', 'fox_w_in', 'fox_w_out', 'ffn_w_up', 'ffn_w_down']
SHARDED = [n for n in WEIGHTS if n not in REPLICATED]


def _params(*sem):
    return pltpu.CompilerParams(dimension_semantics=sem, vmem_limit_bytes=VMEM_LIMIT)


def _tile(n, cap):
    if n <= cap:
        return n
    d = (cap // LANE) * LANE
    while d >= LANE:
        if n % d == 0:
            return d
        d -= LANE
    raise ValueError(f"no lane-aligned tile of {n} under {cap}")


def _raw_dot(a, b, ca, cb, hp):
    batch = ((0,), (0,)) if a.ndim == 3 else ((), ())
    dn = (((ca,), (cb,)), batch)
    if hp:
        return lax.dot_general(a.astype(F32), b.astype(F32), dn, precision=HIGH, preferred_element_type=F32)
    return lax.dot_general(a.astype(BF16), b.astype(BF16), dn, preferred_element_type=F32)


def _raw_nn(a, b, hp=False):
    return _raw_dot(a, b, a.ndim - 1, b.ndim - 2, hp)


def _raw_nt(a, b, hp=False):
    return _raw_dot(a, b, a.ndim - 1, b.ndim - 1, hp)


def _raw_tn(a, b, hp=False):
    return _raw_dot(a, b, a.ndim - 2, b.ndim - 2, hp)


@functools.partial(jax.custom_vjp, nondiff_argnums=(2,))
def _nn(a, b, hp):
    return _raw_nn(a, b, hp)


def _nn_fwd(a, b, hp):
    return _raw_nn(a, b, hp), (a, b)


def _nn_bwd(hp, res, g):
    a, b = res
    return _raw_nt(g, b, hp), _raw_tn(a, g, hp)


_nn.defvjp(_nn_fwd, _nn_bwd)


@functools.partial(jax.custom_vjp, nondiff_argnums=(2,))
def _nt(a, b, hp):
    return _raw_nt(a, b, hp)


def _nt_fwd(a, b, hp):
    return _raw_nt(a, b, hp), (a, b)


def _nt_bwd(hp, res, g):
    a, b = res
    return _raw_nn(g, b, hp), _raw_tn(g, a, hp)


_nt.defvjp(_nt_fwd, _nt_bwd)


def _shift_down(x, s):
    if s == 0:
        return x
    t = lax.broadcasted_iota(jnp.int32, x.shape, 0)
    return jnp.where(t >= s, pltpu.roll(x, s, axis=0), 0.0)


def _shift_up(x, s):
    if s == 0:
        return x
    n = x.shape[0]
    t = lax.broadcasted_iota(jnp.int32, x.shape, 0)
    return jnp.where(t < n - s, pltpu.roll(x, n - s, axis=0), 0.0)


def _row(w, k):
    r = lax.broadcasted_iota(jnp.int32, w.shape, 0)
    return jnp.sum(jnp.where(r == k, w, 0.0), axis=0, keepdims=True)


@jax.custom_vjp
def _dwconv(x, w):
    taps = w.shape[0]
    y = _row(w, taps - 1) * x
    for k in range(taps - 1):
        y = y + _row(w, k) * _shift_down(x, taps - 1 - k)
    return y


def _dwconv_fwd(x, w):
    return _dwconv(x, w), (x, w)


def _dwconv_bwd(res, dy):
    x, w = res
    taps = w.shape[0]
    r = lax.broadcasted_iota(jnp.int32, w.shape, 0)
    dx = _row(w, taps - 1) * dy
    dw = jnp.where(r == taps - 1, jnp.sum(dy * x, axis=0, keepdims=True), 0.0)
    for k in range(taps - 1):
        s = taps - 1 - k
        dx = dx + _row(w, k) * _shift_up(dy, s)
        dw = dw + jnp.where(r == k, jnp.sum(dy * _shift_down(x, s), axis=0, keepdims=True), 0.0)
    return dx, dw


_dwconv.defvjp(_dwconv_fwd, _dwconv_bwd)


def _sigmoid(x):
    return 1.0 / (1.0 + jnp.exp(-x))


def _silu(x):
    return x * _sigmoid(x)


def _softplus(x):
    return jnp.maximum(x, 0.0) + jnp.log(1.0 + jnp.exp(-jnp.abs(x)))


def _head_scale(x, fn):
    tm = x.shape[0]
    x3 = x.reshape(tm, HEADS, DH)
    return (x3 * fn(jnp.sum(x3 * x3, axis=-1, keepdims=True))).reshape(tm, HEADS * DH)


def _tile_lanes(g):
    return jnp.concatenate([g] * HEADS, axis=1)


def _expand_heads(v, first):
    lane = lax.broadcasted_iota(jnp.int32, (LANE, HEADS * DH), 0)
    col = lax.broadcasted_iota(jnp.int32, (LANE, HEADS * DH), 1)
    sel = (lane == col // DH + first).astype(F32)
    return _nn(v, sel, True)


def _f_rms(x, g):
    return (x * lax.rsqrt(jnp.mean(x * x, axis=-1, keepdims=True) + EPS) * g,)


def _f_rms_res(x, g):
    return (_f_rms(x, g)[0], x)


def _f_conf_glu_conv(u, b, w):
    c = u.shape[1] // 2
    return (_dwconv((u[:, :c] + b[:, :c]) * _sigmoid(u[:, c:] + b[:, c:]), w),)


def _f_conf_ln_silu(cv, b_dw, ln_g, ln_b):
    u = cv + b_dw
    xc = u - jnp.mean(u, axis=-1, keepdims=True)
    y = xc * lax.rsqrt(jnp.mean(xc * xc, axis=-1, keepdims=True) + EPS) * ln_g + ln_b
    return (_silu(y),)


def _f_conv_silu(u, w):
    return (_silu(_dwconv(u, w)),)


def _f_gdn_gates(qk, ab, a_log, dt_bias):
    qn = _head_scale(qk[:, :D], lambda ss: lax.rsqrt(ss + EPS)) * (DH ** -0.5)
    kn = _head_scale(qk[:, D:], lambda ss: lax.rsqrt(ss + EPS))
    g = -jnp.exp(a_log) * _softplus(ab + dt_bias)
    beta = _sigmoid(ab)
    return qn, kn, _expand_heads(g, 0), _expand_heads(beta, HEADS)


def _f_gdn_prescan(q, k, v, gb, bb):
    c = GDN_CHUNK
    n = q.shape[0] // c
    r3 = lambda t: t.reshape(n, c, DH)
    q3, k3, v3, g3, b3 = r3(q), r3(k), r3(v), r3(gb), r3(bb)
    ii = lax.broadcasted_iota(jnp.int32, (n, c, c), 1)
    jj = lax.broadcasted_iota(jnp.int32, (n, c, c), 2)
    lower, strict = ii >= jj, ii > jj
    gcb = _nn(lower.astype(F32), g3, True)
    gi = gcb[:, :, :c]
    gj = jnp.swapaxes(gi, 1, 2)
    decay = jnp.where(lower, jnp.exp(jnp.where(lower, gi - gj, 0.0)), 0.0)
    kb, vb = k3 * b3, v3 * b3
    a_mat = jnp.where(strict, _nt(kb, k3, False) * decay, 0.0)
    p = -a_mat
    t_mat = (ii == jj).astype(F32) + p
    for _ in range(5):
        p = _nn(p, p, True)
        t_mat = t_mat + _nn(t_mat, p, True)
    eg = jnp.exp(gcb)
    u = _nn(t_mat, vb, False)
    w = _nn(t_mat, kb * eg, False)
    qk = jnp.where(lower, _nt(q3, k3, False) * decay, 0.0)
    qg = q3 * eg
    g_last = jnp.sum(g3, axis=1, keepdims=True)
    kd = k3 * jnp.exp(g_last - gcb)
    r2 = lambda t: t.reshape(n * c, DH)
    return r2(u), r2(w), r2(qg), qk, r2(kd), jnp.exp(g_last)


def _f_gdn_post(o, z, o_g):
    on = _head_scale(o, lambda ss: lax.rsqrt(ss / DH + EPS)) * _tile_lanes(o_g)
    return (on * _silu(z),)


def _f_fox_pre(qkf, q_g, k_g, b_f):
    qn = _head_scale(qkf[:, :D], lambda ss: lax.rsqrt(ss / DH + EPS)) * _tile_lanes(q_g)
    kn = _head_scale(qkf[:, D:2 * D], lambda ss: lax.rsqrt(ss / DH + EPS)) * _tile_lanes(k_g)
    return qn, kn, -_softplus(-(qkf[:, 2 * D:] + b_f))


def _f_ffn_mid(u, w):
    c = u.shape[1] // 2
    return (_silu(_dwconv(u[:, :c], w[:, :c])) * _dwconv(u[:, c:], w[:, c:]),)


def _seg_fwd(name, f, grid, ins, in_specs, out_shapes, out_specs):
    n_in = len(ins)

    def body(*refs):
        outs = f(*[r[...].astype(F32) for r in refs[:n_in]])
        for r, o in zip(refs[n_in:], outs):
            r[...] = o.astype(r.dtype)

    return pl.pallas_call(body, grid=grid, in_specs=in_specs, out_specs=out_specs, out_shape=out_shapes, name=name,
                          compiler_params=_params(*(["parallel"] * len(grid))))(*ins)


def _seg_bwd(name, f, grid, ins, in_specs, douts, dout_specs, want, into=None):
    into = into or {}
    n_in, n_dy = len(ins), len(douts)
    diff = [i for i, w in enumerate(want) if w is not None]
    kept = [i for i in diff if i in into]
    out_shapes = [jax.ShapeDtypeStruct(ins[i].shape, into[i].dtype if i in into else F32 if want[i] == "acc" else want[i])
                  for i in diff]
    out_specs = [in_specs[i] for i in diff]
    acc_axis = len(grid) - 1

    def body(*refs):
        vals = [r[...].astype(F32) for r in refs[:n_in]]
        dys = [r[...].astype(F32) for r in refs[n_in:n_in + n_dy]]
        out_refs = refs[n_in + n_dy + len(kept):]

        def g(*dv):
            full = list(vals)
            for i, v in zip(diff, dv):
                full[i] = v
            return f(*full)

        _, vjp = jax.vjp(g, *[vals[i] for i in diff])
        grads = vjp(tuple(dys))
        first = pl.program_id(acc_axis) == 0
        for i, r, gr in zip(diff, out_refs, grads):
            if want[i] == "acc":
                @pl.when(first)
                def _(r=r, gr=gr):
                    r[...] = gr

                @pl.when(jnp.logical_not(first))
                def _(r=r, gr=gr):
                    r[...] += gr
            else:
                r[...] = gr.astype(r.dtype)

    sem = ["parallel"] * (len(grid) - 1) + ["arbitrary"]
    untouched = [pl.BlockSpec(memory_space=pl.ANY)] * len(kept)
    aliases = {n_in + n_dy + e: diff.index(i) for e, i in enumerate(kept)}
    return pl.pallas_call(body, grid=grid, in_specs=list(in_specs) + list(dout_specs) + untouched, out_specs=out_specs,
                          out_shape=out_shapes, input_output_aliases=aliases, name=name,
                          compiler_params=_params(*sem))(*ins, *douts, *[into[i] for i in kept])


def _rows(tm, width, col=0):
    return pl.BlockSpec((tm, width), lambda i, col=col: (i, col))


def _const(shape):
    return pl.BlockSpec(shape, lambda i: (0,) * len(shape))


def _cols(t, tc, off=0):
    return pl.BlockSpec((t, tc), lambda j, off=off: (0, j + off))


def _grid2(specs):
    return [pl.BlockSpec(s.block_shape, lambda j, i, f=s.index_map: f(j)) for s in specs]


def _mm(name, a, b, *, ta=False, tb=False, res=None, out_dtype=F32, after=None):
    k_dim, m = (a.shape[0], a.shape[1]) if ta else (a.shape[1], a.shape[0])
    n = b.shape[0] if tb else b.shape[1]
    tm, tn, tk = _tile(m, 1408), _tile(n, 1408), _tile(k_dim, 1408)
    nk = k_dim // tk
    grid = (m // tm, n // tn, nk)
    a_spec = pl.BlockSpec((tk, tm), lambda i, j, k: (k, i)) if ta else pl.BlockSpec((tm, tk), lambda i, j, k: (i, k))
    b_spec = pl.BlockSpec((tn, tk), lambda i, j, k: (j, k)) if tb else pl.BlockSpec((tk, tn), lambda i, j, k: (k, j))
    o_spec = pl.BlockSpec((tm, tn), lambda i, j, k: (i, j))
    dn = (((0 if ta else 1,), (1 if tb else 0,)), ((), ()))
    has_res = res is not None

    def body(*refs):
        a_ref, b_ref = refs[0], refs[1]
        res_ref = refs[2] if has_res else None
        o_ref = refs[n_in]
        p = lax.dot_general(a_ref[...].astype(BF16), b_ref[...].astype(BF16), dn, preferred_element_type=F32)

        def write(acc):
            if has_res:
                acc = acc + res_ref[...]
            o_ref[...] = acc.astype(o_ref.dtype)

        if nk == 1:
            write(p)
        else:
            acc_ref = refs[-1]
            k = pl.program_id(2)

            @pl.when(k == 0)
            def _():
                acc_ref[...] = p

            @pl.when(k > 0)
            def _():
                acc_ref[...] += p

            @pl.when(k == nk - 1)
            def _():
                write(acc_ref[...])

    ins, specs = [a, b], [a_spec, b_spec]
    if has_res:
        ins.append(res)
        specs.append(o_spec)
    if after is not None:
        ins.append(after)
        specs.append(pl.BlockSpec(memory_space=pl.ANY))
    n_in = len(ins)
    scratch = [pltpu.VMEM((tm, tn), F32)] if nk > 1 else []
    return pl.pallas_call(body, grid=grid, in_specs=specs, out_specs=o_spec, scratch_shapes=scratch,
                          out_shape=jax.ShapeDtypeStruct((m, n), out_dtype), name=name,
                          compiler_params=_params("parallel", "parallel", "arbitrary"))(*ins)


def _head_specs(t):
    n = t // GDN_CHUNK
    col = pl.BlockSpec((t, DH), lambda h: (0, h))
    qk = pl.BlockSpec((None, n, GDN_CHUNK, GDN_CHUNK), lambda h: (h, 0, 0, 0))
    gl = pl.BlockSpec((None, n, 1, DH), lambda h: (h, 0, 0, 0))
    st = pl.BlockSpec((None, n, DH, DH), lambda h: (h, 0, 0, 0))
    return n, col, qk, gl, st


def _gdn_scan_fwd(name, u, w, qg, qk, kd, gl):
    t = u.shape[0]
    n, col, qk_spec, gl_spec, st_spec = _head_specs(t)
    c = GDN_CHUNK

    def body(u_ref, w_ref, qg_ref, qk_ref, kd_ref, gl_ref, o_ref, s_ref):
        def step(i, s):
            rows = pl.ds(pl.multiple_of(i * c, c), c)
            s_ref[i] = s
            vn = u_ref[rows, :] - _raw_nn(w_ref[rows, :], s)
            o_ref[rows, :] = _raw_nn(qg_ref[rows, :], s) + _raw_nn(qk_ref[i], vn)
            return s * gl_ref[i] + _raw_tn(kd_ref[rows, :], vn)

        lax.fori_loop(0, n, step, jnp.zeros((DH, DH), F32))

    return pl.pallas_call(
        body, grid=(HEADS,), in_specs=[col, col, col, qk_spec, col, gl_spec], out_specs=[col, st_spec],
        out_shape=[jax.ShapeDtypeStruct((t, D), F32), jax.ShapeDtypeStruct((HEADS, n, DH, DH), F32)], name=name,
        compiler_params=_params("parallel"))(u, w, qg, qk, kd, gl)


def _gdn_scan_bwd(name, u, w, qg, qk, kd, gl, states, do):
    t = u.shape[0]
    n, col, qk_spec, gl_spec, st_spec = _head_specs(t)
    c = GDN_CHUNK

    def body(u_ref, w_ref, qg_ref, qk_ref, kd_ref, gl_ref, s_ref, do_ref,
             du_ref, dw_ref, dqg_ref, dqk_ref, dkd_ref, dgl_ref):
        def step(r, ds):
            i = n - 1 - r
            rows = pl.ds(pl.multiple_of(i * c, c), c)
            s, do_c, w_c = s_ref[i], do_ref[rows, :], w_ref[rows, :]
            vn = u_ref[rows, :] - _raw_nn(w_c, s)
            dvn = _raw_tn(qk_ref[i], do_c) + _raw_nn(kd_ref[rows, :], ds)
            du_ref[rows, :] = dvn
            dw_ref[rows, :] = -_raw_nt(dvn, s)
            dqg_ref[rows, :] = _raw_nt(do_c, s)
            dqk_ref[i] = _raw_nt(do_c, vn)
            dkd_ref[rows, :] = _raw_nt(vn, ds)
            dgl_ref[i] = jnp.sum(ds * s, axis=0, keepdims=True)
            return _raw_tn(qg_ref[rows, :], do_c) + ds * gl_ref[i] - _raw_tn(w_c, dvn)

        lax.fori_loop(0, n, step, jnp.zeros((DH, DH), F32))

    big = jax.ShapeDtypeStruct((t, D), F32)
    return pl.pallas_call(
        body, grid=(HEADS,), in_specs=[col, col, col, qk_spec, col, gl_spec, st_spec, col],
        out_specs=[col, col, col, qk_spec, col, gl_spec],
        out_shape=[big, big, big, jax.ShapeDtypeStruct(qk.shape, F32), big, jax.ShapeDtypeStruct(gl.shape, F32)],
        name=name, compiler_params=_params("parallel"))(u, w, qg, qk, kd, gl, states, do)


def _cumsum_rows(name, parts, reverse):
    t = parts[0].shape[0]
    blk = min(t, 256)
    nb = t // blk
    n_in = len(parts)

    def body(*refs):
        o_ref = refs[n_in]
        ii = lax.broadcasted_iota(jnp.int32, (blk, blk), 0)
        jj = lax.broadcasted_iota(jnp.int32, (blk, blk), 1)
        tri = ((ii <= jj) if reverse else (ii >= jj)).astype(F32)
        carry = jnp.zeros((1, LANE), F32)
        for b in (range(nb - 1, -1, -1) if reverse else range(nb)):
            rows = pl.ds(b * blk, blk)
            x = refs[0][rows, :]
            for r in refs[1:n_in]:
                x = x + r[rows, :]
            o_ref[rows, :] = lax.dot_general(tri, x, (((1,), (0,)), ((), ())), precision=HIGHEST,
                                             preferred_element_type=F32) + carry
            carry = carry + jnp.sum(x, axis=0, keepdims=True)

    return pl.pallas_call(body, out_shape=jax.ShapeDtypeStruct((t, LANE), F32), name=name,
                          compiler_params=_params())(*parts)


def _fox_blocks(t):
    blk = 256 if t % 256 == 0 and t >= 1024 else 128
    return blk, t // blk


def _fox_logits(q, k, cq, ck, row0, col0):
    s = _raw_nt(q, k) * (DH ** -0.5) + cq - ck
    rows = row0 + lax.broadcasted_iota(jnp.int32, s.shape, 0)
    cols = col0 + lax.broadcasted_iota(jnp.int32, s.shape, 1)
    return jnp.where(cols <= rows, s, -jnp.inf)


def _fox_fwd(name, qn, kn, proj, c_col, c_row):
    t = qn.shape[0]
    blk, nb = _fox_blocks(t)
    voff = FOX_V_BLOCK

    def body(q_ref, k_ref, v_ref, cc_ref, cr_ref, o_ref, lse_ref):
        i = pl.program_id(1)
        q, cq = q_ref[...], cc_ref[...]

        def step(j, carry):
            m, l, acc = carry
            rows = pl.ds(pl.multiple_of(j * blk, blk), blk)
            s = _fox_logits(q, k_ref[rows, :], cq, cr_ref[j], i * blk, j * blk)
            m_new = jnp.maximum(m, jnp.max(s, axis=1, keepdims=True))
            p = jnp.exp(s - m_new)
            alpha = jnp.exp(m - m_new)
            return m_new, alpha * l + jnp.sum(p, axis=1, keepdims=True), alpha * acc + _raw_nn(p, v_ref[rows, :])

        init = (jnp.full((blk, 1), -1e30, F32), jnp.zeros((blk, 1), F32), jnp.zeros((blk, DH), F32))
        m, l, acc = lax.fori_loop(0, i + 1, step, init)
        o_ref[...] = acc / l
        lse_ref[...] = m + jnp.log(l)

    tile = pl.BlockSpec((blk, DH), lambda h, i: (i, h))
    colv = pl.BlockSpec((None, blk, 1), lambda h, i: (h, i, 0))
    return pl.pallas_call(
        body, grid=(HEADS, nb),
        in_specs=[tile, pl.BlockSpec((t, DH), lambda h, i: (0, h)), pl.BlockSpec((t, DH), lambda h, i: (0, voff + h)),
                  colv, pl.BlockSpec((None, nb, 1, blk), lambda h, i: (h, 0, 0, 0))],
        out_specs=[tile, colv],
        out_shape=[jax.ShapeDtypeStruct((t, D), F32), jax.ShapeDtypeStruct((HEADS, t, 1), F32)], name=name,
        compiler_params=_params("parallel", "parallel"))(qn, kn, proj, c_col, c_row)


def _fox_dq(name, qn, kn, proj, c_col, c_row, o, lse, do):
    t = qn.shape[0]
    blk, nb = _fox_blocks(t)
    voff = FOX_V_BLOCK

    def body(q_ref, k_ref, v_ref, cc_ref, cr_ref, o_ref, lse_ref, do_ref, dq_ref, dcc_ref, dl_ref):
        i = pl.program_id(1)
        q, cq, do_b, lse_b = q_ref[...], cc_ref[...], do_ref[...], lse_ref[...]
        delta = jnp.sum(do_b * o_ref[...], axis=1, keepdims=True)

        def step(j, carry):
            dq, dcc = carry
            rows = pl.ds(pl.multiple_of(j * blk, blk), blk)
            k = k_ref[rows, :]
            p = jnp.exp(_fox_logits(q, k, cq, cr_ref[j], i * blk, j * blk) - lse_b)
            ds = p * (_raw_nt(do_b, v_ref[rows, :]) - delta)
            return dq + _raw_nn(ds, k), dcc + jnp.sum(ds, axis=1, keepdims=True)

        dq, dcc = lax.fori_loop(0, i + 1, step, (jnp.zeros((blk, DH), F32), jnp.zeros((blk, 1), F32)))
        dq_ref[...] = dq * (DH ** -0.5)
        dcc_ref[...] = dcc
        dl_ref[...] = delta

    tile = pl.BlockSpec((blk, DH), lambda h, i: (i, h))
    colv = pl.BlockSpec((None, blk, 1), lambda h, i: (h, i, 0))
    vec = jax.ShapeDtypeStruct((HEADS, t, 1), F32)
    return pl.pallas_call(
        body, grid=(HEADS, nb),
        in_specs=[tile, pl.BlockSpec((t, DH), lambda h, i: (0, h)), pl.BlockSpec((t, DH), lambda h, i: (0, voff + h)),
                  colv, pl.BlockSpec((None, nb, 1, blk), lambda h, i: (h, 0, 0, 0)), tile, colv, tile],
        out_specs=[tile, colv, colv],
        out_shape=[jax.ShapeDtypeStruct((t, D), F32), vec, vec], name=name,
        compiler_params=_params("parallel", "parallel"))(qn, kn, proj, c_col, c_row, o, lse, do)


def _fox_dkv(name, qn, kn, proj, c_col, c_row, lse, delta, do):
    t = qn.shape[0]
    blk, nb = _fox_blocks(t)
    voff = FOX_V_BLOCK

    def body(q_ref, k_ref, v_ref, cc_ref, cr_ref, lse_ref, dl_ref, do_ref, dk_ref, dv_ref, dcr_ref):
        j = pl.program_id(1)
        k, v, ck = k_ref[...], v_ref[...], cr_ref[...]

        def step(i, carry):
            dk, dv, dcr = carry
            rows = pl.ds(pl.multiple_of(i * blk, blk), blk)
            q, do_b = q_ref[rows, :], do_ref[rows, :]
            p = jnp.exp(_fox_logits(q, k, cc_ref[rows, :], ck, i * blk, j * blk) - lse_ref[rows, :])
            ds = p * (_raw_nt(do_b, v) - dl_ref[rows, :])
            return dk + _raw_tn(ds, q), dv + _raw_tn(p, do_b), dcr - jnp.sum(ds, axis=0, keepdims=True)

        init = (jnp.zeros((blk, DH), F32), jnp.zeros((blk, DH), F32), jnp.zeros((1, blk), F32))
        dk, dv, dcr = lax.fori_loop(j, nb, step, init)
        dk_ref[...] = dk * (DH ** -0.5)
        dv_ref[...] = dv.astype(dv_ref.dtype)
        dcr_ref[...] = dcr

    full = pl.BlockSpec((t, DH), lambda h, j: (0, h))
    colf = pl.BlockSpec((None, t, 1), lambda h, j: (h, 0, 0))
    tile = pl.BlockSpec((blk, DH), lambda h, j: (j, h))
    rowv = pl.BlockSpec((None, None, 1, blk), lambda h, j: (h, j, 0, 0))
    tile_v = pl.BlockSpec((blk, DH), lambda h, j: (j, voff + h))
    big = jax.ShapeDtypeStruct((t, D), F32)
    return pl.pallas_call(
        body, grid=(HEADS, nb),
        in_specs=[full, tile, tile_v, colf, rowv, colf, colf, full],
        out_specs=[tile, tile_v, rowv],
        out_shape=[big, jax.ShapeDtypeStruct(proj.shape, BF16), jax.ShapeDtypeStruct((HEADS, nb, 1, blk), F32)], name=name,
        compiler_params=_params("parallel", "parallel"))(qn, kn, proj, c_col, c_row, lse, delta, do)


def _loss_head(y, target):
    t = y.shape[0]
    tm = min(t, 512)

    def body(y_ref, t_ref, dy_ref, sum_ref):
        err = y_ref[...] - t_ref[...]
        dy_ref[...] = err * (1.0 / D)
        part = jnp.sum(jnp.sum(err * err, axis=1, keepdims=True), axis=0, keepdims=True)

        @pl.when(pl.program_id(0) == 0)
        def _():
            sum_ref[...] = jnp.zeros_like(sum_ref)

        sum_ref[...] += jnp.broadcast_to(part, sum_ref.shape)

    return pl.pallas_call(
        body, grid=(t // tm,), in_specs=[_rows(tm, D), _rows(tm, D)], out_specs=[_rows(tm, D), _const((1, LANE))],
        out_shape=[jax.ShapeDtypeStruct((t, D), F32), jax.ShapeDtypeStruct((1, LANE), F32)], name="loss_head",
        compiler_params=_params("arbitrary"))(y, target)


ADAM_BLOCK_BYTES = 3 * 1024 * 1024


def _adam(name, lands, owns, mine, w, m, v):
    layers = len(lands)
    r, c = owns[0].shape
    block_bytes = lambda rows: N_DEV * rows * c * lands[0].dtype.itemsize
    tr = r
    for cand in (512, 256, 128, 64, 32, 16):
        if block_bytes(tr) > ADAM_BLOCK_BYTES and r % cand == 0:
            tr = cand
    nr = r // tr

    def body(*refs):
        me_ref, land_refs, own_refs = refs[0], refs[1:1 + layers], refs[1 + layers:1 + 2 * layers]
        w_ref, m_ref, v_ref, g_ref, d_ref, nm_ref, nv_ref = refs[1 + 2 * layers:]
        layer = pl.program_id(0)
        for l in range(layers):
            @pl.when(layer == l)
            def _(l=l):
                g = jnp.zeros((tr, c), F32)
                for s in range(N_DEV):
                    g = g + jnp.where(me_ref[0] == s, own_refs[l][...], land_refs[l][s]).astype(F32)
                nm = ADAM_B1 * m_ref[...] + (1.0 - ADAM_B1) * g
                nv = ADAM_B2 * v_ref[...] + (1.0 - ADAM_B2) * (g * g)
                m_hat = nm / (1.0 - ADAM_B1 ** ADAM_STEP)
                v_hat = nv / (1.0 - ADAM_B2 ** ADAM_STEP)
                g_ref[...] = g
                d_ref[...] = -ADAM_LR * (m_hat / (jnp.sqrt(v_hat) + ADAM_EPS) + ADAM_WD * w_ref[...])
                nm_ref[...] = nm
                nv_ref[...] = nv

    at = lambda l: (lambda layer, i: jnp.where(layer == l, i, 0))
    land_specs = [pl.BlockSpec((N_DEV, tr, c), lambda layer, i, f=at(l): (0, f(layer, i), 0)) for l in range(layers)]
    own_specs = [pl.BlockSpec((tr, c), lambda layer, i, f=at(l): (f(layer, i), 0)) for l in range(layers)]
    blk = pl.BlockSpec((tr, c), lambda layer, i: (layer * nr + i, 0))
    out = jax.ShapeDtypeStruct(w.shape, F32)
    return pl.pallas_call(
        body, grid=(layers, nr),
        in_specs=[pl.BlockSpec(memory_space=pltpu.SMEM)] + land_specs + own_specs + [blk, blk, blk],
        out_specs=[blk] * 4, out_shape=[out] * 4, name=name,
        compiler_params=_params("arbitrary", "arbitrary"))(mine, *lands, *owns, w, m, v)


_HBM = pl.BlockSpec(memory_space=pltpu.HBM)
_SEM = pl.BlockSpec(memory_space=pltpu.SEMAPHORE)
_EFFECT = pltpu.SideEffectType.DATAFLOW_SIDE_EFFECTING


def _each_peer(x, y, c):
    flip = lambda v, bit: 1 - v if bit else v
    for p in range(1, N_DEV):
        px, py, pc = flip(x, p & 4), flip(y, p & 2), flip(c, p & 1)
        yield p, (px, py, pc), 4 * px + 2 * py + pc


def _sem(a, p):
    return a * (N_DEV - 1) + p - 1


def _exchange_start(name, arrays, scatter):
    n = len(arrays)
    lands = [lax.empty((N_DEV,) + (a.shape[1:] if sc else a.shape), a.dtype) for a, sc in zip(arrays, scatter)]

    def body(*refs):
        in_refs, land_refs = refs[:n], refs[n:2 * n]
        send_sems, recv_sems, token = refs[2 * n], refs[2 * n + 1], refs[-1]
        x, y, c = lax.axis_index("x"), lax.axis_index("y"), lax.axis_index("c")
        me = 4 * x + 2 * y + c
        for p, coords, peer in _each_peer(x, y, c):
            for a in range(n):
                pltpu.make_async_remote_copy(
                    src_ref=in_refs[a].at[peer] if scatter[a] else in_refs[a], dst_ref=land_refs[a].at[me],
                    send_sem=send_sems.at[_sem(a, p)], recv_sem=recv_sems.at[_sem(a, p)], device_id=coords,
                    device_id_type=pl.DeviceIdType.MESH).start()
        token[...] = jnp.zeros_like(token)

    sems = pltpu.SemaphoreType.DMA((n * (N_DEV - 1),))
    hbm = lambda a: pltpu.HBM(a.shape, a.dtype)
    out = pl.pallas_call(
        body, name=name,
        out_shape=(sems, sems, *[hbm(a) for a in arrays], *[hbm(l) for l in lands],
                   jax.ShapeDtypeStruct((8, LANE), F32)),
        in_specs=[_HBM] * (2 * n), out_specs=(_SEM, _SEM, *[_HBM] * (2 * n), pl.BlockSpec(memory_space=pltpu.VMEM)),
        input_output_aliases={i: 2 + i for i in range(2 * n)},
        compiler_params=pltpu.CompilerParams(has_side_effects=_EFFECT),
    )(*[pltpu.with_memory_space_constraint(a, pltpu.HBM) for a in arrays],
      *[pltpu.with_memory_space_constraint(l, pltpu.HBM) for l in lands])
    return (out[0], out[1], list(out[2:2 + n]), list(out[2 + n:2 + 2 * n]), scatter), out[-1]


def _exchange_wait(name, started, after, me):
    send_sems, recv_sems, sent, lands, scatter = started
    n = len(sent)

    def body(*refs):
        in_refs, land_refs = refs[:n], refs[n:2 * n]
        send_sems, recv_sems = refs[2 * n], refs[2 * n + 1]
        x, y, c = lax.axis_index("x"), lax.axis_index("y"), lax.axis_index("c")
        for p, coords, peer in _each_peer(x, y, c):
            for a in range(n):
                cp = pltpu.make_async_remote_copy(
                    src_ref=in_refs[a].at[peer] if scatter[a] else in_refs[a], dst_ref=land_refs[a].at[peer],
                    send_sem=send_sems.at[_sem(a, p)], recv_sem=recv_sems.at[_sem(a, p)], device_id=coords,
                    device_id_type=pl.DeviceIdType.MESH)
                cp.wait_send()
                cp.wait_recv()

    hbm = lambda a: pltpu.HBM(a.shape, a.dtype)
    out = pl.pallas_call(
        body, name=name, out_shape=(*[hbm(a) for a in sent], *[hbm(l) for l in lands]),
        in_specs=[_HBM] * (2 * n) + [_SEM, _SEM, pl.BlockSpec(memory_space=pl.ANY)], out_specs=[_HBM] * (2 * n),
        input_output_aliases={i: i for i in range(2 * n)},
        compiler_params=pltpu.CompilerParams(has_side_effects=_EFFECT),
    )(*sent, *lands, send_sems, recv_sems, after)
    owns = [lax.dynamic_index_in_dim(out[a], me, 0, keepdims=False) if scatter[a] else out[a] for a in range(n)]
    return list(out[n:]), owns


def _fill_own(land, own, me):
    slot = lax.broadcasted_iota(jnp.int32, (N_DEV,) + (1,) * own.ndim, 0)
    return jnp.where(slot == me, own[None], land)


def _rms_fwd(name, x, g):
    t = x.shape[0]
    tm = min(t, 512)
    return _seg_fwd(name, _f_rms, (t // tm,), [x, g], [_rows(tm, D), _const((1, D))],
                    [jax.ShapeDtypeStruct((t, D), BF16)], [_rows(tm, D)])[0]


def _rms_bwd(name, x, g, dh, dres):
    t = x.shape[0]
    tm = min(t, 512)
    return _seg_bwd(name, _f_rms_res, (t // tm,), [x, g], [_rows(tm, D), _const((1, D))],
                    [dh, dres], [_rows(tm, D), _rows(tm, D)], [F32, "acc"])


FFN_TC = 256
CONF_TC = 128


def _ffn_specs(t):
    nf = D_FF // FFN_TC
    return nf, [_cols(t, 2 * FFN_TC), _cols(FFN_K, 2 * FFN_TC)], [_cols(t, FFN_TC)]


def _ffn_fwd(tag, x, p):
    t = x.shape[0]
    nf, in_specs, out_specs = _ffn_specs(t)
    h = _rms_fwd(tag + "_rms", x, p["g"])
    u0 = _mm(tag + "_up", h, p["w_up"])
    act = _seg_fwd(tag + "_mid", _f_ffn_mid, (nf,), [u0, p["w_dw"]], in_specs,
                   [jax.ShapeDtypeStruct((t, D_FF), BF16)], out_specs)[0]
    out = _mm(tag + "_down", act, p["w_down"], res=x)
    return out, (x, h, u0, act)


def _ffn_bwd(tag, saved, p, dout, after=None):
    x, h, u0, act = saved
    t = x.shape[0]
    nf, in_specs, out_specs = _ffn_specs(t)
    g = {"w_down": _mm(tag + "_dwdown", act, dout, ta=True, out_dtype=BF16, after=after)}
    dact = _mm(tag + "_dact", dout, p["w_down"], tb=True, out_dtype=BF16)
    du0, g["w_dw"] = _seg_bwd(tag + "_dmid", _f_ffn_mid, (nf, 1), [u0, p["w_dw"]], _grid2(in_specs), [dact],
                              _grid2(out_specs), [BF16, "acc"])
    g["w_up"] = _mm(tag + "_dwup", h, du0, ta=True, out_dtype=BF16)
    dh = _mm(tag + "_dh", du0, p["w_up"], tb=True, out_dtype=BF16)
    dx, g["g"] = _rms_bwd(tag + "_drms", x, p["g"], dh, dout)
    return dx, g


def _conf_specs(t):
    tm = min(t, 512)
    nc = D // CONF_TC
    vec = _const((1, D))
    glu_in = [_cols(t, 2 * CONF_TC), _cols(1, 2 * CONF_TC), _cols(CONF_K, CONF_TC)]
    return tm, nc, glu_in, [_cols(t, CONF_TC)], [_rows(tm, D), vec, vec, vec]


def _conf_fwd(tag, x, p):
    t = x.shape[0]
    tm, nc, glu_in, glu_out, ln_in = _conf_specs(t)
    h = _rms_fwd(tag + "_rms", x, p["g"])
    u = _mm(tag + "_in", h, p["w_in"])
    cv = _seg_fwd(tag + "_gluconv", _f_conf_glu_conv, (nc,), [u, p["b_in"], p["w_dw"]], glu_in,
                  [jax.ShapeDtypeStruct((t, D), F32)], glu_out)[0]
    act = _seg_fwd(tag + "_lnsilu", _f_conf_ln_silu, (t // tm,), [cv, p["b_dw"], p["ln_g"], p["ln_b"]], ln_in,
                   [jax.ShapeDtypeStruct((t, D), BF16)], [_rows(tm, D)])[0]
    out = _mm(tag + "_out", act, p["w_out"], res=x)
    return out, (x, h, u, cv, act)


def _conf_bwd(tag, saved, p, dout, after=None):
    x, h, u, cv, act = saved
    t = x.shape[0]
    tm, nc, glu_in, glu_out, ln_in = _conf_specs(t)
    g = {"w_out": _mm(tag + "_dwout", act, dout, ta=True, out_dtype=BF16, after=after)}
    dact = _mm(tag + "_dact", dout, p["w_out"], tb=True, out_dtype=BF16)
    dcv, g["b_dw"], g["ln_g"], g["ln_b"] = _seg_bwd(
        tag + "_dlnsilu", _f_conf_ln_silu, (t // tm,), [cv, p["b_dw"], p["ln_g"], p["ln_b"]], ln_in, [dact],
        [_rows(tm, D)], [F32, "acc", "acc", "acc"])
    du, g["b_in"], g["w_dw"] = _seg_bwd(tag + "_dgluconv", _f_conf_glu_conv, (nc, 1), [u, p["b_in"], p["w_dw"]],
                                        _grid2(glu_in), [dcv], _grid2(glu_out), [BF16, "acc", "acc"])
    g["w_in"] = _mm(tag + "_dwin", h, du, ta=True, out_dtype=BF16)
    dh = _mm(tag + "_dh", du, p["w_in"], tb=True, out_dtype=BF16)
    dx, g["g"] = _rms_bwd(tag + "_drms", x, p["g"], dh, dout)
    return dx, g


def _gdn_specs(t):
    tc, tm, rows = 256, min(t, 256), min(t, 512)
    nq = 3 * D // tc
    conv = ([_cols(t, tc), _cols(GDN_K, tc)], [_cols(t, tc)])
    gate_in = [_rows(tm, 2 * D, 0), _rows(tm, LANE, 4 * D // LANE), _const((1, LANE)), _const((1, LANE))]
    gate_out = [_rows(tm, D)] * 4
    head = pl.BlockSpec((rows, DH), lambda h, i: (i, h))
    headv = pl.BlockSpec((rows, DH), lambda h, i: (i, 2 * HEADS + h))
    nch = rows // GDN_CHUNK
    pre_in = [head, head, headv, head, head]
    pre_out = [head, head, head, pl.BlockSpec((None, nch, GDN_CHUNK, GDN_CHUNK), lambda h, i: (h, i, 0, 0)), head,
               pl.BlockSpec((None, nch, 1, DH), lambda h, i: (h, i, 0, 0))]
    post_in = [_rows(tm, D), _rows(tm, D, 3), _const((1, DH))]
    return tm, rows, nq, conv, gate_in, gate_out, pre_in, pre_out, post_in


def _gdn_fwd(tag, x, p):
    t = x.shape[0]
    tm, rows, nq, conv, gate_in, gate_out, pre_in, pre_out, post_in = _gdn_specs(t)
    n = t // GDN_CHUNK
    big = jax.ShapeDtypeStruct((t, D), F32)
    h = _rms_fwd(tag + "_rms", x, p["g"])
    proj = _mm(tag + "_in", h, p["w_in"])
    qkv = _seg_fwd(tag + "_conv", _f_conv_silu, (nq,), [proj, p["conv_w"]], conv[0],
                   [jax.ShapeDtypeStruct((t, 3 * D), F32)], conv[1])[0]
    gate_ins = [qkv, proj, p["a_log"], p["dt_bias"]]
    qn, kn, gb, bb = _seg_fwd(tag + "_gates", _f_gdn_gates, (t // tm,), gate_ins, gate_in, [big] * 4, gate_out)
    pre_ins = [qn, kn, qkv, gb, bb]
    pre_shapes = [big, big, big, jax.ShapeDtypeStruct((HEADS, n, GDN_CHUNK, GDN_CHUNK), F32), big,
                  jax.ShapeDtypeStruct((HEADS, n, 1, DH), F32)]
    pre = _seg_fwd(tag + "_prescan", _f_gdn_prescan, (HEADS, t // rows), pre_ins, pre_in, pre_shapes, pre_out)
    o, states = _gdn_scan_fwd(tag + "_scan", *pre)
    post_ins = [o, proj, p["o_g"]]
    act = _seg_fwd(tag + "_post", _f_gdn_post, (t // tm,), post_ins, post_in, [jax.ShapeDtypeStruct((t, D), BF16)],
                   [_rows(tm, D)])[0]
    out = _mm(tag + "_out", act, p["w_out"], res=x)
    return out, (x, h, proj, gate_ins, pre_ins, pre, states, post_ins, act)


def _gdn_bwd(tag, saved, p, dout, after=None):
    x, h, proj, gate_ins, pre_ins, pre, states, post_ins, act = saved
    t = x.shape[0]
    tm, rows, nq, conv, gate_in, gate_out, pre_in, pre_out, post_in = _gdn_specs(t)
    g = {"w_out": _mm(tag + "_dwout", act, dout, ta=True, out_dtype=BF16, after=after)}
    dact = _mm(tag + "_dact", dout, p["w_out"], tb=True, out_dtype=BF16)
    do, dproj, g["o_g"] = _seg_bwd(tag + "_dpost", _f_gdn_post, (t // tm,), post_ins, post_in, [dact],
                                   [_rows(tm, D)], [F32, BF16, "acc"])
    dpre = _gdn_scan_bwd(tag + "_dscan", *pre, states, do)
    dqn, dkn, dqkv, dgb, dbb = _seg_bwd(tag + "_dprescan", _f_gdn_prescan, (HEADS, t // rows), pre_ins, pre_in,
                                        dpre, pre_out, [F32] * 5)
    dqkv, dproj, g["a_log"], g["dt_bias"] = _seg_bwd(
        tag + "_dgates", _f_gdn_gates, (t // tm,), gate_ins, gate_in, [dqn, dkn, dgb, dbb], gate_out,
        [F32, BF16, "acc", "acc"], into={0: dqkv, 1: dproj})
    dproj, g["conv_w"] = _seg_bwd(tag + "_dconv", _f_conv_silu, (nq, 1), [proj, p["conv_w"]], _grid2(conv[0]),
                                  [dqkv], _grid2(conv[1]), [BF16, "acc"], into={0: dproj})
    g["w_in"] = _mm(tag + "_dwin", h, dproj, ta=True, out_dtype=BF16)
    dh = _mm(tag + "_dh", dproj, p["w_in"], tb=True, out_dtype=BF16)
    dx, g["g"] = _rms_bwd(tag + "_drms", x, p["g"], dh, dout)
    return dx, g


def _fox_specs(t):
    tm = min(t, 512)
    vec = _const((1, LANE))
    pre_in = [_rows(tm, 2 * D + LANE, 0), vec, vec, vec]
    pre_out = [_rows(tm, D), _rows(tm, D), _rows(tm, LANE)]
    return tm, pre_in, pre_out


def _per_head(c):
    return jnp.transpose(c[:, :HEADS])


def _per_lane(ch):
    return jnp.pad(jnp.transpose(ch), ((0, 0), (0, LANE - HEADS)))


def _fox_fwd_layer(tag, x, p):
    t = x.shape[0]
    tm, pre_in, pre_out = _fox_specs(t)
    blk, nb = _fox_blocks(t)
    h = _rms_fwd(tag + "_rms", x, p["g"])
    proj = _mm(tag + "_in", h, p["w_in"])
    pre_ins = [proj, p["q_g"], p["k_g"], p["b_f"]]
    qn, kn, lf = _seg_fwd(tag + "_pre", _f_fox_pre, (t // tm,), pre_ins, pre_in,
                          [jax.ShapeDtypeStruct((t, D), BF16)] * 2 + [jax.ShapeDtypeStruct((t, LANE), F32)], pre_out)
    ch = _per_head(_cumsum_rows(tag + "_cumsum", [lf], False))
    c_col, c_row = ch.reshape(HEADS, t, 1), ch.reshape(HEADS, nb, 1, blk)
    o, lse = _fox_fwd(tag + "_attn", qn, kn, proj, c_col, c_row)
    out = _mm(tag + "_out", o, p["w_out"], res=x)
    return out, (x, h, proj, pre_ins, qn, kn, c_col, c_row, o, lse)


def _fox_bwd_layer(tag, saved, p, dout, after=None):
    x, h, proj, pre_ins, qn, kn, c_col, c_row, o, lse = saved
    t = x.shape[0]
    tm, pre_in, pre_out = _fox_specs(t)
    g = {"w_out": _mm(tag + "_dwout", o, dout, ta=True, out_dtype=BF16, after=after)}
    do = _mm(tag + "_do", dout, p["w_out"], tb=True)
    dqn, dc_col, delta = _fox_dq(tag + "_dq", qn, kn, proj, c_col, c_row, o, lse, do)
    dkn, dproj, dc_row = _fox_dkv(tag + "_dkv", qn, kn, proj, c_col, c_row, lse, delta, do)
    dlf = _cumsum_rows(tag + "_dcumsum", [_per_lane(dc_col.reshape(HEADS, t)), _per_lane(dc_row.reshape(HEADS, t))],
                       True)
    dproj, g["q_g"], g["k_g"], g["b_f"] = _seg_bwd(
        tag + "_dpre", _f_fox_pre, (t // tm,), pre_ins, pre_in, [dqn, dkn, dlf], pre_out,
        [BF16, "acc", "acc", "acc"], into={0: dproj})
    g["w_in"] = _mm(tag + "_dwin", h, dproj, ta=True, out_dtype=BF16)
    dh = _mm(tag + "_dh", dproj, p["w_in"], tb=True, out_dtype=BF16)
    dx, g["g"] = _rms_bwd(tag + "_drms", x, p["g"], dh, dout)
    return dx, g


_MIXERS = ((_conf_fwd, _conf_bwd), (_gdn_fwd, _gdn_bwd), (_fox_fwd_layer, _fox_bwd_layer))


def _local_step(x, target, params_of, on_grads):
    saved, params = [], []
    for i in range(DEPTH):
        mp = params_of(i, 0, x)
        x, sm = _MIXERS[i % N_MIXERS][0](f"l{i}_mix", x, mp)
        fp = params_of(i, 1, x)
        x, sf = _ffn_fwd(f"l{i}_ffn", x, fp)
        saved.append((sm, sf))
        params.append((mp, fp))
    dx, sq = _loss_head(x, target)
    after = None
    for i in reversed(range(DEPTH)):
        dx, gf = _ffn_bwd(f"l{i}_ffn", saved[i][1], params[i][1], dx, after)
        after = on_grads(i, 1, gf)
        dx, gm = _MIXERS[i % N_MIXERS][1](f"l{i}_mix", saved[i][0], params[i][0], dx, after)
        after = on_grads(i, 0, gm)
    return sq, dx


def _unshard(name, g):
    axis = g.ndim - 2 if name in ROW_SHARDED else g.ndim - 1
    m = jnp.moveaxis(g, 0, axis - 1)
    return m.reshape(m.shape[:axis - 1] + (N_DEV * m.shape[axis],) + m.shape[axis + 1:])


def _reshard(name, full):
    axis = full.ndim - 2 if name in ROW_SHARDED else full.ndim - 1
    s = full.shape
    return jnp.moveaxis(full.reshape(s[:axis] + (N_DEV, s[axis] // N_DEV) + s[axis + 1:]), axis, 0)


def _pad_cols(a, width):
    return jnp.pad(a, [(0, 0)] * (a.ndim - 1) + [(0, width - a.shape[-1])])


def _lane_vec(v):
    return _pad_cols(v.reshape(1, -1), LANE)


REP_ROWS = 16


def _pack_rep(r):
    small = [_pad_cols(r[n].reshape(1, -1), LANE) for n in REPLICATED[2:]]
    row = jnp.concatenate(small + [jnp.zeros((1, D - LANE * len(small)), F32)], axis=1)
    pad = jnp.zeros((REP_ROWS - 2 * DEPTH - 1, D), F32)
    return jnp.concatenate([r['mix_norm_g'].reshape(DEPTH, D), r['ffn_norm_g'].reshape(DEPTH, D), row, pad], axis=0)


def _unpack_rep(a, shapes):
    out = {'mix_norm_g': a[:DEPTH], 'ffn_norm_g': a[DEPTH:2 * DEPTH]}
    for i, n in enumerate(REPLICATED[2:]):
        out[n] = a[2 * DEPTH:2 * DEPTH + 1, i * LANE:i * LANE + shapes[n][1]].reshape(shapes[n])
    return out


def _view2d(a):
    return a.reshape(-1, a.shape[-1])


MIXER_SHARDED = (
    (('conv_w_in', 'w_in'), ('conv_w_out', 'w_out'), ('conv_b_in', 'b_in'), ('conv_w_dw', 'w_dw'),
     ('conv_b_dw', 'b_dw'), ('conv_ln_g', 'ln_g'), ('conv_ln_b', 'ln_b')),
    (('gdn_w_in', 'w_in'), ('gdn_w_out', 'w_out'), ('gdn_conv_w', 'conv_w')),
    (('fox_w_in', 'w_in'), ('fox_w_out', 'w_out')),
)
FFN_SHARDED = (('ffn_w_up', 'w_up'), ('ffn_w_down', 'w_down'), ('ffn_w_dw', 'w_dw'))
MIXER_REPLICATED = (
    (),
    (('gdn_a_log', 'a_log'), ('gdn_dt_bias', 'dt_bias'), ('gdn_o_norm_g', 'o_g')),
    (('fox_b_f', 'b_f'), ('fox_q_norm_g', 'q_g'), ('fox_k_norm_g', 'k_g')),
)
ROW_VECTORS = ('conv_b_in', 'conv_b_dw', 'conv_ln_g', 'conv_ln_b')
INTERLEAVED = {'ffn_w_up': (D_FF, FFN_TC), 'ffn_w_dw': (D_FF, FFN_TC), 'conv_w_in': (D, CONF_TC),
               'conv_b_in': (D, CONF_TC)}
PACK_GROUP = 16 * D


def _part_entries(i, part):
    ent = [(n, i, k) for n, k in FFN_SHARDED] if part else [(n, i // N_MIXERS, k) for n, k in MIXER_SHARDED[i % N_MIXERS]]
    return [e for e in ent if e[0] in MATRICES], [e for e in ent if e[0] not in MATRICES]


def _interleave(a, half, blk):
    s = a.shape[:-1]
    return jnp.swapaxes(a.reshape(s + (2, half // blk, blk)), -3, -2).reshape(s + (2 * half,))


def _deinterleave(a, half, blk):
    s = a.shape[:-1]
    return jnp.swapaxes(a.reshape(s + (half // blk, 2, blk)), -3, -2).reshape(s + (2 * half,))


def _to_param(name, whole):
    if name in ROW_VECTORS:
        whole = whole[None]
    if name in INTERLEAVED:
        return _interleave(whole, *INTERLEAVED[name])
    if name == 'gdn_w_in':
        return _pad_cols(whole, GDN_PAD)
    if name == 'fox_w_in':
        return jnp.concatenate([whole[:, :2 * D], _pad_cols(whole[:, 3 * D:], LANE), whole[:, 2 * D:3 * D]], axis=1)
    return whole


def _from_grad(name, g):
    if name in INTERLEAVED:
        g = _deinterleave(g, *INTERLEAVED[name])
    if name in ROW_VECTORS:
        return g[0]
    if name == 'gdn_w_in':
        return g[:, :4 * D + 2 * HEADS]
    if name == 'fox_w_in':
        return jnp.concatenate([g[:, :2 * D], g[:, 2 * D + LANE:], g[:, 2 * D:2 * D + HEADS]], axis=1)
    return g


def _layer_col(name, n):
    if name in INTERLEAVED:
        half, blk = INTERLEAVED[name]
        return (n % half) // blk * (2 * blk) + (n // half) * blk + n % blk
    if name == 'fox_w_in':
        return np.where(n < 2 * D, n, np.where(n < 3 * D, n + LANE, n - D))
    return n


def _col_runs(name, shard_cols):
    dst = _layer_col(name, np.arange(N_DEV * shard_cols))
    runs, start = [[] for _ in range(N_DEV)], 0
    for k in range(1, dst.size + 1):
        if k == dst.size or dst[k] != dst[k - 1] + 1 or k % shard_cols == 0:
            runs[start // shard_cols].append((start % shard_cols, k - start, int(dst[start])))
            start = k
    width = {'gdn_w_in': GDN_PAD, 'fox_w_in': FOX_PAD}.get(name, dst.size)
    free = np.ones(width + 1, bool)
    free[dst] = False
    free[width] = False
    gaps, start = [], None
    for k in range(width + 1):
        if free[k] and start is None:
            start = k
        if not free[k] and start is not None:
            gaps.append((start, k - start))
            start = None
    return runs, width, gaps


def _unshard_cols(call, name, land, own, mine):
    _, r, c = land.shape
    runs, width, gaps = _col_runs(name, c)
    tr = _tile(r, 256)

    def body(me_ref, land_ref, own_ref, o_ref):
        for s in range(N_DEV):
            for src, ln, dst in runs[s]:
                o_ref[:, dst:dst + ln] = jnp.where(me_ref[0] == s, own_ref[:, src:src + ln], land_ref[s, :, src:src + ln])
        for start, ln in gaps:
            o_ref[:, start:start + ln] = jnp.zeros((tr, ln), o_ref.dtype)

    return pl.pallas_call(
        body, grid=(r // tr,),
        in_specs=[pl.BlockSpec(memory_space=pltpu.SMEM), pl.BlockSpec((N_DEV, tr, c), lambda i: (0, i, 0)),
                  pl.BlockSpec((tr, c), lambda i: (i, 0))],
        out_specs=pl.BlockSpec((tr, width), lambda i: (i, 0)), out_shape=jax.ShapeDtypeStruct((r, width), land.dtype),
        name=call, compiler_params=_params("parallel"))(mine, land, own)


def _reshard_cols(call, name, g, shard_cols):
    r = g.shape[0]
    runs, width, _ = _col_runs(name, shard_cols)
    tr = _tile(r, 256)

    def body(g_ref, o_ref):
        for s in range(N_DEV):
            for src, ln, dst in runs[s]:
                o_ref[s, :, src:src + ln] = g_ref[:, dst:dst + ln]

    return pl.pallas_call(
        body, grid=(r // tr,), in_specs=[pl.BlockSpec((tr, width), lambda i: (i, 0))],
        out_specs=pl.BlockSpec((N_DEV, tr, shard_cols), lambda i: (0, i, 0)),
        out_shape=jax.ShapeDtypeStruct((N_DEV, r, shard_cols), g.dtype), name=call,
        compiler_params=_params("parallel"))(g)


def _pack_rows(parts, lead):
    out = []
    for a in parts:
        flat = a.reshape(a.shape[:lead] + (-1,))
        size = flat.shape[-1]
        padded = -(-size // PACK_GROUP) * PACK_GROUP
        flat = jnp.pad(flat, [(0, 0)] * lead + [(0, padded - size)])
        out.append(flat.reshape(a.shape[:lead] + (padded // D, D)))
    return jnp.concatenate(out, axis=lead)


def _unpack_rows(packed, shapes, lead):
    out, row = [], 0
    head = packed.shape[:lead]
    for s in shapes:
        size = 1
        for d in s:
            size *= d
        rows = -(-size // PACK_GROUP) * (PACK_GROUP // D)
        part = lax.slice_in_dim(packed, row, row + rows, axis=lead)
        out.append(part.reshape(head + (rows * D,))[..., :size].reshape(head + tuple(s)))
        row += rows
    return out


def _part_dict(i, part, whole, rep):
    if part:
        p = {k: _to_param(n, whole[n]) for n, k in FFN_SHARDED if n in whole}
        p["g"] = rep['ffn_norm_g'][i][None]
        return p
    kind, j = i % N_MIXERS, i // N_MIXERS
    p = {k: _to_param(n, whole[n]) for n, k in MIXER_SHARDED[kind] if n in whole}
    for n, k in MIXER_REPLICATED[kind]:
        p[k] = rep[n][j][None] if rep[n].shape[-1] == DH else _lane_vec(rep[n][j])
    p["g"] = rep['mix_norm_g'][i][None]
    return p


def _part_grads(i, part, g):
    return {n: _from_grad(n, g[k]) for n, k in (FFN_SHARDED if part else MIXER_SHARDED[i % N_MIXERS])}


def _replicated_grads(grads):
    rep = {'mix_norm_g': jnp.concatenate([g[0]["g"] for g in grads]),
           'ffn_norm_g': jnp.concatenate([g[1]["g"] for g in grads])}
    for kind in range(N_MIXERS):
        for n, k in MIXER_REPLICATED[kind]:
            rep[n] = jnp.stack([grads[i][0][k] for i in range(kind, DEPTH, N_MIXERS)])
    return rep


def kernel(x, mix_norm_g, ffn_norm_g, conv_w_in, conv_b_in, conv_w_dw, conv_b_dw, conv_ln_g, conv_ln_b, conv_w_out, gdn_w_in, gdn_conv_w, gdn_a_log, gdn_dt_bias, gdn_o_norm_g, gdn_w_out, fox_w_in, fox_b_f, fox_q_norm_g, fox_k_norm_g, fox_w_out, ffn_w_up, ffn_w_dw, ffn_w_down, loss_target, m_mix_norm_g, m_ffn_norm_g, m_conv_w_in, m_conv_b_in, m_conv_w_dw, m_conv_b_dw, m_conv_ln_g, m_conv_ln_b, m_conv_w_out, m_gdn_w_in, m_gdn_conv_w, m_gdn_a_log, m_gdn_dt_bias, m_gdn_o_norm_g, m_gdn_w_out, m_fox_w_in, m_fox_b_f, m_fox_q_norm_g, m_fox_k_norm_g, m_fox_w_out, m_ffn_w_up, m_ffn_w_dw, m_ffn_w_down, v_mix_norm_g, v_ffn_norm_g, v_conv_w_in, v_conv_b_in, v_conv_w_dw, v_conv_b_dw, v_conv_ln_g, v_conv_ln_b, v_conv_w_out, v_gdn_w_in, v_gdn_conv_w, v_gdn_a_log, v_gdn_dt_bias, v_gdn_o_norm_g, v_gdn_w_out, v_fox_w_in, v_fox_b_f, v_fox_q_norm_g, v_fox_k_norm_g, v_fox_w_out, v_ffn_w_up, v_ffn_w_dw, v_ffn_w_down):
    given = dict(zip(
        WEIGHTS + ["m_" + n for n in WEIGHTS] + ["v_" + n for n in WEIGHTS],
        (mix_norm_g, ffn_norm_g, conv_w_in, conv_b_in, conv_w_dw, conv_b_dw, conv_ln_g, conv_ln_b, conv_w_out, gdn_w_in, gdn_conv_w, gdn_a_log, gdn_dt_bias, gdn_o_norm_g, gdn_w_out, fox_w_in, fox_b_f, fox_q_norm_g, fox_k_norm_g, fox_w_out, ffn_w_up, ffn_w_dw, ffn_w_down,
         m_mix_norm_g, m_ffn_norm_g, m_conv_w_in, m_conv_b_in, m_conv_w_dw, m_conv_b_dw, m_conv_ln_g, m_conv_ln_b, m_conv_w_out, m_gdn_w_in, m_gdn_conv_w, m_gdn_a_log, m_gdn_dt_bias, m_gdn_o_norm_g, m_gdn_w_out, m_fox_w_in, m_fox_b_f, m_fox_q_norm_g, m_fox_k_norm_g, m_fox_w_out, m_ffn_w_up, m_ffn_w_dw, m_ffn_w_down,
         v_mix_norm_g, v_ffn_norm_g, v_conv_w_in, v_conv_b_in, v_conv_w_dw, v_conv_b_dw, v_conv_ln_g, v_conv_ln_b, v_conv_w_out, v_gdn_w_in, v_gdn_conv_w, v_gdn_a_log, v_gdn_dt_bias, v_gdn_o_norm_g, v_gdn_w_out, v_fox_w_in, v_fox_b_f, v_fox_q_norm_g, v_fox_k_norm_g, v_fox_w_out, v_ffn_w_up, v_ffn_w_dw, v_ffn_w_down)))

    me = 4 * lax.axis_index("x") + 2 * lax.axis_index("y") + lax.axis_index("c")
    mine = me.astype(jnp.int32).reshape(1)

    gathers, token = {}, jnp.zeros((1, 1), F32)
    for i in range(DEPTH):
        for part in (0, 1):
            mats, smalls = _part_entries(i, part)
            sent = [given[n][j].astype(BF16) for n, j, _ in mats]
            if smalls:
                sent.append(_pack_rows([given[n][j] for n, j, _ in smalls], 0))
            gathers[i, part], tok = _exchange_start(f"gather{i}{'mf'[part]}_start", sent, [False] * len(sent))
            token = token + tok[0:1, 0:1]

    def params_of(i, part, x_in):
        mats, smalls = _part_entries(i, part)
        after = token if (i, part) == (0, 0) else x_in
        lands, owns = _exchange_wait(f"gather{i}{'mf'[part]}_wait", gathers[i, part], after, me)
        whole, relaid = {}, {}
        for (n, _, k), land, own in zip(mats, lands, owns):
            if n in ROW_SHARDED:
                whole[n] = _unshard(n, _fill_own(land, own, me))
            else:
                relaid[k] = _unshard_cols(f"l{i}_{n}_unshard", n, land, own, mine)
        if smalls:
            shapes = [given[n].shape[1:] for n, _, _ in smalls]
            for (n, _, _), g in zip(smalls, _unpack_rows(_fill_own(lands[-1], owns[-1], me), shapes, 1)):
                whole[n] = _unshard(n, g)
        return {**_part_dict(i, part, whole, given), **relaid}

    grads, exchanges = [[None, None] for _ in range(DEPTH)], {}

    def on_grads(i, part, g):
        grads[i][part] = g
        mats, smalls = _part_entries(i, part)
        sent = [_reshard(n, g[k]) if n in ROW_SHARDED else
                _reshard_cols(f"l{i}_{n}_reshard", n, g[k], given[n].shape[-1]) for n, _, k in mats]
        if smalls:
            sent.append(_pack_rows([_reshard(n, _from_grad(n, g[k])) for n, _, k in smalls], 1))
        last = (i, part) == (0, 0)
        if last:
            sent.append(_pack_rep(_replicated_grads(grads)))
        exchanges[i, part], tok = _exchange_start(f"grads{i}{'mf'[part]}_start", sent,
                                                  [True] * (len(sent) - last) + [False] * last)
        return tok

    sq, dx = _local_step(x[0], loss_target[0], params_of, on_grads)
    loss = (0.5 / D) * lax.psum(sq[0, 0], ("x", "y", "c"))

    pieces = {n: [] for n in SHARDED}
    for i in range(DEPTH):
        for part in (0, 1):
            mats, smalls = _part_entries(i, part)
            lands, owns = _exchange_wait(f"grads{i}{'mf'[part]}_wait", exchanges[i, part], dx, me)
            if (i, part) == (0, 0):
                rep_piece = (lands[-1], owns[-1])
            for (n, _, _), land, own in zip(mats, lands, owns):
                pieces[n].append((land, own))
            if smalls:
                shapes = [given[n].shape[1:] for n, _, _ in smalls]
                for (n, _, _), land, own in zip(smalls, _unpack_rows(lands[len(mats)], shapes, 1),
                                                _unpack_rows(owns[len(mats)], shapes, 0)):
                    pieces[n].append((land, own))
    new = {}
    for n in SHARDED:
        lands = [l.reshape((N_DEV, -1, l.shape[-1])) for l, _ in pieces[n]]
        owns = [o.reshape((-1, o.shape[-1])) for _, o in pieces[n]]
        if owns[0].shape[0] % 8:
            lands = [jnp.stack(lands, axis=1).reshape((N_DEV, -1, lands[0].shape[-1]))]
            owns = [jnp.stack(owns).reshape((-1, owns[0].shape[-1]))]
        outs = _adam("adam_" + n, lands, owns, mine, _view2d(given[n]), _view2d(given["m_" + n]),
                     _view2d(given["v_" + n]))
        new[n] = [o.reshape(given[n].shape) for o in outs]
    packed = [_pack_rep({n: given[pre + n] for n in REPLICATED}) for pre in ("", "m_", "v_")]
    outs = _adam("adam_replicated", [rep_piece[0]], [rep_piece[1]], mine, *packed)
    unpacked = [_unpack_rep(o, {n: given[n].shape for n in REPLICATED}) for o in outs]
    for n in REPLICATED:
        new[n] = [u[n] for u in unpacked]
    return (loss, dx[None], *[new[n][0] for n in WEIGHTS], *[new[n][1] for n in WEIGHTS],
            *[new[n][2] for n in WEIGHTS], *[new[n][3] for n in WEIGHTS])
```

```python
import functools

import jax
import jax.numpy as jnp
import numpy as np
from jax import lax
from jax.experimental import pallas as pl
from jax.experimental.pallas import tpu as pltpu

F32 = jnp.float32
BF16 = jnp.bfloat16
HIGHEST = lax.Precision.HIGHEST
HIGH = lax.Precision.HIGH

N_DEV = 8
LANE = 128
EPS = 1e-6
DEPTH = 4
N_MIXERS = 3
HEADS = 8
DH = 128
D = HEADS * DH
D_FF = 2816
CONF_K, GDN_K, FFN_K = 31, 4, 3
GDN_CHUNK = 64
GDN_PAD = 4224
FOX_PAD = 3200
FOX_V_BLOCK = (2 * D + LANE) // DH
ADAM_LR, ADAM_B1, ADAM_B2, ADAM_EPS, ADAM_WD, ADAM_STEP = 0.001, 0.9, 0.999, 1e-08, 0.01, 10
VMEM_LIMIT = 56 * 1024 * 1024

WEIGHTS = ['mix_norm_g', 'ffn_norm_g', 'conv_w_in', 'conv_b_in', 'conv_w_dw', 'conv_b_dw', 'conv_ln_g', 'conv_ln_b',
           'conv_w_out', 'gdn_w_in', 'gdn_conv_w', 'gdn_a_log', 'gdn_dt_bias', 'gdn_o_norm_g', 'gdn_w_out', 'fox_w_in',
           'fox_b_f', 'fox_q_norm_g', 'fox_k_norm_g', 'fox_w_out', 'ffn_w_up', 'ffn_w_dw', 'ffn_w_down']
REPLICATED = ['mix_norm_g', 'ffn_norm_g', 'gdn_a_log', 'gdn_dt_bias', 'gdn_o_norm_g', 'fox_b_f', 'fox_q_norm_g',
              'fox_k_norm_g']
ROW_SHARDED = ['conv_w_out', 'gdn_w_out', 'fox_w_out', 'ffn_w_down']
MATRICES = ['conv_w_in', 'conv_w_out', 'gdn_w_in', 'gdn_w_out', 'fox_w_in', 'fox_w_out', 'ffn_w_up', 'ffn_w_down']
SHARDED = [n for n in WEIGHTS if n not in REPLICATED]


def _params(*sem):
    return pltpu.CompilerParams(dimension_semantics=sem, vmem_limit_bytes=VMEM_LIMIT)


def _tile(n, cap):
    if n <= cap:
        return n
    d = (cap // LANE) * LANE
    while d >= LANE:
        if n % d == 0:
            return d
        d -= LANE
    raise ValueError(f"no lane-aligned tile of {n} under {cap}")


def _raw_dot(a, b, ca, cb, hp):
    batch = ((0,), (0,)) if a.ndim == 3 else ((), ())
    dn = (((ca,), (cb,)), batch)
    if hp:
        return lax.dot_general(a.astype(F32), b.astype(F32), dn, precision=HIGH, preferred_element_type=F32)
    return lax.dot_general(a.astype(BF16), b.astype(BF16), dn, preferred_element_type=F32)


def _raw_nn(a, b, hp=False):
    return _raw_dot(a, b, a.ndim - 1, b.ndim - 2, hp)


def _raw_nt(a, b, hp=False):
    return _raw_dot(a, b, a.ndim - 1, b.ndim - 1, hp)


def _raw_tn(a, b, hp=False):
    return _raw_dot(a, b, a.ndim - 2, b.ndim - 2, hp)


@functools.partial(jax.custom_vjp, nondiff_argnums=(2,))
def _nn(a, b, hp):
    return _raw_nn(a, b, hp)


def _nn_fwd(a, b, hp):
    return _raw_nn(a, b, hp), (a, b)


def _nn_bwd(hp, res, g):
    a, b = res
    return _raw_nt(g, b, hp), _raw_tn(a, g, hp)


_nn.defvjp(_nn_fwd, _nn_bwd)


@functools.partial(jax.custom_vjp, nondiff_argnums=(2,))
def _nt(a, b, hp):
    return _raw_nt(a, b, hp)


def _nt_fwd(a, b, hp):
    return _raw_nt(a, b, hp), (a, b)


def _nt_bwd(hp, res, g):
    a, b = res
    return _raw_nn(g, b, hp), _raw_tn(g, a, hp)


_nt.defvjp(_nt_fwd, _nt_bwd)


def _shift_down(x, s):
    if s == 0:
        return x
    t = lax.broadcasted_iota(jnp.int32, x.shape, 0)
    return jnp.where(t >= s, pltpu.roll(x, s, axis=0), 0.0)


def _shift_up(x, s):
    if s == 0:
        return x
    n = x.shape[0]
    t = lax.broadcasted_iota(jnp.int32, x.shape, 0)
    return jnp.where(t < n - s, pltpu.roll(x, n - s, axis=0), 0.0)


def _row(w, k):
    r = lax.broadcasted_iota(jnp.int32, w.shape, 0)
    return jnp.sum(jnp.where(r == k, w, 0.0), axis=0, keepdims=True)


@jax.custom_vjp
def _dwconv(x, w):
    taps = w.shape[0]
    y = _row(w, taps - 1) * x
    for k in range(taps - 1):
        y = y + _row(w, k) * _shift_down(x, taps - 1 - k)
    return y


def _dwconv_fwd(x, w):
    return _dwconv(x, w), (x, w)


def _dwconv_bwd(res, dy):
    x, w = res
    taps = w.shape[0]
    r = lax.broadcasted_iota(jnp.int32, w.shape, 0)
    dx = _row(w, taps - 1) * dy
    dw = jnp.where(r == taps - 1, jnp.sum(dy * x, axis=0, keepdims=True), 0.0)
    for k in range(taps - 1):
        up = _shift_up(dy, taps - 1 - k)
        dx = dx + _row(w, k) * up
        dw = dw + jnp.where(r == k, jnp.sum(up * x, axis=0, keepdims=True), 0.0)
    return dx, dw


_dwconv.defvjp(_dwconv_fwd, _dwconv_bwd)


def _sigmoid(x):
    return 1.0 / (1.0 + jnp.exp(-x))


def _silu(x):
    return x * _sigmoid(x)


def _softplus(x):
    return jnp.maximum(x, 0.0) + jnp.log(1.0 + jnp.exp(-jnp.abs(x)))


def _head_scale(x, fn):
    tm = x.shape[0]
    x3 = x.reshape(tm, HEADS, DH)
    return (x3 * fn(jnp.sum(x3 * x3, axis=-1, keepdims=True))).reshape(tm, HEADS * DH)


def _tile_lanes(g):
    return jnp.concatenate([g] * HEADS, axis=1)


def _expand_heads(v, first):
    lane = lax.broadcasted_iota(jnp.int32, (LANE, HEADS * DH), 0)
    col = lax.broadcasted_iota(jnp.int32, (LANE, HEADS * DH), 1)
    sel = (lane == col // DH + first).astype(F32)
    return _nn(v, sel, True)


def _f_rms(x, g):
    return (x * lax.rsqrt(jnp.mean(x * x, axis=-1, keepdims=True) + EPS) * g,)


def _f_rms_res(x, g):
    return (_f_rms(x, g)[0], x)


def _f_conf_glu_conv(u, b, w):
    c = u.shape[1] // 2
    return (_dwconv((u[:, :c] + b[:, :c]) * _sigmoid(u[:, c:] + b[:, c:]), w),)


def _f_conf_ln_silu(cv, b_dw, ln_g, ln_b):
    u = cv + b_dw
    xc = u - jnp.mean(u, axis=-1, keepdims=True)
    y = xc * lax.rsqrt(jnp.mean(xc * xc, axis=-1, keepdims=True) + EPS) * ln_g + ln_b
    return (_silu(y),)


def _f_conv_silu(u, w):
    return (_silu(_dwconv(u, w)),)


def _f_gdn_gates(qk, ab, a_log, dt_bias):
    qn = _head_scale(qk[:, :D], lambda ss: lax.rsqrt(ss + EPS)) * (DH ** -0.5)
    kn = _head_scale(qk[:, D:], lambda ss: lax.rsqrt(ss + EPS))
    g = -jnp.exp(a_log) * _softplus(ab + dt_bias)
    beta = _sigmoid(ab)
    return qn, kn, _expand_heads(g, 0), _expand_heads(beta, HEADS)


def _f_gdn_prescan(q, k, v, gb, bb):
    c = GDN_CHUNK
    n = q.shape[0] // c
    r3 = lambda t: t.reshape(n, c, DH)
    q3, k3, v3, g3, b3 = r3(q), r3(k), r3(v), r3(gb), r3(bb)
    ii = lax.broadcasted_iota(jnp.int32, (n, c, c), 1)
    jj = lax.broadcasted_iota(jnp.int32, (n, c, c), 2)
    lower, strict = ii >= jj, ii > jj
    gcb = _nn(lower.astype(F32), g3, True)
    gi = gcb[:, :, :c]
    gj = jnp.swapaxes(gi, 1, 2)
    decay = jnp.where(lower, jnp.exp(jnp.where(lower, gi - gj, 0.0)), 0.0)
    kb, vb = k3 * b3, v3 * b3
    a_mat = jnp.where(strict, _nt(kb, k3, False) * decay, 0.0)
    p = -a_mat
    t_mat = (ii == jj).astype(F32) + p
    for _ in range(5):
        p = _nn(p, p, True)
        t_mat = t_mat + _nn(t_mat, p, True)
    eg = jnp.exp(gcb)
    u = _nn(t_mat, vb, False)
    w = _nn(t_mat, kb * eg, False)
    qk = jnp.where(lower, _nt(q3, k3, False) * decay, 0.0)
    qg = q3 * eg
    g_last = jnp.sum(g3, axis=1, keepdims=True)
    kd = k3 * jnp.exp(g_last - gcb)
    r2 = lambda t: t.reshape(n * c, DH)
    return r2(u), r2(w), r2(qg), qk, r2(kd), jnp.exp(g_last)


def _f_gdn_post(o, z, o_g):
    on = _head_scale(o, lambda ss: lax.rsqrt(ss / DH + EPS)) * _tile_lanes(o_g)
    return (on * _silu(z),)


def _f_fox_pre(qkf, q_g, k_g, b_f):
    qn = _head_scale(qkf[:, :D], lambda ss: lax.rsqrt(ss / DH + EPS)) * _tile_lanes(q_g)
    kn = _head_scale(qkf[:, D:2 * D], lambda ss: lax.rsqrt(ss / DH + EPS)) * _tile_lanes(k_g)
    return qn, kn, -_softplus(-(qkf[:, 2 * D:] + b_f))


def _f_ffn_mid(u, w):
    c = u.shape[1] // 2
    return (_silu(_dwconv(u[:, :c], w[:, :c])) * _dwconv(u[:, c:], w[:, c:]),)


def _seg_fwd(name, f, grid, ins, in_specs, out_shapes, out_specs):
    n_in = len(ins)

    def body(*refs):
        outs = f(*[r[...].astype(F32) for r in refs[:n_in]])
        for r, o in zip(refs[n_in:], outs):
            r[...] = o.astype(r.dtype)

    return pl.pallas_call(body, grid=grid, in_specs=in_specs, out_specs=out_specs, out_shape=out_shapes, name=name,
                          compiler_params=_params(*(["parallel"] * len(grid))))(*ins)


def _seg_bwd(name, f, grid, ins, in_specs, douts, dout_specs, want, into=None):
    into = into or {}
    n_in, n_dy = len(ins), len(douts)
    diff = [i for i, w in enumerate(want) if w is not None]
    kept = [i for i in diff if i in into]
    out_shapes = [jax.ShapeDtypeStruct(ins[i].shape, into[i].dtype if i in into else F32 if want[i] == "acc" else want[i])
                  for i in diff]
    out_specs = [in_specs[i] for i in diff]
    acc_axis = len(grid) - 1

    def body(*refs):
        vals = [r[...].astype(F32) for r in refs[:n_in]]
        dys = [r[...].astype(F32) for r in refs[n_in:n_in + n_dy]]
        out_refs = refs[n_in + n_dy + len(kept):]

        def g(*dv):
            full = list(vals)
            for i, v in zip(diff, dv):
                full[i] = v
            return f(*full)

        _, vjp = jax.vjp(g, *[vals[i] for i in diff])
        grads = vjp(tuple(dys))
        first = pl.program_id(acc_axis) == 0
        for i, r, gr in zip(diff, out_refs, grads):
            if want[i] == "acc":
                @pl.when(first)
                def _(r=r, gr=gr):
                    r[...] = gr

                @pl.when(jnp.logical_not(first))
                def _(r=r, gr=gr):
                    r[...] += gr
            else:
                r[...] = gr.astype(r.dtype)

    sem = ["parallel"] * (len(grid) - 1) + ["arbitrary"]
    untouched = [pl.BlockSpec(memory_space=pl.ANY)] * len(kept)
    aliases = {n_in + n_dy + e: diff.index(i) for e, i in enumerate(kept)}
    return pl.pallas_call(body, grid=grid, in_specs=list(in_specs) + list(dout_specs) + untouched, out_specs=out_specs,
                          out_shape=out_shapes, input_output_aliases=aliases, name=name,
                          compiler_params=_params(*sem))(*ins, *douts, *[into[i] for i in kept])


def _rows(tm, width, col=0):
    return pl.BlockSpec((tm, width), lambda i, col=col: (i, col))


def _const(shape):
    return pl.BlockSpec(shape, lambda i: (0,) * len(shape))


def _cols(t, tc, off=0):
    return pl.BlockSpec((t, tc), lambda j, off=off: (0, j + off))


def _grid2(specs):
    return [pl.BlockSpec(s.block_shape, lambda j, i, f=s.index_map: f(j)) for s in specs]


def _mm(name, a, b, *, ta=False, tb=False, res=None, out_dtype=F32, after=None):
    k_dim, m = (a.shape[0], a.shape[1]) if ta else (a.shape[1], a.shape[0])
    n = b.shape[0] if tb else b.shape[1]
    tm, tn, tk = _tile(m, 1408), _tile(n, 1408), _tile(k_dim, 1408)
    nk = k_dim // tk
    grid = (m // tm, n // tn, nk)
    a_spec = pl.BlockSpec((tk, tm), lambda i, j, k: (k, i)) if ta else pl.BlockSpec((tm, tk), lambda i, j, k: (i, k))
    b_spec = pl.BlockSpec((tn, tk), lambda i, j, k: (j, k)) if tb else pl.BlockSpec((tk, tn), lambda i, j, k: (k, j))
    o_spec = pl.BlockSpec((tm, tn), lambda i, j, k: (i, j))
    dn = (((0 if ta else 1,), (1 if tb else 0,)), ((), ()))
    has_res = res is not None

    def body(*refs):
        a_ref, b_ref = refs[0], refs[1]
        res_ref = refs[2] if has_res else None
        o_ref = refs[n_in]
        p = lax.dot_general(a_ref[...].astype(BF16), b_ref[...].astype(BF16), dn, preferred_element_type=F32)

        def write(acc):
            if has_res:
                acc = acc + res_ref[...]
            if after is not None:
                acc = acc + refs[n_in - 1][0:1, 0:1]
            o_ref[...] = acc.astype(o_ref.dtype)

        if nk == 1:
            write(p)
        else:
            acc_ref = refs[-1]
            k = pl.program_id(2)

            @pl.when(k == 0)
            def _():
                acc_ref[...] = p

            @pl.when(k > 0)
            def _():
                acc_ref[...] += p

            @pl.when(k == nk - 1)
            def _():
                write(acc_ref[...])

    ins, specs = [a, b], [a_spec, b_spec]
    if has_res:
        ins.append(res)
        specs.append(o_spec)
    if after is not None:
        ins.append(after)
        specs.append(pl.BlockSpec(after.shape, lambda i, j, k: (0, 0)))
    n_in = len(ins)
    scratch = [pltpu.VMEM((tm, tn), F32)] if nk > 1 else []
    return pl.pallas_call(body, grid=grid, in_specs=specs, out_specs=o_spec, scratch_shapes=scratch,
                          out_shape=jax.ShapeDtypeStruct((m, n), out_dtype), name=name,
                          compiler_params=_params("parallel", "parallel", "arbitrary"))(*ins)


def _head_specs(t):
    n = t // GDN_CHUNK
    col = pl.BlockSpec((t, DH), lambda h: (0, h))
    qk = pl.BlockSpec((None, n, GDN_CHUNK, GDN_CHUNK), lambda h: (h, 0, 0, 0))
    gl = pl.BlockSpec((None, n, 1, DH), lambda h: (h, 0, 0, 0))
    st = pl.BlockSpec((None, n, DH, DH), lambda h: (h, 0, 0, 0))
    return n, col, qk, gl, st


def _gdn_scan_fwd(name, u, w, qg, qk, kd, gl):
    t = u.shape[0]
    n, col, qk_spec, gl_spec, st_spec = _head_specs(t)
    c = GDN_CHUNK

    def body(u_ref, w_ref, qg_ref, qk_ref, kd_ref, gl_ref, o_ref, s_ref):
        def step(i, s):
            rows = pl.ds(pl.multiple_of(i * c, c), c)
            s_ref[i] = s
            vn = u_ref[rows, :] - _raw_nn(w_ref[rows, :], s)
            o_ref[rows, :] = _raw_nn(qg_ref[rows, :], s) + _raw_nn(qk_ref[i], vn)
            return s * gl_ref[i] + _raw_tn(kd_ref[rows, :], vn)

        lax.fori_loop(0, n, step, jnp.zeros((DH, DH), F32))

    return pl.pallas_call(
        body, grid=(HEADS,), in_specs=[col, col, col, qk_spec, col, gl_spec], out_specs=[col, st_spec],
        out_shape=[jax.ShapeDtypeStruct((t, D), F32), jax.ShapeDtypeStruct((HEADS, n, DH, DH), F32)], name=name,
        compiler_params=_params("parallel"))(u, w, qg, qk, kd, gl)


def _gdn_scan_bwd(name, u, w, qg, qk, kd, gl, states, do):
    t = u.shape[0]
    n, col, qk_spec, gl_spec, st_spec = _head_specs(t)
    c = GDN_CHUNK

    def body(u_ref, w_ref, qg_ref, qk_ref, kd_ref, gl_ref, s_ref, do_ref,
             du_ref, dw_ref, dqg_ref, dqk_ref, dkd_ref, dgl_ref):
        def step(r, ds):
            i = n - 1 - r
            rows = pl.ds(pl.multiple_of(i * c, c), c)
            s, do_c, w_c = s_ref[i], do_ref[rows, :], w_ref[rows, :]
            vn = u_ref[rows, :] - _raw_nn(w_c, s)
            dvn = _raw_tn(qk_ref[i], do_c) + _raw_nn(kd_ref[rows, :], ds)
            du_ref[rows, :] = dvn
            dw_ref[rows, :] = -_raw_nt(dvn, s)
            dqg_ref[rows, :] = _raw_nt(do_c, s)
            dqk_ref[i] = _raw_nt(do_c, vn)
            dkd_ref[rows, :] = _raw_nt(vn, ds)
            dgl_ref[i] = jnp.sum(ds * s, axis=0, keepdims=True)
            return _raw_tn(qg_ref[rows, :], do_c) + ds * gl_ref[i] - _raw_tn(w_c, dvn)

        lax.fori_loop(0, n, step, jnp.zeros((DH, DH), F32))

    big = jax.ShapeDtypeStruct((t, D), F32)
    return pl.pallas_call(
        body, grid=(HEADS,), in_specs=[col, col, col, qk_spec, col, gl_spec, st_spec, col],
        out_specs=[col, col, col, qk_spec, col, gl_spec],
        out_shape=[big, big, big, jax.ShapeDtypeStruct(qk.shape, F32), big, jax.ShapeDtypeStruct(gl.shape, F32)],
        name=name, compiler_params=_params("parallel"))(u, w, qg, qk, kd, gl, states, do)


def _cumsum_rows(name, parts, reverse):
    t = parts[0].shape[0]
    blk = min(t, 256)
    nb = t // blk
    n_in = len(parts)

    def body(*refs):
        o_ref = refs[n_in]
        ii = lax.broadcasted_iota(jnp.int32, (blk, blk), 0)
        jj = lax.broadcasted_iota(jnp.int32, (blk, blk), 1)
        tri = ((ii <= jj) if reverse else (ii >= jj)).astype(F32)
        carry = jnp.zeros((1, LANE), F32)
        for b in (range(nb - 1, -1, -1) if reverse else range(nb)):
            rows = pl.ds(b * blk, blk)
            x = refs[0][rows, :]
            for r in refs[1:n_in]:
                x = x + r[rows, :]
            o_ref[rows, :] = lax.dot_general(tri, x, (((1,), (0,)), ((), ())), precision=HIGHEST,
                                             preferred_element_type=F32) + carry
            carry = carry + jnp.sum(x, axis=0, keepdims=True)

    return pl.pallas_call(body, out_shape=jax.ShapeDtypeStruct((t, LANE), F32), name=name,
                          compiler_params=_params())(*parts)


def _fox_blocks(t):
    blk = 256 if t % 256 == 0 and t >= 1024 else 128
    return blk, t // blk


def _fox_logits(q, k, cq, ck, diagonal):
    s = _raw_nt(q, k) * (DH ** -0.5) + cq - ck
    if not diagonal:
        return s
    rows = lax.broadcasted_iota(jnp.int32, s.shape, 0)
    cols = lax.broadcasted_iota(jnp.int32, s.shape, 1)
    return jnp.where(cols <= rows, s, -jnp.inf)


def _fox_fwd(name, qn, kn, proj, c_col, c_row):
    t = qn.shape[0]
    blk, nb = _fox_blocks(t)
    voff = FOX_V_BLOCK

    def body(q_ref, k_ref, v_ref, cc_ref, cr_ref, o_ref, lse_ref):
        i = pl.program_id(1)
        q, cq = q_ref[...], cc_ref[...]

        def step(j, carry, diagonal=False):
            m, l, acc = carry
            rows = pl.ds(pl.multiple_of(j * blk, blk), blk)
            s = _fox_logits(q, k_ref[rows, :], cq, cr_ref[j], diagonal)
            m_new = jnp.maximum(m, jnp.max(s, axis=1, keepdims=True))
            p = jnp.exp(s - m_new)
            alpha = jnp.exp(m - m_new)
            return m_new, alpha * l + jnp.sum(p, axis=1, keepdims=True), alpha * acc + _raw_nn(p, v_ref[rows, :])

        init = (jnp.full((blk, 1), -1e30, F32), jnp.zeros((blk, 1), F32), jnp.zeros((blk, DH), F32))
        m, l, acc = step(i, lax.fori_loop(0, i, step, init), True)
        o_ref[...] = acc / l
        lse_ref[...] = m + jnp.log(l)

    tile = pl.BlockSpec((blk, DH), lambda h, i: (i, h))
    colv = pl.BlockSpec((None, blk, 1), lambda h, i: (h, i, 0))
    return pl.pallas_call(
        body, grid=(HEADS, nb),
        in_specs=[tile, pl.BlockSpec((t, DH), lambda h, i: (0, h)), pl.BlockSpec((t, DH), lambda h, i: (0, voff + h)),
                  colv, pl.BlockSpec((None, nb, 1, blk), lambda h, i: (h, 0, 0, 0))],
        out_specs=[tile, colv],
        out_shape=[jax.ShapeDtypeStruct((t, D), F32), jax.ShapeDtypeStruct((HEADS, t, 1), F32)], name=name,
        compiler_params=_params("parallel", "parallel"))(qn, kn, proj, c_col, c_row)


def _fox_dq(name, qn, kn, proj, c_col, c_row, o, lse, do):
    t = qn.shape[0]
    blk, nb = _fox_blocks(t)
    voff = FOX_V_BLOCK

    def body(q_ref, k_ref, v_ref, cc_ref, cr_ref, o_ref, lse_ref, do_ref, dq_ref, dcc_ref, dl_ref):
        i = pl.program_id(1)
        q, cq, do_b, lse_b = q_ref[...], cc_ref[...], do_ref[...], lse_ref[...]
        delta = jnp.sum(do_b * o_ref[...], axis=1, keepdims=True)

        def step(j, carry, diagonal=False):
            dq, dcc = carry
            rows = pl.ds(pl.multiple_of(j * blk, blk), blk)
            k = k_ref[rows, :]
            p = jnp.exp(_fox_logits(q, k, cq, cr_ref[j], diagonal) - lse_b)
            ds = p * (_raw_nt(do_b, v_ref[rows, :]) - delta)
            return dq + _raw_nn(ds, k), dcc + jnp.sum(ds, axis=1, keepdims=True)

        init = (jnp.zeros((blk, DH), F32), jnp.zeros((blk, 1), F32))
        dq, dcc = step(i, lax.fori_loop(0, i, step, init), True)
        dq_ref[...] = dq * (DH ** -0.5)
        dcc_ref[...] = dcc
        dl_ref[...] = delta

    tile = pl.BlockSpec((blk, DH), lambda h, i: (i, h))
    colv = pl.BlockSpec((None, blk, 1), lambda h, i: (h, i, 0))
    vec = jax.ShapeDtypeStruct((HEADS, t, 1), F32)
    return pl.pallas_call(
        body, grid=(HEADS, nb),
        in_specs=[tile, pl.BlockSpec((t, DH), lambda h, i: (0, h)), pl.BlockSpec((t, DH), lambda h, i: (0, voff + h)),
                  colv, pl.BlockSpec((None, nb, 1, blk), lambda h, i: (h, 0, 0, 0)), tile, colv, tile],
        out_specs=[tile, colv, colv],
        out_shape=[jax.ShapeDtypeStruct((t, D), F32), vec, vec], name=name,
        compiler_params=_params("parallel", "parallel"))(qn, kn, proj, c_col, c_row, o, lse, do)


def _fox_dkv(name, qn, kn, proj, c_col, c_row, lse, delta, do):
    t = qn.shape[0]
    blk, nb = _fox_blocks(t)
    voff = FOX_V_BLOCK

    def body(q_ref, k_ref, v_ref, cc_ref, cr_ref, lse_ref, dl_ref, do_ref, dk_ref, dv_ref, dcr_ref):
        j = pl.program_id(1)
        k, v, ck = k_ref[...], v_ref[...], cr_ref[...]

        def step(i, carry, diagonal=False):
            dk, dv, dcr = carry
            rows = pl.ds(pl.multiple_of(i * blk, blk), blk)
            q, do_b = q_ref[rows, :], do_ref[rows, :]
            p = jnp.exp(_fox_logits(q, k, cc_ref[rows, :], ck, diagonal) - lse_ref[rows, :])
            ds = p * (_raw_nt(do_b, v) - dl_ref[rows, :])
            return dk + _raw_tn(ds, q), dv + _raw_tn(p, do_b), dcr - jnp.sum(ds, axis=0, keepdims=True)

        init = (jnp.zeros((blk, DH), F32), jnp.zeros((blk, DH), F32), jnp.zeros((1, blk), F32))
        dk, dv, dcr = lax.fori_loop(j + 1, nb, step, step(j, init, True))
        dk_ref[...] = dk * (DH ** -0.5)
        dv_ref[...] = dv.astype(dv_ref.dtype)
        dcr_ref[...] = dcr

    full = pl.BlockSpec((t, DH), lambda h, j: (0, h))
    colf = pl.BlockSpec((None, t, 1), lambda h, j: (h, 0, 0))
    tile = pl.BlockSpec((blk, DH), lambda h, j: (j, h))
    rowv = pl.BlockSpec((None, None, 1, blk), lambda h, j: (h, j, 0, 0))
    tile_v = pl.BlockSpec((blk, DH), lambda h, j: (j, voff + h))
    big = jax.ShapeDtypeStruct((t, D), F32)
    return pl.pallas_call(
        body, grid=(HEADS, nb),
        in_specs=[full, tile, tile_v, colf, rowv, colf, colf, full],
        out_specs=[tile, tile_v, rowv],
        out_shape=[big, jax.ShapeDtypeStruct(proj.shape, BF16), jax.ShapeDtypeStruct((HEADS, nb, 1, blk), F32)], name=name,
        compiler_params=_params("parallel", "parallel"))(qn, kn, proj, c_col, c_row, lse, delta, do)


def _loss_head(y, target):
    t = y.shape[0]
    tm = min(t, 512)

    def body(y_ref, t_ref, dy_ref, sum_ref):
        err = y_ref[...] - t_ref[...]
        dy_ref[...] = err * (1.0 / D)
        part = jnp.sum(jnp.sum(err * err, axis=1, keepdims=True), axis=0, keepdims=True)

        @pl.when(pl.program_id(0) == 0)
        def _():
            sum_ref[...] = jnp.zeros_like(sum_ref)

        sum_ref[...] += jnp.broadcast_to(part, sum_ref.shape)

    return pl.pallas_call(
        body, grid=(t // tm,), in_specs=[_rows(tm, D), _rows(tm, D)], out_specs=[_rows(tm, D), _const((1, LANE))],
        out_shape=[jax.ShapeDtypeStruct((t, D), F32), jax.ShapeDtypeStruct((1, LANE), F32)], name="loss_head",
        compiler_params=_params("arbitrary"))(y, target)


ADAM_BLOCK_BYTES = 3 * 1024 * 1024


def _adam(name, lands, owns, mine, w, m, v):
    layers = len(lands)
    r, c = owns[0].shape
    block_bytes = lambda rows: N_DEV * rows * c * lands[0].dtype.itemsize
    tr = r
    for cand in (512, 256, 128, 64, 32, 16):
        if block_bytes(tr) > ADAM_BLOCK_BYTES and r % cand == 0:
            tr = cand
    nr = r // tr

    def body(*refs):
        me_ref, land_refs, own_refs = refs[0], refs[1:1 + layers], refs[1 + layers:1 + 2 * layers]
        w_ref, m_ref, v_ref, g_ref, d_ref, nm_ref, nv_ref = refs[1 + 2 * layers:]
        layer = pl.program_id(0)
        for l in range(layers):
            @pl.when(layer == l)
            def _(l=l):
                g = jnp.zeros((tr, c), F32)
                for s in range(N_DEV):
                    g = g + jnp.where(me_ref[0] == s, own_refs[l][...], land_refs[l][s]).astype(F32)
                nm = ADAM_B1 * m_ref[...] + (1.0 - ADAM_B1) * g
                nv = ADAM_B2 * v_ref[...] + (1.0 - ADAM_B2) * (g * g)
                m_hat = nm / (1.0 - ADAM_B1 ** ADAM_STEP)
                v_hat = nv / (1.0 - ADAM_B2 ** ADAM_STEP)
                g_ref[...] = g
                d_ref[...] = -ADAM_LR * (m_hat / (jnp.sqrt(v_hat) + ADAM_EPS) + ADAM_WD * w_ref[...])
                nm_ref[...] = nm
                nv_ref[...] = nv

    at = lambda l: (lambda layer, i: jnp.where(layer == l, i, 0))
    land_specs = [pl.BlockSpec((N_DEV, tr, c), lambda layer, i, f=at(l): (0, f(layer, i), 0)) for l in range(layers)]
    own_specs = [pl.BlockSpec((tr, c), lambda layer, i, f=at(l): (f(layer, i), 0)) for l in range(layers)]
    blk = pl.BlockSpec((tr, c), lambda layer, i: (layer * nr + i, 0))
    out = jax.ShapeDtypeStruct(w.shape, F32)
    return pl.pallas_call(
        body, grid=(layers, nr),
        in_specs=[pl.BlockSpec(memory_space=pltpu.SMEM)] + land_specs + own_specs + [blk, blk, blk],
        out_specs=[blk] * 4, out_shape=[out] * 4, name=name,
        compiler_params=_params("arbitrary", "arbitrary"))(mine, *lands, *owns, w, m, v)


_HBM = pl.BlockSpec(memory_space=pltpu.HBM)
_SEM = pl.BlockSpec(memory_space=pltpu.SEMAPHORE)
_EFFECT = pltpu.SideEffectType.DATAFLOW_SIDE_EFFECTING


def _each_peer(x, y, c):
    flip = lambda v, bit: 1 - v if bit else v
    for p in range(1, N_DEV):
        px, py, pc = flip(x, p & 4), flip(y, p & 2), flip(c, p & 1)
        yield p, (px, py, pc), 4 * px + 2 * py + pc


def _sem(a, p):
    return a * (N_DEV - 1) + p - 1


def _exchange_start(name, arrays, scatter):
    n = len(arrays)
    lands = [lax.empty((N_DEV,) + (a.shape[1:] if sc else a.shape), a.dtype) for a, sc in zip(arrays, scatter)]

    def body(*refs):
        in_refs, land_refs = refs[:n], refs[n:2 * n]
        send_sems, recv_sems, token = refs[2 * n], refs[2 * n + 1], refs[-1]
        x, y, c = lax.axis_index("x"), lax.axis_index("y"), lax.axis_index("c")
        me = 4 * x + 2 * y + c
        for p, coords, peer in _each_peer(x, y, c):
            for a in range(n):
                pltpu.make_async_remote_copy(
                    src_ref=in_refs[a].at[peer] if scatter[a] else in_refs[a], dst_ref=land_refs[a].at[me],
                    send_sem=send_sems.at[_sem(a, p)], recv_sem=recv_sems.at[_sem(a, p)], device_id=coords,
                    device_id_type=pl.DeviceIdType.MESH).start()
        token[...] = jnp.zeros_like(token)

    sems = pltpu.SemaphoreType.DMA((n * (N_DEV - 1),))
    hbm = lambda a: pltpu.HBM(a.shape, a.dtype)
    out = pl.pallas_call(
        body, name=name,
        out_shape=(sems, sems, *[hbm(a) for a in arrays], *[hbm(l) for l in lands],
                   jax.ShapeDtypeStruct((8, LANE), F32)),
        in_specs=[_HBM] * (2 * n), out_specs=(_SEM, _SEM, *[_HBM] * (2 * n), pl.BlockSpec(memory_space=pltpu.VMEM)),
        input_output_aliases={i: 2 + i for i in range(2 * n)},
        compiler_params=pltpu.CompilerParams(has_side_effects=_EFFECT),
    )(*[pltpu.with_memory_space_constraint(a, pltpu.HBM) for a in arrays],
      *[pltpu.with_memory_space_constraint(l, pltpu.HBM) for l in lands])
    return (out[0], out[1], list(out[2:2 + n]), list(out[2 + n:2 + 2 * n]), scatter), out[-1]


def _exchange_wait(name, started, after, me):
    send_sems, recv_sems, sent, lands, scatter = started
    n = len(sent)

    def body(*refs):
        in_refs, land_refs = refs[:n], refs[n:2 * n]
        send_sems, recv_sems = refs[2 * n], refs[2 * n + 1]
        x, y, c = lax.axis_index("x"), lax.axis_index("y"), lax.axis_index("c")
        for p, coords, peer in _each_peer(x, y, c):
            for a in range(n):
                cp = pltpu.make_async_remote_copy(
                    src_ref=in_refs[a].at[peer] if scatter[a] else in_refs[a], dst_ref=land_refs[a].at[peer],
                    send_sem=send_sems.at[_sem(a, p)], recv_sem=recv_sems.at[_sem(a, p)], device_id=coords,
                    device_id_type=pl.DeviceIdType.MESH)
                cp.wait_send()
                cp.wait_recv()

    hbm = lambda a: pltpu.HBM(a.shape, a.dtype)
    out = pl.pallas_call(
        body, name=name, out_shape=(*[hbm(a) for a in sent], *[hbm(l) for l in lands]),
        in_specs=[_HBM] * (2 * n) + [_SEM, _SEM, pl.BlockSpec(memory_space=pl.ANY)], out_specs=[_HBM] * (2 * n),
        input_output_aliases={i: i for i in range(2 * n)},
        compiler_params=pltpu.CompilerParams(has_side_effects=_EFFECT),
    )(*sent, *lands, send_sems, recv_sems, after)
    owns = [lax.dynamic_index_in_dim(out[a], me, 0, keepdims=False) if scatter[a] else out[a] for a in range(n)]
    return list(out[n:]), owns


def _fill_own(land, own, me):
    slot = lax.broadcasted_iota(jnp.int32, (N_DEV,) + (1,) * own.ndim, 0)
    return jnp.where(slot == me, own[None], land)


def _rms_fwd(name, x, g):
    t = x.shape[0]
    tm = min(t, 512)
    return _seg_fwd(name, _f_rms, (t // tm,), [x, g], [_rows(tm, D), _const((1, D))],
                    [jax.ShapeDtypeStruct((t, D), BF16)], [_rows(tm, D)])[0]


def _rms_bwd(name, x, g, dh, dres):
    t = x.shape[0]
    tm = min(t, 512)
    return _seg_bwd(name, _f_rms_res, (t // tm,), [x, g], [_rows(tm, D), _const((1, D))],
                    [dh, dres], [_rows(tm, D), _rows(tm, D)], [F32, "acc"])


FFN_TC = 256
CONF_TC = 128


def _ffn_specs(t):
    nf = D_FF // FFN_TC
    return nf, [_cols(t, 2 * FFN_TC), _cols(FFN_K, 2 * FFN_TC)], [_cols(t, FFN_TC)]


def _ffn_fwd(tag, x, p):
    t = x.shape[0]
    nf, in_specs, out_specs = _ffn_specs(t)
    h = _rms_fwd(tag + "_rms", x, p["g"])
    u0 = _mm(tag + "_up", h, p["w_up"])
    act = _seg_fwd(tag + "_mid", _f_ffn_mid, (nf,), [u0, p["w_dw"]], in_specs,
                   [jax.ShapeDtypeStruct((t, D_FF), BF16)], out_specs)[0]
    out = _mm(tag + "_down", act, p["w_down"], res=x)
    return out, (x, h, u0, act)


def _ffn_bwd(tag, saved, p, dout, after=None):
    x, h, u0, act = saved
    t = x.shape[0]
    nf, in_specs, out_specs = _ffn_specs(t)
    g = {"w_down": _mm(tag + "_dwdown", act, dout, ta=True, out_dtype=BF16, after=after)}
    dact = _mm(tag + "_dact", dout, p["w_down"], tb=True, out_dtype=BF16)
    du0, g["w_dw"] = _seg_bwd(tag + "_dmid", _f_ffn_mid, (nf, 1), [u0, p["w_dw"]], _grid2(in_specs), [dact],
                              _grid2(out_specs), [BF16, "acc"])
    g["w_up"] = _mm(tag + "_dwup", h, du0, ta=True, out_dtype=BF16)
    dh = _mm(tag + "_dh", du0, p["w_up"], tb=True, out_dtype=BF16)
    dx, g["g"] = _rms_bwd(tag + "_drms", x, p["g"], dh, dout)
    return dx, g


def _conf_specs(t):
    tm = min(t, 512)
    nc = D // CONF_TC
    vec = _const((1, D))
    glu_in = [_cols(t, 2 * CONF_TC), _cols(1, 2 * CONF_TC), _cols(CONF_K, CONF_TC)]
    return tm, nc, glu_in, [_cols(t, CONF_TC)], [_rows(tm, D), vec, vec, vec]


def _conf_fwd(tag, x, p):
    t = x.shape[0]
    tm, nc, glu_in, glu_out, ln_in = _conf_specs(t)
    h = _rms_fwd(tag + "_rms", x, p["g"])
    u = _mm(tag + "_in", h, p["w_in"])
    cv = _seg_fwd(tag + "_gluconv", _f_conf_glu_conv, (nc,), [u, p["b_in"], p["w_dw"]], glu_in,
                  [jax.ShapeDtypeStruct((t, D), F32)], glu_out)[0]
    act = _seg_fwd(tag + "_lnsilu", _f_conf_ln_silu, (t // tm,), [cv, p["b_dw"], p["ln_g"], p["ln_b"]], ln_in,
                   [jax.ShapeDtypeStruct((t, D), BF16)], [_rows(tm, D)])[0]
    out = _mm(tag + "_out", act, p["w_out"], res=x)
    return out, (x, h, u, cv, act)


def _conf_bwd(tag, saved, p, dout, after=None):
    x, h, u, cv, act = saved
    t = x.shape[0]
    tm, nc, glu_in, glu_out, ln_in = _conf_specs(t)
    g = {"w_out": _mm(tag + "_dwout", act, dout, ta=True, out_dtype=BF16, after=after)}
    dact = _mm(tag + "_dact", dout, p["w_out"], tb=True, out_dtype=BF16)
    dcv, g["b_dw"], g["ln_g"], g["ln_b"] = _seg_bwd(
        tag + "_dlnsilu", _f_conf_ln_silu, (t // tm,), [cv, p["b_dw"], p["ln_g"], p["ln_b"]], ln_in, [dact],
        [_rows(tm, D)], [F32, "acc", "acc", "acc"])
    du, g["b_in"], g["w_dw"] = _seg_bwd(tag + "_dgluconv", _f_conf_glu_conv, (nc, 1), [u, p["b_in"], p["w_dw"]],
                                        _grid2(glu_in), [dcv], _grid2(glu_out), [BF16, "acc", "acc"])
    g["w_in"] = _mm(tag + "_dwin", h, du, ta=True, out_dtype=BF16)
    dh = _mm(tag + "_dh", du, p["w_in"], tb=True, out_dtype=BF16)
    dx, g["g"] = _rms_bwd(tag + "_drms", x, p["g"], dh, dout)
    return dx, g


def _gdn_specs(t):
    tc, tm, rows = 256, min(t, 256), min(t, 512)
    nq = 3 * D // tc
    conv = ([_cols(t, tc), _cols(GDN_K, tc)], [_cols(t, tc)])
    gate_in = [_rows(tm, 2 * D, 0), _rows(tm, LANE, 4 * D // LANE), _const((1, LANE)), _const((1, LANE))]
    gate_out = [_rows(tm, D)] * 4
    head = pl.BlockSpec((rows, DH), lambda h, i: (i, h))
    headv = pl.BlockSpec((rows, DH), lambda h, i: (i, 2 * HEADS + h))
    nch = rows // GDN_CHUNK
    pre_in = [head, head, headv, head, head]
    pre_out = [head, head, head, pl.BlockSpec((None, nch, GDN_CHUNK, GDN_CHUNK), lambda h, i: (h, i, 0, 0)), head,
               pl.BlockSpec((None, nch, 1, DH), lambda h, i: (h, i, 0, 0))]
    post_in = [_rows(tm, D), _rows(tm, D, 3), _const((1, DH))]
    return tm, rows, nq, conv, gate_in, gate_out, pre_in, pre_out, post_in


def _gdn_fwd(tag, x, p):
    t = x.shape[0]
    tm, rows, nq, conv, gate_in, gate_out, pre_in, pre_out, post_in = _gdn_specs(t)
    n = t // GDN_CHUNK
    big = jax.ShapeDtypeStruct((t, D), F32)
    h = _rms_fwd(tag + "_rms", x, p["g"])
    proj = _mm(tag + "_in", h, p["w_in"])
    qkv = _seg_fwd(tag + "_conv", _f_conv_silu, (nq,), [proj, p["conv_w"]], conv[0],
                   [jax.ShapeDtypeStruct((t, 3 * D), F32)], conv[1])[0]
    gate_ins = [qkv, proj, p["a_log"], p["dt_bias"]]
    qn, kn, gb, bb = _seg_fwd(tag + "_gates", _f_gdn_gates, (t // tm,), gate_ins, gate_in, [big] * 4, gate_out)
    pre_ins = [qn, kn, qkv, gb, bb]
    pre_shapes = [big, big, big, jax.ShapeDtypeStruct((HEADS, n, GDN_CHUNK, GDN_CHUNK), F32), big,
                  jax.ShapeDtypeStruct((HEADS, n, 1, DH), F32)]
    pre = _seg_fwd(tag + "_prescan", _f_gdn_prescan, (HEADS, t // rows), pre_ins, pre_in, pre_shapes, pre_out)
    o, states = _gdn_scan_fwd(tag + "_scan", *pre)
    post_ins = [o, proj, p["o_g"]]
    act = _seg_fwd(tag + "_post", _f_gdn_post, (t // tm,), post_ins, post_in, [jax.ShapeDtypeStruct((t, D), BF16)],
                   [_rows(tm, D)])[0]
    out = _mm(tag + "_out", act, p["w_out"], res=x)
    return out, (x, h, proj, gate_ins, pre_ins, pre, states, post_ins, act)


def _gdn_bwd(tag, saved, p, dout, after=None):
    x, h, proj, gate_ins, pre_ins, pre, states, post_ins, act = saved
    t = x.shape[0]
    tm, rows, nq, conv, gate_in, gate_out, pre_in, pre_out, post_in = _gdn_specs(t)
    g = {"w_out": _mm(tag + "_dwout", act, dout, ta=True, out_dtype=BF16, after=after)}
    dact = _mm(tag + "_dact", dout, p["w_out"], tb=True, out_dtype=BF16)
    do, dproj, g["o_g"] = _seg_bwd(tag + "_dpost", _f_gdn_post, (t // tm,), post_ins, post_in, [dact],
                                   [_rows(tm, D)], [F32, BF16, "acc"])
    dpre = _gdn_scan_bwd(tag + "_dscan", *pre, states, do)
    dqn, dkn, dqkv, dgb, dbb = _seg_bwd(tag + "_dprescan", _f_gdn_prescan, (HEADS, t // rows), pre_ins, pre_in,
                                        dpre, pre_out, [F32] * 5)
    dqkv, dproj, g["a_log"], g["dt_bias"] = _seg_bwd(
        tag + "_dgates", _f_gdn_gates, (t // tm,), gate_ins, gate_in, [dqn, dkn, dgb, dbb], gate_out,
        [F32, BF16, "acc", "acc"], into={0: dqkv, 1: dproj})
    dproj, g["conv_w"] = _seg_bwd(tag + "_dconv", _f_conv_silu, (nq, 1), [proj, p["conv_w"]], _grid2(conv[0]),
                                  [dqkv], _grid2(conv[1]), [BF16, "acc"], into={0: dproj})
    g["w_in"] = _mm(tag + "_dwin", h, dproj, ta=True, out_dtype=BF16)
    dh = _mm(tag + "_dh", dproj, p["w_in"], tb=True, out_dtype=BF16)
    dx, g["g"] = _rms_bwd(tag + "_drms", x, p["g"], dh, dout)
    return dx, g


def _fox_specs(t):
    tm = min(t, 512)
    vec = _const((1, LANE))
    pre_in = [_rows(tm, 2 * D + LANE, 0), vec, vec, vec]
    pre_out = [_rows(tm, D), _rows(tm, D), _rows(tm, LANE)]
    return tm, pre_in, pre_out


def _per_head(c):
    return jnp.transpose(c[:, :HEADS])


def _per_lane(ch):
    return jnp.pad(jnp.transpose(ch), ((0, 0), (0, LANE - HEADS)))


def _fox_fwd_layer(tag, x, p):
    t = x.shape[0]
    tm, pre_in, pre_out = _fox_specs(t)
    blk, nb = _fox_blocks(t)
    h = _rms_fwd(tag + "_rms", x, p["g"])
    proj = _mm(tag + "_in", h, p["w_in"])
    pre_ins = [proj, p["q_g"], p["k_g"], p["b_f"]]
    qn, kn, lf = _seg_fwd(tag + "_pre", _f_fox_pre, (t // tm,), pre_ins, pre_in,
                          [jax.ShapeDtypeStruct((t, D), BF16)] * 2 + [jax.ShapeDtypeStruct((t, LANE), F32)], pre_out)
    ch = _per_head(_cumsum_rows(tag + "_cumsum", [lf], False))
    c_col, c_row = ch.reshape(HEADS, t, 1), ch.reshape(HEADS, nb, 1, blk)
    o, lse = _fox_fwd(tag + "_attn", qn, kn, proj, c_col, c_row)
    out = _mm(tag + "_out", o, p["w_out"], res=x)
    return out, (x, h, proj, pre_ins, qn, kn, c_col, c_row, o, lse)


def _fox_bwd_layer(tag, saved, p, dout, after=None):
    x, h, proj, pre_ins, qn, kn, c_col, c_row, o, lse = saved
    t = x.shape[0]
    tm, pre_in, pre_out = _fox_specs(t)
    g = {"w_out": _mm(tag + "_dwout", o, dout, ta=True, out_dtype=BF16, after=after)}
    do = _mm(tag + "_do", dout, p["w_out"], tb=True)
    dqn, dc_col, delta = _fox_dq(tag + "_dq", qn, kn, proj, c_col, c_row, o, lse, do)
    dkn, dproj, dc_row = _fox_dkv(tag + "_dkv", qn, kn, proj, c_col, c_row, lse, delta, do)
    dlf = _cumsum_rows(tag + "_dcumsum", [_per_lane(dc_col.reshape(HEADS, t)), _per_lane(dc_row.reshape(HEADS, t))],
                       True)
    dproj, g["q_g"], g["k_g"], g["b_f"] = _seg_bwd(
        tag + "_dpre", _f_fox_pre, (t // tm,), pre_ins, pre_in, [dqn, dkn, dlf], pre_out,
        [BF16, "acc", "acc", "acc"], into={0: dproj})
    g["w_in"] = _mm(tag + "_dwin", h, dproj, ta=True, out_dtype=BF16)
    dh = _mm(tag + "_dh", dproj, p["w_in"], tb=True, out_dtype=BF16)
    dx, g["g"] = _rms_bwd(tag + "_drms", x, p["g"], dh, dout)
    return dx, g


_MIXERS = ((_conf_fwd, _conf_bwd), (_gdn_fwd, _gdn_bwd), (_fox_fwd_layer, _fox_bwd_layer))


def _local_step(x, target, params_of, on_grads):
    saved, params = [], []
    for i in range(DEPTH):
        mp = params_of(i, 0, x)
        x, sm = _MIXERS[i % N_MIXERS][0](f"l{i}_mix", x, mp)
        fp = params_of(i, 1, x)
        x, sf = _ffn_fwd(f"l{i}_ffn", x, fp)
        saved.append((sm, sf))
        params.append((mp, fp))
    dx, sq = _loss_head(x, target)
    after = None
    for i in reversed(range(DEPTH)):
        dx, gf = _ffn_bwd(f"l{i}_ffn", saved[i][1], params[i][1], dx, after)
        after = on_grads(i, 1, gf)
        dx, gm = _MIXERS[i % N_MIXERS][1](f"l{i}_mix", saved[i][0], params[i][0], dx, after)
        after = on_grads(i, 0, gm)
    return sq, dx


def _unshard(name, g):
    axis = g.ndim - 2 if name in ROW_SHARDED else g.ndim - 1
    m = jnp.moveaxis(g, 0, axis - 1)
    return m.reshape(m.shape[:axis - 1] + (N_DEV * m.shape[axis],) + m.shape[axis + 1:])


def _reshard(name, full):
    axis = full.ndim - 2 if name in ROW_SHARDED else full.ndim - 1
    s = full.shape
    return jnp.moveaxis(full.reshape(s[:axis] + (N_DEV, s[axis] // N_DEV) + s[axis + 1:]), axis, 0)


def _pad_cols(a, width):
    return jnp.pad(a, [(0, 0)] * (a.ndim - 1) + [(0, width - a.shape[-1])])


def _lane_vec(v):
    return _pad_cols(v.reshape(1, -1), LANE)


REP_ROWS = 16


def _pack_rep(r):
    small = [_pad_cols(r[n].reshape(1, -1), LANE) for n in REPLICATED[2:]]
    row = jnp.concatenate(small + [jnp.zeros((1, D - LANE * len(small)), F32)], axis=1)
    pad = jnp.zeros((REP_ROWS - 2 * DEPTH - 1, D), F32)
    return jnp.concatenate([r['mix_norm_g'].reshape(DEPTH, D), r['ffn_norm_g'].reshape(DEPTH, D), row, pad], axis=0)


def _unpack_rep(a, shapes):
    out = {'mix_norm_g': a[:DEPTH], 'ffn_norm_g': a[DEPTH:2 * DEPTH]}
    for i, n in enumerate(REPLICATED[2:]):
        out[n] = a[2 * DEPTH:2 * DEPTH + 1, i * LANE:i * LANE + shapes[n][1]].reshape(shapes[n])
    return out


def _view2d(a):
    return a.reshape(-1, a.shape[-1])


MIXER_SHARDED = (
    (('conv_w_in', 'w_in'), ('conv_w_out', 'w_out'), ('conv_b_in', 'b_in'), ('conv_w_dw', 'w_dw'),
     ('conv_b_dw', 'b_dw'), ('conv_ln_g', 'ln_g'), ('conv_ln_b', 'ln_b')),
    (('gdn_w_in', 'w_in'), ('gdn_w_out', 'w_out'), ('gdn_conv_w', 'conv_w')),
    (('fox_w_in', 'w_in'), ('fox_w_out', 'w_out')),
)
FFN_SHARDED = (('ffn_w_up', 'w_up'), ('ffn_w_down', 'w_down'), ('ffn_w_dw', 'w_dw'))
MIXER_REPLICATED = (
    (),
    (('gdn_a_log', 'a_log'), ('gdn_dt_bias', 'dt_bias'), ('gdn_o_norm_g', 'o_g')),
    (('fox_b_f', 'b_f'), ('fox_q_norm_g', 'q_g'), ('fox_k_norm_g', 'k_g')),
)
ROW_VECTORS = ('conv_b_in', 'conv_b_dw', 'conv_ln_g', 'conv_ln_b')
INTERLEAVED = {'ffn_w_up': (D_FF, FFN_TC), 'ffn_w_dw': (D_FF, FFN_TC), 'conv_w_in': (D, CONF_TC),
               'conv_b_in': (D, CONF_TC)}
PACK_GROUP = 16 * D


def _part_entries(i, part):
    ent = [(n, i, k) for n, k in FFN_SHARDED] if part else [(n, i // N_MIXERS, k) for n, k in MIXER_SHARDED[i % N_MIXERS]]
    return [e for e in ent if e[0] in MATRICES], [e for e in ent if e[0] not in MATRICES]


def _interleave(a, half, blk):
    s = a.shape[:-1]
    return jnp.swapaxes(a.reshape(s + (2, half // blk, blk)), -3, -2).reshape(s + (2 * half,))


def _deinterleave(a, half, blk):
    s = a.shape[:-1]
    return jnp.swapaxes(a.reshape(s + (half // blk, 2, blk)), -3, -2).reshape(s + (2 * half,))


def _to_param(name, whole):
    if name in ROW_VECTORS:
        whole = whole[None]
    if name in INTERLEAVED:
        return _interleave(whole, *INTERLEAVED[name])
    if name == 'gdn_w_in':
        return _pad_cols(whole, GDN_PAD)
    if name == 'fox_w_in':
        return jnp.concatenate([whole[:, :2 * D], _pad_cols(whole[:, 3 * D:], LANE), whole[:, 2 * D:3 * D]], axis=1)
    return whole


def _from_grad(name, g):
    if name in INTERLEAVED:
        g = _deinterleave(g, *INTERLEAVED[name])
    if name in ROW_VECTORS:
        return g[0]
    if name == 'gdn_w_in':
        return g[:, :4 * D + 2 * HEADS]
    if name == 'fox_w_in':
        return jnp.concatenate([g[:, :2 * D], g[:, 2 * D + LANE:], g[:, 2 * D:2 * D + HEADS]], axis=1)
    return g


def _layer_col(name, n):
    if name in INTERLEAVED:
        half, blk = INTERLEAVED[name]
        return (n % half) // blk * (2 * blk) + (n // half) * blk + n % blk
    if name == 'fox_w_in':
        return np.where(n < 2 * D, n, np.where(n < 3 * D, n + LANE, n - D))
    return n


def _col_runs(name, shard_cols):
    dst = _layer_col(name, np.arange(N_DEV * shard_cols))
    runs, start = [[] for _ in range(N_DEV)], 0
    for k in range(1, dst.size + 1):
        if k == dst.size or dst[k] != dst[k - 1] + 1 or k % shard_cols == 0:
            runs[start // shard_cols].append((start % shard_cols, k - start, int(dst[start])))
            start = k
    width = {'gdn_w_in': GDN_PAD, 'fox_w_in': FOX_PAD}.get(name, dst.size)
    free = np.ones(width + 1, bool)
    free[dst] = False
    free[width] = False
    gaps, start = [], None
    for k in range(width + 1):
        if free[k] and start is None:
            start = k
        if not free[k] and start is not None:
            gaps.append((start, k - start))
            start = None
    return runs, width, gaps


def _unshard_cols(call, name, land, own, mine):
    _, r, c = land.shape
    runs, width, gaps = _col_runs(name, c)
    tr = _tile(r, 256)

    def body(me_ref, land_ref, own_ref, o_ref):
        for s in range(N_DEV):
            for src, ln, dst in runs[s]:
                o_ref[:, dst:dst + ln] = jnp.where(me_ref[0] == s, own_ref[:, src:src + ln], land_ref[s, :, src:src + ln])
        for start, ln in gaps:
            o_ref[:, start:start + ln] = jnp.zeros((tr, ln), o_ref.dtype)

    return pl.pallas_call(
        body, grid=(r // tr,),
        in_specs=[pl.BlockSpec(memory_space=pltpu.SMEM), pl.BlockSpec((N_DEV, tr, c), lambda i: (0, i, 0)),
                  pl.BlockSpec((tr, c), lambda i: (i, 0))],
        out_specs=pl.BlockSpec((tr, width), lambda i: (i, 0)), out_shape=jax.ShapeDtypeStruct((r, width), land.dtype),
        name=call, compiler_params=_params("parallel"))(mine, land, own)


def _reshard_cols(call, name, g, shard_cols):
    r = g.shape[0]
    runs, width, _ = _col_runs(name, shard_cols)
    tr = _tile(r, 256)

    def body(g_ref, o_ref):
        for s in range(N_DEV):
            for src, ln, dst in runs[s]:
                o_ref[s, :, src:src + ln] = g_ref[:, dst:dst + ln]

    return pl.pallas_call(
        body, grid=(r // tr,), in_specs=[pl.BlockSpec((tr, width), lambda i: (i, 0))],
        out_specs=pl.BlockSpec((N_DEV, tr, shard_cols), lambda i: (0, i, 0)),
        out_shape=jax.ShapeDtypeStruct((N_DEV, r, shard_cols), g.dtype), name=call,
        compiler_params=_params("parallel"))(g)


def _pack_rows(parts, lead):
    out = []
    for a in parts:
        flat = a.reshape(a.shape[:lead] + (-1,))
        size = flat.shape[-1]
        padded = -(-size // PACK_GROUP) * PACK_GROUP
        flat = jnp.pad(flat, [(0, 0)] * lead + [(0, padded - size)])
        out.append(flat.reshape(a.shape[:lead] + (padded // D, D)))
    return jnp.concatenate(out, axis=lead)


def _unpack_rows(packed, shapes, lead):
    out, row = [], 0
    head = packed.shape[:lead]
    for s in shapes:
        size = 1
        for d in s:
            size *= d
        rows = -(-size // PACK_GROUP) * (PACK_GROUP // D)
        part = lax.slice_in_dim(packed, row, row + rows, axis=lead)
        out.append(part.reshape(head + (rows * D,))[..., :size].reshape(head + tuple(s)))
        row += rows
    return out


def _part_dict(i, part, whole, rep):
    if part:
        p = {k: _to_param(n, whole[n]) for n, k in FFN_SHARDED if n in whole}
        p["g"] = rep['ffn_norm_g'][i][None]
        return p
    kind, j = i % N_MIXERS, i // N_MIXERS
    p = {k: _to_param(n, whole[n]) for n, k in MIXER_SHARDED[kind] if n in whole}
    for n, k in MIXER_REPLICATED[kind]:
        p[k] = rep[n][j][None] if rep[n].shape[-1] == DH else _lane_vec(rep[n][j])
    p["g"] = rep['mix_norm_g'][i][None]
    return p


def _part_grads(i, part, g):
    return {n: _from_grad(n, g[k]) for n, k in (FFN_SHARDED if part else MIXER_SHARDED[i % N_MIXERS])}


def _replicated_grads(grads):
    rep = {'mix_norm_g': jnp.concatenate([g[0]["g"] for g in grads]),
           'ffn_norm_g': jnp.concatenate([g[1]["g"] for g in grads])}
    for kind in range(N_MIXERS):
        for n, k in MIXER_REPLICATED[kind]:
            rep[n] = jnp.stack([grads[i][0][k] for i in range(kind, DEPTH, N_MIXERS)])
    return rep


def kernel(x, mix_norm_g, ffn_norm_g, conv_w_in, conv_b_in, conv_w_dw, conv_b_dw, conv_ln_g, conv_ln_b, conv_w_out, gdn_w_in, gdn_conv_w, gdn_a_log, gdn_dt_bias, gdn_o_norm_g, gdn_w_out, fox_w_in, fox_b_f, fox_q_norm_g, fox_k_norm_g, fox_w_out, ffn_w_up, ffn_w_dw, ffn_w_down, loss_target, m_mix_norm_g, m_ffn_norm_g, m_conv_w_in, m_conv_b_in, m_conv_w_dw, m_conv_b_dw, m_conv_ln_g, m_conv_ln_b, m_conv_w_out, m_gdn_w_in, m_gdn_conv_w, m_gdn_a_log, m_gdn_dt_bias, m_gdn_o_norm_g, m_gdn_w_out, m_fox_w_in, m_fox_b_f, m_fox_q_norm_g, m_fox_k_norm_g, m_fox_w_out, m_ffn_w_up, m_ffn_w_dw, m_ffn_w_down, v_mix_norm_g, v_ffn_norm_g, v_conv_w_in, v_conv_b_in, v_conv_w_dw, v_conv_b_dw, v_conv_ln_g, v_conv_ln_b, v_conv_w_out, v_gdn_w_in, v_gdn_conv_w, v_gdn_a_log, v_gdn_dt_bias, v_gdn_o_norm_g, v_gdn_w_out, v_fox_w_in, v_fox_b_f, v_fox_q_norm_g, v_fox_k_norm_g, v_fox_w_out, v_ffn_w_up, v_ffn_w_dw, v_ffn_w_down):
    given = dict(zip(
        WEIGHTS + ["m_" + n for n in WEIGHTS] + ["v_" + n for n in WEIGHTS],
        (mix_norm_g, ffn_norm_g, conv_w_in, conv_b_in, conv_w_dw, conv_b_dw, conv_ln_g, conv_ln_b, conv_w_out, gdn_w_in, gdn_conv_w, gdn_a_log, gdn_dt_bias, gdn_o_norm_g, gdn_w_out, fox_w_in, fox_b_f, fox_q_norm_g, fox_k_norm_g, fox_w_out, ffn_w_up, ffn_w_dw, ffn_w_down,
         m_mix_norm_g, m_ffn_norm_g, m_conv_w_in, m_conv_b_in, m_conv_w_dw, m_conv_b_dw, m_conv_ln_g, m_conv_ln_b, m_conv_w_out, m_gdn_w_in, m_gdn_conv_w, m_gdn_a_log, m_gdn_dt_bias, m_gdn_o_norm_g, m_gdn_w_out, m_fox_w_in, m_fox_b_f, m_fox_q_norm_g, m_fox_k_norm_g, m_fox_w_out, m_ffn_w_up, m_ffn_w_dw, m_ffn_w_down,
         v_mix_norm_g, v_ffn_norm_g, v_conv_w_in, v_conv_b_in, v_conv_w_dw, v_conv_b_dw, v_conv_ln_g, v_conv_ln_b, v_conv_w_out, v_gdn_w_in, v_gdn_conv_w, v_gdn_a_log, v_gdn_dt_bias, v_gdn_o_norm_g, v_gdn_w_out, v_fox_w_in, v_fox_b_f, v_fox_q_norm_g, v_fox_k_norm_g, v_fox_w_out, v_ffn_w_up, v_ffn_w_dw, v_ffn_w_down)))

    me = 4 * lax.axis_index("x") + 2 * lax.axis_index("y") + lax.axis_index("c")
    mine = me.astype(jnp.int32).reshape(1)

    gathers, token = {}, jnp.zeros((1, 1), F32)
    for i in range(DEPTH):
        for part in (0, 1):
            mats, smalls = _part_entries(i, part)
            sent = [given[n][j].astype(BF16) for n, j, _ in mats]
            if smalls:
                sent.append(_pack_rows([given[n][j] for n, j, _ in smalls], 0))
            gathers[i, part], tok = _exchange_start(f"gather{i}{'mf'[part]}_start", sent, [False] * len(sent))
            token = token + tok[0:1, 0:1]

    def params_of(i, part, x_in):
        mats, smalls = _part_entries(i, part)
        after = token if (i, part) == (0, 0) else x_in
        lands, owns = _exchange_wait(f"gather{i}{'mf'[part]}_wait", gathers[i, part], after, me)
        whole, relaid = {}, {}
        for (n, _, k), land, own in zip(mats, lands, owns):
            if n in ROW_SHARDED:
                whole[n] = _unshard(n, _fill_own(land, own, me))
            else:
                relaid[k] = _unshard_cols(f"l{i}_{n}_unshard", n, land, own, mine)
        if smalls:
            shapes = [given[n].shape[1:] for n, _, _ in smalls]
            for (n, _, _), g in zip(smalls, _unpack_rows(_fill_own(lands[-1], owns[-1], me), shapes, 1)):
                whole[n] = _unshard(n, g)
        return {**_part_dict(i, part, whole, given), **relaid}

    grads, exchanges = [[None, None] for _ in range(DEPTH)], {}

    def on_grads(i, part, g):
        grads[i][part] = g
        mats, smalls = _part_entries(i, part)
        sent = [_reshard(n, g[k]) if n in ROW_SHARDED else
                _reshard_cols(f"l{i}_{n}_reshard", n, g[k], given[n].shape[-1]) for n, _, k in mats]
        if smalls:
            sent.append(_pack_rows([_reshard(n, _from_grad(n, g[k])) for n, _, k in smalls], 1))
        last = (i, part) == (0, 0)
        if last:
            sent.append(_pack_rep(_replicated_grads(grads)))
        exchanges[i, part], tok = _exchange_start(f"grads{i}{'mf'[part]}_start", sent,
                                                  [True] * (len(sent) - last) + [False] * last)
        tokens.append(tok)
        return tok

    tokens = []
    sq, dx = _local_step(x[0], loss_target[0], params_of, on_grads)
    loss = (0.5 / D) * lax.psum(sq[0, 0], ("x", "y", "c"))

    pieces = {n: [] for n in SHARDED}
    for i in range(DEPTH):
        for part in (0, 1):
            mats, smalls = _part_entries(i, part)
            lands, owns = _exchange_wait(f"grads{i}{'mf'[part]}_wait", exchanges[i, part], tokens[-1], me)
            if (i, part) == (0, 0):
                rep_piece = (lands[-1], owns[-1])
            for (n, _, _), land, own in zip(mats, lands, owns):
                pieces[n].append((land, own))
            if smalls:
                shapes = [given[n].shape[1:] for n, _, _ in smalls]
                for (n, _, _), land, own in zip(smalls, _unpack_rows(lands[len(mats)], shapes, 1),
                                                _unpack_rows(owns[len(mats)], shapes, 0)):
                    pieces[n].append((land, own))
    new = {}
    for n in SHARDED:
        lands = [l.reshape((N_DEV, -1, l.shape[-1])) for l, _ in pieces[n]]
        owns = [o.reshape((-1, o.shape[-1])) for _, o in pieces[n]]
        if owns[0].shape[0] % 8:
            lands = [jnp.stack(lands, axis=1).reshape((N_DEV, -1, lands[0].shape[-1]))]
            owns = [jnp.stack(owns).reshape((-1, owns[0].shape[-1]))]
        outs = _adam("adam_" + n, lands, owns, mine, _view2d(given[n]), _view2d(given["m_" + n]),
                     _view2d(given["v_" + n]))
        new[n] = [o.reshape(given[n].shape) for o in outs]
    packed = [_pack_rep({n: given[pre + n] for n in REPLICATED}) for pre in ("", "m_", "v_")]
    outs = _adam("adam_replicated", [rep_piece[0]], [rep_piece[1]], mine, *packed)
    unpacked = [_unpack_rep(o, {n: given[n].shape for n in REPLICATED}) for o in outs]
    for n in REPLICATED:
        new[n] = [u[n] for u in unpacked]
    return (loss, dx[None], *[new[n][0] for n in WEIGHTS], *[new[n][1] for n in WEIGHTS],
            *[new[n][2] for n in WEIGHTS], *[new[n][3] for n in WEIGHTS])
```

```python
import functools

import jax
import jax.numpy as jnp
import numpy as np
from jax import lax
from jax.experimental import pallas as pl
from jax.experimental.pallas import tpu as pltpu

F32 = jnp.float32
BF16 = jnp.bfloat16
HIGHEST = lax.Precision.HIGHEST
HIGH = lax.Precision.HIGH

N_DEV = 8
LANE = 128
EPS = 1e-6
DEPTH = 4
N_MIXERS = 3
HEADS = 8
DH = 128
D = HEADS * DH
D_FF = 2816
CONF_K, GDN_K, FFN_K = 31, 4, 3
GDN_CHUNK = 64
GDN_PAD = 4224
FOX_PAD = 3200
FOX_V_BLOCK = (2 * D + LANE) // DH
ADAM_LR, ADAM_B1, ADAM_B2, ADAM_EPS, ADAM_WD, ADAM_STEP = 0.001, 0.9, 0.999, 1e-08, 0.01, 10
VMEM_LIMIT = 56 * 1024 * 1024

WEIGHTS = ['mix_norm_g', 'ffn_norm_g', 'conv_w_in', 'conv_b_in', 'conv_w_dw', 'conv_b_dw', 'conv_ln_g', 'conv_ln_b',
           'conv_w_out', 'gdn_w_in', 'gdn_conv_w', 'gdn_a_log', 'gdn_dt_bias', 'gdn_o_norm_g', 'gdn_w_out', 'fox_w_in',
           'fox_b_f', 'fox_q_norm_g', 'fox_k_norm_g', 'fox_w_out', 'ffn_w_up', 'ffn_w_dw', 'ffn_w_down']
REPLICATED = ['mix_norm_g', 'ffn_norm_g', 'gdn_a_log', 'gdn_dt_bias', 'gdn_o_norm_g', 'fox_b_f', 'fox_q_norm_g',
              'fox_k_norm_g']
ROW_SHARDED = ['conv_w_out', 'gdn_w_out', 'fox_w_out', 'ffn_w_down']
MATRICES = ['conv_w_in', 'conv_w_out', 'gdn_w_in', 'gdn_w_out', 'fox_w_in', 'fox_w_out', 'ffn_w_up', 'ffn_w_down']
SHARDED = [n for n in WEIGHTS if n not in REPLICATED]


def _params(*sem):
    return pltpu.CompilerParams(dimension_semantics=sem, vmem_limit_bytes=VMEM_LIMIT)


def _tile(n, cap):
    if n <= cap:
        return n
    d = (cap // LANE) * LANE
    while d >= LANE:
        if n % d == 0:
            return d
        d -= LANE
    raise ValueError(f"no lane-aligned tile of {n} under {cap}")


def _raw_dot(a, b, ca, cb, hp):
    batch = ((0,), (0,)) if a.ndim == 3 else ((), ())
    dn = (((ca,), (cb,)), batch)
    if hp:
        return lax.dot_general(a.astype(F32), b.astype(F32), dn, precision=HIGH, preferred_element_type=F32)
    return lax.dot_general(a.astype(BF16), b.astype(BF16), dn, preferred_element_type=F32)


def _raw_nn(a, b, hp=False):
    return _raw_dot(a, b, a.ndim - 1, b.ndim - 2, hp)


def _raw_nt(a, b, hp=False):
    return _raw_dot(a, b, a.ndim - 1, b.ndim - 1, hp)


def _raw_tn(a, b, hp=False):
    return _raw_dot(a, b, a.ndim - 2, b.ndim - 2, hp)


@functools.partial(jax.custom_vjp, nondiff_argnums=(2,))
def _nn(a, b, hp):
    return _raw_nn(a, b, hp)


def _nn_fwd(a, b, hp):
    return _raw_nn(a, b, hp), (a, b)


def _nn_bwd(hp, res, g):
    a, b = res
    return _raw_nt(g, b, hp), _raw_tn(a, g, hp)


_nn.defvjp(_nn_fwd, _nn_bwd)


@functools.partial(jax.custom_vjp, nondiff_argnums=(2,))
def _nt(a, b, hp):
    return _raw_nt(a, b, hp)


def _nt_fwd(a, b, hp):
    return _raw_nt(a, b, hp), (a, b)


def _nt_bwd(hp, res, g):
    a, b = res
    return _raw_nn(g, b, hp), _raw_tn(g, a, hp)


_nt.defvjp(_nt_fwd, _nt_bwd)


def _shift_down(x, s):
    if s == 0:
        return x
    t = lax.broadcasted_iota(jnp.int32, x.shape, 0)
    return jnp.where(t >= s, pltpu.roll(x, s, axis=0), 0.0)


def _shift_up(x, s):
    if s == 0:
        return x
    n = x.shape[0]
    t = lax.broadcasted_iota(jnp.int32, x.shape, 0)
    return jnp.where(t < n - s, pltpu.roll(x, n - s, axis=0), 0.0)


def _row(w, k):
    r = lax.broadcasted_iota(jnp.int32, w.shape, 0)
    return jnp.sum(jnp.where(r == k, w, 0.0), axis=0, keepdims=True)


@jax.custom_vjp
def _dwconv(x, w):
    taps = w.shape[0]
    y = _row(w, taps - 1) * x
    for k in range(taps - 1):
        y = y + _row(w, k) * _shift_down(x, taps - 1 - k)
    return y


def _dwconv_fwd(x, w):
    return _dwconv(x, w), (x, w)


def _dwconv_bwd(res, dy):
    x, w = res
    taps = w.shape[0]
    r = lax.broadcasted_iota(jnp.int32, w.shape, 0)
    dx = _row(w, taps - 1) * dy
    dw = jnp.where(r == taps - 1, jnp.sum(dy * x, axis=0, keepdims=True), 0.0)
    for k in range(taps - 1):
        up = _shift_up(dy, taps - 1 - k)
        dx = dx + _row(w, k) * up
        dw = dw + jnp.where(r == k, jnp.sum(up * x, axis=0, keepdims=True), 0.0)
    return dx, dw


_dwconv.defvjp(_dwconv_fwd, _dwconv_bwd)


def _sigmoid(x):
    return 1.0 / (1.0 + jnp.exp(-x))


def _silu(x):
    return x * _sigmoid(x)


def _softplus(x):
    return jnp.maximum(x, 0.0) + jnp.log(1.0 + jnp.exp(-jnp.abs(x)))


def _head_scale(x, fn):
    tm = x.shape[0]
    x3 = x.reshape(tm, HEADS, DH)
    return (x3 * fn(jnp.sum(x3 * x3, axis=-1, keepdims=True))).reshape(tm, HEADS * DH)


def _tile_lanes(g):
    return jnp.concatenate([g] * HEADS, axis=1)


def _expand_heads(v, first):
    lane = lax.broadcasted_iota(jnp.int32, (LANE, HEADS * DH), 0)
    col = lax.broadcasted_iota(jnp.int32, (LANE, HEADS * DH), 1)
    sel = (lane == col // DH + first).astype(F32)
    return _nn(v, sel, True)


def _f_rms(x, g):
    return (x * lax.rsqrt(jnp.mean(x * x, axis=-1, keepdims=True) + EPS) * g,)


def _f_rms_res(x, g):
    return (_f_rms(x, g)[0], x)


def _f_conf_glu_conv(u, b, w):
    c = u.shape[1] // 2
    return (_dwconv((u[:, :c] + b[:, :c]) * _sigmoid(u[:, c:] + b[:, c:]), w),)


def _f_conf_ln_silu(cv, b_dw, ln_g, ln_b):
    u = cv + b_dw
    xc = u - jnp.mean(u, axis=-1, keepdims=True)
    y = xc * lax.rsqrt(jnp.mean(xc * xc, axis=-1, keepdims=True) + EPS) * ln_g + ln_b
    return (_silu(y),)


def _f_conv_silu(u, w):
    return (_silu(_dwconv(u, w)),)


def _f_gdn_gates(qk, ab, a_log, dt_bias):
    qn = _head_scale(qk[:, :D], lambda ss: lax.rsqrt(ss + EPS)) * (DH ** -0.5)
    kn = _head_scale(qk[:, D:], lambda ss: lax.rsqrt(ss + EPS))
    g = -jnp.exp(a_log) * _softplus(ab + dt_bias)
    beta = _sigmoid(ab)
    return qn, kn, _expand_heads(g, 0), _expand_heads(beta, HEADS)


def _f_gdn_prescan(q, k, v, gb, bb):
    c = GDN_CHUNK
    n = q.shape[0] // c
    r3 = lambda t: t.reshape(n, c, DH)
    q3, k3, v3, g3, b3 = r3(q), r3(k), r3(v), r3(gb), r3(bb)
    ii = lax.broadcasted_iota(jnp.int32, (n, c, c), 1)
    jj = lax.broadcasted_iota(jnp.int32, (n, c, c), 2)
    lower, strict = ii >= jj, ii > jj
    gcb = _nn(lower.astype(F32), g3, True)
    gi = gcb[:, :, :c]
    gj = jnp.swapaxes(gi, 1, 2)
    decay = jnp.where(lower, jnp.exp(jnp.where(lower, gi - gj, 0.0)), 0.0)
    kb, vb = k3 * b3, v3 * b3
    a_mat = jnp.where(strict, _nt(kb, k3, False) * decay, 0.0)
    p = -a_mat
    t_mat = (ii == jj).astype(F32) + p
    for _ in range(5):
        p = _nn(p, p, True)
        t_mat = t_mat + _nn(t_mat, p, True)
    eg = jnp.exp(gcb)
    u = _nn(t_mat, vb, False)
    w = _nn(t_mat, kb * eg, False)
    qk = jnp.where(lower, _nt(q3, k3, False) * decay, 0.0)
    qg = q3 * eg
    g_last = jnp.sum(g3, axis=1, keepdims=True)
    kd = k3 * jnp.exp(g_last - gcb)
    r2 = lambda t: t.reshape(n * c, DH)
    return r2(u), r2(w), r2(qg), qk, r2(kd), jnp.exp(g_last)


def _f_gdn_post(o, z, o_g):
    on = _head_scale(o, lambda ss: lax.rsqrt(ss / DH + EPS)) * _tile_lanes(o_g)
    return (on * _silu(z),)


def _f_fox_pre(qkf, q_g, k_g, b_f):
    qn = _head_scale(qkf[:, :D], lambda ss: lax.rsqrt(ss / DH + EPS)) * _tile_lanes(q_g)
    kn = _head_scale(qkf[:, D:2 * D], lambda ss: lax.rsqrt(ss / DH + EPS)) * _tile_lanes(k_g)
    return qn, kn, -_softplus(-(qkf[:, 2 * D:] + b_f))


def _f_ffn_mid(u, w):
    c = u.shape[1] // 2
    return (_silu(_dwconv(u[:, :c], w[:, :c])) * _dwconv(u[:, c:], w[:, c:]),)


def _seg_fwd(name, f, grid, ins, in_specs, out_shapes, out_specs):
    n_in = len(ins)

    def body(*refs):
        outs = f(*[r[...].astype(F32) for r in refs[:n_in]])
        for r, o in zip(refs[n_in:], outs):
            r[...] = o.astype(r.dtype)

    return pl.pallas_call(body, grid=grid, in_specs=in_specs, out_specs=out_specs, out_shape=out_shapes, name=name,
                          compiler_params=_params(*(["parallel"] * len(grid))))(*ins)


def _seg_bwd(name, f, grid, ins, in_specs, douts, dout_specs, want, into=None):
    into = into or {}
    n_in, n_dy = len(ins), len(douts)
    diff = [i for i, w in enumerate(want) if w is not None]
    kept = [i for i in diff if i in into]
    out_shapes = [jax.ShapeDtypeStruct(ins[i].shape, into[i].dtype if i in into else F32 if want[i] == "acc" else want[i])
                  for i in diff]
    out_specs = [in_specs[i] for i in diff]
    acc_axis = len(grid) - 1

    def body(*refs):
        vals = [r[...].astype(F32) for r in refs[:n_in]]
        dys = [r[...].astype(F32) for r in refs[n_in:n_in + n_dy]]
        out_refs = refs[n_in + n_dy + len(kept):]

        def g(*dv):
            full = list(vals)
            for i, v in zip(diff, dv):
                full[i] = v
            return f(*full)

        _, vjp = jax.vjp(g, *[vals[i] for i in diff])
        grads = vjp(tuple(dys))
        first = pl.program_id(acc_axis) == 0
        for i, r, gr in zip(diff, out_refs, grads):
            if want[i] == "acc":
                @pl.when(first)
                def _(r=r, gr=gr):
                    r[...] = gr

                @pl.when(jnp.logical_not(first))
                def _(r=r, gr=gr):
                    r[...] += gr
            else:
                r[...] = gr.astype(r.dtype)

    sem = ["parallel"] * (len(grid) - 1) + ["arbitrary"]
    untouched = [pl.BlockSpec(memory_space=pl.ANY)] * len(kept)
    aliases = {n_in + n_dy + e: diff.index(i) for e, i in enumerate(kept)}
    return pl.pallas_call(body, grid=grid, in_specs=list(in_specs) + list(dout_specs) + untouched, out_specs=out_specs,
                          out_shape=out_shapes, input_output_aliases=aliases, name=name,
                          compiler_params=_params(*sem))(*ins, *douts, *[into[i] for i in kept])


def _rows(tm, width, col=0):
    return pl.BlockSpec((tm, width), lambda i, col=col: (i, col))


def _const(shape):
    return pl.BlockSpec(shape, lambda i: (0,) * len(shape))


def _cols(t, tc, off=0):
    return pl.BlockSpec((t, tc), lambda j, off=off: (0, j + off))


def _grid2(specs):
    return [pl.BlockSpec(s.block_shape, lambda j, i, f=s.index_map: f(j)) for s in specs]


def _mm(name, a, b, *, ta=False, tb=False, res=None, out_dtype=F32, after=None):
    k_dim, m = (a.shape[0], a.shape[1]) if ta else (a.shape[1], a.shape[0])
    n = b.shape[0] if tb else b.shape[1]
    tm, tn, tk = _tile(m, 1408), _tile(n, 1408), _tile(k_dim, 1408)
    nk = k_dim // tk
    grid = (m // tm, n // tn, nk)
    a_spec = pl.BlockSpec((tk, tm), lambda i, j, k: (k, i)) if ta else pl.BlockSpec((tm, tk), lambda i, j, k: (i, k))
    b_spec = pl.BlockSpec((tn, tk), lambda i, j, k: (j, k)) if tb else pl.BlockSpec((tk, tn), lambda i, j, k: (k, j))
    o_spec = pl.BlockSpec((tm, tn), lambda i, j, k: (i, j))
    dn = (((0 if ta else 1,), (1 if tb else 0,)), ((), ()))
    has_res = res is not None

    def body(*refs):
        a_ref, b_ref = refs[0], refs[1]
        res_ref = refs[2] if has_res else None
        o_ref = refs[n_in]
        p = lax.dot_general(a_ref[...].astype(BF16), b_ref[...].astype(BF16), dn, preferred_element_type=F32)

        def write(acc):
            if has_res:
                acc = acc + res_ref[...]
            if after is not None:
                acc = acc + refs[n_in - 1][0:1, 0:1]
            o_ref[...] = acc.astype(o_ref.dtype)

        if nk == 1:
            write(p)
        else:
            acc_ref = refs[-1]
            k = pl.program_id(2)

            @pl.when(k == 0)
            def _():
                acc_ref[...] = p

            @pl.when(k > 0)
            def _():
                acc_ref[...] += p

            @pl.when(k == nk - 1)
            def _():
                write(acc_ref[...])

    ins, specs = [a, b], [a_spec, b_spec]
    if has_res:
        ins.append(res)
        specs.append(o_spec)
    if after is not None:
        ins.append(after)
        specs.append(pl.BlockSpec(after.shape, lambda i, j, k: (0, 0)))
    n_in = len(ins)
    scratch = [pltpu.VMEM((tm, tn), F32)] if nk > 1 else []
    return pl.pallas_call(body, grid=grid, in_specs=specs, out_specs=o_spec, scratch_shapes=scratch,
                          out_shape=jax.ShapeDtypeStruct((m, n), out_dtype), name=name,
                          compiler_params=_params("parallel", "parallel", "arbitrary"))(*ins)


def _head_specs(t):
    n = t // GDN_CHUNK
    col = pl.BlockSpec((t, DH), lambda h: (0, h))
    qk = pl.BlockSpec((None, n, GDN_CHUNK, GDN_CHUNK), lambda h: (h, 0, 0, 0))
    gl = pl.BlockSpec((None, n, 1, DH), lambda h: (h, 0, 0, 0))
    st = pl.BlockSpec((None, n, DH, DH), lambda h: (h, 0, 0, 0))
    return n, col, qk, gl, st


def _gdn_scan_fwd(name, u, w, qg, qk, kd, gl):
    t = u.shape[0]
    n, col, qk_spec, gl_spec, st_spec = _head_specs(t)
    c = GDN_CHUNK

    def body(u_ref, w_ref, qg_ref, qk_ref, kd_ref, gl_ref, o_ref, s_ref):
        def step(i, s):
            rows = pl.ds(pl.multiple_of(i * c, c), c)
            s_ref[i] = s
            vn = u_ref[rows, :] - _raw_nn(w_ref[rows, :], s)
            o_ref[rows, :] = _raw_nn(qg_ref[rows, :], s) + _raw_nn(qk_ref[i], vn)
            return s * gl_ref[i] + _raw_tn(kd_ref[rows, :], vn)

        lax.fori_loop(0, n, step, jnp.zeros((DH, DH), F32))

    return pl.pallas_call(
        body, grid=(HEADS,), in_specs=[col, col, col, qk_spec, col, gl_spec], out_specs=[col, st_spec],
        out_shape=[jax.ShapeDtypeStruct((t, D), F32), jax.ShapeDtypeStruct((HEADS, n, DH, DH), F32)], name=name,
        compiler_params=_params("parallel"))(u, w, qg, qk, kd, gl)


def _gdn_scan_bwd(name, u, w, qg, qk, kd, gl, states, do):
    t = u.shape[0]
    n, col, qk_spec, gl_spec, st_spec = _head_specs(t)
    c = GDN_CHUNK

    def body(u_ref, w_ref, qg_ref, qk_ref, kd_ref, gl_ref, s_ref, do_ref,
             du_ref, dw_ref, dqg_ref, dqk_ref, dkd_ref, dgl_ref):
        def step(r, ds):
            i = n - 1 - r
            rows = pl.ds(pl.multiple_of(i * c, c), c)
            s, do_c, w_c = s_ref[i], do_ref[rows, :], w_ref[rows, :]
            vn = u_ref[rows, :] - _raw_nn(w_c, s)
            dvn = _raw_tn(qk_ref[i], do_c) + _raw_nn(kd_ref[rows, :], ds)
            du_ref[rows, :] = dvn
            dw_ref[rows, :] = -_raw_nt(dvn, s)
            dqg_ref[rows, :] = _raw_nt(do_c, s)
            dqk_ref[i] = _raw_nt(do_c, vn)
            dkd_ref[rows, :] = _raw_nt(vn, ds)
            dgl_ref[i] = jnp.sum(ds * s, axis=0, keepdims=True)
            return _raw_tn(qg_ref[rows, :], do_c) + ds * gl_ref[i] - _raw_tn(w_c, dvn)

        lax.fori_loop(0, n, step, jnp.zeros((DH, DH), F32))

    big = jax.ShapeDtypeStruct((t, D), F32)
    return pl.pallas_call(
        body, grid=(HEADS,), in_specs=[col, col, col, qk_spec, col, gl_spec, st_spec, col],
        out_specs=[col, col, col, qk_spec, col, gl_spec],
        out_shape=[big, big, big, jax.ShapeDtypeStruct(qk.shape, F32), big, jax.ShapeDtypeStruct(gl.shape, F32)],
        name=name, compiler_params=_params("parallel"))(u, w, qg, qk, kd, gl, states, do)


def _cumsum_rows(name, parts, reverse):
    t = parts[0].shape[0]
    blk = min(t, 256)
    nb = t // blk
    n_in = len(parts)

    def body(*refs):
        o_ref = refs[n_in]
        ii = lax.broadcasted_iota(jnp.int32, (blk, blk), 0)
        jj = lax.broadcasted_iota(jnp.int32, (blk, blk), 1)
        tri = ((ii <= jj) if reverse else (ii >= jj)).astype(F32)
        carry = jnp.zeros((1, LANE), F32)
        for b in (range(nb - 1, -1, -1) if reverse else range(nb)):
            rows = pl.ds(b * blk, blk)
            x = refs[0][rows, :]
            for r in refs[1:n_in]:
                x = x + r[rows, :]
            o_ref[rows, :] = lax.dot_general(tri, x, (((1,), (0,)), ((), ())), precision=HIGHEST,
                                             preferred_element_type=F32) + carry
            carry = carry + jnp.sum(x, axis=0, keepdims=True)

    return pl.pallas_call(body, out_shape=jax.ShapeDtypeStruct((t, LANE), F32), name=name,
                          compiler_params=_params())(*parts)


def _fox_blocks(t):
    blk = 256 if t % 256 == 0 and t >= 1024 else 128
    return blk, t // blk


def _fox_logits(q, k, cq, ck, diagonal):
    s = _raw_nt(q, k) * (DH ** -0.5) + cq - ck
    if not diagonal:
        return s
    rows = lax.broadcasted_iota(jnp.int32, s.shape, 0)
    cols = lax.broadcasted_iota(jnp.int32, s.shape, 1)
    return jnp.where(cols <= rows, s, -jnp.inf)


def _fox_fwd(name, qn, kn, proj, c_col, c_row):
    t = qn.shape[0]
    blk, nb = _fox_blocks(t)
    voff = FOX_V_BLOCK

    def body(q_ref, k_ref, v_ref, cc_ref, cr_ref, o_ref, lse_ref):
        i = pl.program_id(1)
        q, cq = q_ref[...], cc_ref[...]

        def step(j, carry, diagonal=False):
            m, l, acc = carry
            rows = pl.ds(pl.multiple_of(j * blk, blk), blk)
            s = _fox_logits(q, k_ref[rows, :], cq, cr_ref[j], diagonal)
            m_new = jnp.maximum(m, jnp.max(s, axis=1, keepdims=True))
            p = jnp.exp(s - m_new)
            alpha = jnp.exp(m - m_new)
            return m_new, alpha * l + jnp.sum(p, axis=1, keepdims=True), alpha * acc + _raw_nn(p, v_ref[rows, :])

        init = (jnp.full((blk, 1), -1e30, F32), jnp.zeros((blk, 1), F32), jnp.zeros((blk, DH), F32))
        m, l, acc = step(i, lax.fori_loop(0, i, step, init), True)
        o_ref[...] = acc / l
        lse_ref[...] = m + jnp.log(l)

    tile = pl.BlockSpec((blk, DH), lambda h, i: (i, h))
    colv = pl.BlockSpec((None, blk, 1), lambda h, i: (h, i, 0))
    return pl.pallas_call(
        body, grid=(HEADS, nb),
        in_specs=[tile, pl.BlockSpec((t, DH), lambda h, i: (0, h)), pl.BlockSpec((t, DH), lambda h, i: (0, voff + h)),
                  colv, pl.BlockSpec((None, nb, 1, blk), lambda h, i: (h, 0, 0, 0))],
        out_specs=[tile, colv],
        out_shape=[jax.ShapeDtypeStruct((t, D), F32), jax.ShapeDtypeStruct((HEADS, t, 1), F32)], name=name,
        compiler_params=_params("parallel", "parallel"))(qn, kn, proj, c_col, c_row)


def _fox_dq(name, qn, kn, proj, c_col, c_row, o, lse, do):
    t = qn.shape[0]
    blk, nb = _fox_blocks(t)
    voff = FOX_V_BLOCK

    def body(q_ref, k_ref, v_ref, cc_ref, cr_ref, o_ref, lse_ref, do_ref, dq_ref, dcc_ref, dl_ref):
        i = pl.program_id(1)
        q, cq, do_b, lse_b = q_ref[...], cc_ref[...], do_ref[...], lse_ref[...]
        delta = jnp.sum(do_b * o_ref[...], axis=1, keepdims=True)

        def step(j, carry, diagonal=False):
            dq, dcc = carry
            rows = pl.ds(pl.multiple_of(j * blk, blk), blk)
            k = k_ref[rows, :]
            p = jnp.exp(_fox_logits(q, k, cq, cr_ref[j], diagonal) - lse_b)
            ds = p * (_raw_nt(do_b, v_ref[rows, :]) - delta)
            return dq + _raw_nn(ds, k), dcc + jnp.sum(ds, axis=1, keepdims=True)

        init = (jnp.zeros((blk, DH), F32), jnp.zeros((blk, 1), F32))
        dq, dcc = step(i, lax.fori_loop(0, i, step, init), True)
        dq_ref[...] = dq * (DH ** -0.5)
        dcc_ref[...] = dcc
        dl_ref[...] = delta

    tile = pl.BlockSpec((blk, DH), lambda h, i: (i, h))
    colv = pl.BlockSpec((None, blk, 1), lambda h, i: (h, i, 0))
    vec = jax.ShapeDtypeStruct((HEADS, t, 1), F32)
    return pl.pallas_call(
        body, grid=(HEADS, nb),
        in_specs=[tile, pl.BlockSpec((t, DH), lambda h, i: (0, h)), pl.BlockSpec((t, DH), lambda h, i: (0, voff + h)),
                  colv, pl.BlockSpec((None, nb, 1, blk), lambda h, i: (h, 0, 0, 0)), tile, colv, tile],
        out_specs=[tile, colv, colv],
        out_shape=[jax.ShapeDtypeStruct((t, D), F32), vec, vec], name=name,
        compiler_params=_params("parallel", "parallel"))(qn, kn, proj, c_col, c_row, o, lse, do)


def _fox_dkv(name, qn, kn, proj, c_col, c_row, lse, delta, do):
    t = qn.shape[0]
    blk, nb = _fox_blocks(t)
    voff = FOX_V_BLOCK

    def body(q_ref, k_ref, v_ref, cc_ref, cr_ref, lse_ref, dl_ref, do_ref, dk_ref, dv_ref, dcr_ref):
        j = pl.program_id(1)
        k, v, ck = k_ref[...], v_ref[...], cr_ref[...]

        def step(i, carry, diagonal=False):
            dk, dv, dcr = carry
            rows = pl.ds(pl.multiple_of(i * blk, blk), blk)
            q, do_b = q_ref[rows, :], do_ref[rows, :]
            p = jnp.exp(_fox_logits(q, k, cc_ref[rows, :], ck, diagonal) - lse_ref[rows, :])
            ds = p * (_raw_nt(do_b, v) - dl_ref[rows, :])
            return dk + _raw_tn(ds, q), dv + _raw_tn(p, do_b), dcr - jnp.sum(ds, axis=0, keepdims=True)

        init = (jnp.zeros((blk, DH), F32), jnp.zeros((blk, DH), F32), jnp.zeros((1, blk), F32))
        dk, dv, dcr = lax.fori_loop(j + 1, nb, step, step(j, init, True))
        dk_ref[...] = dk * (DH ** -0.5)
        dv_ref[...] = dv.astype(dv_ref.dtype)
        dcr_ref[...] = dcr

    full = pl.BlockSpec((t, DH), lambda h, j: (0, h))
    colf = pl.BlockSpec((None, t, 1), lambda h, j: (h, 0, 0))
    tile = pl.BlockSpec((blk, DH), lambda h, j: (j, h))
    rowv = pl.BlockSpec((None, None, 1, blk), lambda h, j: (h, j, 0, 0))
    tile_v = pl.BlockSpec((blk, DH), lambda h, j: (j, voff + h))
    big = jax.ShapeDtypeStruct((t, D), F32)
    return pl.pallas_call(
        body, grid=(HEADS, nb),
        in_specs=[full, tile, tile_v, colf, rowv, colf, colf, full],
        out_specs=[tile, tile_v, rowv],
        out_shape=[big, jax.ShapeDtypeStruct(proj.shape, BF16), jax.ShapeDtypeStruct((HEADS, nb, 1, blk), F32)], name=name,
        compiler_params=_params("parallel", "parallel"))(qn, kn, proj, c_col, c_row, lse, delta, do)


def _loss_head(y, target):
    t = y.shape[0]
    tm = min(t, 512)

    def body(y_ref, t_ref, dy_ref, sum_ref):
        err = y_ref[...] - t_ref[...]
        dy_ref[...] = err * (1.0 / D)
        part = jnp.sum(jnp.sum(err * err, axis=1, keepdims=True), axis=0, keepdims=True)

        @pl.when(pl.program_id(0) == 0)
        def _():
            sum_ref[...] = jnp.zeros_like(sum_ref)

        sum_ref[...] += jnp.broadcast_to(part, sum_ref.shape)

    return pl.pallas_call(
        body, grid=(t // tm,), in_specs=[_rows(tm, D), _rows(tm, D)], out_specs=[_rows(tm, D), _const((1, LANE))],
        out_shape=[jax.ShapeDtypeStruct((t, D), F32), jax.ShapeDtypeStruct((1, LANE), F32)], name="loss_head",
        compiler_params=_params("arbitrary"))(y, target)


ADAM_BLOCK_BYTES = 3 * 1024 * 1024


def _adam(name, lands, owns, mine, w, m, v):
    layers = len(lands)
    r, c = owns[0].shape
    block_bytes = lambda rows: N_DEV * rows * c * lands[0].dtype.itemsize
    tr = r
    for cand in (512, 256, 128, 64, 32, 16):
        if block_bytes(tr) > ADAM_BLOCK_BYTES and r % cand == 0:
            tr = cand
    nr = r // tr

    def body(*refs):
        me_ref, land_refs, own_refs = refs[0], refs[1:1 + layers], refs[1 + layers:1 + 2 * layers]
        w_ref, m_ref, v_ref, g_ref, d_ref, nm_ref, nv_ref = refs[1 + 2 * layers:]
        layer = pl.program_id(0)
        for l in range(layers):
            @pl.when(layer == l)
            def _(l=l):
                g = jnp.zeros((tr, c), F32)
                for s in range(N_DEV):
                    g = g + jnp.where(me_ref[0] == s, own_refs[l][...], land_refs[l][s]).astype(F32)
                nm = ADAM_B1 * m_ref[...] + (1.0 - ADAM_B1) * g
                nv = ADAM_B2 * v_ref[...] + (1.0 - ADAM_B2) * (g * g)
                m_hat = nm / (1.0 - ADAM_B1 ** ADAM_STEP)
                v_hat = nv / (1.0 - ADAM_B2 ** ADAM_STEP)
                g_ref[...] = g
                d_ref[...] = -ADAM_LR * (m_hat / (jnp.sqrt(v_hat) + ADAM_EPS) + ADAM_WD * w_ref[...])
                nm_ref[...] = nm
                nv_ref[...] = nv

    at = lambda l: (lambda layer, i: jnp.where(layer == l, i, 0))
    land_specs = [pl.BlockSpec((N_DEV, tr, c), lambda layer, i, f=at(l): (0, f(layer, i), 0)) for l in range(layers)]
    own_specs = [pl.BlockSpec((tr, c), lambda layer, i, f=at(l): (f(layer, i), 0)) for l in range(layers)]
    blk = pl.BlockSpec((tr, c), lambda layer, i: (layer * nr + i, 0))
    out = jax.ShapeDtypeStruct(w.shape, F32)
    return pl.pallas_call(
        body, grid=(layers, nr),
        in_specs=[pl.BlockSpec(memory_space=pltpu.SMEM)] + land_specs + own_specs + [blk, blk, blk],
        out_specs=[blk] * 4, out_shape=[out] * 4, name=name,
        compiler_params=_params("arbitrary", "arbitrary"))(mine, *lands, *owns, w, m, v)


_HBM = pl.BlockSpec(memory_space=pltpu.HBM)
_SEM = pl.BlockSpec(memory_space=pltpu.SEMAPHORE)
_EFFECT = pltpu.SideEffectType.DATAFLOW_SIDE_EFFECTING


def _each_peer(x, y, c):
    flip = lambda v, bit: 1 - v if bit else v
    for p in range(1, N_DEV):
        px, py, pc = flip(x, p & 4), flip(y, p & 2), flip(c, p & 1)
        yield p, (px, py, pc), 4 * px + 2 * py + pc


def _sem(a, p):
    return a * (N_DEV - 1) + p - 1


def _exchange_start(name, arrays, scatter):
    n = len(arrays)
    lands = [lax.empty((N_DEV,) + (a.shape[1:] if sc else a.shape), a.dtype) for a, sc in zip(arrays, scatter)]

    def body(*refs):
        in_refs, land_refs = refs[:n], refs[n:2 * n]
        send_sems, recv_sems, token = refs[2 * n], refs[2 * n + 1], refs[-1]
        x, y, c = lax.axis_index("x"), lax.axis_index("y"), lax.axis_index("c")
        me = 4 * x + 2 * y + c
        for p, coords, peer in _each_peer(x, y, c):
            for a in range(n):
                pltpu.make_async_remote_copy(
                    src_ref=in_refs[a].at[peer] if scatter[a] else in_refs[a], dst_ref=land_refs[a].at[me],
                    send_sem=send_sems.at[_sem(a, p)], recv_sem=recv_sems.at[_sem(a, p)], device_id=coords,
                    device_id_type=pl.DeviceIdType.MESH).start()
        token[...] = jnp.zeros_like(token)

    sems = pltpu.SemaphoreType.DMA((n * (N_DEV - 1),))
    hbm = lambda a: pltpu.HBM(a.shape, a.dtype)
    out = pl.pallas_call(
        body, name=name,
        out_shape=(sems, sems, *[hbm(a) for a in arrays], *[hbm(l) for l in lands],
                   jax.ShapeDtypeStruct((8, LANE), F32)),
        in_specs=[_HBM] * (2 * n), out_specs=(_SEM, _SEM, *[_HBM] * (2 * n), pl.BlockSpec(memory_space=pltpu.VMEM)),
        input_output_aliases={i: 2 + i for i in range(2 * n)},
        compiler_params=pltpu.CompilerParams(has_side_effects=_EFFECT),
    )(*[pltpu.with_memory_space_constraint(a, pltpu.HBM) for a in arrays],
      *[pltpu.with_memory_space_constraint(l, pltpu.HBM) for l in lands])
    return (out[0], out[1], list(out[2:2 + n]), list(out[2 + n:2 + 2 * n]), scatter), out[-1]


def _exchange_wait(name, started, after, me):
    send_sems, recv_sems, sent, lands, scatter = started
    n = len(sent)

    def body(*refs):
        in_refs, land_refs = refs[:n], refs[n:2 * n]
        send_sems, recv_sems = refs[2 * n], refs[2 * n + 1]
        x, y, c = lax.axis_index("x"), lax.axis_index("y"), lax.axis_index("c")
        for p, coords, peer in _each_peer(x, y, c):
            for a in range(n):
                cp = pltpu.make_async_remote_copy(
                    src_ref=in_refs[a].at[peer] if scatter[a] else in_refs[a], dst_ref=land_refs[a].at[peer],
                    send_sem=send_sems.at[_sem(a, p)], recv_sem=recv_sems.at[_sem(a, p)], device_id=coords,
                    device_id_type=pl.DeviceIdType.MESH)
                cp.wait_send()
                cp.wait_recv()

    hbm = lambda a: pltpu.HBM(a.shape, a.dtype)
    out = pl.pallas_call(
        body, name=name, out_shape=(*[hbm(a) for a in sent], *[hbm(l) for l in lands]),
        in_specs=[_HBM] * (2 * n) + [_SEM, _SEM, pl.BlockSpec(memory_space=pl.ANY)], out_specs=[_HBM] * (2 * n),
        input_output_aliases={i: i for i in range(2 * n)},
        compiler_params=pltpu.CompilerParams(has_side_effects=_EFFECT),
    )(*sent, *lands, send_sems, recv_sems, after)
    owns = [lax.dynamic_index_in_dim(out[a], me, 0, keepdims=False) if scatter[a] else out[a] for a in range(n)]
    return list(out[n:]), owns


def _fill_own(land, own, me):
    slot = lax.broadcasted_iota(jnp.int32, (N_DEV,) + (1,) * own.ndim, 0)
    return jnp.where(slot == me, own[None], land)


def _rms_fwd(name, x, g):
    t = x.shape[0]
    tm = min(t, 512)
    return _seg_fwd(name, _f_rms, (t // tm,), [x, g], [_rows(tm, D), _const((1, D))],
                    [jax.ShapeDtypeStruct((t, D), BF16)], [_rows(tm, D)])[0]


def _rms_bwd(name, x, g, dh, dres):
    t = x.shape[0]
    tm = min(t, 512)
    return _seg_bwd(name, _f_rms_res, (t // tm,), [x, g], [_rows(tm, D), _const((1, D))],
                    [dh, dres], [_rows(tm, D), _rows(tm, D)], [F32, "acc"])


FFN_TC = 256
CONF_TC = 128


def _ffn_specs(t):
    nf = D_FF // FFN_TC
    return nf, [_cols(t, 2 * FFN_TC), _cols(FFN_K, 2 * FFN_TC)], [_cols(t, FFN_TC)]


def _ffn_fwd(tag, x, p):
    t = x.shape[0]
    nf, in_specs, out_specs = _ffn_specs(t)
    h = _rms_fwd(tag + "_rms", x, p["g"])
    u0 = _mm(tag + "_up", h, p["w_up"])
    act = _seg_fwd(tag + "_mid", _f_ffn_mid, (nf,), [u0, p["w_dw"]], in_specs,
                   [jax.ShapeDtypeStruct((t, D_FF), BF16)], out_specs)[0]
    out = _mm(tag + "_down", act, p["w_down"], res=x)
    return out, (x, h, u0, act)


def _ffn_bwd(tag, saved, p, dout, after=None):
    x, h, u0, act = saved
    t = x.shape[0]
    nf, in_specs, out_specs = _ffn_specs(t)
    g = {"w_down": _mm(tag + "_dwdown", act, dout, ta=True, out_dtype=BF16)}
    dact = _mm(tag + "_dact", dout, p["w_down"], tb=True, out_dtype=BF16, after=after)
    du0, g["w_dw"] = _seg_bwd(tag + "_dmid", _f_ffn_mid, (nf, 1), [u0, p["w_dw"]], _grid2(in_specs), [dact],
                              _grid2(out_specs), [BF16, "acc"])
    g["w_up"] = _mm(tag + "_dwup", h, du0, ta=True, out_dtype=BF16)
    dh = _mm(tag + "_dh", du0, p["w_up"], tb=True, out_dtype=BF16)
    dx, g["g"] = _rms_bwd(tag + "_drms", x, p["g"], dh, dout)
    return dx, g


def _conf_specs(t):
    tm = min(t, 512)
    nc = D // CONF_TC
    vec = _const((1, D))
    glu_in = [_cols(t, 2 * CONF_TC), _cols(1, 2 * CONF_TC), _cols(CONF_K, CONF_TC)]
    return tm, nc, glu_in, [_cols(t, CONF_TC)], [_rows(tm, D), vec, vec, vec]


def _conf_fwd(tag, x, p):
    t = x.shape[0]
    tm, nc, glu_in, glu_out, ln_in = _conf_specs(t)
    h = _rms_fwd(tag + "_rms", x, p["g"])
    u = _mm(tag + "_in", h, p["w_in"])
    cv = _seg_fwd(tag + "_gluconv", _f_conf_glu_conv, (nc,), [u, p["b_in"], p["w_dw"]], glu_in,
                  [jax.ShapeDtypeStruct((t, D), F32)], glu_out)[0]
    act = _seg_fwd(tag + "_lnsilu", _f_conf_ln_silu, (t // tm,), [cv, p["b_dw"], p["ln_g"], p["ln_b"]], ln_in,
                   [jax.ShapeDtypeStruct((t, D), BF16)], [_rows(tm, D)])[0]
    out = _mm(tag + "_out", act, p["w_out"], res=x)
    return out, (x, h, u, cv, act)


def _conf_bwd(tag, saved, p, dout, after=None):
    x, h, u, cv, act = saved
    t = x.shape[0]
    tm, nc, glu_in, glu_out, ln_in = _conf_specs(t)
    g = {"w_out": _mm(tag + "_dwout", act, dout, ta=True, out_dtype=BF16)}
    dact = _mm(tag + "_dact", dout, p["w_out"], tb=True, out_dtype=BF16, after=after)
    dcv, g["b_dw"], g["ln_g"], g["ln_b"] = _seg_bwd(
        tag + "_dlnsilu", _f_conf_ln_silu, (t // tm,), [cv, p["b_dw"], p["ln_g"], p["ln_b"]], ln_in, [dact],
        [_rows(tm, D)], [F32, "acc", "acc", "acc"])
    du, g["b_in"], g["w_dw"] = _seg_bwd(tag + "_dgluconv", _f_conf_glu_conv, (nc, 1), [u, p["b_in"], p["w_dw"]],
                                        _grid2(glu_in), [dcv], _grid2(glu_out), [BF16, "acc", "acc"])
    g["w_in"] = _mm(tag + "_dwin", h, du, ta=True, out_dtype=BF16)
    dh = _mm(tag + "_dh", du, p["w_in"], tb=True, out_dtype=BF16)
    dx, g["g"] = _rms_bwd(tag + "_drms", x, p["g"], dh, dout)
    return dx, g


def _gdn_specs(t):
    tc, tm, rows = 256, min(t, 256), min(t, 512)
    nq = 3 * D // tc
    conv = ([_cols(t, tc), _cols(GDN_K, tc)], [_cols(t, tc)])
    gate_in = [_rows(tm, 2 * D, 0), _rows(tm, LANE, 4 * D // LANE), _const((1, LANE)), _const((1, LANE))]
    gate_out = [_rows(tm, D)] * 4
    head = pl.BlockSpec((rows, DH), lambda h, i: (i, h))
    headv = pl.BlockSpec((rows, DH), lambda h, i: (i, 2 * HEADS + h))
    nch = rows // GDN_CHUNK
    pre_in = [head, head, headv, head, head]
    pre_out = [head, head, head, pl.BlockSpec((None, nch, GDN_CHUNK, GDN_CHUNK), lambda h, i: (h, i, 0, 0)), head,
               pl.BlockSpec((None, nch, 1, DH), lambda h, i: (h, i, 0, 0))]
    post_in = [_rows(tm, D), _rows(tm, D, 3), _const((1, DH))]
    return tm, rows, nq, conv, gate_in, gate_out, pre_in, pre_out, post_in


def _gdn_fwd(tag, x, p):
    t = x.shape[0]
    tm, rows, nq, conv, gate_in, gate_out, pre_in, pre_out, post_in = _gdn_specs(t)
    n = t // GDN_CHUNK
    big = jax.ShapeDtypeStruct((t, D), F32)
    h = _rms_fwd(tag + "_rms", x, p["g"])
    proj = _mm(tag + "_in", h, p["w_in"])
    qkv = _seg_fwd(tag + "_conv", _f_conv_silu, (nq,), [proj, p["conv_w"]], conv[0],
                   [jax.ShapeDtypeStruct((t, 3 * D), F32)], conv[1])[0]
    gate_ins = [qkv, proj, p["a_log"], p["dt_bias"]]
    qn, kn, gb, bb = _seg_fwd(tag + "_gates", _f_gdn_gates, (t // tm,), gate_ins, gate_in, [big] * 4, gate_out)
    pre_ins = [qn, kn, qkv, gb, bb]
    pre_shapes = [big, big, big, jax.ShapeDtypeStruct((HEADS, n, GDN_CHUNK, GDN_CHUNK), F32), big,
                  jax.ShapeDtypeStruct((HEADS, n, 1, DH), F32)]
    pre = _seg_fwd(tag + "_prescan", _f_gdn_prescan, (HEADS, t // rows), pre_ins, pre_in, pre_shapes, pre_out)
    o, states = _gdn_scan_fwd(tag + "_scan", *pre)
    post_ins = [o, proj, p["o_g"]]
    act = _seg_fwd(tag + "_post", _f_gdn_post, (t // tm,), post_ins, post_in, [jax.ShapeDtypeStruct((t, D), BF16)],
                   [_rows(tm, D)])[0]
    out = _mm(tag + "_out", act, p["w_out"], res=x)
    return out, (x, h, proj, gate_ins, pre_ins, pre, states, post_ins, act)


def _gdn_bwd(tag, saved, p, dout, after=None):
    x, h, proj, gate_ins, pre_ins, pre, states, post_ins, act = saved
    t = x.shape[0]
    tm, rows, nq, conv, gate_in, gate_out, pre_in, pre_out, post_in = _gdn_specs(t)
    g = {"w_out": _mm(tag + "_dwout", act, dout, ta=True, out_dtype=BF16)}
    dact = _mm(tag + "_dact", dout, p["w_out"], tb=True, out_dtype=BF16, after=after)
    do, dproj, g["o_g"] = _seg_bwd(tag + "_dpost", _f_gdn_post, (t // tm,), post_ins, post_in, [dact],
                                   [_rows(tm, D)], [F32, BF16, "acc"])
    dpre = _gdn_scan_bwd(tag + "_dscan", *pre, states, do)
    dqn, dkn, dqkv, dgb, dbb = _seg_bwd(tag + "_dprescan", _f_gdn_prescan, (HEADS, t // rows), pre_ins, pre_in,
                                        dpre, pre_out, [F32] * 5)
    dqkv, dproj, g["a_log"], g["dt_bias"] = _seg_bwd(
        tag + "_dgates", _f_gdn_gates, (t // tm,), gate_ins, gate_in, [dqn, dkn, dgb, dbb], gate_out,
        [F32, BF16, "acc", "acc"], into={0: dqkv, 1: dproj})
    dproj, g["conv_w"] = _seg_bwd(tag + "_dconv", _f_conv_silu, (nq, 1), [proj, p["conv_w"]], _grid2(conv[0]),
                                  [dqkv], _grid2(conv[1]), [BF16, "acc"], into={0: dproj})
    g["w_in"] = _mm(tag + "_dwin", h, dproj, ta=True, out_dtype=BF16)
    dh = _mm(tag + "_dh", dproj, p["w_in"], tb=True, out_dtype=BF16)
    dx, g["g"] = _rms_bwd(tag + "_drms", x, p["g"], dh, dout)
    return dx, g


def _fox_specs(t):
    tm = min(t, 512)
    vec = _const((1, LANE))
    pre_in = [_rows(tm, 2 * D + LANE, 0), vec, vec, vec]
    pre_out = [_rows(tm, D), _rows(tm, D), _rows(tm, LANE)]
    return tm, pre_in, pre_out


def _per_head(c):
    return jnp.transpose(c[:, :HEADS])


def _per_lane(ch):
    return jnp.pad(jnp.transpose(ch), ((0, 0), (0, LANE - HEADS)))


def _fox_fwd_layer(tag, x, p):
    t = x.shape[0]
    tm, pre_in, pre_out = _fox_specs(t)
    blk, nb = _fox_blocks(t)
    h = _rms_fwd(tag + "_rms", x, p["g"])
    proj = _mm(tag + "_in", h, p["w_in"])
    pre_ins = [proj, p["q_g"], p["k_g"], p["b_f"]]
    qn, kn, lf = _seg_fwd(tag + "_pre", _f_fox_pre, (t // tm,), pre_ins, pre_in,
                          [jax.ShapeDtypeStruct((t, D), BF16)] * 2 + [jax.ShapeDtypeStruct((t, LANE), F32)], pre_out)
    ch = _per_head(_cumsum_rows(tag + "_cumsum", [lf], False))
    c_col, c_row = ch.reshape(HEADS, t, 1), ch.reshape(HEADS, nb, 1, blk)
    o, lse = _fox_fwd(tag + "_attn", qn, kn, proj, c_col, c_row)
    out = _mm(tag + "_out", o, p["w_out"], res=x)
    return out, (x, h, proj, pre_ins, qn, kn, c_col, c_row, o, lse)


def _fox_bwd_layer(tag, saved, p, dout, after=None):
    x, h, proj, pre_ins, qn, kn, c_col, c_row, o, lse = saved
    t = x.shape[0]
    tm, pre_in, pre_out = _fox_specs(t)
    g = {"w_out": _mm(tag + "_dwout", o, dout, ta=True, out_dtype=BF16)}
    do = _mm(tag + "_do", dout, p["w_out"], tb=True, after=after)
    dqn, dc_col, delta = _fox_dq(tag + "_dq", qn, kn, proj, c_col, c_row, o, lse, do)
    dkn, dproj, dc_row = _fox_dkv(tag + "_dkv", qn, kn, proj, c_col, c_row, lse, delta, do)
    dlf = _cumsum_rows(tag + "_dcumsum", [_per_lane(dc_col.reshape(HEADS, t)), _per_lane(dc_row.reshape(HEADS, t))],
                       True)
    dproj, g["q_g"], g["k_g"], g["b_f"] = _seg_bwd(
        tag + "_dpre", _f_fox_pre, (t // tm,), pre_ins, pre_in, [dqn, dkn, dlf], pre_out,
        [BF16, "acc", "acc", "acc"], into={0: dproj})
    g["w_in"] = _mm(tag + "_dwin", h, dproj, ta=True, out_dtype=BF16)
    dh = _mm(tag + "_dh", dproj, p["w_in"], tb=True, out_dtype=BF16)
    dx, g["g"] = _rms_bwd(tag + "_drms", x, p["g"], dh, dout)
    return dx, g


_MIXERS = ((_conf_fwd, _conf_bwd), (_gdn_fwd, _gdn_bwd), (_fox_fwd_layer, _fox_bwd_layer))


def _local_step(x, target, params_of, on_grads):
    saved, params = [], []
    for i in range(DEPTH):
        mp = params_of(i, 0, x)
        x, sm = _MIXERS[i % N_MIXERS][0](f"l{i}_mix", x, mp)
        fp = params_of(i, 1, x)
        x, sf = _ffn_fwd(f"l{i}_ffn", x, fp)
        saved.append((sm, sf))
        params.append((mp, fp))
    dx, sq = _loss_head(x, target)
    after = None
    for i in reversed(range(DEPTH)):
        dx, gf = _ffn_bwd(f"l{i}_ffn", saved[i][1], params[i][1], dx, after)
        after = on_grads(i, 1, gf)
        dx, gm = _MIXERS[i % N_MIXERS][1](f"l{i}_mix", saved[i][0], params[i][0], dx, after)
        after = on_grads(i, 0, gm)
    return sq, dx


def _unshard(name, g):
    axis = g.ndim - 2 if name in ROW_SHARDED else g.ndim - 1
    m = jnp.moveaxis(g, 0, axis - 1)
    return m.reshape(m.shape[:axis - 1] + (N_DEV * m.shape[axis],) + m.shape[axis + 1:])


def _reshard(name, full):
    axis = full.ndim - 2 if name in ROW_SHARDED else full.ndim - 1
    s = full.shape
    return jnp.moveaxis(full.reshape(s[:axis] + (N_DEV, s[axis] // N_DEV) + s[axis + 1:]), axis, 0)


def _pad_cols(a, width):
    return jnp.pad(a, [(0, 0)] * (a.ndim - 1) + [(0, width - a.shape[-1])])


def _lane_vec(v):
    return _pad_cols(v.reshape(1, -1), LANE)


REP_ROWS = 16


def _pack_rep(r):
    small = [_pad_cols(r[n].reshape(1, -1), LANE) for n in REPLICATED[2:]]
    row = jnp.concatenate(small + [jnp.zeros((1, D - LANE * len(small)), F32)], axis=1)
    pad = jnp.zeros((REP_ROWS - 2 * DEPTH - 1, D), F32)
    return jnp.concatenate([r['mix_norm_g'].reshape(DEPTH, D), r['ffn_norm_g'].reshape(DEPTH, D), row, pad], axis=0)


def _unpack_rep(a, shapes):
    out = {'mix_norm_g': a[:DEPTH], 'ffn_norm_g': a[DEPTH:2 * DEPTH]}
    for i, n in enumerate(REPLICATED[2:]):
        out[n] = a[2 * DEPTH:2 * DEPTH + 1, i * LANE:i * LANE + shapes[n][1]].reshape(shapes[n])
    return out


def _view2d(a):
    return a.reshape(-1, a.shape[-1])


MIXER_SHARDED = (
    (('conv_w_in', 'w_in'), ('conv_w_out', 'w_out'), ('conv_b_in', 'b_in'), ('conv_w_dw', 'w_dw'),
     ('conv_b_dw', 'b_dw'), ('conv_ln_g', 'ln_g'), ('conv_ln_b', 'ln_b')),
    (('gdn_w_in', 'w_in'), ('gdn_w_out', 'w_out'), ('gdn_conv_w', 'conv_w')),
    (('fox_w_in', 'w_in'), ('fox_w_out', 'w_out')),
)
FFN_SHARDED = (('ffn_w_up', 'w_up'), ('ffn_w_down', 'w_down'), ('ffn_w_dw', 'w_dw'))
MIXER_REPLICATED = (
    (),
    (('gdn_a_log', 'a_log'), ('gdn_dt_bias', 'dt_bias'), ('gdn_o_norm_g', 'o_g')),
    (('fox_b_f', 'b_f'), ('fox_q_norm_g', 'q_g'), ('fox_k_norm_g', 'k_g')),
)
ROW_VECTORS = ('conv_b_in', 'conv_b_dw', 'conv_ln_g', 'conv_ln_b')
INTERLEAVED = {'ffn_w_up': (D_FF, FFN_TC), 'ffn_w_dw': (D_FF, FFN_TC), 'conv_w_in': (D, CONF_TC),
               'conv_b_in': (D, CONF_TC)}
PACK_GROUP = 16 * D


def _part_entries(i, part):
    ent = [(n, i, k) for n, k in FFN_SHARDED] if part else [(n, i // N_MIXERS, k) for n, k in MIXER_SHARDED[i % N_MIXERS]]
    return [e for e in ent if e[0] in MATRICES], [e for e in ent if e[0] not in MATRICES]


def _interleave(a, half, blk):
    s = a.shape[:-1]
    return jnp.swapaxes(a.reshape(s + (2, half // blk, blk)), -3, -2).reshape(s + (2 * half,))


def _deinterleave(a, half, blk):
    s = a.shape[:-1]
    return jnp.swapaxes(a.reshape(s + (half // blk, 2, blk)), -3, -2).reshape(s + (2 * half,))


def _to_param(name, whole):
    if name in ROW_VECTORS:
        whole = whole[None]
    if name in INTERLEAVED:
        return _interleave(whole, *INTERLEAVED[name])
    if name == 'gdn_w_in':
        return _pad_cols(whole, GDN_PAD)
    if name == 'fox_w_in':
        return jnp.concatenate([whole[:, :2 * D], _pad_cols(whole[:, 3 * D:], LANE), whole[:, 2 * D:3 * D]], axis=1)
    return whole


def _from_grad(name, g):
    if name in INTERLEAVED:
        g = _deinterleave(g, *INTERLEAVED[name])
    if name in ROW_VECTORS:
        return g[0]
    if name == 'gdn_w_in':
        return g[:, :4 * D + 2 * HEADS]
    if name == 'fox_w_in':
        return jnp.concatenate([g[:, :2 * D], g[:, 2 * D + LANE:], g[:, 2 * D:2 * D + HEADS]], axis=1)
    return g


def _layer_col(name, n):
    if name in INTERLEAVED:
        half, blk = INTERLEAVED[name]
        return (n % half) // blk * (2 * blk) + (n // half) * blk + n % blk
    if name == 'fox_w_in':
        return np.where(n < 2 * D, n, np.where(n < 3 * D, n + LANE, n - D))
    return n


def _col_runs(name, shard_cols):
    dst = _layer_col(name, np.arange(N_DEV * shard_cols))
    runs, start = [[] for _ in range(N_DEV)], 0
    for k in range(1, dst.size + 1):
        if k == dst.size or dst[k] != dst[k - 1] + 1 or k % shard_cols == 0:
            runs[start // shard_cols].append((start % shard_cols, k - start, int(dst[start])))
            start = k
    width = {'gdn_w_in': GDN_PAD, 'fox_w_in': FOX_PAD}.get(name, dst.size)
    free = np.ones(width + 1, bool)
    free[dst] = False
    free[width] = False
    gaps, start = [], None
    for k in range(width + 1):
        if free[k] and start is None:
            start = k
        if not free[k] and start is not None:
            gaps.append((start, k - start))
            start = None
    return runs, width, gaps


def _unshard_cols(call, name, land, own, mine):
    _, r, c = land.shape
    runs, width, gaps = _col_runs(name, c)
    tr = _tile(r, 256)

    def body(me_ref, land_ref, own_ref, o_ref):
        for s in range(N_DEV):
            for src, ln, dst in runs[s]:
                o_ref[:, dst:dst + ln] = jnp.where(me_ref[0] == s, own_ref[:, src:src + ln], land_ref[s, :, src:src + ln])
        for start, ln in gaps:
            o_ref[:, start:start + ln] = jnp.zeros((tr, ln), o_ref.dtype)

    return pl.pallas_call(
        body, grid=(r // tr,),
        in_specs=[pl.BlockSpec(memory_space=pltpu.SMEM), pl.BlockSpec((N_DEV, tr, c), lambda i: (0, i, 0)),
                  pl.BlockSpec((tr, c), lambda i: (i, 0))],
        out_specs=pl.BlockSpec((tr, width), lambda i: (i, 0)), out_shape=jax.ShapeDtypeStruct((r, width), land.dtype),
        name=call, compiler_params=_params("parallel"))(mine, land, own)


def _reshard_cols(call, name, g, shard_cols):
    r = g.shape[0]
    runs, width, _ = _col_runs(name, shard_cols)
    tr = _tile(r, 256)

    def body(g_ref, o_ref):
        for s in range(N_DEV):
            for src, ln, dst in runs[s]:
                o_ref[s, :, src:src + ln] = g_ref[:, dst:dst + ln]

    return pl.pallas_call(
        body, grid=(r // tr,), in_specs=[pl.BlockSpec((tr, width), lambda i: (i, 0))],
        out_specs=pl.BlockSpec((N_DEV, tr, shard_cols), lambda i: (0, i, 0)),
        out_shape=jax.ShapeDtypeStruct((N_DEV, r, shard_cols), g.dtype), name=call,
        compiler_params=_params("parallel"))(g)


def _pack_rows(parts, lead):
    out = []
    for a in parts:
        flat = a.reshape(a.shape[:lead] + (-1,))
        size = flat.shape[-1]
        padded = -(-size // PACK_GROUP) * PACK_GROUP
        flat = jnp.pad(flat, [(0, 0)] * lead + [(0, padded - size)])
        out.append(flat.reshape(a.shape[:lead] + (padded // D, D)))
    return jnp.concatenate(out, axis=lead)


def _unpack_rows(packed, shapes, lead):
    out, row = [], 0
    head = packed.shape[:lead]
    for s in shapes:
        size = 1
        for d in s:
            size *= d
        rows = -(-size // PACK_GROUP) * (PACK_GROUP // D)
        part = lax.slice_in_dim(packed, row, row + rows, axis=lead)
        out.append(part.reshape(head + (rows * D,))[..., :size].reshape(head + tuple(s)))
        row += rows
    return out


def _part_dict(i, part, whole, rep):
    if part:
        p = {k: _to_param(n, whole[n]) for n, k in FFN_SHARDED if n in whole}
        p["g"] = rep['ffn_norm_g'][i][None]
        return p
    kind, j = i % N_MIXERS, i // N_MIXERS
    p = {k: _to_param(n, whole[n]) for n, k in MIXER_SHARDED[kind] if n in whole}
    for n, k in MIXER_REPLICATED[kind]:
        p[k] = rep[n][j][None] if rep[n].shape[-1] == DH else _lane_vec(rep[n][j])
    p["g"] = rep['mix_norm_g'][i][None]
    return p


def _part_grads(i, part, g):
    return {n: _from_grad(n, g[k]) for n, k in (FFN_SHARDED if part else MIXER_SHARDED[i % N_MIXERS])}


def _replicated_grads(grads):
    rep = {'mix_norm_g': jnp.concatenate([g[0]["g"] for g in grads]),
           'ffn_norm_g': jnp.concatenate([g[1]["g"] for g in grads])}
    for kind in range(N_MIXERS):
        for n, k in MIXER_REPLICATED[kind]:
            rep[n] = jnp.stack([grads[i][0][k] for i in range(kind, DEPTH, N_MIXERS)])
    return rep


def kernel(x, mix_norm_g, ffn_norm_g, conv_w_in, conv_b_in, conv_w_dw, conv_b_dw, conv_ln_g, conv_ln_b, conv_w_out, gdn_w_in, gdn_conv_w, gdn_a_log, gdn_dt_bias, gdn_o_norm_g, gdn_w_out, fox_w_in, fox_b_f, fox_q_norm_g, fox_k_norm_g, fox_w_out, ffn_w_up, ffn_w_dw, ffn_w_down, loss_target, m_mix_norm_g, m_ffn_norm_g, m_conv_w_in, m_conv_b_in, m_conv_w_dw, m_conv_b_dw, m_conv_ln_g, m_conv_ln_b, m_conv_w_out, m_gdn_w_in, m_gdn_conv_w, m_gdn_a_log, m_gdn_dt_bias, m_gdn_o_norm_g, m_gdn_w_out, m_fox_w_in, m_fox_b_f, m_fox_q_norm_g, m_fox_k_norm_g, m_fox_w_out, m_ffn_w_up, m_ffn_w_dw, m_ffn_w_down, v_mix_norm_g, v_ffn_norm_g, v_conv_w_in, v_conv_b_in, v_conv_w_dw, v_conv_b_dw, v_conv_ln_g, v_conv_ln_b, v_conv_w_out, v_gdn_w_in, v_gdn_conv_w, v_gdn_a_log, v_gdn_dt_bias, v_gdn_o_norm_g, v_gdn_w_out, v_fox_w_in, v_fox_b_f, v_fox_q_norm_g, v_fox_k_norm_g, v_fox_w_out, v_ffn_w_up, v_ffn_w_dw, v_ffn_w_down):
    given = dict(zip(
        WEIGHTS + ["m_" + n for n in WEIGHTS] + ["v_" + n for n in WEIGHTS],
        (mix_norm_g, ffn_norm_g, conv_w_in, conv_b_in, conv_w_dw, conv_b_dw, conv_ln_g, conv_ln_b, conv_w_out, gdn_w_in, gdn_conv_w, gdn_a_log, gdn_dt_bias, gdn_o_norm_g, gdn_w_out, fox_w_in, fox_b_f, fox_q_norm_g, fox_k_norm_g, fox_w_out, ffn_w_up, ffn_w_dw, ffn_w_down,
         m_mix_norm_g, m_ffn_norm_g, m_conv_w_in, m_conv_b_in, m_conv_w_dw, m_conv_b_dw, m_conv_ln_g, m_conv_ln_b, m_conv_w_out, m_gdn_w_in, m_gdn_conv_w, m_gdn_a_log, m_gdn_dt_bias, m_gdn_o_norm_g, m_gdn_w_out, m_fox_w_in, m_fox_b_f, m_fox_q_norm_g, m_fox_k_norm_g, m_fox_w_out, m_ffn_w_up, m_ffn_w_dw, m_ffn_w_down,
         v_mix_norm_g, v_ffn_norm_g, v_conv_w_in, v_conv_b_in, v_conv_w_dw, v_conv_b_dw, v_conv_ln_g, v_conv_ln_b, v_conv_w_out, v_gdn_w_in, v_gdn_conv_w, v_gdn_a_log, v_gdn_dt_bias, v_gdn_o_norm_g, v_gdn_w_out, v_fox_w_in, v_fox_b_f, v_fox_q_norm_g, v_fox_k_norm_g, v_fox_w_out, v_ffn_w_up, v_ffn_w_dw, v_ffn_w_down)))

    me = 4 * lax.axis_index("x") + 2 * lax.axis_index("y") + lax.axis_index("c")
    mine = me.astype(jnp.int32).reshape(1)

    gathers, token = {}, jnp.zeros((1, 1), F32)
    for i in range(DEPTH):
        for part in (0, 1):
            mats, smalls = _part_entries(i, part)
            sent = [given[n][j].astype(BF16) for n, j, _ in mats]
            if smalls:
                sent.append(_pack_rows([given[n][j] for n, j, _ in smalls], 0))
            gathers[i, part], tok = _exchange_start(f"gather{i}{'mf'[part]}_start", sent, [False] * len(sent))
            token = token + tok[0:1, 0:1]

    def params_of(i, part, x_in):
        mats, smalls = _part_entries(i, part)
        after = token if (i, part) == (0, 0) else x_in
        lands, owns = _exchange_wait(f"gather{i}{'mf'[part]}_wait", gathers[i, part], after, me)
        whole, relaid = {}, {}
        for (n, _, k), land, own in zip(mats, lands, owns):
            if n in ROW_SHARDED:
                whole[n] = _unshard(n, _fill_own(land, own, me))
            else:
                relaid[k] = _unshard_cols(f"l{i}_{n}_unshard", n, land, own, mine)
        if smalls:
            shapes = [given[n].shape[1:] for n, _, _ in smalls]
            for (n, _, _), g in zip(smalls, _unpack_rows(_fill_own(lands[-1], owns[-1], me), shapes, 1)):
                whole[n] = _unshard(n, g)
        return {**_part_dict(i, part, whole, given), **relaid}

    grads, exchanges = [[None, None] for _ in range(DEPTH)], {}

    def on_grads(i, part, g):
        grads[i][part] = g
        mats, smalls = _part_entries(i, part)
        sent = [_reshard(n, g[k]) if n in ROW_SHARDED else
                _reshard_cols(f"l{i}_{n}_reshard", n, g[k], given[n].shape[-1]) for n, _, k in mats]
        if smalls:
            sent.append(_pack_rows([_reshard(n, _from_grad(n, g[k])) for n, _, k in smalls], 1))
        last = (i, part) == (0, 0)
        if last:
            sent.append(_pack_rep(_replicated_grads(grads)))
        exchanges[i, part], tok = _exchange_start(f"grads{i}{'mf'[part]}_start", sent,
                                                  [True] * (len(sent) - last) + [False] * last)
        tokens.append(tok)
        return tok

    tokens = []
    sq, dx = _local_step(x[0], loss_target[0], params_of, on_grads)
    loss = (0.5 / D) * lax.psum(sq[0, 0], ("x", "y", "c"))

    pieces = {n: [] for n in SHARDED}
    for i in range(DEPTH):
        for part in (0, 1):
            mats, smalls = _part_entries(i, part)
            lands, owns = _exchange_wait(f"grads{i}{'mf'[part]}_wait", exchanges[i, part], tokens[-1], me)
            if (i, part) == (0, 0):
                rep_piece = (lands[-1], owns[-1])
            for (n, _, _), land, own in zip(mats, lands, owns):
                pieces[n].append((land, own))
            if smalls:
                shapes = [given[n].shape[1:] for n, _, _ in smalls]
                for (n, _, _), land, own in zip(smalls, _unpack_rows(lands[len(mats)], shapes, 1),
                                                _unpack_rows(owns[len(mats)], shapes, 0)):
                    pieces[n].append((land, own))
    new = {}
    for n in SHARDED:
        lands = [l.reshape((N_DEV, -1, l.shape[-1])) for l, _ in pieces[n]]
        owns = [o.reshape((-1, o.shape[-1])) for _, o in pieces[n]]
        if owns[0].shape[0] % 8:
            lands = [jnp.stack(lands, axis=1).reshape((N_DEV, -1, lands[0].shape[-1]))]
            owns = [jnp.stack(owns).reshape((-1, owns[0].shape[-1]))]
        outs = _adam("adam_" + n, lands, owns, mine, _view2d(given[n]), _view2d(given["m_" + n]),
                     _view2d(given["v_" + n]))
        new[n] = [o.reshape(given[n].shape) for o in outs]
    packed = [_pack_rep({n: given[pre + n] for n in REPLICATED}) for pre in ("", "m_", "v_")]
    outs = _adam("adam_replicated", [rep_piece[0]], [rep_piece[1]], mine, *packed)
    unpacked = [_unpack_rep(o, {n: given[n].shape for n in REPLICATED}) for o in outs]
    for n in REPLICATED:
        new[n] = [u[n] for u in unpacked]
    return (loss, dx[None], *[new[n][0] for n in WEIGHTS], *[new[n][1] for n in WEIGHTS],
            *[new[n][2] for n in WEIGHTS], *[new[n][3] for n in WEIGHTS])
```

```python
import functools

import jax
import jax.numpy as jnp
import numpy as np
from jax import lax
from jax.experimental import pallas as pl
from jax.experimental.pallas import tpu as pltpu

F32 = jnp.float32
BF16 = jnp.bfloat16
HIGHEST = lax.Precision.HIGHEST
HIGH = lax.Precision.HIGH

N_DEV = 8
LANE = 128
EPS = 1e-6
DEPTH = 4
N_MIXERS = 3
HEADS = 8
DH = 128
D = HEADS * DH
D_FF = 2816
CONF_K, GDN_K, FFN_K = 31, 4, 3
GDN_CHUNK = 64
GDN_PAD = 4224
FOX_PAD = 3200
FOX_V_BLOCK = (2 * D + LANE) // DH
ADAM_LR, ADAM_B1, ADAM_B2, ADAM_EPS, ADAM_WD, ADAM_STEP = 0.001, 0.9, 0.999, 1e-08, 0.01, 10
VMEM_LIMIT = 56 * 1024 * 1024

WEIGHTS = ['mix_norm_g', 'ffn_norm_g', 'conv_w_in', 'conv_b_in', 'conv_w_dw', 'conv_b_dw', 'conv_ln_g', 'conv_ln_b',
           'conv_w_out', 'gdn_w_in', 'gdn_conv_w', 'gdn_a_log', 'gdn_dt_bias', 'gdn_o_norm_g', 'gdn_w_out', 'fox_w_in',
           'fox_b_f', 'fox_q_norm_g', 'fox_k_norm_g', 'fox_w_out', 'ffn_w_up', 'ffn_w_dw', 'ffn_w_down']
REPLICATED = ['mix_norm_g', 'ffn_norm_g', 'gdn_a_log', 'gdn_dt_bias', 'gdn_o_norm_g', 'fox_b_f', 'fox_q_norm_g',
              'fox_k_norm_g']
ROW_SHARDED = ['conv_w_out', 'gdn_w_out', 'fox_w_out', 'ffn_w_down']
MATRICES = ['conv_w_in', 'conv_w_out', 'gdn_w_in', 'gdn_w_out', 'fox_w_in', 'fox_w_out', 'ffn_w_up', 'ffn_w_down']
SHARDED = [n for n in WEIGHTS if n not in REPLICATED]


def _params(*sem):
    return pltpu.CompilerParams(dimension_semantics=sem, vmem_limit_bytes=VMEM_LIMIT)


def _tile(n, cap):
    if n <= cap:
        return n
    d = (cap // LANE) * LANE
    while d >= LANE:
        if n % d == 0:
            return d
        d -= LANE
    raise ValueError(f"no lane-aligned tile of {n} under {cap}")


def _raw_dot(a, b, ca, cb, hp):
    batch = ((0,), (0,)) if a.ndim == 3 else ((), ())
    dn = (((ca,), (cb,)), batch)
    if hp:
        return lax.dot_general(a.astype(F32), b.astype(F32), dn, precision=HIGH, preferred_element_type=F32)
    return lax.dot_general(a.astype(BF16), b.astype(BF16), dn, preferred_element_type=F32)


def _raw_nn(a, b, hp=False):
    return _raw_dot(a, b, a.ndim - 1, b.ndim - 2, hp)


def _raw_nt(a, b, hp=False):
    return _raw_dot(a, b, a.ndim - 1, b.ndim - 1, hp)


def _raw_tn(a, b, hp=False):
    return _raw_dot(a, b, a.ndim - 2, b.ndim - 2, hp)


@functools.partial(jax.custom_vjp, nondiff_argnums=(2,))
def _nn(a, b, hp):
    return _raw_nn(a, b, hp)


def _nn_fwd(a, b, hp):
    return _raw_nn(a, b, hp), (a, b)


def _nn_bwd(hp, res, g):
    a, b = res
    return _raw_nt(g, b, hp), _raw_tn(a, g, hp)


_nn.defvjp(_nn_fwd, _nn_bwd)


@functools.partial(jax.custom_vjp, nondiff_argnums=(2,))
def _nt(a, b, hp):
    return _raw_nt(a, b, hp)


def _nt_fwd(a, b, hp):
    return _raw_nt(a, b, hp), (a, b)


def _nt_bwd(hp, res, g):
    a, b = res
    return _raw_nn(g, b, hp), _raw_tn(g, a, hp)


_nt.defvjp(_nt_fwd, _nt_bwd)


def _shift_down(x, s):
    if s == 0:
        return x
    t = lax.broadcasted_iota(jnp.int32, x.shape, 0)
    return jnp.where(t >= s, pltpu.roll(x, s, axis=0), 0.0)


def _shift_up(x, s):
    if s == 0:
        return x
    n = x.shape[0]
    t = lax.broadcasted_iota(jnp.int32, x.shape, 0)
    return jnp.where(t < n - s, pltpu.roll(x, n - s, axis=0), 0.0)


def _row(w, k):
    r = lax.broadcasted_iota(jnp.int32, w.shape, 0)
    return jnp.sum(jnp.where(r == k, w, 0.0), axis=0, keepdims=True)


@jax.custom_vjp
def _dwconv(x, w):
    taps = w.shape[0]
    y = _row(w, taps - 1) * x
    for k in range(taps - 1):
        y = y + _row(w, k) * _shift_down(x, taps - 1 - k)
    return y


def _dwconv_fwd(x, w):
    return _dwconv(x, w), (x, w)


def _dwconv_bwd(res, dy):
    x, w = res
    taps = w.shape[0]
    r = lax.broadcasted_iota(jnp.int32, w.shape, 0)
    dx = _row(w, taps - 1) * dy
    dw = jnp.where(r == taps - 1, jnp.sum(dy * x, axis=0, keepdims=True), 0.0)
    for k in range(taps - 1):
        up = _shift_up(dy, taps - 1 - k)
        dx = dx + _row(w, k) * up
        dw = dw + jnp.where(r == k, jnp.sum(up * x, axis=0, keepdims=True), 0.0)
    return dx, dw


_dwconv.defvjp(_dwconv_fwd, _dwconv_bwd)


def _sigmoid(x):
    return 1.0 / (1.0 + jnp.exp(-x))


def _silu(x):
    return x * _sigmoid(x)


def _softplus(x):
    return jnp.maximum(x, 0.0) + jnp.log(1.0 + jnp.exp(-jnp.abs(x)))


def _head_scale(x, fn):
    tm = x.shape[0]
    x3 = x.reshape(tm, HEADS, DH)
    return (x3 * fn(jnp.sum(x3 * x3, axis=-1, keepdims=True))).reshape(tm, HEADS * DH)


def _tile_lanes(g):
    return jnp.concatenate([g] * HEADS, axis=1)


def _expand_heads(v, first):
    lane = lax.broadcasted_iota(jnp.int32, (LANE, HEADS * DH), 0)
    col = lax.broadcasted_iota(jnp.int32, (LANE, HEADS * DH), 1)
    sel = (lane == col // DH + first).astype(F32)
    return _nn(v, sel, True)


def _f_rms(x, g):
    return (x * lax.rsqrt(jnp.mean(x * x, axis=-1, keepdims=True) + EPS) * g,)


def _f_rms_res(x, g):
    return (_f_rms(x, g)[0], x)


def _f_conf_glu_conv(u, b, w):
    c = u.shape[1] // 2
    return (_dwconv((u[:, :c] + b[:, :c]) * _sigmoid(u[:, c:] + b[:, c:]), w),)


def _f_conf_ln_silu(cv, b_dw, ln_g, ln_b):
    u = cv + b_dw
    xc = u - jnp.mean(u, axis=-1, keepdims=True)
    y = xc * lax.rsqrt(jnp.mean(xc * xc, axis=-1, keepdims=True) + EPS) * ln_g + ln_b
    return (_silu(y),)


def _f_conv_silu(u, w):
    return (_silu(_dwconv(u, w)),)


def _f_gdn_gates(qk, ab, a_log, dt_bias):
    qn = _head_scale(qk[:, :D], lambda ss: lax.rsqrt(ss + EPS)) * (DH ** -0.5)
    kn = _head_scale(qk[:, D:], lambda ss: lax.rsqrt(ss + EPS))
    g = -jnp.exp(a_log) * _softplus(ab + dt_bias)
    beta = _sigmoid(ab)
    return qn, kn, _expand_heads(g, 0), _expand_heads(beta, HEADS)


def _f_gdn_prescan(q, k, v, gb, bb):
    c = GDN_CHUNK
    n = q.shape[0] // c
    r3 = lambda t: t.reshape(n, c, DH)
    q3, k3, v3, g3, b3 = r3(q), r3(k), r3(v), r3(gb), r3(bb)
    ii = lax.broadcasted_iota(jnp.int32, (n, c, c), 1)
    jj = lax.broadcasted_iota(jnp.int32, (n, c, c), 2)
    lower, strict = ii >= jj, ii > jj
    gcb = _nn(lower.astype(F32), g3, True)
    gi = gcb[:, :, :c]
    gj = jnp.swapaxes(gi, 1, 2)
    decay = jnp.where(lower, jnp.exp(jnp.where(lower, gi - gj, 0.0)), 0.0)
    kb, vb = k3 * b3, v3 * b3
    a_mat = jnp.where(strict, _nt(kb, k3, False) * decay, 0.0)
    p = -a_mat
    t_mat = (ii == jj).astype(F32) + p
    for _ in range(5):
        p = _nn(p, p, True)
        t_mat = t_mat + _nn(t_mat, p, True)
    eg = jnp.exp(gcb)
    u = _nn(t_mat, vb, False)
    w = _nn(t_mat, kb * eg, False)
    qk = jnp.where(lower, _nt(q3, k3, False) * decay, 0.0)
    qg = q3 * eg
    g_last = jnp.sum(g3, axis=1, keepdims=True)
    kd = k3 * jnp.exp(g_last - gcb)
    r2 = lambda t: t.reshape(n * c, DH)
    return r2(u), r2(w), r2(qg), qk, r2(kd), jnp.exp(g_last)


def _f_gdn_post(o, z, o_g):
    on = _head_scale(o, lambda ss: lax.rsqrt(ss / DH + EPS)) * _tile_lanes(o_g)
    return (on * _silu(z),)


def _f_fox_pre(qkf, q_g, k_g, b_f):
    qn = _head_scale(qkf[:, :D], lambda ss: lax.rsqrt(ss / DH + EPS)) * _tile_lanes(q_g)
    kn = _head_scale(qkf[:, D:2 * D], lambda ss: lax.rsqrt(ss / DH + EPS)) * _tile_lanes(k_g)
    return qn, kn, -_softplus(-(qkf[:, 2 * D:] + b_f))


def _f_ffn_mid(u, w):
    c = u.shape[1] // 2
    return (_silu(_dwconv(u[:, :c], w[:, :c])) * _dwconv(u[:, c:], w[:, c:]),)


def _seg_fwd(name, f, grid, ins, in_specs, out_shapes, out_specs):
    n_in = len(ins)

    def body(*refs):
        outs = f(*[r[...].astype(F32) for r in refs[:n_in]])
        for r, o in zip(refs[n_in:], outs):
            r[...] = o.astype(r.dtype)

    return pl.pallas_call(body, grid=grid, in_specs=in_specs, out_specs=out_specs, out_shape=out_shapes, name=name,
                          compiler_params=_params(*(["parallel"] * len(grid))))(*ins)


def _seg_bwd(name, f, grid, ins, in_specs, douts, dout_specs, want, into=None):
    into = into or {}
    n_in, n_dy = len(ins), len(douts)
    diff = [i for i, w in enumerate(want) if w is not None]
    kept = [i for i in diff if i in into]
    out_shapes = [jax.ShapeDtypeStruct(ins[i].shape, into[i].dtype if i in into else F32 if want[i] == "acc" else want[i])
                  for i in diff]
    out_specs = [in_specs[i] for i in diff]
    acc_axis = len(grid) - 1

    def body(*refs):
        vals = [r[...].astype(F32) for r in refs[:n_in]]
        dys = [r[...].astype(F32) for r in refs[n_in:n_in + n_dy]]
        out_refs = refs[n_in + n_dy + len(kept):]

        def g(*dv):
            full = list(vals)
            for i, v in zip(diff, dv):
                full[i] = v
            return f(*full)

        _, vjp = jax.vjp(g, *[vals[i] for i in diff])
        grads = vjp(tuple(dys))
        first = pl.program_id(acc_axis) == 0
        for i, r, gr in zip(diff, out_refs, grads):
            if want[i] == "acc":
                @pl.when(first)
                def _(r=r, gr=gr):
                    r[...] = gr

                @pl.when(jnp.logical_not(first))
                def _(r=r, gr=gr):
                    r[...] += gr
            else:
                r[...] = gr.astype(r.dtype)

    sem = ["parallel"] * (len(grid) - 1) + ["arbitrary"]
    untouched = [pl.BlockSpec(memory_space=pl.ANY)] * len(kept)
    aliases = {n_in + n_dy + e: diff.index(i) for e, i in enumerate(kept)}
    return pl.pallas_call(body, grid=grid, in_specs=list(in_specs) + list(dout_specs) + untouched, out_specs=out_specs,
                          out_shape=out_shapes, input_output_aliases=aliases, name=name,
                          compiler_params=_params(*sem))(*ins, *douts, *[into[i] for i in kept])


def _rows(tm, width, col=0):
    return pl.BlockSpec((tm, width), lambda i, col=col: (i, col))


def _const(shape):
    return pl.BlockSpec(shape, lambda i: (0,) * len(shape))


def _cols(t, tc, off=0):
    return pl.BlockSpec((t, tc), lambda j, off=off: (0, j + off))


def _grid2(specs):
    return [pl.BlockSpec(s.block_shape, lambda j, i, f=s.index_map: f(j)) for s in specs]


def _mm(name, a, b, *, ta=False, tb=False, res=None, out_dtype=F32, after=None):
    k_dim, m = (a.shape[0], a.shape[1]) if ta else (a.shape[1], a.shape[0])
    n = b.shape[0] if tb else b.shape[1]
    tm, tn, tk = _tile(m, 1408), _tile(n, 1408), _tile(k_dim, 1408)
    nk = k_dim // tk
    grid = (m // tm, n // tn, nk)
    a_spec = pl.BlockSpec((tk, tm), lambda i, j, k: (k, i)) if ta else pl.BlockSpec((tm, tk), lambda i, j, k: (i, k))
    b_spec = pl.BlockSpec((tn, tk), lambda i, j, k: (j, k)) if tb else pl.BlockSpec((tk, tn), lambda i, j, k: (k, j))
    o_spec = pl.BlockSpec((tm, tn), lambda i, j, k: (i, j))
    dn = (((0 if ta else 1,), (1 if tb else 0,)), ((), ()))
    has_res = res is not None

    def body(*refs):
        a_ref, b_ref = refs[0], refs[1]
        res_ref = refs[2] if has_res else None
        o_ref = refs[n_in]
        p = lax.dot_general(a_ref[...].astype(BF16), b_ref[...].astype(BF16), dn, preferred_element_type=F32)

        def write(acc):
            if has_res:
                acc = acc + res_ref[...]
            if after is not None:
                acc = acc + refs[n_in - 1][0:1, 0:1]
            o_ref[...] = acc.astype(o_ref.dtype)

        if nk == 1:
            write(p)
        else:
            acc_ref = refs[-1]
            k = pl.program_id(2)

            @pl.when(k == 0)
            def _():
                acc_ref[...] = p

            @pl.when(k > 0)
            def _():
                acc_ref[...] += p

            @pl.when(k == nk - 1)
            def _():
                write(acc_ref[...])

    ins, specs = [a, b], [a_spec, b_spec]
    if has_res:
        ins.append(res)
        specs.append(o_spec)
    if after is not None:
        ins.append(after)
        specs.append(pl.BlockSpec(after.shape, lambda i, j, k: (0, 0)))
    n_in = len(ins)
    scratch = [pltpu.VMEM((tm, tn), F32)] if nk > 1 else []
    return pl.pallas_call(body, grid=grid, in_specs=specs, out_specs=o_spec, scratch_shapes=scratch,
                          out_shape=jax.ShapeDtypeStruct((m, n), out_dtype), name=name,
                          compiler_params=_params("parallel", "parallel", "arbitrary"))(*ins)


SCAN_HEADS = 2


def _head_specs(t):
    n = t // GDN_CHUNK
    col = pl.BlockSpec((t, SCAN_HEADS * DH), lambda h: (0, h))
    qk = pl.BlockSpec((SCAN_HEADS, n, GDN_CHUNK, GDN_CHUNK), lambda h: (h, 0, 0, 0))
    gl = pl.BlockSpec((SCAN_HEADS, n, 1, DH), lambda h: (h, 0, 0, 0))
    st = pl.BlockSpec((SCAN_HEADS, n, DH, DH), lambda h: (h, 0, 0, 0))
    return n, col, qk, gl, st


def _gdn_scan_fwd(name, u, w, qg, qk, kd, gl):
    t = u.shape[0]
    n, col, qk_spec, gl_spec, st_spec = _head_specs(t)
    c = GDN_CHUNK

    def body(u_ref, w_ref, qg_ref, qk_ref, kd_ref, gl_ref, o_ref, s_ref):
        def step(i, states):
            rows = pl.ds(pl.multiple_of(i * c, c), c)
            out = []
            for h, s in enumerate(states):
                cols = slice(h * DH, (h + 1) * DH)
                s_ref[h, i] = s
                vn = u_ref[rows, cols] - _raw_nn(w_ref[rows, cols], s)
                o_ref[rows, cols] = _raw_nn(qg_ref[rows, cols], s) + _raw_nn(qk_ref[h, i], vn)
                out.append(s * gl_ref[h, i] + _raw_tn(kd_ref[rows, cols], vn))
            return tuple(out)

        lax.fori_loop(0, n, step, (jnp.zeros((DH, DH), F32),) * SCAN_HEADS)

    return pl.pallas_call(
        body, grid=(HEADS // SCAN_HEADS,), in_specs=[col, col, col, qk_spec, col, gl_spec], out_specs=[col, st_spec],
        out_shape=[jax.ShapeDtypeStruct((t, D), F32), jax.ShapeDtypeStruct((HEADS, n, DH, DH), F32)], name=name,
        compiler_params=_params("parallel"))(u, w, qg, qk, kd, gl)


def _gdn_scan_bwd(name, u, w, qg, qk, kd, gl, states, do):
    t = u.shape[0]
    n, col, qk_spec, gl_spec, st_spec = _head_specs(t)
    c = GDN_CHUNK

    def body(u_ref, w_ref, qg_ref, qk_ref, kd_ref, gl_ref, s_ref, do_ref,
             du_ref, dw_ref, dqg_ref, dqk_ref, dkd_ref, dgl_ref):
        def step(r, dstates):
            i = n - 1 - r
            rows = pl.ds(pl.multiple_of(i * c, c), c)
            out = []
            for h, ds in enumerate(dstates):
                cols = slice(h * DH, (h + 1) * DH)
                s, do_c, w_c = s_ref[h, i], do_ref[rows, cols], w_ref[rows, cols]
                vn = u_ref[rows, cols] - _raw_nn(w_c, s)
                dvn = _raw_tn(qk_ref[h, i], do_c) + _raw_nn(kd_ref[rows, cols], ds)
                du_ref[rows, cols] = dvn
                dw_ref[rows, cols] = -_raw_nt(dvn, s)
                dqg_ref[rows, cols] = _raw_nt(do_c, s)
                dqk_ref[h, i] = _raw_nt(do_c, vn)
                dkd_ref[rows, cols] = _raw_nt(vn, ds)
                dgl_ref[h, i] = jnp.sum(ds * s, axis=0, keepdims=True)
                out.append(_raw_tn(qg_ref[rows, cols], do_c) + ds * gl_ref[h, i] - _raw_tn(w_c, dvn))
            return tuple(out)

        lax.fori_loop(0, n, step, (jnp.zeros((DH, DH), F32),) * SCAN_HEADS)

    big = jax.ShapeDtypeStruct((t, D), F32)
    return pl.pallas_call(
        body, grid=(HEADS // SCAN_HEADS,), in_specs=[col, col, col, qk_spec, col, gl_spec, st_spec, col],
        out_specs=[col, col, col, qk_spec, col, gl_spec],
        out_shape=[big, big, big, jax.ShapeDtypeStruct(qk.shape, F32), big, jax.ShapeDtypeStruct(gl.shape, F32)],
        name=name, compiler_params=_params("parallel"))(u, w, qg, qk, kd, gl, states, do)


def _cumsum_rows(name, parts, reverse):
    t = parts[0].shape[0]
    blk = min(t, 256)
    nb = t // blk
    n_in = len(parts)

    def body(*refs):
        o_ref = refs[n_in]
        ii = lax.broadcasted_iota(jnp.int32, (blk, blk), 0)
        jj = lax.broadcasted_iota(jnp.int32, (blk, blk), 1)
        tri = ((ii <= jj) if reverse else (ii >= jj)).astype(F32)
        carry = jnp.zeros((1, LANE), F32)
        for b in (range(nb - 1, -1, -1) if reverse else range(nb)):
            rows = pl.ds(b * blk, blk)
            x = refs[0][rows, :]
            for r in refs[1:n_in]:
                x = x + r[rows, :]
            o_ref[rows, :] = lax.dot_general(tri, x, (((1,), (0,)), ((), ())), precision=HIGHEST,
                                             preferred_element_type=F32) + carry
            carry = carry + jnp.sum(x, axis=0, keepdims=True)

    return pl.pallas_call(body, out_shape=jax.ShapeDtypeStruct((t, LANE), F32), name=name,
                          compiler_params=_params())(*parts)


def _fox_blocks(t):
    blk = 512 if t % 512 == 0 and t >= 2048 else 256 if t % 256 == 0 and t >= 1024 else 128
    return blk, t // blk


def _fox_logits(q, k, cq, ck, diagonal):
    s = _raw_nt(q, k) * (DH ** -0.5) + cq - ck
    if not diagonal:
        return s
    rows = lax.broadcasted_iota(jnp.int32, s.shape, 0)
    cols = lax.broadcasted_iota(jnp.int32, s.shape, 1)
    return jnp.where(cols <= rows, s, -jnp.inf)


def _fox_fwd(name, qn, kn, proj, c_col, c_row):
    t = qn.shape[0]
    blk, nb = _fox_blocks(t)
    voff = FOX_V_BLOCK

    def body(q_ref, k_ref, v_ref, cc_ref, cr_ref, o_ref, lse_ref):
        i = pl.program_id(1)
        q, cq = q_ref[...], cc_ref[...]

        def step(j, carry, diagonal=False):
            m, l, acc = carry
            rows = pl.ds(pl.multiple_of(j * blk, blk), blk)
            s = _fox_logits(q, k_ref[rows, :], cq, cr_ref[j], diagonal)
            m_new = jnp.maximum(m, jnp.max(s, axis=1, keepdims=True))
            p = jnp.exp(s - m_new)
            alpha = jnp.exp(m - m_new)
            return m_new, alpha * l + jnp.sum(p, axis=1, keepdims=True), alpha * acc + _raw_nn(p, v_ref[rows, :])

        init = (jnp.full((blk, 1), -1e30, F32), jnp.zeros((blk, 1), F32), jnp.zeros((blk, DH), F32))
        m, l, acc = step(i, lax.fori_loop(0, i, step, init), True)
        o_ref[...] = acc / l
        lse_ref[...] = m + jnp.log(l)

    tile = pl.BlockSpec((blk, DH), lambda h, i: (i, h))
    colv = pl.BlockSpec((None, blk, 1), lambda h, i: (h, i, 0))
    return pl.pallas_call(
        body, grid=(HEADS, nb),
        in_specs=[tile, pl.BlockSpec((t, DH), lambda h, i: (0, h)), pl.BlockSpec((t, DH), lambda h, i: (0, voff + h)),
                  colv, pl.BlockSpec((None, nb, 1, blk), lambda h, i: (h, 0, 0, 0))],
        out_specs=[tile, colv],
        out_shape=[jax.ShapeDtypeStruct((t, D), F32), jax.ShapeDtypeStruct((HEADS, t, 1), F32)], name=name,
        compiler_params=_params("parallel", "parallel"))(qn, kn, proj, c_col, c_row)


def _fox_dq(name, qn, kn, proj, c_col, c_row, o, lse, do):
    t = qn.shape[0]
    blk, nb = _fox_blocks(t)
    voff = FOX_V_BLOCK

    def body(q_ref, k_ref, v_ref, cc_ref, cr_ref, o_ref, lse_ref, do_ref, dq_ref, dcc_ref, dl_ref):
        i = pl.program_id(1)
        q, cq, do_b, lse_b = q_ref[...], cc_ref[...], do_ref[...], lse_ref[...]
        delta = jnp.sum(do_b * o_ref[...], axis=1, keepdims=True)

        def step(j, carry, diagonal=False):
            dq, dcc = carry
            rows = pl.ds(pl.multiple_of(j * blk, blk), blk)
            k = k_ref[rows, :]
            p = jnp.exp(_fox_logits(q, k, cq, cr_ref[j], diagonal) - lse_b)
            ds = p * (_raw_nt(do_b, v_ref[rows, :]) - delta)
            return dq + _raw_nn(ds, k), dcc + jnp.sum(ds, axis=1, keepdims=True)

        init = (jnp.zeros((blk, DH), F32), jnp.zeros((blk, 1), F32))
        dq, dcc = step(i, lax.fori_loop(0, i, step, init), True)
        dq_ref[...] = dq * (DH ** -0.5)
        dcc_ref[...] = dcc
        dl_ref[...] = delta

    tile = pl.BlockSpec((blk, DH), lambda h, i: (i, h))
    colv = pl.BlockSpec((None, blk, 1), lambda h, i: (h, i, 0))
    vec = jax.ShapeDtypeStruct((HEADS, t, 1), F32)
    return pl.pallas_call(
        body, grid=(HEADS, nb),
        in_specs=[tile, pl.BlockSpec((t, DH), lambda h, i: (0, h)), pl.BlockSpec((t, DH), lambda h, i: (0, voff + h)),
                  colv, pl.BlockSpec((None, nb, 1, blk), lambda h, i: (h, 0, 0, 0)), tile, colv, tile],
        out_specs=[tile, colv, colv],
        out_shape=[jax.ShapeDtypeStruct((t, D), F32), vec, vec], name=name,
        compiler_params=_params("parallel", "parallel"))(qn, kn, proj, c_col, c_row, o, lse, do)


def _fox_dkv(name, qn, kn, proj, c_col, c_row, lse, delta, do):
    t = qn.shape[0]
    blk, nb = _fox_blocks(t)
    voff = FOX_V_BLOCK

    def body(q_ref, k_ref, v_ref, cc_ref, cr_ref, lse_ref, dl_ref, do_ref, dk_ref, dv_ref, dcr_ref):
        j = pl.program_id(1)
        k, v, ck = k_ref[...], v_ref[...], cr_ref[...]

        def step(i, carry, diagonal=False):
            dk, dv, dcr = carry
            rows = pl.ds(pl.multiple_of(i * blk, blk), blk)
            q, do_b = q_ref[rows, :], do_ref[rows, :]
            p = jnp.exp(_fox_logits(q, k, cc_ref[rows, :], ck, diagonal) - lse_ref[rows, :])
            ds = p * (_raw_nt(do_b, v) - dl_ref[rows, :])
            return dk + _raw_tn(ds, q), dv + _raw_tn(p, do_b), dcr - jnp.sum(ds, axis=0, keepdims=True)

        init = (jnp.zeros((blk, DH), F32), jnp.zeros((blk, DH), F32), jnp.zeros((1, blk), F32))
        dk, dv, dcr = lax.fori_loop(j + 1, nb, step, step(j, init, True))
        dk_ref[...] = dk * (DH ** -0.5)
        dv_ref[...] = dv.astype(dv_ref.dtype)
        dcr_ref[...] = dcr

    full = pl.BlockSpec((t, DH), lambda h, j: (0, h))
    colf = pl.BlockSpec((None, t, 1), lambda h, j: (h, 0, 0))
    tile = pl.BlockSpec((blk, DH), lambda h, j: (j, h))
    rowv = pl.BlockSpec((None, None, 1, blk), lambda h, j: (h, j, 0, 0))
    tile_v = pl.BlockSpec((blk, DH), lambda h, j: (j, voff + h))
    big = jax.ShapeDtypeStruct((t, D), F32)
    return pl.pallas_call(
        body, grid=(HEADS, nb),
        in_specs=[full, tile, tile_v, colf, rowv, colf, colf, full],
        out_specs=[tile, tile_v, rowv],
        out_shape=[big, jax.ShapeDtypeStruct(proj.shape, BF16), jax.ShapeDtypeStruct((HEADS, nb, 1, blk), F32)], name=name,
        compiler_params=_params("parallel", "parallel"))(qn, kn, proj, c_col, c_row, lse, delta, do)


def _loss_head(y, target):
    t = y.shape[0]
    tm = min(t, 512)

    def body(y_ref, t_ref, dy_ref, sum_ref):
        err = y_ref[...] - t_ref[...]
        dy_ref[...] = err * (1.0 / D)
        part = jnp.sum(jnp.sum(err * err, axis=1, keepdims=True), axis=0, keepdims=True)

        @pl.when(pl.program_id(0) == 0)
        def _():
            sum_ref[...] = jnp.zeros_like(sum_ref)

        sum_ref[...] += jnp.broadcast_to(part, sum_ref.shape)

    return pl.pallas_call(
        body, grid=(t // tm,), in_specs=[_rows(tm, D), _rows(tm, D)], out_specs=[_rows(tm, D), _const((1, LANE))],
        out_shape=[jax.ShapeDtypeStruct((t, D), F32), jax.ShapeDtypeStruct((1, LANE), F32)], name="loss_head",
        compiler_params=_params("arbitrary"))(y, target)


ADAM_BLOCK_BYTES = 3 * 1024 * 1024


def _adam(name, lands, owns, mine, w, m, v):
    layers = len(lands)
    r, c = owns[0].shape
    block_bytes = lambda rows: N_DEV * rows * c * lands[0].dtype.itemsize
    tr = r
    for cand in (512, 256, 128, 64, 32, 16):
        if block_bytes(tr) > ADAM_BLOCK_BYTES and r % cand == 0:
            tr = cand
    nr = r // tr

    def body(*refs):
        me_ref, land_refs, own_refs = refs[0], refs[1:1 + layers], refs[1 + layers:1 + 2 * layers]
        w_ref, m_ref, v_ref, g_ref, d_ref, nm_ref, nv_ref = refs[1 + 2 * layers:]
        layer = pl.program_id(0)
        for l in range(layers):
            @pl.when(layer == l)
            def _(l=l):
                g = jnp.zeros((tr, c), F32)
                for s in range(N_DEV):
                    g = g + jnp.where(me_ref[0] == s, own_refs[l][...], land_refs[l][s]).astype(F32)
                nm = ADAM_B1 * m_ref[...] + (1.0 - ADAM_B1) * g
                nv = ADAM_B2 * v_ref[...] + (1.0 - ADAM_B2) * (g * g)
                m_hat = nm / (1.0 - ADAM_B1 ** ADAM_STEP)
                v_hat = nv / (1.0 - ADAM_B2 ** ADAM_STEP)
                g_ref[...] = g
                d_ref[...] = -ADAM_LR * (m_hat / (jnp.sqrt(v_hat) + ADAM_EPS) + ADAM_WD * w_ref[...])
                nm_ref[...] = nm
                nv_ref[...] = nv

    at = lambda l: (lambda layer, i: jnp.where(layer == l, i, 0))
    land_specs = [pl.BlockSpec((N_DEV, tr, c), lambda layer, i, f=at(l): (0, f(layer, i), 0)) for l in range(layers)]
    own_specs = [pl.BlockSpec((tr, c), lambda layer, i, f=at(l): (f(layer, i), 0)) for l in range(layers)]
    blk = pl.BlockSpec((tr, c), lambda layer, i: (layer * nr + i, 0))
    out = jax.ShapeDtypeStruct(w.shape, F32)
    return pl.pallas_call(
        body, grid=(layers, nr),
        in_specs=[pl.BlockSpec(memory_space=pltpu.SMEM)] + land_specs + own_specs + [blk, blk, blk],
        out_specs=[blk] * 4, out_shape=[out] * 4, name=name,
        compiler_params=_params("arbitrary", "arbitrary"))(mine, *lands, *owns, w, m, v)


_HBM = pl.BlockSpec(memory_space=pltpu.HBM)
_SEM = pl.BlockSpec(memory_space=pltpu.SEMAPHORE)
_EFFECT = pltpu.SideEffectType.DATAFLOW_SIDE_EFFECTING


def _each_peer(x, y, c):
    flip = lambda v, bit: 1 - v if bit else v
    for p in range(1, N_DEV):
        px, py, pc = flip(x, p & 4), flip(y, p & 2), flip(c, p & 1)
        yield p, (px, py, pc), 4 * px + 2 * py + pc


def _sem(a, p):
    return a * (N_DEV - 1) + p - 1


def _exchange_start(name, arrays, scatter):
    n = len(arrays)
    lands = [lax.empty((N_DEV,) + (a.shape[1:] if sc else a.shape), a.dtype) for a, sc in zip(arrays, scatter)]

    def body(*refs):
        in_refs, land_refs = refs[:n], refs[n:2 * n]
        send_sems, recv_sems, token = refs[2 * n], refs[2 * n + 1], refs[-1]
        x, y, c = lax.axis_index("x"), lax.axis_index("y"), lax.axis_index("c")
        me = 4 * x + 2 * y + c
        for p, coords, peer in _each_peer(x, y, c):
            for a in range(n):
                pltpu.make_async_remote_copy(
                    src_ref=in_refs[a].at[peer] if scatter[a] else in_refs[a], dst_ref=land_refs[a].at[me],
                    send_sem=send_sems.at[_sem(a, p)], recv_sem=recv_sems.at[_sem(a, p)], device_id=coords,
                    device_id_type=pl.DeviceIdType.MESH).start()
        token[...] = jnp.zeros_like(token)

    sems = pltpu.SemaphoreType.DMA((n * (N_DEV - 1),))
    hbm = lambda a: pltpu.HBM(a.shape, a.dtype)
    out = pl.pallas_call(
        body, name=name,
        out_shape=(sems, sems, *[hbm(a) for a in arrays], *[hbm(l) for l in lands],
                   jax.ShapeDtypeStruct((8, LANE), F32)),
        in_specs=[_HBM] * (2 * n), out_specs=(_SEM, _SEM, *[_HBM] * (2 * n), pl.BlockSpec(memory_space=pltpu.VMEM)),
        input_output_aliases={i: 2 + i for i in range(2 * n)},
        compiler_params=pltpu.CompilerParams(has_side_effects=_EFFECT),
    )(*[pltpu.with_memory_space_constraint(a, pltpu.HBM) for a in arrays],
      *[pltpu.with_memory_space_constraint(l, pltpu.HBM) for l in lands])
    return (out[0], out[1], list(out[2:2 + n]), list(out[2 + n:2 + 2 * n]), scatter), out[-1]


def _exchange_wait(name, started, after, me):
    send_sems, recv_sems, sent, lands, scatter = started
    n = len(sent)

    def body(*refs):
        in_refs, land_refs = refs[:n], refs[n:2 * n]
        send_sems, recv_sems = refs[2 * n], refs[2 * n + 1]
        x, y, c = lax.axis_index("x"), lax.axis_index("y"), lax.axis_index("c")
        for p, coords, peer in _each_peer(x, y, c):
            for a in range(n):
                cp = pltpu.make_async_remote_copy(
                    src_ref=in_refs[a].at[peer] if scatter[a] else in_refs[a], dst_ref=land_refs[a].at[peer],
                    send_sem=send_sems.at[_sem(a, p)], recv_sem=recv_sems.at[_sem(a, p)], device_id=coords,
                    device_id_type=pl.DeviceIdType.MESH)
                cp.wait_send()
                cp.wait_recv()

    hbm = lambda a: pltpu.HBM(a.shape, a.dtype)
    out = pl.pallas_call(
        body, name=name, out_shape=(*[hbm(a) for a in sent], *[hbm(l) for l in lands]),
        in_specs=[_HBM] * (2 * n) + [_SEM, _SEM, pl.BlockSpec(memory_space=pl.ANY)], out_specs=[_HBM] * (2 * n),
        input_output_aliases={i: i for i in range(2 * n)},
        compiler_params=pltpu.CompilerParams(has_side_effects=_EFFECT),
    )(*sent, *lands, send_sems, recv_sems, after)
    owns = [lax.dynamic_index_in_dim(out[a], me, 0, keepdims=False) if scatter[a] else out[a] for a in range(n)]
    return list(out[n:]), owns


def _fill_own(land, own, me):
    slot = lax.broadcasted_iota(jnp.int32, (N_DEV,) + (1,) * own.ndim, 0)
    return jnp.where(slot == me, own[None], land)


def _rms_fwd(name, x, g):
    t = x.shape[0]
    tm = min(t, 512)
    return _seg_fwd(name, _f_rms, (t // tm,), [x, g], [_rows(tm, D), _const((1, D))],
                    [jax.ShapeDtypeStruct((t, D), BF16)], [_rows(tm, D)])[0]


def _rms_bwd(name, x, g, dh, dres):
    t = x.shape[0]
    tm = min(t, 512)
    return _seg_bwd(name, _f_rms_res, (t // tm,), [x, g], [_rows(tm, D), _const((1, D))],
                    [dh, dres], [_rows(tm, D), _rows(tm, D)], [F32, "acc"])


FFN_TC = 256
CONF_TC = 128


def _ffn_specs(t):
    nf = D_FF // FFN_TC
    return nf, [_cols(t, 2 * FFN_TC), _cols(FFN_K, 2 * FFN_TC)], [_cols(t, FFN_TC)]


def _ffn_fwd(tag, x, p):
    t = x.shape[0]
    nf, in_specs, out_specs = _ffn_specs(t)
    h = _rms_fwd(tag + "_rms", x, p["g"])
    u0 = _mm(tag + "_up", h, p["w_up"])
    act = _seg_fwd(tag + "_mid", _f_ffn_mid, (nf,), [u0, p["w_dw"]], in_specs,
                   [jax.ShapeDtypeStruct((t, D_FF), BF16)], out_specs)[0]
    out = _mm(tag + "_down", act, p["w_down"], res=x)
    return out, (x, h, u0, act)


def _ffn_bwd(tag, saved, p, dout, after=None):
    x, h, u0, act = saved
    t = x.shape[0]
    nf, in_specs, out_specs = _ffn_specs(t)
    g = {"w_down": _mm(tag + "_dwdown", act, dout, ta=True, out_dtype=BF16)}
    dact = _mm(tag + "_dact", dout, p["w_down"], tb=True, out_dtype=BF16, after=after)
    du0, g["w_dw"] = _seg_bwd(tag + "_dmid", _f_ffn_mid, (nf, 1), [u0, p["w_dw"]], _grid2(in_specs), [dact],
                              _grid2(out_specs), [BF16, "acc"])
    g["w_up"] = _mm(tag + "_dwup", h, du0, ta=True, out_dtype=BF16)
    dh = _mm(tag + "_dh", du0, p["w_up"], tb=True, out_dtype=BF16)
    dx, g["g"] = _rms_bwd(tag + "_drms", x, p["g"], dh, dout)
    return dx, g


def _conf_specs(t):
    tm = min(t, 512)
    nc = D // CONF_TC
    vec = _const((1, D))
    glu_in = [_cols(t, 2 * CONF_TC), _cols(1, 2 * CONF_TC), _cols(CONF_K, CONF_TC)]
    return tm, nc, glu_in, [_cols(t, CONF_TC)], [_rows(tm, D), vec, vec, vec]


def _conf_fwd(tag, x, p):
    t = x.shape[0]
    tm, nc, glu_in, glu_out, ln_in = _conf_specs(t)
    h = _rms_fwd(tag + "_rms", x, p["g"])
    u = _mm(tag + "_in", h, p["w_in"])
    cv = _seg_fwd(tag + "_gluconv", _f_conf_glu_conv, (nc,), [u, p["b_in"], p["w_dw"]], glu_in,
                  [jax.ShapeDtypeStruct((t, D), F32)], glu_out)[0]
    act = _seg_fwd(tag + "_lnsilu", _f_conf_ln_silu, (t // tm,), [cv, p["b_dw"], p["ln_g"], p["ln_b"]], ln_in,
                   [jax.ShapeDtypeStruct((t, D), BF16)], [_rows(tm, D)])[0]
    out = _mm(tag + "_out", act, p["w_out"], res=x)
    return out, (x, h, u, cv, act)


def _conf_bwd(tag, saved, p, dout, after=None):
    x, h, u, cv, act = saved
    t = x.shape[0]
    tm, nc, glu_in, glu_out, ln_in = _conf_specs(t)
    g = {"w_out": _mm(tag + "_dwout", act, dout, ta=True, out_dtype=BF16)}
    dact = _mm(tag + "_dact", dout, p["w_out"], tb=True, out_dtype=BF16, after=after)
    dcv, g["b_dw"], g["ln_g"], g["ln_b"] = _seg_bwd(
        tag + "_dlnsilu", _f_conf_ln_silu, (t // tm,), [cv, p["b_dw"], p["ln_g"], p["ln_b"]], ln_in, [dact],
        [_rows(tm, D)], [F32, "acc", "acc", "acc"])
    du, g["b_in"], g["w_dw"] = _seg_bwd(tag + "_dgluconv", _f_conf_glu_conv, (nc, 1), [u, p["b_in"], p["w_dw"]],
                                        _grid2(glu_in), [dcv], _grid2(glu_out), [BF16, "acc", "acc"])
    g["w_in"] = _mm(tag + "_dwin", h, du, ta=True, out_dtype=BF16)
    dh = _mm(tag + "_dh", du, p["w_in"], tb=True, out_dtype=BF16)
    dx, g["g"] = _rms_bwd(tag + "_drms", x, p["g"], dh, dout)
    return dx, g


def _gdn_specs(t):
    tc, tm, rows = 256, min(t, 256), min(t, 512)
    nq = 3 * D // tc
    conv = ([_cols(t, tc), _cols(GDN_K, tc)], [_cols(t, tc)])
    gate_in = [_rows(tm, 2 * D, 0), _rows(tm, LANE, 4 * D // LANE), _const((1, LANE)), _const((1, LANE))]
    gate_out = [_rows(tm, D)] * 4
    head = pl.BlockSpec((rows, DH), lambda h, i: (i, h))
    headv = pl.BlockSpec((rows, DH), lambda h, i: (i, 2 * HEADS + h))
    nch = rows // GDN_CHUNK
    pre_in = [head, head, headv, head, head]
    pre_out = [head, head, head, pl.BlockSpec((None, nch, GDN_CHUNK, GDN_CHUNK), lambda h, i: (h, i, 0, 0)), head,
               pl.BlockSpec((None, nch, 1, DH), lambda h, i: (h, i, 0, 0))]
    post_in = [_rows(tm, D), _rows(tm, D, 3), _const((1, DH))]
    return tm, rows, nq, conv, gate_in, gate_out, pre_in, pre_out, post_in


def _gdn_fwd(tag, x, p):
    t = x.shape[0]
    tm, rows, nq, conv, gate_in, gate_out, pre_in, pre_out, post_in = _gdn_specs(t)
    n = t // GDN_CHUNK
    big = jax.ShapeDtypeStruct((t, D), F32)
    h = _rms_fwd(tag + "_rms", x, p["g"])
    proj = _mm(tag + "_in", h, p["w_in"])
    qkv = _seg_fwd(tag + "_conv", _f_conv_silu, (nq,), [proj, p["conv_w"]], conv[0],
                   [jax.ShapeDtypeStruct((t, 3 * D), F32)], conv[1])[0]
    gate_ins = [qkv, proj, p["a_log"], p["dt_bias"]]
    qn, kn, gb, bb = _seg_fwd(tag + "_gates", _f_gdn_gates, (t // tm,), gate_ins, gate_in, [big] * 4, gate_out)
    pre_ins = [qn, kn, qkv, gb, bb]
    pre_shapes = [big, big, big, jax.ShapeDtypeStruct((HEADS, n, GDN_CHUNK, GDN_CHUNK), F32), big,
                  jax.ShapeDtypeStruct((HEADS, n, 1, DH), F32)]
    pre = _seg_fwd(tag + "_prescan", _f_gdn_prescan, (HEADS, t // rows), pre_ins, pre_in, pre_shapes, pre_out)
    o, states = _gdn_scan_fwd(tag + "_scan", *pre)
    post_ins = [o, proj, p["o_g"]]
    act = _seg_fwd(tag + "_post", _f_gdn_post, (t // tm,), post_ins, post_in, [jax.ShapeDtypeStruct((t, D), BF16)],
                   [_rows(tm, D)])[0]
    out = _mm(tag + "_out", act, p["w_out"], res=x)
    return out, (x, h, proj, gate_ins, pre_ins, pre, states, post_ins, act)


def _gdn_bwd(tag, saved, p, dout, after=None):
    x, h, proj, gate_ins, pre_ins, pre, states, post_ins, act = saved
    t = x.shape[0]
    tm, rows, nq, conv, gate_in, gate_out, pre_in, pre_out, post_in = _gdn_specs(t)
    g = {"w_out": _mm(tag + "_dwout", act, dout, ta=True, out_dtype=BF16)}
    dact = _mm(tag + "_dact", dout, p["w_out"], tb=True, out_dtype=BF16, after=after)
    do, dproj, g["o_g"] = _seg_bwd(tag + "_dpost", _f_gdn_post, (t // tm,), post_ins, post_in, [dact],
                                   [_rows(tm, D)], [F32, BF16, "acc"])
    dpre = _gdn_scan_bwd(tag + "_dscan", *pre, states, do)
    dqn, dkn, dqkv, dgb, dbb = _seg_bwd(tag + "_dprescan", _f_gdn_prescan, (HEADS, t // rows), pre_ins, pre_in,
                                        dpre, pre_out, [F32] * 5)
    dqkv, dproj, g["a_log"], g["dt_bias"] = _seg_bwd(
        tag + "_dgates", _f_gdn_gates, (t // tm,), gate_ins, gate_in, [dqn, dkn, dgb, dbb], gate_out,
        [F32, BF16, "acc", "acc"], into={0: dqkv, 1: dproj})
    dproj, g["conv_w"] = _seg_bwd(tag + "_dconv", _f_conv_silu, (nq, 1), [proj, p["conv_w"]], _grid2(conv[0]),
                                  [dqkv], _grid2(conv[1]), [BF16, "acc"], into={0: dproj})
    g["w_in"] = _mm(tag + "_dwin", h, dproj, ta=True, out_dtype=BF16)
    dh = _mm(tag + "_dh", dproj, p["w_in"], tb=True, out_dtype=BF16)
    dx, g["g"] = _rms_bwd(tag + "_drms", x, p["g"], dh, dout)
    return dx, g


def _fox_specs(t):
    tm = min(t, 512)
    vec = _const((1, LANE))
    pre_in = [_rows(tm, 2 * D + LANE, 0), vec, vec, vec]
    pre_out = [_rows(tm, D), _rows(tm, D), _rows(tm, LANE)]
    return tm, pre_in, pre_out


def _per_head(c):
    return jnp.transpose(c[:, :HEADS])


def _per_lane(ch):
    return jnp.pad(jnp.transpose(ch), ((0, 0), (0, LANE - HEADS)))


def _fox_fwd_layer(tag, x, p):
    t = x.shape[0]
    tm, pre_in, pre_out = _fox_specs(t)
    blk, nb = _fox_blocks(t)
    h = _rms_fwd(tag + "_rms", x, p["g"])
    proj = _mm(tag + "_in", h, p["w_in"])
    pre_ins = [proj, p["q_g"], p["k_g"], p["b_f"]]
    qn, kn, lf = _seg_fwd(tag + "_pre", _f_fox_pre, (t // tm,), pre_ins, pre_in,
                          [jax.ShapeDtypeStruct((t, D), BF16)] * 2 + [jax.ShapeDtypeStruct((t, LANE), F32)], pre_out)
    ch = _per_head(_cumsum_rows(tag + "_cumsum", [lf], False))
    c_col, c_row = ch.reshape(HEADS, t, 1), ch.reshape(HEADS, nb, 1, blk)
    o, lse = _fox_fwd(tag + "_attn", qn, kn, proj, c_col, c_row)
    out = _mm(tag + "_out", o, p["w_out"], res=x)
    return out, (x, h, proj, pre_ins, qn, kn, c_col, c_row, o, lse)


def _fox_bwd_layer(tag, saved, p, dout, after=None):
    x, h, proj, pre_ins, qn, kn, c_col, c_row, o, lse = saved
    t = x.shape[0]
    tm, pre_in, pre_out = _fox_specs(t)
    g = {"w_out": _mm(tag + "_dwout", o, dout, ta=True, out_dtype=BF16)}
    do = _mm(tag + "_do", dout, p["w_out"], tb=True, after=after)
    dqn, dc_col, delta = _fox_dq(tag + "_dq", qn, kn, proj, c_col, c_row, o, lse, do)
    dkn, dproj, dc_row = _fox_dkv(tag + "_dkv", qn, kn, proj, c_col, c_row, lse, delta, do)
    dlf = _cumsum_rows(tag + "_dcumsum", [_per_lane(dc_col.reshape(HEADS, t)), _per_lane(dc_row.reshape(HEADS, t))],
                       True)
    dproj, g["q_g"], g["k_g"], g["b_f"] = _seg_bwd(
        tag + "_dpre", _f_fox_pre, (t // tm,), pre_ins, pre_in, [dqn, dkn, dlf], pre_out,
        [BF16, "acc", "acc", "acc"], into={0: dproj})
    g["w_in"] = _mm(tag + "_dwin", h, dproj, ta=True, out_dtype=BF16)
    dh = _mm(tag + "_dh", dproj, p["w_in"], tb=True, out_dtype=BF16)
    dx, g["g"] = _rms_bwd(tag + "_drms", x, p["g"], dh, dout)
    return dx, g


_MIXERS = ((_conf_fwd, _conf_bwd), (_gdn_fwd, _gdn_bwd), (_fox_fwd_layer, _fox_bwd_layer))


def _local_step(x, target, params_of, on_grads):
    saved, params = [], []
    for i in range(DEPTH):
        mp = params_of(i, 0, x)
        x, sm = _MIXERS[i % N_MIXERS][0](f"l{i}_mix", x, mp)
        fp = params_of(i, 1, x)
        x, sf = _ffn_fwd(f"l{i}_ffn", x, fp)
        saved.append((sm, sf))
        params.append((mp, fp))
    dx, sq = _loss_head(x, target)
    after = None
    for i in reversed(range(DEPTH)):
        dx, gf = _ffn_bwd(f"l{i}_ffn", saved[i][1], params[i][1], dx, after)
        after = on_grads(i, 1, gf)
        dx, gm = _MIXERS[i % N_MIXERS][1](f"l{i}_mix", saved[i][0], params[i][0], dx, after)
        after = on_grads(i, 0, gm)
    return sq, dx


def _unshard(name, g):
    axis = g.ndim - 2 if name in ROW_SHARDED else g.ndim - 1
    m = jnp.moveaxis(g, 0, axis - 1)
    return m.reshape(m.shape[:axis - 1] + (N_DEV * m.shape[axis],) + m.shape[axis + 1:])


def _reshard(name, full):
    axis = full.ndim - 2 if name in ROW_SHARDED else full.ndim - 1
    s = full.shape
    return jnp.moveaxis(full.reshape(s[:axis] + (N_DEV, s[axis] // N_DEV) + s[axis + 1:]), axis, 0)


def _pad_cols(a, width):
    return jnp.pad(a, [(0, 0)] * (a.ndim - 1) + [(0, width - a.shape[-1])])


def _lane_vec(v):
    return _pad_cols(v.reshape(1, -1), LANE)


REP_ROWS = 16


def _pack_rep(r):
    small = [_pad_cols(r[n].reshape(1, -1), LANE) for n in REPLICATED[2:]]
    row = jnp.concatenate(small + [jnp.zeros((1, D - LANE * len(small)), F32)], axis=1)
    pad = jnp.zeros((REP_ROWS - 2 * DEPTH - 1, D), F32)
    return jnp.concatenate([r['mix_norm_g'].reshape(DEPTH, D), r['ffn_norm_g'].reshape(DEPTH, D), row, pad], axis=0)


def _unpack_rep(a, shapes):
    out = {'mix_norm_g': a[:DEPTH], 'ffn_norm_g': a[DEPTH:2 * DEPTH]}
    for i, n in enumerate(REPLICATED[2:]):
        out[n] = a[2 * DEPTH:2 * DEPTH + 1, i * LANE:i * LANE + shapes[n][1]].reshape(shapes[n])
    return out


def _view2d(a):
    return a.reshape(-1, a.shape[-1])


MIXER_SHARDED = (
    (('conv_w_in', 'w_in'), ('conv_w_out', 'w_out'), ('conv_b_in', 'b_in'), ('conv_w_dw', 'w_dw'),
     ('conv_b_dw', 'b_dw'), ('conv_ln_g', 'ln_g'), ('conv_ln_b', 'ln_b')),
    (('gdn_w_in', 'w_in'), ('gdn_w_out', 'w_out'), ('gdn_conv_w', 'conv_w')),
    (('fox_w_in', 'w_in'), ('fox_w_out', 'w_out')),
)
FFN_SHARDED = (('ffn_w_up', 'w_up'), ('ffn_w_down', 'w_down'), ('ffn_w_dw', 'w_dw'))
MIXER_REPLICATED = (
    (),
    (('gdn_a_log', 'a_log'), ('gdn_dt_bias', 'dt_bias'), ('gdn_o_norm_g', 'o_g')),
    (('fox_b_f', 'b_f'), ('fox_q_norm_g', 'q_g'), ('fox_k_norm_g', 'k_g')),
)
ROW_VECTORS = ('conv_b_in', 'conv_b_dw', 'conv_ln_g', 'conv_ln_b')
INTERLEAVED = {'ffn_w_up': (D_FF, FFN_TC), 'ffn_w_dw': (D_FF, FFN_TC), 'conv_w_in': (D, CONF_TC),
               'conv_b_in': (D, CONF_TC)}
PACK_GROUP = 16 * D


def _part_entries(i, part):
    ent = [(n, i, k) for n, k in FFN_SHARDED] if part else [(n, i // N_MIXERS, k) for n, k in MIXER_SHARDED[i % N_MIXERS]]
    return [e for e in ent if e[0] in MATRICES], [e for e in ent if e[0] not in MATRICES]


def _interleave(a, half, blk):
    s = a.shape[:-1]
    return jnp.swapaxes(a.reshape(s + (2, half // blk, blk)), -3, -2).reshape(s + (2 * half,))


def _deinterleave(a, half, blk):
    s = a.shape[:-1]
    return jnp.swapaxes(a.reshape(s + (half // blk, 2, blk)), -3, -2).reshape(s + (2 * half,))


def _to_param(name, whole):
    if name in ROW_VECTORS:
        whole = whole[None]
    if name in INTERLEAVED:
        return _interleave(whole, *INTERLEAVED[name])
    if name == 'gdn_w_in':
        return _pad_cols(whole, GDN_PAD)
    if name == 'fox_w_in':
        return jnp.concatenate([whole[:, :2 * D], _pad_cols(whole[:, 3 * D:], LANE), whole[:, 2 * D:3 * D]], axis=1)
    return whole


def _from_grad(name, g):
    if name in INTERLEAVED:
        g = _deinterleave(g, *INTERLEAVED[name])
    if name in ROW_VECTORS:
        return g[0]
    if name == 'gdn_w_in':
        return g[:, :4 * D + 2 * HEADS]
    if name == 'fox_w_in':
        return jnp.concatenate([g[:, :2 * D], g[:, 2 * D + LANE:], g[:, 2 * D:2 * D + HEADS]], axis=1)
    return g


def _layer_col(name, n):
    if name in INTERLEAVED:
        half, blk = INTERLEAVED[name]
        return (n % half) // blk * (2 * blk) + (n // half) * blk + n % blk
    if name == 'fox_w_in':
        return np.where(n < 2 * D, n, np.where(n < 3 * D, n + LANE, n - D))
    return n


def _col_runs(name, shard_cols):
    dst = _layer_col(name, np.arange(N_DEV * shard_cols))
    runs, start = [[] for _ in range(N_DEV)], 0
    for k in range(1, dst.size + 1):
        if k == dst.size or dst[k] != dst[k - 1] + 1 or k % shard_cols == 0:
            runs[start // shard_cols].append((start % shard_cols, k - start, int(dst[start])))
            start = k
    width = {'gdn_w_in': GDN_PAD, 'fox_w_in': FOX_PAD}.get(name, dst.size)
    free = np.ones(width + 1, bool)
    free[dst] = False
    free[width] = False
    gaps, start = [], None
    for k in range(width + 1):
        if free[k] and start is None:
            start = k
        if not free[k] and start is not None:
            gaps.append((start, k - start))
            start = None
    return runs, width, gaps


def _unshard_cols(call, name, land, own, mine):
    _, r, c = land.shape
    runs, width, gaps = _col_runs(name, c)
    tr = _tile(r, 256)

    def body(me_ref, land_ref, own_ref, o_ref):
        for s in range(N_DEV):
            for src, ln, dst in runs[s]:
                o_ref[:, dst:dst + ln] = jnp.where(me_ref[0] == s, own_ref[:, src:src + ln], land_ref[s, :, src:src + ln])
        for start, ln in gaps:
            o_ref[:, start:start + ln] = jnp.zeros((tr, ln), o_ref.dtype)

    return pl.pallas_call(
        body, grid=(r // tr,),
        in_specs=[pl.BlockSpec(memory_space=pltpu.SMEM), pl.BlockSpec((N_DEV, tr, c), lambda i: (0, i, 0)),
                  pl.BlockSpec((tr, c), lambda i: (i, 0))],
        out_specs=pl.BlockSpec((tr, width), lambda i: (i, 0)), out_shape=jax.ShapeDtypeStruct((r, width), land.dtype),
        name=call, compiler_params=_params("parallel"))(mine, land, own)


def _reshard_cols(call, name, g, shard_cols):
    r = g.shape[0]
    runs, width, _ = _col_runs(name, shard_cols)
    tr = _tile(r, 256)

    def body(g_ref, o_ref):
        for s in range(N_DEV):
            for src, ln, dst in runs[s]:
                o_ref[s, :, src:src + ln] = g_ref[:, dst:dst + ln]

    return pl.pallas_call(
        body, grid=(r // tr,), in_specs=[pl.BlockSpec((tr, width), lambda i: (i, 0))],
        out_specs=pl.BlockSpec((N_DEV, tr, shard_cols), lambda i: (0, i, 0)),
        out_shape=jax.ShapeDtypeStruct((N_DEV, r, shard_cols), g.dtype), name=call,
        compiler_params=_params("parallel"))(g)


def _pack_rows(parts, lead):
    out = []
    for a in parts:
        flat = a.reshape(a.shape[:lead] + (-1,))
        size = flat.shape[-1]
        padded = -(-size // PACK_GROUP) * PACK_GROUP
        flat = jnp.pad(flat, [(0, 0)] * lead + [(0, padded - size)])
        out.append(flat.reshape(a.shape[:lead] + (padded // D, D)))
    return jnp.concatenate(out, axis=lead)


def _unpack_rows(packed, shapes, lead):
    out, row = [], 0
    head = packed.shape[:lead]
    for s in shapes:
        size = 1
        for d in s:
            size *= d
        rows = -(-size // PACK_GROUP) * (PACK_GROUP // D)
        part = lax.slice_in_dim(packed, row, row + rows, axis=lead)
        out.append(part.reshape(head + (rows * D,))[..., :size].reshape(head + tuple(s)))
        row += rows
    return out


def _part_dict(i, part, whole, rep):
    if part:
        p = {k: _to_param(n, whole[n]) for n, k in FFN_SHARDED if n in whole}
        p["g"] = rep['ffn_norm_g'][i][None]
        return p
    kind, j = i % N_MIXERS, i // N_MIXERS
    p = {k: _to_param(n, whole[n]) for n, k in MIXER_SHARDED[kind] if n in whole}
    for n, k in MIXER_REPLICATED[kind]:
        p[k] = rep[n][j][None] if rep[n].shape[-1] == DH else _lane_vec(rep[n][j])
    p["g"] = rep['mix_norm_g'][i][None]
    return p


def _part_grads(i, part, g):
    return {n: _from_grad(n, g[k]) for n, k in (FFN_SHARDED if part else MIXER_SHARDED[i % N_MIXERS])}


def _replicated_grads(grads):
    rep = {'mix_norm_g': jnp.concatenate([g[0]["g"] for g in grads]),
           'ffn_norm_g': jnp.concatenate([g[1]["g"] for g in grads])}
    for kind in range(N_MIXERS):
        for n, k in MIXER_REPLICATED[kind]:
            rep[n] = jnp.stack([grads[i][0][k] for i in range(kind, DEPTH, N_MIXERS)])
    return rep


def kernel(x, mix_norm_g, ffn_norm_g, conv_w_in, conv_b_in, conv_w_dw, conv_b_dw, conv_ln_g, conv_ln_b, conv_w_out, gdn_w_in, gdn_conv_w, gdn_a_log, gdn_dt_bias, gdn_o_norm_g, gdn_w_out, fox_w_in, fox_b_f, fox_q_norm_g, fox_k_norm_g, fox_w_out, ffn_w_up, ffn_w_dw, ffn_w_down, loss_target, m_mix_norm_g, m_ffn_norm_g, m_conv_w_in, m_conv_b_in, m_conv_w_dw, m_conv_b_dw, m_conv_ln_g, m_conv_ln_b, m_conv_w_out, m_gdn_w_in, m_gdn_conv_w, m_gdn_a_log, m_gdn_dt_bias, m_gdn_o_norm_g, m_gdn_w_out, m_fox_w_in, m_fox_b_f, m_fox_q_norm_g, m_fox_k_norm_g, m_fox_w_out, m_ffn_w_up, m_ffn_w_dw, m_ffn_w_down, v_mix_norm_g, v_ffn_norm_g, v_conv_w_in, v_conv_b_in, v_conv_w_dw, v_conv_b_dw, v_conv_ln_g, v_conv_ln_b, v_conv_w_out, v_gdn_w_in, v_gdn_conv_w, v_gdn_a_log, v_gdn_dt_bias, v_gdn_o_norm_g, v_gdn_w_out, v_fox_w_in, v_fox_b_f, v_fox_q_norm_g, v_fox_k_norm_g, v_fox_w_out, v_ffn_w_up, v_ffn_w_dw, v_ffn_w_down):
    given = dict(zip(
        WEIGHTS + ["m_" + n for n in WEIGHTS] + ["v_" + n for n in WEIGHTS],
        (mix_norm_g, ffn_norm_g, conv_w_in, conv_b_in, conv_w_dw, conv_b_dw, conv_ln_g, conv_ln_b, conv_w_out, gdn_w_in, gdn_conv_w, gdn_a_log, gdn_dt_bias, gdn_o_norm_g, gdn_w_out, fox_w_in, fox_b_f, fox_q_norm_g, fox_k_norm_g, fox_w_out, ffn_w_up, ffn_w_dw, ffn_w_down,
         m_mix_norm_g, m_ffn_norm_g, m_conv_w_in, m_conv_b_in, m_conv_w_dw, m_conv_b_dw, m_conv_ln_g, m_conv_ln_b, m_conv_w_out, m_gdn_w_in, m_gdn_conv_w, m_gdn_a_log, m_gdn_dt_bias, m_gdn_o_norm_g, m_gdn_w_out, m_fox_w_in, m_fox_b_f, m_fox_q_norm_g, m_fox_k_norm_g, m_fox_w_out, m_ffn_w_up, m_ffn_w_dw, m_ffn_w_down,
         v_mix_norm_g, v_ffn_norm_g, v_conv_w_in, v_conv_b_in, v_conv_w_dw, v_conv_b_dw, v_conv_ln_g, v_conv_ln_b, v_conv_w_out, v_gdn_w_in, v_gdn_conv_w, v_gdn_a_log, v_gdn_dt_bias, v_gdn_o_norm_g, v_gdn_w_out, v_fox_w_in, v_fox_b_f, v_fox_q_norm_g, v_fox_k_norm_g, v_fox_w_out, v_ffn_w_up, v_ffn_w_dw, v_ffn_w_down)))

    me = 4 * lax.axis_index("x") + 2 * lax.axis_index("y") + lax.axis_index("c")
    mine = me.astype(jnp.int32).reshape(1)

    gathers, token = {}, jnp.zeros((1, 1), F32)
    for i in range(DEPTH):
        for part in (0, 1):
            mats, smalls = _part_entries(i, part)
            sent = [given[n][j].astype(BF16) for n, j, _ in mats]
            if smalls:
                sent.append(_pack_rows([given[n][j] for n, j, _ in smalls], 0))
            gathers[i, part], tok = _exchange_start(f"gather{i}{'mf'[part]}_start", sent, [False] * len(sent))
            token = token + tok[0:1, 0:1]

    def params_of(i, part, x_in):
        mats, smalls = _part_entries(i, part)
        after = token if (i, part) == (0, 0) else x_in
        lands, owns = _exchange_wait(f"gather{i}{'mf'[part]}_wait", gathers[i, part], after, me)
        whole, relaid = {}, {}
        for (n, _, k), land, own in zip(mats, lands, owns):
            if n in ROW_SHARDED:
                whole[n] = _unshard(n, _fill_own(land, own, me))
            else:
                relaid[k] = _unshard_cols(f"l{i}_{n}_unshard", n, land, own, mine)
        if smalls:
            shapes = [given[n].shape[1:] for n, _, _ in smalls]
            for (n, _, _), g in zip(smalls, _unpack_rows(_fill_own(lands[-1], owns[-1], me), shapes, 1)):
                whole[n] = _unshard(n, g)
        return {**_part_dict(i, part, whole, given), **relaid}

    grads, exchanges = [[None, None] for _ in range(DEPTH)], {}

    def on_grads(i, part, g):
        grads[i][part] = g
        mats, smalls = _part_entries(i, part)
        sent = [_reshard(n, g[k]) if n in ROW_SHARDED else
                _reshard_cols(f"l{i}_{n}_reshard", n, g[k], given[n].shape[-1]) for n, _, k in mats]
        if smalls:
            sent.append(_pack_rows([_reshard(n, _from_grad(n, g[k])) for n, _, k in smalls], 1))
        last = (i, part) == (0, 0)
        if last:
            sent.append(_pack_rep(_replicated_grads(grads)))
        exchanges[i, part], tok = _exchange_start(f"grads{i}{'mf'[part]}_start", sent,
                                                  [True] * (len(sent) - last) + [False] * last)
        tokens.append(tok)
        return tok

    tokens = []
    sq, dx = _local_step(x[0], loss_target[0], params_of, on_grads)
    loss = (0.5 / D) * lax.psum(sq[0, 0], ("x", "y", "c"))

    pieces = {n: [] for n in SHARDED}
    for i in range(DEPTH):
        for part in (0, 1):
            mats, smalls = _part_entries(i, part)
            lands, owns = _exchange_wait(f"grads{i}{'mf'[part]}_wait", exchanges[i, part], tokens[-1], me)
            if (i, part) == (0, 0):
                rep_piece = (lands[-1], owns[-1])
            for (n, _, _), land, own in zip(mats, lands, owns):
                pieces[n].append((land, own))
            if smalls:
                shapes = [given[n].shape[1:] for n, _, _ in smalls]
                for (n, _, _), land, own in zip(smalls, _unpack_rows(lands[len(mats)], shapes, 1),
                                                _unpack_rows(owns[len(mats)], shapes, 0)):
                    pieces[n].append((land, own))
    new = {}
    for n in SHARDED:
        lands = [l.reshape((N_DEV, -1, l.shape[-1])) for l, _ in pieces[n]]
        owns = [o.reshape((-1, o.shape[-1])) for _, o in pieces[n]]
        if owns[0].shape[0] % 8:
            lands = [jnp.stack(lands, axis=1).reshape((N_DEV, -1, lands[0].shape[-1]))]
            owns = [jnp.stack(owns).reshape((-1, owns[0].shape[-1]))]
        outs = _adam("adam_" + n, lands, owns, mine, _view2d(given[n]), _view2d(given["m_" + n]),
                     _view2d(given["v_" + n]))
        new[n] = [o.reshape(given[n].shape) for o in outs]
    packed = [_pack_rep({n: given[pre + n] for n in REPLICATED}) for pre in ("", "m_", "v_")]
    outs = _adam("adam_replicated", [rep_piece[0]], [rep_piece[1]], mine, *packed)
    unpacked = [_unpack_rep(o, {n: given[n].shape for n in REPLICATED}) for o in outs]
    for n in REPLICATED:
        new[n] = [u[n] for u in unpacked]
    return (loss, dx[None], *[new[n][0] for n in WEIGHTS], *[new[n][1] for n in WEIGHTS],
            *[new[n][2] for n in WEIGHTS], *[new[n][3] for n in WEIGHTS])
```

```python
import functools

import jax
import jax.numpy as jnp
import numpy as np
from jax import lax
from jax.experimental import pallas as pl
from jax.experimental.pallas import tpu as pltpu

F32 = jnp.float32
BF16 = jnp.bfloat16
HIGHEST = lax.Precision.HIGHEST
HIGH = lax.Precision.HIGH

N_DEV = 8
LANE = 128
EPS = 1e-6
DEPTH = 4
N_MIXERS = 3
HEADS = 8
DH = 128
D = HEADS * DH
D_FF = 2816
CONF_K, GDN_K, FFN_K = 31, 4, 3
GDN_CHUNK = 64
GDN_PAD = 4224
FOX_PAD = 3200
FOX_V_BLOCK = (2 * D + LANE) // DH
ADAM_LR, ADAM_B1, ADAM_B2, ADAM_EPS, ADAM_WD, ADAM_STEP = 0.001, 0.9, 0.999, 1e-08, 0.01, 10
VMEM_LIMIT = 56 * 1024 * 1024

WEIGHTS = ['mix_norm_g', 'ffn_norm_g', 'conv_w_in', 'conv_b_in', 'conv_w_dw', 'conv_b_dw', 'conv_ln_g', 'conv_ln_b',
           'conv_w_out', 'gdn_w_in', 'gdn_conv_w', 'gdn_a_log', 'gdn_dt_bias', 'gdn_o_norm_g', 'gdn_w_out', 'fox_w_in',
           'fox_b_f', 'fox_q_norm_g', 'fox_k_norm_g', 'fox_w_out', 'ffn_w_up', 'ffn_w_dw', 'ffn_w_down']
REPLICATED = ['mix_norm_g', 'ffn_norm_g', 'gdn_a_log', 'gdn_dt_bias', 'gdn_o_norm_g', 'fox_b_f', 'fox_q_norm_g',
              'fox_k_norm_g']
ROW_SHARDED = ['conv_w_out', 'gdn_w_out', 'fox_w_out', 'ffn_w_down']
MATRICES = ['conv_w_in', 'conv_w_out', 'gdn_w_in', 'gdn_w_out', 'fox_w_in', 'fox_w_out', 'ffn_w_up', 'ffn_w_down']
SHARDED = [n for n in WEIGHTS if n not in REPLICATED]


def _params(*sem):
    return pltpu.CompilerParams(dimension_semantics=sem, vmem_limit_bytes=VMEM_LIMIT)


def _tile(n, cap):
    if n <= cap:
        return n
    d = (cap // LANE) * LANE
    while d >= LANE:
        if n % d == 0:
            return d
        d -= LANE
    raise ValueError(f"no lane-aligned tile of {n} under {cap}")


def _raw_dot(a, b, ca, cb, hp):
    batch = ((0,), (0,)) if a.ndim == 3 else ((), ())
    dn = (((ca,), (cb,)), batch)
    if hp:
        return lax.dot_general(a.astype(F32), b.astype(F32), dn, precision=HIGH, preferred_element_type=F32)
    return lax.dot_general(a.astype(BF16), b.astype(BF16), dn, preferred_element_type=F32)


def _raw_nn(a, b, hp=False):
    return _raw_dot(a, b, a.ndim - 1, b.ndim - 2, hp)


def _raw_nt(a, b, hp=False):
    return _raw_dot(a, b, a.ndim - 1, b.ndim - 1, hp)


def _raw_tn(a, b, hp=False):
    return _raw_dot(a, b, a.ndim - 2, b.ndim - 2, hp)


@functools.partial(jax.custom_vjp, nondiff_argnums=(2,))
def _nn(a, b, hp):
    return _raw_nn(a, b, hp)


def _nn_fwd(a, b, hp):
    return _raw_nn(a, b, hp), (a, b)


def _nn_bwd(hp, res, g):
    a, b = res
    return _raw_nt(g, b, hp), _raw_tn(a, g, hp)


_nn.defvjp(_nn_fwd, _nn_bwd)


@functools.partial(jax.custom_vjp, nondiff_argnums=(2,))
def _nt(a, b, hp):
    return _raw_nt(a, b, hp)


def _nt_fwd(a, b, hp):
    return _raw_nt(a, b, hp), (a, b)


def _nt_bwd(hp, res, g):
    a, b = res
    return _raw_nn(g, b, hp), _raw_tn(g, a, hp)


_nt.defvjp(_nt_fwd, _nt_bwd)


CONV_TAIL = 32


def _zero_tail(x):
    return jnp.concatenate([x, jnp.zeros((CONV_TAIL, x.shape[1]), x.dtype)], axis=0)


def _row(w, k):
    r = lax.broadcasted_iota(jnp.int32, w.shape, 0)
    return jnp.sum(jnp.where(r == k, w, 0.0), axis=0, keepdims=True)


@jax.custom_vjp
def _dwconv(x, w):
    t, taps = x.shape[0], w.shape[0]
    xe = _zero_tail(x)
    y = _row(w, taps - 1) * xe
    for k in range(taps - 1):
        y = y + _row(w, k) * pltpu.roll(xe, taps - 1 - k, axis=0)
    return y[:t]


def _dwconv_fwd(x, w):
    return _dwconv(x, w), (x, w)


def _dwconv_bwd(res, dy):
    x, w = res
    t, taps = x.shape[0], w.shape[0]
    xe, dye = _zero_tail(x), _zero_tail(dy)
    r = lax.broadcasted_iota(jnp.int32, w.shape, 0)
    dx = _row(w, taps - 1) * dye
    dw = jnp.where(r == taps - 1, jnp.sum(dye * xe, axis=0, keepdims=True), 0.0)
    for k in range(taps - 1):
        up = pltpu.roll(dye, t + CONV_TAIL - (taps - 1 - k), axis=0)
        dx = dx + _row(w, k) * up
        dw = dw + jnp.where(r == k, jnp.sum(up * xe, axis=0, keepdims=True), 0.0)
    return dx[:t], dw


_dwconv.defvjp(_dwconv_fwd, _dwconv_bwd)


def _sigmoid(x):
    return 1.0 / (1.0 + jnp.exp(-x))


def _silu(x):
    return x * _sigmoid(x)


def _softplus(x):
    return jnp.maximum(x, 0.0) + jnp.log(1.0 + jnp.exp(-jnp.abs(x)))


def _head_scale(x, fn):
    tm = x.shape[0]
    x3 = x.reshape(tm, HEADS, DH)
    return (x3 * fn(jnp.sum(x3 * x3, axis=-1, keepdims=True))).reshape(tm, HEADS * DH)


def _tile_lanes(g):
    return jnp.concatenate([g] * HEADS, axis=1)


def _expand_heads(v, first):
    lane = lax.broadcasted_iota(jnp.int32, (LANE, HEADS * DH), 0)
    col = lax.broadcasted_iota(jnp.int32, (LANE, HEADS * DH), 1)
    sel = (lane == col // DH + first).astype(F32)
    return _nn(v, sel, True)


def _f_rms(x, g):
    return (x * lax.rsqrt(jnp.mean(x * x, axis=-1, keepdims=True) + EPS) * g,)


def _f_rms_res(x, g):
    return (_f_rms(x, g)[0], x)


def _f_conf_glu_conv(u, b, w):
    c = u.shape[1] // 2
    return (_dwconv((u[:, :c] + b[:, :c]) * _sigmoid(u[:, c:] + b[:, c:]), w),)


def _f_conf_ln_silu(cv, b_dw, ln_g, ln_b):
    u = cv + b_dw
    xc = u - jnp.mean(u, axis=-1, keepdims=True)
    y = xc * lax.rsqrt(jnp.mean(xc * xc, axis=-1, keepdims=True) + EPS) * ln_g + ln_b
    return (_silu(y),)


def _f_conv_silu(u, w):
    return (_silu(_dwconv(u, w)),)


def _f_gdn_gates(qk, ab, a_log, dt_bias):
    qn = _head_scale(qk[:, :D], lambda ss: lax.rsqrt(ss + EPS)) * (DH ** -0.5)
    kn = _head_scale(qk[:, D:], lambda ss: lax.rsqrt(ss + EPS))
    g = -jnp.exp(a_log) * _softplus(ab + dt_bias)
    beta = _sigmoid(ab)
    return qn, kn, _expand_heads(g, 0), _expand_heads(beta, HEADS)


def _f_gdn_prescan(q, k, v, gb, bb):
    c = GDN_CHUNK
    n = q.shape[0] // c
    r3 = lambda t: t.reshape(n, c, DH)
    q3, k3, v3, g3, b3 = r3(q), r3(k), r3(v), r3(gb), r3(bb)
    ii = lax.broadcasted_iota(jnp.int32, (n, c, c), 1)
    jj = lax.broadcasted_iota(jnp.int32, (n, c, c), 2)
    lower, strict = ii >= jj, ii > jj
    gcb = _nn(lower.astype(F32), g3, True)
    gi = gcb[:, :, :c]
    gj = jnp.swapaxes(gi, 1, 2)
    decay = jnp.where(lower, jnp.exp(jnp.where(lower, gi - gj, 0.0)), 0.0)
    kb, vb = k3 * b3, v3 * b3
    a_mat = jnp.where(strict, _nt(kb, k3, False) * decay, 0.0)
    p = -a_mat
    t_mat = (ii == jj).astype(F32) + p
    for _ in range(5):
        p = _nn(p, p, True)
        t_mat = t_mat + _nn(t_mat, p, True)
    eg = jnp.exp(gcb)
    u = _nn(t_mat, vb, False)
    w = _nn(t_mat, kb * eg, False)
    qk = jnp.where(lower, _nt(q3, k3, False) * decay, 0.0)
    qg = q3 * eg
    g_last = jnp.sum(g3, axis=1, keepdims=True)
    kd = k3 * jnp.exp(g_last - gcb)
    r2 = lambda t: t.reshape(n * c, DH)
    return r2(u), r2(w), r2(qg), qk, r2(kd), jnp.exp(g_last)


def _f_gdn_post(o, z, o_g):
    on = _head_scale(o, lambda ss: lax.rsqrt(ss / DH + EPS)) * _tile_lanes(o_g)
    return (on * _silu(z),)


def _f_fox_pre(qkf, q_g, k_g, b_f):
    qn = _head_scale(qkf[:, :D], lambda ss: lax.rsqrt(ss / DH + EPS)) * _tile_lanes(q_g)
    kn = _head_scale(qkf[:, D:2 * D], lambda ss: lax.rsqrt(ss / DH + EPS)) * _tile_lanes(k_g)
    return qn, kn, -_softplus(-(qkf[:, 2 * D:] + b_f))


def _f_ffn_mid(u, w):
    c = u.shape[1] // 2
    return (_silu(_dwconv(u[:, :c], w[:, :c])) * _dwconv(u[:, c:], w[:, c:]),)


def _seg_fwd(name, f, grid, ins, in_specs, out_shapes, out_specs):
    n_in = len(ins)

    def body(*refs):
        outs = f(*[r[...].astype(F32) for r in refs[:n_in]])
        for r, o in zip(refs[n_in:], outs):
            r[...] = o.astype(r.dtype)

    return pl.pallas_call(body, grid=grid, in_specs=in_specs, out_specs=out_specs, out_shape=out_shapes, name=name,
                          compiler_params=_params(*(["parallel"] * len(grid))))(*ins)


def _seg_bwd(name, f, grid, ins, in_specs, douts, dout_specs, want, into=None):
    into = into or {}
    n_in, n_dy = len(ins), len(douts)
    diff = [i for i, w in enumerate(want) if w is not None]
    kept = [i for i in diff if i in into]
    out_shapes = [jax.ShapeDtypeStruct(ins[i].shape, into[i].dtype if i in into else F32 if want[i] == "acc" else want[i])
                  for i in diff]
    out_specs = [in_specs[i] for i in diff]
    acc_axis = len(grid) - 1

    def body(*refs):
        vals = [r[...].astype(F32) for r in refs[:n_in]]
        dys = [r[...].astype(F32) for r in refs[n_in:n_in + n_dy]]
        out_refs = refs[n_in + n_dy + len(kept):]

        def g(*dv):
            full = list(vals)
            for i, v in zip(diff, dv):
                full[i] = v
            return f(*full)

        _, vjp = jax.vjp(g, *[vals[i] for i in diff])
        grads = vjp(tuple(dys))
        first = pl.program_id(acc_axis) == 0
        for i, r, gr in zip(diff, out_refs, grads):
            if want[i] == "acc":
                @pl.when(first)
                def _(r=r, gr=gr):
                    r[...] = gr

                @pl.when(jnp.logical_not(first))
                def _(r=r, gr=gr):
                    r[...] += gr
            else:
                r[...] = gr.astype(r.dtype)

    sem = ["parallel"] * (len(grid) - 1) + ["arbitrary"]
    untouched = [pl.BlockSpec(memory_space=pl.ANY)] * len(kept)
    aliases = {n_in + n_dy + e: diff.index(i) for e, i in enumerate(kept)}
    return pl.pallas_call(body, grid=grid, in_specs=list(in_specs) + list(dout_specs) + untouched, out_specs=out_specs,
                          out_shape=out_shapes, input_output_aliases=aliases, name=name,
                          compiler_params=_params(*sem))(*ins, *douts, *[into[i] for i in kept])


def _rows(tm, width, col=0):
    return pl.BlockSpec((tm, width), lambda i, col=col: (i, col))


def _const(shape):
    return pl.BlockSpec(shape, lambda i: (0,) * len(shape))


def _cols(t, tc, off=0):
    return pl.BlockSpec((t, tc), lambda j, off=off: (0, j + off))


def _grid2(specs):
    return [pl.BlockSpec(s.block_shape, lambda j, i, f=s.index_map: f(j)) for s in specs]


def _mm(name, a, b, *, ta=False, tb=False, res=None, out_dtype=F32, after=None):
    k_dim, m = (a.shape[0], a.shape[1]) if ta else (a.shape[1], a.shape[0])
    n = b.shape[0] if tb else b.shape[1]
    tm, tn, tk = _tile(m, 1408), _tile(n, 1408), _tile(k_dim, 1408)
    nk = k_dim // tk
    grid = (m // tm, n // tn, nk)
    a_spec = pl.BlockSpec((tk, tm), lambda i, j, k: (k, i)) if ta else pl.BlockSpec((tm, tk), lambda i, j, k: (i, k))
    b_spec = pl.BlockSpec((tn, tk), lambda i, j, k: (j, k)) if tb else pl.BlockSpec((tk, tn), lambda i, j, k: (k, j))
    o_spec = pl.BlockSpec((tm, tn), lambda i, j, k: (i, j))
    dn = (((0 if ta else 1,), (1 if tb else 0,)), ((), ()))
    has_res = res is not None

    def body(*refs):
        a_ref, b_ref = refs[0], refs[1]
        res_ref = refs[2] if has_res else None
        o_ref = refs[n_in]
        p = lax.dot_general(a_ref[...].astype(BF16), b_ref[...].astype(BF16), dn, preferred_element_type=F32)

        def write(acc):
            if has_res:
                acc = acc + res_ref[...]
            if after is not None:
                acc = acc + refs[n_in - 1][0:1, 0:1]
            o_ref[...] = acc.astype(o_ref.dtype)

        if nk == 1:
            write(p)
        else:
            acc_ref = refs[-1]
            k = pl.program_id(2)

            @pl.when(k == 0)
            def _():
                acc_ref[...] = p

            @pl.when(k > 0)
            def _():
                acc_ref[...] += p

            @pl.when(k == nk - 1)
            def _():
                write(acc_ref[...])

    ins, specs = [a, b], [a_spec, b_spec]
    if has_res:
        ins.append(res)
        specs.append(o_spec)
    if after is not None:
        ins.append(after)
        specs.append(pl.BlockSpec(after.shape, lambda i, j, k: (0, 0)))
    n_in = len(ins)
    scratch = [pltpu.VMEM((tm, tn), F32)] if nk > 1 else []
    return pl.pallas_call(body, grid=grid, in_specs=specs, out_specs=o_spec, scratch_shapes=scratch,
                          out_shape=jax.ShapeDtypeStruct((m, n), out_dtype), name=name,
                          compiler_params=_params("parallel", "parallel", "arbitrary"))(*ins)


SCAN_HEADS = 2


def _head_specs(t):
    n = t // GDN_CHUNK
    col = pl.BlockSpec((t, SCAN_HEADS * DH), lambda h: (0, h))
    qk = pl.BlockSpec((SCAN_HEADS, n, GDN_CHUNK, GDN_CHUNK), lambda h: (h, 0, 0, 0))
    gl = pl.BlockSpec((SCAN_HEADS, n, 1, DH), lambda h: (h, 0, 0, 0))
    st = pl.BlockSpec((SCAN_HEADS, n, DH, DH), lambda h: (h, 0, 0, 0))
    return n, col, qk, gl, st


def _gdn_scan_fwd(name, u, w, qg, qk, kd, gl):
    t = u.shape[0]
    n, col, qk_spec, gl_spec, st_spec = _head_specs(t)
    c = GDN_CHUNK

    def body(u_ref, w_ref, qg_ref, qk_ref, kd_ref, gl_ref, o_ref, s_ref):
        def step(i, states):
            rows = pl.ds(pl.multiple_of(i * c, c), c)
            out = []
            for h, s in enumerate(states):
                cols = slice(h * DH, (h + 1) * DH)
                s_ref[h, i] = s
                vn = u_ref[rows, cols] - _raw_nn(w_ref[rows, cols], s)
                o_ref[rows, cols] = _raw_nn(qg_ref[rows, cols], s) + _raw_nn(qk_ref[h, i], vn)
                out.append(s * gl_ref[h, i] + _raw_tn(kd_ref[rows, cols], vn))
            return tuple(out)

        lax.fori_loop(0, n, step, (jnp.zeros((DH, DH), F32),) * SCAN_HEADS)

    return pl.pallas_call(
        body, grid=(HEADS // SCAN_HEADS,), in_specs=[col, col, col, qk_spec, col, gl_spec], out_specs=[col, st_spec],
        out_shape=[jax.ShapeDtypeStruct((t, D), F32), jax.ShapeDtypeStruct((HEADS, n, DH, DH), F32)], name=name,
        compiler_params=_params("parallel"))(u, w, qg, qk, kd, gl)


def _gdn_scan_bwd(name, u, w, qg, qk, kd, gl, states, do):
    t = u.shape[0]
    n, col, qk_spec, gl_spec, st_spec = _head_specs(t)
    c = GDN_CHUNK

    def body(u_ref, w_ref, qg_ref, qk_ref, kd_ref, gl_ref, s_ref, do_ref,
             du_ref, dw_ref, dqg_ref, dqk_ref, dkd_ref, dgl_ref):
        def step(r, dstates):
            i = n - 1 - r
            rows = pl.ds(pl.multiple_of(i * c, c), c)
            out = []
            for h, ds in enumerate(dstates):
                cols = slice(h * DH, (h + 1) * DH)
                s, do_c, w_c = s_ref[h, i], do_ref[rows, cols], w_ref[rows, cols]
                vn = u_ref[rows, cols] - _raw_nn(w_c, s)
                dvn = _raw_tn(qk_ref[h, i], do_c) + _raw_nn(kd_ref[rows, cols], ds)
                du_ref[rows, cols] = dvn
                dw_ref[rows, cols] = -_raw_nt(dvn, s)
                dqg_ref[rows, cols] = _raw_nt(do_c, s)
                dqk_ref[h, i] = _raw_nt(do_c, vn)
                dkd_ref[rows, cols] = _raw_nt(vn, ds)
                dgl_ref[h, i] = jnp.sum(ds * s, axis=0, keepdims=True)
                out.append(_raw_tn(qg_ref[rows, cols], do_c) + ds * gl_ref[h, i] - _raw_tn(w_c, dvn))
            return tuple(out)

        lax.fori_loop(0, n, step, (jnp.zeros((DH, DH), F32),) * SCAN_HEADS)

    big = jax.ShapeDtypeStruct((t, D), F32)
    return pl.pallas_call(
        body, grid=(HEADS // SCAN_HEADS,), in_specs=[col, col, col, qk_spec, col, gl_spec, st_spec, col],
        out_specs=[col, col, col, qk_spec, col, gl_spec],
        out_shape=[big, big, big, jax.ShapeDtypeStruct(qk.shape, F32), big, jax.ShapeDtypeStruct(gl.shape, F32)],
        name=name, compiler_params=_params("parallel"))(u, w, qg, qk, kd, gl, states, do)


def _cumsum_rows(name, parts, reverse):
    t = parts[0].shape[0]
    blk = min(t, 256)
    nb = t // blk
    n_in = len(parts)

    def body(*refs):
        o_ref = refs[n_in]
        ii = lax.broadcasted_iota(jnp.int32, (blk, blk), 0)
        jj = lax.broadcasted_iota(jnp.int32, (blk, blk), 1)
        tri = ((ii <= jj) if reverse else (ii >= jj)).astype(F32)
        carry = jnp.zeros((1, LANE), F32)
        for b in (range(nb - 1, -1, -1) if reverse else range(nb)):
            rows = pl.ds(b * blk, blk)
            x = refs[0][rows, :]
            for r in refs[1:n_in]:
                x = x + r[rows, :]
            o_ref[rows, :] = lax.dot_general(tri, x, (((1,), (0,)), ((), ())), precision=HIGHEST,
                                             preferred_element_type=F32) + carry
            carry = carry + jnp.sum(x, axis=0, keepdims=True)

    return pl.pallas_call(body, out_shape=jax.ShapeDtypeStruct((t, LANE), F32), name=name,
                          compiler_params=_params())(*parts)


def _fox_blocks(t):
    blk = 512 if t % 512 == 0 and t >= 2048 else 256 if t % 256 == 0 and t >= 1024 else 128
    return blk, t // blk


def _fox_logits(q, k, cq, ck, diagonal):
    s = _raw_nt(q, k) * (DH ** -0.5) + cq - ck
    if not diagonal:
        return s
    rows = lax.broadcasted_iota(jnp.int32, s.shape, 0)
    cols = lax.broadcasted_iota(jnp.int32, s.shape, 1)
    return jnp.where(cols <= rows, s, -jnp.inf)


def _fox_fwd(name, qn, kn, proj, c_col, c_row):
    t = qn.shape[0]
    blk, nb = _fox_blocks(t)
    voff = FOX_V_BLOCK

    def body(q_ref, k_ref, v_ref, cc_ref, cr_ref, o_ref, lse_ref):
        i = pl.program_id(1)
        q, cq = q_ref[...], cc_ref[...]

        def step(j, carry, diagonal=False):
            m, l, acc = carry
            rows = pl.ds(pl.multiple_of(j * blk, blk), blk)
            s = _fox_logits(q, k_ref[rows, :], cq, cr_ref[j], diagonal)
            m_new = jnp.maximum(m, jnp.max(s, axis=1, keepdims=True))
            p = jnp.exp(s - m_new)
            alpha = jnp.exp(m - m_new)
            return m_new, alpha * l + jnp.sum(p, axis=1, keepdims=True), alpha * acc + _raw_nn(p, v_ref[rows, :])

        init = (jnp.full((blk, 1), -1e30, F32), jnp.zeros((blk, 1), F32), jnp.zeros((blk, DH), F32))
        m, l, acc = step(i, lax.fori_loop(0, i, step, init), True)
        o_ref[...] = acc / l
        lse_ref[...] = m + jnp.log(l)

    tile = pl.BlockSpec((blk, DH), lambda h, i: (i, h))
    colv = pl.BlockSpec((None, blk, 1), lambda h, i: (h, i, 0))
    return pl.pallas_call(
        body, grid=(HEADS, nb),
        in_specs=[tile, pl.BlockSpec((t, DH), lambda h, i: (0, h)), pl.BlockSpec((t, DH), lambda h, i: (0, voff + h)),
                  colv, pl.BlockSpec((None, nb, 1, blk), lambda h, i: (h, 0, 0, 0))],
        out_specs=[tile, colv],
        out_shape=[jax.ShapeDtypeStruct((t, D), F32), jax.ShapeDtypeStruct((HEADS, t, 1), F32)], name=name,
        compiler_params=_params("parallel", "parallel"))(qn, kn, proj, c_col, c_row)


def _fox_dq(name, qn, kn, proj, c_col, c_row, o, lse, do):
    t = qn.shape[0]
    blk, nb = _fox_blocks(t)
    voff = FOX_V_BLOCK

    def body(q_ref, k_ref, v_ref, cc_ref, cr_ref, o_ref, lse_ref, do_ref, dq_ref, dcc_ref, dl_ref):
        i = pl.program_id(1)
        q, cq, do_b, lse_b = q_ref[...], cc_ref[...], do_ref[...], lse_ref[...]
        delta = jnp.sum(do_b * o_ref[...], axis=1, keepdims=True)

        def step(j, carry, diagonal=False):
            dq, dcc = carry
            rows = pl.ds(pl.multiple_of(j * blk, blk), blk)
            k = k_ref[rows, :]
            p = jnp.exp(_fox_logits(q, k, cq, cr_ref[j], diagonal) - lse_b)
            ds = p * (_raw_nt(do_b, v_ref[rows, :]) - delta)
            return dq + _raw_nn(ds, k), dcc + jnp.sum(ds, axis=1, keepdims=True)

        init = (jnp.zeros((blk, DH), F32), jnp.zeros((blk, 1), F32))
        dq, dcc = step(i, lax.fori_loop(0, i, step, init), True)
        dq_ref[...] = dq * (DH ** -0.5)
        dcc_ref[...] = dcc
        dl_ref[...] = delta

    tile = pl.BlockSpec((blk, DH), lambda h, i: (i, h))
    colv = pl.BlockSpec((None, blk, 1), lambda h, i: (h, i, 0))
    vec = jax.ShapeDtypeStruct((HEADS, t, 1), F32)
    return pl.pallas_call(
        body, grid=(HEADS, nb),
        in_specs=[tile, pl.BlockSpec((t, DH), lambda h, i: (0, h)), pl.BlockSpec((t, DH), lambda h, i: (0, voff + h)),
                  colv, pl.BlockSpec((None, nb, 1, blk), lambda h, i: (h, 0, 0, 0)), tile, colv, tile],
        out_specs=[tile, colv, colv],
        out_shape=[jax.ShapeDtypeStruct((t, D), F32), vec, vec], name=name,
        compiler_params=_params("parallel", "parallel"))(qn, kn, proj, c_col, c_row, o, lse, do)


def _fox_dkv(name, qn, kn, proj, c_col, c_row, lse, delta, do):
    t = qn.shape[0]
    blk, nb = _fox_blocks(t)
    voff = FOX_V_BLOCK

    def body(q_ref, k_ref, v_ref, cc_ref, cr_ref, lse_ref, dl_ref, do_ref, dk_ref, dv_ref, dcr_ref):
        j = pl.program_id(1)
        k, v, ck = k_ref[...], v_ref[...], cr_ref[...]

        def step(i, carry, diagonal=False):
            dk, dv, dcr = carry
            rows = pl.ds(pl.multiple_of(i * blk, blk), blk)
            q, do_b = q_ref[rows, :], do_ref[rows, :]
            p = jnp.exp(_fox_logits(q, k, cc_ref[rows, :], ck, diagonal) - lse_ref[rows, :])
            ds = p * (_raw_nt(do_b, v) - dl_ref[rows, :])
            return dk + _raw_tn(ds, q), dv + _raw_tn(p, do_b), dcr - jnp.sum(ds, axis=0, keepdims=True)

        init = (jnp.zeros((blk, DH), F32), jnp.zeros((blk, DH), F32), jnp.zeros((1, blk), F32))
        dk, dv, dcr = lax.fori_loop(j + 1, nb, step, step(j, init, True))
        dk_ref[...] = dk * (DH ** -0.5)
        dv_ref[...] = dv.astype(dv_ref.dtype)
        dcr_ref[...] = dcr

    full = pl.BlockSpec((t, DH), lambda h, j: (0, h))
    colf = pl.BlockSpec((None, t, 1), lambda h, j: (h, 0, 0))
    tile = pl.BlockSpec((blk, DH), lambda h, j: (j, h))
    rowv = pl.BlockSpec((None, None, 1, blk), lambda h, j: (h, j, 0, 0))
    tile_v = pl.BlockSpec((blk, DH), lambda h, j: (j, voff + h))
    big = jax.ShapeDtypeStruct((t, D), F32)
    return pl.pallas_call(
        body, grid=(HEADS, nb),
        in_specs=[full, tile, tile_v, colf, rowv, colf, colf, full],
        out_specs=[tile, tile_v, rowv],
        out_shape=[big, jax.ShapeDtypeStruct(proj.shape, BF16), jax.ShapeDtypeStruct((HEADS, nb, 1, blk), F32)], name=name,
        compiler_params=_params("parallel", "parallel"))(qn, kn, proj, c_col, c_row, lse, delta, do)


def _loss_head(y, target):
    t = y.shape[0]
    tm = min(t, 512)

    def body(y_ref, t_ref, dy_ref, sum_ref):
        err = y_ref[...] - t_ref[...]
        dy_ref[...] = err * (1.0 / D)
        part = jnp.sum(jnp.sum(err * err, axis=1, keepdims=True), axis=0, keepdims=True)

        @pl.when(pl.program_id(0) == 0)
        def _():
            sum_ref[...] = jnp.zeros_like(sum_ref)

        sum_ref[...] += jnp.broadcast_to(part, sum_ref.shape)

    return pl.pallas_call(
        body, grid=(t // tm,), in_specs=[_rows(tm, D), _rows(tm, D)], out_specs=[_rows(tm, D), _const((1, LANE))],
        out_shape=[jax.ShapeDtypeStruct((t, D), F32), jax.ShapeDtypeStruct((1, LANE), F32)], name="loss_head",
        compiler_params=_params("arbitrary"))(y, target)


ADAM_BLOCK_BYTES = 3 * 1024 * 1024


def _adam(name, lands, owns, mine, w, m, v):
    layers = len(lands)
    r, c = owns[0].shape
    block_bytes = lambda rows: N_DEV * rows * c * lands[0].dtype.itemsize
    tr = r
    for cand in (512, 256, 128, 64, 32, 16):
        if block_bytes(tr) > ADAM_BLOCK_BYTES and r % cand == 0:
            tr = cand
    nr = r // tr

    def body(*refs):
        me_ref, land_refs, own_refs = refs[0], refs[1:1 + layers], refs[1 + layers:1 + 2 * layers]
        w_ref, m_ref, v_ref, g_ref, d_ref, nm_ref, nv_ref = refs[1 + 2 * layers:]
        layer = pl.program_id(0)
        for l in range(layers):
            @pl.when(layer == l)
            def _(l=l):
                g = jnp.zeros((tr, c), F32)
                for s in range(N_DEV):
                    g = g + jnp.where(me_ref[0] == s, own_refs[l][...], land_refs[l][s]).astype(F32)
                nm = ADAM_B1 * m_ref[...] + (1.0 - ADAM_B1) * g
                nv = ADAM_B2 * v_ref[...] + (1.0 - ADAM_B2) * (g * g)
                m_hat = nm / (1.0 - ADAM_B1 ** ADAM_STEP)
                v_hat = nv / (1.0 - ADAM_B2 ** ADAM_STEP)
                g_ref[...] = g
                d_ref[...] = -ADAM_LR * (m_hat / (jnp.sqrt(v_hat) + ADAM_EPS) + ADAM_WD * w_ref[...])
                nm_ref[...] = nm
                nv_ref[...] = nv

    at = lambda l: (lambda layer, i: jnp.where(layer == l, i, 0))
    land_specs = [pl.BlockSpec((N_DEV, tr, c), lambda layer, i, f=at(l): (0, f(layer, i), 0)) for l in range(layers)]
    own_specs = [pl.BlockSpec((tr, c), lambda layer, i, f=at(l): (f(layer, i), 0)) for l in range(layers)]
    blk = pl.BlockSpec((tr, c), lambda layer, i: (layer * nr + i, 0))
    out = jax.ShapeDtypeStruct(w.shape, F32)
    return pl.pallas_call(
        body, grid=(layers, nr),
        in_specs=[pl.BlockSpec(memory_space=pltpu.SMEM)] + land_specs + own_specs + [blk, blk, blk],
        out_specs=[blk] * 4, out_shape=[out] * 4, name=name,
        compiler_params=_params("arbitrary", "arbitrary"))(mine, *lands, *owns, w, m, v)


_HBM = pl.BlockSpec(memory_space=pltpu.HBM)
_SEM = pl.BlockSpec(memory_space=pltpu.SEMAPHORE)
_EFFECT = pltpu.SideEffectType.DATAFLOW_SIDE_EFFECTING


def _each_peer(x, y, c):
    flip = lambda v, bit: 1 - v if bit else v
    for p in range(1, N_DEV):
        px, py, pc = flip(x, p & 4), flip(y, p & 2), flip(c, p & 1)
        yield p, (px, py, pc), 4 * px + 2 * py + pc


def _sem(a, p):
    return a * (N_DEV - 1) + p - 1


def _exchange_start(name, arrays, scatter):
    n = len(arrays)
    lands = [lax.empty((N_DEV,) + (a.shape[1:] if sc else a.shape), a.dtype) for a, sc in zip(arrays, scatter)]

    def body(*refs):
        in_refs, land_refs = refs[:n], refs[n:2 * n]
        send_sems, recv_sems, token = refs[2 * n], refs[2 * n + 1], refs[-1]
        x, y, c = lax.axis_index("x"), lax.axis_index("y"), lax.axis_index("c")
        me = 4 * x + 2 * y + c
        for p, coords, peer in _each_peer(x, y, c):
            for a in range(n):
                pltpu.make_async_remote_copy(
                    src_ref=in_refs[a].at[peer] if scatter[a] else in_refs[a], dst_ref=land_refs[a].at[me],
                    send_sem=send_sems.at[_sem(a, p)], recv_sem=recv_sems.at[_sem(a, p)], device_id=coords,
                    device_id_type=pl.DeviceIdType.MESH).start()
        token[...] = jnp.zeros_like(token)

    sems = pltpu.SemaphoreType.DMA((n * (N_DEV - 1),))
    hbm = lambda a: pltpu.HBM(a.shape, a.dtype)
    out = pl.pallas_call(
        body, name=name,
        out_shape=(sems, sems, *[hbm(a) for a in arrays], *[hbm(l) for l in lands],
                   jax.ShapeDtypeStruct((8, LANE), F32)),
        in_specs=[_HBM] * (2 * n), out_specs=(_SEM, _SEM, *[_HBM] * (2 * n), pl.BlockSpec(memory_space=pltpu.VMEM)),
        input_output_aliases={i: 2 + i for i in range(2 * n)},
        compiler_params=pltpu.CompilerParams(has_side_effects=_EFFECT),
    )(*[pltpu.with_memory_space_constraint(a, pltpu.HBM) for a in arrays],
      *[pltpu.with_memory_space_constraint(l, pltpu.HBM) for l in lands])
    return (out[0], out[1], list(out[2:2 + n]), list(out[2 + n:2 + 2 * n]), scatter), out[-1]


def _exchange_wait(name, started, after, me):
    send_sems, recv_sems, sent, lands, scatter = started
    n = len(sent)

    def body(*refs):
        in_refs, land_refs = refs[:n], refs[n:2 * n]
        send_sems, recv_sems = refs[2 * n], refs[2 * n + 1]
        x, y, c = lax.axis_index("x"), lax.axis_index("y"), lax.axis_index("c")
        for p, coords, peer in _each_peer(x, y, c):
            for a in range(n):
                cp = pltpu.make_async_remote_copy(
                    src_ref=in_refs[a].at[peer] if scatter[a] else in_refs[a], dst_ref=land_refs[a].at[peer],
                    send_sem=send_sems.at[_sem(a, p)], recv_sem=recv_sems.at[_sem(a, p)], device_id=coords,
                    device_id_type=pl.DeviceIdType.MESH)
                cp.wait_send()
                cp.wait_recv()

    hbm = lambda a: pltpu.HBM(a.shape, a.dtype)
    out = pl.pallas_call(
        body, name=name, out_shape=(*[hbm(a) for a in sent], *[hbm(l) for l in lands]),
        in_specs=[_HBM] * (2 * n) + [_SEM, _SEM, pl.BlockSpec(memory_space=pl.ANY)], out_specs=[_HBM] * (2 * n),
        input_output_aliases={i: i for i in range(2 * n)},
        compiler_params=pltpu.CompilerParams(has_side_effects=_EFFECT),
    )(*sent, *lands, send_sems, recv_sems, after)
    owns = [lax.dynamic_index_in_dim(out[a], me, 0, keepdims=False) if scatter[a] else out[a] for a in range(n)]
    return list(out[n:]), owns


def _fill_own(land, own, me):
    slot = lax.broadcasted_iota(jnp.int32, (N_DEV,) + (1,) * own.ndim, 0)
    return jnp.where(slot == me, own[None], land)


def _rms_fwd(name, x, g):
    t = x.shape[0]
    tm = min(t, 512)
    return _seg_fwd(name, _f_rms, (t // tm,), [x, g], [_rows(tm, D), _const((1, D))],
                    [jax.ShapeDtypeStruct((t, D), BF16)], [_rows(tm, D)])[0]


def _rms_bwd(name, x, g, dh, dres):
    t = x.shape[0]
    tm = min(t, 512)
    return _seg_bwd(name, _f_rms_res, (t // tm,), [x, g], [_rows(tm, D), _const((1, D))],
                    [dh, dres], [_rows(tm, D), _rows(tm, D)], [F32, "acc"])


FFN_TC = 256
CONF_TC = 128


def _ffn_specs(t):
    nf = D_FF // FFN_TC
    return nf, [_cols(t, 2 * FFN_TC), _cols(FFN_K, 2 * FFN_TC)], [_cols(t, FFN_TC)]


def _ffn_fwd(tag, x, p):
    t = x.shape[0]
    nf, in_specs, out_specs = _ffn_specs(t)
    h = _rms_fwd(tag + "_rms", x, p["g"])
    u0 = _mm(tag + "_up", h, p["w_up"])
    act = _seg_fwd(tag + "_mid", _f_ffn_mid, (nf,), [u0, p["w_dw"]], in_specs,
                   [jax.ShapeDtypeStruct((t, D_FF), BF16)], out_specs)[0]
    out = _mm(tag + "_down", act, p["w_down"], res=x)
    return out, (x, h, u0, act)


def _ffn_bwd(tag, saved, p, dout, after=None):
    x, h, u0, act = saved
    t = x.shape[0]
    nf, in_specs, out_specs = _ffn_specs(t)
    g = {"w_down": _mm(tag + "_dwdown", act, dout, ta=True, out_dtype=BF16)}
    dact = _mm(tag + "_dact", dout, p["w_down"], tb=True, out_dtype=BF16, after=after)
    du0, g["w_dw"] = _seg_bwd(tag + "_dmid", _f_ffn_mid, (nf, 1), [u0, p["w_dw"]], _grid2(in_specs), [dact],
                              _grid2(out_specs), [BF16, "acc"])
    g["w_up"] = _mm(tag + "_dwup", h, du0, ta=True, out_dtype=BF16)
    dh = _mm(tag + "_dh", du0, p["w_up"], tb=True, out_dtype=BF16)
    dx, g["g"] = _rms_bwd(tag + "_drms", x, p["g"], dh, dout)
    return dx, g


def _conf_specs(t):
    tm = min(t, 512)
    nc = D // CONF_TC
    vec = _const((1, D))
    glu_in = [_cols(t, 2 * CONF_TC), _cols(1, 2 * CONF_TC), _cols(CONF_K, CONF_TC)]
    return tm, nc, glu_in, [_cols(t, CONF_TC)], [_rows(tm, D), vec, vec, vec]


def _conf_fwd(tag, x, p):
    t = x.shape[0]
    tm, nc, glu_in, glu_out, ln_in = _conf_specs(t)
    h = _rms_fwd(tag + "_rms", x, p["g"])
    u = _mm(tag + "_in", h, p["w_in"])
    cv = _seg_fwd(tag + "_gluconv", _f_conf_glu_conv, (nc,), [u, p["b_in"], p["w_dw"]], glu_in,
                  [jax.ShapeDtypeStruct((t, D), F32)], glu_out)[0]
    act = _seg_fwd(tag + "_lnsilu", _f_conf_ln_silu, (t // tm,), [cv, p["b_dw"], p["ln_g"], p["ln_b"]], ln_in,
                   [jax.ShapeDtypeStruct((t, D), BF16)], [_rows(tm, D)])[0]
    out = _mm(tag + "_out", act, p["w_out"], res=x)
    return out, (x, h, u, cv, act)


def _conf_bwd(tag, saved, p, dout, after=None):
    x, h, u, cv, act = saved
    t = x.shape[0]
    tm, nc, glu_in, glu_out, ln_in = _conf_specs(t)
    g = {"w_out": _mm(tag + "_dwout", act, dout, ta=True, out_dtype=BF16)}
    dact = _mm(tag + "_dact", dout, p["w_out"], tb=True, out_dtype=BF16, after=after)
    dcv, g["b_dw"], g["ln_g"], g["ln_b"] = _seg_bwd(
        tag + "_dlnsilu", _f_conf_ln_silu, (t // tm,), [cv, p["b_dw"], p["ln_g"], p["ln_b"]], ln_in, [dact],
        [_rows(tm, D)], [F32, "acc", "acc", "acc"])
    du, g["b_in"], g["w_dw"] = _seg_bwd(tag + "_dgluconv", _f_conf_glu_conv, (nc, 1), [u, p["b_in"], p["w_dw"]],
                                        _grid2(glu_in), [dcv], _grid2(glu_out), [BF16, "acc", "acc"])
    g["w_in"] = _mm(tag + "_dwin", h, du, ta=True, out_dtype=BF16)
    dh = _mm(tag + "_dh", du, p["w_in"], tb=True, out_dtype=BF16)
    dx, g["g"] = _rms_bwd(tag + "_drms", x, p["g"], dh, dout)
    return dx, g


def _gdn_specs(t):
    tc, tm, rows = 256, min(t, 256), min(t, 1024)
    nq = 3 * D // tc
    conv = ([_cols(t, tc), _cols(GDN_K, tc)], [_cols(t, tc)])
    gate_in = [_rows(tm, 2 * D, 0), _rows(tm, LANE, 4 * D // LANE), _const((1, LANE)), _const((1, LANE))]
    gate_out = [_rows(tm, D)] * 4
    head = pl.BlockSpec((rows, DH), lambda h, i: (i, h))
    headv = pl.BlockSpec((rows, DH), lambda h, i: (i, 2 * HEADS + h))
    nch = rows // GDN_CHUNK
    pre_in = [head, head, headv, head, head]
    pre_out = [head, head, head, pl.BlockSpec((None, nch, GDN_CHUNK, GDN_CHUNK), lambda h, i: (h, i, 0, 0)), head,
               pl.BlockSpec((None, nch, 1, DH), lambda h, i: (h, i, 0, 0))]
    post_in = [_rows(tm, D), _rows(tm, D, 3), _const((1, DH))]
    return tm, rows, nq, conv, gate_in, gate_out, pre_in, pre_out, post_in


def _gdn_fwd(tag, x, p):
    t = x.shape[0]
    tm, rows, nq, conv, gate_in, gate_out, pre_in, pre_out, post_in = _gdn_specs(t)
    n = t // GDN_CHUNK
    big = jax.ShapeDtypeStruct((t, D), F32)
    h = _rms_fwd(tag + "_rms", x, p["g"])
    proj = _mm(tag + "_in", h, p["w_in"])
    qkv = _seg_fwd(tag + "_conv", _f_conv_silu, (nq,), [proj, p["conv_w"]], conv[0],
                   [jax.ShapeDtypeStruct((t, 3 * D), F32)], conv[1])[0]
    gate_ins = [qkv, proj, p["a_log"], p["dt_bias"]]
    qn, kn, gb, bb = _seg_fwd(tag + "_gates", _f_gdn_gates, (t // tm,), gate_ins, gate_in, [big] * 4, gate_out)
    pre_ins = [qn, kn, qkv, gb, bb]
    pre_shapes = [big, big, big, jax.ShapeDtypeStruct((HEADS, n, GDN_CHUNK, GDN_CHUNK), F32), big,
                  jax.ShapeDtypeStruct((HEADS, n, 1, DH), F32)]
    pre = _seg_fwd(tag + "_prescan", _f_gdn_prescan, (HEADS, t // rows), pre_ins, pre_in, pre_shapes, pre_out)
    o, states = _gdn_scan_fwd(tag + "_scan", *pre)
    post_ins = [o, proj, p["o_g"]]
    act = _seg_fwd(tag + "_post", _f_gdn_post, (t // tm,), post_ins, post_in, [jax.ShapeDtypeStruct((t, D), BF16)],
                   [_rows(tm, D)])[0]
    out = _mm(tag + "_out", act, p["w_out"], res=x)
    return out, (x, h, proj, gate_ins, pre_ins, pre, states, post_ins, act)


def _gdn_bwd(tag, saved, p, dout, after=None):
    x, h, proj, gate_ins, pre_ins, pre, states, post_ins, act = saved
    t = x.shape[0]
    tm, rows, nq, conv, gate_in, gate_out, pre_in, pre_out, post_in = _gdn_specs(t)
    g = {"w_out": _mm(tag + "_dwout", act, dout, ta=True, out_dtype=BF16)}
    dact = _mm(tag + "_dact", dout, p["w_out"], tb=True, out_dtype=BF16, after=after)
    do, dproj, g["o_g"] = _seg_bwd(tag + "_dpost", _f_gdn_post, (t // tm,), post_ins, post_in, [dact],
                                   [_rows(tm, D)], [F32, BF16, "acc"])
    dpre = _gdn_scan_bwd(tag + "_dscan", *pre, states, do)
    dqn, dkn, dqkv, dgb, dbb = _seg_bwd(tag + "_dprescan", _f_gdn_prescan, (HEADS, t // rows), pre_ins, pre_in,
                                        dpre, pre_out, [F32] * 5)
    dqkv, dproj, g["a_log"], g["dt_bias"] = _seg_bwd(
        tag + "_dgates", _f_gdn_gates, (t // tm,), gate_ins, gate_in, [dqn, dkn, dgb, dbb], gate_out,
        [F32, BF16, "acc", "acc"], into={0: dqkv, 1: dproj})
    dproj, g["conv_w"] = _seg_bwd(tag + "_dconv", _f_conv_silu, (nq, 1), [proj, p["conv_w"]], _grid2(conv[0]),
                                  [dqkv], _grid2(conv[1]), [BF16, "acc"], into={0: dproj})
    g["w_in"] = _mm(tag + "_dwin", h, dproj, ta=True, out_dtype=BF16)
    dh = _mm(tag + "_dh", dproj, p["w_in"], tb=True, out_dtype=BF16)
    dx, g["g"] = _rms_bwd(tag + "_drms", x, p["g"], dh, dout)
    return dx, g


def _fox_specs(t):
    tm = min(t, 512)
    vec = _const((1, LANE))
    pre_in = [_rows(tm, 2 * D + LANE, 0), vec, vec, vec]
    pre_out = [_rows(tm, D), _rows(tm, D), _rows(tm, LANE)]
    return tm, pre_in, pre_out


def _per_head(c):
    return jnp.transpose(c[:, :HEADS])


def _per_lane(ch):
    return jnp.pad(jnp.transpose(ch), ((0, 0), (0, LANE - HEADS)))


def _fox_fwd_layer(tag, x, p):
    t = x.shape[0]
    tm, pre_in, pre_out = _fox_specs(t)
    blk, nb = _fox_blocks(t)
    h = _rms_fwd(tag + "_rms", x, p["g"])
    proj = _mm(tag + "_in", h, p["w_in"])
    pre_ins = [proj, p["q_g"], p["k_g"], p["b_f"]]
    qn, kn, lf = _seg_fwd(tag + "_pre", _f_fox_pre, (t // tm,), pre_ins, pre_in,
                          [jax.ShapeDtypeStruct((t, D), BF16)] * 2 + [jax.ShapeDtypeStruct((t, LANE), F32)], pre_out)
    ch = _per_head(_cumsum_rows(tag + "_cumsum", [lf], False))
    c_col, c_row = ch.reshape(HEADS, t, 1), ch.reshape(HEADS, nb, 1, blk)
    o, lse = _fox_fwd(tag + "_attn", qn, kn, proj, c_col, c_row)
    out = _mm(tag + "_out", o, p["w_out"], res=x)
    return out, (x, h, proj, pre_ins, qn, kn, c_col, c_row, o, lse)


def _fox_bwd_layer(tag, saved, p, dout, after=None):
    x, h, proj, pre_ins, qn, kn, c_col, c_row, o, lse = saved
    t = x.shape[0]
    tm, pre_in, pre_out = _fox_specs(t)
    g = {"w_out": _mm(tag + "_dwout", o, dout, ta=True, out_dtype=BF16)}
    do = _mm(tag + "_do", dout, p["w_out"], tb=True, after=after)
    dqn, dc_col, delta = _fox_dq(tag + "_dq", qn, kn, proj, c_col, c_row, o, lse, do)
    dkn, dproj, dc_row = _fox_dkv(tag + "_dkv", qn, kn, proj, c_col, c_row, lse, delta, do)
    dlf = _cumsum_rows(tag + "_dcumsum", [_per_lane(dc_col.reshape(HEADS, t)), _per_lane(dc_row.reshape(HEADS, t))],
                       True)
    dproj, g["q_g"], g["k_g"], g["b_f"] = _seg_bwd(
        tag + "_dpre", _f_fox_pre, (t // tm,), pre_ins, pre_in, [dqn, dkn, dlf], pre_out,
        [BF16, "acc", "acc", "acc"], into={0: dproj})
    g["w_in"] = _mm(tag + "_dwin", h, dproj, ta=True, out_dtype=BF16)
    dh = _mm(tag + "_dh", dproj, p["w_in"], tb=True, out_dtype=BF16)
    dx, g["g"] = _rms_bwd(tag + "_drms", x, p["g"], dh, dout)
    return dx, g


_MIXERS = ((_conf_fwd, _conf_bwd), (_gdn_fwd, _gdn_bwd), (_fox_fwd_layer, _fox_bwd_layer))


def _local_step(x, target, params_of, on_grads):
    saved, params = [], []
    for i in range(DEPTH):
        mp = params_of(i, 0, x)
        x, sm = _MIXERS[i % N_MIXERS][0](f"l{i}_mix", x, mp)
        fp = params_of(i, 1, x)
        x, sf = _ffn_fwd(f"l{i}_ffn", x, fp)
        saved.append((sm, sf))
        params.append((mp, fp))
    dx, sq = _loss_head(x, target)
    after = None
    for i in reversed(range(DEPTH)):
        dx, gf = _ffn_bwd(f"l{i}_ffn", saved[i][1], params[i][1], dx, after)
        after = on_grads(i, 1, gf)
        dx, gm = _MIXERS[i % N_MIXERS][1](f"l{i}_mix", saved[i][0], params[i][0], dx, after)
        after = on_grads(i, 0, gm)
    return sq, dx


def _unshard(name, g):
    axis = g.ndim - 2 if name in ROW_SHARDED else g.ndim - 1
    m = jnp.moveaxis(g, 0, axis - 1)
    return m.reshape(m.shape[:axis - 1] + (N_DEV * m.shape[axis],) + m.shape[axis + 1:])


def _reshard(name, full):
    axis = full.ndim - 2 if name in ROW_SHARDED else full.ndim - 1
    s = full.shape
    return jnp.moveaxis(full.reshape(s[:axis] + (N_DEV, s[axis] // N_DEV) + s[axis + 1:]), axis, 0)


def _pad_cols(a, width):
    return jnp.pad(a, [(0, 0)] * (a.ndim - 1) + [(0, width - a.shape[-1])])


def _lane_vec(v):
    return _pad_cols(v.reshape(1, -1), LANE)


REP_ROWS = 16


def _pack_rep(r):
    small = [_pad_cols(r[n].reshape(1, -1), LANE) for n in REPLICATED[2:]]
    row = jnp.concatenate(small + [jnp.zeros((1, D - LANE * len(small)), F32)], axis=1)
    pad = jnp.zeros((REP_ROWS - 2 * DEPTH - 1, D), F32)
    return jnp.concatenate([r['mix_norm_g'].reshape(DEPTH, D), r['ffn_norm_g'].reshape(DEPTH, D), row, pad], axis=0)


def _unpack_rep(a, shapes):
    out = {'mix_norm_g': a[:DEPTH], 'ffn_norm_g': a[DEPTH:2 * DEPTH]}
    for i, n in enumerate(REPLICATED[2:]):
        out[n] = a[2 * DEPTH:2 * DEPTH + 1, i * LANE:i * LANE + shapes[n][1]].reshape(shapes[n])
    return out


def _view2d(a):
    return a.reshape(-1, a.shape[-1])


MIXER_SHARDED = (
    (('conv_w_in', 'w_in'), ('conv_w_out', 'w_out'), ('conv_b_in', 'b_in'), ('conv_w_dw', 'w_dw'),
     ('conv_b_dw', 'b_dw'), ('conv_ln_g', 'ln_g'), ('conv_ln_b', 'ln_b')),
    (('gdn_w_in', 'w_in'), ('gdn_w_out', 'w_out'), ('gdn_conv_w', 'conv_w')),
    (('fox_w_in', 'w_in'), ('fox_w_out', 'w_out')),
)
FFN_SHARDED = (('ffn_w_up', 'w_up'), ('ffn_w_down', 'w_down'), ('ffn_w_dw', 'w_dw'))
MIXER_REPLICATED = (
    (),
    (('gdn_a_log', 'a_log'), ('gdn_dt_bias', 'dt_bias'), ('gdn_o_norm_g', 'o_g')),
    (('fox_b_f', 'b_f'), ('fox_q_norm_g', 'q_g'), ('fox_k_norm_g', 'k_g')),
)
ROW_VECTORS = ('conv_b_in', 'conv_b_dw', 'conv_ln_g', 'conv_ln_b')
INTERLEAVED = {'ffn_w_up': (D_FF, FFN_TC), 'ffn_w_dw': (D_FF, FFN_TC), 'conv_w_in': (D, CONF_TC),
               'conv_b_in': (D, CONF_TC)}
PACK_GROUP = 16 * D


def _part_entries(i, part):
    ent = [(n, i, k) for n, k in FFN_SHARDED] if part else [(n, i // N_MIXERS, k) for n, k in MIXER_SHARDED[i % N_MIXERS]]
    return [e for e in ent if e[0] in MATRICES], [e for e in ent if e[0] not in MATRICES]


def _interleave(a, half, blk):
    s = a.shape[:-1]
    return jnp.swapaxes(a.reshape(s + (2, half // blk, blk)), -3, -2).reshape(s + (2 * half,))


def _deinterleave(a, half, blk):
    s = a.shape[:-1]
    return jnp.swapaxes(a.reshape(s + (half // blk, 2, blk)), -3, -2).reshape(s + (2 * half,))


def _to_param(name, whole):
    if name in ROW_VECTORS:
        whole = whole[None]
    if name in INTERLEAVED:
        return _interleave(whole, *INTERLEAVED[name])
    if name == 'gdn_w_in':
        return _pad_cols(whole, GDN_PAD)
    if name == 'fox_w_in':
        return jnp.concatenate([whole[:, :2 * D], _pad_cols(whole[:, 3 * D:], LANE), whole[:, 2 * D:3 * D]], axis=1)
    return whole


def _from_grad(name, g):
    if name in INTERLEAVED:
        g = _deinterleave(g, *INTERLEAVED[name])
    if name in ROW_VECTORS:
        return g[0]
    if name == 'gdn_w_in':
        return g[:, :4 * D + 2 * HEADS]
    if name == 'fox_w_in':
        return jnp.concatenate([g[:, :2 * D], g[:, 2 * D + LANE:], g[:, 2 * D:2 * D + HEADS]], axis=1)
    return g


def _layer_col(name, n):
    if name in INTERLEAVED:
        half, blk = INTERLEAVED[name]
        return (n % half) // blk * (2 * blk) + (n // half) * blk + n % blk
    if name == 'fox_w_in':
        return np.where(n < 2 * D, n, np.where(n < 3 * D, n + LANE, n - D))
    return n


def _col_runs(name, shard_cols):
    dst = _layer_col(name, np.arange(N_DEV * shard_cols))
    runs, start = [[] for _ in range(N_DEV)], 0
    for k in range(1, dst.size + 1):
        if k == dst.size or dst[k] != dst[k - 1] + 1 or k % shard_cols == 0:
            runs[start // shard_cols].append((start % shard_cols, k - start, int(dst[start])))
            start = k
    width = {'gdn_w_in': GDN_PAD, 'fox_w_in': FOX_PAD}.get(name, dst.size)
    free = np.ones(width + 1, bool)
    free[dst] = False
    free[width] = False
    gaps, start = [], None
    for k in range(width + 1):
        if free[k] and start is None:
            start = k
        if not free[k] and start is not None:
            gaps.append((start, k - start))
            start = None
    return runs, width, gaps


def _unshard_cols(call, name, land, own, mine):
    _, r, c = land.shape
    runs, width, gaps = _col_runs(name, c)
    tr = _tile(r, 256)

    def body(me_ref, land_ref, own_ref, o_ref):
        for s in range(N_DEV):
            for src, ln, dst in runs[s]:
                o_ref[:, dst:dst + ln] = jnp.where(me_ref[0] == s, own_ref[:, src:src + ln], land_ref[s, :, src:src + ln])
        for start, ln in gaps:
            o_ref[:, start:start + ln] = jnp.zeros((tr, ln), o_ref.dtype)

    return pl.pallas_call(
        body, grid=(r // tr,),
        in_specs=[pl.BlockSpec(memory_space=pltpu.SMEM), pl.BlockSpec((N_DEV, tr, c), lambda i: (0, i, 0)),
                  pl.BlockSpec((tr, c), lambda i: (i, 0))],
        out_specs=pl.BlockSpec((tr, width), lambda i: (i, 0)), out_shape=jax.ShapeDtypeStruct((r, width), land.dtype),
        name=call, compiler_params=_params("parallel"))(mine, land, own)


def _reshard_cols(call, name, g, shard_cols):
    r = g.shape[0]
    runs, width, _ = _col_runs(name, shard_cols)
    tr = _tile(r, 256)

    def body(g_ref, o_ref):
        for s in range(N_DEV):
            for src, ln, dst in runs[s]:
                o_ref[s, :, src:src + ln] = g_ref[:, dst:dst + ln]

    return pl.pallas_call(
        body, grid=(r // tr,), in_specs=[pl.BlockSpec((tr, width), lambda i: (i, 0))],
        out_specs=pl.BlockSpec((N_DEV, tr, shard_cols), lambda i: (0, i, 0)),
        out_shape=jax.ShapeDtypeStruct((N_DEV, r, shard_cols), g.dtype), name=call,
        compiler_params=_params("parallel"))(g)


def _pack_rows(parts, lead):
    out = []
    for a in parts:
        flat = a.reshape(a.shape[:lead] + (-1,))
        size = flat.shape[-1]
        padded = -(-size // PACK_GROUP) * PACK_GROUP
        flat = jnp.pad(flat, [(0, 0)] * lead + [(0, padded - size)])
        out.append(flat.reshape(a.shape[:lead] + (padded // D, D)))
    return jnp.concatenate(out, axis=lead)


def _unpack_rows(packed, shapes, lead):
    out, row = [], 0
    head = packed.shape[:lead]
    for s in shapes:
        size = 1
        for d in s:
            size *= d
        rows = -(-size // PACK_GROUP) * (PACK_GROUP // D)
        part = lax.slice_in_dim(packed, row, row + rows, axis=lead)
        out.append(part.reshape(head + (rows * D,))[..., :size].reshape(head + tuple(s)))
        row += rows
    return out


def _part_dict(i, part, whole, rep):
    if part:
        p = {k: _to_param(n, whole[n]) for n, k in FFN_SHARDED if n in whole}
        p["g"] = rep['ffn_norm_g'][i][None]
        return p
    kind, j = i % N_MIXERS, i // N_MIXERS
    p = {k: _to_param(n, whole[n]) for n, k in MIXER_SHARDED[kind] if n in whole}
    for n, k in MIXER_REPLICATED[kind]:
        p[k] = rep[n][j][None] if rep[n].shape[-1] == DH else _lane_vec(rep[n][j])
    p["g"] = rep['mix_norm_g'][i][None]
    return p


def _part_grads(i, part, g):
    return {n: _from_grad(n, g[k]) for n, k in (FFN_SHARDED if part else MIXER_SHARDED[i % N_MIXERS])}


def _replicated_grads(grads):
    rep = {'mix_norm_g': jnp.concatenate([g[0]["g"] for g in grads]),
           'ffn_norm_g': jnp.concatenate([g[1]["g"] for g in grads])}
    for kind in range(N_MIXERS):
        for n, k in MIXER_REPLICATED[kind]:
            rep[n] = jnp.stack([grads[i][0][k] for i in range(kind, DEPTH, N_MIXERS)])
    return rep


def kernel(x, mix_norm_g, ffn_norm_g, conv_w_in, conv_b_in, conv_w_dw, conv_b_dw, conv_ln_g, conv_ln_b, conv_w_out, gdn_w_in, gdn_conv_w, gdn_a_log, gdn_dt_bias, gdn_o_norm_g, gdn_w_out, fox_w_in, fox_b_f, fox_q_norm_g, fox_k_norm_g, fox_w_out, ffn_w_up, ffn_w_dw, ffn_w_down, loss_target, m_mix_norm_g, m_ffn_norm_g, m_conv_w_in, m_conv_b_in, m_conv_w_dw, m_conv_b_dw, m_conv_ln_g, m_conv_ln_b, m_conv_w_out, m_gdn_w_in, m_gdn_conv_w, m_gdn_a_log, m_gdn_dt_bias, m_gdn_o_norm_g, m_gdn_w_out, m_fox_w_in, m_fox_b_f, m_fox_q_norm_g, m_fox_k_norm_g, m_fox_w_out, m_ffn_w_up, m_ffn_w_dw, m_ffn_w_down, v_mix_norm_g, v_ffn_norm_g, v_conv_w_in, v_conv_b_in, v_conv_w_dw, v_conv_b_dw, v_conv_ln_g, v_conv_ln_b, v_conv_w_out, v_gdn_w_in, v_gdn_conv_w, v_gdn_a_log, v_gdn_dt_bias, v_gdn_o_norm_g, v_gdn_w_out, v_fox_w_in, v_fox_b_f, v_fox_q_norm_g, v_fox_k_norm_g, v_fox_w_out, v_ffn_w_up, v_ffn_w_dw, v_ffn_w_down):
    given = dict(zip(
        WEIGHTS + ["m_" + n for n in WEIGHTS] + ["v_" + n for n in WEIGHTS],
        (mix_norm_g, ffn_norm_g, conv_w_in, conv_b_in, conv_w_dw, conv_b_dw, conv_ln_g, conv_ln_b, conv_w_out, gdn_w_in, gdn_conv_w, gdn_a_log, gdn_dt_bias, gdn_o_norm_g, gdn_w_out, fox_w_in, fox_b_f, fox_q_norm_g, fox_k_norm_g, fox_w_out, ffn_w_up, ffn_w_dw, ffn_w_down,
         m_mix_norm_g, m_ffn_norm_g, m_conv_w_in, m_conv_b_in, m_conv_w_dw, m_conv_b_dw, m_conv_ln_g, m_conv_ln_b, m_conv_w_out, m_gdn_w_in, m_gdn_conv_w, m_gdn_a_log, m_gdn_dt_bias, m_gdn_o_norm_g, m_gdn_w_out, m_fox_w_in, m_fox_b_f, m_fox_q_norm_g, m_fox_k_norm_g, m_fox_w_out, m_ffn_w_up, m_ffn_w_dw, m_ffn_w_down,
         v_mix_norm_g, v_ffn_norm_g, v_conv_w_in, v_conv_b_in, v_conv_w_dw, v_conv_b_dw, v_conv_ln_g, v_conv_ln_b, v_conv_w_out, v_gdn_w_in, v_gdn_conv_w, v_gdn_a_log, v_gdn_dt_bias, v_gdn_o_norm_g, v_gdn_w_out, v_fox_w_in, v_fox_b_f, v_fox_q_norm_g, v_fox_k_norm_g, v_fox_w_out, v_ffn_w_up, v_ffn_w_dw, v_ffn_w_down)))

    me = 4 * lax.axis_index("x") + 2 * lax.axis_index("y") + lax.axis_index("c")
    mine = me.astype(jnp.int32).reshape(1)

    gathers, token = {}, jnp.zeros((1, 1), F32)
    for i in range(DEPTH):
        for part in (0, 1):
            mats, smalls = _part_entries(i, part)
            sent = [given[n][j].astype(BF16) for n, j, _ in mats]
            if smalls:
                sent.append(_pack_rows([given[n][j] for n, j, _ in smalls], 0))
            gathers[i, part], tok = _exchange_start(f"gather{i}{'mf'[part]}_start", sent, [False] * len(sent))
            token = token + tok[0:1, 0:1]

    def params_of(i, part, x_in):
        mats, smalls = _part_entries(i, part)
        after = token if (i, part) == (0, 0) else x_in
        lands, owns = _exchange_wait(f"gather{i}{'mf'[part]}_wait", gathers[i, part], after, me)
        whole, relaid = {}, {}
        for (n, _, k), land, own in zip(mats, lands, owns):
            if n in ROW_SHARDED:
                whole[n] = _unshard(n, _fill_own(land, own, me))
            else:
                relaid[k] = _unshard_cols(f"l{i}_{n}_unshard", n, land, own, mine)
        if smalls:
            shapes = [given[n].shape[1:] for n, _, _ in smalls]
            for (n, _, _), g in zip(smalls, _unpack_rows(_fill_own(lands[-1], owns[-1], me), shapes, 1)):
                whole[n] = _unshard(n, g)
        return {**_part_dict(i, part, whole, given), **relaid}

    grads, exchanges = [[None, None] for _ in range(DEPTH)], {}

    def on_grads(i, part, g):
        grads[i][part] = g
        mats, smalls = _part_entries(i, part)
        sent = [_reshard(n, g[k]) if n in ROW_SHARDED else
                _reshard_cols(f"l{i}_{n}_reshard", n, g[k], given[n].shape[-1]) for n, _, k in mats]
        if smalls:
            sent.append(_pack_rows([_reshard(n, _from_grad(n, g[k])) for n, _, k in smalls], 1))
        last = (i, part) == (0, 0)
        if last:
            sent.append(_pack_rep(_replicated_grads(grads)))
        exchanges[i, part], tok = _exchange_start(f"grads{i}{'mf'[part]}_start", sent,
                                                  [True] * (len(sent) - last) + [False] * last)
        tokens.append(tok)
        return tok

    tokens = []
    sq, dx = _local_step(x[0], loss_target[0], params_of, on_grads)
    loss = (0.5 / D) * lax.psum(sq[0, 0], ("x", "y", "c"))

    pieces = {n: [] for n in SHARDED}
    for i in range(DEPTH):
        for part in (0, 1):
            mats, smalls = _part_entries(i, part)
            lands, owns = _exchange_wait(f"grads{i}{'mf'[part]}_wait", exchanges[i, part], tokens[-1], me)
            if (i, part) == (0, 0):
                rep_piece = (lands[-1], owns[-1])
            for (n, _, _), land, own in zip(mats, lands, owns):
                pieces[n].append((land, own))
            if smalls:
                shapes = [given[n].shape[1:] for n, _, _ in smalls]
                for (n, _, _), land, own in zip(smalls, _unpack_rows(lands[len(mats)], shapes, 1),
                                                _unpack_rows(owns[len(mats)], shapes, 0)):
                    pieces[n].append((land, own))
    new = {}
    for n in SHARDED:
        lands = [l.reshape((N_DEV, -1, l.shape[-1])) for l, _ in pieces[n]]
        owns = [o.reshape((-1, o.shape[-1])) for _, o in pieces[n]]
        if owns[0].shape[0] % 8:
            lands = [jnp.stack(lands, axis=1).reshape((N_DEV, -1, lands[0].shape[-1]))]
            owns = [jnp.stack(owns).reshape((-1, owns[0].shape[-1]))]
        outs = _adam("adam_" + n, lands, owns, mine, _view2d(given[n]), _view2d(given["m_" + n]),
                     _view2d(given["v_" + n]))
        new[n] = [o.reshape(given[n].shape) for o in outs]
    packed = [_pack_rep({n: given[pre + n] for n in REPLICATED}) for pre in ("", "m_", "v_")]
    outs = _adam("adam_replicated", [rep_piece[0]], [rep_piece[1]], mine, *packed)
    unpacked = [_unpack_rep(o, {n: given[n].shape for n in REPLICATED}) for o in outs]
    for n in REPLICATED:
        new[n] = [u[n] for u in unpacked]
    return (loss, dx[None], *[new[n][0] for n in WEIGHTS], *[new[n][1] for n in WEIGHTS],
            *[new[n][2] for n in WEIGHTS], *[new[n][3] for n in WEIGHTS])
```

```python
import functools

import jax
import jax.numpy as jnp
import numpy as np
from jax import lax
from jax.experimental import pallas as pl
from jax.experimental.pallas import tpu as pltpu

F32 = jnp.float32
BF16 = jnp.bfloat16
HIGHEST = lax.Precision.HIGHEST
HIGH = lax.Precision.HIGH

N_DEV = 8
LANE = 128
EPS = 1e-6
DEPTH = 4
N_MIXERS = 3
HEADS = 8
DH = 128
D = HEADS * DH
D_FF = 2816
CONF_K, GDN_K, FFN_K = 31, 4, 3
GDN_CHUNK = 64
GDN_PAD = 4224
FOX_PAD = 3200
FOX_V_BLOCK = (2 * D + LANE) // DH
ADAM_LR, ADAM_B1, ADAM_B2, ADAM_EPS, ADAM_WD, ADAM_STEP = 0.001, 0.9, 0.999, 1e-08, 0.01, 10
VMEM_LIMIT = 56 * 1024 * 1024

WEIGHTS = ['mix_norm_g', 'ffn_norm_g', 'conv_w_in', 'conv_b_in', 'conv_w_dw', 'conv_b_dw', 'conv_ln_g', 'conv_ln_b',
           'conv_w_out', 'gdn_w_in', 'gdn_conv_w', 'gdn_a_log', 'gdn_dt_bias', 'gdn_o_norm_g', 'gdn_w_out', 'fox_w_in',
           'fox_b_f', 'fox_q_norm_g', 'fox_k_norm_g', 'fox_w_out', 'ffn_w_up', 'ffn_w_dw', 'ffn_w_down']
REPLICATED = ['mix_norm_g', 'ffn_norm_g', 'gdn_a_log', 'gdn_dt_bias', 'gdn_o_norm_g', 'fox_b_f', 'fox_q_norm_g',
              'fox_k_norm_g']
ROW_SHARDED = ['conv_w_out', 'gdn_w_out', 'fox_w_out', 'ffn_w_down']
MATRICES = ['conv_w_in', 'conv_w_out', 'gdn_w_in', 'gdn_w_out', 'fox_w_in', 'fox_w_out', 'ffn_w_up', 'ffn_w_down']
SHARDED = [n for n in WEIGHTS if n not in REPLICATED]


def _params(*sem):
    return pltpu.CompilerParams(dimension_semantics=sem, vmem_limit_bytes=VMEM_LIMIT)


def _tile(n, cap):
    if n <= cap:
        return n
    d = (cap // LANE) * LANE
    while d >= LANE:
        if n % d == 0:
            return d
        d -= LANE
    raise ValueError(f"no lane-aligned tile of {n} under {cap}")


def _raw_dot(a, b, ca, cb, hp):
    batch = ((0,), (0,)) if a.ndim == 3 else ((), ())
    dn = (((ca,), (cb,)), batch)
    if hp:
        return lax.dot_general(a.astype(F32), b.astype(F32), dn, precision=HIGH, preferred_element_type=F32)
    return lax.dot_general(a.astype(BF16), b.astype(BF16), dn, preferred_element_type=F32)


def _raw_nn(a, b, hp=False):
    return _raw_dot(a, b, a.ndim - 1, b.ndim - 2, hp)


def _raw_nt(a, b, hp=False):
    return _raw_dot(a, b, a.ndim - 1, b.ndim - 1, hp)


def _raw_tn(a, b, hp=False):
    return _raw_dot(a, b, a.ndim - 2, b.ndim - 2, hp)


@functools.partial(jax.custom_vjp, nondiff_argnums=(2,))
def _nn(a, b, hp):
    return _raw_nn(a, b, hp)


def _nn_fwd(a, b, hp):
    return _raw_nn(a, b, hp), (a, b)


def _nn_bwd(hp, res, g):
    a, b = res
    return _raw_nt(g, b, hp), _raw_tn(a, g, hp)


_nn.defvjp(_nn_fwd, _nn_bwd)


@functools.partial(jax.custom_vjp, nondiff_argnums=(2,))
def _nt(a, b, hp):
    return _raw_nt(a, b, hp)


def _nt_fwd(a, b, hp):
    return _raw_nt(a, b, hp), (a, b)


def _nt_bwd(hp, res, g):
    a, b = res
    return _raw_nn(g, b, hp), _raw_tn(g, a, hp)


_nt.defvjp(_nt_fwd, _nt_bwd)


CONV_TAIL = 32


def _zero_tail(x):
    return jnp.concatenate([x, jnp.zeros((CONV_TAIL, x.shape[1]), x.dtype)], axis=0)


def _row(w, k):
    r = lax.broadcasted_iota(jnp.int32, w.shape, 0)
    return jnp.sum(jnp.where(r == k, w, 0.0), axis=0, keepdims=True)


@jax.custom_vjp
def _dwconv(x, w):
    t, taps = x.shape[0], w.shape[0]
    xe = _zero_tail(x)
    y = _row(w, taps - 1) * xe
    for k in range(taps - 1):
        y = y + _row(w, k) * pltpu.roll(xe, taps - 1 - k, axis=0)
    return y[:t]


def _dwconv_fwd(x, w):
    return _dwconv(x, w), (x, w)


def _dwconv_bwd(res, dy):
    x, w = res
    t, taps = x.shape[0], w.shape[0]
    xe, dye = _zero_tail(x), _zero_tail(dy)
    r = lax.broadcasted_iota(jnp.int32, w.shape, 0)
    dx = _row(w, taps - 1) * dye
    dw = jnp.where(r == taps - 1, jnp.sum(dye * xe, axis=0, keepdims=True), 0.0)
    for k in range(taps - 1):
        up = pltpu.roll(dye, t + CONV_TAIL - (taps - 1 - k), axis=0)
        dx = dx + _row(w, k) * up
        dw = dw + jnp.where(r == k, jnp.sum(up * xe, axis=0, keepdims=True), 0.0)
    return dx[:t], dw


_dwconv.defvjp(_dwconv_fwd, _dwconv_bwd)


def _sigmoid(x):
    return 1.0 / (1.0 + jnp.exp(-x))


def _silu(x):
    return x * _sigmoid(x)


def _softplus(x):
    return jnp.maximum(x, 0.0) + jnp.log(1.0 + jnp.exp(-jnp.abs(x)))


def _head_scale(x, fn):
    tm = x.shape[0]
    x3 = x.reshape(tm, HEADS, DH)
    return (x3 * fn(jnp.sum(x3 * x3, axis=-1, keepdims=True))).reshape(tm, HEADS * DH)


def _tile_lanes(g):
    return jnp.concatenate([g] * HEADS, axis=1)


def _expand_heads(v, first):
    lane = lax.broadcasted_iota(jnp.int32, (LANE, HEADS * DH), 0)
    col = lax.broadcasted_iota(jnp.int32, (LANE, HEADS * DH), 1)
    sel = (lane == col // DH + first).astype(F32)
    return _nn(v, sel, True)


def _f_rms(x, g):
    return (x * lax.rsqrt(jnp.mean(x * x, axis=-1, keepdims=True) + EPS) * g,)


def _f_rms_res(x, g):
    return (_f_rms(x, g)[0], x)


def _f_conf_glu_conv(u, b, w):
    c = u.shape[1] // 2
    return (_dwconv((u[:, :c] + b[:, :c]) * _sigmoid(u[:, c:] + b[:, c:]), w),)


def _f_conf_ln_silu(cv, b_dw, ln_g, ln_b):
    u = cv + b_dw
    xc = u - jnp.mean(u, axis=-1, keepdims=True)
    y = xc * lax.rsqrt(jnp.mean(xc * xc, axis=-1, keepdims=True) + EPS) * ln_g + ln_b
    return (_silu(y),)


def _f_conv_silu(u, w):
    return (_silu(_dwconv(u, w)),)


def _f_gdn_gates(qk, ab, a_log, dt_bias):
    qn = _head_scale(qk[:, :D], lambda ss: lax.rsqrt(ss + EPS)) * (DH ** -0.5)
    kn = _head_scale(qk[:, D:], lambda ss: lax.rsqrt(ss + EPS))
    g = -jnp.exp(a_log) * _softplus(ab + dt_bias)
    beta = _sigmoid(ab)
    return qn, kn, _expand_heads(g, 0), _expand_heads(beta, HEADS)


def _f_gdn_prescan(q, k, v, gb, bb):
    c = GDN_CHUNK
    n = q.shape[0] // c
    r3 = lambda t: t.reshape(n, c, DH)
    q3, k3, v3, g3, b3 = r3(q), r3(k), r3(v), r3(gb), r3(bb)
    ii = lax.broadcasted_iota(jnp.int32, (n, c, c), 1)
    jj = lax.broadcasted_iota(jnp.int32, (n, c, c), 2)
    lower, strict = ii >= jj, ii > jj
    gcb = _nn(lower.astype(F32), g3, True)
    gi = gcb[:, :, :c]
    gj = jnp.swapaxes(gi, 1, 2)
    decay = jnp.where(lower, jnp.exp(jnp.where(lower, gi - gj, 0.0)), 0.0)
    kb, vb = k3 * b3, v3 * b3
    a_mat = jnp.where(strict, _nt(kb, k3, False) * decay, 0.0)
    p = -a_mat
    t_mat = (ii == jj).astype(F32) + p
    for _ in range(5):
        p = _nn(p, p, True)
        t_mat = t_mat + _nn(t_mat, p, True)
    eg = jnp.exp(gcb)
    u = _nn(t_mat, vb, False)
    w = _nn(t_mat, kb * eg, False)
    qk = jnp.where(lower, _nt(q3, k3, False) * decay, 0.0)
    qg = q3 * eg
    g_last = jnp.sum(g3, axis=1, keepdims=True)
    kd = k3 * jnp.exp(g_last - gcb)
    r2 = lambda t: t.reshape(n * c, DH)
    return r2(u), r2(w), r2(qg), qk, r2(kd), jnp.exp(g_last)


def _f_gdn_post(o, z, o_g):
    on = _head_scale(o, lambda ss: lax.rsqrt(ss / DH + EPS)) * _tile_lanes(o_g)
    return (on * _silu(z),)


def _f_fox_pre(qkf, q_g, k_g, b_f):
    qn = _head_scale(qkf[:, :D], lambda ss: lax.rsqrt(ss / DH + EPS)) * _tile_lanes(q_g)
    kn = _head_scale(qkf[:, D:2 * D], lambda ss: lax.rsqrt(ss / DH + EPS)) * _tile_lanes(k_g)
    return qn, kn, -_softplus(-(qkf[:, 2 * D:] + b_f))


def _f_ffn_mid(u, w):
    c = u.shape[1] // 2
    return (_silu(_dwconv(u[:, :c], w[:, :c])) * _dwconv(u[:, c:], w[:, c:]),)


def _seg_fwd(name, f, grid, ins, in_specs, out_shapes, out_specs):
    n_in = len(ins)

    def body(*refs):
        outs = f(*[r[...].astype(F32) for r in refs[:n_in]])
        for r, o in zip(refs[n_in:], outs):
            r[...] = o.astype(r.dtype)

    return pl.pallas_call(body, grid=grid, in_specs=in_specs, out_specs=out_specs, out_shape=out_shapes, name=name,
                          compiler_params=_params(*(["parallel"] * len(grid))))(*ins)


def _seg_bwd(name, f, grid, ins, in_specs, douts, dout_specs, want, into=None):
    into = into or {}
    n_in, n_dy = len(ins), len(douts)
    diff = [i for i, w in enumerate(want) if w is not None]
    kept = [i for i in diff if i in into]
    out_shapes = [jax.ShapeDtypeStruct(ins[i].shape, into[i].dtype if i in into else F32 if want[i] == "acc" else want[i])
                  for i in diff]
    out_specs = [in_specs[i] for i in diff]
    acc_axis = len(grid) - 1

    def body(*refs):
        vals = [r[...].astype(F32) for r in refs[:n_in]]
        dys = [r[...].astype(F32) for r in refs[n_in:n_in + n_dy]]
        out_refs = refs[n_in + n_dy + len(kept):]

        def g(*dv):
            full = list(vals)
            for i, v in zip(diff, dv):
                full[i] = v
            return f(*full)

        _, vjp = jax.vjp(g, *[vals[i] for i in diff])
        grads = vjp(tuple(dys))
        first = pl.program_id(acc_axis) == 0
        for i, r, gr in zip(diff, out_refs, grads):
            if want[i] == "acc":
                @pl.when(first)
                def _(r=r, gr=gr):
                    r[...] = gr

                @pl.when(jnp.logical_not(first))
                def _(r=r, gr=gr):
                    r[...] += gr
            else:
                r[...] = gr.astype(r.dtype)

    sem = ["parallel"] * (len(grid) - 1) + ["arbitrary"]
    untouched = [pl.BlockSpec(memory_space=pl.ANY)] * len(kept)
    aliases = {n_in + n_dy + e: diff.index(i) for e, i in enumerate(kept)}
    return pl.pallas_call(body, grid=grid, in_specs=list(in_specs) + list(dout_specs) + untouched, out_specs=out_specs,
                          out_shape=out_shapes, input_output_aliases=aliases, name=name,
                          compiler_params=_params(*sem))(*ins, *douts, *[into[i] for i in kept])


def _rows(tm, width, col=0):
    return pl.BlockSpec((tm, width), lambda i, col=col: (i, col))


def _const(shape):
    return pl.BlockSpec(shape, lambda i: (0,) * len(shape))


def _cols(t, tc, off=0):
    return pl.BlockSpec((t, tc), lambda j, off=off: (0, j + off))


def _grid2(specs):
    return [pl.BlockSpec(s.block_shape, lambda j, i, f=s.index_map: f(j)) for s in specs]


def _mm(name, a, b, *, ta=False, tb=False, res=None, out_dtype=F32, after=None):
    k_dim, m = (a.shape[0], a.shape[1]) if ta else (a.shape[1], a.shape[0])
    n = b.shape[0] if tb else b.shape[1]
    tm, tn, tk = _tile(m, 1408), _tile(n, 1408), _tile(k_dim, 1408)
    nk = k_dim // tk
    grid = (m // tm, n // tn, nk)
    a_spec = pl.BlockSpec((tk, tm), lambda i, j, k: (k, i)) if ta else pl.BlockSpec((tm, tk), lambda i, j, k: (i, k))
    b_spec = pl.BlockSpec((tn, tk), lambda i, j, k: (j, k)) if tb else pl.BlockSpec((tk, tn), lambda i, j, k: (k, j))
    o_spec = pl.BlockSpec((tm, tn), lambda i, j, k: (i, j))
    dn = (((0 if ta else 1,), (1 if tb else 0,)), ((), ()))
    has_res = res is not None

    def body(*refs):
        a_ref, b_ref = refs[0], refs[1]
        res_ref = refs[2] if has_res else None
        o_ref = refs[n_in]
        p = lax.dot_general(a_ref[...].astype(BF16), b_ref[...].astype(BF16), dn, preferred_element_type=F32)

        def write(acc):
            if has_res:
                acc = acc + res_ref[...]
            if after is not None:
                acc = acc + refs[n_in - 1][0:1, 0:1]
            o_ref[...] = acc.astype(o_ref.dtype)

        if nk == 1:
            write(p)
        else:
            acc_ref = refs[-1]
            k = pl.program_id(2)

            @pl.when(k == 0)
            def _():
                acc_ref[...] = p

            @pl.when(k > 0)
            def _():
                acc_ref[...] += p

            @pl.when(k == nk - 1)
            def _():
                write(acc_ref[...])

    ins, specs = [a, b], [a_spec, b_spec]
    if has_res:
        ins.append(res)
        specs.append(o_spec)
    if after is not None:
        ins.append(after)
        specs.append(pl.BlockSpec(after.shape, lambda i, j, k: (0, 0)))
    n_in = len(ins)
    scratch = [pltpu.VMEM((tm, tn), F32)] if nk > 1 else []
    return pl.pallas_call(body, grid=grid, in_specs=specs, out_specs=o_spec, scratch_shapes=scratch,
                          out_shape=jax.ShapeDtypeStruct((m, n), out_dtype), name=name,
                          compiler_params=_params("parallel", "parallel", "arbitrary"))(*ins)


SCAN_HEADS = 2


def _head_specs(t):
    n = t // GDN_CHUNK
    col = pl.BlockSpec((t, SCAN_HEADS * DH), lambda h: (0, h))
    qk = pl.BlockSpec((SCAN_HEADS, n, GDN_CHUNK, GDN_CHUNK), lambda h: (h, 0, 0, 0))
    gl = pl.BlockSpec((SCAN_HEADS, n, 1, DH), lambda h: (h, 0, 0, 0))
    st = pl.BlockSpec((SCAN_HEADS, n, DH, DH), lambda h: (h, 0, 0, 0))
    return n, col, qk, gl, st


def _gdn_scan_fwd(name, u, w, qg, qk, kd, gl):
    t = u.shape[0]
    n, col, qk_spec, gl_spec, st_spec = _head_specs(t)
    c = GDN_CHUNK

    def body(u_ref, w_ref, qg_ref, qk_ref, kd_ref, gl_ref, o_ref, s_ref):
        def step(i, states):
            rows = pl.ds(pl.multiple_of(i * c, c), c)
            out = []
            for h, s in enumerate(states):
                cols = slice(h * DH, (h + 1) * DH)
                s_ref[h, i] = s
                vn = u_ref[rows, cols] - _raw_nn(w_ref[rows, cols], s)
                o_ref[rows, cols] = _raw_nn(qg_ref[rows, cols], s) + _raw_nn(qk_ref[h, i], vn)
                out.append(s * gl_ref[h, i] + _raw_tn(kd_ref[rows, cols], vn))
            return tuple(out)

        lax.fori_loop(0, n, step, (jnp.zeros((DH, DH), F32),) * SCAN_HEADS)

    return pl.pallas_call(
        body, grid=(HEADS // SCAN_HEADS,), in_specs=[col, col, col, qk_spec, col, gl_spec], out_specs=[col, st_spec],
        out_shape=[jax.ShapeDtypeStruct((t, D), F32), jax.ShapeDtypeStruct((HEADS, n, DH, DH), F32)], name=name,
        compiler_params=_params("parallel"))(u, w, qg, qk, kd, gl)


def _gdn_scan_bwd(name, u, w, qg, qk, kd, gl, states, do):
    t = u.shape[0]
    n, col, qk_spec, gl_spec, st_spec = _head_specs(t)
    c = GDN_CHUNK

    def body(u_ref, w_ref, qg_ref, qk_ref, kd_ref, gl_ref, s_ref, do_ref,
             du_ref, dw_ref, dqg_ref, dqk_ref, dkd_ref, dgl_ref):
        def step(r, dstates):
            i = n - 1 - r
            rows = pl.ds(pl.multiple_of(i * c, c), c)
            out = []
            for h, ds in enumerate(dstates):
                cols = slice(h * DH, (h + 1) * DH)
                s, do_c, w_c = s_ref[h, i], do_ref[rows, cols], w_ref[rows, cols]
                vn = u_ref[rows, cols] - _raw_nn(w_c, s)
                dvn = _raw_tn(qk_ref[h, i], do_c) + _raw_nn(kd_ref[rows, cols], ds)
                du_ref[rows, cols] = dvn
                dw_ref[rows, cols] = -_raw_nt(dvn, s)
                dqg_ref[rows, cols] = _raw_nt(do_c, s)
                dqk_ref[h, i] = _raw_nt(do_c, vn)
                dkd_ref[rows, cols] = _raw_nt(vn, ds)
                dgl_ref[h, i] = jnp.sum(ds * s, axis=0, keepdims=True)
                out.append(_raw_tn(qg_ref[rows, cols], do_c) + ds * gl_ref[h, i] - _raw_tn(w_c, dvn))
            return tuple(out)

        lax.fori_loop(0, n, step, (jnp.zeros((DH, DH), F32),) * SCAN_HEADS)

    big = jax.ShapeDtypeStruct((t, D), F32)
    return pl.pallas_call(
        body, grid=(HEADS // SCAN_HEADS,), in_specs=[col, col, col, qk_spec, col, gl_spec, st_spec, col],
        out_specs=[col, col, col, qk_spec, col, gl_spec],
        out_shape=[big, big, big, jax.ShapeDtypeStruct(qk.shape, F32), big, jax.ShapeDtypeStruct(gl.shape, F32)],
        name=name, compiler_params=_params("parallel"))(u, w, qg, qk, kd, gl, states, do)


def _cumsum_rows(name, parts, reverse):
    t = parts[0].shape[0]
    blk = min(t, 256)
    nb = t // blk
    n_in = len(parts)

    def body(*refs):
        o_ref = refs[n_in]
        ii = lax.broadcasted_iota(jnp.int32, (blk, blk), 0)
        jj = lax.broadcasted_iota(jnp.int32, (blk, blk), 1)
        tri = ((ii <= jj) if reverse else (ii >= jj)).astype(F32)
        carry = jnp.zeros((1, LANE), F32)
        for b in (range(nb - 1, -1, -1) if reverse else range(nb)):
            rows = pl.ds(b * blk, blk)
            x = refs[0][rows, :]
            for r in refs[1:n_in]:
                x = x + r[rows, :]
            o_ref[rows, :] = lax.dot_general(tri, x, (((1,), (0,)), ((), ())), precision=HIGHEST,
                                             preferred_element_type=F32) + carry
            carry = carry + jnp.sum(x, axis=0, keepdims=True)

    return pl.pallas_call(body, out_shape=jax.ShapeDtypeStruct((t, LANE), F32), name=name,
                          compiler_params=_params())(*parts)


def _fox_blocks(t):
    blk = 1024 if t % 1024 == 0 and t >= 2048 else 256 if t % 256 == 0 and t >= 1024 else 128
    return blk, t // blk


def _fox_logits(q, k, cq, ck, diagonal):
    s = _raw_nt(q, k) * (DH ** -0.5) + cq - ck
    if not diagonal:
        return s
    rows = lax.broadcasted_iota(jnp.int32, s.shape, 0)
    cols = lax.broadcasted_iota(jnp.int32, s.shape, 1)
    return jnp.where(cols <= rows, s, -jnp.inf)


def _fox_fwd(name, qn, kn, proj, c_col, c_row):
    t = qn.shape[0]
    blk, nb = _fox_blocks(t)
    voff = FOX_V_BLOCK

    def body(q_ref, k_ref, v_ref, cc_ref, cr_ref, o_ref, lse_ref):
        i = pl.program_id(1)
        q, cq = q_ref[...], cc_ref[...]

        def step(j, carry, diagonal=False):
            m, l, acc = carry
            rows = pl.ds(pl.multiple_of(j * blk, blk), blk)
            s = _fox_logits(q, k_ref[rows, :], cq, cr_ref[j], diagonal)
            m_new = jnp.maximum(m, jnp.max(s, axis=1, keepdims=True))
            p = jnp.exp(s - m_new)
            alpha = jnp.exp(m - m_new)
            return m_new, alpha * l + jnp.sum(p, axis=1, keepdims=True), alpha * acc + _raw_nn(p, v_ref[rows, :])

        init = (jnp.full((blk, 1), -1e30, F32), jnp.zeros((blk, 1), F32), jnp.zeros((blk, DH), F32))
        m, l, acc = step(i, lax.fori_loop(0, i, step, init), True)
        o_ref[...] = acc / l
        lse_ref[...] = m + jnp.log(l)

    tile = pl.BlockSpec((blk, DH), lambda h, i: (i, h))
    colv = pl.BlockSpec((None, blk, 1), lambda h, i: (h, i, 0))
    return pl.pallas_call(
        body, grid=(HEADS, nb),
        in_specs=[tile, pl.BlockSpec((t, DH), lambda h, i: (0, h)), pl.BlockSpec((t, DH), lambda h, i: (0, voff + h)),
                  colv, pl.BlockSpec((None, nb, 1, blk), lambda h, i: (h, 0, 0, 0))],
        out_specs=[tile, colv],
        out_shape=[jax.ShapeDtypeStruct((t, D), F32), jax.ShapeDtypeStruct((HEADS, t, 1), F32)], name=name,
        compiler_params=_params("parallel", "parallel"))(qn, kn, proj, c_col, c_row)


def _fox_dq(name, qn, kn, proj, c_col, c_row, o, lse, do):
    t = qn.shape[0]
    blk, nb = _fox_blocks(t)
    voff = FOX_V_BLOCK

    def body(q_ref, k_ref, v_ref, cc_ref, cr_ref, o_ref, lse_ref, do_ref, dq_ref, dcc_ref, dl_ref):
        i = pl.program_id(1)
        q, cq, do_b, lse_b = q_ref[...], cc_ref[...], do_ref[...], lse_ref[...]
        delta = jnp.sum(do_b * o_ref[...], axis=1, keepdims=True)

        def step(j, carry, diagonal=False):
            dq, dcc = carry
            rows = pl.ds(pl.multiple_of(j * blk, blk), blk)
            k = k_ref[rows, :]
            p = jnp.exp(_fox_logits(q, k, cq, cr_ref[j], diagonal) - lse_b)
            ds = p * (_raw_nt(do_b, v_ref[rows, :]) - delta)
            return dq + _raw_nn(ds, k), dcc + jnp.sum(ds, axis=1, keepdims=True)

        init = (jnp.zeros((blk, DH), F32), jnp.zeros((blk, 1), F32))
        dq, dcc = step(i, lax.fori_loop(0, i, step, init), True)
        dq_ref[...] = dq * (DH ** -0.5)
        dcc_ref[...] = dcc
        dl_ref[...] = delta

    tile = pl.BlockSpec((blk, DH), lambda h, i: (i, h))
    colv = pl.BlockSpec((None, blk, 1), lambda h, i: (h, i, 0))
    vec = jax.ShapeDtypeStruct((HEADS, t, 1), F32)
    return pl.pallas_call(
        body, grid=(HEADS, nb),
        in_specs=[tile, pl.BlockSpec((t, DH), lambda h, i: (0, h)), pl.BlockSpec((t, DH), lambda h, i: (0, voff + h)),
                  colv, pl.BlockSpec((None, nb, 1, blk), lambda h, i: (h, 0, 0, 0)), tile, colv, tile],
        out_specs=[tile, colv, colv],
        out_shape=[jax.ShapeDtypeStruct((t, D), F32), vec, vec], name=name,
        compiler_params=_params("parallel", "parallel"))(qn, kn, proj, c_col, c_row, o, lse, do)


def _fox_dkv(name, qn, kn, proj, c_col, c_row, lse, delta, do):
    t = qn.shape[0]
    blk, nb = _fox_blocks(t)
    voff = FOX_V_BLOCK

    def body(q_ref, k_ref, v_ref, cc_ref, cr_ref, lse_ref, dl_ref, do_ref, dk_ref, dv_ref, dcr_ref):
        j = pl.program_id(1)
        k, v, ck = k_ref[...], v_ref[...], cr_ref[...]

        def step(i, carry, diagonal=False):
            dk, dv, dcr = carry
            rows = pl.ds(pl.multiple_of(i * blk, blk), blk)
            q, do_b = q_ref[rows, :], do_ref[rows, :]
            p = jnp.exp(_fox_logits(q, k, cc_ref[rows, :], ck, diagonal) - lse_ref[rows, :])
            ds = p * (_raw_nt(do_b, v) - dl_ref[rows, :])
            return dk + _raw_tn(ds, q), dv + _raw_tn(p, do_b), dcr - jnp.sum(ds, axis=0, keepdims=True)

        init = (jnp.zeros((blk, DH), F32), jnp.zeros((blk, DH), F32), jnp.zeros((1, blk), F32))
        dk, dv, dcr = lax.fori_loop(j + 1, nb, step, step(j, init, True))
        dk_ref[...] = dk * (DH ** -0.5)
        dv_ref[...] = dv.astype(dv_ref.dtype)
        dcr_ref[...] = dcr

    full = pl.BlockSpec((t, DH), lambda h, j: (0, h))
    colf = pl.BlockSpec((None, t, 1), lambda h, j: (h, 0, 0))
    tile = pl.BlockSpec((blk, DH), lambda h, j: (j, h))
    rowv = pl.BlockSpec((None, None, 1, blk), lambda h, j: (h, j, 0, 0))
    tile_v = pl.BlockSpec((blk, DH), lambda h, j: (j, voff + h))
    big = jax.ShapeDtypeStruct((t, D), F32)
    return pl.pallas_call(
        body, grid=(HEADS, nb),
        in_specs=[full, tile, tile_v, colf, rowv, colf, colf, full],
        out_specs=[tile, tile_v, rowv],
        out_shape=[big, jax.ShapeDtypeStruct(proj.shape, BF16), jax.ShapeDtypeStruct((HEADS, nb, 1, blk), F32)], name=name,
        compiler_params=_params("parallel", "parallel"))(qn, kn, proj, c_col, c_row, lse, delta, do)


def _loss_head(y, target):
    t = y.shape[0]
    tm = min(t, 512)

    def body(y_ref, t_ref, dy_ref, sum_ref):
        err = y_ref[...] - t_ref[...]
        dy_ref[...] = err * (1.0 / D)
        part = jnp.sum(jnp.sum(err * err, axis=1, keepdims=True), axis=0, keepdims=True)

        @pl.when(pl.program_id(0) == 0)
        def _():
            sum_ref[...] = jnp.zeros_like(sum_ref)

        sum_ref[...] += jnp.broadcast_to(part, sum_ref.shape)

    return pl.pallas_call(
        body, grid=(t // tm,), in_specs=[_rows(tm, D), _rows(tm, D)], out_specs=[_rows(tm, D), _const((1, LANE))],
        out_shape=[jax.ShapeDtypeStruct((t, D), F32), jax.ShapeDtypeStruct((1, LANE), F32)], name="loss_head",
        compiler_params=_params("arbitrary"))(y, target)


ADAM_BLOCK_BYTES = 3 * 1024 * 1024


def _adam(name, lands, owns, mine, w, m, v):
    layers = len(lands)
    r, c = owns[0].shape
    block_bytes = lambda rows: N_DEV * rows * c * lands[0].dtype.itemsize
    tr = r
    if block_bytes(r) > ADAM_BLOCK_BYTES:
        tr = max(d for d in range(16, r, 16) if r % d == 0 and (block_bytes(d) <= ADAM_BLOCK_BYTES or d == 16))
    nr = r // tr

    def body(*refs):
        me_ref, land_refs, own_refs = refs[0], refs[1:1 + layers], refs[1 + layers:1 + 2 * layers]
        w_ref, m_ref, v_ref, g_ref, d_ref, nm_ref, nv_ref = refs[1 + 2 * layers:]
        layer = pl.program_id(0)
        for l in range(layers):
            @pl.when(layer == l)
            def _(l=l):
                g = jnp.zeros((tr, c), F32)
                for s in range(N_DEV):
                    g = g + jnp.where(me_ref[0] == s, own_refs[l][...], land_refs[l][s]).astype(F32)
                nm = ADAM_B1 * m_ref[...] + (1.0 - ADAM_B1) * g
                nv = ADAM_B2 * v_ref[...] + (1.0 - ADAM_B2) * (g * g)
                m_hat = nm / (1.0 - ADAM_B1 ** ADAM_STEP)
                v_hat = nv / (1.0 - ADAM_B2 ** ADAM_STEP)
                g_ref[...] = g
                d_ref[...] = -ADAM_LR * (m_hat / (jnp.sqrt(v_hat) + ADAM_EPS) + ADAM_WD * w_ref[...])
                nm_ref[...] = nm
                nv_ref[...] = nv

    at = lambda l: (lambda layer, i: jnp.where(layer == l, i, 0))
    land_specs = [pl.BlockSpec((N_DEV, tr, c), lambda layer, i, f=at(l): (0, f(layer, i), 0)) for l in range(layers)]
    own_specs = [pl.BlockSpec((tr, c), lambda layer, i, f=at(l): (f(layer, i), 0)) for l in range(layers)]
    blk = pl.BlockSpec((tr, c), lambda layer, i: (layer * nr + i, 0))
    out = jax.ShapeDtypeStruct(w.shape, F32)
    return pl.pallas_call(
        body, grid=(layers, nr),
        in_specs=[pl.BlockSpec(memory_space=pltpu.SMEM)] + land_specs + own_specs + [blk, blk, blk],
        out_specs=[blk] * 4, out_shape=[out] * 4, name=name,
        compiler_params=_params("arbitrary", "arbitrary"))(mine, *lands, *owns, w, m, v)


_HBM = pl.BlockSpec(memory_space=pltpu.HBM)
_SEM = pl.BlockSpec(memory_space=pltpu.SEMAPHORE)
_EFFECT = pltpu.SideEffectType.DATAFLOW_SIDE_EFFECTING


def _each_peer(x, y, c):
    flip = lambda v, bit: 1 - v if bit else v
    for p in range(1, N_DEV):
        px, py, pc = flip(x, p & 4), flip(y, p & 2), flip(c, p & 1)
        yield p, (px, py, pc), 4 * px + 2 * py + pc


def _sem(a, p):
    return a * (N_DEV - 1) + p - 1


def _exchange_start(name, arrays, scatter):
    n = len(arrays)
    lands = [lax.empty((N_DEV,) + (a.shape[1:] if sc else a.shape), a.dtype) for a, sc in zip(arrays, scatter)]

    def body(*refs):
        in_refs, land_refs = refs[:n], refs[n:2 * n]
        send_sems, recv_sems, token = refs[2 * n], refs[2 * n + 1], refs[-1]
        x, y, c = lax.axis_index("x"), lax.axis_index("y"), lax.axis_index("c")
        me = 4 * x + 2 * y + c
        for p, coords, peer in _each_peer(x, y, c):
            for a in range(n):
                pltpu.make_async_remote_copy(
                    src_ref=in_refs[a].at[peer] if scatter[a] else in_refs[a], dst_ref=land_refs[a].at[me],
                    send_sem=send_sems.at[_sem(a, p)], recv_sem=recv_sems.at[_sem(a, p)], device_id=coords,
                    device_id_type=pl.DeviceIdType.MESH).start()
        token[...] = jnp.zeros_like(token)

    sems = pltpu.SemaphoreType.DMA((n * (N_DEV - 1),))
    hbm = lambda a: pltpu.HBM(a.shape, a.dtype)
    out = pl.pallas_call(
        body, name=name,
        out_shape=(sems, sems, *[hbm(a) for a in arrays], *[hbm(l) for l in lands],
                   jax.ShapeDtypeStruct((8, LANE), F32)),
        in_specs=[_HBM] * (2 * n), out_specs=(_SEM, _SEM, *[_HBM] * (2 * n), pl.BlockSpec(memory_space=pltpu.VMEM)),
        input_output_aliases={i: 2 + i for i in range(2 * n)},
        compiler_params=pltpu.CompilerParams(has_side_effects=_EFFECT),
    )(*[pltpu.with_memory_space_constraint(a, pltpu.HBM) for a in arrays],
      *[pltpu.with_memory_space_constraint(l, pltpu.HBM) for l in lands])
    return (out[0], out[1], list(out[2:2 + n]), list(out[2 + n:2 + 2 * n]), scatter), out[-1]


def _exchange_wait(name, started, after, me):
    send_sems, recv_sems, sent, lands, scatter = started
    n = len(sent)

    def body(*refs):
        in_refs, land_refs = refs[:n], refs[n:2 * n]
        send_sems, recv_sems = refs[2 * n], refs[2 * n + 1]
        x, y, c = lax.axis_index("x"), lax.axis_index("y"), lax.axis_index("c")
        for p, coords, peer in _each_peer(x, y, c):
            for a in range(n):
                cp = pltpu.make_async_remote_copy(
                    src_ref=in_refs[a].at[peer] if scatter[a] else in_refs[a], dst_ref=land_refs[a].at[peer],
                    send_sem=send_sems.at[_sem(a, p)], recv_sem=recv_sems.at[_sem(a, p)], device_id=coords,
                    device_id_type=pl.DeviceIdType.MESH)
                cp.wait_send()
                cp.wait_recv()

    hbm = lambda a: pltpu.HBM(a.shape, a.dtype)
    out = pl.pallas_call(
        body, name=name, out_shape=(*[hbm(a) for a in sent], *[hbm(l) for l in lands]),
        in_specs=[_HBM] * (2 * n) + [_SEM, _SEM, pl.BlockSpec(memory_space=pl.ANY)], out_specs=[_HBM] * (2 * n),
        input_output_aliases={i: i for i in range(2 * n)},
        compiler_params=pltpu.CompilerParams(has_side_effects=_EFFECT),
    )(*sent, *lands, send_sems, recv_sems, after)
    owns = [lax.dynamic_index_in_dim(out[a], me, 0, keepdims=False) if scatter[a] else out[a] for a in range(n)]
    return list(out[n:]), owns


def _fill_own(land, own, me):
    slot = lax.broadcasted_iota(jnp.int32, (N_DEV,) + (1,) * own.ndim, 0)
    return jnp.where(slot == me, own[None], land)


def _rms_fwd(name, x, g):
    t = x.shape[0]
    tm = min(t, 512)
    return _seg_fwd(name, _f_rms, (t // tm,), [x, g], [_rows(tm, D), _const((1, D))],
                    [jax.ShapeDtypeStruct((t, D), BF16)], [_rows(tm, D)])[0]


def _rms_bwd(name, x, g, dh, dres):
    t = x.shape[0]
    tm = min(t, 512)
    return _seg_bwd(name, _f_rms_res, (t // tm,), [x, g], [_rows(tm, D), _const((1, D))],
                    [dh, dres], [_rows(tm, D), _rows(tm, D)], [F32, "acc"])


FFN_TC = 256
CONF_TC = 128


def _ffn_specs(t):
    nf = D_FF // FFN_TC
    return nf, [_cols(t, 2 * FFN_TC), _cols(FFN_K, 2 * FFN_TC)], [_cols(t, FFN_TC)]


def _ffn_fwd(tag, x, p):
    t = x.shape[0]
    nf, in_specs, out_specs = _ffn_specs(t)
    h = _rms_fwd(tag + "_rms", x, p["g"])
    u0 = _mm(tag + "_up", h, p["w_up"])
    act = _seg_fwd(tag + "_mid", _f_ffn_mid, (nf,), [u0, p["w_dw"]], in_specs,
                   [jax.ShapeDtypeStruct((t, D_FF), BF16)], out_specs)[0]
    out = _mm(tag + "_down", act, p["w_down"], res=x)
    return out, (x, h, u0, act)


def _ffn_bwd(tag, saved, p, dout, after=None):
    x, h, u0, act = saved
    t = x.shape[0]
    nf, in_specs, out_specs = _ffn_specs(t)
    g = {"w_down": _mm(tag + "_dwdown", act, dout, ta=True, out_dtype=BF16)}
    dact = _mm(tag + "_dact", dout, p["w_down"], tb=True, out_dtype=BF16, after=after)
    du0, g["w_dw"] = _seg_bwd(tag + "_dmid", _f_ffn_mid, (nf, 1), [u0, p["w_dw"]], _grid2(in_specs), [dact],
                              _grid2(out_specs), [BF16, "acc"])
    g["w_up"] = _mm(tag + "_dwup", h, du0, ta=True, out_dtype=BF16)
    dh = _mm(tag + "_dh", du0, p["w_up"], tb=True, out_dtype=BF16)
    dx, g["g"] = _rms_bwd(tag + "_drms", x, p["g"], dh, dout)
    return dx, g


def _conf_specs(t):
    tm = min(t, 512)
    nc = D // CONF_TC
    vec = _const((1, D))
    glu_in = [_cols(t, 2 * CONF_TC), _cols(1, 2 * CONF_TC), _cols(CONF_K, CONF_TC)]
    return tm, nc, glu_in, [_cols(t, CONF_TC)], [_rows(tm, D), vec, vec, vec]


def _conf_fwd(tag, x, p):
    t = x.shape[0]
    tm, nc, glu_in, glu_out, ln_in = _conf_specs(t)
    h = _rms_fwd(tag + "_rms", x, p["g"])
    u = _mm(tag + "_in", h, p["w_in"])
    cv = _seg_fwd(tag + "_gluconv", _f_conf_glu_conv, (nc,), [u, p["b_in"], p["w_dw"]], glu_in,
                  [jax.ShapeDtypeStruct((t, D), F32)], glu_out)[0]
    act = _seg_fwd(tag + "_lnsilu", _f_conf_ln_silu, (t // tm,), [cv, p["b_dw"], p["ln_g"], p["ln_b"]], ln_in,
                   [jax.ShapeDtypeStruct((t, D), BF16)], [_rows(tm, D)])[0]
    out = _mm(tag + "_out", act, p["w_out"], res=x)
    return out, (x, h, u, cv, act)


def _conf_bwd(tag, saved, p, dout, after=None):
    x, h, u, cv, act = saved
    t = x.shape[0]
    tm, nc, glu_in, glu_out, ln_in = _conf_specs(t)
    g = {"w_out": _mm(tag + "_dwout", act, dout, ta=True, out_dtype=BF16)}
    dact = _mm(tag + "_dact", dout, p["w_out"], tb=True, out_dtype=BF16, after=after)
    dcv, g["b_dw"], g["ln_g"], g["ln_b"] = _seg_bwd(
        tag + "_dlnsilu", _f_conf_ln_silu, (t // tm,), [cv, p["b_dw"], p["ln_g"], p["ln_b"]], ln_in, [dact],
        [_rows(tm, D)], [F32, "acc", "acc", "acc"])
    du, g["b_in"], g["w_dw"] = _seg_bwd(tag + "_dgluconv", _f_conf_glu_conv, (nc, 1), [u, p["b_in"], p["w_dw"]],
                                        _grid2(glu_in), [dcv], _grid2(glu_out), [BF16, "acc", "acc"])
    g["w_in"] = _mm(tag + "_dwin", h, du, ta=True, out_dtype=BF16)
    dh = _mm(tag + "_dh", du, p["w_in"], tb=True, out_dtype=BF16)
    dx, g["g"] = _rms_bwd(tag + "_drms", x, p["g"], dh, dout)
    return dx, g


def _gdn_specs(t):
    tc, tm, rows = 256, min(t, 256), min(t, 1024)
    nq = 3 * D // tc
    conv = ([_cols(t, tc), _cols(GDN_K, tc)], [_cols(t, tc)])
    gate_in = [_rows(tm, 2 * D, 0), _rows(tm, LANE, 4 * D // LANE), _const((1, LANE)), _const((1, LANE))]
    gate_out = [_rows(tm, D)] * 4
    head = pl.BlockSpec((rows, DH), lambda h, i: (i, h))
    headv = pl.BlockSpec((rows, DH), lambda h, i: (i, 2 * HEADS + h))
    nch = rows // GDN_CHUNK
    pre_in = [head, head, headv, head, head]
    pre_out = [head, head, head, pl.BlockSpec((None, nch, GDN_CHUNK, GDN_CHUNK), lambda h, i: (h, i, 0, 0)), head,
               pl.BlockSpec((None, nch, 1, DH), lambda h, i: (h, i, 0, 0))]
    post_in = [_rows(tm, D), _rows(tm, D, 3), _const((1, DH))]
    return tm, rows, nq, conv, gate_in, gate_out, pre_in, pre_out, post_in


def _gdn_fwd(tag, x, p):
    t = x.shape[0]
    tm, rows, nq, conv, gate_in, gate_out, pre_in, pre_out, post_in = _gdn_specs(t)
    n = t // GDN_CHUNK
    big = jax.ShapeDtypeStruct((t, D), F32)
    h = _rms_fwd(tag + "_rms", x, p["g"])
    proj = _mm(tag + "_in", h, p["w_in"])
    qkv = _seg_fwd(tag + "_conv", _f_conv_silu, (nq,), [proj, p["conv_w"]], conv[0],
                   [jax.ShapeDtypeStruct((t, 3 * D), F32)], conv[1])[0]
    gate_ins = [qkv, proj, p["a_log"], p["dt_bias"]]
    qn, kn, gb, bb = _seg_fwd(tag + "_gates", _f_gdn_gates, (t // tm,), gate_ins, gate_in, [big] * 4, gate_out)
    pre_ins = [qn, kn, qkv, gb, bb]
    pre_shapes = [big, big, big, jax.ShapeDtypeStruct((HEADS, n, GDN_CHUNK, GDN_CHUNK), F32), big,
                  jax.ShapeDtypeStruct((HEADS, n, 1, DH), F32)]
    pre = _seg_fwd(tag + "_prescan", _f_gdn_prescan, (HEADS, t // rows), pre_ins, pre_in, pre_shapes, pre_out)
    o, states = _gdn_scan_fwd(tag + "_scan", *pre)
    post_ins = [o, proj, p["o_g"]]
    act = _seg_fwd(tag + "_post", _f_gdn_post, (t // tm,), post_ins, post_in, [jax.ShapeDtypeStruct((t, D), BF16)],
                   [_rows(tm, D)])[0]
    out = _mm(tag + "_out", act, p["w_out"], res=x)
    return out, (x, h, proj, gate_ins, pre_ins, pre, states, post_ins, act)


def _gdn_bwd(tag, saved, p, dout, after=None):
    x, h, proj, gate_ins, pre_ins, pre, states, post_ins, act = saved
    t = x.shape[0]
    tm, rows, nq, conv, gate_in, gate_out, pre_in, pre_out, post_in = _gdn_specs(t)
    g = {"w_out": _mm(tag + "_dwout", act, dout, ta=True, out_dtype=BF16)}
    dact = _mm(tag + "_dact", dout, p["w_out"], tb=True, out_dtype=BF16, after=after)
    do, dproj, g["o_g"] = _seg_bwd(tag + "_dpost", _f_gdn_post, (t // tm,), post_ins, post_in, [dact],
                                   [_rows(tm, D)], [F32, BF16, "acc"])
    dpre = _gdn_scan_bwd(tag + "_dscan", *pre, states, do)
    dqn, dkn, dqkv, dgb, dbb = _seg_bwd(tag + "_dprescan", _f_gdn_prescan, (HEADS, t // rows), pre_ins, pre_in,
                                        dpre, pre_out, [F32] * 5)
    dqkv, dproj, g["a_log"], g["dt_bias"] = _seg_bwd(
        tag + "_dgates", _f_gdn_gates, (t // tm,), gate_ins, gate_in, [dqn, dkn, dgb, dbb], gate_out,
        [F32, BF16, "acc", "acc"], into={0: dqkv, 1: dproj})
    dproj, g["conv_w"] = _seg_bwd(tag + "_dconv", _f_conv_silu, (nq, 1), [proj, p["conv_w"]], _grid2(conv[0]),
                                  [dqkv], _grid2(conv[1]), [BF16, "acc"], into={0: dproj})
    g["w_in"] = _mm(tag + "_dwin", h, dproj, ta=True, out_dtype=BF16)
    dh = _mm(tag + "_dh", dproj, p["w_in"], tb=True, out_dtype=BF16)
    dx, g["g"] = _rms_bwd(tag + "_drms", x, p["g"], dh, dout)
    return dx, g


def _fox_specs(t):
    tm = min(t, 512)
    vec = _const((1, LANE))
    pre_in = [_rows(tm, 2 * D + LANE, 0), vec, vec, vec]
    pre_out = [_rows(tm, D), _rows(tm, D), _rows(tm, LANE)]
    return tm, pre_in, pre_out


def _per_head(c):
    return jnp.transpose(c[:, :HEADS])


def _per_lane(ch):
    return jnp.pad(jnp.transpose(ch), ((0, 0), (0, LANE - HEADS)))


def _fox_fwd_layer(tag, x, p):
    t = x.shape[0]
    tm, pre_in, pre_out = _fox_specs(t)
    blk, nb = _fox_blocks(t)
    h = _rms_fwd(tag + "_rms", x, p["g"])
    proj = _mm(tag + "_in", h, p["w_in"])
    pre_ins = [proj, p["q_g"], p["k_g"], p["b_f"]]
    qn, kn, lf = _seg_fwd(tag + "_pre", _f_fox_pre, (t // tm,), pre_ins, pre_in,
                          [jax.ShapeDtypeStruct((t, D), BF16)] * 2 + [jax.ShapeDtypeStruct((t, LANE), F32)], pre_out)
    ch = _per_head(_cumsum_rows(tag + "_cumsum", [lf], False))
    c_col, c_row = ch.reshape(HEADS, t, 1), ch.reshape(HEADS, nb, 1, blk)
    o, lse = _fox_fwd(tag + "_attn", qn, kn, proj, c_col, c_row)
    out = _mm(tag + "_out", o, p["w_out"], res=x)
    return out, (x, h, proj, pre_ins, qn, kn, c_col, c_row, o, lse)


def _fox_bwd_layer(tag, saved, p, dout, after=None):
    x, h, proj, pre_ins, qn, kn, c_col, c_row, o, lse = saved
    t = x.shape[0]
    tm, pre_in, pre_out = _fox_specs(t)
    g = {"w_out": _mm(tag + "_dwout", o, dout, ta=True, out_dtype=BF16)}
    do = _mm(tag + "_do", dout, p["w_out"], tb=True, after=after)
    dqn, dc_col, delta = _fox_dq(tag + "_dq", qn, kn, proj, c_col, c_row, o, lse, do)
    dkn, dproj, dc_row = _fox_dkv(tag + "_dkv", qn, kn, proj, c_col, c_row, lse, delta, do)
    dlf = _cumsum_rows(tag + "_dcumsum", [_per_lane(dc_col.reshape(HEADS, t)), _per_lane(dc_row.reshape(HEADS, t))],
                       True)
    dproj, g["q_g"], g["k_g"], g["b_f"] = _seg_bwd(
        tag + "_dpre", _f_fox_pre, (t // tm,), pre_ins, pre_in, [dqn, dkn, dlf], pre_out,
        [BF16, "acc", "acc", "acc"], into={0: dproj})
    g["w_in"] = _mm(tag + "_dwin", h, dproj, ta=True, out_dtype=BF16)
    dh = _mm(tag + "_dh", dproj, p["w_in"], tb=True, out_dtype=BF16)
    dx, g["g"] = _rms_bwd(tag + "_drms", x, p["g"], dh, dout)
    return dx, g


_MIXERS = ((_conf_fwd, _conf_bwd), (_gdn_fwd, _gdn_bwd), (_fox_fwd_layer, _fox_bwd_layer))


def _local_step(x, target, params_of, on_grads):
    saved, params = [], []
    for i in range(DEPTH):
        mp = params_of(i, 0, x)
        x, sm = _MIXERS[i % N_MIXERS][0](f"l{i}_mix", x, mp)
        fp = params_of(i, 1, x)
        x, sf = _ffn_fwd(f"l{i}_ffn", x, fp)
        saved.append((sm, sf))
        params.append((mp, fp))
    dx, sq = _loss_head(x, target)
    after = None
    for i in reversed(range(DEPTH)):
        dx, gf = _ffn_bwd(f"l{i}_ffn", saved[i][1], params[i][1], dx, after)
        after = on_grads(i, 1, gf)
        dx, gm = _MIXERS[i % N_MIXERS][1](f"l{i}_mix", saved[i][0], params[i][0], dx, after)
        after = on_grads(i, 0, gm)
    return sq, dx


def _unshard(name, g):
    axis = g.ndim - 2 if name in ROW_SHARDED else g.ndim - 1
    m = jnp.moveaxis(g, 0, axis - 1)
    return m.reshape(m.shape[:axis - 1] + (N_DEV * m.shape[axis],) + m.shape[axis + 1:])


def _reshard(name, full):
    axis = full.ndim - 2 if name in ROW_SHARDED else full.ndim - 1
    s = full.shape
    return jnp.moveaxis(full.reshape(s[:axis] + (N_DEV, s[axis] // N_DEV) + s[axis + 1:]), axis, 0)


def _pad_cols(a, width):
    return jnp.pad(a, [(0, 0)] * (a.ndim - 1) + [(0, width - a.shape[-1])])


def _lane_vec(v):
    return _pad_cols(v.reshape(1, -1), LANE)


REP_ROWS = 16


def _pack_rep(r):
    small = [_pad_cols(r[n].reshape(1, -1), LANE) for n in REPLICATED[2:]]
    row = jnp.concatenate(small + [jnp.zeros((1, D - LANE * len(small)), F32)], axis=1)
    pad = jnp.zeros((REP_ROWS - 2 * DEPTH - 1, D), F32)
    return jnp.concatenate([r['mix_norm_g'].reshape(DEPTH, D), r['ffn_norm_g'].reshape(DEPTH, D), row, pad], axis=0)


def _unpack_rep(a, shapes):
    out = {'mix_norm_g': a[:DEPTH], 'ffn_norm_g': a[DEPTH:2 * DEPTH]}
    for i, n in enumerate(REPLICATED[2:]):
        out[n] = a[2 * DEPTH:2 * DEPTH + 1, i * LANE:i * LANE + shapes[n][1]].reshape(shapes[n])
    return out


def _view2d(a):
    return a.reshape(-1, a.shape[-1])


MIXER_SHARDED = (
    (('conv_w_in', 'w_in'), ('conv_w_out', 'w_out'), ('conv_b_in', 'b_in'), ('conv_w_dw', 'w_dw'),
     ('conv_b_dw', 'b_dw'), ('conv_ln_g', 'ln_g'), ('conv_ln_b', 'ln_b')),
    (('gdn_w_in', 'w_in'), ('gdn_w_out', 'w_out'), ('gdn_conv_w', 'conv_w')),
    (('fox_w_in', 'w_in'), ('fox_w_out', 'w_out')),
)
FFN_SHARDED = (('ffn_w_up', 'w_up'), ('ffn_w_down', 'w_down'), ('ffn_w_dw', 'w_dw'))
MIXER_REPLICATED = (
    (),
    (('gdn_a_log', 'a_log'), ('gdn_dt_bias', 'dt_bias'), ('gdn_o_norm_g', 'o_g')),
    (('fox_b_f', 'b_f'), ('fox_q_norm_g', 'q_g'), ('fox_k_norm_g', 'k_g')),
)
ROW_VECTORS = ('conv_b_in', 'conv_b_dw', 'conv_ln_g', 'conv_ln_b')
INTERLEAVED = {'ffn_w_up': (D_FF, FFN_TC), 'ffn_w_dw': (D_FF, FFN_TC), 'conv_w_in': (D, CONF_TC),
               'conv_b_in': (D, CONF_TC)}
PACK_GROUP = 16 * D


def _part_entries(i, part):
    ent = [(n, i, k) for n, k in FFN_SHARDED] if part else [(n, i // N_MIXERS, k) for n, k in MIXER_SHARDED[i % N_MIXERS]]
    return [e for e in ent if e[0] in MATRICES], [e for e in ent if e[0] not in MATRICES]


def _interleave(a, half, blk):
    s = a.shape[:-1]
    return jnp.swapaxes(a.reshape(s + (2, half // blk, blk)), -3, -2).reshape(s + (2 * half,))


def _deinterleave(a, half, blk):
    s = a.shape[:-1]
    return jnp.swapaxes(a.reshape(s + (half // blk, 2, blk)), -3, -2).reshape(s + (2 * half,))


def _to_param(name, whole):
    if name in ROW_VECTORS:
        whole = whole[None]
    if name in INTERLEAVED:
        return _interleave(whole, *INTERLEAVED[name])
    if name == 'gdn_w_in':
        return _pad_cols(whole, GDN_PAD)
    if name == 'fox_w_in':
        return jnp.concatenate([whole[:, :2 * D], _pad_cols(whole[:, 3 * D:], LANE), whole[:, 2 * D:3 * D]], axis=1)
    return whole


def _from_grad(name, g):
    if name in INTERLEAVED:
        g = _deinterleave(g, *INTERLEAVED[name])
    if name in ROW_VECTORS:
        return g[0]
    if name == 'gdn_w_in':
        return g[:, :4 * D + 2 * HEADS]
    if name == 'fox_w_in':
        return jnp.concatenate([g[:, :2 * D], g[:, 2 * D + LANE:], g[:, 2 * D:2 * D + HEADS]], axis=1)
    return g


def _layer_col(name, n):
    if name in INTERLEAVED:
        half, blk = INTERLEAVED[name]
        return (n % half) // blk * (2 * blk) + (n // half) * blk + n % blk
    if name == 'fox_w_in':
        return np.where(n < 2 * D, n, np.where(n < 3 * D, n + LANE, n - D))
    return n


def _col_runs(name, shard_cols):
    dst = _layer_col(name, np.arange(N_DEV * shard_cols))
    runs, start = [[] for _ in range(N_DEV)], 0
    for k in range(1, dst.size + 1):
        if k == dst.size or dst[k] != dst[k - 1] + 1 or k % shard_cols == 0:
            runs[start // shard_cols].append((start % shard_cols, k - start, int(dst[start])))
            start = k
    width = {'gdn_w_in': GDN_PAD, 'fox_w_in': FOX_PAD}.get(name, dst.size)
    free = np.ones(width + 1, bool)
    free[dst] = False
    free[width] = False
    gaps, start = [], None
    for k in range(width + 1):
        if free[k] and start is None:
            start = k
        if not free[k] and start is not None:
            gaps.append((start, k - start))
            start = None
    return runs, width, gaps


def _unshard_cols(call, name, land, own, mine):
    _, r, c = land.shape
    runs, width, gaps = _col_runs(name, c)
    tr = _tile(r, 256)

    def body(me_ref, land_ref, own_ref, o_ref):
        for s in range(N_DEV):
            for src, ln, dst in runs[s]:
                o_ref[:, dst:dst + ln] = jnp.where(me_ref[0] == s, own_ref[:, src:src + ln], land_ref[s, :, src:src + ln])
        for start, ln in gaps:
            o_ref[:, start:start + ln] = jnp.zeros((tr, ln), o_ref.dtype)

    return pl.pallas_call(
        body, grid=(r // tr,),
        in_specs=[pl.BlockSpec(memory_space=pltpu.SMEM), pl.BlockSpec((N_DEV, tr, c), lambda i: (0, i, 0)),
                  pl.BlockSpec((tr, c), lambda i: (i, 0))],
        out_specs=pl.BlockSpec((tr, width), lambda i: (i, 0)), out_shape=jax.ShapeDtypeStruct((r, width), land.dtype),
        name=call, compiler_params=_params("parallel"))(mine, land, own)


def _reshard_cols(call, name, g, shard_cols):
    r = g.shape[0]
    runs, width, _ = _col_runs(name, shard_cols)
    tr = _tile(r, 256)

    def body(g_ref, o_ref):
        for s in range(N_DEV):
            for src, ln, dst in runs[s]:
                o_ref[s, :, src:src + ln] = g_ref[:, dst:dst + ln]

    return pl.pallas_call(
        body, grid=(r // tr,), in_specs=[pl.BlockSpec((tr, width), lambda i: (i, 0))],
        out_specs=pl.BlockSpec((N_DEV, tr, shard_cols), lambda i: (0, i, 0)),
        out_shape=jax.ShapeDtypeStruct((N_DEV, r, shard_cols), g.dtype), name=call,
        compiler_params=_params("parallel"))(g)


def _pack_rows(parts, lead):
    out = []
    for a in parts:
        flat = a.reshape(a.shape[:lead] + (-1,))
        size = flat.shape[-1]
        padded = -(-size // PACK_GROUP) * PACK_GROUP
        flat = jnp.pad(flat, [(0, 0)] * lead + [(0, padded - size)])
        out.append(flat.reshape(a.shape[:lead] + (padded // D, D)))
    return jnp.concatenate(out, axis=lead)


def _unpack_rows(packed, shapes, lead):
    out, row = [], 0
    head = packed.shape[:lead]
    for s in shapes:
        size = 1
        for d in s:
            size *= d
        rows = -(-size // PACK_GROUP) * (PACK_GROUP // D)
        part = lax.slice_in_dim(packed, row, row + rows, axis=lead)
        out.append(part.reshape(head + (rows * D,))[..., :size].reshape(head + tuple(s)))
        row += rows
    return out


def _part_dict(i, part, whole, rep):
    if part:
        p = {k: _to_param(n, whole[n]) for n, k in FFN_SHARDED if n in whole}
        p["g"] = rep['ffn_norm_g'][i][None]
        return p
    kind, j = i % N_MIXERS, i // N_MIXERS
    p = {k: _to_param(n, whole[n]) for n, k in MIXER_SHARDED[kind] if n in whole}
    for n, k in MIXER_REPLICATED[kind]:
        p[k] = rep[n][j][None] if rep[n].shape[-1] == DH else _lane_vec(rep[n][j])
    p["g"] = rep['mix_norm_g'][i][None]
    return p


def _part_grads(i, part, g):
    return {n: _from_grad(n, g[k]) for n, k in (FFN_SHARDED if part else MIXER_SHARDED[i % N_MIXERS])}


def _replicated_grads(grads):
    rep = {'mix_norm_g': jnp.concatenate([g[0]["g"] for g in grads]),
           'ffn_norm_g': jnp.concatenate([g[1]["g"] for g in grads])}
    for kind in range(N_MIXERS):
        for n, k in MIXER_REPLICATED[kind]:
            rep[n] = jnp.stack([grads[i][0][k] for i in range(kind, DEPTH, N_MIXERS)])
    return rep


def kernel(x, mix_norm_g, ffn_norm_g, conv_w_in, conv_b_in, conv_w_dw, conv_b_dw, conv_ln_g, conv_ln_b, conv_w_out, gdn_w_in, gdn_conv_w, gdn_a_log, gdn_dt_bias, gdn_o_norm_g, gdn_w_out, fox_w_in, fox_b_f, fox_q_norm_g, fox_k_norm_g, fox_w_out, ffn_w_up, ffn_w_dw, ffn_w_down, loss_target, m_mix_norm_g, m_ffn_norm_g, m_conv_w_in, m_conv_b_in, m_conv_w_dw, m_conv_b_dw, m_conv_ln_g, m_conv_ln_b, m_conv_w_out, m_gdn_w_in, m_gdn_conv_w, m_gdn_a_log, m_gdn_dt_bias, m_gdn_o_norm_g, m_gdn_w_out, m_fox_w_in, m_fox_b_f, m_fox_q_norm_g, m_fox_k_norm_g, m_fox_w_out, m_ffn_w_up, m_ffn_w_dw, m_ffn_w_down, v_mix_norm_g, v_ffn_norm_g, v_conv_w_in, v_conv_b_in, v_conv_w_dw, v_conv_b_dw, v_conv_ln_g, v_conv_ln_b, v_conv_w_out, v_gdn_w_in, v_gdn_conv_w, v_gdn_a_log, v_gdn_dt_bias, v_gdn_o_norm_g, v_gdn_w_out, v_fox_w_in, v_fox_b_f, v_fox_q_norm_g, v_fox_k_norm_g, v_fox_w_out, v_ffn_w_up, v_ffn_w_dw, v_ffn_w_down):
    given = dict(zip(
        WEIGHTS + ["m_" + n for n in WEIGHTS] + ["v_" + n for n in WEIGHTS],
        (mix_norm_g, ffn_norm_g, conv_w_in, conv_b_in, conv_w_dw, conv_b_dw, conv_ln_g, conv_ln_b, conv_w_out, gdn_w_in, gdn_conv_w, gdn_a_log, gdn_dt_bias, gdn_o_norm_g, gdn_w_out, fox_w_in, fox_b_f, fox_q_norm_g, fox_k_norm_g, fox_w_out, ffn_w_up, ffn_w_dw, ffn_w_down,
         m_mix_norm_g, m_ffn_norm_g, m_conv_w_in, m_conv_b_in, m_conv_w_dw, m_conv_b_dw, m_conv_ln_g, m_conv_ln_b, m_conv_w_out, m_gdn_w_in, m_gdn_conv_w, m_gdn_a_log, m_gdn_dt_bias, m_gdn_o_norm_g, m_gdn_w_out, m_fox_w_in, m_fox_b_f, m_fox_q_norm_g, m_fox_k_norm_g, m_fox_w_out, m_ffn_w_up, m_ffn_w_dw, m_ffn_w_down,
         v_mix_norm_g, v_ffn_norm_g, v_conv_w_in, v_conv_b_in, v_conv_w_dw, v_conv_b_dw, v_conv_ln_g, v_conv_ln_b, v_conv_w_out, v_gdn_w_in, v_gdn_conv_w, v_gdn_a_log, v_gdn_dt_bias, v_gdn_o_norm_g, v_gdn_w_out, v_fox_w_in, v_fox_b_f, v_fox_q_norm_g, v_fox_k_norm_g, v_fox_w_out, v_ffn_w_up, v_ffn_w_dw, v_ffn_w_down)))

    me = 4 * lax.axis_index("x") + 2 * lax.axis_index("y") + lax.axis_index("c")
    mine = me.astype(jnp.int32).reshape(1)

    gathers, token = {}, jnp.zeros((1, 1), F32)
    for i in range(DEPTH):
        for part in (0, 1):
            mats, smalls = _part_entries(i, part)
            sent = [given[n][j].astype(BF16) for n, j, _ in mats]
            if smalls:
                sent.append(_pack_rows([given[n][j] for n, j, _ in smalls], 0))
            gathers[i, part], tok = _exchange_start(f"gather{i}{'mf'[part]}_start", sent, [False] * len(sent))
            token = token + tok[0:1, 0:1]

    def params_of(i, part, x_in):
        mats, smalls = _part_entries(i, part)
        after = token if (i, part) == (0, 0) else x_in
        lands, owns = _exchange_wait(f"gather{i}{'mf'[part]}_wait", gathers[i, part], after, me)
        whole, relaid = {}, {}
        for (n, _, k), land, own in zip(mats, lands, owns):
            if n in ROW_SHARDED:
                whole[n] = _unshard(n, _fill_own(land, own, me))
            else:
                relaid[k] = _unshard_cols(f"l{i}_{n}_unshard", n, land, own, mine)
        if smalls:
            shapes = [given[n].shape[1:] for n, _, _ in smalls]
            for (n, _, _), g in zip(smalls, _unpack_rows(_fill_own(lands[-1], owns[-1], me), shapes, 1)):
                whole[n] = _unshard(n, g)
        return {**_part_dict(i, part, whole, given), **relaid}

    grads, exchanges = [[None, None] for _ in range(DEPTH)], {}

    def on_grads(i, part, g):
        grads[i][part] = g
        mats, smalls = _part_entries(i, part)
        sent = [_reshard(n, g[k]) if n in ROW_SHARDED else
                _reshard_cols(f"l{i}_{n}_reshard", n, g[k], given[n].shape[-1]) for n, _, k in mats]
        if smalls:
            sent.append(_pack_rows([_reshard(n, _from_grad(n, g[k])) for n, _, k in smalls], 1))
        last = (i, part) == (0, 0)
        if last:
            sent.append(_pack_rep(_replicated_grads(grads)))
        exchanges[i, part], tok = _exchange_start(f"grads{i}{'mf'[part]}_start", sent,
                                                  [True] * (len(sent) - last) + [False] * last)
        tokens.append(tok)
        return tok

    tokens = []
    sq, dx = _local_step(x[0], loss_target[0], params_of, on_grads)
    loss = (0.5 / D) * lax.psum(sq[0, 0], ("x", "y", "c"))

    pieces = {n: [] for n in SHARDED}
    for i in range(DEPTH):
        for part in (0, 1):
            mats, smalls = _part_entries(i, part)
            lands, owns = _exchange_wait(f"grads{i}{'mf'[part]}_wait", exchanges[i, part], tokens[-1], me)
            if (i, part) == (0, 0):
                rep_piece = (lands[-1], owns[-1])
            for (n, _, _), land, own in zip(mats, lands, owns):
                pieces[n].append((land, own))
            if smalls:
                shapes = [given[n].shape[1:] for n, _, _ in smalls]
                for (n, _, _), land, own in zip(smalls, _unpack_rows(lands[len(mats)], shapes, 1),
                                                _unpack_rows(owns[len(mats)], shapes, 0)):
                    pieces[n].append((land, own))
    new = {}
    for n in SHARDED:
        lands = [l.reshape((N_DEV, -1, l.shape[-1])) for l, _ in pieces[n]]
        owns = [o.reshape((-1, o.shape[-1])) for _, o in pieces[n]]
        if owns[0].shape[0] % 8:
            lands = [jnp.stack(lands, axis=1).reshape((N_DEV, -1, lands[0].shape[-1]))]
            owns = [jnp.stack(owns).reshape((-1, owns[0].shape[-1]))]
        outs = _adam("adam_" + n, lands, owns, mine, _view2d(given[n]), _view2d(given["m_" + n]),
                     _view2d(given["v_" + n]))
        new[n] = [o.reshape(given[n].shape) for o in outs]
    packed = [_pack_rep({n: given[pre + n] for n in REPLICATED}) for pre in ("", "m_", "v_")]
    outs = _adam("adam_replicated", [rep_piece[0]], [rep_piece[1]], mine, *packed)
    unpacked = [_unpack_rep(o, {n: given[n].shape for n in REPLICATED}) for o in outs]
    for n in REPLICATED:
        new[n] = [u[n] for u in unpacked]
    return (loss, dx[None], *[new[n][0] for n in WEIGHTS], *[new[n][1] for n in WEIGHTS],
            *[new[n][2] for n in WEIGHTS], *[new[n][3] for n in WEIGHTS])
```

```python
import functools

import jax
import jax.numpy as jnp
import numpy as np
from jax import lax
from jax.experimental import pallas as pl
from jax.experimental.pallas import tpu as pltpu

F32 = jnp.float32
BF16 = jnp.bfloat16
HIGHEST = lax.Precision.HIGHEST
HIGH = lax.Precision.HIGH

N_DEV = 8
LANE = 128
EPS = 1e-6
DEPTH = 4
N_MIXERS = 3
HEADS = 8
DH = 128
D = HEADS * DH
D_FF = 2816
CONF_K, GDN_K, FFN_K = 31, 4, 3
GDN_CHUNK = 64
GDN_PAD = 4224
FOX_PAD = 3200
FOX_V_BLOCK = (2 * D + LANE) // DH
ADAM_LR, ADAM_B1, ADAM_B2, ADAM_EPS, ADAM_WD, ADAM_STEP = 0.001, 0.9, 0.999, 1e-08, 0.01, 10
VMEM_LIMIT = 56 * 1024 * 1024

WEIGHTS = ['mix_norm_g', 'ffn_norm_g', 'conv_w_in', 'conv_b_in', 'conv_w_dw', 'conv_b_dw', 'conv_ln_g', 'conv_ln_b',
           'conv_w_out', 'gdn_w_in', 'gdn_conv_w', 'gdn_a_log', 'gdn_dt_bias', 'gdn_o_norm_g', 'gdn_w_out', 'fox_w_in',
           'fox_b_f', 'fox_q_norm_g', 'fox_k_norm_g', 'fox_w_out', 'ffn_w_up', 'ffn_w_dw', 'ffn_w_down']
REPLICATED = ['mix_norm_g', 'ffn_norm_g', 'gdn_a_log', 'gdn_dt_bias', 'gdn_o_norm_g', 'fox_b_f', 'fox_q_norm_g',
              'fox_k_norm_g']
ROW_SHARDED = ['conv_w_out', 'gdn_w_out', 'fox_w_out', 'ffn_w_down']
MATRICES = ['conv_w_in', 'conv_w_out', 'gdn_w_in', 'gdn_w_out', 'fox_w_in', 'fox_w_out', 'ffn_w_up', 'ffn_w_down']
SHARDED = [n for n in WEIGHTS if n not in REPLICATED]


def _params(*sem):
    return pltpu.CompilerParams(dimension_semantics=sem, vmem_limit_bytes=VMEM_LIMIT)


def _tile(n, cap):
    if n <= cap:
        return n
    d = (cap // LANE) * LANE
    while d >= LANE:
        if n % d == 0:
            return d
        d -= LANE
    raise ValueError(f"no lane-aligned tile of {n} under {cap}")


def _raw_dot(a, b, ca, cb, hp):
    batch = ((0,), (0,)) if a.ndim == 3 else ((), ())
    dn = (((ca,), (cb,)), batch)
    if hp:
        return lax.dot_general(a.astype(F32), b.astype(F32), dn, precision=HIGH, preferred_element_type=F32)
    return lax.dot_general(a.astype(BF16), b.astype(BF16), dn, preferred_element_type=F32)


def _raw_nn(a, b, hp=False):
    return _raw_dot(a, b, a.ndim - 1, b.ndim - 2, hp)


def _raw_nt(a, b, hp=False):
    return _raw_dot(a, b, a.ndim - 1, b.ndim - 1, hp)


def _raw_tn(a, b, hp=False):
    return _raw_dot(a, b, a.ndim - 2, b.ndim - 2, hp)


@functools.partial(jax.custom_vjp, nondiff_argnums=(2,))
def _nn(a, b, hp):
    return _raw_nn(a, b, hp)


def _nn_fwd(a, b, hp):
    return _raw_nn(a, b, hp), (a, b)


def _nn_bwd(hp, res, g):
    a, b = res
    return _raw_nt(g, b, hp), _raw_tn(a, g, hp)


_nn.defvjp(_nn_fwd, _nn_bwd)


@functools.partial(jax.custom_vjp, nondiff_argnums=(2,))
def _nt(a, b, hp):
    return _raw_nt(a, b, hp)


def _nt_fwd(a, b, hp):
    return _raw_nt(a, b, hp), (a, b)


def _nt_bwd(hp, res, g):
    a, b = res
    return _raw_nn(g, b, hp), _raw_tn(g, a, hp)


_nt.defvjp(_nt_fwd, _nt_bwd)


CONV_TAIL = 32


def _zero_tail(x):
    return jnp.concatenate([x, jnp.zeros((CONV_TAIL, x.shape[1]), x.dtype)], axis=0)


def _row(w, k):
    r = lax.broadcasted_iota(jnp.int32, w.shape, 0)
    return jnp.sum(jnp.where(r == k, w, 0.0), axis=0, keepdims=True)


@jax.custom_vjp
def _dwconv(x, w):
    t, taps = x.shape[0], w.shape[0]
    xe = _zero_tail(x)
    y = _row(w, taps - 1) * xe
    for k in range(taps - 1):
        y = y + _row(w, k) * pltpu.roll(xe, taps - 1 - k, axis=0)
    return y[:t]


def _dwconv_fwd(x, w):
    return _dwconv(x, w), (x, w)


def _dwconv_bwd(res, dy):
    x, w = res
    t, taps = x.shape[0], w.shape[0]
    xe, dye = _zero_tail(x), _zero_tail(dy)
    r = lax.broadcasted_iota(jnp.int32, w.shape, 0)
    dx = _row(w, taps - 1) * dye
    dw = jnp.where(r == taps - 1, jnp.sum(dye * xe, axis=0, keepdims=True), 0.0)
    for k in range(taps - 1):
        up = pltpu.roll(dye, t + CONV_TAIL - (taps - 1 - k), axis=0)
        dx = dx + _row(w, k) * up
        dw = dw + jnp.where(r == k, jnp.sum(up * xe, axis=0, keepdims=True), 0.0)
    return dx[:t], dw


_dwconv.defvjp(_dwconv_fwd, _dwconv_bwd)


def _sigmoid(x):
    return 1.0 / (1.0 + jnp.exp(-x))


def _silu(x):
    return x * _sigmoid(x)


def _softplus(x):
    return jnp.maximum(x, 0.0) + jnp.log(1.0 + jnp.exp(-jnp.abs(x)))


def _head_scale(x, fn):
    tm = x.shape[0]
    x3 = x.reshape(tm, HEADS, DH)
    return (x3 * fn(jnp.sum(x3 * x3, axis=-1, keepdims=True))).reshape(tm, HEADS * DH)


def _tile_lanes(g):
    return jnp.concatenate([g] * HEADS, axis=1)


def _expand_heads(v, first):
    lane = lax.broadcasted_iota(jnp.int32, (LANE, HEADS * DH), 0)
    col = lax.broadcasted_iota(jnp.int32, (LANE, HEADS * DH), 1)
    sel = (lane == col // DH + first).astype(F32)
    return _nn(v, sel, True)


def _f_rms(x, g):
    return (x * lax.rsqrt(jnp.mean(x * x, axis=-1, keepdims=True) + EPS) * g,)


def _f_rms_res(x, g):
    return (_f_rms(x, g)[0], x)


def _f_conf_glu_conv(u, b, w):
    c = u.shape[1] // 2
    return (_dwconv((u[:, :c] + b[:, :c]) * _sigmoid(u[:, c:] + b[:, c:]), w),)


def _f_conf_ln_silu(cv, b_dw, ln_g, ln_b):
    u = cv + b_dw
    xc = u - jnp.mean(u, axis=-1, keepdims=True)
    y = xc * lax.rsqrt(jnp.mean(xc * xc, axis=-1, keepdims=True) + EPS) * ln_g + ln_b
    return (_silu(y),)


def _f_conv_silu(u, w):
    return (_silu(_dwconv(u, w)),)


def _f_gdn_gates(qk, ab, a_log, dt_bias):
    qn = _head_scale(qk[:, :D], lambda ss: lax.rsqrt(ss + EPS)) * (DH ** -0.5)
    kn = _head_scale(qk[:, D:], lambda ss: lax.rsqrt(ss + EPS))
    g = -jnp.exp(a_log) * _softplus(ab + dt_bias)
    beta = _sigmoid(ab)
    return qn, kn, _expand_heads(g, 0), _expand_heads(beta, HEADS)


def _f_gdn_prescan(q, k, v, gb, bb):
    c = GDN_CHUNK
    n = q.shape[0] // c
    r3 = lambda t: t.reshape(n, c, DH)
    q3, k3, v3, g3, b3 = r3(q), r3(k), r3(v), r3(gb), r3(bb)
    ii = lax.broadcasted_iota(jnp.int32, (n, c, c), 1)
    jj = lax.broadcasted_iota(jnp.int32, (n, c, c), 2)
    lower, strict = ii >= jj, ii > jj
    gcb = _nn(lower.astype(F32), g3, True)
    gi = gcb[:, :, :c]
    gj = jnp.swapaxes(gi, 1, 2)
    decay = jnp.where(lower, jnp.exp(jnp.where(lower, gi - gj, 0.0)), 0.0)
    kb, vb = k3 * b3, v3 * b3
    a_mat = jnp.where(strict, _nt(kb, k3, False) * decay, 0.0)
    p = -a_mat
    t_mat = (ii == jj).astype(F32) + p
    for _ in range(5):
        p = _nn(p, p, True)
        t_mat = t_mat + _nn(t_mat, p, True)
    eg = jnp.exp(gcb)
    u = _nn(t_mat, vb, False)
    w = _nn(t_mat, kb * eg, False)
    qk = jnp.where(lower, _nt(q3, k3, False) * decay, 0.0)
    qg = q3 * eg
    g_last = jnp.sum(g3, axis=1, keepdims=True)
    kd = k3 * jnp.exp(g_last - gcb)
    r2 = lambda t: t.reshape(n * c, DH)
    return r2(u), r2(w), r2(qg), qk, r2(kd), jnp.exp(g_last)


def _f_gdn_post(o, z, o_g):
    on = _head_scale(o, lambda ss: lax.rsqrt(ss / DH + EPS)) * _tile_lanes(o_g)
    return (on * _silu(z),)


def _f_fox_pre(qkf, q_g, k_g, b_f):
    qn = _head_scale(qkf[:, :D], lambda ss: lax.rsqrt(ss / DH + EPS)) * _tile_lanes(q_g)
    kn = _head_scale(qkf[:, D:2 * D], lambda ss: lax.rsqrt(ss / DH + EPS)) * _tile_lanes(k_g)
    return qn, kn, -_softplus(-(qkf[:, 2 * D:] + b_f))


def _f_ffn_mid(u, w):
    c = u.shape[1] // 2
    return (_silu(_dwconv(u[:, :c], w[:, :c])) * _dwconv(u[:, c:], w[:, c:]),)


def _seg_fwd(name, f, grid, ins, in_specs, out_shapes, out_specs):
    n_in = len(ins)

    def body(*refs):
        outs = f(*[r[...].astype(F32) for r in refs[:n_in]])
        for r, o in zip(refs[n_in:], outs):
            r[...] = o.astype(r.dtype)

    return pl.pallas_call(body, grid=grid, in_specs=in_specs, out_specs=out_specs, out_shape=out_shapes, name=name,
                          compiler_params=_params(*(["parallel"] * len(grid))))(*ins)


def _seg_bwd(name, f, grid, ins, in_specs, douts, dout_specs, want, into=None):
    into = into or {}
    n_in, n_dy = len(ins), len(douts)
    diff = [i for i, w in enumerate(want) if w is not None]
    kept = [i for i in diff if i in into]
    out_shapes = [jax.ShapeDtypeStruct(ins[i].shape, into[i].dtype if i in into else F32 if want[i] == "acc" else want[i])
                  for i in diff]
    out_specs = [in_specs[i] for i in diff]
    acc_axis = len(grid) - 1

    def body(*refs):
        vals = [r[...].astype(F32) for r in refs[:n_in]]
        dys = [r[...].astype(F32) for r in refs[n_in:n_in + n_dy]]
        out_refs = refs[n_in + n_dy + len(kept):]

        def g(*dv):
            full = list(vals)
            for i, v in zip(diff, dv):
                full[i] = v
            return f(*full)

        _, vjp = jax.vjp(g, *[vals[i] for i in diff])
        grads = vjp(tuple(dys))
        first = pl.program_id(acc_axis) == 0
        for i, r, gr in zip(diff, out_refs, grads):
            if want[i] == "acc":
                @pl.when(first)
                def _(r=r, gr=gr):
                    r[...] = gr

                @pl.when(jnp.logical_not(first))
                def _(r=r, gr=gr):
                    r[...] += gr
            else:
                r[...] = gr.astype(r.dtype)

    sem = ["parallel"] * (len(grid) - 1) + ["arbitrary"]
    untouched = [pl.BlockSpec(memory_space=pl.ANY)] * len(kept)
    aliases = {n_in + n_dy + e: diff.index(i) for e, i in enumerate(kept)}
    return pl.pallas_call(body, grid=grid, in_specs=list(in_specs) + list(dout_specs) + untouched, out_specs=out_specs,
                          out_shape=out_shapes, input_output_aliases=aliases, name=name,
                          compiler_params=_params(*sem))(*ins, *douts, *[into[i] for i in kept])


def _rows(tm, width, col=0):
    return pl.BlockSpec((tm, width), lambda i, col=col: (i, col))


def _const(shape):
    return pl.BlockSpec(shape, lambda i: (0,) * len(shape))


def _cols(t, tc, off=0):
    return pl.BlockSpec((t, tc), lambda j, off=off: (0, j + off))


def _grid2(specs):
    return [pl.BlockSpec(s.block_shape, lambda j, i, f=s.index_map: f(j)) for s in specs]


def _mm(name, a, b, *, ta=False, tb=False, res=None, out_dtype=F32, after=None):
    k_dim, m = (a.shape[0], a.shape[1]) if ta else (a.shape[1], a.shape[0])
    n = b.shape[0] if tb else b.shape[1]
    tm, tn, tk = _tile(m, 1408), _tile(n, 1408), _tile(k_dim, 1408)
    nk = k_dim // tk
    grid = (m // tm, n // tn, nk)
    a_spec = pl.BlockSpec((tk, tm), lambda i, j, k: (k, i)) if ta else pl.BlockSpec((tm, tk), lambda i, j, k: (i, k))
    b_spec = pl.BlockSpec((tn, tk), lambda i, j, k: (j, k)) if tb else pl.BlockSpec((tk, tn), lambda i, j, k: (k, j))
    o_spec = pl.BlockSpec((tm, tn), lambda i, j, k: (i, j))
    dn = (((0 if ta else 1,), (1 if tb else 0,)), ((), ()))
    has_res = res is not None

    def body(*refs):
        a_ref, b_ref = refs[0], refs[1]
        res_ref = refs[2] if has_res else None
        o_ref = refs[n_in]
        p = lax.dot_general(a_ref[...].astype(BF16), b_ref[...].astype(BF16), dn, preferred_element_type=F32)

        def write(acc):
            if has_res:
                acc = acc + res_ref[...]
            if after is not None:
                acc = acc + refs[n_in - 1][0:1, 0:1]
            o_ref[...] = acc.astype(o_ref.dtype)

        if nk == 1:
            write(p)
        else:
            acc_ref = refs[-1]
            k = pl.program_id(2)

            @pl.when(k == 0)
            def _():
                acc_ref[...] = p

            @pl.when(k > 0)
            def _():
                acc_ref[...] += p

            @pl.when(k == nk - 1)
            def _():
                write(acc_ref[...])

    ins, specs = [a, b], [a_spec, b_spec]
    if has_res:
        ins.append(res)
        specs.append(o_spec)
    if after is not None:
        ins.append(after)
        specs.append(pl.BlockSpec(after.shape, lambda i, j, k: (0, 0)))
    n_in = len(ins)
    scratch = [pltpu.VMEM((tm, tn), F32)] if nk > 1 else []
    return pl.pallas_call(body, grid=grid, in_specs=specs, out_specs=o_spec, scratch_shapes=scratch,
                          out_shape=jax.ShapeDtypeStruct((m, n), out_dtype), name=name,
                          compiler_params=_params("parallel", "parallel", "arbitrary"))(*ins)


SCAN_HEADS = 2


def _head_specs(t):
    n = t // GDN_CHUNK
    col = pl.BlockSpec((t, SCAN_HEADS * DH), lambda h: (0, h))
    qk = pl.BlockSpec((SCAN_HEADS, n, GDN_CHUNK, GDN_CHUNK), lambda h: (h, 0, 0, 0))
    gl = pl.BlockSpec((SCAN_HEADS, n, 1, DH), lambda h: (h, 0, 0, 0))
    st = pl.BlockSpec((SCAN_HEADS, n, DH, DH), lambda h: (h, 0, 0, 0))
    return n, col, qk, gl, st


def _gdn_scan_fwd(name, u, w, qg, qk, kd, gl):
    t = u.shape[0]
    n, col, qk_spec, gl_spec, st_spec = _head_specs(t)
    c = GDN_CHUNK

    def body(u_ref, w_ref, qg_ref, qk_ref, kd_ref, gl_ref, o_ref, s_ref):
        def step(i, states):
            rows = pl.ds(pl.multiple_of(i * c, c), c)
            out = []
            for h, s in enumerate(states):
                cols = slice(h * DH, (h + 1) * DH)
                s_ref[h, i] = s
                vn = u_ref[rows, cols] - _raw_nn(w_ref[rows, cols], s)
                o_ref[rows, cols] = _raw_nn(qg_ref[rows, cols], s) + _raw_nn(qk_ref[h, i], vn)
                out.append(s * gl_ref[h, i] + _raw_tn(kd_ref[rows, cols], vn))
            return tuple(out)

        lax.fori_loop(0, n, step, (jnp.zeros((DH, DH), F32),) * SCAN_HEADS)

    return pl.pallas_call(
        body, grid=(HEADS // SCAN_HEADS,), in_specs=[col, col, col, qk_spec, col, gl_spec], out_specs=[col, st_spec],
        out_shape=[jax.ShapeDtypeStruct((t, D), F32), jax.ShapeDtypeStruct((HEADS, n, DH, DH), F32)], name=name,
        compiler_params=_params("parallel"))(u, w, qg, qk, kd, gl)


def _gdn_scan_bwd(name, u, w, qg, qk, kd, gl, states, do):
    t = u.shape[0]
    n, col, qk_spec, gl_spec, st_spec = _head_specs(t)
    c = GDN_CHUNK

    def body(u_ref, w_ref, qg_ref, qk_ref, kd_ref, gl_ref, s_ref, do_ref,
             du_ref, dw_ref, dqg_ref, dqk_ref, dkd_ref, dgl_ref):
        def step(r, dstates):
            i = n - 1 - r
            rows = pl.ds(pl.multiple_of(i * c, c), c)
            out = []
            for h, ds in enumerate(dstates):
                cols = slice(h * DH, (h + 1) * DH)
                s, do_c, w_c = s_ref[h, i], do_ref[rows, cols], w_ref[rows, cols]
                vn = u_ref[rows, cols] - _raw_nn(w_c, s)
                dvn = _raw_tn(qk_ref[h, i], do_c) + _raw_nn(kd_ref[rows, cols], ds)
                du_ref[rows, cols] = dvn
                dw_ref[rows, cols] = -_raw_nt(dvn, s)
                dqg_ref[rows, cols] = _raw_nt(do_c, s)
                dqk_ref[h, i] = _raw_nt(do_c, vn)
                dkd_ref[rows, cols] = _raw_nt(vn, ds)
                dgl_ref[h, i] = jnp.sum(ds * s, axis=0, keepdims=True)
                out.append(_raw_tn(qg_ref[rows, cols], do_c) + ds * gl_ref[h, i] - _raw_tn(w_c, dvn))
            return tuple(out)

        lax.fori_loop(0, n, step, (jnp.zeros((DH, DH), F32),) * SCAN_HEADS)

    big = jax.ShapeDtypeStruct((t, D), F32)
    return pl.pallas_call(
        body, grid=(HEADS // SCAN_HEADS,), in_specs=[col, col, col, qk_spec, col, gl_spec, st_spec, col],
        out_specs=[col, col, col, qk_spec, col, gl_spec],
        out_shape=[big, big, big, jax.ShapeDtypeStruct(qk.shape, F32), big, jax.ShapeDtypeStruct(gl.shape, F32)],
        name=name, compiler_params=_params("parallel"))(u, w, qg, qk, kd, gl, states, do)


def _cumsum_rows(name, parts, reverse):
    t = parts[0].shape[0]
    blk = min(t, 256)
    nb = t // blk
    n_in = len(parts)

    def body(*refs):
        o_ref = refs[n_in]
        ii = lax.broadcasted_iota(jnp.int32, (blk, blk), 0)
        jj = lax.broadcasted_iota(jnp.int32, (blk, blk), 1)
        tri = ((ii <= jj) if reverse else (ii >= jj)).astype(F32)
        carry = jnp.zeros((1, LANE), F32)
        for b in (range(nb - 1, -1, -1) if reverse else range(nb)):
            rows = pl.ds(b * blk, blk)
            x = refs[0][rows, :]
            for r in refs[1:n_in]:
                x = x + r[rows, :]
            o_ref[rows, :] = lax.dot_general(tri, x, (((1,), (0,)), ((), ())), precision=HIGHEST,
                                             preferred_element_type=F32) + carry
            carry = carry + jnp.sum(x, axis=0, keepdims=True)

    return pl.pallas_call(body, out_shape=jax.ShapeDtypeStruct((t, LANE), F32), name=name,
                          compiler_params=_params())(*parts)


def _fox_blocks(t):
    blk = 1024 if t % 1024 == 0 and t >= 2048 else 256 if t % 256 == 0 and t >= 1024 else 128
    return blk, t // blk


def _fox_logits(q, k, cq, ck, diagonal):
    s = _raw_nt(q, k) * (DH ** -0.5) + cq - ck
    if not diagonal:
        return s
    rows = lax.broadcasted_iota(jnp.int32, s.shape, 0)
    cols = lax.broadcasted_iota(jnp.int32, s.shape, 1)
    return jnp.where(cols <= rows, s, -jnp.inf)


def _fox_fwd(name, qn, kn, proj, c_col, c_row):
    t = qn.shape[0]
    blk, nb = _fox_blocks(t)
    voff = FOX_V_BLOCK

    def body(q_ref, k_ref, v_ref, cc_ref, cr_ref, o_ref, lse_ref):
        i = pl.program_id(1)
        q, cq = q_ref[...], cc_ref[...]

        def step(j, carry, diagonal=False):
            m, l, acc = carry
            rows = pl.ds(pl.multiple_of(j * blk, blk), blk)
            s = _fox_logits(q, k_ref[rows, :], cq, cr_ref[j], diagonal)
            m_new = jnp.maximum(m, jnp.max(s, axis=1, keepdims=True))
            p = jnp.exp(s - m_new)
            alpha = jnp.exp(m - m_new)
            return m_new, alpha * l + jnp.sum(p, axis=1, keepdims=True), alpha * acc + _raw_nn(p, v_ref[rows, :])

        init = (jnp.full((blk, 1), -1e30, F32), jnp.zeros((blk, 1), F32), jnp.zeros((blk, DH), F32))
        m, l, acc = step(i, lax.fori_loop(0, i, step, init), True)
        o_ref[...] = acc / l
        lse_ref[...] = m + jnp.log(l)

    tile = pl.BlockSpec((blk, DH), lambda h, i: (i, h))
    colv = pl.BlockSpec((None, blk, 1), lambda h, i: (h, i, 0))
    return pl.pallas_call(
        body, grid=(HEADS, nb),
        in_specs=[tile, pl.BlockSpec((t, DH), lambda h, i: (0, h)), pl.BlockSpec((t, DH), lambda h, i: (0, voff + h)),
                  colv, pl.BlockSpec((None, nb, 1, blk), lambda h, i: (h, 0, 0, 0))],
        out_specs=[tile, colv],
        out_shape=[jax.ShapeDtypeStruct((t, D), F32), jax.ShapeDtypeStruct((HEADS, t, 1), F32)], name=name,
        compiler_params=_params("parallel", "parallel"))(qn, kn, proj, c_col, c_row)


def _fox_dq(name, qn, kn, proj, c_col, c_row, o, lse, do):
    t = qn.shape[0]
    blk, nb = _fox_blocks(t)
    voff = FOX_V_BLOCK

    def body(q_ref, k_ref, v_ref, cc_ref, cr_ref, o_ref, lse_ref, do_ref, dq_ref, dcc_ref, dl_ref):
        i = pl.program_id(1)
        q, cq, do_b, lse_b = q_ref[...], cc_ref[...], do_ref[...], lse_ref[...]
        delta = jnp.sum(do_b * o_ref[...], axis=1, keepdims=True)

        def step(j, carry, diagonal=False):
            dq, dcc = carry
            rows = pl.ds(pl.multiple_of(j * blk, blk), blk)
            k = k_ref[rows, :]
            p = jnp.exp(_fox_logits(q, k, cq, cr_ref[j], diagonal) - lse_b)
            ds = p * (_raw_nt(do_b, v_ref[rows, :]) - delta)
            return dq + _raw_nn(ds, k), dcc + jnp.sum(ds, axis=1, keepdims=True)

        init = (jnp.zeros((blk, DH), F32), jnp.zeros((blk, 1), F32))
        dq, dcc = step(i, lax.fori_loop(0, i, step, init), True)
        dq_ref[...] = dq * (DH ** -0.5)
        dcc_ref[...] = dcc
        dl_ref[...] = delta

    tile = pl.BlockSpec((blk, DH), lambda h, i: (i, h))
    colv = pl.BlockSpec((None, blk, 1), lambda h, i: (h, i, 0))
    vec = jax.ShapeDtypeStruct((HEADS, t, 1), F32)
    return pl.pallas_call(
        body, grid=(HEADS, nb),
        in_specs=[tile, pl.BlockSpec((t, DH), lambda h, i: (0, h)), pl.BlockSpec((t, DH), lambda h, i: (0, voff + h)),
                  colv, pl.BlockSpec((None, nb, 1, blk), lambda h, i: (h, 0, 0, 0)), tile, colv, tile],
        out_specs=[tile, colv, colv],
        out_shape=[jax.ShapeDtypeStruct((t, D), F32), vec, vec], name=name,
        compiler_params=_params("parallel", "parallel"))(qn, kn, proj, c_col, c_row, o, lse, do)


def _fox_dkv(name, qn, kn, proj, c_col, c_row, lse, delta, do):
    t = qn.shape[0]
    blk, nb = _fox_blocks(t)
    voff = FOX_V_BLOCK

    def body(q_ref, k_ref, v_ref, cc_ref, cr_ref, lse_ref, dl_ref, do_ref, dk_ref, dv_ref, dcr_ref):
        j = pl.program_id(1)
        k, v, ck = k_ref[...], v_ref[...], cr_ref[...]

        def step(i, carry, diagonal=False):
            dk, dv, dcr = carry
            rows = pl.ds(pl.multiple_of(i * blk, blk), blk)
            q, do_b = q_ref[rows, :], do_ref[rows, :]
            p = jnp.exp(_fox_logits(q, k, cc_ref[rows, :], ck, diagonal) - lse_ref[rows, :])
            ds = p * (_raw_nt(do_b, v) - dl_ref[rows, :])
            return dk + _raw_tn(ds, q), dv + _raw_tn(p, do_b), dcr - jnp.sum(ds, axis=0, keepdims=True)

        init = (jnp.zeros((blk, DH), F32), jnp.zeros((blk, DH), F32), jnp.zeros((1, blk), F32))
        dk, dv, dcr = lax.fori_loop(j + 1, nb, step, step(j, init, True))
        dk_ref[...] = dk * (DH ** -0.5)
        dv_ref[...] = dv.astype(dv_ref.dtype)
        dcr_ref[...] = dcr

    full = pl.BlockSpec((t, DH), lambda h, j: (0, h))
    colf = pl.BlockSpec((None, t, 1), lambda h, j: (h, 0, 0))
    tile = pl.BlockSpec((blk, DH), lambda h, j: (j, h))
    rowv = pl.BlockSpec((None, None, 1, blk), lambda h, j: (h, j, 0, 0))
    tile_v = pl.BlockSpec((blk, DH), lambda h, j: (j, voff + h))
    big = jax.ShapeDtypeStruct((t, D), F32)
    return pl.pallas_call(
        body, grid=(HEADS, nb),
        in_specs=[full, tile, tile_v, colf, rowv, colf, colf, full],
        out_specs=[tile, tile_v, rowv],
        out_shape=[big, jax.ShapeDtypeStruct(proj.shape, BF16), jax.ShapeDtypeStruct((HEADS, nb, 1, blk), F32)], name=name,
        compiler_params=_params("parallel", "parallel"))(qn, kn, proj, c_col, c_row, lse, delta, do)


def _loss_head(y, target):
    t = y.shape[0]
    tm = min(t, 512)

    def body(y_ref, t_ref, dy_ref, sum_ref):
        err = y_ref[...] - t_ref[...]
        dy_ref[...] = err * (1.0 / D)
        part = jnp.sum(jnp.sum(err * err, axis=1, keepdims=True), axis=0, keepdims=True)

        @pl.when(pl.program_id(0) == 0)
        def _():
            sum_ref[...] = jnp.zeros_like(sum_ref)

        sum_ref[...] += jnp.broadcast_to(part, sum_ref.shape)

    return pl.pallas_call(
        body, grid=(t // tm,), in_specs=[_rows(tm, D), _rows(tm, D)], out_specs=[_rows(tm, D), _const((1, LANE))],
        out_shape=[jax.ShapeDtypeStruct((t, D), F32), jax.ShapeDtypeStruct((1, LANE), F32)], name="loss_head",
        compiler_params=_params("arbitrary"))(y, target)


ADAM_BLOCK_BYTES = 3 * 1024 * 1024


def _adam(name, lands, owns, mine, w, m, v):
    layers = len(lands)
    r, c = owns[0].shape
    block_bytes = lambda rows: N_DEV * rows * c * lands[0].dtype.itemsize
    tr = r
    if block_bytes(r) > ADAM_BLOCK_BYTES:
        tr = max(d for d in range(16, r, 16) if r % d == 0 and (block_bytes(d) <= ADAM_BLOCK_BYTES or d == 16))
    nr = r // tr

    def body(*refs):
        me_ref, land_refs, own_refs = refs[0], refs[1:1 + layers], refs[1 + layers:1 + 2 * layers]
        w_ref, m_ref, v_ref, g_ref, d_ref, nm_ref, nv_ref = refs[1 + 2 * layers:]
        layer = pl.program_id(0)
        for l in range(layers):
            @pl.when(layer == l)
            def _(l=l):
                g = jnp.zeros((tr, c), F32)
                for s in range(N_DEV):
                    g = g + jnp.where(me_ref[0] == s, own_refs[l][...], land_refs[l][s]).astype(F32)
                nm = ADAM_B1 * m_ref[...] + (1.0 - ADAM_B1) * g
                nv = ADAM_B2 * v_ref[...] + (1.0 - ADAM_B2) * (g * g)
                m_hat = nm / (1.0 - ADAM_B1 ** ADAM_STEP)
                v_hat = nv / (1.0 - ADAM_B2 ** ADAM_STEP)
                g_ref[...] = g
                d_ref[...] = -ADAM_LR * (m_hat / (jnp.sqrt(v_hat) + ADAM_EPS) + ADAM_WD * w_ref[...])
                nm_ref[...] = nm
                nv_ref[...] = nv

    at = lambda l: (lambda layer, i: jnp.where(layer == l, i, 0))
    land_specs = [pl.BlockSpec((N_DEV, tr, c), lambda layer, i, f=at(l): (0, f(layer, i), 0)) for l in range(layers)]
    own_specs = [pl.BlockSpec((tr, c), lambda layer, i, f=at(l): (f(layer, i), 0)) for l in range(layers)]
    blk = pl.BlockSpec((tr, c), lambda layer, i: (layer * nr + i, 0))
    out = jax.ShapeDtypeStruct(w.shape, F32)
    return pl.pallas_call(
        body, grid=(layers, nr),
        in_specs=[pl.BlockSpec(memory_space=pltpu.SMEM)] + land_specs + own_specs + [blk, blk, blk],
        out_specs=[blk] * 4, out_shape=[out] * 4, name=name,
        compiler_params=_params("arbitrary", "arbitrary"))(mine, *lands, *owns, w, m, v)


_HBM = pl.BlockSpec(memory_space=pltpu.HBM)
_SEM = pl.BlockSpec(memory_space=pltpu.SEMAPHORE)
_EFFECT = pltpu.SideEffectType.DATAFLOW_SIDE_EFFECTING


def _each_peer(x, y, c):
    flip = lambda v, bit: 1 - v if bit else v
    for p in range(1, N_DEV):
        px, py, pc = flip(x, p & 4), flip(y, p & 2), flip(c, p & 1)
        yield p, (px, py, pc), 4 * px + 2 * py + pc


def _sem(a, p):
    return a * (N_DEV - 1) + p - 1


def _exchange_start(name, arrays, scatter):
    n = len(arrays)
    lands = [lax.empty((N_DEV,) + (a.shape[1:] if sc else a.shape), a.dtype) for a, sc in zip(arrays, scatter)]

    def body(*refs):
        in_refs, land_refs = refs[:n], refs[n:2 * n]
        send_sems, recv_sems, token = refs[2 * n], refs[2 * n + 1], refs[-1]
        x, y, c = lax.axis_index("x"), lax.axis_index("y"), lax.axis_index("c")
        me = 4 * x + 2 * y + c
        for p, coords, peer in _each_peer(x, y, c):
            for a in range(n):
                pltpu.make_async_remote_copy(
                    src_ref=in_refs[a].at[peer] if scatter[a] else in_refs[a], dst_ref=land_refs[a].at[me],
                    send_sem=send_sems.at[_sem(a, p)], recv_sem=recv_sems.at[_sem(a, p)], device_id=coords,
                    device_id_type=pl.DeviceIdType.MESH).start()
        token[...] = jnp.zeros_like(token)

    sems = pltpu.SemaphoreType.DMA((n * (N_DEV - 1),))
    hbm = lambda a: pltpu.HBM(a.shape, a.dtype)
    out = pl.pallas_call(
        body, name=name,
        out_shape=(sems, sems, *[hbm(a) for a in arrays], *[hbm(l) for l in lands],
                   jax.ShapeDtypeStruct((8, LANE), F32)),
        in_specs=[_HBM] * (2 * n), out_specs=(_SEM, _SEM, *[_HBM] * (2 * n), pl.BlockSpec(memory_space=pltpu.VMEM)),
        input_output_aliases={i: 2 + i for i in range(2 * n)},
        compiler_params=pltpu.CompilerParams(has_side_effects=_EFFECT),
    )(*[pltpu.with_memory_space_constraint(a, pltpu.HBM) for a in arrays],
      *[pltpu.with_memory_space_constraint(l, pltpu.HBM) for l in lands])
    return (out[0], out[1], list(out[2:2 + n]), list(out[2 + n:2 + 2 * n]), scatter), out[-1]


def _exchange_wait(name, started, after, me):
    send_sems, recv_sems, sent, lands, scatter = started
    n = len(sent)

    def body(*refs):
        in_refs, land_refs = refs[:n], refs[n:2 * n]
        send_sems, recv_sems = refs[2 * n], refs[2 * n + 1]
        x, y, c = lax.axis_index("x"), lax.axis_index("y"), lax.axis_index("c")
        for p, coords, peer in _each_peer(x, y, c):
            for a in range(n):
                cp = pltpu.make_async_remote_copy(
                    src_ref=in_refs[a].at[peer] if scatter[a] else in_refs[a], dst_ref=land_refs[a].at[peer],
                    send_sem=send_sems.at[_sem(a, p)], recv_sem=recv_sems.at[_sem(a, p)], device_id=coords,
                    device_id_type=pl.DeviceIdType.MESH)
                cp.wait_send()
                cp.wait_recv()

    hbm = lambda a: pltpu.HBM(a.shape, a.dtype)
    out = pl.pallas_call(
        body, name=name, out_shape=(*[hbm(a) for a in sent], *[hbm(l) for l in lands]),
        in_specs=[_HBM] * (2 * n) + [_SEM, _SEM, pl.BlockSpec(memory_space=pl.ANY)], out_specs=[_HBM] * (2 * n),
        input_output_aliases={i: i for i in range(2 * n)},
        compiler_params=pltpu.CompilerParams(has_side_effects=_EFFECT),
    )(*sent, *lands, send_sems, recv_sems, after)
    owns = [lax.dynamic_index_in_dim(out[a], me, 0, keepdims=False) if scatter[a] else out[a] for a in range(n)]
    return list(out[n:]), owns


def _fill_own(land, own, me):
    slot = lax.broadcasted_iota(jnp.int32, (N_DEV,) + (1,) * own.ndim, 0)
    return jnp.where(slot == me, own[None], land)


def _rms_fwd(name, x, g):
    t = x.shape[0]
    tm = min(t, 512)
    return _seg_fwd(name, _f_rms, (t // tm,), [x, g], [_rows(tm, D), _const((1, D))],
                    [jax.ShapeDtypeStruct((t, D), BF16)], [_rows(tm, D)])[0]


def _rms_bwd(name, x, g, dh, dres):
    t = x.shape[0]
    tm = min(t, 512)
    return _seg_bwd(name, _f_rms_res, (t // tm,), [x, g], [_rows(tm, D), _const((1, D))],
                    [dh, dres], [_rows(tm, D), _rows(tm, D)], [F32, "acc"])


FFN_TC = 256
CONF_TC = 128


def _ffn_specs(t):
    nf = D_FF // FFN_TC
    return nf, [_cols(t, 2 * FFN_TC), _cols(FFN_K, 2 * FFN_TC)], [_cols(t, FFN_TC)]


def _ffn_fwd(tag, x, p):
    t = x.shape[0]
    nf, in_specs, out_specs = _ffn_specs(t)
    h = _rms_fwd(tag + "_rms", x, p["g"])
    u0 = _mm(tag + "_up", h, p["w_up"])
    act = _seg_fwd(tag + "_mid", _f_ffn_mid, (nf,), [u0, p["w_dw"]], in_specs,
                   [jax.ShapeDtypeStruct((t, D_FF), BF16)], out_specs)[0]
    if callable(p["w_down"]):
        p["w_down"] = p["w_down"](act)
    out = _mm(tag + "_down", act, p["w_down"], res=x)
    return out, (x, h, u0, act)


def _ffn_bwd(tag, saved, p, dout, after=None):
    x, h, u0, act = saved
    t = x.shape[0]
    nf, in_specs, out_specs = _ffn_specs(t)
    g = {"w_down": _mm(tag + "_dwdown", act, dout, ta=True, out_dtype=BF16)}
    dact = _mm(tag + "_dact", dout, p["w_down"], tb=True, out_dtype=BF16, after=after)
    du0, g["w_dw"] = _seg_bwd(tag + "_dmid", _f_ffn_mid, (nf, 1), [u0, p["w_dw"]], _grid2(in_specs), [dact],
                              _grid2(out_specs), [BF16, "acc"])
    g["w_up"] = _mm(tag + "_dwup", h, du0, ta=True, out_dtype=BF16)
    dh = _mm(tag + "_dh", du0, p["w_up"], tb=True, out_dtype=BF16)
    dx, g["g"] = _rms_bwd(tag + "_drms", x, p["g"], dh, dout)
    return dx, g


def _conf_specs(t):
    tm = min(t, 512)
    nc = D // CONF_TC
    vec = _const((1, D))
    glu_in = [_cols(t, 2 * CONF_TC), _cols(1, 2 * CONF_TC), _cols(CONF_K, CONF_TC)]
    return tm, nc, glu_in, [_cols(t, CONF_TC)], [_rows(tm, D), vec, vec, vec]


def _conf_fwd(tag, x, p):
    t = x.shape[0]
    tm, nc, glu_in, glu_out, ln_in = _conf_specs(t)
    h = _rms_fwd(tag + "_rms", x, p["g"])
    u = _mm(tag + "_in", h, p["w_in"])
    cv = _seg_fwd(tag + "_gluconv", _f_conf_glu_conv, (nc,), [u, p["b_in"], p["w_dw"]], glu_in,
                  [jax.ShapeDtypeStruct((t, D), F32)], glu_out)[0]
    act = _seg_fwd(tag + "_lnsilu", _f_conf_ln_silu, (t // tm,), [cv, p["b_dw"], p["ln_g"], p["ln_b"]], ln_in,
                   [jax.ShapeDtypeStruct((t, D), BF16)], [_rows(tm, D)])[0]
    out = _mm(tag + "_out", act, p["w_out"], res=x)
    return out, (x, h, u, cv, act)


def _conf_bwd(tag, saved, p, dout, after=None):
    x, h, u, cv, act = saved
    t = x.shape[0]
    tm, nc, glu_in, glu_out, ln_in = _conf_specs(t)
    g = {"w_out": _mm(tag + "_dwout", act, dout, ta=True, out_dtype=BF16)}
    dact = _mm(tag + "_dact", dout, p["w_out"], tb=True, out_dtype=BF16, after=after)
    dcv, g["b_dw"], g["ln_g"], g["ln_b"] = _seg_bwd(
        tag + "_dlnsilu", _f_conf_ln_silu, (t // tm,), [cv, p["b_dw"], p["ln_g"], p["ln_b"]], ln_in, [dact],
        [_rows(tm, D)], [F32, "acc", "acc", "acc"])
    du, g["b_in"], g["w_dw"] = _seg_bwd(tag + "_dgluconv", _f_conf_glu_conv, (nc, 1), [u, p["b_in"], p["w_dw"]],
                                        _grid2(glu_in), [dcv], _grid2(glu_out), [BF16, "acc", "acc"])
    g["w_in"] = _mm(tag + "_dwin", h, du, ta=True, out_dtype=BF16)
    dh = _mm(tag + "_dh", du, p["w_in"], tb=True, out_dtype=BF16)
    dx, g["g"] = _rms_bwd(tag + "_drms", x, p["g"], dh, dout)
    return dx, g


def _gdn_specs(t):
    tc, tm, rows = 256, min(t, 256), min(t, 1024)
    nq = 3 * D // tc
    conv = ([_cols(t, tc), _cols(GDN_K, tc)], [_cols(t, tc)])
    gate_in = [_rows(tm, 2 * D, 0), _rows(tm, LANE, 4 * D // LANE), _const((1, LANE)), _const((1, LANE))]
    gate_out = [_rows(tm, D)] * 4
    head = pl.BlockSpec((rows, DH), lambda h, i: (i, h))
    headv = pl.BlockSpec((rows, DH), lambda h, i: (i, 2 * HEADS + h))
    nch = rows // GDN_CHUNK
    pre_in = [head, head, headv, head, head]
    pre_out = [head, head, head, pl.BlockSpec((None, nch, GDN_CHUNK, GDN_CHUNK), lambda h, i: (h, i, 0, 0)), head,
               pl.BlockSpec((None, nch, 1, DH), lambda h, i: (h, i, 0, 0))]
    post_in = [_rows(tm, D), _rows(tm, D, 3), _const((1, DH))]
    return tm, rows, nq, conv, gate_in, gate_out, pre_in, pre_out, post_in


def _gdn_fwd(tag, x, p):
    t = x.shape[0]
    tm, rows, nq, conv, gate_in, gate_out, pre_in, pre_out, post_in = _gdn_specs(t)
    n = t // GDN_CHUNK
    big = jax.ShapeDtypeStruct((t, D), F32)
    h = _rms_fwd(tag + "_rms", x, p["g"])
    proj = _mm(tag + "_in", h, p["w_in"])
    qkv = _seg_fwd(tag + "_conv", _f_conv_silu, (nq,), [proj, p["conv_w"]], conv[0],
                   [jax.ShapeDtypeStruct((t, 3 * D), F32)], conv[1])[0]
    gate_ins = [qkv, proj, p["a_log"], p["dt_bias"]]
    qn, kn, gb, bb = _seg_fwd(tag + "_gates", _f_gdn_gates, (t // tm,), gate_ins, gate_in, [big] * 4, gate_out)
    pre_ins = [qn, kn, qkv, gb, bb]
    pre_shapes = [big, big, big, jax.ShapeDtypeStruct((HEADS, n, GDN_CHUNK, GDN_CHUNK), F32), big,
                  jax.ShapeDtypeStruct((HEADS, n, 1, DH), F32)]
    pre = _seg_fwd(tag + "_prescan", _f_gdn_prescan, (HEADS, t // rows), pre_ins, pre_in, pre_shapes, pre_out)
    o, states = _gdn_scan_fwd(tag + "_scan", *pre)
    post_ins = [o, proj, p["o_g"]]
    act = _seg_fwd(tag + "_post", _f_gdn_post, (t // tm,), post_ins, post_in, [jax.ShapeDtypeStruct((t, D), BF16)],
                   [_rows(tm, D)])[0]
    out = _mm(tag + "_out", act, p["w_out"], res=x)
    return out, (x, h, proj, gate_ins, pre_ins, pre, states, post_ins, act)


def _gdn_bwd(tag, saved, p, dout, after=None):
    x, h, proj, gate_ins, pre_ins, pre, states, post_ins, act = saved
    t = x.shape[0]
    tm, rows, nq, conv, gate_in, gate_out, pre_in, pre_out, post_in = _gdn_specs(t)
    g = {"w_out": _mm(tag + "_dwout", act, dout, ta=True, out_dtype=BF16)}
    dact = _mm(tag + "_dact", dout, p["w_out"], tb=True, out_dtype=BF16, after=after)
    do, dproj, g["o_g"] = _seg_bwd(tag + "_dpost", _f_gdn_post, (t // tm,), post_ins, post_in, [dact],
                                   [_rows(tm, D)], [F32, BF16, "acc"])
    dpre = _gdn_scan_bwd(tag + "_dscan", *pre, states, do)
    dqn, dkn, dqkv, dgb, dbb = _seg_bwd(tag + "_dprescan", _f_gdn_prescan, (HEADS, t // rows), pre_ins, pre_in,
                                        dpre, pre_out, [F32] * 5)
    dqkv, dproj, g["a_log"], g["dt_bias"] = _seg_bwd(
        tag + "_dgates", _f_gdn_gates, (t // tm,), gate_ins, gate_in, [dqn, dkn, dgb, dbb], gate_out,
        [F32, BF16, "acc", "acc"], into={0: dqkv, 1: dproj})
    dproj, g["conv_w"] = _seg_bwd(tag + "_dconv", _f_conv_silu, (nq, 1), [proj, p["conv_w"]], _grid2(conv[0]),
                                  [dqkv], _grid2(conv[1]), [BF16, "acc"], into={0: dproj})
    g["w_in"] = _mm(tag + "_dwin", h, dproj, ta=True, out_dtype=BF16)
    dh = _mm(tag + "_dh", dproj, p["w_in"], tb=True, out_dtype=BF16)
    dx, g["g"] = _rms_bwd(tag + "_drms", x, p["g"], dh, dout)
    return dx, g


def _fox_specs(t):
    tm = min(t, 512)
    vec = _const((1, LANE))
    pre_in = [_rows(tm, 2 * D + LANE, 0), vec, vec, vec]
    pre_out = [_rows(tm, D), _rows(tm, D), _rows(tm, LANE)]
    return tm, pre_in, pre_out


def _per_head(c):
    return jnp.transpose(c[:, :HEADS])


def _per_lane(ch):
    return jnp.pad(jnp.transpose(ch), ((0, 0), (0, LANE - HEADS)))


def _fox_fwd_layer(tag, x, p):
    t = x.shape[0]
    tm, pre_in, pre_out = _fox_specs(t)
    blk, nb = _fox_blocks(t)
    h = _rms_fwd(tag + "_rms", x, p["g"])
    proj = _mm(tag + "_in", h, p["w_in"])
    pre_ins = [proj, p["q_g"], p["k_g"], p["b_f"]]
    qn, kn, lf = _seg_fwd(tag + "_pre", _f_fox_pre, (t // tm,), pre_ins, pre_in,
                          [jax.ShapeDtypeStruct((t, D), BF16)] * 2 + [jax.ShapeDtypeStruct((t, LANE), F32)], pre_out)
    ch = _per_head(_cumsum_rows(tag + "_cumsum", [lf], False))
    c_col, c_row = ch.reshape(HEADS, t, 1), ch.reshape(HEADS, nb, 1, blk)
    o, lse = _fox_fwd(tag + "_attn", qn, kn, proj, c_col, c_row)
    out = _mm(tag + "_out", o, p["w_out"], res=x)
    return out, (x, h, proj, pre_ins, qn, kn, c_col, c_row, o, lse)


def _fox_bwd_layer(tag, saved, p, dout, after=None):
    x, h, proj, pre_ins, qn, kn, c_col, c_row, o, lse = saved
    t = x.shape[0]
    tm, pre_in, pre_out = _fox_specs(t)
    g = {"w_out": _mm(tag + "_dwout", o, dout, ta=True, out_dtype=BF16)}
    do = _mm(tag + "_do", dout, p["w_out"], tb=True, after=after)
    dqn, dc_col, delta = _fox_dq(tag + "_dq", qn, kn, proj, c_col, c_row, o, lse, do)
    dkn, dproj, dc_row = _fox_dkv(tag + "_dkv", qn, kn, proj, c_col, c_row, lse, delta, do)
    dlf = _cumsum_rows(tag + "_dcumsum", [_per_lane(dc_col.reshape(HEADS, t)), _per_lane(dc_row.reshape(HEADS, t))],
                       True)
    dproj, g["q_g"], g["k_g"], g["b_f"] = _seg_bwd(
        tag + "_dpre", _f_fox_pre, (t // tm,), pre_ins, pre_in, [dqn, dkn, dlf], pre_out,
        [BF16, "acc", "acc", "acc"], into={0: dproj})
    g["w_in"] = _mm(tag + "_dwin", h, dproj, ta=True, out_dtype=BF16)
    dh = _mm(tag + "_dh", dproj, p["w_in"], tb=True, out_dtype=BF16)
    dx, g["g"] = _rms_bwd(tag + "_drms", x, p["g"], dh, dout)
    return dx, g


_MIXERS = ((_conf_fwd, _conf_bwd), (_gdn_fwd, _gdn_bwd), (_fox_fwd_layer, _fox_bwd_layer))


def _local_step(x, target, params_of, on_grads):
    saved, params = [], []
    for i in range(DEPTH):
        mp = params_of(i, 0, x)
        x, sm = _MIXERS[i % N_MIXERS][0](f"l{i}_mix", x, mp)
        fp = params_of(i, 1, x)
        x, sf = _ffn_fwd(f"l{i}_ffn", x, fp)
        saved.append((sm, sf))
        params.append((mp, fp))
    dx, sq = _loss_head(x, target)
    after = None
    for i in reversed(range(DEPTH)):
        dx, gf = _ffn_bwd(f"l{i}_ffn", saved[i][1], params[i][1], dx, after)
        after = on_grads(i, 1, gf)
        dx, gm = _MIXERS[i % N_MIXERS][1](f"l{i}_mix", saved[i][0], params[i][0], dx, after)
        after = on_grads(i, 0, gm)
    return sq, dx


def _unshard(name, g):
    axis = g.ndim - 2 if name in ROW_SHARDED else g.ndim - 1
    m = jnp.moveaxis(g, 0, axis - 1)
    return m.reshape(m.shape[:axis - 1] + (N_DEV * m.shape[axis],) + m.shape[axis + 1:])


def _reshard(name, full):
    axis = full.ndim - 2 if name in ROW_SHARDED else full.ndim - 1
    s = full.shape
    return jnp.moveaxis(full.reshape(s[:axis] + (N_DEV, s[axis] // N_DEV) + s[axis + 1:]), axis, 0)


def _pad_cols(a, width):
    return jnp.pad(a, [(0, 0)] * (a.ndim - 1) + [(0, width - a.shape[-1])])


def _lane_vec(v):
    return _pad_cols(v.reshape(1, -1), LANE)


REP_ROWS = 16


def _pack_rep(r):
    small = [_pad_cols(r[n].reshape(1, -1), LANE) for n in REPLICATED[2:]]
    row = jnp.concatenate(small + [jnp.zeros((1, D - LANE * len(small)), F32)], axis=1)
    pad = jnp.zeros((REP_ROWS - 2 * DEPTH - 1, D), F32)
    return jnp.concatenate([r['mix_norm_g'].reshape(DEPTH, D), r['ffn_norm_g'].reshape(DEPTH, D), row, pad], axis=0)


def _unpack_rep(a, shapes):
    out = {'mix_norm_g': a[:DEPTH], 'ffn_norm_g': a[DEPTH:2 * DEPTH]}
    for i, n in enumerate(REPLICATED[2:]):
        out[n] = a[2 * DEPTH:2 * DEPTH + 1, i * LANE:i * LANE + shapes[n][1]].reshape(shapes[n])
    return out


def _view2d(a):
    return a.reshape(-1, a.shape[-1])


MIXER_SHARDED = (
    (('conv_w_in', 'w_in'), ('conv_w_out', 'w_out'), ('conv_b_in', 'b_in'), ('conv_w_dw', 'w_dw'),
     ('conv_b_dw', 'b_dw'), ('conv_ln_g', 'ln_g'), ('conv_ln_b', 'ln_b')),
    (('gdn_w_in', 'w_in'), ('gdn_w_out', 'w_out'), ('gdn_conv_w', 'conv_w')),
    (('fox_w_in', 'w_in'), ('fox_w_out', 'w_out')),
)
FFN_SHARDED = (('ffn_w_up', 'w_up'), ('ffn_w_down', 'w_down'), ('ffn_w_dw', 'w_dw'))
MIXER_REPLICATED = (
    (),
    (('gdn_a_log', 'a_log'), ('gdn_dt_bias', 'dt_bias'), ('gdn_o_norm_g', 'o_g')),
    (('fox_b_f', 'b_f'), ('fox_q_norm_g', 'q_g'), ('fox_k_norm_g', 'k_g')),
)
ROW_VECTORS = ('conv_b_in', 'conv_b_dw', 'conv_ln_g', 'conv_ln_b')
INTERLEAVED = {'ffn_w_up': (D_FF, FFN_TC), 'ffn_w_dw': (D_FF, FFN_TC), 'conv_w_in': (D, CONF_TC),
               'conv_b_in': (D, CONF_TC)}
PACK_GROUP = 16 * D


def _part_entries(i, part):
    ent = [(n, i, k) for n, k in FFN_SHARDED] if part else [(n, i // N_MIXERS, k) for n, k in MIXER_SHARDED[i % N_MIXERS]]
    return [e for e in ent if e[0] in MATRICES], [e for e in ent if e[0] not in MATRICES]


def _interleave(a, half, blk):
    s = a.shape[:-1]
    return jnp.swapaxes(a.reshape(s + (2, half // blk, blk)), -3, -2).reshape(s + (2 * half,))


def _deinterleave(a, half, blk):
    s = a.shape[:-1]
    return jnp.swapaxes(a.reshape(s + (half // blk, 2, blk)), -3, -2).reshape(s + (2 * half,))


def _to_param(name, whole):
    if name in ROW_VECTORS:
        whole = whole[None]
    if name in INTERLEAVED:
        return _interleave(whole, *INTERLEAVED[name])
    if name == 'gdn_w_in':
        return _pad_cols(whole, GDN_PAD)
    if name == 'fox_w_in':
        return jnp.concatenate([whole[:, :2 * D], _pad_cols(whole[:, 3 * D:], LANE), whole[:, 2 * D:3 * D]], axis=1)
    return whole


def _from_grad(name, g):
    if name in INTERLEAVED:
        g = _deinterleave(g, *INTERLEAVED[name])
    if name in ROW_VECTORS:
        return g[0]
    if name == 'gdn_w_in':
        return g[:, :4 * D + 2 * HEADS]
    if name == 'fox_w_in':
        return jnp.concatenate([g[:, :2 * D], g[:, 2 * D + LANE:], g[:, 2 * D:2 * D + HEADS]], axis=1)
    return g


def _layer_col(name, n):
    if name in INTERLEAVED:
        half, blk = INTERLEAVED[name]
        return (n % half) // blk * (2 * blk) + (n // half) * blk + n % blk
    if name == 'fox_w_in':
        return np.where(n < 2 * D, n, np.where(n < 3 * D, n + LANE, n - D))
    return n


def _col_runs(name, shard_cols):
    dst = _layer_col(name, np.arange(N_DEV * shard_cols))
    runs, start = [[] for _ in range(N_DEV)], 0
    for k in range(1, dst.size + 1):
        if k == dst.size or dst[k] != dst[k - 1] + 1 or k % shard_cols == 0:
            runs[start // shard_cols].append((start % shard_cols, k - start, int(dst[start])))
            start = k
    width = {'gdn_w_in': GDN_PAD, 'fox_w_in': FOX_PAD}.get(name, dst.size)
    free = np.ones(width + 1, bool)
    free[dst] = False
    free[width] = False
    gaps, start = [], None
    for k in range(width + 1):
        if free[k] and start is None:
            start = k
        if not free[k] and start is not None:
            gaps.append((start, k - start))
            start = None
    return runs, width, gaps


def _unshard_cols(call, name, land, own, mine):
    _, r, c = land.shape
    runs, width, gaps = _col_runs(name, c)
    tr = _tile(r, 256)

    def body(me_ref, land_ref, own_ref, o_ref):
        for s in range(N_DEV):
            for src, ln, dst in runs[s]:
                o_ref[:, dst:dst + ln] = jnp.where(me_ref[0] == s, own_ref[:, src:src + ln], land_ref[s, :, src:src + ln])
        for start, ln in gaps:
            o_ref[:, start:start + ln] = jnp.zeros((tr, ln), o_ref.dtype)

    return pl.pallas_call(
        body, grid=(r // tr,),
        in_specs=[pl.BlockSpec(memory_space=pltpu.SMEM), pl.BlockSpec((N_DEV, tr, c), lambda i: (0, i, 0)),
                  pl.BlockSpec((tr, c), lambda i: (i, 0))],
        out_specs=pl.BlockSpec((tr, width), lambda i: (i, 0)), out_shape=jax.ShapeDtypeStruct((r, width), land.dtype),
        name=call, compiler_params=_params("parallel"))(mine, land, own)


def _reshard_cols(call, name, g, shard_cols):
    r = g.shape[0]
    runs, width, _ = _col_runs(name, shard_cols)
    tr = _tile(r, 256)

    def body(g_ref, o_ref):
        for s in range(N_DEV):
            for src, ln, dst in runs[s]:
                o_ref[s, :, src:src + ln] = g_ref[:, dst:dst + ln]

    return pl.pallas_call(
        body, grid=(r // tr,), in_specs=[pl.BlockSpec((tr, width), lambda i: (i, 0))],
        out_specs=pl.BlockSpec((N_DEV, tr, shard_cols), lambda i: (0, i, 0)),
        out_shape=jax.ShapeDtypeStruct((N_DEV, r, shard_cols), g.dtype), name=call,
        compiler_params=_params("parallel"))(g)


def _pack_rows(parts, lead):
    out = []
    for a in parts:
        flat = a.reshape(a.shape[:lead] + (-1,))
        size = flat.shape[-1]
        padded = -(-size // PACK_GROUP) * PACK_GROUP
        flat = jnp.pad(flat, [(0, 0)] * lead + [(0, padded - size)])
        out.append(flat.reshape(a.shape[:lead] + (padded // D, D)))
    return jnp.concatenate(out, axis=lead)


def _unpack_rows(packed, shapes, lead):
    out, row = [], 0
    head = packed.shape[:lead]
    for s in shapes:
        size = 1
        for d in s:
            size *= d
        rows = -(-size // PACK_GROUP) * (PACK_GROUP // D)
        part = lax.slice_in_dim(packed, row, row + rows, axis=lead)
        out.append(part.reshape(head + (rows * D,))[..., :size].reshape(head + tuple(s)))
        row += rows
    return out


def _part_dict(i, part, whole, rep):
    if part:
        p = {k: _to_param(n, whole[n]) for n, k in FFN_SHARDED if n in whole}
        p["g"] = rep['ffn_norm_g'][i][None]
        return p
    kind, j = i % N_MIXERS, i // N_MIXERS
    p = {k: _to_param(n, whole[n]) for n, k in MIXER_SHARDED[kind] if n in whole}
    for n, k in MIXER_REPLICATED[kind]:
        p[k] = rep[n][j][None] if rep[n].shape[-1] == DH else _lane_vec(rep[n][j])
    p["g"] = rep['mix_norm_g'][i][None]
    return p


def _part_grads(i, part, g):
    return {n: _from_grad(n, g[k]) for n, k in (FFN_SHARDED if part else MIXER_SHARDED[i % N_MIXERS])}


def _replicated_grads(grads):
    rep = {'mix_norm_g': jnp.concatenate([g[0]["g"] for g in grads]),
           'ffn_norm_g': jnp.concatenate([g[1]["g"] for g in grads])}
    for kind in range(N_MIXERS):
        for n, k in MIXER_REPLICATED[kind]:
            rep[n] = jnp.stack([grads[i][0][k] for i in range(kind, DEPTH, N_MIXERS)])
    return rep


def kernel(x, mix_norm_g, ffn_norm_g, conv_w_in, conv_b_in, conv_w_dw, conv_b_dw, conv_ln_g, conv_ln_b, conv_w_out, gdn_w_in, gdn_conv_w, gdn_a_log, gdn_dt_bias, gdn_o_norm_g, gdn_w_out, fox_w_in, fox_b_f, fox_q_norm_g, fox_k_norm_g, fox_w_out, ffn_w_up, ffn_w_dw, ffn_w_down, loss_target, m_mix_norm_g, m_ffn_norm_g, m_conv_w_in, m_conv_b_in, m_conv_w_dw, m_conv_b_dw, m_conv_ln_g, m_conv_ln_b, m_conv_w_out, m_gdn_w_in, m_gdn_conv_w, m_gdn_a_log, m_gdn_dt_bias, m_gdn_o_norm_g, m_gdn_w_out, m_fox_w_in, m_fox_b_f, m_fox_q_norm_g, m_fox_k_norm_g, m_fox_w_out, m_ffn_w_up, m_ffn_w_dw, m_ffn_w_down, v_mix_norm_g, v_ffn_norm_g, v_conv_w_in, v_conv_b_in, v_conv_w_dw, v_conv_b_dw, v_conv_ln_g, v_conv_ln_b, v_conv_w_out, v_gdn_w_in, v_gdn_conv_w, v_gdn_a_log, v_gdn_dt_bias, v_gdn_o_norm_g, v_gdn_w_out, v_fox_w_in, v_fox_b_f, v_fox_q_norm_g, v_fox_k_norm_g, v_fox_w_out, v_ffn_w_up, v_ffn_w_dw, v_ffn_w_down):
    given = dict(zip(
        WEIGHTS + ["m_" + n for n in WEIGHTS] + ["v_" + n for n in WEIGHTS],
        (mix_norm_g, ffn_norm_g, conv_w_in, conv_b_in, conv_w_dw, conv_b_dw, conv_ln_g, conv_ln_b, conv_w_out, gdn_w_in, gdn_conv_w, gdn_a_log, gdn_dt_bias, gdn_o_norm_g, gdn_w_out, fox_w_in, fox_b_f, fox_q_norm_g, fox_k_norm_g, fox_w_out, ffn_w_up, ffn_w_dw, ffn_w_down,
         m_mix_norm_g, m_ffn_norm_g, m_conv_w_in, m_conv_b_in, m_conv_w_dw, m_conv_b_dw, m_conv_ln_g, m_conv_ln_b, m_conv_w_out, m_gdn_w_in, m_gdn_conv_w, m_gdn_a_log, m_gdn_dt_bias, m_gdn_o_norm_g, m_gdn_w_out, m_fox_w_in, m_fox_b_f, m_fox_q_norm_g, m_fox_k_norm_g, m_fox_w_out, m_ffn_w_up, m_ffn_w_dw, m_ffn_w_down,
         v_mix_norm_g, v_ffn_norm_g, v_conv_w_in, v_conv_b_in, v_conv_w_dw, v_conv_b_dw, v_conv_ln_g, v_conv_ln_b, v_conv_w_out, v_gdn_w_in, v_gdn_conv_w, v_gdn_a_log, v_gdn_dt_bias, v_gdn_o_norm_g, v_gdn_w_out, v_fox_w_in, v_fox_b_f, v_fox_q_norm_g, v_fox_k_norm_g, v_fox_w_out, v_ffn_w_up, v_ffn_w_dw, v_ffn_w_down)))

    me = 4 * lax.axis_index("x") + 2 * lax.axis_index("y") + lax.axis_index("c")
    mine = me.astype(jnp.int32).reshape(1)

    def gather_entries(i, group):
        mats, smalls = _part_entries(i, min(group, 1))
        mats = [e for e in mats if (e[0] == 'ffn_w_down') == (group == 2)]
        return mats, ([] if group == 2 else smalls)

    gathers, token = {}, jnp.zeros((1, 1), F32)
    for i in range(DEPTH):
        for group in (0, 1, 2):
            mats, smalls = gather_entries(i, group)
            sent = [given[n][j].astype(BF16) for n, j, _ in mats]
            if smalls:
                sent.append(_pack_rows([given[n][j] for n, j, _ in smalls], 0))
            gathers[i, group], tok = _exchange_start(f"gather{i}{'mud'[group]}_start", sent, [False] * len(sent))
            token = token + tok[0:1, 0:1]

    def params_of(i, part, x_in):
        p = fetch(i, part, token if (i, part) == (0, 0) else x_in)
        if part == 1:
            p["w_down"] = lambda act: fetch(i, 2, act)["w_down"]
        return p

    def fetch(i, group, after):
        mats, smalls = gather_entries(i, group)
        part = min(group, 1)
        lands, owns = _exchange_wait(f"gather{i}{'mud'[group]}_wait", gathers[i, group], after, me)
        whole, relaid = {}, {}
        for (n, _, k), land, own in zip(mats, lands, owns):
            if n in ROW_SHARDED:
                whole[n] = _unshard(n, _fill_own(land, own, me))
            else:
                relaid[k] = _unshard_cols(f"l{i}_{n}_unshard", n, land, own, mine)
        if smalls:
            shapes = [given[n].shape[1:] for n, _, _ in smalls]
            for (n, _, _), g in zip(smalls, _unpack_rows(_fill_own(lands[-1], owns[-1], me), shapes, 1)):
                whole[n] = _unshard(n, g)
        return {**_part_dict(i, part, whole, given), **relaid}

    grads, exchanges = [[None, None] for _ in range(DEPTH)], {}

    def on_grads(i, part, g):
        grads[i][part] = g
        mats, smalls = _part_entries(i, part)
        sent = [_reshard(n, g[k]) if n in ROW_SHARDED else
                _reshard_cols(f"l{i}_{n}_reshard", n, g[k], given[n].shape[-1]) for n, _, k in mats]
        if smalls:
            sent.append(_pack_rows([_reshard(n, _from_grad(n, g[k])) for n, _, k in smalls], 1))
        last = (i, part) == (0, 0)
        if last:
            sent.append(_pack_rep(_replicated_grads(grads)))
        exchanges[i, part], tok = _exchange_start(f"grads{i}{'mf'[part]}_start", sent,
                                                  [True] * (len(sent) - last) + [False] * last)
        tokens.append(tok)
        return tok

    tokens = []
    sq, dx = _local_step(x[0], loss_target[0], params_of, on_grads)
    loss = (0.5 / D) * lax.psum(sq[0, 0], ("x", "y", "c"))

    new, after = {}, tokens[-1]
    for part in (1, 0):
        pieces = {}
        for i in range(DEPTH):
            mats, smalls = _part_entries(i, part)
            lands, owns = _exchange_wait(f"grads{i}{'mf'[part]}_wait", exchanges[i, part], after, me)
            if (i, part) == (0, 0):
                rep_piece = (lands[-1], owns[-1])
            for (n, _, _), land, own in zip(mats, lands, owns):
                pieces.setdefault(n, []).append((land, own))
            if smalls:
                shapes = [given[n].shape[1:] for n, _, _ in smalls]
                for (n, _, _), land, own in zip(smalls, _unpack_rows(lands[len(mats)], shapes, 1),
                                                _unpack_rows(owns[len(mats)], shapes, 0)):
                    pieces.setdefault(n, []).append((land, own))
        for n in pieces:
            lands = [l.reshape((N_DEV, -1, l.shape[-1])) for l, _ in pieces[n]]
            owns = [o.reshape((-1, o.shape[-1])) for _, o in pieces[n]]
            if owns[0].shape[0] % 8:
                lands = [jnp.stack(lands, axis=1).reshape((N_DEV, -1, lands[0].shape[-1]))]
                owns = [jnp.stack(owns).reshape((-1, owns[0].shape[-1]))]
            outs = _adam("adam_" + n, lands, owns, mine, _view2d(given[n]), _view2d(given["m_" + n]),
                         _view2d(given["v_" + n]))
            new[n] = [o.reshape(given[n].shape) for o in outs]
        after = new['ffn_w_up'][0]
    packed = [_pack_rep({n: given[pre + n] for n in REPLICATED}) for pre in ("", "m_", "v_")]
    outs = _adam("adam_replicated", [rep_piece[0]], [rep_piece[1]], mine, *packed)
    unpacked = [_unpack_rep(o, {n: given[n].shape for n in REPLICATED}) for o in outs]
    for n in REPLICATED:
        new[n] = [u[n] for u in unpacked]
    return (loss, dx[None], *[new[n][0] for n in WEIGHTS], *[new[n][1] for n in WEIGHTS],
            *[new[n][2] for n in WEIGHTS], *[new[n][3] for n in WEIGHTS])
```

```python
import functools

import jax
import jax.numpy as jnp
import numpy as np
from jax import lax
from jax.experimental import pallas as pl
from jax.experimental.pallas import tpu as pltpu

F32 = jnp.float32
BF16 = jnp.bfloat16
HIGHEST = lax.Precision.HIGHEST
HIGH = lax.Precision.HIGH

N_DEV = 8
LANE = 128
EPS = 1e-6
DEPTH = 4
N_MIXERS = 3
HEADS = 8
DH = 128
D = HEADS * DH
D_FF = 2816
CONF_K, GDN_K, FFN_K = 31, 4, 3
GDN_CHUNK = 64
GDN_PAD = 4224
FOX_PAD = 3200
FOX_V_BLOCK = (2 * D + LANE) // DH
ADAM_LR, ADAM_B1, ADAM_B2, ADAM_EPS, ADAM_WD, ADAM_STEP = 0.001, 0.9, 0.999, 1e-08, 0.01, 10
VMEM_LIMIT = 56 * 1024 * 1024

WEIGHTS = ['mix_norm_g', 'ffn_norm_g', 'conv_w_in', 'conv_b_in', 'conv_w_dw', 'conv_b_dw', 'conv_ln_g', 'conv_ln_b',
           'conv_w_out', 'gdn_w_in', 'gdn_conv_w', 'gdn_a_log', 'gdn_dt_bias', 'gdn_o_norm_g', 'gdn_w_out', 'fox_w_in',
           'fox_b_f', 'fox_q_norm_g', 'fox_k_norm_g', 'fox_w_out', 'ffn_w_up', 'ffn_w_dw', 'ffn_w_down']
REPLICATED = ['mix_norm_g', 'ffn_norm_g', 'gdn_a_log', 'gdn_dt_bias', 'gdn_o_norm_g', 'fox_b_f', 'fox_q_norm_g',
              'fox_k_norm_g']
ROW_SHARDED = ['conv_w_out', 'gdn_w_out', 'fox_w_out', 'ffn_w_down']
MATRICES = ['conv_w_in', 'conv_w_out', 'gdn_w_in', 'gdn_w_out', 'fox_w_in', 'fox_w_out', 'ffn_w_up', 'ffn_w_down']
SHARDED = [n for n in WEIGHTS if n not in REPLICATED]


def _params(*sem):
    return pltpu.CompilerParams(dimension_semantics=sem, vmem_limit_bytes=VMEM_LIMIT)


def _tile(n, cap):
    if n <= cap:
        return n
    d = (cap // LANE) * LANE
    while d >= LANE:
        if n % d == 0:
            return d
        d -= LANE
    raise ValueError(f"no lane-aligned tile of {n} under {cap}")


def _raw_dot(a, b, ca, cb, hp):
    batch = ((0,), (0,)) if a.ndim == 3 else ((), ())
    dn = (((ca,), (cb,)), batch)
    if hp:
        return lax.dot_general(a.astype(F32), b.astype(F32), dn, precision=HIGH, preferred_element_type=F32)
    return lax.dot_general(a.astype(BF16), b.astype(BF16), dn, preferred_element_type=F32)


def _raw_nn(a, b, hp=False):
    return _raw_dot(a, b, a.ndim - 1, b.ndim - 2, hp)


def _raw_nt(a, b, hp=False):
    return _raw_dot(a, b, a.ndim - 1, b.ndim - 1, hp)


def _raw_tn(a, b, hp=False):
    return _raw_dot(a, b, a.ndim - 2, b.ndim - 2, hp)


@functools.partial(jax.custom_vjp, nondiff_argnums=(2,))
def _nn(a, b, hp):
    return _raw_nn(a, b, hp)


def _nn_fwd(a, b, hp):
    return _raw_nn(a, b, hp), (a, b)


def _nn_bwd(hp, res, g):
    a, b = res
    return _raw_nt(g, b, hp), _raw_tn(a, g, hp)


_nn.defvjp(_nn_fwd, _nn_bwd)


@functools.partial(jax.custom_vjp, nondiff_argnums=(2,))
def _nt(a, b, hp):
    return _raw_nt(a, b, hp)


def _nt_fwd(a, b, hp):
    return _raw_nt(a, b, hp), (a, b)


def _nt_bwd(hp, res, g):
    a, b = res
    return _raw_nn(g, b, hp), _raw_tn(g, a, hp)


_nt.defvjp(_nt_fwd, _nt_bwd)


CONV_TAIL = 32


def _zero_tail(x):
    return jnp.concatenate([x, jnp.zeros((CONV_TAIL, x.shape[1]), x.dtype)], axis=0)


def _row(w, k):
    r = lax.broadcasted_iota(jnp.int32, w.shape, 0)
    return jnp.sum(jnp.where(r == k, w, 0.0), axis=0, keepdims=True)


CONV_ROWS = 128


@jax.custom_vjp
def _dwconv(x, w):
    t, taps = x.shape[0], w.shape[0]
    rows = min(t, CONV_ROWS)
    xp = jnp.concatenate([jnp.zeros((CONV_TAIL, x.shape[1]), x.dtype), x], axis=0)
    wk = [_row(w, k) for k in range(taps)]
    out = []
    for r0 in range(0, t, rows):
        acc = wk[taps - 1] * x[r0:r0 + rows]
        for k in range(taps - 1):
            lo = CONV_TAIL + r0 - (taps - 1 - k)
            acc = acc + wk[k] * xp[lo:lo + rows]
        out.append(acc)
    return jnp.concatenate(out, axis=0)


def _dwconv_fwd(x, w):
    return _dwconv(x, w), (x, w)


def _dwconv_bwd(res, dy):
    x, w = res
    t, taps = x.shape[0], w.shape[0]
    rows = min(t, CONV_ROWS)
    dye = _zero_tail(dy)
    wk = [_row(w, k) for k in range(taps)]
    dws = [jnp.zeros((1, x.shape[1]), F32) for _ in range(taps)]
    out = []
    for r0 in range(0, t, rows):
        xc, dyc = x[r0:r0 + rows], dy[r0:r0 + rows]
        acc = wk[taps - 1] * dyc
        dws[taps - 1] = dws[taps - 1] + jnp.sum(dyc * xc, axis=0, keepdims=True)
        for k in range(taps - 1):
            lo = r0 + taps - 1 - k
            up = dye[lo:lo + rows]
            acc = acc + wk[k] * up
            dws[k] = dws[k] + jnp.sum(up * xc, axis=0, keepdims=True)
        out.append(acc)
    r = lax.broadcasted_iota(jnp.int32, w.shape, 0)
    dw = jnp.zeros(w.shape, F32)
    for k in range(taps):
        dw = dw + jnp.where(r == k, dws[k], 0.0)
    return jnp.concatenate(out, axis=0), dw


_dwconv.defvjp(_dwconv_fwd, _dwconv_bwd)


def _sigmoid(x):
    return 1.0 / (1.0 + jnp.exp(-x))


def _silu(x):
    return x * _sigmoid(x)


def _softplus(x):
    return jnp.maximum(x, 0.0) + jnp.log(1.0 + jnp.exp(-jnp.abs(x)))


def _head_scale(x, fn):
    tm = x.shape[0]
    x3 = x.reshape(tm, HEADS, DH)
    return (x3 * fn(jnp.sum(x3 * x3, axis=-1, keepdims=True))).reshape(tm, HEADS * DH)


def _tile_lanes(g):
    return jnp.concatenate([g] * HEADS, axis=1)


def _expand_heads(v, first):
    lane = lax.broadcasted_iota(jnp.int32, (LANE, HEADS * DH), 0)
    col = lax.broadcasted_iota(jnp.int32, (LANE, HEADS * DH), 1)
    sel = (lane == col // DH + first).astype(F32)
    return _nn(v, sel, True)


def _f_rms(x, g):
    return (x * lax.rsqrt(jnp.mean(x * x, axis=-1, keepdims=True) + EPS) * g,)


def _f_rms_res(x, g):
    return (_f_rms(x, g)[0], x)


def _f_conf_glu_conv(u, b, w):
    c = u.shape[1] // 2
    return (_dwconv((u[:, :c] + b[:, :c]) * _sigmoid(u[:, c:] + b[:, c:]), w),)


def _f_conf_ln_silu(cv, b_dw, ln_g, ln_b):
    u = cv + b_dw
    xc = u - jnp.mean(u, axis=-1, keepdims=True)
    y = xc * lax.rsqrt(jnp.mean(xc * xc, axis=-1, keepdims=True) + EPS) * ln_g + ln_b
    return (_silu(y),)


def _f_conv_silu(u, w):
    return (_silu(_dwconv(u, w)),)


def _f_gdn_gates(qk, ab, a_log, dt_bias):
    qn = _head_scale(qk[:, :D], lambda ss: lax.rsqrt(ss + EPS)) * (DH ** -0.5)
    kn = _head_scale(qk[:, D:], lambda ss: lax.rsqrt(ss + EPS))
    g = -jnp.exp(a_log) * _softplus(ab + dt_bias)
    beta = _sigmoid(ab)
    return qn, kn, _expand_heads(g, 0), _expand_heads(beta, HEADS)


def _f_gdn_prescan(q, k, v, gb, bb):
    c = GDN_CHUNK
    n = q.shape[0] // c
    r3 = lambda t: t.reshape(n, c, DH)
    q3, k3, v3, g3, b3 = r3(q), r3(k), r3(v), r3(gb), r3(bb)
    ii = lax.broadcasted_iota(jnp.int32, (n, c, c), 1)
    jj = lax.broadcasted_iota(jnp.int32, (n, c, c), 2)
    lower, strict = ii >= jj, ii > jj
    gcb = _nn(lower.astype(F32), g3, True)
    gi = gcb[:, :, :c]
    gj = jnp.swapaxes(gi, 1, 2)
    decay = jnp.where(lower, jnp.exp(jnp.where(lower, gi - gj, 0.0)), 0.0)
    kb, vb = k3 * b3, v3 * b3
    a_mat = jnp.where(strict, _nt(kb, k3, False) * decay, 0.0)
    p = -a_mat
    t_mat = (ii == jj).astype(F32) + p
    for _ in range(5):
        p = _nn(p, p, True)
        t_mat = t_mat + _nn(t_mat, p, True)
    eg = jnp.exp(gcb)
    u = _nn(t_mat, vb, False)
    w = _nn(t_mat, kb * eg, False)
    qk = jnp.where(lower, _nt(q3, k3, False) * decay, 0.0)
    qg = q3 * eg
    g_last = jnp.sum(g3, axis=1, keepdims=True)
    kd = k3 * jnp.exp(g_last - gcb)
    r2 = lambda t: t.reshape(n * c, DH)
    return r2(u), r2(w), r2(qg), qk, r2(kd), jnp.exp(g_last)


def _f_gdn_post(o, z, o_g):
    on = _head_scale(o, lambda ss: lax.rsqrt(ss / DH + EPS)) * _tile_lanes(o_g)
    return (on * _silu(z),)


def _f_fox_pre(qkf, q_g, k_g, b_f):
    qn = _head_scale(qkf[:, :D], lambda ss: lax.rsqrt(ss / DH + EPS)) * _tile_lanes(q_g)
    kn = _head_scale(qkf[:, D:2 * D], lambda ss: lax.rsqrt(ss / DH + EPS)) * _tile_lanes(k_g)
    return qn, kn, -_softplus(-(qkf[:, 2 * D:] + b_f))


def _f_ffn_mid(u, w):
    c = u.shape[1] // 2
    return (_silu(_dwconv(u[:, :c], w[:, :c])) * _dwconv(u[:, c:], w[:, c:]),)


def _seg_fwd(name, f, grid, ins, in_specs, out_shapes, out_specs):
    n_in = len(ins)

    def body(*refs):
        outs = f(*[r[...].astype(F32) for r in refs[:n_in]])
        for r, o in zip(refs[n_in:], outs):
            r[...] = o.astype(r.dtype)

    return pl.pallas_call(body, grid=grid, in_specs=in_specs, out_specs=out_specs, out_shape=out_shapes, name=name,
                          compiler_params=_params(*(["parallel"] * len(grid))))(*ins)


def _seg_bwd(name, f, grid, ins, in_specs, douts, dout_specs, want, into=None):
    into = into or {}
    n_in, n_dy = len(ins), len(douts)
    diff = [i for i, w in enumerate(want) if w is not None]
    kept = [i for i in diff if i in into]
    out_shapes = [jax.ShapeDtypeStruct(ins[i].shape, into[i].dtype if i in into else F32 if want[i] == "acc" else want[i])
                  for i in diff]
    out_specs = [in_specs[i] for i in diff]
    acc_axis = len(grid) - 1

    def body(*refs):
        vals = [r[...].astype(F32) for r in refs[:n_in]]
        dys = [r[...].astype(F32) for r in refs[n_in:n_in + n_dy]]
        out_refs = refs[n_in + n_dy + len(kept):]

        def g(*dv):
            full = list(vals)
            for i, v in zip(diff, dv):
                full[i] = v
            return f(*full)

        _, vjp = jax.vjp(g, *[vals[i] for i in diff])
        grads = vjp(tuple(dys))
        first = pl.program_id(acc_axis) == 0
        for i, r, gr in zip(diff, out_refs, grads):
            if want[i] == "acc":
                @pl.when(first)
                def _(r=r, gr=gr):
                    r[...] = gr

                @pl.when(jnp.logical_not(first))
                def _(r=r, gr=gr):
                    r[...] += gr
            else:
                r[...] = gr.astype(r.dtype)

    sem = ["parallel"] * (len(grid) - 1) + ["arbitrary"]
    untouched = [pl.BlockSpec(memory_space=pl.ANY)] * len(kept)
    aliases = {n_in + n_dy + e: diff.index(i) for e, i in enumerate(kept)}
    return pl.pallas_call(body, grid=grid, in_specs=list(in_specs) + list(dout_specs) + untouched, out_specs=out_specs,
                          out_shape=out_shapes, input_output_aliases=aliases, name=name,
                          compiler_params=_params(*sem))(*ins, *douts, *[into[i] for i in kept])


def _rows(tm, width, col=0):
    return pl.BlockSpec((tm, width), lambda i, col=col: (i, col))


def _const(shape):
    return pl.BlockSpec(shape, lambda i: (0,) * len(shape))


def _cols(t, tc, off=0):
    return pl.BlockSpec((t, tc), lambda j, off=off: (0, j + off))


def _grid2(specs):
    return [pl.BlockSpec(s.block_shape, lambda j, i, f=s.index_map: f(j)) for s in specs]


def _mm(name, a, b, *, ta=False, tb=False, res=None, out_dtype=F32, after=None):
    k_dim, m = (a.shape[0], a.shape[1]) if ta else (a.shape[1], a.shape[0])
    n = b.shape[0] if tb else b.shape[1]
    tm, tn, tk = _tile(m, 1408), _tile(n, 1408), _tile(k_dim, 1408)
    nk = k_dim // tk
    grid = (m // tm, n // tn, nk)
    a_spec = pl.BlockSpec((tk, tm), lambda i, j, k: (k, i)) if ta else pl.BlockSpec((tm, tk), lambda i, j, k: (i, k))
    b_spec = pl.BlockSpec((tn, tk), lambda i, j, k: (j, k)) if tb else pl.BlockSpec((tk, tn), lambda i, j, k: (k, j))
    o_spec = pl.BlockSpec((tm, tn), lambda i, j, k: (i, j))
    dn = (((0 if ta else 1,), (1 if tb else 0,)), ((), ()))
    has_res = res is not None

    def body(*refs):
        a_ref, b_ref = refs[0], refs[1]
        res_ref = refs[2] if has_res else None
        o_ref = refs[n_in]
        p = lax.dot_general(a_ref[...].astype(BF16), b_ref[...].astype(BF16), dn, preferred_element_type=F32)

        def write(acc):
            if has_res:
                acc = acc + res_ref[...]
            if after is not None:
                acc = acc + refs[n_in - 1][0:1, 0:1]
            o_ref[...] = acc.astype(o_ref.dtype)

        if nk == 1:
            write(p)
        else:
            acc_ref = refs[-1]
            k = pl.program_id(2)

            @pl.when(k == 0)
            def _():
                acc_ref[...] = p

            @pl.when(k > 0)
            def _():
                acc_ref[...] += p

            @pl.when(k == nk - 1)
            def _():
                write(acc_ref[...])

    ins, specs = [a, b], [a_spec, b_spec]
    if has_res:
        ins.append(res)
        specs.append(o_spec)
    if after is not None:
        ins.append(after)
        specs.append(pl.BlockSpec(after.shape, lambda i, j, k: (0, 0)))
    n_in = len(ins)
    scratch = [pltpu.VMEM((tm, tn), F32)] if nk > 1 else []
    return pl.pallas_call(body, grid=grid, in_specs=specs, out_specs=o_spec, scratch_shapes=scratch,
                          out_shape=jax.ShapeDtypeStruct((m, n), out_dtype), name=name,
                          compiler_params=_params("parallel", "parallel", "arbitrary"))(*ins)


SCAN_HEADS = 2


def _head_specs(t):
    n = t // GDN_CHUNK
    col = pl.BlockSpec((t, SCAN_HEADS * DH), lambda h: (0, h))
    qk = pl.BlockSpec((SCAN_HEADS, n, GDN_CHUNK, GDN_CHUNK), lambda h: (h, 0, 0, 0))
    gl = pl.BlockSpec((SCAN_HEADS, n, 1, DH), lambda h: (h, 0, 0, 0))
    st = pl.BlockSpec((SCAN_HEADS, n, DH, DH), lambda h: (h, 0, 0, 0))
    return n, col, qk, gl, st


def _gdn_scan_fwd(name, u, w, qg, qk, kd, gl):
    t = u.shape[0]
    n, col, qk_spec, gl_spec, st_spec = _head_specs(t)
    c = GDN_CHUNK

    def body(u_ref, w_ref, qg_ref, qk_ref, kd_ref, gl_ref, o_ref, s_ref):
        def step(i, states):
            rows = pl.ds(pl.multiple_of(i * c, c), c)
            out = []
            for h, s in enumerate(states):
                cols = slice(h * DH, (h + 1) * DH)
                s_ref[h, i] = s
                vn = u_ref[rows, cols] - _raw_nn(w_ref[rows, cols], s)
                o_ref[rows, cols] = _raw_nn(qg_ref[rows, cols], s) + _raw_nn(qk_ref[h, i], vn)
                out.append(s * gl_ref[h, i] + _raw_tn(kd_ref[rows, cols], vn))
            return tuple(out)

        lax.fori_loop(0, n, step, (jnp.zeros((DH, DH), F32),) * SCAN_HEADS)

    return pl.pallas_call(
        body, grid=(HEADS // SCAN_HEADS,), in_specs=[col, col, col, qk_spec, col, gl_spec], out_specs=[col, st_spec],
        out_shape=[jax.ShapeDtypeStruct((t, D), F32), jax.ShapeDtypeStruct((HEADS, n, DH, DH), F32)], name=name,
        compiler_params=_params("parallel"))(u, w, qg, qk, kd, gl)


def _gdn_scan_bwd(name, u, w, qg, qk, kd, gl, states, do):
    t = u.shape[0]
    n, col, qk_spec, gl_spec, st_spec = _head_specs(t)
    c = GDN_CHUNK

    def body(u_ref, w_ref, qg_ref, qk_ref, kd_ref, gl_ref, s_ref, do_ref,
             du_ref, dw_ref, dqg_ref, dqk_ref, dkd_ref, dgl_ref):
        def step(r, dstates):
            i = n - 1 - r
            rows = pl.ds(pl.multiple_of(i * c, c), c)
            out = []
            for h, ds in enumerate(dstates):
                cols = slice(h * DH, (h + 1) * DH)
                s, do_c, w_c = s_ref[h, i], do_ref[rows, cols], w_ref[rows, cols]
                vn = u_ref[rows, cols] - _raw_nn(w_c, s)
                dvn = _raw_tn(qk_ref[h, i], do_c) + _raw_nn(kd_ref[rows, cols], ds)
                du_ref[rows, cols] = dvn
                dw_ref[rows, cols] = -_raw_nt(dvn, s)
                dqg_ref[rows, cols] = _raw_nt(do_c, s)
                dqk_ref[h, i] = _raw_nt(do_c, vn)
                dkd_ref[rows, cols] = _raw_nt(vn, ds)
                dgl_ref[h, i] = jnp.sum(ds * s, axis=0, keepdims=True)
                out.append(_raw_tn(qg_ref[rows, cols], do_c) + ds * gl_ref[h, i] - _raw_tn(w_c, dvn))
            return tuple(out)

        lax.fori_loop(0, n, step, (jnp.zeros((DH, DH), F32),) * SCAN_HEADS)

    big = jax.ShapeDtypeStruct((t, D), F32)
    return pl.pallas_call(
        body, grid=(HEADS // SCAN_HEADS,), in_specs=[col, col, col, qk_spec, col, gl_spec, st_spec, col],
        out_specs=[col, col, col, qk_spec, col, gl_spec],
        out_shape=[big, big, big, jax.ShapeDtypeStruct(qk.shape, F32), big, jax.ShapeDtypeStruct(gl.shape, F32)],
        name=name, compiler_params=_params("parallel"))(u, w, qg, qk, kd, gl, states, do)


def _cumsum_rows(name, parts, reverse):
    t = parts[0].shape[0]
    blk = min(t, 256)
    nb = t // blk
    n_in = len(parts)

    def body(*refs):
        o_ref = refs[n_in]
        ii = lax.broadcasted_iota(jnp.int32, (blk, blk), 0)
        jj = lax.broadcasted_iota(jnp.int32, (blk, blk), 1)
        tri = ((ii <= jj) if reverse else (ii >= jj)).astype(F32)
        carry = jnp.zeros((1, LANE), F32)
        for b in (range(nb - 1, -1, -1) if reverse else range(nb)):
            rows = pl.ds(b * blk, blk)
            x = refs[0][rows, :]
            for r in refs[1:n_in]:
                x = x + r[rows, :]
            o_ref[rows, :] = lax.dot_general(tri, x, (((1,), (0,)), ((), ())), precision=HIGHEST,
                                             preferred_element_type=F32) + carry
            carry = carry + jnp.sum(x, axis=0, keepdims=True)

    return pl.pallas_call(body, out_shape=jax.ShapeDtypeStruct((t, LANE), F32), name=name,
                          compiler_params=_params())(*parts)


def _fox_blocks(t):
    blk = 1024 if t % 1024 == 0 and t >= 2048 else 256 if t % 256 == 0 and t >= 1024 else 128
    return blk, t // blk


def _fox_logits(q, k, cq, ck, diagonal):
    s = _raw_nt(q, k) * (DH ** -0.5) + cq - ck
    if not diagonal:
        return s
    rows = lax.broadcasted_iota(jnp.int32, s.shape, 0)
    cols = lax.broadcasted_iota(jnp.int32, s.shape, 1)
    return jnp.where(cols <= rows, s, -jnp.inf)


def _fox_fwd(name, qn, kn, proj, c_col, c_row):
    t = qn.shape[0]
    blk, nb = _fox_blocks(t)
    voff = FOX_V_BLOCK

    def body(q_ref, k_ref, v_ref, cc_ref, cr_ref, o_ref, lse_ref):
        i = pl.program_id(1)
        q, cq = q_ref[...], cc_ref[...]

        def step(j, carry, diagonal=False):
            m, l, acc = carry
            rows = pl.ds(pl.multiple_of(j * blk, blk), blk)
            s = _fox_logits(q, k_ref[rows, :], cq, cr_ref[j], diagonal)
            m_new = jnp.maximum(m, jnp.max(s, axis=1, keepdims=True))
            p = jnp.exp(s - m_new)
            alpha = jnp.exp(m - m_new)
            return m_new, alpha * l + jnp.sum(p, axis=1, keepdims=True), alpha * acc + _raw_nn(p, v_ref[rows, :])

        init = (jnp.full((blk, 1), -1e30, F32), jnp.zeros((blk, 1), F32), jnp.zeros((blk, DH), F32))
        m, l, acc = step(i, lax.fori_loop(0, i, step, init), True)
        o_ref[...] = acc / l
        lse_ref[...] = m + jnp.log(l)

    tile = pl.BlockSpec((blk, DH), lambda h, i: (i, h))
    colv = pl.BlockSpec((None, blk, 1), lambda h, i: (h, i, 0))
    return pl.pallas_call(
        body, grid=(HEADS, nb),
        in_specs=[tile, pl.BlockSpec((t, DH), lambda h, i: (0, h)), pl.BlockSpec((t, DH), lambda h, i: (0, voff + h)),
                  colv, pl.BlockSpec((None, nb, 1, blk), lambda h, i: (h, 0, 0, 0))],
        out_specs=[tile, colv],
        out_shape=[jax.ShapeDtypeStruct((t, D), F32), jax.ShapeDtypeStruct((HEADS, t, 1), F32)], name=name,
        compiler_params=_params("parallel", "parallel"))(qn, kn, proj, c_col, c_row)


def _fox_dq(name, qn, kn, proj, c_col, c_row, o, lse, do):
    t = qn.shape[0]
    blk, nb = _fox_blocks(t)
    voff = FOX_V_BLOCK

    def body(q_ref, k_ref, v_ref, cc_ref, cr_ref, o_ref, lse_ref, do_ref, dq_ref, dcc_ref, dl_ref):
        i = pl.program_id(1)
        q, cq, do_b, lse_b = q_ref[...], cc_ref[...], do_ref[...], lse_ref[...]
        delta = jnp.sum(do_b * o_ref[...], axis=1, keepdims=True)

        def step(j, carry, diagonal=False):
            dq, dcc = carry
            rows = pl.ds(pl.multiple_of(j * blk, blk), blk)
            k = k_ref[rows, :]
            p = jnp.exp(_fox_logits(q, k, cq, cr_ref[j], diagonal) - lse_b)
            ds = p * (_raw_nt(do_b, v_ref[rows, :]) - delta)
            return dq + _raw_nn(ds, k), dcc + jnp.sum(ds, axis=1, keepdims=True)

        init = (jnp.zeros((blk, DH), F32), jnp.zeros((blk, 1), F32))
        dq, dcc = step(i, lax.fori_loop(0, i, step, init), True)
        dq_ref[...] = dq * (DH ** -0.5)
        dcc_ref[...] = dcc
        dl_ref[...] = delta

    tile = pl.BlockSpec((blk, DH), lambda h, i: (i, h))
    colv = pl.BlockSpec((None, blk, 1), lambda h, i: (h, i, 0))
    vec = jax.ShapeDtypeStruct((HEADS, t, 1), F32)
    return pl.pallas_call(
        body, grid=(HEADS, nb),
        in_specs=[tile, pl.BlockSpec((t, DH), lambda h, i: (0, h)), pl.BlockSpec((t, DH), lambda h, i: (0, voff + h)),
                  colv, pl.BlockSpec((None, nb, 1, blk), lambda h, i: (h, 0, 0, 0)), tile, colv, tile],
        out_specs=[tile, colv, colv],
        out_shape=[jax.ShapeDtypeStruct((t, D), F32), vec, vec], name=name,
        compiler_params=_params("parallel", "parallel"))(qn, kn, proj, c_col, c_row, o, lse, do)


def _fox_dkv(name, qn, kn, proj, c_col, c_row, lse, delta, do):
    t = qn.shape[0]
    blk, nb = _fox_blocks(t)
    voff = FOX_V_BLOCK

    def body(q_ref, k_ref, v_ref, cc_ref, cr_ref, lse_ref, dl_ref, do_ref, dk_ref, dv_ref, dcr_ref):
        j = pl.program_id(1)
        k, v, ck = k_ref[...], v_ref[...], cr_ref[...]

        def step(i, carry, diagonal=False):
            dk, dv, dcr = carry
            rows = pl.ds(pl.multiple_of(i * blk, blk), blk)
            q, do_b = q_ref[rows, :], do_ref[rows, :]
            p = jnp.exp(_fox_logits(q, k, cc_ref[rows, :], ck, diagonal) - lse_ref[rows, :])
            ds = p * (_raw_nt(do_b, v) - dl_ref[rows, :])
            return dk + _raw_tn(ds, q), dv + _raw_tn(p, do_b), dcr - jnp.sum(ds, axis=0, keepdims=True)

        init = (jnp.zeros((blk, DH), F32), jnp.zeros((blk, DH), F32), jnp.zeros((1, blk), F32))
        dk, dv, dcr = lax.fori_loop(j + 1, nb, step, step(j, init, True))
        dk_ref[...] = dk * (DH ** -0.5)
        dv_ref[...] = dv.astype(dv_ref.dtype)
        dcr_ref[...] = dcr

    full = pl.BlockSpec((t, DH), lambda h, j: (0, h))
    colf = pl.BlockSpec((None, t, 1), lambda h, j: (h, 0, 0))
    tile = pl.BlockSpec((blk, DH), lambda h, j: (j, h))
    rowv = pl.BlockSpec((None, None, 1, blk), lambda h, j: (h, j, 0, 0))
    tile_v = pl.BlockSpec((blk, DH), lambda h, j: (j, voff + h))
    big = jax.ShapeDtypeStruct((t, D), F32)
    return pl.pallas_call(
        body, grid=(HEADS, nb),
        in_specs=[full, tile, tile_v, colf, rowv, colf, colf, full],
        out_specs=[tile, tile_v, rowv],
        out_shape=[big, jax.ShapeDtypeStruct(proj.shape, BF16), jax.ShapeDtypeStruct((HEADS, nb, 1, blk), F32)], name=name,
        compiler_params=_params("parallel", "parallel"))(qn, kn, proj, c_col, c_row, lse, delta, do)


def _loss_head(y, target):
    t = y.shape[0]
    tm = min(t, 512)

    def body(y_ref, t_ref, dy_ref, sum_ref):
        err = y_ref[...] - t_ref[...]
        dy_ref[...] = err * (1.0 / D)
        part = jnp.sum(jnp.sum(err * err, axis=1, keepdims=True), axis=0, keepdims=True)

        @pl.when(pl.program_id(0) == 0)
        def _():
            sum_ref[...] = jnp.zeros_like(sum_ref)

        sum_ref[...] += jnp.broadcast_to(part, sum_ref.shape)

    return pl.pallas_call(
        body, grid=(t // tm,), in_specs=[_rows(tm, D), _rows(tm, D)], out_specs=[_rows(tm, D), _const((1, LANE))],
        out_shape=[jax.ShapeDtypeStruct((t, D), F32), jax.ShapeDtypeStruct((1, LANE), F32)], name="loss_head",
        compiler_params=_params("arbitrary"))(y, target)


ADAM_BLOCK_BYTES = 3 * 1024 * 1024


def _adam(name, lands, owns, mine, w, m, v):
    layers = len(lands)
    r, c = owns[0].shape
    block_bytes = lambda rows: N_DEV * rows * c * lands[0].dtype.itemsize
    tr = r
    if block_bytes(r) > ADAM_BLOCK_BYTES:
        tr = max(d for d in range(16, r, 16) if r % d == 0 and (block_bytes(d) <= ADAM_BLOCK_BYTES or d == 16))
    nr = r // tr

    def body(*refs):
        me_ref, land_refs, own_refs = refs[0], refs[1:1 + layers], refs[1 + layers:1 + 2 * layers]
        w_ref, m_ref, v_ref, g_ref, d_ref, nm_ref, nv_ref = refs[1 + 2 * layers:]
        layer = pl.program_id(0)
        for l in range(layers):
            @pl.when(layer == l)
            def _(l=l):
                g = jnp.zeros((tr, c), F32)
                for s in range(N_DEV):
                    g = g + jnp.where(me_ref[0] == s, own_refs[l][...], land_refs[l][s]).astype(F32)
                nm = ADAM_B1 * m_ref[...] + (1.0 - ADAM_B1) * g
                nv = ADAM_B2 * v_ref[...] + (1.0 - ADAM_B2) * (g * g)
                m_hat = nm / (1.0 - ADAM_B1 ** ADAM_STEP)
                v_hat = nv / (1.0 - ADAM_B2 ** ADAM_STEP)
                g_ref[...] = g
                d_ref[...] = -ADAM_LR * (m_hat / (jnp.sqrt(v_hat) + ADAM_EPS) + ADAM_WD * w_ref[...])
                nm_ref[...] = nm
                nv_ref[...] = nv

    at = lambda l: (lambda layer, i: jnp.where(layer == l, i, 0))
    land_specs = [pl.BlockSpec((N_DEV, tr, c), lambda layer, i, f=at(l): (0, f(layer, i), 0)) for l in range(layers)]
    own_specs = [pl.BlockSpec((tr, c), lambda layer, i, f=at(l): (f(layer, i), 0)) for l in range(layers)]
    blk = pl.BlockSpec((tr, c), lambda layer, i: (layer * nr + i, 0))
    out = jax.ShapeDtypeStruct(w.shape, F32)
    return pl.pallas_call(
        body, grid=(layers, nr),
        in_specs=[pl.BlockSpec(memory_space=pltpu.SMEM)] + land_specs + own_specs + [blk, blk, blk],
        out_specs=[blk] * 4, out_shape=[out] * 4, name=name,
        compiler_params=_params("arbitrary", "arbitrary"))(mine, *lands, *owns, w, m, v)


_HBM = pl.BlockSpec(memory_space=pltpu.HBM)
_SEM = pl.BlockSpec(memory_space=pltpu.SEMAPHORE)
_EFFECT = pltpu.SideEffectType.DATAFLOW_SIDE_EFFECTING


def _each_peer(x, y, c):
    flip = lambda v, bit: 1 - v if bit else v
    for p in range(1, N_DEV):
        px, py, pc = flip(x, p & 4), flip(y, p & 2), flip(c, p & 1)
        yield p, (px, py, pc), 4 * px + 2 * py + pc


def _sem(a, p):
    return a * (N_DEV - 1) + p - 1


def _exchange_start(name, arrays, scatter):
    n = len(arrays)
    lands = [lax.empty((N_DEV,) + (a.shape[1:] if sc else a.shape), a.dtype) for a, sc in zip(arrays, scatter)]

    def body(*refs):
        in_refs, land_refs = refs[:n], refs[n:2 * n]
        send_sems, recv_sems, token = refs[2 * n], refs[2 * n + 1], refs[-1]
        x, y, c = lax.axis_index("x"), lax.axis_index("y"), lax.axis_index("c")
        me = 4 * x + 2 * y + c
        for p, coords, peer in _each_peer(x, y, c):
            for a in range(n):
                pltpu.make_async_remote_copy(
                    src_ref=in_refs[a].at[peer] if scatter[a] else in_refs[a], dst_ref=land_refs[a].at[me],
                    send_sem=send_sems.at[_sem(a, p)], recv_sem=recv_sems.at[_sem(a, p)], device_id=coords,
                    device_id_type=pl.DeviceIdType.MESH).start()
        token[...] = jnp.zeros_like(token)

    sems = pltpu.SemaphoreType.DMA((n * (N_DEV - 1),))
    hbm = lambda a: pltpu.HBM(a.shape, a.dtype)
    out = pl.pallas_call(
        body, name=name,
        out_shape=(sems, sems, *[hbm(a) for a in arrays], *[hbm(l) for l in lands],
                   jax.ShapeDtypeStruct((8, LANE), F32)),
        in_specs=[_HBM] * (2 * n), out_specs=(_SEM, _SEM, *[_HBM] * (2 * n), pl.BlockSpec(memory_space=pltpu.VMEM)),
        input_output_aliases={i: 2 + i for i in range(2 * n)},
        compiler_params=pltpu.CompilerParams(has_side_effects=_EFFECT),
    )(*[pltpu.with_memory_space_constraint(a, pltpu.HBM) for a in arrays],
      *[pltpu.with_memory_space_constraint(l, pltpu.HBM) for l in lands])
    return (out[0], out[1], list(out[2:2 + n]), list(out[2 + n:2 + 2 * n]), scatter), out[-1]


def _exchange_wait(name, started, after, me):
    send_sems, recv_sems, sent, lands, scatter = started
    n = len(sent)

    def body(*refs):
        in_refs, land_refs = refs[:n], refs[n:2 * n]
        send_sems, recv_sems = refs[2 * n], refs[2 * n + 1]
        x, y, c = lax.axis_index("x"), lax.axis_index("y"), lax.axis_index("c")
        for p, coords, peer in _each_peer(x, y, c):
            for a in range(n):
                cp = pltpu.make_async_remote_copy(
                    src_ref=in_refs[a].at[peer] if scatter[a] else in_refs[a], dst_ref=land_refs[a].at[peer],
                    send_sem=send_sems.at[_sem(a, p)], recv_sem=recv_sems.at[_sem(a, p)], device_id=coords,
                    device_id_type=pl.DeviceIdType.MESH)
                cp.wait_send()
                cp.wait_recv()

    hbm = lambda a: pltpu.HBM(a.shape, a.dtype)
    out = pl.pallas_call(
        body, name=name, out_shape=(*[hbm(a) for a in sent], *[hbm(l) for l in lands]),
        in_specs=[_HBM] * (2 * n) + [_SEM, _SEM, pl.BlockSpec(memory_space=pl.ANY)], out_specs=[_HBM] * (2 * n),
        input_output_aliases={i: i for i in range(2 * n)},
        compiler_params=pltpu.CompilerParams(has_side_effects=_EFFECT),
    )(*sent, *lands, send_sems, recv_sems, after)
    owns = [lax.dynamic_index_in_dim(out[a], me, 0, keepdims=False) if scatter[a] else out[a] for a in range(n)]
    return list(out[n:]), owns


def _fill_own(land, own, me):
    slot = lax.broadcasted_iota(jnp.int32, (N_DEV,) + (1,) * own.ndim, 0)
    return jnp.where(slot == me, own[None], land)


def _rms_fwd(name, x, g):
    t = x.shape[0]
    tm = min(t, 512)
    return _seg_fwd(name, _f_rms, (t // tm,), [x, g], [_rows(tm, D), _const((1, D))],
                    [jax.ShapeDtypeStruct((t, D), BF16)], [_rows(tm, D)])[0]


def _rms_bwd(name, x, g, dh, dres):
    t = x.shape[0]
    tm = min(t, 512)
    return _seg_bwd(name, _f_rms_res, (t // tm,), [x, g], [_rows(tm, D), _const((1, D))],
                    [dh, dres], [_rows(tm, D), _rows(tm, D)], [F32, "acc"])


FFN_TC = 256
CONF_TC = 128


def _ffn_specs(t):
    nf = D_FF // FFN_TC
    return nf, [_cols(t, 2 * FFN_TC), _cols(FFN_K, 2 * FFN_TC)], [_cols(t, FFN_TC)]


def _ffn_fwd(tag, x, p):
    t = x.shape[0]
    nf, in_specs, out_specs = _ffn_specs(t)
    h = _rms_fwd(tag + "_rms", x, p["g"])
    u0 = _mm(tag + "_up", h, p["w_up"])
    act = _seg_fwd(tag + "_mid", _f_ffn_mid, (nf,), [u0, p["w_dw"]], in_specs,
                   [jax.ShapeDtypeStruct((t, D_FF), BF16)], out_specs)[0]
    if callable(p["w_down"]):
        p["w_down"] = p["w_down"](act)
    out = _mm(tag + "_down", act, p["w_down"], res=x)
    return out, (x, h, u0, act)


def _ffn_bwd(tag, saved, p, dout, after=None):
    x, h, u0, act = saved
    t = x.shape[0]
    nf, in_specs, out_specs = _ffn_specs(t)
    g = {"w_down": _mm(tag + "_dwdown", act, dout, ta=True, out_dtype=BF16)}
    dact = _mm(tag + "_dact", dout, p["w_down"], tb=True, out_dtype=BF16, after=after)
    du0, g["w_dw"] = _seg_bwd(tag + "_dmid", _f_ffn_mid, (nf, 1), [u0, p["w_dw"]], _grid2(in_specs), [dact],
                              _grid2(out_specs), [BF16, "acc"])
    g["w_up"] = _mm(tag + "_dwup", h, du0, ta=True, out_dtype=BF16)
    dh = _mm(tag + "_dh", du0, p["w_up"], tb=True, out_dtype=BF16)
    dx, g["g"] = _rms_bwd(tag + "_drms", x, p["g"], dh, dout)
    return dx, g


def _conf_specs(t):
    tm = min(t, 512)
    nc = D // CONF_TC
    vec = _const((1, D))
    glu_in = [_cols(t, 2 * CONF_TC), _cols(1, 2 * CONF_TC), _cols(CONF_K, CONF_TC)]
    return tm, nc, glu_in, [_cols(t, CONF_TC)], [_rows(tm, D), vec, vec, vec]


def _conf_fwd(tag, x, p):
    t = x.shape[0]
    tm, nc, glu_in, glu_out, ln_in = _conf_specs(t)
    h = _rms_fwd(tag + "_rms", x, p["g"])
    u = _mm(tag + "_in", h, p["w_in"])
    cv = _seg_fwd(tag + "_gluconv", _f_conf_glu_conv, (nc,), [u, p["b_in"], p["w_dw"]], glu_in,
                  [jax.ShapeDtypeStruct((t, D), F32)], glu_out)[0]
    act = _seg_fwd(tag + "_lnsilu", _f_conf_ln_silu, (t // tm,), [cv, p["b_dw"], p["ln_g"], p["ln_b"]], ln_in,
                   [jax.ShapeDtypeStruct((t, D), BF16)], [_rows(tm, D)])[0]
    out = _mm(tag + "_out", act, p["w_out"], res=x)
    return out, (x, h, u, cv, act)


def _conf_bwd(tag, saved, p, dout, after=None):
    x, h, u, cv, act = saved
    t = x.shape[0]
    tm, nc, glu_in, glu_out, ln_in = _conf_specs(t)
    g = {"w_out": _mm(tag + "_dwout", act, dout, ta=True, out_dtype=BF16)}
    dact = _mm(tag + "_dact", dout, p["w_out"], tb=True, out_dtype=BF16, after=after)
    dcv, g["b_dw"], g["ln_g"], g["ln_b"] = _seg_bwd(
        tag + "_dlnsilu", _f_conf_ln_silu, (t // tm,), [cv, p["b_dw"], p["ln_g"], p["ln_b"]], ln_in, [dact],
        [_rows(tm, D)], [F32, "acc", "acc", "acc"])
    du, g["b_in"], g["w_dw"] = _seg_bwd(tag + "_dgluconv", _f_conf_glu_conv, (nc, 1), [u, p["b_in"], p["w_dw"]],
                                        _grid2(glu_in), [dcv], _grid2(glu_out), [BF16, "acc", "acc"])
    g["w_in"] = _mm(tag + "_dwin", h, du, ta=True, out_dtype=BF16)
    dh = _mm(tag + "_dh", du, p["w_in"], tb=True, out_dtype=BF16)
    dx, g["g"] = _rms_bwd(tag + "_drms", x, p["g"], dh, dout)
    return dx, g


def _gdn_specs(t):
    tc, tm, rows = 256, min(t, 256), min(t, 1024)
    nq = 3 * D // tc
    conv = ([_cols(t, tc), _cols(GDN_K, tc)], [_cols(t, tc)])
    gate_in = [_rows(tm, 2 * D, 0), _rows(tm, LANE, 4 * D // LANE), _const((1, LANE)), _const((1, LANE))]
    gate_out = [_rows(tm, D)] * 4
    head = pl.BlockSpec((rows, DH), lambda h, i: (i, h))
    headv = pl.BlockSpec((rows, DH), lambda h, i: (i, 2 * HEADS + h))
    nch = rows // GDN_CHUNK
    pre_in = [head, head, headv, head, head]
    pre_out = [head, head, head, pl.BlockSpec((None, nch, GDN_CHUNK, GDN_CHUNK), lambda h, i: (h, i, 0, 0)), head,
               pl.BlockSpec((None, nch, 1, DH), lambda h, i: (h, i, 0, 0))]
    post_in = [_rows(tm, D), _rows(tm, D, 3), _const((1, DH))]
    return tm, rows, nq, conv, gate_in, gate_out, pre_in, pre_out, post_in


def _gdn_fwd(tag, x, p):
    t = x.shape[0]
    tm, rows, nq, conv, gate_in, gate_out, pre_in, pre_out, post_in = _gdn_specs(t)
    n = t // GDN_CHUNK
    big = jax.ShapeDtypeStruct((t, D), F32)
    h = _rms_fwd(tag + "_rms", x, p["g"])
    proj = _mm(tag + "_in", h, p["w_in"])
    qkv = _seg_fwd(tag + "_conv", _f_conv_silu, (nq,), [proj, p["conv_w"]], conv[0],
                   [jax.ShapeDtypeStruct((t, 3 * D), F32)], conv[1])[0]
    gate_ins = [qkv, proj, p["a_log"], p["dt_bias"]]
    qn, kn, gb, bb = _seg_fwd(tag + "_gates", _f_gdn_gates, (t // tm,), gate_ins, gate_in, [big] * 4, gate_out)
    pre_ins = [qn, kn, qkv, gb, bb]
    pre_shapes = [big, big, big, jax.ShapeDtypeStruct((HEADS, n, GDN_CHUNK, GDN_CHUNK), F32), big,
                  jax.ShapeDtypeStruct((HEADS, n, 1, DH), F32)]
    pre = _seg_fwd(tag + "_prescan", _f_gdn_prescan, (HEADS, t // rows), pre_ins, pre_in, pre_shapes, pre_out)
    o, states = _gdn_scan_fwd(tag + "_scan", *pre)
    post_ins = [o, proj, p["o_g"]]
    act = _seg_fwd(tag + "_post", _f_gdn_post, (t // tm,), post_ins, post_in, [jax.ShapeDtypeStruct((t, D), BF16)],
                   [_rows(tm, D)])[0]
    out = _mm(tag + "_out", act, p["w_out"], res=x)
    return out, (x, h, proj, gate_ins, pre_ins, pre, states, post_ins, act)


def _gdn_bwd(tag, saved, p, dout, after=None):
    x, h, proj, gate_ins, pre_ins, pre, states, post_ins, act = saved
    t = x.shape[0]
    tm, rows, nq, conv, gate_in, gate_out, pre_in, pre_out, post_in = _gdn_specs(t)
    g = {"w_out": _mm(tag + "_dwout", act, dout, ta=True, out_dtype=BF16)}
    dact = _mm(tag + "_dact", dout, p["w_out"], tb=True, out_dtype=BF16, after=after)
    do, dproj, g["o_g"] = _seg_bwd(tag + "_dpost", _f_gdn_post, (t // tm,), post_ins, post_in, [dact],
                                   [_rows(tm, D)], [F32, BF16, "acc"])
    dpre = _gdn_scan_bwd(tag + "_dscan", *pre, states, do)
    dqn, dkn, dqkv, dgb, dbb = _seg_bwd(tag + "_dprescan", _f_gdn_prescan, (HEADS, t // rows), pre_ins, pre_in,
                                        dpre, pre_out, [F32] * 5)
    dqkv, dproj, g["a_log"], g["dt_bias"] = _seg_bwd(
        tag + "_dgates", _f_gdn_gates, (t // tm,), gate_ins, gate_in, [dqn, dkn, dgb, dbb], gate_out,
        [F32, BF16, "acc", "acc"], into={0: dqkv, 1: dproj})
    dproj, g["conv_w"] = _seg_bwd(tag + "_dconv", _f_conv_silu, (nq, 1), [proj, p["conv_w"]], _grid2(conv[0]),
                                  [dqkv], _grid2(conv[1]), [BF16, "acc"], into={0: dproj})
    g["w_in"] = _mm(tag + "_dwin", h, dproj, ta=True, out_dtype=BF16)
    dh = _mm(tag + "_dh", dproj, p["w_in"], tb=True, out_dtype=BF16)
    dx, g["g"] = _rms_bwd(tag + "_drms", x, p["g"], dh, dout)
    return dx, g


def _fox_specs(t):
    tm = min(t, 512)
    vec = _const((1, LANE))
    pre_in = [_rows(tm, 2 * D + LANE, 0), vec, vec, vec]
    pre_out = [_rows(tm, D), _rows(tm, D), _rows(tm, LANE)]
    return tm, pre_in, pre_out


def _per_head(c):
    return jnp.transpose(c[:, :HEADS])


def _per_lane(ch):
    return jnp.pad(jnp.transpose(ch), ((0, 0), (0, LANE - HEADS)))


def _fox_fwd_layer(tag, x, p):
    t = x.shape[0]
    tm, pre_in, pre_out = _fox_specs(t)
    blk, nb = _fox_blocks(t)
    h = _rms_fwd(tag + "_rms", x, p["g"])
    proj = _mm(tag + "_in", h, p["w_in"])
    pre_ins = [proj, p["q_g"], p["k_g"], p["b_f"]]
    qn, kn, lf = _seg_fwd(tag + "_pre", _f_fox_pre, (t // tm,), pre_ins, pre_in,
                          [jax.ShapeDtypeStruct((t, D), BF16)] * 2 + [jax.ShapeDtypeStruct((t, LANE), F32)], pre_out)
    ch = _per_head(_cumsum_rows(tag + "_cumsum", [lf], False))
    c_col, c_row = ch.reshape(HEADS, t, 1), ch.reshape(HEADS, nb, 1, blk)
    o, lse = _fox_fwd(tag + "_attn", qn, kn, proj, c_col, c_row)
    out = _mm(tag + "_out", o, p["w_out"], res=x)
    return out, (x, h, proj, pre_ins, qn, kn, c_col, c_row, o, lse)


def _fox_bwd_layer(tag, saved, p, dout, after=None):
    x, h, proj, pre_ins, qn, kn, c_col, c_row, o, lse = saved
    t = x.shape[0]
    tm, pre_in, pre_out = _fox_specs(t)
    g = {"w_out": _mm(tag + "_dwout", o, dout, ta=True, out_dtype=BF16)}
    do = _mm(tag + "_do", dout, p["w_out"], tb=True, after=after)
    dqn, dc_col, delta = _fox_dq(tag + "_dq", qn, kn, proj, c_col, c_row, o, lse, do)
    dkn, dproj, dc_row = _fox_dkv(tag + "_dkv", qn, kn, proj, c_col, c_row, lse, delta, do)
    dlf = _cumsum_rows(tag + "_dcumsum", [_per_lane(dc_col.reshape(HEADS, t)), _per_lane(dc_row.reshape(HEADS, t))],
                       True)
    dproj, g["q_g"], g["k_g"], g["b_f"] = _seg_bwd(
        tag + "_dpre", _f_fox_pre, (t // tm,), pre_ins, pre_in, [dqn, dkn, dlf], pre_out,
        [BF16, "acc", "acc", "acc"], into={0: dproj})
    g["w_in"] = _mm(tag + "_dwin", h, dproj, ta=True, out_dtype=BF16)
    dh = _mm(tag + "_dh", dproj, p["w_in"], tb=True, out_dtype=BF16)
    dx, g["g"] = _rms_bwd(tag + "_drms", x, p["g"], dh, dout)
    return dx, g


_MIXERS = ((_conf_fwd, _conf_bwd), (_gdn_fwd, _gdn_bwd), (_fox_fwd_layer, _fox_bwd_layer))


def _local_step(x, target, params_of, on_grads):
    saved, params = [], []
    for i in range(DEPTH):
        mp = params_of(i, 0, x)
        x, sm = _MIXERS[i % N_MIXERS][0](f"l{i}_mix", x, mp)
        fp = params_of(i, 1, x)
        x, sf = _ffn_fwd(f"l{i}_ffn", x, fp)
        saved.append((sm, sf))
        params.append((mp, fp))
    dx, sq = _loss_head(x, target)
    after = None
    for i in reversed(range(DEPTH)):
        dx, gf = _ffn_bwd(f"l{i}_ffn", saved[i][1], params[i][1], dx, after)
        after = on_grads(i, 1, gf)
        dx, gm = _MIXERS[i % N_MIXERS][1](f"l{i}_mix", saved[i][0], params[i][0], dx, after)
        after = on_grads(i, 0, gm)
    return sq, dx


def _unshard(name, g):
    axis = g.ndim - 2 if name in ROW_SHARDED else g.ndim - 1
    m = jnp.moveaxis(g, 0, axis - 1)
    return m.reshape(m.shape[:axis - 1] + (N_DEV * m.shape[axis],) + m.shape[axis + 1:])


def _reshard(name, full):
    axis = full.ndim - 2 if name in ROW_SHARDED else full.ndim - 1
    s = full.shape
    return jnp.moveaxis(full.reshape(s[:axis] + (N_DEV, s[axis] // N_DEV) + s[axis + 1:]), axis, 0)


def _pad_cols(a, width):
    return jnp.pad(a, [(0, 0)] * (a.ndim - 1) + [(0, width - a.shape[-1])])


def _lane_vec(v):
    return _pad_cols(v.reshape(1, -1), LANE)


REP_ROWS = 16


def _pack_rep(r):
    small = [_pad_cols(r[n].reshape(1, -1), LANE) for n in REPLICATED[2:]]
    row = jnp.concatenate(small + [jnp.zeros((1, D - LANE * len(small)), F32)], axis=1)
    pad = jnp.zeros((REP_ROWS - 2 * DEPTH - 1, D), F32)
    return jnp.concatenate([r['mix_norm_g'].reshape(DEPTH, D), r['ffn_norm_g'].reshape(DEPTH, D), row, pad], axis=0)


def _unpack_rep(a, shapes):
    out = {'mix_norm_g': a[:DEPTH], 'ffn_norm_g': a[DEPTH:2 * DEPTH]}
    for i, n in enumerate(REPLICATED[2:]):
        out[n] = a[2 * DEPTH:2 * DEPTH + 1, i * LANE:i * LANE + shapes[n][1]].reshape(shapes[n])
    return out


def _view2d(a):
    return a.reshape(-1, a.shape[-1])


MIXER_SHARDED = (
    (('conv_w_in', 'w_in'), ('conv_w_out', 'w_out'), ('conv_b_in', 'b_in'), ('conv_w_dw', 'w_dw'),
     ('conv_b_dw', 'b_dw'), ('conv_ln_g', 'ln_g'), ('conv_ln_b', 'ln_b')),
    (('gdn_w_in', 'w_in'), ('gdn_w_out', 'w_out'), ('gdn_conv_w', 'conv_w')),
    (('fox_w_in', 'w_in'), ('fox_w_out', 'w_out')),
)
FFN_SHARDED = (('ffn_w_up', 'w_up'), ('ffn_w_down', 'w_down'), ('ffn_w_dw', 'w_dw'))
MIXER_REPLICATED = (
    (),
    (('gdn_a_log', 'a_log'), ('gdn_dt_bias', 'dt_bias'), ('gdn_o_norm_g', 'o_g')),
    (('fox_b_f', 'b_f'), ('fox_q_norm_g', 'q_g'), ('fox_k_norm_g', 'k_g')),
)
ROW_VECTORS = ('conv_b_in', 'conv_b_dw', 'conv_ln_g', 'conv_ln_b')
INTERLEAVED = {'ffn_w_up': (D_FF, FFN_TC), 'ffn_w_dw': (D_FF, FFN_TC), 'conv_w_in': (D, CONF_TC),
               'conv_b_in': (D, CONF_TC)}
PACK_GROUP = 16 * D


def _part_entries(i, part):
    ent = [(n, i, k) for n, k in FFN_SHARDED] if part else [(n, i // N_MIXERS, k) for n, k in MIXER_SHARDED[i % N_MIXERS]]
    return [e for e in ent if e[0] in MATRICES], [e for e in ent if e[0] not in MATRICES]


def _interleave(a, half, blk):
    s = a.shape[:-1]
    return jnp.swapaxes(a.reshape(s + (2, half // blk, blk)), -3, -2).reshape(s + (2 * half,))


def _deinterleave(a, half, blk):
    s = a.shape[:-1]
    return jnp.swapaxes(a.reshape(s + (half // blk, 2, blk)), -3, -2).reshape(s + (2 * half,))


def _to_param(name, whole):
    if name in ROW_VECTORS:
        whole = whole[None]
    if name in INTERLEAVED:
        return _interleave(whole, *INTERLEAVED[name])
    if name == 'gdn_w_in':
        return _pad_cols(whole, GDN_PAD)
    if name == 'fox_w_in':
        return jnp.concatenate([whole[:, :2 * D], _pad_cols(whole[:, 3 * D:], LANE), whole[:, 2 * D:3 * D]], axis=1)
    return whole


def _from_grad(name, g):
    if name in INTERLEAVED:
        g = _deinterleave(g, *INTERLEAVED[name])
    if name in ROW_VECTORS:
        return g[0]
    if name == 'gdn_w_in':
        return g[:, :4 * D + 2 * HEADS]
    if name == 'fox_w_in':
        return jnp.concatenate([g[:, :2 * D], g[:, 2 * D + LANE:], g[:, 2 * D:2 * D + HEADS]], axis=1)
    return g


def _layer_col(name, n):
    if name in INTERLEAVED:
        half, blk = INTERLEAVED[name]
        return (n % half) // blk * (2 * blk) + (n // half) * blk + n % blk
    if name == 'fox_w_in':
        return np.where(n < 2 * D, n, np.where(n < 3 * D, n + LANE, n - D))
    return n


def _col_runs(name, shard_cols):
    dst = _layer_col(name, np.arange(N_DEV * shard_cols))
    runs, start = [[] for _ in range(N_DEV)], 0
    for k in range(1, dst.size + 1):
        if k == dst.size or dst[k] != dst[k - 1] + 1 or k % shard_cols == 0:
            runs[start // shard_cols].append((start % shard_cols, k - start, int(dst[start])))
            start = k
    width = {'gdn_w_in': GDN_PAD, 'fox_w_in': FOX_PAD}.get(name, dst.size)
    free = np.ones(width + 1, bool)
    free[dst] = False
    free[width] = False
    gaps, start = [], None
    for k in range(width + 1):
        if free[k] and start is None:
            start = k
        if not free[k] and start is not None:
            gaps.append((start, k - start))
            start = None
    return runs, width, gaps


def _unshard_cols(call, name, land, own, mine):
    _, r, c = land.shape
    runs, width, gaps = _col_runs(name, c)
    tr = _tile(r, 256)

    def body(me_ref, land_ref, own_ref, o_ref):
        for s in range(N_DEV):
            for src, ln, dst in runs[s]:
                o_ref[:, dst:dst + ln] = jnp.where(me_ref[0] == s, own_ref[:, src:src + ln], land_ref[s, :, src:src + ln])
        for start, ln in gaps:
            o_ref[:, start:start + ln] = jnp.zeros((tr, ln), o_ref.dtype)

    return pl.pallas_call(
        body, grid=(r // tr,),
        in_specs=[pl.BlockSpec(memory_space=pltpu.SMEM), pl.BlockSpec((N_DEV, tr, c), lambda i: (0, i, 0)),
                  pl.BlockSpec((tr, c), lambda i: (i, 0))],
        out_specs=pl.BlockSpec((tr, width), lambda i: (i, 0)), out_shape=jax.ShapeDtypeStruct((r, width), land.dtype),
        name=call, compiler_params=_params("parallel"))(mine, land, own)


def _reshard_cols(call, name, g, shard_cols):
    r = g.shape[0]
    runs, width, _ = _col_runs(name, shard_cols)
    tr = _tile(r, 256)

    def body(g_ref, o_ref):
        for s in range(N_DEV):
            for src, ln, dst in runs[s]:
                o_ref[s, :, src:src + ln] = g_ref[:, dst:dst + ln]

    return pl.pallas_call(
        body, grid=(r // tr,), in_specs=[pl.BlockSpec((tr, width), lambda i: (i, 0))],
        out_specs=pl.BlockSpec((N_DEV, tr, shard_cols), lambda i: (0, i, 0)),
        out_shape=jax.ShapeDtypeStruct((N_DEV, r, shard_cols), g.dtype), name=call,
        compiler_params=_params("parallel"))(g)


def _pack_rows(parts, lead):
    out = []
    for a in parts:
        flat = a.reshape(a.shape[:lead] + (-1,))
        size = flat.shape[-1]
        padded = -(-size // PACK_GROUP) * PACK_GROUP
        flat = jnp.pad(flat, [(0, 0)] * lead + [(0, padded - size)])
        out.append(flat.reshape(a.shape[:lead] + (padded // D, D)))
    return jnp.concatenate(out, axis=lead)


def _unpack_rows(packed, shapes, lead):
    out, row = [], 0
    head = packed.shape[:lead]
    for s in shapes:
        size = 1
        for d in s:
            size *= d
        rows = -(-size // PACK_GROUP) * (PACK_GROUP // D)
        part = lax.slice_in_dim(packed, row, row + rows, axis=lead)
        out.append(part.reshape(head + (rows * D,))[..., :size].reshape(head + tuple(s)))
        row += rows
    return out


def _part_dict(i, part, whole, rep):
    if part:
        p = {k: _to_param(n, whole[n]) for n, k in FFN_SHARDED if n in whole}
        p["g"] = rep['ffn_norm_g'][i][None]
        return p
    kind, j = i % N_MIXERS, i // N_MIXERS
    p = {k: _to_param(n, whole[n]) for n, k in MIXER_SHARDED[kind] if n in whole}
    for n, k in MIXER_REPLICATED[kind]:
        p[k] = rep[n][j][None] if rep[n].shape[-1] == DH else _lane_vec(rep[n][j])
    p["g"] = rep['mix_norm_g'][i][None]
    return p


def _part_grads(i, part, g):
    return {n: _from_grad(n, g[k]) for n, k in (FFN_SHARDED if part else MIXER_SHARDED[i % N_MIXERS])}


def _replicated_grads(grads):
    rep = {'mix_norm_g': jnp.concatenate([g[0]["g"] for g in grads]),
           'ffn_norm_g': jnp.concatenate([g[1]["g"] for g in grads])}
    for kind in range(N_MIXERS):
        for n, k in MIXER_REPLICATED[kind]:
            rep[n] = jnp.stack([grads[i][0][k] for i in range(kind, DEPTH, N_MIXERS)])
    return rep


def kernel(x, mix_norm_g, ffn_norm_g, conv_w_in, conv_b_in, conv_w_dw, conv_b_dw, conv_ln_g, conv_ln_b, conv_w_out, gdn_w_in, gdn_conv_w, gdn_a_log, gdn_dt_bias, gdn_o_norm_g, gdn_w_out, fox_w_in, fox_b_f, fox_q_norm_g, fox_k_norm_g, fox_w_out, ffn_w_up, ffn_w_dw, ffn_w_down, loss_target, m_mix_norm_g, m_ffn_norm_g, m_conv_w_in, m_conv_b_in, m_conv_w_dw, m_conv_b_dw, m_conv_ln_g, m_conv_ln_b, m_conv_w_out, m_gdn_w_in, m_gdn_conv_w, m_gdn_a_log, m_gdn_dt_bias, m_gdn_o_norm_g, m_gdn_w_out, m_fox_w_in, m_fox_b_f, m_fox_q_norm_g, m_fox_k_norm_g, m_fox_w_out, m_ffn_w_up, m_ffn_w_dw, m_ffn_w_down, v_mix_norm_g, v_ffn_norm_g, v_conv_w_in, v_conv_b_in, v_conv_w_dw, v_conv_b_dw, v_conv_ln_g, v_conv_ln_b, v_conv_w_out, v_gdn_w_in, v_gdn_conv_w, v_gdn_a_log, v_gdn_dt_bias, v_gdn_o_norm_g, v_gdn_w_out, v_fox_w_in, v_fox_b_f, v_fox_q_norm_g, v_fox_k_norm_g, v_fox_w_out, v_ffn_w_up, v_ffn_w_dw, v_ffn_w_down):
    given = dict(zip(
        WEIGHTS + ["m_" + n for n in WEIGHTS] + ["v_" + n for n in WEIGHTS],
        (mix_norm_g, ffn_norm_g, conv_w_in, conv_b_in, conv_w_dw, conv_b_dw, conv_ln_g, conv_ln_b, conv_w_out, gdn_w_in, gdn_conv_w, gdn_a_log, gdn_dt_bias, gdn_o_norm_g, gdn_w_out, fox_w_in, fox_b_f, fox_q_norm_g, fox_k_norm_g, fox_w_out, ffn_w_up, ffn_w_dw, ffn_w_down,
         m_mix_norm_g, m_ffn_norm_g, m_conv_w_in, m_conv_b_in, m_conv_w_dw, m_conv_b_dw, m_conv_ln_g, m_conv_ln_b, m_conv_w_out, m_gdn_w_in, m_gdn_conv_w, m_gdn_a_log, m_gdn_dt_bias, m_gdn_o_norm_g, m_gdn_w_out, m_fox_w_in, m_fox_b_f, m_fox_q_norm_g, m_fox_k_norm_g, m_fox_w_out, m_ffn_w_up, m_ffn_w_dw, m_ffn_w_down,
         v_mix_norm_g, v_ffn_norm_g, v_conv_w_in, v_conv_b_in, v_conv_w_dw, v_conv_b_dw, v_conv_ln_g, v_conv_ln_b, v_conv_w_out, v_gdn_w_in, v_gdn_conv_w, v_gdn_a_log, v_gdn_dt_bias, v_gdn_o_norm_g, v_gdn_w_out, v_fox_w_in, v_fox_b_f, v_fox_q_norm_g, v_fox_k_norm_g, v_fox_w_out, v_ffn_w_up, v_ffn_w_dw, v_ffn_w_down)))

    me = 4 * lax.axis_index("x") + 2 * lax.axis_index("y") + lax.axis_index("c")
    mine = me.astype(jnp.int32).reshape(1)

    def gather_entries(i, group):
        mats, smalls = _part_entries(i, min(group, 1))
        mats = [e for e in mats if (e[0] == 'ffn_w_down') == (group == 2)]
        return mats, ([] if group == 2 else smalls)

    gathers, token = {}, jnp.zeros((1, 1), F32)
    for i in range(DEPTH):
        for group in (0, 1, 2):
            mats, smalls = gather_entries(i, group)
            sent = [given[n][j].astype(BF16) for n, j, _ in mats]
            if smalls:
                sent.append(_pack_rows([given[n][j] for n, j, _ in smalls], 0))
            gathers[i, group], tok = _exchange_start(f"gather{i}{'mud'[group]}_start", sent, [False] * len(sent))
            token = token + tok[0:1, 0:1]

    def params_of(i, part, x_in):
        p = fetch(i, part, token if (i, part) == (0, 0) else x_in)
        if part == 1:
            p["w_down"] = lambda act: fetch(i, 2, act)["w_down"]
        return p

    def fetch(i, group, after):
        mats, smalls = gather_entries(i, group)
        part = min(group, 1)
        lands, owns = _exchange_wait(f"gather{i}{'mud'[group]}_wait", gathers[i, group], after, me)
        whole, relaid = {}, {}
        for (n, _, k), land, own in zip(mats, lands, owns):
            if n in ROW_SHARDED:
                whole[n] = _unshard(n, _fill_own(land, own, me))
            else:
                relaid[k] = _unshard_cols(f"l{i}_{n}_unshard", n, land, own, mine)
        if smalls:
            shapes = [given[n].shape[1:] for n, _, _ in smalls]
            for (n, _, _), g in zip(smalls, _unpack_rows(_fill_own(lands[-1], owns[-1], me), shapes, 1)):
                whole[n] = _unshard(n, g)
        return {**_part_dict(i, part, whole, given), **relaid}

    grads, exchanges = [[None, None] for _ in range(DEPTH)], {}

    def on_grads(i, part, g):
        grads[i][part] = g
        mats, smalls = _part_entries(i, part)
        sent = [_reshard(n, g[k]) if n in ROW_SHARDED else
                _reshard_cols(f"l{i}_{n}_reshard", n, g[k], given[n].shape[-1]) for n, _, k in mats]
        if smalls:
            sent.append(_pack_rows([_reshard(n, _from_grad(n, g[k])) for n, _, k in smalls], 1))
        last = (i, part) == (0, 0)
        if last:
            sent.append(_pack_rep(_replicated_grads(grads)))
        exchanges[i, part], tok = _exchange_start(f"grads{i}{'mf'[part]}_start", sent,
                                                  [True] * (len(sent) - last) + [False] * last)
        tokens.append(tok)
        return tok

    tokens = []
    sq, dx = _local_step(x[0], loss_target[0], params_of, on_grads)
    loss = (0.5 / D) * lax.psum(sq[0, 0], ("x", "y", "c"))

    new, after = {}, tokens[-1]
    for part in (1, 0):
        pieces = {}
        for i in range(DEPTH):
            mats, smalls = _part_entries(i, part)
            lands, owns = _exchange_wait(f"grads{i}{'mf'[part]}_wait", exchanges[i, part], after, me)
            if (i, part) == (0, 0):
                rep_piece = (lands[-1], owns[-1])
            for (n, _, _), land, own in zip(mats, lands, owns):
                pieces.setdefault(n, []).append((land, own))
            if smalls:
                shapes = [given[n].shape[1:] for n, _, _ in smalls]
                for (n, _, _), land, own in zip(smalls, _unpack_rows(lands[len(mats)], shapes, 1),
                                                _unpack_rows(owns[len(mats)], shapes, 0)):
                    pieces.setdefault(n, []).append((land, own))
        for n in pieces:
            lands = [l.reshape((N_DEV, -1, l.shape[-1])) for l, _ in pieces[n]]
            owns = [o.reshape((-1, o.shape[-1])) for _, o in pieces[n]]
            if owns[0].shape[0] % 8:
                lands = [jnp.stack(lands, axis=1).reshape((N_DEV, -1, lands[0].shape[-1]))]
                owns = [jnp.stack(owns).reshape((-1, owns[0].shape[-1]))]
            outs = _adam("adam_" + n, lands, owns, mine, _view2d(given[n]), _view2d(given["m_" + n]),
                         _view2d(given["v_" + n]))
            new[n] = [o.reshape(given[n].shape) for o in outs]
        after = new['ffn_w_up'][0]
    packed = [_pack_rep({n: given[pre + n] for n in REPLICATED}) for pre in ("", "m_", "v_")]
    outs = _adam("adam_replicated", [rep_piece[0]], [rep_piece[1]], mine, *packed)
    unpacked = [_unpack_rep(o, {n: given[n].shape for n in REPLICATED}) for o in outs]
    for n in REPLICATED:
        new[n] = [u[n] for u in unpacked]
    return (loss, dx[None], *[new[n][0] for n in WEIGHTS], *[new[n][1] for n in WEIGHTS],
            *[new[n][2] for n in WEIGHTS], *[new[n][3] for n in WEIGHTS])
```

```python
import functools

import jax
import jax.numpy as jnp
import numpy as np
from jax import lax
from jax.experimental import pallas as pl
from jax.experimental.pallas import tpu as pltpu

F32 = jnp.float32
BF16 = jnp.bfloat16
HIGHEST = lax.Precision.HIGHEST
HIGH = lax.Precision.HIGH

N_DEV = 8
LANE = 128
EPS = 1e-6
DEPTH = 4
N_MIXERS = 3
HEADS = 8
DH = 128
D = HEADS * DH
D_FF = 2816
CONF_K, GDN_K, FFN_K = 31, 4, 3
GDN_CHUNK = 64
GDN_PAD = 4224
FOX_PAD = 3200
FOX_V_BLOCK = (2 * D + LANE) // DH
ADAM_LR, ADAM_B1, ADAM_B2, ADAM_EPS, ADAM_WD, ADAM_STEP = 0.001, 0.9, 0.999, 1e-08, 0.01, 10
VMEM_LIMIT = 56 * 1024 * 1024

WEIGHTS = ['mix_norm_g', 'ffn_norm_g', 'conv_w_in', 'conv_b_in', 'conv_w_dw', 'conv_b_dw', 'conv_ln_g', 'conv_ln_b',
           'conv_w_out', 'gdn_w_in', 'gdn_conv_w', 'gdn_a_log', 'gdn_dt_bias', 'gdn_o_norm_g', 'gdn_w_out', 'fox_w_in',
           'fox_b_f', 'fox_q_norm_g', 'fox_k_norm_g', 'fox_w_out', 'ffn_w_up', 'ffn_w_dw', 'ffn_w_down']
REPLICATED = ['mix_norm_g', 'ffn_norm_g', 'gdn_a_log', 'gdn_dt_bias', 'gdn_o_norm_g', 'fox_b_f', 'fox_q_norm_g',
              'fox_k_norm_g']
ROW_SHARDED = ['conv_w_out', 'gdn_w_out', 'fox_w_out', 'ffn_w_down']
MATRICES = ['conv_w_in', 'conv_w_out', 'gdn_w_in', 'gdn_w_out', 'fox_w_in', 'fox_w_out', 'ffn_w_up', 'ffn_w_down']
SHARDED = [n for n in WEIGHTS if n not in REPLICATED]


def _params(*sem):
    return pltpu.CompilerParams(dimension_semantics=sem, vmem_limit_bytes=VMEM_LIMIT)


def _tile(n, cap):
    if n <= cap:
        return n
    d = (cap // LANE) * LANE
    while d >= LANE:
        if n % d == 0:
            return d
        d -= LANE
    raise ValueError(f"no lane-aligned tile of {n} under {cap}")


def _raw_dot(a, b, ca, cb, hp):
    batch = ((0,), (0,)) if a.ndim == 3 else ((), ())
    dn = (((ca,), (cb,)), batch)
    if hp:
        return lax.dot_general(a.astype(F32), b.astype(F32), dn, precision=HIGH, preferred_element_type=F32)
    return lax.dot_general(a.astype(BF16), b.astype(BF16), dn, preferred_element_type=F32)


def _raw_nn(a, b, hp=False):
    return _raw_dot(a, b, a.ndim - 1, b.ndim - 2, hp)


def _raw_nt(a, b, hp=False):
    return _raw_dot(a, b, a.ndim - 1, b.ndim - 1, hp)


def _raw_tn(a, b, hp=False):
    return _raw_dot(a, b, a.ndim - 2, b.ndim - 2, hp)


@functools.partial(jax.custom_vjp, nondiff_argnums=(2,))
def _nn(a, b, hp):
    return _raw_nn(a, b, hp)


def _nn_fwd(a, b, hp):
    return _raw_nn(a, b, hp), (a, b)


def _nn_bwd(hp, res, g):
    a, b = res
    return _raw_nt(g, b, hp), _raw_tn(a, g, hp)


_nn.defvjp(_nn_fwd, _nn_bwd)


@functools.partial(jax.custom_vjp, nondiff_argnums=(2,))
def _nt(a, b, hp):
    return _raw_nt(a, b, hp)


def _nt_fwd(a, b, hp):
    return _raw_nt(a, b, hp), (a, b)


def _nt_bwd(hp, res, g):
    a, b = res
    return _raw_nn(g, b, hp), _raw_tn(g, a, hp)


_nt.defvjp(_nt_fwd, _nt_bwd)


CONV_TAIL = 32


def _zero_tail(x):
    return jnp.concatenate([x, jnp.zeros((CONV_TAIL, x.shape[1]), x.dtype)], axis=0)


def _row(w, k):
    r = lax.broadcasted_iota(jnp.int32, w.shape, 0)
    return jnp.sum(jnp.where(r == k, w, 0.0), axis=0, keepdims=True)


@jax.custom_vjp
def _dwconv(x, w):
    t, taps = x.shape[0], w.shape[0]
    xe = _zero_tail(x)
    y = _row(w, taps - 1) * xe
    for k in range(taps - 1):
        y = y + _row(w, k) * pltpu.roll(xe, taps - 1 - k, axis=0)
    return y[:t]


def _dwconv_fwd(x, w):
    return _dwconv(x, w), (x, w)


def _dwconv_bwd(res, dy):
    x, w = res
    t, taps = x.shape[0], w.shape[0]
    xe, dye = _zero_tail(x), _zero_tail(dy)
    r = lax.broadcasted_iota(jnp.int32, w.shape, 0)
    dx = _row(w, taps - 1) * dye
    dw = jnp.where(r == taps - 1, jnp.sum(dye * xe, axis=0, keepdims=True), 0.0)
    for k in range(taps - 1):
        up = pltpu.roll(dye, t + CONV_TAIL - (taps - 1 - k), axis=0)
        dx = dx + _row(w, k) * up
        dw = dw + jnp.where(r == k, jnp.sum(up * xe, axis=0, keepdims=True), 0.0)
    return dx[:t], dw


_dwconv.defvjp(_dwconv_fwd, _dwconv_bwd)


def _sigmoid(x):
    return 1.0 / (1.0 + jnp.exp(-x))


def _silu(x):
    return x * _sigmoid(x)


def _softplus(x):
    return jnp.maximum(x, 0.0) + jnp.log(1.0 + jnp.exp(-jnp.abs(x)))


def _head_scale(x, fn):
    tm = x.shape[0]
    x3 = x.reshape(tm, HEADS, DH)
    return (x3 * fn(jnp.sum(x3 * x3, axis=-1, keepdims=True))).reshape(tm, HEADS * DH)


def _tile_lanes(g):
    return jnp.concatenate([g] * HEADS, axis=1)


def _expand_heads(v, first):
    lane = lax.broadcasted_iota(jnp.int32, (LANE, HEADS * DH), 0)
    col = lax.broadcasted_iota(jnp.int32, (LANE, HEADS * DH), 1)
    sel = (lane == col // DH + first).astype(F32)
    return _nn(v, sel, True)


def _f_rms(x, g):
    return (x * lax.rsqrt(jnp.mean(x * x, axis=-1, keepdims=True) + EPS) * g,)


def _f_rms_res(x, g):
    return (_f_rms(x, g)[0], x)


def _f_conf_glu_conv(u, b, w):
    c = u.shape[1] // 2
    return (_dwconv((u[:, :c] + b[:, :c]) * _sigmoid(u[:, c:] + b[:, c:]), w),)


def _f_conf_ln_silu(cv, b_dw, ln_g, ln_b):
    u = cv + b_dw
    xc = u - jnp.mean(u, axis=-1, keepdims=True)
    y = xc * lax.rsqrt(jnp.mean(xc * xc, axis=-1, keepdims=True) + EPS) * ln_g + ln_b
    return (_silu(y),)


def _f_conv_silu(u, w):
    return (_silu(_dwconv(u, w)),)


def _f_gdn_gates(qk, ab, a_log, dt_bias):
    qn = _head_scale(qk[:, :D], lambda ss: lax.rsqrt(ss + EPS)) * (DH ** -0.5)
    kn = _head_scale(qk[:, D:], lambda ss: lax.rsqrt(ss + EPS))
    g = -jnp.exp(a_log) * _softplus(ab + dt_bias)
    beta = _sigmoid(ab)
    return qn, kn, _expand_heads(g, 0), _expand_heads(beta, HEADS)


def _f_gdn_prescan(q, k, v, gb, bb):
    c = GDN_CHUNK
    n = q.shape[0] // c
    r3 = lambda t: t.reshape(n, c, DH)
    q3, k3, v3, g3, b3 = r3(q), r3(k), r3(v), r3(gb), r3(bb)
    ii = lax.broadcasted_iota(jnp.int32, (n, c, c), 1)
    jj = lax.broadcasted_iota(jnp.int32, (n, c, c), 2)
    lower, strict = ii >= jj, ii > jj
    gcb = _nn(lower.astype(F32), g3, True)
    gi = gcb[:, :, :c]
    gj = jnp.swapaxes(gi, 1, 2)
    decay = jnp.where(lower, jnp.exp(jnp.where(lower, gi - gj, 0.0)), 0.0)
    kb, vb = k3 * b3, v3 * b3
    a_mat = jnp.where(strict, _nt(kb, k3, False) * decay, 0.0)
    p = -a_mat
    t_mat = (ii == jj).astype(F32) + p
    for _ in range(5):
        p = _nn(p, p, True)
        t_mat = t_mat + _nn(t_mat, p, True)
    eg = jnp.exp(gcb)
    u = _nn(t_mat, vb, False)
    w = _nn(t_mat, kb * eg, False)
    qk = jnp.where(lower, _nt(q3, k3, False) * decay, 0.0)
    qg = q3 * eg
    g_last = jnp.sum(g3, axis=1, keepdims=True)
    kd = k3 * jnp.exp(g_last - gcb)
    r2 = lambda t: t.reshape(n * c, DH)
    return r2(u), r2(w), r2(qg), qk, r2(kd), jnp.exp(g_last)


def _f_gdn_post(o, z, o_g):
    on = _head_scale(o, lambda ss: lax.rsqrt(ss / DH + EPS)) * _tile_lanes(o_g)
    return (on * _silu(z),)


def _f_fox_pre(qkf, q_g, k_g, b_f):
    qn = _head_scale(qkf[:, :D], lambda ss: lax.rsqrt(ss / DH + EPS)) * _tile_lanes(q_g)
    kn = _head_scale(qkf[:, D:2 * D], lambda ss: lax.rsqrt(ss / DH + EPS)) * _tile_lanes(k_g)
    return qn, kn, -_softplus(-(qkf[:, 2 * D:] + b_f))


def _f_ffn_mid(u, w):
    c = u.shape[1] // 2
    return (_silu(_dwconv(u[:, :c], w[:, :c])) * _dwconv(u[:, c:], w[:, c:]),)


def _seg_fwd(name, f, grid, ins, in_specs, out_shapes, out_specs):
    n_in = len(ins)

    def body(*refs):
        outs = f(*[r[...].astype(F32) for r in refs[:n_in]])
        for r, o in zip(refs[n_in:], outs):
            r[...] = o.astype(r.dtype)

    return pl.pallas_call(body, grid=grid, in_specs=in_specs, out_specs=out_specs, out_shape=out_shapes, name=name,
                          compiler_params=_params(*(["parallel"] * len(grid))))(*ins)


def _seg_bwd(name, f, grid, ins, in_specs, douts, dout_specs, want, into=None):
    into = into or {}
    n_in, n_dy = len(ins), len(douts)
    diff = [i for i, w in enumerate(want) if w is not None]
    kept = [i for i in diff if i in into]
    out_shapes = [jax.ShapeDtypeStruct(ins[i].shape, into[i].dtype if i in into else F32 if want[i] == "acc" else want[i])
                  for i in diff]
    out_specs = [in_specs[i] for i in diff]
    acc_axis = len(grid) - 1

    def body(*refs):
        vals = [r[...].astype(F32) for r in refs[:n_in]]
        dys = [r[...].astype(F32) for r in refs[n_in:n_in + n_dy]]
        out_refs = refs[n_in + n_dy + len(kept):]

        def g(*dv):
            full = list(vals)
            for i, v in zip(diff, dv):
                full[i] = v
            return f(*full)

        _, vjp = jax.vjp(g, *[vals[i] for i in diff])
        grads = vjp(tuple(dys))
        first = pl.program_id(acc_axis) == 0
        for i, r, gr in zip(diff, out_refs, grads):
            if want[i] == "acc":
                @pl.when(first)
                def _(r=r, gr=gr):
                    r[...] = gr

                @pl.when(jnp.logical_not(first))
                def _(r=r, gr=gr):
                    r[...] += gr
            else:
                r[...] = gr.astype(r.dtype)

    sem = ["parallel"] * (len(grid) - 1) + ["arbitrary"]
    untouched = [pl.BlockSpec(memory_space=pl.ANY)] * len(kept)
    aliases = {n_in + n_dy + e: diff.index(i) for e, i in enumerate(kept)}
    return pl.pallas_call(body, grid=grid, in_specs=list(in_specs) + list(dout_specs) + untouched, out_specs=out_specs,
                          out_shape=out_shapes, input_output_aliases=aliases, name=name,
                          compiler_params=_params(*sem))(*ins, *douts, *[into[i] for i in kept])


def _rows(tm, width, col=0):
    return pl.BlockSpec((tm, width), lambda i, col=col: (i, col))


def _const(shape):
    return pl.BlockSpec(shape, lambda i: (0,) * len(shape))


def _cols(t, tc, off=0):
    return pl.BlockSpec((t, tc), lambda j, off=off: (0, j + off))


def _grid2(specs):
    return [pl.BlockSpec(s.block_shape, lambda j, i, f=s.index_map: f(j)) for s in specs]


def _mm(name, a, b, *, ta=False, tb=False, res=None, out_dtype=F32, after=None):
    k_dim, m = (a.shape[0], a.shape[1]) if ta else (a.shape[1], a.shape[0])
    n = b.shape[0] if tb else b.shape[1]
    tm, tn, tk = _tile(m, 1408), _tile(n, 2816), _tile(k_dim, 1408)
    nk = k_dim // tk
    grid = (m // tm, n // tn, nk)
    a_spec = pl.BlockSpec((tk, tm), lambda i, j, k: (k, i)) if ta else pl.BlockSpec((tm, tk), lambda i, j, k: (i, k))
    b_spec = pl.BlockSpec((tn, tk), lambda i, j, k: (j, k)) if tb else pl.BlockSpec((tk, tn), lambda i, j, k: (k, j))
    o_spec = pl.BlockSpec((tm, tn), lambda i, j, k: (i, j))
    dn = (((0 if ta else 1,), (1 if tb else 0,)), ((), ()))
    has_res = res is not None

    def body(*refs):
        a_ref, b_ref = refs[0], refs[1]
        res_ref = refs[2] if has_res else None
        o_ref = refs[n_in]
        p = lax.dot_general(a_ref[...].astype(BF16), b_ref[...].astype(BF16), dn, preferred_element_type=F32)

        def write(acc):
            if has_res:
                acc = acc + res_ref[...]
            if after is not None:
                acc = acc + refs[n_in - 1][0:1, 0:1]
            o_ref[...] = acc.astype(o_ref.dtype)

        if nk == 1:
            write(p)
        else:
            acc_ref = refs[-1]
            k = pl.program_id(2)

            @pl.when(k == 0)
            def _():
                acc_ref[...] = p

            @pl.when(k > 0)
            def _():
                acc_ref[...] += p

            @pl.when(k == nk - 1)
            def _():
                write(acc_ref[...])

    ins, specs = [a, b], [a_spec, b_spec]
    if has_res:
        ins.append(res)
        specs.append(o_spec)
    if after is not None:
        ins.append(after)
        specs.append(pl.BlockSpec(after.shape, lambda i, j, k: (0, 0)))
    n_in = len(ins)
    scratch = [pltpu.VMEM((tm, tn), F32)] if nk > 1 else []
    return pl.pallas_call(body, grid=grid, in_specs=specs, out_specs=o_spec, scratch_shapes=scratch,
                          out_shape=jax.ShapeDtypeStruct((m, n), out_dtype), name=name,
                          compiler_params=_params("parallel", "parallel", "arbitrary"))(*ins)


SCAN_HEADS = 2


def _head_specs(t):
    n = t // GDN_CHUNK
    col = pl.BlockSpec((t, SCAN_HEADS * DH), lambda h: (0, h))
    qk = pl.BlockSpec((SCAN_HEADS, n, GDN_CHUNK, GDN_CHUNK), lambda h: (h, 0, 0, 0))
    gl = pl.BlockSpec((SCAN_HEADS, n, 1, DH), lambda h: (h, 0, 0, 0))
    st = pl.BlockSpec((SCAN_HEADS, n, DH, DH), lambda h: (h, 0, 0, 0))
    return n, col, qk, gl, st


def _gdn_scan_fwd(name, u, w, qg, qk, kd, gl):
    t = u.shape[0]
    n, col, qk_spec, gl_spec, st_spec = _head_specs(t)
    c = GDN_CHUNK

    def body(u_ref, w_ref, qg_ref, qk_ref, kd_ref, gl_ref, o_ref, s_ref):
        def step(i, states):
            rows = pl.ds(pl.multiple_of(i * c, c), c)
            out = []
            for h, s in enumerate(states):
                cols = slice(h * DH, (h + 1) * DH)
                s_ref[h, i] = s
                vn = u_ref[rows, cols] - _raw_nn(w_ref[rows, cols], s)
                o_ref[rows, cols] = _raw_nn(qg_ref[rows, cols], s) + _raw_nn(qk_ref[h, i], vn)
                out.append(s * gl_ref[h, i] + _raw_tn(kd_ref[rows, cols], vn))
            return tuple(out)

        lax.fori_loop(0, n, step, (jnp.zeros((DH, DH), F32),) * SCAN_HEADS)

    return pl.pallas_call(
        body, grid=(HEADS // SCAN_HEADS,), in_specs=[col, col, col, qk_spec, col, gl_spec], out_specs=[col, st_spec],
        out_shape=[jax.ShapeDtypeStruct((t, D), F32), jax.ShapeDtypeStruct((HEADS, n, DH, DH), F32)], name=name,
        compiler_params=_params("parallel"))(u, w, qg, qk, kd, gl)


def _gdn_scan_bwd(name, u, w, qg, qk, kd, gl, states, do):
    t = u.shape[0]
    n, col, qk_spec, gl_spec, st_spec = _head_specs(t)
    c = GDN_CHUNK

    def body(u_ref, w_ref, qg_ref, qk_ref, kd_ref, gl_ref, s_ref, do_ref,
             du_ref, dw_ref, dqg_ref, dqk_ref, dkd_ref, dgl_ref):
        def step(r, dstates):
            i = n - 1 - r
            rows = pl.ds(pl.multiple_of(i * c, c), c)
            out = []
            for h, ds in enumerate(dstates):
                cols = slice(h * DH, (h + 1) * DH)
                s, do_c, w_c = s_ref[h, i], do_ref[rows, cols], w_ref[rows, cols]
                vn = u_ref[rows, cols] - _raw_nn(w_c, s)
                dvn = _raw_tn(qk_ref[h, i], do_c) + _raw_nn(kd_ref[rows, cols], ds)
                du_ref[rows, cols] = dvn
                dw_ref[rows, cols] = -_raw_nt(dvn, s)
                dqg_ref[rows, cols] = _raw_nt(do_c, s)
                dqk_ref[h, i] = _raw_nt(do_c, vn)
                dkd_ref[rows, cols] = _raw_nt(vn, ds)
                dgl_ref[h, i] = jnp.sum(ds * s, axis=0, keepdims=True)
                out.append(_raw_tn(qg_ref[rows, cols], do_c) + ds * gl_ref[h, i] - _raw_tn(w_c, dvn))
            return tuple(out)

        lax.fori_loop(0, n, step, (jnp.zeros((DH, DH), F32),) * SCAN_HEADS)

    big = jax.ShapeDtypeStruct((t, D), F32)
    return pl.pallas_call(
        body, grid=(HEADS // SCAN_HEADS,), in_specs=[col, col, col, qk_spec, col, gl_spec, st_spec, col],
        out_specs=[col, col, col, qk_spec, col, gl_spec],
        out_shape=[big, big, big, jax.ShapeDtypeStruct(qk.shape, F32), big, jax.ShapeDtypeStruct(gl.shape, F32)],
        name=name, compiler_params=_params("parallel"))(u, w, qg, qk, kd, gl, states, do)


def _cumsum_rows(name, parts, reverse):
    t = parts[0].shape[0]
    blk = min(t, 256)
    nb = t // blk
    n_in = len(parts)

    def body(*refs):
        o_ref = refs[n_in]
        ii = lax.broadcasted_iota(jnp.int32, (blk, blk), 0)
        jj = lax.broadcasted_iota(jnp.int32, (blk, blk), 1)
        tri = ((ii <= jj) if reverse else (ii >= jj)).astype(F32)
        carry = jnp.zeros((1, LANE), F32)
        for b in (range(nb - 1, -1, -1) if reverse else range(nb)):
            rows = pl.ds(b * blk, blk)
            x = refs[0][rows, :]
            for r in refs[1:n_in]:
                x = x + r[rows, :]
            o_ref[rows, :] = lax.dot_general(tri, x, (((1,), (0,)), ((), ())), precision=HIGHEST,
                                             preferred_element_type=F32) + carry
            carry = carry + jnp.sum(x, axis=0, keepdims=True)

    return pl.pallas_call(body, out_shape=jax.ShapeDtypeStruct((t, LANE), F32), name=name,
                          compiler_params=_params())(*parts)


def _fox_blocks(t):
    blk = 1024 if t % 1024 == 0 and t >= 2048 else 256 if t % 256 == 0 and t >= 1024 else 128
    return blk, t // blk


def _fox_logits(q, k, cq, ck, diagonal):
    s = _raw_nt(q, k) * (DH ** -0.5) + cq - ck
    if not diagonal:
        return s
    rows = lax.broadcasted_iota(jnp.int32, s.shape, 0)
    cols = lax.broadcasted_iota(jnp.int32, s.shape, 1)
    return jnp.where(cols <= rows, s, -jnp.inf)


def _fox_fwd(name, qn, kn, proj, c_col, c_row):
    t = qn.shape[0]
    blk, nb = _fox_blocks(t)
    voff = FOX_V_BLOCK

    def body(q_ref, k_ref, v_ref, cc_ref, cr_ref, o_ref, lse_ref):
        i = pl.program_id(1)
        q, cq = q_ref[...], cc_ref[...]

        def step(j, carry, diagonal=False):
            m, l, acc = carry
            rows = pl.ds(pl.multiple_of(j * blk, blk), blk)
            s = _fox_logits(q, k_ref[rows, :], cq, cr_ref[j], diagonal)
            m_new = jnp.maximum(m, jnp.max(s, axis=1, keepdims=True))
            p = jnp.exp(s - m_new)
            alpha = jnp.exp(m - m_new)
            return m_new, alpha * l + jnp.sum(p, axis=1, keepdims=True), alpha * acc + _raw_nn(p, v_ref[rows, :])

        init = (jnp.full((blk, 1), -1e30, F32), jnp.zeros((blk, 1), F32), jnp.zeros((blk, DH), F32))
        m, l, acc = step(i, lax.fori_loop(0, i, step, init), True)
        o_ref[...] = acc / l
        lse_ref[...] = m + jnp.log(l)

    tile = pl.BlockSpec((blk, DH), lambda h, i: (i, h))
    colv = pl.BlockSpec((None, blk, 1), lambda h, i: (h, i, 0))
    return pl.pallas_call(
        body, grid=(HEADS, nb),
        in_specs=[tile, pl.BlockSpec((t, DH), lambda h, i: (0, h)), pl.BlockSpec((t, DH), lambda h, i: (0, voff + h)),
                  colv, pl.BlockSpec((None, nb, 1, blk), lambda h, i: (h, 0, 0, 0))],
        out_specs=[tile, colv],
        out_shape=[jax.ShapeDtypeStruct((t, D), F32), jax.ShapeDtypeStruct((HEADS, t, 1), F32)], name=name,
        compiler_params=_params("parallel", "parallel"))(qn, kn, proj, c_col, c_row)


def _fox_dq(name, qn, kn, proj, c_col, c_row, o, lse, do):
    t = qn.shape[0]
    blk, nb = _fox_blocks(t)
    voff = FOX_V_BLOCK

    def body(q_ref, k_ref, v_ref, cc_ref, cr_ref, o_ref, lse_ref, do_ref, dq_ref, dcc_ref, dl_ref):
        i = pl.program_id(1)
        q, cq, do_b, lse_b = q_ref[...], cc_ref[...], do_ref[...], lse_ref[...]
        delta = jnp.sum(do_b * o_ref[...], axis=1, keepdims=True)

        def step(j, carry, diagonal=False):
            dq, dcc = carry
            rows = pl.ds(pl.multiple_of(j * blk, blk), blk)
            k = k_ref[rows, :]
            p = jnp.exp(_fox_logits(q, k, cq, cr_ref[j], diagonal) - lse_b)
            ds = p * (_raw_nt(do_b, v_ref[rows, :]) - delta)
            return dq + _raw_nn(ds, k), dcc + jnp.sum(ds, axis=1, keepdims=True)

        init = (jnp.zeros((blk, DH), F32), jnp.zeros((blk, 1), F32))
        dq, dcc = step(i, lax.fori_loop(0, i, step, init), True)
        dq_ref[...] = dq * (DH ** -0.5)
        dcc_ref[...] = dcc
        dl_ref[...] = delta

    tile = pl.BlockSpec((blk, DH), lambda h, i: (i, h))
    colv = pl.BlockSpec((None, blk, 1), lambda h, i: (h, i, 0))
    vec = jax.ShapeDtypeStruct((HEADS, t, 1), F32)
    return pl.pallas_call(
        body, grid=(HEADS, nb),
        in_specs=[tile, pl.BlockSpec((t, DH), lambda h, i: (0, h)), pl.BlockSpec((t, DH), lambda h, i: (0, voff + h)),
                  colv, pl.BlockSpec((None, nb, 1, blk), lambda h, i: (h, 0, 0, 0)), tile, colv, tile],
        out_specs=[tile, colv, colv],
        out_shape=[jax.ShapeDtypeStruct((t, D), F32), vec, vec], name=name,
        compiler_params=_params("parallel", "parallel"))(qn, kn, proj, c_col, c_row, o, lse, do)


def _fox_dkv(name, qn, kn, proj, c_col, c_row, lse, delta, do):
    t = qn.shape[0]
    blk, nb = _fox_blocks(t)
    voff = FOX_V_BLOCK

    def body(q_ref, k_ref, v_ref, cc_ref, cr_ref, lse_ref, dl_ref, do_ref, dk_ref, dv_ref, dcr_ref):
        j = pl.program_id(1)
        k, v, ck = k_ref[...], v_ref[...], cr_ref[...]

        def step(i, carry, diagonal=False):
            dk, dv, dcr = carry
            rows = pl.ds(pl.multiple_of(i * blk, blk), blk)
            q, do_b = q_ref[rows, :], do_ref[rows, :]
            p = jnp.exp(_fox_logits(q, k, cc_ref[rows, :], ck, diagonal) - lse_ref[rows, :])
            ds = p * (_raw_nt(do_b, v) - dl_ref[rows, :])
            return dk + _raw_tn(ds, q), dv + _raw_tn(p, do_b), dcr - jnp.sum(ds, axis=0, keepdims=True)

        init = (jnp.zeros((blk, DH), F32), jnp.zeros((blk, DH), F32), jnp.zeros((1, blk), F32))
        dk, dv, dcr = lax.fori_loop(j + 1, nb, step, step(j, init, True))
        dk_ref[...] = dk * (DH ** -0.5)
        dv_ref[...] = dv.astype(dv_ref.dtype)
        dcr_ref[...] = dcr

    full = pl.BlockSpec((t, DH), lambda h, j: (0, h))
    colf = pl.BlockSpec((None, t, 1), lambda h, j: (h, 0, 0))
    tile = pl.BlockSpec((blk, DH), lambda h, j: (j, h))
    rowv = pl.BlockSpec((None, None, 1, blk), lambda h, j: (h, j, 0, 0))
    tile_v = pl.BlockSpec((blk, DH), lambda h, j: (j, voff + h))
    big = jax.ShapeDtypeStruct((t, D), F32)
    return pl.pallas_call(
        body, grid=(HEADS, nb),
        in_specs=[full, tile, tile_v, colf, rowv, colf, colf, full],
        out_specs=[tile, tile_v, rowv],
        out_shape=[big, jax.ShapeDtypeStruct(proj.shape, BF16), jax.ShapeDtypeStruct((HEADS, nb, 1, blk), F32)], name=name,
        compiler_params=_params("parallel", "parallel"))(qn, kn, proj, c_col, c_row, lse, delta, do)


def _loss_head(y, target):
    t = y.shape[0]
    tm = min(t, 512)

    def body(y_ref, t_ref, dy_ref, sum_ref):
        err = y_ref[...] - t_ref[...]
        dy_ref[...] = err * (1.0 / D)
        part = jnp.sum(jnp.sum(err * err, axis=1, keepdims=True), axis=0, keepdims=True)

        @pl.when(pl.program_id(0) == 0)
        def _():
            sum_ref[...] = jnp.zeros_like(sum_ref)

        sum_ref[...] += jnp.broadcast_to(part, sum_ref.shape)

    return pl.pallas_call(
        body, grid=(t // tm,), in_specs=[_rows(tm, D), _rows(tm, D)], out_specs=[_rows(tm, D), _const((1, LANE))],
        out_shape=[jax.ShapeDtypeStruct((t, D), F32), jax.ShapeDtypeStruct((1, LANE), F32)], name="loss_head",
        compiler_params=_params("arbitrary"))(y, target)


ADAM_BLOCK_BYTES = 3 * 1024 * 1024


def _adam(name, lands, owns, mine, w, m, v):
    layers = len(lands)
    r, c = owns[0].shape
    block_bytes = lambda rows: N_DEV * rows * c * lands[0].dtype.itemsize
    tr = r
    if block_bytes(r) > ADAM_BLOCK_BYTES:
        tr = max(d for d in range(16, r, 16) if r % d == 0 and (block_bytes(d) <= ADAM_BLOCK_BYTES or d == 16))
    nr = r // tr

    def body(*refs):
        me_ref, land_refs, own_refs = refs[0], refs[1:1 + layers], refs[1 + layers:1 + 2 * layers]
        w_ref, m_ref, v_ref, g_ref, d_ref, nm_ref, nv_ref = refs[1 + 2 * layers:]
        layer = pl.program_id(0)
        for l in range(layers):
            @pl.when(layer == l)
            def _(l=l):
                g = jnp.zeros((tr, c), F32)
                for s in range(N_DEV):
                    g = g + jnp.where(me_ref[0] == s, own_refs[l][...], land_refs[l][s]).astype(F32)
                nm = ADAM_B1 * m_ref[...] + (1.0 - ADAM_B1) * g
                nv = ADAM_B2 * v_ref[...] + (1.0 - ADAM_B2) * (g * g)
                m_hat = nm / (1.0 - ADAM_B1 ** ADAM_STEP)
                v_hat = nv / (1.0 - ADAM_B2 ** ADAM_STEP)
                g_ref[...] = g
                d_ref[...] = -ADAM_LR * (m_hat / (jnp.sqrt(v_hat) + ADAM_EPS) + ADAM_WD * w_ref[...])
                nm_ref[...] = nm
                nv_ref[...] = nv

    at = lambda l: (lambda layer, i: jnp.where(layer == l, i, 0))
    land_specs = [pl.BlockSpec((N_DEV, tr, c), lambda layer, i, f=at(l): (0, f(layer, i), 0)) for l in range(layers)]
    own_specs = [pl.BlockSpec((tr, c), lambda layer, i, f=at(l): (f(layer, i), 0)) for l in range(layers)]
    blk = pl.BlockSpec((tr, c), lambda layer, i: (layer * nr + i, 0))
    out = jax.ShapeDtypeStruct(w.shape, F32)
    return pl.pallas_call(
        body, grid=(layers, nr),
        in_specs=[pl.BlockSpec(memory_space=pltpu.SMEM)] + land_specs + own_specs + [blk, blk, blk],
        out_specs=[blk] * 4, out_shape=[out] * 4, name=name,
        compiler_params=_params("arbitrary", "arbitrary"))(mine, *lands, *owns, w, m, v)


_HBM = pl.BlockSpec(memory_space=pltpu.HBM)
_SEM = pl.BlockSpec(memory_space=pltpu.SEMAPHORE)
_EFFECT = pltpu.SideEffectType.DATAFLOW_SIDE_EFFECTING


def _each_peer(x, y, c):
    flip = lambda v, bit: 1 - v if bit else v
    for p in range(1, N_DEV):
        px, py, pc = flip(x, p & 4), flip(y, p & 2), flip(c, p & 1)
        yield p, (px, py, pc), 4 * px + 2 * py + pc


def _sem(a, p):
    return a * (N_DEV - 1) + p - 1


def _exchange_start(name, arrays, scatter):
    n = len(arrays)
    lands = [lax.empty((N_DEV,) + (a.shape[1:] if sc else a.shape), a.dtype) for a, sc in zip(arrays, scatter)]

    def body(*refs):
        in_refs, land_refs = refs[:n], refs[n:2 * n]
        send_sems, recv_sems, token = refs[2 * n], refs[2 * n + 1], refs[-1]
        x, y, c = lax.axis_index("x"), lax.axis_index("y"), lax.axis_index("c")
        me = 4 * x + 2 * y + c
        for p, coords, peer in _each_peer(x, y, c):
            for a in range(n):
                pltpu.make_async_remote_copy(
                    src_ref=in_refs[a].at[peer] if scatter[a] else in_refs[a], dst_ref=land_refs[a].at[me],
                    send_sem=send_sems.at[_sem(a, p)], recv_sem=recv_sems.at[_sem(a, p)], device_id=coords,
                    device_id_type=pl.DeviceIdType.MESH).start()
        token[...] = jnp.zeros_like(token)

    sems = pltpu.SemaphoreType.DMA((n * (N_DEV - 1),))
    hbm = lambda a: pltpu.HBM(a.shape, a.dtype)
    out = pl.pallas_call(
        body, name=name,
        out_shape=(sems, sems, *[hbm(a) for a in arrays], *[hbm(l) for l in lands],
                   jax.ShapeDtypeStruct((8, LANE), F32)),
        in_specs=[_HBM] * (2 * n), out_specs=(_SEM, _SEM, *[_HBM] * (2 * n), pl.BlockSpec(memory_space=pltpu.VMEM)),
        input_output_aliases={i: 2 + i for i in range(2 * n)},
        compiler_params=pltpu.CompilerParams(has_side_effects=_EFFECT),
    )(*[pltpu.with_memory_space_constraint(a, pltpu.HBM) for a in arrays],
      *[pltpu.with_memory_space_constraint(l, pltpu.HBM) for l in lands])
    return (out[0], out[1], list(out[2:2 + n]), list(out[2 + n:2 + 2 * n]), scatter), out[-1]


def _exchange_wait(name, started, after, me):
    send_sems, recv_sems, sent, lands, scatter = started
    n = len(sent)

    def body(*refs):
        in_refs, land_refs = refs[:n], refs[n:2 * n]
        send_sems, recv_sems = refs[2 * n], refs[2 * n + 1]
        x, y, c = lax.axis_index("x"), lax.axis_index("y"), lax.axis_index("c")
        for p, coords, peer in _each_peer(x, y, c):
            for a in range(n):
                cp = pltpu.make_async_remote_copy(
                    src_ref=in_refs[a].at[peer] if scatter[a] else in_refs[a], dst_ref=land_refs[a].at[peer],
                    send_sem=send_sems.at[_sem(a, p)], recv_sem=recv_sems.at[_sem(a, p)], device_id=coords,
                    device_id_type=pl.DeviceIdType.MESH)
                cp.wait_send()
                cp.wait_recv()

    hbm = lambda a: pltpu.HBM(a.shape, a.dtype)
    out = pl.pallas_call(
        body, name=name, out_shape=(*[hbm(a) for a in sent], *[hbm(l) for l in lands]),
        in_specs=[_HBM] * (2 * n) + [_SEM, _SEM, pl.BlockSpec(memory_space=pl.ANY)], out_specs=[_HBM] * (2 * n),
        input_output_aliases={i: i for i in range(2 * n)},
        compiler_params=pltpu.CompilerParams(has_side_effects=_EFFECT),
    )(*sent, *lands, send_sems, recv_sems, after)
    owns = [lax.dynamic_index_in_dim(out[a], me, 0, keepdims=False) if scatter[a] else out[a] for a in range(n)]
    return list(out[n:]), owns


def _fill_own(land, own, me):
    slot = lax.broadcasted_iota(jnp.int32, (N_DEV,) + (1,) * own.ndim, 0)
    return jnp.where(slot == me, own[None], land)


def _rms_fwd(name, x, g):
    t = x.shape[0]
    tm = min(t, 512)
    return _seg_fwd(name, _f_rms, (t // tm,), [x, g], [_rows(tm, D), _const((1, D))],
                    [jax.ShapeDtypeStruct((t, D), BF16)], [_rows(tm, D)])[0]


def _rms_bwd(name, x, g, dh, dres):
    t = x.shape[0]
    tm = min(t, 512)
    return _seg_bwd(name, _f_rms_res, (t // tm,), [x, g], [_rows(tm, D), _const((1, D))],
                    [dh, dres], [_rows(tm, D), _rows(tm, D)], [F32, "acc"])


FFN_TC = 256
CONF_TC = 128


def _ffn_specs(t):
    nf = D_FF // FFN_TC
    return nf, [_cols(t, 2 * FFN_TC), _cols(FFN_K, 2 * FFN_TC)], [_cols(t, FFN_TC)]


def _ffn_fwd(tag, x, p):
    t = x.shape[0]
    nf, in_specs, out_specs = _ffn_specs(t)
    h = _rms_fwd(tag + "_rms", x, p["g"])
    u0 = _mm(tag + "_up", h, p["w_up"])
    act = _seg_fwd(tag + "_mid", _f_ffn_mid, (nf,), [u0, p["w_dw"]], in_specs,
                   [jax.ShapeDtypeStruct((t, D_FF), BF16)], out_specs)[0]
    if callable(p["w_down"]):
        p["w_down"] = p["w_down"](act)
    out = _mm(tag + "_down", act, p["w_down"], res=x)
    return out, (x, h, u0, act)


def _ffn_bwd(tag, saved, p, dout, after=None):
    x, h, u0, act = saved
    t = x.shape[0]
    nf, in_specs, out_specs = _ffn_specs(t)
    g = {"w_down": _mm(tag + "_dwdown", act, dout, ta=True, out_dtype=BF16)}
    dact = _mm(tag + "_dact", dout, p["w_down"], tb=True, out_dtype=BF16, after=after)
    du0, g["w_dw"] = _seg_bwd(tag + "_dmid", _f_ffn_mid, (nf, 1), [u0, p["w_dw"]], _grid2(in_specs), [dact],
                              _grid2(out_specs), [BF16, "acc"])
    g["w_up"] = _mm(tag + "_dwup", h, du0, ta=True, out_dtype=BF16)
    dh = _mm(tag + "_dh", du0, p["w_up"], tb=True, out_dtype=BF16)
    dx, g["g"] = _rms_bwd(tag + "_drms", x, p["g"], dh, dout)
    return dx, g


def _conf_specs(t):
    tm = min(t, 512)
    nc = D // CONF_TC
    vec = _const((1, D))
    glu_in = [_cols(t, 2 * CONF_TC), _cols(1, 2 * CONF_TC), _cols(CONF_K, CONF_TC)]
    return tm, nc, glu_in, [_cols(t, CONF_TC)], [_rows(tm, D), vec, vec, vec]


def _conf_fwd(tag, x, p):
    t = x.shape[0]
    tm, nc, glu_in, glu_out, ln_in = _conf_specs(t)
    h = _rms_fwd(tag + "_rms", x, p["g"])
    u = _mm(tag + "_in", h, p["w_in"])
    cv = _seg_fwd(tag + "_gluconv", _f_conf_glu_conv, (nc,), [u, p["b_in"], p["w_dw"]], glu_in,
                  [jax.ShapeDtypeStruct((t, D), F32)], glu_out)[0]
    act = _seg_fwd(tag + "_lnsilu", _f_conf_ln_silu, (t // tm,), [cv, p["b_dw"], p["ln_g"], p["ln_b"]], ln_in,
                   [jax.ShapeDtypeStruct((t, D), BF16)], [_rows(tm, D)])[0]
    out = _mm(tag + "_out", act, p["w_out"], res=x)
    return out, (x, h, u, cv, act)


def _conf_bwd(tag, saved, p, dout, after=None):
    x, h, u, cv, act = saved
    t = x.shape[0]
    tm, nc, glu_in, glu_out, ln_in = _conf_specs(t)
    g = {"w_out": _mm(tag + "_dwout", act, dout, ta=True, out_dtype=BF16)}
    dact = _mm(tag + "_dact", dout, p["w_out"], tb=True, out_dtype=BF16, after=after)
    dcv, g["b_dw"], g["ln_g"], g["ln_b"] = _seg_bwd(
        tag + "_dlnsilu", _f_conf_ln_silu, (t // tm,), [cv, p["b_dw"], p["ln_g"], p["ln_b"]], ln_in, [dact],
        [_rows(tm, D)], [F32, "acc", "acc", "acc"])
    du, g["b_in"], g["w_dw"] = _seg_bwd(tag + "_dgluconv", _f_conf_glu_conv, (nc, 1), [u, p["b_in"], p["w_dw"]],
                                        _grid2(glu_in), [dcv], _grid2(glu_out), [BF16, "acc", "acc"])
    g["w_in"] = _mm(tag + "_dwin", h, du, ta=True, out_dtype=BF16)
    dh = _mm(tag + "_dh", du, p["w_in"], tb=True, out_dtype=BF16)
    dx, g["g"] = _rms_bwd(tag + "_drms", x, p["g"], dh, dout)
    return dx, g


def _gdn_specs(t):
    tc, tm, rows = 256, min(t, 256), min(t, 1024)
    nq = 3 * D // tc
    conv = ([_cols(t, tc), _cols(GDN_K, tc)], [_cols(t, tc)])
    gate_in = [_rows(tm, 2 * D, 0), _rows(tm, LANE, 4 * D // LANE), _const((1, LANE)), _const((1, LANE))]
    gate_out = [_rows(tm, D)] * 4
    head = pl.BlockSpec((rows, DH), lambda h, i: (i, h))
    headv = pl.BlockSpec((rows, DH), lambda h, i: (i, 2 * HEADS + h))
    nch = rows // GDN_CHUNK
    pre_in = [head, head, headv, head, head]
    pre_out = [head, head, head, pl.BlockSpec((None, nch, GDN_CHUNK, GDN_CHUNK), lambda h, i: (h, i, 0, 0)), head,
               pl.BlockSpec((None, nch, 1, DH), lambda h, i: (h, i, 0, 0))]
    post_in = [_rows(tm, D), _rows(tm, D, 3), _const((1, DH))]
    return tm, rows, nq, conv, gate_in, gate_out, pre_in, pre_out, post_in


def _gdn_fwd(tag, x, p):
    t = x.shape[0]
    tm, rows, nq, conv, gate_in, gate_out, pre_in, pre_out, post_in = _gdn_specs(t)
    n = t // GDN_CHUNK
    big = jax.ShapeDtypeStruct((t, D), F32)
    h = _rms_fwd(tag + "_rms", x, p["g"])
    proj = _mm(tag + "_in", h, p["w_in"])
    qkv = _seg_fwd(tag + "_conv", _f_conv_silu, (nq,), [proj, p["conv_w"]], conv[0],
                   [jax.ShapeDtypeStruct((t, 3 * D), F32)], conv[1])[0]
    gate_ins = [qkv, proj, p["a_log"], p["dt_bias"]]
    qn, kn, gb, bb = _seg_fwd(tag + "_gates", _f_gdn_gates, (t // tm,), gate_ins, gate_in, [big] * 4, gate_out)
    pre_ins = [qn, kn, qkv, gb, bb]
    pre_shapes = [big, big, big, jax.ShapeDtypeStruct((HEADS, n, GDN_CHUNK, GDN_CHUNK), F32), big,
                  jax.ShapeDtypeStruct((HEADS, n, 1, DH), F32)]
    pre = _seg_fwd(tag + "_prescan", _f_gdn_prescan, (HEADS, t // rows), pre_ins, pre_in, pre_shapes, pre_out)
    o, states = _gdn_scan_fwd(tag + "_scan", *pre)
    post_ins = [o, proj, p["o_g"]]
    act = _seg_fwd(tag + "_post", _f_gdn_post, (t // tm,), post_ins, post_in, [jax.ShapeDtypeStruct((t, D), BF16)],
                   [_rows(tm, D)])[0]
    out = _mm(tag + "_out", act, p["w_out"], res=x)
    return out, (x, h, proj, gate_ins, pre_ins, pre, states, post_ins, act)


def _gdn_bwd(tag, saved, p, dout, after=None):
    x, h, proj, gate_ins, pre_ins, pre, states, post_ins, act = saved
    t = x.shape[0]
    tm, rows, nq, conv, gate_in, gate_out, pre_in, pre_out, post_in = _gdn_specs(t)
    g = {"w_out": _mm(tag + "_dwout", act, dout, ta=True, out_dtype=BF16)}
    dact = _mm(tag + "_dact", dout, p["w_out"], tb=True, out_dtype=BF16, after=after)
    do, dproj, g["o_g"] = _seg_bwd(tag + "_dpost", _f_gdn_post, (t // tm,), post_ins, post_in, [dact],
                                   [_rows(tm, D)], [F32, BF16, "acc"])
    dpre = _gdn_scan_bwd(tag + "_dscan", *pre, states, do)
    dqn, dkn, dqkv, dgb, dbb = _seg_bwd(tag + "_dprescan", _f_gdn_prescan, (HEADS, t // rows), pre_ins, pre_in,
                                        dpre, pre_out, [F32] * 5)
    dqkv, dproj, g["a_log"], g["dt_bias"] = _seg_bwd(
        tag + "_dgates", _f_gdn_gates, (t // tm,), gate_ins, gate_in, [dqn, dkn, dgb, dbb], gate_out,
        [F32, BF16, "acc", "acc"], into={0: dqkv, 1: dproj})
    dproj, g["conv_w"] = _seg_bwd(tag + "_dconv", _f_conv_silu, (nq, 1), [proj, p["conv_w"]], _grid2(conv[0]),
                                  [dqkv], _grid2(conv[1]), [BF16, "acc"], into={0: dproj})
    g["w_in"] = _mm(tag + "_dwin", h, dproj, ta=True, out_dtype=BF16)
    dh = _mm(tag + "_dh", dproj, p["w_in"], tb=True, out_dtype=BF16)
    dx, g["g"] = _rms_bwd(tag + "_drms", x, p["g"], dh, dout)
    return dx, g


def _fox_specs(t):
    tm = min(t, 512)
    vec = _const((1, LANE))
    pre_in = [_rows(tm, 2 * D + LANE, 0), vec, vec, vec]
    pre_out = [_rows(tm, D), _rows(tm, D), _rows(tm, LANE)]
    return tm, pre_in, pre_out


def _per_head(c):
    return jnp.transpose(c[:, :HEADS])


def _per_lane(ch):
    return jnp.pad(jnp.transpose(ch), ((0, 0), (0, LANE - HEADS)))


def _fox_fwd_layer(tag, x, p):
    t = x.shape[0]
    tm, pre_in, pre_out = _fox_specs(t)
    blk, nb = _fox_blocks(t)
    h = _rms_fwd(tag + "_rms", x, p["g"])
    proj = _mm(tag + "_in", h, p["w_in"])
    pre_ins = [proj, p["q_g"], p["k_g"], p["b_f"]]
    qn, kn, lf = _seg_fwd(tag + "_pre", _f_fox_pre, (t // tm,), pre_ins, pre_in,
                          [jax.ShapeDtypeStruct((t, D), BF16)] * 2 + [jax.ShapeDtypeStruct((t, LANE), F32)], pre_out)
    ch = _per_head(_cumsum_rows(tag + "_cumsum", [lf], False))
    c_col, c_row = ch.reshape(HEADS, t, 1), ch.reshape(HEADS, nb, 1, blk)
    o, lse = _fox_fwd(tag + "_attn", qn, kn, proj, c_col, c_row)
    out = _mm(tag + "_out", o, p["w_out"], res=x)
    return out, (x, h, proj, pre_ins, qn, kn, c_col, c_row, o, lse)


def _fox_bwd_layer(tag, saved, p, dout, after=None):
    x, h, proj, pre_ins, qn, kn, c_col, c_row, o, lse = saved
    t = x.shape[0]
    tm, pre_in, pre_out = _fox_specs(t)
    g = {"w_out": _mm(tag + "_dwout", o, dout, ta=True, out_dtype=BF16)}
    do = _mm(tag + "_do", dout, p["w_out"], tb=True, after=after)
    dqn, dc_col, delta = _fox_dq(tag + "_dq", qn, kn, proj, c_col, c_row, o, lse, do)
    dkn, dproj, dc_row = _fox_dkv(tag + "_dkv", qn, kn, proj, c_col, c_row, lse, delta, do)
    dlf = _cumsum_rows(tag + "_dcumsum", [_per_lane(dc_col.reshape(HEADS, t)), _per_lane(dc_row.reshape(HEADS, t))],
                       True)
    dproj, g["q_g"], g["k_g"], g["b_f"] = _seg_bwd(
        tag + "_dpre", _f_fox_pre, (t // tm,), pre_ins, pre_in, [dqn, dkn, dlf], pre_out,
        [BF16, "acc", "acc", "acc"], into={0: dproj})
    g["w_in"] = _mm(tag + "_dwin", h, dproj, ta=True, out_dtype=BF16)
    dh = _mm(tag + "_dh", dproj, p["w_in"], tb=True, out_dtype=BF16)
    dx, g["g"] = _rms_bwd(tag + "_drms", x, p["g"], dh, dout)
    return dx, g


_MIXERS = ((_conf_fwd, _conf_bwd), (_gdn_fwd, _gdn_bwd), (_fox_fwd_layer, _fox_bwd_layer))


def _local_step(x, target, params_of, on_grads):
    saved, params = [], []
    for i in range(DEPTH):
        mp = params_of(i, 0, x)
        x, sm = _MIXERS[i % N_MIXERS][0](f"l{i}_mix", x, mp)
        fp = params_of(i, 1, x)
        x, sf = _ffn_fwd(f"l{i}_ffn", x, fp)
        saved.append((sm, sf))
        params.append((mp, fp))
    dx, sq = _loss_head(x, target)
    after = None
    for i in reversed(range(DEPTH)):
        dx, gf = _ffn_bwd(f"l{i}_ffn", saved[i][1], params[i][1], dx, after)
        after = on_grads(i, 1, gf)
        dx, gm = _MIXERS[i % N_MIXERS][1](f"l{i}_mix", saved[i][0], params[i][0], dx, after)
        after = on_grads(i, 0, gm)
    return sq, dx


def _unshard(name, g):
    axis = g.ndim - 2 if name in ROW_SHARDED else g.ndim - 1
    m = jnp.moveaxis(g, 0, axis - 1)
    return m.reshape(m.shape[:axis - 1] + (N_DEV * m.shape[axis],) + m.shape[axis + 1:])


def _reshard(name, full):
    axis = full.ndim - 2 if name in ROW_SHARDED else full.ndim - 1
    s = full.shape
    return jnp.moveaxis(full.reshape(s[:axis] + (N_DEV, s[axis] // N_DEV) + s[axis + 1:]), axis, 0)


def _pad_cols(a, width):
    return jnp.pad(a, [(0, 0)] * (a.ndim - 1) + [(0, width - a.shape[-1])])


def _lane_vec(v):
    return _pad_cols(v.reshape(1, -1), LANE)


REP_ROWS = 16


def _pack_rep(r):
    small = [_pad_cols(r[n].reshape(1, -1), LANE) for n in REPLICATED[2:]]
    row = jnp.concatenate(small + [jnp.zeros((1, D - LANE * len(small)), F32)], axis=1)
    pad = jnp.zeros((REP_ROWS - 2 * DEPTH - 1, D), F32)
    return jnp.concatenate([r['mix_norm_g'].reshape(DEPTH, D), r['ffn_norm_g'].reshape(DEPTH, D), row, pad], axis=0)


def _unpack_rep(a, shapes):
    out = {'mix_norm_g': a[:DEPTH], 'ffn_norm_g': a[DEPTH:2 * DEPTH]}
    for i, n in enumerate(REPLICATED[2:]):
        out[n] = a[2 * DEPTH:2 * DEPTH + 1, i * LANE:i * LANE + shapes[n][1]].reshape(shapes[n])
    return out


def _view2d(a):
    return a.reshape(-1, a.shape[-1])


MIXER_SHARDED = (
    (('conv_w_in', 'w_in'), ('conv_w_out', 'w_out'), ('conv_b_in', 'b_in'), ('conv_w_dw', 'w_dw'),
     ('conv_b_dw', 'b_dw'), ('conv_ln_g', 'ln_g'), ('conv_ln_b', 'ln_b')),
    (('gdn_w_in', 'w_in'), ('gdn_w_out', 'w_out'), ('gdn_conv_w', 'conv_w')),
    (('fox_w_in', 'w_in'), ('fox_w_out', 'w_out')),
)
FFN_SHARDED = (('ffn_w_up', 'w_up'), ('ffn_w_down', 'w_down'), ('ffn_w_dw', 'w_dw'))
MIXER_REPLICATED = (
    (),
    (('gdn_a_log', 'a_log'), ('gdn_dt_bias', 'dt_bias'), ('gdn_o_norm_g', 'o_g')),
    (('fox_b_f', 'b_f'), ('fox_q_norm_g', 'q_g'), ('fox_k_norm_g', 'k_g')),
)
ROW_VECTORS = ('conv_b_in', 'conv_b_dw', 'conv_ln_g', 'conv_ln_b')
INTERLEAVED = {'ffn_w_up': (D_FF, FFN_TC), 'ffn_w_dw': (D_FF, FFN_TC), 'conv_w_in': (D, CONF_TC),
               'conv_b_in': (D, CONF_TC)}
PACK_GROUP = 16 * D


def _part_entries(i, part):
    ent = [(n, i, k) for n, k in FFN_SHARDED] if part else [(n, i // N_MIXERS, k) for n, k in MIXER_SHARDED[i % N_MIXERS]]
    return [e for e in ent if e[0] in MATRICES], [e for e in ent if e[0] not in MATRICES]


def _interleave(a, half, blk):
    s = a.shape[:-1]
    return jnp.swapaxes(a.reshape(s + (2, half // blk, blk)), -3, -2).reshape(s + (2 * half,))


def _deinterleave(a, half, blk):
    s = a.shape[:-1]
    return jnp.swapaxes(a.reshape(s + (half // blk, 2, blk)), -3, -2).reshape(s + (2 * half,))


def _to_param(name, whole):
    if name in ROW_VECTORS:
        whole = whole[None]
    if name in INTERLEAVED:
        return _interleave(whole, *INTERLEAVED[name])
    if name == 'gdn_w_in':
        return _pad_cols(whole, GDN_PAD)
    if name == 'fox_w_in':
        return jnp.concatenate([whole[:, :2 * D], _pad_cols(whole[:, 3 * D:], LANE), whole[:, 2 * D:3 * D]], axis=1)
    return whole


def _from_grad(name, g):
    if name in INTERLEAVED:
        g = _deinterleave(g, *INTERLEAVED[name])
    if name in ROW_VECTORS:
        return g[0]
    if name == 'gdn_w_in':
        return g[:, :4 * D + 2 * HEADS]
    if name == 'fox_w_in':
        return jnp.concatenate([g[:, :2 * D], g[:, 2 * D + LANE:], g[:, 2 * D:2 * D + HEADS]], axis=1)
    return g


def _layer_col(name, n):
    if name in INTERLEAVED:
        half, blk = INTERLEAVED[name]
        return (n % half) // blk * (2 * blk) + (n // half) * blk + n % blk
    if name == 'fox_w_in':
        return np.where(n < 2 * D, n, np.where(n < 3 * D, n + LANE, n - D))
    return n


def _col_runs(name, shard_cols):
    dst = _layer_col(name, np.arange(N_DEV * shard_cols))
    runs, start = [[] for _ in range(N_DEV)], 0
    for k in range(1, dst.size + 1):
        if k == dst.size or dst[k] != dst[k - 1] + 1 or k % shard_cols == 0:
            runs[start // shard_cols].append((start % shard_cols, k - start, int(dst[start])))
            start = k
    width = {'gdn_w_in': GDN_PAD, 'fox_w_in': FOX_PAD}.get(name, dst.size)
    free = np.ones(width + 1, bool)
    free[dst] = False
    free[width] = False
    gaps, start = [], None
    for k in range(width + 1):
        if free[k] and start is None:
            start = k
        if not free[k] and start is not None:
            gaps.append((start, k - start))
            start = None
    return runs, width, gaps


def _unshard_cols(call, name, land, own, mine):
    _, r, c = land.shape
    runs, width, gaps = _col_runs(name, c)
    tr = _tile(r, 256)

    def body(me_ref, land_ref, own_ref, o_ref):
        for s in range(N_DEV):
            for src, ln, dst in runs[s]:
                o_ref[:, dst:dst + ln] = jnp.where(me_ref[0] == s, own_ref[:, src:src + ln], land_ref[s, :, src:src + ln])
        for start, ln in gaps:
            o_ref[:, start:start + ln] = jnp.zeros((tr, ln), o_ref.dtype)

    return pl.pallas_call(
        body, grid=(r // tr,),
        in_specs=[pl.BlockSpec(memory_space=pltpu.SMEM), pl.BlockSpec((N_DEV, tr, c), lambda i: (0, i, 0)),
                  pl.BlockSpec((tr, c), lambda i: (i, 0))],
        out_specs=pl.BlockSpec((tr, width), lambda i: (i, 0)), out_shape=jax.ShapeDtypeStruct((r, width), land.dtype),
        name=call, compiler_params=_params("parallel"))(mine, land, own)


def _reshard_cols(call, name, g, shard_cols):
    r = g.shape[0]
    runs, width, _ = _col_runs(name, shard_cols)
    tr = _tile(r, 256)

    def body(g_ref, o_ref):
        for s in range(N_DEV):
            for src, ln, dst in runs[s]:
                o_ref[s, :, src:src + ln] = g_ref[:, dst:dst + ln]

    return pl.pallas_call(
        body, grid=(r // tr,), in_specs=[pl.BlockSpec((tr, width), lambda i: (i, 0))],
        out_specs=pl.BlockSpec((N_DEV, tr, shard_cols), lambda i: (0, i, 0)),
        out_shape=jax.ShapeDtypeStruct((N_DEV, r, shard_cols), g.dtype), name=call,
        compiler_params=_params("parallel"))(g)


def _pack_rows(parts, lead):
    out = []
    for a in parts:
        flat = a.reshape(a.shape[:lead] + (-1,))
        size = flat.shape[-1]
        padded = -(-size // PACK_GROUP) * PACK_GROUP
        flat = jnp.pad(flat, [(0, 0)] * lead + [(0, padded - size)])
        out.append(flat.reshape(a.shape[:lead] + (padded // D, D)))
    return jnp.concatenate(out, axis=lead)


def _unpack_rows(packed, shapes, lead):
    out, row = [], 0
    head = packed.shape[:lead]
    for s in shapes:
        size = 1
        for d in s:
            size *= d
        rows = -(-size // PACK_GROUP) * (PACK_GROUP // D)
        part = lax.slice_in_dim(packed, row, row + rows, axis=lead)
        out.append(part.reshape(head + (rows * D,))[..., :size].reshape(head + tuple(s)))
        row += rows
    return out


def _part_dict(i, part, whole, rep):
    if part:
        p = {k: _to_param(n, whole[n]) for n, k in FFN_SHARDED if n in whole}
        p["g"] = rep['ffn_norm_g'][i][None]
        return p
    kind, j = i % N_MIXERS, i // N_MIXERS
    p = {k: _to_param(n, whole[n]) for n, k in MIXER_SHARDED[kind] if n in whole}
    for n, k in MIXER_REPLICATED[kind]:
        p[k] = rep[n][j][None] if rep[n].shape[-1] == DH else _lane_vec(rep[n][j])
    p["g"] = rep['mix_norm_g'][i][None]
    return p


def _part_grads(i, part, g):
    return {n: _from_grad(n, g[k]) for n, k in (FFN_SHARDED if part else MIXER_SHARDED[i % N_MIXERS])}


def _replicated_grads(grads):
    rep = {'mix_norm_g': jnp.concatenate([g[0]["g"] for g in grads]),
           'ffn_norm_g': jnp.concatenate([g[1]["g"] for g in grads])}
    for kind in range(N_MIXERS):
        for n, k in MIXER_REPLICATED[kind]:
            rep[n] = jnp.stack([grads[i][0][k] for i in range(kind, DEPTH, N_MIXERS)])
    return rep


def kernel(x, mix_norm_g, ffn_norm_g, conv_w_in, conv_b_in, conv_w_dw, conv_b_dw, conv_ln_g, conv_ln_b, conv_w_out, gdn_w_in, gdn_conv_w, gdn_a_log, gdn_dt_bias, gdn_o_norm_g, gdn_w_out, fox_w_in, fox_b_f, fox_q_norm_g, fox_k_norm_g, fox_w_out, ffn_w_up, ffn_w_dw, ffn_w_down, loss_target, m_mix_norm_g, m_ffn_norm_g, m_conv_w_in, m_conv_b_in, m_conv_w_dw, m_conv_b_dw, m_conv_ln_g, m_conv_ln_b, m_conv_w_out, m_gdn_w_in, m_gdn_conv_w, m_gdn_a_log, m_gdn_dt_bias, m_gdn_o_norm_g, m_gdn_w_out, m_fox_w_in, m_fox_b_f, m_fox_q_norm_g, m_fox_k_norm_g, m_fox_w_out, m_ffn_w_up, m_ffn_w_dw, m_ffn_w_down, v_mix_norm_g, v_ffn_norm_g, v_conv_w_in, v_conv_b_in, v_conv_w_dw, v_conv_b_dw, v_conv_ln_g, v_conv_ln_b, v_conv_w_out, v_gdn_w_in, v_gdn_conv_w, v_gdn_a_log, v_gdn_dt_bias, v_gdn_o_norm_g, v_gdn_w_out, v_fox_w_in, v_fox_b_f, v_fox_q_norm_g, v_fox_k_norm_g, v_fox_w_out, v_ffn_w_up, v_ffn_w_dw, v_ffn_w_down):
    given = dict(zip(
        WEIGHTS + ["m_" + n for n in WEIGHTS] + ["v_" + n for n in WEIGHTS],
        (mix_norm_g, ffn_norm_g, conv_w_in, conv_b_in, conv_w_dw, conv_b_dw, conv_ln_g, conv_ln_b, conv_w_out, gdn_w_in, gdn_conv_w, gdn_a_log, gdn_dt_bias, gdn_o_norm_g, gdn_w_out, fox_w_in, fox_b_f, fox_q_norm_g, fox_k_norm_g, fox_w_out, ffn_w_up, ffn_w_dw, ffn_w_down,
         m_mix_norm_g, m_ffn_norm_g, m_conv_w_in, m_conv_b_in, m_conv_w_dw, m_conv_b_dw, m_conv_ln_g, m_conv_ln_b, m_conv_w_out, m_gdn_w_in, m_gdn_conv_w, m_gdn_a_log, m_gdn_dt_bias, m_gdn_o_norm_g, m_gdn_w_out, m_fox_w_in, m_fox_b_f, m_fox_q_norm_g, m_fox_k_norm_g, m_fox_w_out, m_ffn_w_up, m_ffn_w_dw, m_ffn_w_down,
         v_mix_norm_g, v_ffn_norm_g, v_conv_w_in, v_conv_b_in, v_conv_w_dw, v_conv_b_dw, v_conv_ln_g, v_conv_ln_b, v_conv_w_out, v_gdn_w_in, v_gdn_conv_w, v_gdn_a_log, v_gdn_dt_bias, v_gdn_o_norm_g, v_gdn_w_out, v_fox_w_in, v_fox_b_f, v_fox_q_norm_g, v_fox_k_norm_g, v_fox_w_out, v_ffn_w_up, v_ffn_w_dw, v_ffn_w_down)))

    me = 4 * lax.axis_index("x") + 2 * lax.axis_index("y") + lax.axis_index("c")
    mine = me.astype(jnp.int32).reshape(1)

    def gather_entries(i, group):
        mats, smalls = _part_entries(i, min(group, 1))
        mats = [e for e in mats if (e[0] == 'ffn_w_down') == (group == 2)]
        return mats, ([] if group == 2 else smalls)

    gathers, token = {}, jnp.zeros((1, 1), F32)
    for i in range(DEPTH):
        for group in (0, 1, 2):
            mats, smalls = gather_entries(i, group)
            sent = [given[n][j].astype(BF16) for n, j, _ in mats]
            if smalls:
                sent.append(_pack_rows([given[n][j] for n, j, _ in smalls], 0))
            gathers[i, group], tok = _exchange_start(f"gather{i}{'mud'[group]}_start", sent, [False] * len(sent))
            token = token + tok[0:1, 0:1]

    def params_of(i, part, x_in):
        p = fetch(i, part, token if (i, part) == (0, 0) else x_in)
        if part == 1:
            p["w_down"] = lambda act: fetch(i, 2, act)["w_down"]
        return p

    def fetch(i, group, after):
        mats, smalls = gather_entries(i, group)
        part = min(group, 1)
        lands, owns = _exchange_wait(f"gather{i}{'mud'[group]}_wait", gathers[i, group], after, me)
        whole, relaid = {}, {}
        for (n, _, k), land, own in zip(mats, lands, owns):
            if n in ROW_SHARDED:
                whole[n] = _unshard(n, _fill_own(land, own, me))
            else:
                relaid[k] = _unshard_cols(f"l{i}_{n}_unshard", n, land, own, mine)
        if smalls:
            shapes = [given[n].shape[1:] for n, _, _ in smalls]
            for (n, _, _), g in zip(smalls, _unpack_rows(_fill_own(lands[-1], owns[-1], me), shapes, 1)):
                whole[n] = _unshard(n, g)
        return {**_part_dict(i, part, whole, given), **relaid}

    grads, exchanges = [[None, None] for _ in range(DEPTH)], {}

    def on_grads(i, part, g):
        grads[i][part] = g
        mats, smalls = _part_entries(i, part)
        sent = [_reshard(n, g[k]) if n in ROW_SHARDED else
                _reshard_cols(f"l{i}_{n}_reshard", n, g[k], given[n].shape[-1]) for n, _, k in mats]
        if smalls:
            sent.append(_pack_rows([_reshard(n, _from_grad(n, g[k])) for n, _, k in smalls], 1))
        last = (i, part) == (0, 0)
        if last:
            sent.append(_pack_rep(_replicated_grads(grads)))
        exchanges[i, part], tok = _exchange_start(f"grads{i}{'mf'[part]}_start", sent,
                                                  [True] * (len(sent) - last) + [False] * last)
        tokens.append(tok)
        return tok

    tokens = []
    sq, dx = _local_step(x[0], loss_target[0], params_of, on_grads)
    loss = (0.5 / D) * lax.psum(sq[0, 0], ("x", "y", "c"))

    new, after = {}, tokens[-1]
    for part in (1, 0):
        pieces = {}
        for i in range(DEPTH):
            mats, smalls = _part_entries(i, part)
            lands, owns = _exchange_wait(f"grads{i}{'mf'[part]}_wait", exchanges[i, part], after, me)
            if (i, part) == (0, 0):
                rep_piece = (lands[-1], owns[-1])
            for (n, _, _), land, own in zip(mats, lands, owns):
                pieces.setdefault(n, []).append((land, own))
            if smalls:
                shapes = [given[n].shape[1:] for n, _, _ in smalls]
                for (n, _, _), land, own in zip(smalls, _unpack_rows(lands[len(mats)], shapes, 1),
                                                _unpack_rows(owns[len(mats)], shapes, 0)):
                    pieces.setdefault(n, []).append((land, own))
        for n in pieces:
            lands = [l.reshape((N_DEV, -1, l.shape[-1])) for l, _ in pieces[n]]
            owns = [o.reshape((-1, o.shape[-1])) for _, o in pieces[n]]
            if owns[0].shape[0] % 8:
                lands = [jnp.stack(lands, axis=1).reshape((N_DEV, -1, lands[0].shape[-1]))]
                owns = [jnp.stack(owns).reshape((-1, owns[0].shape[-1]))]
            outs = _adam("adam_" + n, lands, owns, mine, _view2d(given[n]), _view2d(given["m_" + n]),
                         _view2d(given["v_" + n]))
            new[n] = [o.reshape(given[n].shape) for o in outs]
        after = new['ffn_w_up'][0]
    packed = [_pack_rep({n: given[pre + n] for n in REPLICATED}) for pre in ("", "m_", "v_")]
    outs = _adam("adam_replicated", [rep_piece[0]], [rep_piece[1]], mine, *packed)
    unpacked = [_unpack_rep(o, {n: given[n].shape for n in REPLICATED}) for o in outs]
    for n in REPLICATED:
        new[n] = [u[n] for u in unpacked]
    return (loss, dx[None], *[new[n][0] for n in WEIGHTS], *[new[n][1] for n in WEIGHTS],
            *[new[n][2] for n in WEIGHTS], *[new[n][3] for n in WEIGHTS])
```
